```python
import math
import jax, jax.numpy as jnp
from jax import lax
import numpy as np

D_MODEL = 1024
BATCH = 8
SEQ = 4096
DEPTH = 4

CHUNK = 64
EPS = 1e-6
CONV_WIDTH = 3
CONV_DIM = D_MODEL // 2
RET_HEADS = 4
RET_HEAD_DIM = (D_MODEL // 2) // RET_HEADS
RET_DIM = RET_HEADS * RET_HEAD_DIM
ROPE_THETA = 10000.0
SB_HEADS = 8
SB_HEAD_DIM = D_MODEL // SB_HEADS
SB_DIM = SB_HEADS * SB_HEAD_DIM
SB_BLOCK = 128
FFN_MULT_OF = 256
D_FF = -(-8 * D_MODEL // (3 * FFN_MULT_OF)) * FFN_MULT_OF
EVEN_IN = 3 * CONV_DIM + 4 * RET_DIM
N_EVEN = (DEPTH + 1) // 2
N_ODD = DEPTH // 2

kernel_name = 'hybrid_conv_retention_stickbreaking_adaln_trunk'


def rms_norm(x, g):
    xf = x.astype(jnp.float32)
    y = xf * lax.rsqrt(jnp.mean(xf * xf, axis=-1, keepdims=True) + EPS)
    return (y * g.astype(jnp.float32)).astype(x.dtype)


def rotary(x, pos):
    dh = x.shape[-1]
    inv_freq = 1.0 / (ROPE_THETA ** (jnp.arange(0, dh, 2, dtype=jnp.float32) / dh))
    ang = pos.astype(jnp.float32)[:, None] * inv_freq[None, :]
    cos = jnp.cos(ang)[None, :, None, :].astype(x.dtype)
    sin = jnp.sin(ang)[None, :, None, :].astype(x.dtype)
    x1, x2 = jnp.split(x, 2, axis=-1)
    return jnp.concatenate([x1 * cos - x2 * sin, x1 * sin + x2 * cos], axis=-1)


def short_gated_conv(b_gate, c_gate, u, conv_w):
    z = c_gate * u
    w = conv_w[:, None, :].astype(z.dtype)
    y = lax.conv_general_dilated(z, w, window_strides=(1,), padding=[(CONV_WIDTH - 1, 0)],
                                 dimension_numbers=('NWC', 'WIO', 'NWC'),
                                 feature_group_count=CONV_DIM)
    return b_gate * y


def retention(q, k, v):
    bsz, s_len, h, dh = q.shape
    n = s_len // CHUNK
    dt = q.dtype
    log_g = jnp.log1p(-jnp.exp2(-5.0 - jnp.arange(h, dtype=jnp.float32)))
    idx = jnp.arange(CHUNK, dtype=jnp.float32)
    intra_dec = jnp.exp(jnp.abs(idx[:, None] - idx[None, :])[None] * log_g[:, None, None]).astype(dt)
    q_dec = jnp.exp((idx + 1.0)[None, :] * log_g[:, None]).astype(dt)
    k_dec = jnp.exp((CHUNK - 1.0 - idx)[None, :] * log_g[:, None]).astype(dt)
    chunk_dec = jnp.exp(CHUNK * log_g).astype(dt)
    qc = q.reshape(bsz, n, CHUNK, h, dh)
    kc = k.reshape(bsz, n, CHUNK, h, dh)
    vc = v.reshape(bsz, n, CHUNK, h, dh)
    scores = jnp.einsum('bnihd,bnjhd->bnhij', qc, kc) * intra_dec
    o_intra = jnp.einsum('bnhij,bnjhd->bnihd', scores, vc)
    kv = jnp.einsum('bnjhd,hj,bnjhe->bnhde', kc, k_dec, vc)

    def step(state, kv_i):
        return state * chunk_dec[None, :, None, None] + kv_i, state

    _, prev = lax.scan(step, jnp.zeros_like(kv[:, 0]), jnp.moveaxis(kv, 1, 0))
    prev = jnp.moveaxis(prev, 0, 1)
    o_inter = jnp.einsum('bnihd,hi,bnhde->bnihe', qc, q_dec, prev)
    return (o_intra + o_inter).reshape(bsz, s_len, h, dh)


def stick_breaking(q, k, v):
    s_len, dh = q.shape[2], q.shape[3]
    scale = dh ** -0.5
    outs = []
    for i in range(s_len // SB_BLOCK):
        q0 = i * SB_BLOCK
        kl = q0 + SB_BLOCK
        qb, kb, vb = q[:, :, q0:kl], k[:, :, :kl], v[:, :, :kl]
        z = jnp.einsum('bhqd,bhkd->bhqk', qb, kb).astype(jnp.float32) * scale
        qpos = q0 + jnp.arange(SB_BLOCK)
        kpos = jnp.arange(kl)
        mask = kpos[None, :] < qpos[:, None]
        log_beta = jax.nn.log_sigmoid(z)
        log_keep = jnp.where(mask, jax.nn.log_sigmoid(-z), 0.0)
        acc = lax.cumsum(log_keep, axis=3, reverse=True) - log_keep
        w = jnp.where(mask, jnp.exp(log_beta + acc), 0.0)
        outs.append(jnp.einsum('bhqk,bhkd->bhqd', w.astype(vb.dtype), vb))
    return jnp.concatenate(outs, axis=2)


def even_mixer(h, w_in, conv_w, ret_norm_g, w_out):
    bsz, s_len, _ = h.shape
    proj = h @ w_in
    cuts = [CONV_DIM, 2 * CONV_DIM, 3 * CONV_DIM, 3 * CONV_DIM + RET_DIM,
            3 * CONV_DIM + 2 * RET_DIM, 3 * CONV_DIM + 3 * RET_DIM]
    b_gate, c_gate, u, q, k, v, g = jnp.split(proj, cuts, axis=-1)
    a_out = short_gated_conv(b_gate, c_gate, u, conv_w)
    pos = jnp.arange(s_len)
    q = rotary(q.reshape(bsz, s_len, RET_HEADS, RET_HEAD_DIM), pos)
    k = rotary(k.reshape(bsz, s_len, RET_HEADS, RET_HEAD_DIM), pos) * (RET_HEAD_DIM ** -0.5)
    v = v.reshape(bsz, s_len, RET_HEADS, RET_HEAD_DIM)
    r = retention(q, k, v)
    r = rms_norm(r, ret_norm_g.reshape(RET_HEADS, RET_HEAD_DIM))
    r = jax.nn.silu(g) * r.reshape(bsz, s_len, RET_DIM)
    return jnp.concatenate([a_out, r], axis=-1) @ w_out


def odd_mixer(h, w_qkv, q_norm_g, k_norm_g, w_out):
    bsz, s_len, _ = h.shape
    q, k, v = jnp.split(h @ w_qkv, 3, axis=-1)
    q = rms_norm(q.reshape(bsz, s_len, SB_HEADS, SB_HEAD_DIM), q_norm_g)
    k = rms_norm(k.reshape(bsz, s_len, SB_HEADS, SB_HEAD_DIM), k_norm_g)
    v = v.reshape(bsz, s_len, SB_HEADS, SB_HEAD_DIM)
    o = stick_breaking(q.transpose(0, 2, 1, 3), k.transpose(0, 2, 1, 3), v.transpose(0, 2, 1, 3))
    return o.transpose(0, 2, 1, 3).reshape(bsz, s_len, SB_DIM) @ w_out


def swiglu(h, w_gate, w_up, w_down):
    return (jax.nn.silu(h @ w_gate) * (h @ w_up)) @ w_down


def _fwd_setup_inputs(seed: int = 0) -> dict:
    key = jax.random.key(seed)
    ks = jax.random.split(key, 17)

    def nrm(k, shape, fan_in, mult=1.0):
        return jax.random.normal(k, shape, jnp.float32) * (mult * fan_in ** -0.5)

    def gain(k, shape):
        return 1.0 + 0.02 * jax.random.normal(k, shape, jnp.float32)

    return {
        'x': jax.random.normal(ks[0], (BATCH, SEQ, D_MODEL), jnp.float32),
        'c': jax.random.normal(ks[1], (BATCH, D_MODEL), jnp.float32),
        'ada_w': nrm(ks[2], (DEPTH, D_MODEL, 6 * D_MODEL), D_MODEL, 0.5),
        'ada_b': 0.02 * jax.random.normal(ks[3], (DEPTH, 6 * D_MODEL), jnp.float32),
        'norm_mix_g': gain(ks[4], (DEPTH, D_MODEL)),
        'norm_ffn_g': gain(ks[5], (DEPTH, D_MODEL)),
        'ev_w_in': nrm(ks[6], (N_EVEN, D_MODEL, EVEN_IN), D_MODEL),
        'ev_conv_w': nrm(ks[7], (N_EVEN, CONV_WIDTH, CONV_DIM), CONV_WIDTH),
        'ev_ret_norm_g': gain(ks[8], (N_EVEN, RET_DIM)),
        'ev_w_out': nrm(ks[9], (N_EVEN, CONV_DIM + RET_DIM, D_MODEL), CONV_DIM + RET_DIM),
        'od_w_qkv': nrm(ks[10], (N_ODD, D_MODEL, 3 * SB_DIM), D_MODEL),
        'od_q_norm_g': gain(ks[11], (N_ODD, SB_HEAD_DIM)),
        'od_k_norm_g': gain(ks[12], (N_ODD, SB_HEAD_DIM)),
        'od_w_out': nrm(ks[13], (N_ODD, SB_DIM, D_MODEL), SB_DIM),
        'ffn_w_gate': nrm(ks[14], (DEPTH, D_MODEL, D_FF), D_MODEL),
        'ffn_w_up': nrm(ks[15], (DEPTH, D_MODEL, D_FF), D_MODEL),
        'ffn_w_down': nrm(ks[16], (DEPTH, D_FF, D_MODEL), D_FF),
    }


def _fwd_reference(x, c, ada_w, ada_b, norm_mix_g, norm_ffn_g, ev_w_in, ev_conv_w, ev_ret_norm_g,
              ev_w_out, od_w_qkv, od_q_norm_g, od_k_norm_g, od_w_out, ffn_w_gate, ffn_w_up,
              ffn_w_down):
    c_act = jax.nn.silu(c)
    for l in range(DEPTH):
        mod = c_act @ ada_w[l] + ada_b[l]
        sh1, sc1, g1, sh2, sc2, g2 = [m[:, None, :] for m in jnp.split(mod, 6, axis=-1)]
        h = rms_norm(x, norm_mix_g[l]) * (1 + sc1) + sh1
        j = l // 2
        if l % 2 == 0:
            y = even_mixer(h, ev_w_in[j], ev_conv_w[j], ev_ret_norm_g[j], ev_w_out[j])
        else:
            y = odd_mixer(h, od_w_qkv[j], od_q_norm_g[j], od_k_norm_g[j], od_w_out[j])
        x = x + g1 * y
        h = rms_norm(x, norm_ffn_g[l]) * (1 + sc2) + sh2
        x = x + g2 * swiglu(h, ffn_w_gate[l], ffn_w_up[l], ffn_w_down[l])
    return x


import jax as _jax
import jax.numpy as _jnp

TWIN_FORMAT = 'train_step'
FWD_PARAMS = ['x', 'c', 'ada_w', 'ada_b', 'norm_mix_g', 'norm_ffn_g', 'ev_w_in', 'ev_conv_w', 'ev_ret_norm_g', 'ev_w_out', 'od_w_qkv', 'od_q_norm_g', 'od_k_norm_g', 'od_w_out', 'ffn_w_gate', 'ffn_w_up', 'ffn_w_down']
TWIN_WEIGHTS = ['ada_w', 'ada_b', 'norm_mix_g', 'norm_ffn_g', 'ev_w_in', 'ev_conv_w', 'ev_ret_norm_g', 'ev_w_out', 'od_w_qkv', 'od_q_norm_g', 'od_k_norm_g', 'od_w_out', 'ffn_w_gate', 'ffn_w_up', 'ffn_w_down']
TWIN_DIFF_INPUT = 'x'
TWIN_INPUTS = ['x', 'c', 'ada_w', 'ada_b', 'norm_mix_g', 'norm_ffn_g', 'ev_w_in', 'ev_conv_w', 'ev_ret_norm_g', 'ev_w_out', 'od_w_qkv', 'od_q_norm_g', 'od_k_norm_g', 'od_w_out', 'ffn_w_gate', 'ffn_w_up', 'ffn_w_down', 'loss_target', 'm_ada_w', 'm_ada_b', 'm_norm_mix_g', 'm_norm_ffn_g', 'm_ev_w_in', 'm_ev_conv_w', 'm_ev_ret_norm_g', 'm_ev_w_out', 'm_od_w_qkv', 'm_od_q_norm_g', 'm_od_k_norm_g', 'm_od_w_out', 'm_ffn_w_gate', 'm_ffn_w_up', 'm_ffn_w_down', 'v_ada_w', 'v_ada_b', 'v_norm_mix_g', 'v_norm_ffn_g', 'v_ev_w_in', 'v_ev_conv_w', 'v_ev_ret_norm_g', 'v_ev_w_out', 'v_od_w_qkv', 'v_od_q_norm_g', 'v_od_k_norm_g', 'v_od_w_out', 'v_ffn_w_gate', 'v_ffn_w_up', 'v_ffn_w_down']
TWIN_OUTPUTS = ['loss', 'grad_x', 'grad_ada_w', 'grad_ada_b', 'grad_norm_mix_g', 'grad_norm_ffn_g', 'grad_ev_w_in', 'grad_ev_conv_w', 'grad_ev_ret_norm_g', 'grad_ev_w_out', 'grad_od_w_qkv', 'grad_od_q_norm_g', 'grad_od_k_norm_g', 'grad_od_w_out', 'grad_ffn_w_gate', 'grad_ffn_w_up', 'grad_ffn_w_down', 'delta_ada_w', 'delta_ada_b', 'delta_norm_mix_g', 'delta_norm_ffn_g', 'delta_ev_w_in', 'delta_ev_conv_w', 'delta_ev_ret_norm_g', 'delta_ev_w_out', 'delta_od_w_qkv', 'delta_od_q_norm_g', 'delta_od_k_norm_g', 'delta_od_w_out', 'delta_ffn_w_gate', 'delta_ffn_w_up', 'delta_ffn_w_down', 'new_m_ada_w', 'new_m_ada_b', 'new_m_norm_mix_g', 'new_m_norm_ffn_g', 'new_m_ev_w_in', 'new_m_ev_conv_w', 'new_m_ev_ret_norm_g', 'new_m_ev_w_out', 'new_m_od_w_qkv', 'new_m_od_q_norm_g', 'new_m_od_k_norm_g', 'new_m_od_w_out', 'new_m_ffn_w_gate', 'new_m_ffn_w_up', 'new_m_ffn_w_down', 'new_v_ada_w', 'new_v_ada_b', 'new_v_norm_mix_g', 'new_v_norm_ffn_g', 'new_v_ev_w_in', 'new_v_ev_conv_w', 'new_v_ev_ret_norm_g', 'new_v_ev_w_out', 'new_v_od_w_qkv', 'new_v_od_q_norm_g', 'new_v_od_k_norm_g', 'new_v_od_w_out', 'new_v_ffn_w_gate', 'new_v_ffn_w_up', 'new_v_ffn_w_down']
TWIN_LEAF_KINDS = {'loss': 'loss', 'grad_x': 'grad_x', 'grad_ada_w': 'grad_w', 'grad_ada_b': 'grad_w', 'grad_norm_mix_g': 'grad_w', 'grad_norm_ffn_g': 'grad_w', 'grad_ev_w_in': 'grad_w', 'grad_ev_conv_w': 'grad_w', 'grad_ev_ret_norm_g': 'grad_w', 'grad_ev_w_out': 'grad_w', 'grad_od_w_qkv': 'grad_w', 'grad_od_q_norm_g': 'grad_w', 'grad_od_k_norm_g': 'grad_w', 'grad_od_w_out': 'grad_w', 'grad_ffn_w_gate': 'grad_w', 'grad_ffn_w_up': 'grad_w', 'grad_ffn_w_down': 'grad_w', 'delta_ada_w': 'delta_w', 'delta_ada_b': 'delta_w', 'delta_norm_mix_g': 'delta_w', 'delta_norm_ffn_g': 'delta_w', 'delta_ev_w_in': 'delta_w', 'delta_ev_conv_w': 'delta_w', 'delta_ev_ret_norm_g': 'delta_w', 'delta_ev_w_out': 'delta_w', 'delta_od_w_qkv': 'delta_w', 'delta_od_q_norm_g': 'delta_w', 'delta_od_k_norm_g': 'delta_w', 'delta_od_w_out': 'delta_w', 'delta_ffn_w_gate': 'delta_w', 'delta_ffn_w_up': 'delta_w', 'delta_ffn_w_down': 'delta_w', 'new_m_ada_w': 'new_m', 'new_m_ada_b': 'new_m', 'new_m_norm_mix_g': 'new_m', 'new_m_norm_ffn_g': 'new_m', 'new_m_ev_w_in': 'new_m', 'new_m_ev_conv_w': 'new_m', 'new_m_ev_ret_norm_g': 'new_m', 'new_m_ev_w_out': 'new_m', 'new_m_od_w_qkv': 'new_m', 'new_m_od_q_norm_g': 'new_m', 'new_m_od_k_norm_g': 'new_m', 'new_m_od_w_out': 'new_m', 'new_m_ffn_w_gate': 'new_m', 'new_m_ffn_w_up': 'new_m', 'new_m_ffn_w_down': 'new_m', 'new_v_ada_w': 'new_v', 'new_v_ada_b': 'new_v', 'new_v_norm_mix_g': 'new_v', 'new_v_norm_ffn_g': 'new_v', 'new_v_ev_w_in': 'new_v', 'new_v_ev_conv_w': 'new_v', 'new_v_ev_ret_norm_g': 'new_v', 'new_v_ev_w_out': 'new_v', 'new_v_od_w_qkv': 'new_v', 'new_v_od_q_norm_g': 'new_v', 'new_v_od_k_norm_g': 'new_v', 'new_v_od_w_out': 'new_v', 'new_v_ffn_w_gate': 'new_v', 'new_v_ffn_w_up': 'new_v', 'new_v_ffn_w_down': 'new_v'}


def _forward(args):
    return _fwd_reference(*[args[k] for k in FWD_PARAMS])


def _output_shape():
    def fwd():
        inp = _fwd_setup_inputs(0)
        return _fwd_reference(*[inp[k] for k in FWD_PARAMS])
    out = _jax.eval_shape(fwd)
    return out.shape, out.dtype

N_MICROBATCH = 1
ADAM_LR = 0.001
ADAM_B1 = 0.9
ADAM_B2 = 0.999
ADAM_EPS = 1e-08
ADAM_WD = 0.01
ADAM_STEP = 10
PER_EXAMPLE_BATCH_AXIS = {'x': 0, 'c': 0, 'loss_target': 0}
SHARED_INPUTS = []
_WEIGHT_DTYPES = {'ada_w': _jnp.float32, 'ada_b': _jnp.float32, 'norm_mix_g': _jnp.float32, 'norm_ffn_g': _jnp.float32, 'ev_w_in': _jnp.float32, 'ev_conv_w': _jnp.float32, 'ev_ret_norm_g': _jnp.float32, 'ev_w_out': _jnp.float32, 'od_w_qkv': _jnp.float32, 'od_q_norm_g': _jnp.float32, 'od_k_norm_g': _jnp.float32, 'od_w_out': _jnp.float32, 'ffn_w_gate': _jnp.float32, 'ffn_w_up': _jnp.float32, 'ffn_w_down': _jnp.float32}
MOMENT_SCALE = {'ada_w': 1.136431e+00, 'ada_b': 3.042388e+00, 'norm_mix_g': 5.472522e+00, 'norm_ffn_g': 3.165437e+00, 'ev_w_in': 1.947249e-01, 'ev_conv_w': 2.792222e+00, 'ev_ret_norm_g': 1.295459e+00, 'ev_w_out': 1.653372e-01, 'od_w_qkv': 1.603268e-01, 'od_q_norm_g': 1.554087e+00, 'od_k_norm_g': 1.549153e+00, 'od_w_out': 2.266339e-01, 'ffn_w_gate': 7.221252e-02, 'ffn_w_up': 6.029230e-02, 'ffn_w_down': 9.664996e-02}


def _to_microbatches(a, axis):
    t = _jnp.moveaxis(a, axis, 0)
    t = t.reshape((N_MICROBATCH, t.shape[0] // N_MICROBATCH) + t.shape[1:])
    return _jnp.moveaxis(t, 1, axis + 1)


def setup_inputs(seed: int = 0) -> dict:
    inp = _fwd_setup_inputs(seed)
    key = _jax.random.fold_in(_jax.random.key(seed), 7919)
    shape, _ = _output_shape()
    out = dict(inp)
    out["loss_target"] = _jax.random.normal(_jax.random.fold_in(key, 0), shape, _jnp.float32)
    for i, name in enumerate(TWIN_WEIGHTS):
        w = inp[name].astype(_jnp.float32)
        if MOMENT_SCALE is None:
            s = _jnp.sqrt(_jnp.mean(_jnp.square(w)) + 1e-30)
        else:
            s = MOMENT_SCALE[name]
        km, kv = _jax.random.split(_jax.random.fold_in(key, i + 1))
        out[name] = w
        out["m_" + name] = s * _jax.random.normal(km, w.shape, _jnp.float32)
        out["v_" + name] = (s * s) * _jax.random.uniform(kv, w.shape, _jnp.float32, 0.5, 1.5)
    if N_MICROBATCH > 1:
        for name, axis in PER_EXAMPLE_BATCH_AXIS.items():
            out[name] = _to_microbatches(out[name], axis)
    return {'x': out['x'], 'c': out['c'], 'ada_w': out['ada_w'], 'ada_b': out['ada_b'], 'norm_mix_g': out['norm_mix_g'], 'norm_ffn_g': out['norm_ffn_g'], 'ev_w_in': out['ev_w_in'], 'ev_conv_w': out['ev_conv_w'], 'ev_ret_norm_g': out['ev_ret_norm_g'], 'ev_w_out': out['ev_w_out'], 'od_w_qkv': out['od_w_qkv'], 'od_q_norm_g': out['od_q_norm_g'], 'od_k_norm_g': out['od_k_norm_g'], 'od_w_out': out['od_w_out'], 'ffn_w_gate': out['ffn_w_gate'], 'ffn_w_up': out['ffn_w_up'], 'ffn_w_down': out['ffn_w_down'], 'loss_target': out['loss_target'], 'm_ada_w': out['m_ada_w'], 'm_ada_b': out['m_ada_b'], 'm_norm_mix_g': out['m_norm_mix_g'], 'm_norm_ffn_g': out['m_norm_ffn_g'], 'm_ev_w_in': out['m_ev_w_in'], 'm_ev_conv_w': out['m_ev_conv_w'], 'm_ev_ret_norm_g': out['m_ev_ret_norm_g'], 'm_ev_w_out': out['m_ev_w_out'], 'm_od_w_qkv': out['m_od_w_qkv'], 'm_od_q_norm_g': out['m_od_q_norm_g'], 'm_od_k_norm_g': out['m_od_k_norm_g'], 'm_od_w_out': out['m_od_w_out'], 'm_ffn_w_gate': out['m_ffn_w_gate'], 'm_ffn_w_up': out['m_ffn_w_up'], 'm_ffn_w_down': out['m_ffn_w_down'], 'v_ada_w': out['v_ada_w'], 'v_ada_b': out['v_ada_b'], 'v_norm_mix_g': out['v_norm_mix_g'], 'v_norm_ffn_g': out['v_norm_ffn_g'], 'v_ev_w_in': out['v_ev_w_in'], 'v_ev_conv_w': out['v_ev_conv_w'], 'v_ev_ret_norm_g': out['v_ev_ret_norm_g'], 'v_ev_w_out': out['v_ev_w_out'], 'v_od_w_qkv': out['v_od_w_qkv'], 'v_od_q_norm_g': out['v_od_q_norm_g'], 'v_od_k_norm_g': out['v_od_k_norm_g'], 'v_od_w_out': out['v_od_w_out'], 'v_ffn_w_gate': out['v_ffn_w_gate'], 'v_ffn_w_up': out['v_ffn_w_up'], 'v_ffn_w_down': out['v_ffn_w_down']}


def _loss(weights, diff, rest, loss_target):
    with _jax.named_scope("forward"):
        args = {**rest, TWIN_DIFF_INPUT: diff, **{k: w.astype(_WEIGHT_DTYPES[k]) for k, w in weights.items()}}
        y = _forward(args)
    with _jax.named_scope("loss_head"):
        err = _jnp.square(y.astype(_jnp.float32) - loss_target)
        return 0.5 * _jnp.sum(_jnp.mean(err, axis=-1)) if err.ndim else 0.5 * err


def _adamw(w, g, m, v):
    m = ADAM_B1 * m + (1.0 - ADAM_B1) * g
    v = ADAM_B2 * v + (1.0 - ADAM_B2) * _jnp.square(g)
    m_hat = m / (1.0 - ADAM_B1 ** ADAM_STEP)
    v_hat = v / (1.0 - ADAM_B2 ** ADAM_STEP)
    delta = -ADAM_LR * (m_hat / (_jnp.sqrt(v_hat) + ADAM_EPS) + ADAM_WD * w)
    return delta, m, v


def reference(x, c, ada_w, ada_b, norm_mix_g, norm_ffn_g, ev_w_in, ev_conv_w, ev_ret_norm_g, ev_w_out, od_w_qkv, od_q_norm_g, od_k_norm_g, od_w_out, ffn_w_gate, ffn_w_up, ffn_w_down, loss_target, m_ada_w, m_ada_b, m_norm_mix_g, m_norm_ffn_g, m_ev_w_in, m_ev_conv_w, m_ev_ret_norm_g, m_ev_w_out, m_od_w_qkv, m_od_q_norm_g, m_od_k_norm_g, m_od_w_out, m_ffn_w_gate, m_ffn_w_up, m_ffn_w_down, v_ada_w, v_ada_b, v_norm_mix_g, v_norm_ffn_g, v_ev_w_in, v_ev_conv_w, v_ev_ret_norm_g, v_ev_w_out, v_od_w_qkv, v_od_q_norm_g, v_od_k_norm_g, v_od_w_out, v_ffn_w_gate, v_ffn_w_up, v_ffn_w_down):
    given = dict(x=x, c=c, ada_w=ada_w, ada_b=ada_b, norm_mix_g=norm_mix_g, norm_ffn_g=norm_ffn_g, ev_w_in=ev_w_in, ev_conv_w=ev_conv_w, ev_ret_norm_g=ev_ret_norm_g, ev_w_out=ev_w_out, od_w_qkv=od_w_qkv, od_q_norm_g=od_q_norm_g, od_k_norm_g=od_k_norm_g, od_w_out=od_w_out, ffn_w_gate=ffn_w_gate, ffn_w_up=ffn_w_up, ffn_w_down=ffn_w_down, loss_target=loss_target, m_ada_w=m_ada_w, m_ada_b=m_ada_b, m_norm_mix_g=m_norm_mix_g, m_norm_ffn_g=m_norm_ffn_g, m_ev_w_in=m_ev_w_in, m_ev_conv_w=m_ev_conv_w, m_ev_ret_norm_g=m_ev_ret_norm_g, m_ev_w_out=m_ev_w_out, m_od_w_qkv=m_od_w_qkv, m_od_q_norm_g=m_od_q_norm_g, m_od_k_norm_g=m_od_k_norm_g, m_od_w_out=m_od_w_out, m_ffn_w_gate=m_ffn_w_gate, m_ffn_w_up=m_ffn_w_up, m_ffn_w_down=m_ffn_w_down, v_ada_w=v_ada_w, v_ada_b=v_ada_b, v_norm_mix_g=v_norm_mix_g, v_norm_ffn_g=v_norm_ffn_g, v_ev_w_in=v_ev_w_in, v_ev_conv_w=v_ev_conv_w, v_ev_ret_norm_g=v_ev_ret_norm_g, v_ev_w_out=v_ev_w_out, v_od_w_qkv=v_od_w_qkv, v_od_q_norm_g=v_od_q_norm_g, v_od_k_norm_g=v_od_k_norm_g, v_od_w_out=v_od_w_out, v_ffn_w_gate=v_ffn_w_gate, v_ffn_w_up=v_ffn_w_up, v_ffn_w_down=v_ffn_w_down)
    weights = {n: given[n] for n in TWIN_WEIGHTS}
    shared = {n: given[n] for n in SHARED_INPUTS}
    per_example = {n: given[n] for n in ['x', 'c']}
    grad_fn = _jax.value_and_grad(_loss, argnums=(0, 1))

    def one_microbatch(ex, loss_target):
        ex = dict(ex)
        diff = ex.pop(TWIN_DIFF_INPUT)
        return grad_fn(weights, diff, {**shared, **ex}, loss_target)

    if N_MICROBATCH == 1:
        loss, (grad_w, grad_x) = one_microbatch(per_example, given["loss_target"])
    else:
        def body(carry, xs):
            loss_sum, grad_sum = carry
            l_k, (gw_k, gx_k) = one_microbatch(xs[0], xs[1])
            with _jax.named_scope("update"):
                return (loss_sum + l_k, _jax.tree.map(_jnp.add, grad_sum, gw_k)), gx_k

        init = (_jnp.zeros((), _jnp.float32), _jax.tree.map(_jnp.zeros_like, weights))
        (loss, grad_w), grad_x = _jax.lax.scan(body, init, (per_example, given["loss_target"]))
    with _jax.named_scope("update"):
        delta_w, new_m, new_v = {}, {}, {}
        for n in TWIN_WEIGHTS:
            delta_w[n], new_m[n], new_v[n] = _adamw(weights[n], grad_w[n], given["m_" + n], given["v_" + n])
    return (loss, grad_x, *[grad_w[n] for n in TWIN_WEIGHTS], *[delta_w[n] for n in TWIN_WEIGHTS],
            *[new_m[n] for n in TWIN_WEIGHTS], *[new_v[n] for n in TWIN_WEIGHTS])
```

```python
import functools
import math

import numpy as np
import jax
import jax.numpy as jnp
from jax import lax
from jax.experimental import pallas as pl
from jax.experimental.pallas import tpu as pltpu

F32 = jnp.float32
BF16 = jnp.bfloat16
MESH = pl.DeviceIdType.MESH

N_DEV = 8
EPS = 1e-6
CHUNK = 64
HEAD = 128
RET_HEADS = 4
SB_HEADS = 8
ROPE_THETA = 10000.0
KEY_BLOCK = 128
ADAM_LR, ADAM_B1, ADAM_B2, ADAM_EPS, ADAM_WD, ADAM_STEP = 0.001, 0.9, 0.999, 1e-08, 0.01, 10
VMEM_LIMIT = 56 * 1024 * 1024


def _pcall(body, **kw):
    return pl.pallas_call(body, **kw)


def _params(n_grid=1, vmem=VMEM_LIMIT):
    return pltpu.CompilerParams(dimension_semantics=("arbitrary",) * n_grid, vmem_limit_bytes=vmem)


def _mm(a, b):
    return jnp.dot(a, b, preferred_element_type=F32)


def _mm_nt(a, b):
    return lax.dot_general(a, b, (((1,), (1,)), ((), ())), preferred_element_type=F32)


def _mm_tn(a, b):
    return lax.dot_general(a, b, (((0,), (0,)), ((), ())), preferred_element_type=F32)


def _bf(a):
    return a.astype(BF16)


def _sigmoid(a):
    return 1.0 / (1.0 + jnp.exp(-a))


def _sum0(a):
    return jnp.sum(a, axis=0, keepdims=True)


def _full(shape):
    nd = len(shape)
    return pl.BlockSpec(shape, lambda *_: (0,) * nd)


def _normmod_fwd(x, g, sc, sh):
    rstd = lax.rsqrt(jnp.mean(x * x, axis=-1, keepdims=True) + EPS)
    n = x * rstd
    return n, rstd, (n * g) * (1.0 + sc) + sh


def _normmod_bwd(dh, n, rstd, g, sc):
    dsh = _sum0(dh)
    dsc = _sum0(dh * (n * g))
    dg = _sum0(dh * n * (1.0 + sc))
    dn = dh * (g * (1.0 + sc))
    dx = rstd * (dn - n * jnp.mean(dn * n, axis=-1, keepdims=True))
    return dx, dsh, dsc, dg


def _rms_fwd(o):
    rstd = lax.rsqrt(jnp.mean(o * o, axis=-1, keepdims=True) + EPS)
    return o * rstd, rstd


def _rms_bwd(dn, n, rstd):
    return rstd * (dn - n * jnp.mean(dn * n, axis=-1, keepdims=True))


def _all_gather(x2d, name, in_hbm):
    m_per, n = x2d.shape
    space = pltpu.HBM if in_hbm else pltpu.VMEM

    def body(x_ref, out_ref, send_sems, recv_sems, local_sem):
        x, y, c = lax.axis_index("x"), lax.axis_index("y"), lax.axis_index("c")
        me, sibling = (x, y, c), (x, y, 1 - c)
        chips = [(1 - x, y), (x, 1 - y), (1 - x, 1 - y)]

        def rows(px, py, pc):
            return out_ref.at[pl.ds((4 * px + 2 * py + pc) * m_per, m_per), :]

        def copy(k, block, to, src=None):
            return pltpu.make_async_remote_copy(
                src_ref=rows(*block) if src is None else src, dst_ref=rows(*block),
                send_sem=send_sems.at[k], recv_sem=recv_sems.at[k],
                device_id=to, device_id_type=MESH)

        mine = pltpu.make_async_copy(x_ref, rows(*me), local_sem)
        mine.start()
        first = [copy(1 + j, me, (*chip, c), src=x_ref) for j, chip in enumerate(chips)]
        first += [copy(0, me, sibling, src=x_ref)]
        for cp in first:
            cp.start()
        passed = [copy(4 + j, (*chip, c), sibling) for j, chip in enumerate(chips)]
        for j, chip in enumerate(chips):
            copy(1 + j, (*chip, c), me).wait_recv()
            passed[j].start()
        copy(0, sibling, me).wait_recv()
        for j, chip in enumerate(chips):
            copy(4 + j, (*chip, 1 - c), me).wait_recv()
        for cp in first + passed:
            cp.wait_send()
        mine.wait()

    return _pcall(
        body, name=name,
        out_shape=jax.ShapeDtypeStruct((N_DEV * m_per, n), x2d.dtype),
        in_specs=[pl.BlockSpec(memory_space=space)],
        out_specs=pl.BlockSpec(memory_space=space),
        scratch_shapes=[pltpu.SemaphoreType.DMA((7,)), pltpu.SemaphoreType.DMA((7,)),
                        pltpu.SemaphoreType.DMA],
    )(x2d)


def _all_to_all(g, name):
    _, r, n = g.shape

    def body(g_ref, out_ref, send_sems, recv_sems, local_sem):
        x, y, c = lax.axis_index("x"), lax.axis_index("y"), lax.axis_index("c")
        me = 4 * x + 2 * y + c

        def peer(k):
            px = (1 - x) if (k >> 2) & 1 else x
            py = (1 - y) if (k >> 1) & 1 else y
            pc = (1 - c) if k & 1 else c
            return px, py, pc

        def copy(k):
            px, py, pc = peer(k)
            return pltpu.make_async_remote_copy(
                src_ref=g_ref.at[4 * px + 2 * py + pc], dst_ref=out_ref.at[me],
                send_sem=send_sems.at[k - 1], recv_sem=recv_sems.at[k - 1],
                device_id=(px, py, pc), device_id_type=MESH)

        def landed(k):
            px, py, pc = peer(k)
            q = 4 * px + 2 * py + pc
            return pltpu.make_async_remote_copy(
                src_ref=g_ref.at[q], dst_ref=out_ref.at[q],
                send_sem=send_sems.at[k - 1], recv_sem=recv_sems.at[k - 1],
                device_id=(px, py, pc), device_id_type=MESH)

        mine = pltpu.make_async_copy(g_ref.at[me], out_ref.at[me], local_sem)
        mine.start()
        order = [2, 4, 6, 3, 5, 7, 1]
        sent = [copy(k) for k in order]
        for cp in sent:
            cp.start()
        for k in order:
            landed(k).wait_recv()
        for cp in sent:
            cp.wait_send()
        mine.wait()

    return _pcall(
        body, name=name,
        out_shape=jax.ShapeDtypeStruct(g.shape, g.dtype),
        in_specs=[pl.BlockSpec(memory_space=pltpu.HBM)],
        out_specs=pl.BlockSpec(memory_space=pltpu.HBM),
        scratch_shapes=[pltpu.SemaphoreType.DMA((7,)), pltpu.SemaphoreType.DMA((7,)),
                        pltpu.SemaphoreType.DMA],
    )(g)


def _sum_slots(recv, name):
    _, r, n = recv.shape
    tr = r
    for cand in (512, 448, 384, 352, 256, 128, 64, 32, 16, 8):
        if r % cand == 0:
            tr = cand
            break

    def body(r_ref, o_ref):
        acc = r_ref[0].astype(F32)
        for p in range(1, N_DEV):
            acc = acc + r_ref[p].astype(F32)
        o_ref[...] = acc

    return _pcall(
        body, name=name, grid=(r // tr,),
        out_shape=jax.ShapeDtypeStruct((r, n), F32),
        in_specs=[pl.BlockSpec((N_DEV, tr, n), lambda i: (0, i, 0))],
        out_specs=pl.BlockSpec((tr, n), lambda i: (i, 0)),
        compiler_params=_params(1),
    )(recv)


def _adamw(w, g, m, v, name):
    r, n = w.shape
    tr = r
    for cand in (512, 256, 128, 64, 32, 16, 8):
        if r % cand == 0:
            tr = cand
            break
    bc1 = 1.0 / (1.0 - ADAM_B1 ** ADAM_STEP)
    bc2 = 1.0 / (1.0 - ADAM_B2 ** ADAM_STEP)

    def body(w_ref, g_ref, m_ref, v_ref, d_ref, nm_ref, nv_ref):
        gv = g_ref[...]
        nm = ADAM_B1 * m_ref[...] + (1.0 - ADAM_B1) * gv
        nv = ADAM_B2 * v_ref[...] + (1.0 - ADAM_B2) * (gv * gv)
        d_ref[...] = -ADAM_LR * ((nm * bc1) / (jnp.sqrt(nv * bc2) + ADAM_EPS) + ADAM_WD * w_ref[...])
        nm_ref[...] = nm
        nv_ref[...] = nv

    spec = pl.BlockSpec((tr, n), lambda i: (i, 0))
    shp = jax.ShapeDtypeStruct((r, n), F32)
    return _pcall(
        body, name=name, grid=(r // tr,), out_shape=(shp, shp, shp),
        in_specs=[spec] * 4, out_specs=(spec, spec, spec), compiler_params=_params(1),
    )(w, g, m, v)


def _adamw_nd(w, g, m, v, name):
    shp = w.shape
    f = lambda a: a.reshape(-1, shp[-1])
    d, nm, nv = _adamw(f(w), f(g), f(m), f(v), name)
    return d.reshape(shp), nm.reshape(shp), nv.reshape(shp)


def _ada_fwd(c_all, ada_w, ada_b_cols):
    n_l, d, cols = ada_w.shape

    def body(c_ref, w_ref, b_ref, o_ref):
        cv = c_ref[...]
        ca = cv * _sigmoid(cv)
        o_ref[...] = _mm(_bf(ca), _bf(w_ref[...])) + b_ref[...]

    return _pcall(
        body, name="ada_fwd", grid=(n_l,),
        out_shape=jax.ShapeDtypeStruct((n_l, N_DEV, cols), F32),
        in_specs=[_full((N_DEV, d)), pl.BlockSpec((None, d, cols), lambda l: (l, 0, 0)),
                  pl.BlockSpec((None, 1, cols), lambda l: (l, 0, 0))],
        out_specs=pl.BlockSpec((None, N_DEV, cols), lambda l: (l, 0, 0)),
        compiler_params=_params(1),
    )(c_all, ada_w, ada_b_cols.reshape(n_l, 1, cols))


def _ada_bwd(c_all_t, dmod_cols):
    d = c_all_t.shape[0]
    n_l, _, cols = dmod_cols.shape

    def body(ct_ref, dm_ref, o_ref):
        cv = ct_ref[...]
        ca = cv * _sigmoid(cv)
        dm = dm_ref[...]
        acc = ca[:, 0:1] * dm[0:1, :]
        for b in range(1, N_DEV):
            acc = acc + ca[:, b:b + 1] * dm[b:b + 1, :]
        o_ref[...] = acc

    return _pcall(
        body, name="ada_bwd", grid=(n_l,),
        out_shape=jax.ShapeDtypeStruct((n_l, d, cols), F32),
        in_specs=[_full((d, N_DEV)), pl.BlockSpec((None, N_DEV, cols), lambda l: (l, 0, 0))],
        out_specs=pl.BlockSpec((None, d, cols), lambda l: (l, 0, 0)),
        compiler_params=_params(1),
    )(c_all_t, dmod_cols)


def _sum_small(gathered):
    _, r, n = gathered.shape

    def body(g_ref, o_ref):
        acc = g_ref[0]
        for p in range(1, N_DEV):
            acc = acc + g_ref[p]
        o_ref[...] = acc

    return _pcall(
        body, name="sum_small", out_shape=jax.ShapeDtypeStruct((r, n), F32),
        in_specs=[_full((N_DEV, r, n))], out_specs=_full((r, n)),
    )(gathered)


def _loss_grad(xf, tgt, ts):
    s, d = xf.shape

    def body(x_ref, t_ref, dx_ref, l_ref):
        @pl.when(pl.program_id(0) == 0)
        def _():
            l_ref[...] = jnp.zeros_like(l_ref)
        e = x_ref[...] - t_ref[...]
        dx_ref[...] = e * (1.0 / d)
        l_ref[...] += (0.5 / d) * jnp.sum(jnp.sum(e * e, axis=1, keepdims=True), axis=0, keepdims=True)

    spec = pl.BlockSpec((ts, d), lambda i: (i, 0))
    return _pcall(
        body, name="loss_grad", grid=(s // ts,),
        out_shape=(jax.ShapeDtypeStruct((s, d), F32), jax.ShapeDtypeStruct((1, 1), F32)),
        in_specs=[spec, spec], out_specs=(spec, _full((1, 1))), compiler_params=_params(1),
    )(xf, tgt)


def _tn_matmul(a, col_block, b, buf, slot, name):
    s = a.shape[0]
    k = b.shape[1]
    n_p = buf.shape[2]
    mcols = N_DEV * n_p
    ts = 512 if s % 512 == 0 else 256
    nt = s // ts

    def body(a_ref, b_ref, buf_ref, o_ref, acc):
        i = pl.program_id(0)

        @pl.when(i == 0)
        def _():
            acc[...] = jnp.zeros_like(acc)
        acc[...] += _mm_tn(a_ref[...], b_ref[...])

        @pl.when(i == nt - 1)
        def _():
            o_ref[...] = acc[...].reshape(N_DEV, n_p, k).astype(BF16)

    return _pcall(
        body, name=name, grid=(nt,),
        out_shape=jax.ShapeDtypeStruct(buf.shape, BF16),
        in_specs=[pl.BlockSpec((ts, mcols), lambda i: (i, col_block)),
                  pl.BlockSpec((ts, k), lambda i: (i, 0)),
                  pl.BlockSpec(memory_space=pl.ANY)],
        out_specs=pl.BlockSpec((N_DEV, None, n_p, k), lambda i: (0, slot, 0, 0)),
        scratch_shapes=[pltpu.VMEM((mcols, k), F32)],
        input_output_aliases={2: 0},
        compiler_params=_params(1),
    )(a, b, buf)


def _wspec4(w, slot):
    _, _, n_p, k = w.shape
    return pl.BlockSpec((N_DEV, None, n_p, k), lambda i: (0, slot, 0, 0), pipeline_mode=pl.Buffered(1))


def _ffn_fwd(x1, modp, w352, l, ts):
    s, d = x1.shape
    n_l = w352.shape[1] // 3
    f_dim = N_DEV * w352.shape[2]

    def body(x_ref, mp_ref, wg_ref, wu_ref, wd_ref, x2_ref, f_ref):
        x = x_ref[...]
        _, _, h2 = _normmod_fwd(x, mp_ref[7:8, :], mp_ref[4:5, :], mp_ref[3:4, :])
        hb = _bf(h2)
        f = jnp.zeros((ts, d), F32)
        half_dev, fc = N_DEV // 2, f_dim // 2
        for part in range(2):
            dev0 = part * half_dev
            a = _mm_nt(hb, wg_ref[dev0:dev0 + half_dev].reshape(fc, d))
            b = _mm_nt(hb, wu_ref[dev0:dev0 + half_dev].reshape(fc, d))
            sv = (a * _sigmoid(a)) * b
            f = f + _mm(_bf(sv), wd_ref[dev0:dev0 + half_dev].reshape(fc, d))
        f_ref[...] = f
        x2_ref[...] = x + mp_ref[5:6, :] * f

    tile = pl.BlockSpec((ts, d), lambda i: (i, 0))
    shp = jax.ShapeDtypeStruct((s, d), F32)
    return _pcall(
        body, name="ffn_fwd", grid=(s // ts,), out_shape=(shp, shp),
        in_specs=[tile, _full(modp.shape), _wspec4(w352, l), _wspec4(w352, n_l + l),
                  _wspec4(w352, 2 * n_l + l)],
        out_specs=(tile, tile), compiler_params=_params(1),
    )(x1, modp, w352, w352, w352)


def _ffn_bwd(x1, f, dx2, modp, w352, l, ts):
    s, d = x1.shape
    n_l = w352.shape[1] // 3
    f_dim = N_DEV * w352.shape[2]

    def body(x_ref, f_ref, dx2_ref, mp_ref, wg_ref, wu_ref, wd_ref,
             dx1_ref, dab_ref, h2_ref, s_ref, df_ref, sg_ref):
        @pl.when(pl.program_id(0) == 0)
        def _():
            sg_ref[...] = jnp.zeros_like(sg_ref)
        x = x_ref[...]
        gffn, sc2, g2 = mp_ref[7:8, :], mp_ref[4:5, :], mp_ref[5:6, :]
        n, rstd, h2 = _normmod_fwd(x, gffn, sc2, mp_ref[3:4, :])
        hb = _bf(h2)
        dx2 = dx2_ref[...]
        dfb = _bf(g2 * dx2)
        dh2 = jnp.zeros((ts, d), F32)
        half_dev, fc = N_DEV // 2, f_dim // 2
        for part in range(2):
            dev0, c0 = part * half_dev, part * fc
            wg = wg_ref[dev0:dev0 + half_dev].reshape(fc, d)
            wu = wu_ref[dev0:dev0 + half_dev].reshape(fc, d)
            a = _mm_nt(hb, wg)
            b = _mm_nt(hb, wu)
            sig = _sigmoid(a)
            sa = a * sig
            s_ref[:, c0:c0 + fc] = _bf(sa * b)
            ds = _mm_nt(dfb, wd_ref[dev0:dev0 + half_dev].reshape(fc, d))
            dab = _bf(ds * b * (sig * (1.0 + a * (1.0 - sig))))
            dbb = _bf(ds * sa)
            dab_ref[:, c0:c0 + fc] = dab
            dab_ref[:, f_dim + c0:f_dim + c0 + fc] = dbb
            dh2 = dh2 + _mm(dab, wg) + _mm(dbb, wu)
        dxn, dsh, dsc, dg = _normmod_bwd(dh2, n, rstd, gffn, sc2)
        dx1_ref[...] = dx2 + dxn
        h2_ref[...] = hb
        df_ref[...] = dfb
        sg_ref[0:1, :] += dsh
        sg_ref[1:2, :] += dsc
        sg_ref[2:3, :] += _sum0(dx2 * f_ref[...])
        sg_ref[3:4, :] += dg

    tile = pl.BlockSpec((ts, d), lambda i: (i, 0))
    f32t = jax.ShapeDtypeStruct((s, d), F32)
    bft = jax.ShapeDtypeStruct((s, d), BF16)
    return _pcall(
        body, name="ffn_bwd", grid=(s // ts,),
        out_shape=(f32t, jax.ShapeDtypeStruct((s, 2 * f_dim), BF16), bft,
                   jax.ShapeDtypeStruct((s, f_dim), BF16), bft, jax.ShapeDtypeStruct((8, d), F32)),
        in_specs=[tile, tile, tile, _full(modp.shape), _wspec4(w352, l), _wspec4(w352, n_l + l),
                  _wspec4(w352, 2 * n_l + l)],
        out_specs=(tile, pl.BlockSpec((ts, 2 * f_dim), lambda i: (i, 0)), tile,
                   pl.BlockSpec((ts, f_dim), lambda i: (i, 0)), tile, _full((8, d))),
        compiler_params=_params(1),
    )(x1, f, dx2, modp, w352, w352, w352)


def _retention_consts(ts):
    h = np.arange(RET_HEADS, dtype=np.float64)
    log_g = np.log1p(-np.exp2(-5.0 - h))
    t = np.arange(ts)
    diff = t[:, None] - t[None, :]
    same = (t[:, None] // CHUNK) == (t[None, :] // CHUNK)
    later = (t[:, None] // CHUNK) > (t[None, :] // CHUNK)
    dm = np.where(same, np.abs(diff), np.where(later, diff, 0))[None] * log_g[:, None, None]
    dm = np.where((same | later)[None], np.exp(dm), 0.0)
    qd = np.exp((t[:, None] + 1.0) * log_g[None, :])
    kd = np.exp((ts - 1.0 - t[:, None]) * log_g[None, :])
    qd = np.repeat(qd, HEAD, axis=1)
    kd = np.repeat(kd, HEAD, axis=1)
    tdec = [float(np.exp(ts * lg)) for lg in log_g]
    return (jnp.asarray(dm, F32), jnp.asarray(qd, F32), jnp.asarray(kd, F32), tdec)


def _rope_tables(s):
    inv_freq = 1.0 / (ROPE_THETA ** (jnp.arange(0, HEAD, 2, dtype=F32) / HEAD))
    ang = jnp.arange(s, dtype=F32)[:, None] * inv_freq[None, :]
    cos, sin = jnp.cos(ang), jnp.sin(ang)
    return jnp.concatenate([cos, cos], axis=1), jnp.concatenate([-sin, sin], axis=1)


def _rope(v, cos, sin):
    return v * cos + pltpu.roll(v, HEAD // 2, 1) * sin


def _rope_t(dv, cos, sin):
    return dv * cos + pltpu.roll(dv * sin, HEAD // 2, 1)


def _shift_down(z, k, halo_ref):
    r = pltpu.roll(z, k, 0)
    rows = lax.broadcasted_iota(jnp.int32, z.shape, 0)
    for j in range(k):
        r = jnp.where(rows == j, halo_ref[8 - k + j:8 - k + j + 1, :], r)
    return r


def _shift_up(z, k, halo_ref):
    n = z.shape[0]
    r = pltpu.roll(z, n - k, 0)
    rows = lax.broadcasted_iota(jnp.int32, z.shape, 0)
    for j in range(k):
        r = jnp.where(rows == n - k + j, halo_ref[j:j + 1, :], r)
    return r


def _even_recompute(x, mp_ref, win, cw_ref, cos, sin, dm_ref, qd_ref, kd_ref, halo_ref, state_of):
    cd = 4 * HEAD
    n, rstd, h = _normmod_fwd(x, mp_ref[6:7, :], mp_ref[1:2, :], mp_ref[0:1, :])
    proj = _mm_nt(_bf(h), win)
    bg, cg, u = proj[:, 0:cd], proj[:, cd:2 * cd], proj[:, 2 * cd:3 * cd]
    z = cg * u
    z1 = _shift_down(z, 1, halo_ref)
    z2 = _shift_down(z, 2, halo_ref)
    conv = cw_ref[0:1, :] * z2 + cw_ref[1:2, :] * z1 + cw_ref[2:3, :] * z
    heads = []
    scale = HEAD ** -0.5
    for hh in range(RET_HEADS):
        lo = hh * HEAD
        q = proj[:, 3 * cd + lo:3 * cd + lo + HEAD]
        k = proj[:, 4 * cd + lo:4 * cd + lo + HEAD]
        v = proj[:, 5 * cd + lo:5 * cd + lo + HEAD]
        gate = proj[:, 6 * cd + lo:6 * cd + lo + HEAD]
        qr = _rope(q, cos, sin)
        kr = _rope(k, cos, sin) * scale
        sc = _mm_nt(_bf(qr), _bf(kr)) * dm_ref[hh]
        qs = qr * qd_ref[:, lo:lo + HEAD]
        ks = kr * kd_ref[:, lo:lo + HEAD]
        o = _mm(_bf(sc), _bf(v)) + _mm(_bf(qs), _bf(state_of(hh)))
        on, orstd = _rms_fwd(o)
        sig = _sigmoid(gate)
        heads.append(dict(qr=qr, kr=kr, v=v, gate=gate, sc=sc, qs=qs, ks=ks, on=on, orstd=orstd, sig=sig))
    return dict(n=n, rstd=rstd, h=h, bg=bg, cg=cg, u=u, z=z, z1=z1, z2=z2, conv=conv, heads=heads)


def _even_fwd(x, modp, w448, w128, l, cw, cos, sin, consts, ts):
    s, d = x.shape
    nt = s // ts
    dm, qd, kd, tdec = consts
    cd = 4 * HEAD
    e_in = N_DEV * w448.shape[2]

    def body(x_ref, mp_ref, win_ref, cw_ref, cos_ref, sin_ref, dm_ref, qd_ref, kd_ref, wout_ref,
             x1_ref, y_ref, st_ref, zh_ref, state, halo):
        @pl.when(pl.program_id(0) == 0)
        def _():
            state[...] = jnp.zeros_like(state)
            halo[...] = jnp.zeros_like(halo)
        xv = x_ref[...]
        st_ref[...] = state[...]
        zh_ref[...] = halo[...]
        r = _even_recompute(xv, mp_ref, win_ref[...].reshape(e_in, d), cw_ref, cos_ref[...], sin_ref[...],
                            dm_ref, qd_ref, kd_ref, halo, lambda hh: state[hh])
        halo[...] = r["z"][ts - 8:ts, :]
        parts = [r["bg"] * r["conv"]]
        for hh, hd in enumerate(r["heads"]):
            state[hh] = state[hh] * tdec[hh] + _mm_tn(_bf(hd["ks"]), _bf(hd["v"]))
            rg = cw_ref[3:4, hh * HEAD:(hh + 1) * HEAD]
            parts.append((hd["gate"] * hd["sig"]) * (hd["on"] * rg))
        mcat = jnp.concatenate(parts, axis=1)
        y = _mm(_bf(mcat), wout_ref[...].reshape(d, d))
        y_ref[...] = y
        x1_ref[...] = xv + mp_ref[2:3, :] * y

    tile = pl.BlockSpec((ts, d), lambda i: (i, 0))
    rt = pl.BlockSpec((ts, HEAD), lambda i: (i, 0))
    shp = jax.ShapeDtypeStruct((s, d), F32)
    return _pcall(
        body, name="even_fwd", grid=(nt,),
        out_shape=(shp, shp, jax.ShapeDtypeStruct((nt, RET_HEADS, HEAD, HEAD), F32),
                   jax.ShapeDtypeStruct((nt, 8, cd), F32)),
        in_specs=[tile, _full(modp.shape), _wspec4(w448, l), _full(cw.shape), rt, rt,
                  _full(dm.shape), _full(qd.shape), _full(kd.shape), _wspec4(w128, l)],
        out_specs=(tile, tile, pl.BlockSpec((None, RET_HEADS, HEAD, HEAD), lambda i: (i, 0, 0, 0)),
                   pl.BlockSpec((None, 8, cd), lambda i: (i, 0, 0))),
        scratch_shapes=[pltpu.VMEM((RET_HEADS, HEAD, HEAD), F32), pltpu.VMEM((8, cd), F32)],
        compiler_params=_params(1),
    )(x, modp, w448, cw, cos, sin, dm, qd, kd, w128)


def _even_bwd(x, dx1, y, states, zhalo, modp, w448, w128, l, cw, cos, sin, consts, ts):
    s, d = x.shape
    nt = s // ts
    dm, qd, kd, tdec = consts
    cd = 4 * HEAD
    e_in = N_DEV * w448.shape[2]
    scale = HEAD ** -0.5

    def body(x_ref, dx1_ref, y_ref, st_ref, zh_ref, mp_ref, win_ref, cw_ref, cos_ref, sin_ref,
             dm_ref, qd_ref, kd_ref, wout_ref,
             dx_ref, dproj_ref, h_ref, m_ref, dy_ref, sg_ref, gstate, halo_d):
        @pl.when(pl.program_id(0) == 0)
        def _():
            gstate[...] = jnp.zeros_like(gstate)
            halo_d[...] = jnp.zeros_like(halo_d)
            sg_ref[...] = jnp.zeros_like(sg_ref)
        xv = x_ref[...]
        cos, sin = cos_ref[...], sin_ref[...]
        win = win_ref[...].reshape(e_in, d)
        r = _even_recompute(xv, mp_ref, win, cw_ref, cos, sin, dm_ref, qd_ref, kd_ref, zh_ref,
                            lambda hh: st_ref[hh])
        parts = [r["bg"] * r["conv"]]
        for hh, hd in enumerate(r["heads"]):
            rg = cw_ref[3:4, hh * HEAD:(hh + 1) * HEAD]
            parts.append((hd["gate"] * hd["sig"]) * (hd["on"] * rg))
        m_ref[...] = _bf(jnp.concatenate(parts, axis=1))
        h_ref[...] = _bf(r["h"])

        dx1 = dx1_ref[...]
        dy = mp_ref[2:3, :] * dx1
        dyb = _bf(dy)
        dy_ref[...] = dyb
        sg_ref[2:3, :] += _sum0(dx1 * y_ref[...])
        dmix = _mm_nt(dyb, wout_ref[...].reshape(d, d))

        da_out = dmix[:, 0:cd]
        dbg = da_out * r["conv"]
        dconv = da_out * r["bg"]
        dc1 = _shift_up(dconv, 1, halo_d)
        dc2 = _shift_up(dconv, 2, halo_d)
        dz = cw_ref[2:3, :] * dconv + cw_ref[1:2, :] * dc1 + cw_ref[0:1, :] * dc2
        halo_d[...] = dconv[0:8, :]
        sg_ref[4:5, 0:cd] += _sum0(dconv * r["z2"])
        sg_ref[5:6, 0:cd] += _sum0(dconv * r["z1"])
        sg_ref[6:7, 0:cd] += _sum0(dconv * r["z"])
        dcg = dz * r["u"]
        du = dz * r["cg"]

        dqs, dks, dvs, dgs = [], [], [], []
        for hh, hd in enumerate(r["heads"]):
            lo = hh * HEAD
            rg = cw_ref[3:4, lo:lo + HEAD]
            dr = dmix[:, cd + lo:cd + lo + HEAD]
            sig, gate, on = hd["sig"], hd["gate"], hd["on"]
            rn = on * rg
            dgate = dr * rn * (sig * (1.0 + gate * (1.0 - sig)))
            drn = dr * (gate * sig)
            sg_ref[7:8, lo:lo + HEAD] += _sum0(drn * on)
            do = _rms_bwd(drn * rg, on, hd["orstd"])
            dob = _bf(do)
            gst = _bf(gstate[hh])
            scb = _bf(hd["sc"])
            vb = _bf(hd["v"])
            qrb, krb = _bf(hd["qr"]), _bf(hd["kr"])
            dv = _mm_tn(scb, dob) + _mm(_bf(hd["ks"]), gst)
            dsc = _bf(_mm_nt(dob, vb) * dm_ref[hh])
            dqr = _mm(dsc, krb) + _mm_nt(dob, _bf(st_ref[hh])) * qd_ref[:, lo:lo + HEAD]
            dkr = _mm_tn(dsc, qrb) + _mm_nt(vb, gst) * kd_ref[:, lo:lo + HEAD]
            gstate[hh] = gstate[hh] * tdec[hh] + _mm_tn(_bf(hd["qs"]), dob)
            dqs.append(_rope_t(dqr, cos, sin))
            dks.append(_rope_t(dkr * scale, cos, sin))
            dvs.append(dv)
            dgs.append(dgate)

        dproj = _bf(jnp.concatenate([dbg, dcg, du] + dqs + dks + dvs + dgs, axis=1))
        dproj_ref[...] = dproj
        dh = _mm(dproj, win)
        dxn, dsh, dsc1, dg = _normmod_bwd(dh, r["n"], r["rstd"], mp_ref[6:7, :], mp_ref[1:2, :])
        dx_ref[...] = dx1 + dxn
        sg_ref[0:1, :] += dsh
        sg_ref[1:2, :] += dsc1
        sg_ref[3:4, :] += dg

    rev = lambda i: (nt - 1 - i, 0)
    tile = pl.BlockSpec((ts, d), rev)
    rt = pl.BlockSpec((ts, HEAD), rev)
    bft = jax.ShapeDtypeStruct((s, d), BF16)
    return _pcall(
        body, name="even_bwd", grid=(nt,),
        out_shape=(jax.ShapeDtypeStruct((s, d), F32), jax.ShapeDtypeStruct((s, e_in), BF16), bft, bft, bft,
                   jax.ShapeDtypeStruct((8, d), F32)),
        in_specs=[tile, tile, tile,
                  pl.BlockSpec((None, RET_HEADS, HEAD, HEAD), lambda i: (nt - 1 - i, 0, 0, 0)),
                  pl.BlockSpec((None, 8, cd), lambda i: (nt - 1 - i, 0, 0)),
                  _full(modp.shape), _wspec4(w448, l), _full(cw.shape), rt, rt,
                  _full(dm.shape), _full(qd.shape), _full(kd.shape), _wspec4(w128, l)],
        out_specs=(tile, pl.BlockSpec((ts, e_in), rev), tile, tile, tile, _full((8, d))),
        scratch_shapes=[pltpu.VMEM((RET_HEADS, HEAD, HEAD), F32), pltpu.VMEM((8, cd), F32)],
        compiler_params=_params(1),
    )(x, dx1, y, states, zhalo, modp, w448, cw, cos, sin, dm, qd, kd, w128)


def _odd_qkv_fwd(x, modp, w384, j, qkg, ts):
    s, d = x.shape
    n3 = N_DEV * w384.shape[2]

    def body(x_ref, mp_ref, w_ref, g_ref, o_ref):
        _, _, h = _normmod_fwd(x_ref[...], mp_ref[6:7, :], mp_ref[1:2, :], mp_ref[0:1, :])
        qkv = _mm_nt(_bf(h), w_ref[...].reshape(n3, d))
        for hh in range(SB_HEADS):
            lo = hh * HEAD
            qn, _ = _rms_fwd(qkv[:, lo:lo + HEAD])
            kn, _ = _rms_fwd(qkv[:, d + lo:d + lo + HEAD])
            o_ref[:, lo:lo + HEAD] = _bf(qn * g_ref[0:1, :])
            o_ref[:, d + lo:d + lo + HEAD] = _bf(kn * g_ref[1:2, :])
        o_ref[:, 2 * d:3 * d] = _bf(qkv[:, 2 * d:3 * d])

    return _pcall(
        body, name="odd_qkv_fwd", grid=(s // ts,),
        out_shape=jax.ShapeDtypeStruct((s, n3), BF16),
        in_specs=[pl.BlockSpec((ts, d), lambda i: (i, 0)), _full(modp.shape), _wspec4(w384, j),
                  _full(qkg.shape)],
        out_specs=pl.BlockSpec((ts, n3), lambda i: (i, 0)), compiler_params=_params(1),
    )(x, modp, w384, qkg)


def _sb_scores(q, kblk, qpos, ks):
    z = _mm_nt(q, kblk) * (HEAD ** -0.5)
    kpos = ks + lax.broadcasted_iota(jnp.int32, z.shape, 1)
    mask = qpos > kpos
    e = jnp.exp(-jnp.abs(z))
    l1p = jnp.log(1.0 + e)
    lb = jnp.minimum(z, 0.0) - l1p
    lkm = jnp.where(mask, lb - z, 0.0)
    return z, mask, e, lb, lkm


def _split_dot(a, tri):
    hi = _bf(a)
    lo = _bf(a - hi.astype(F32))
    return _mm(hi, tri) + _mm(lo, tri)


def _sb_fwd(qkv, tq):
    s = qkv.shape[0]
    d = qkv.shape[1] // 3
    nq = s // tq
    per = tq // KEY_BLOCK

    def body(q_ref, k_ref, v_ref, o_ref, t_ref):
        qi = pl.program_id(1)
        q = q_ref[...]
        qpos = qi * tq + lax.broadcasted_iota(jnp.int32, (tq, 1), 0)
        ri = lax.broadcasted_iota(jnp.int32, (KEY_BLOCK, KEY_BLOCK), 0)
        ci = lax.broadcasted_iota(jnp.int32, (KEY_BLOCK, KEY_BLOCK), 1)
        upper = (ri > ci).astype(BF16)
        nkb = (qi + 1) * per

        def step(it, carry):
            o_acc, run = carry
            ks = pl.multiple_of((nkb - 1 - it) * KEY_BLOCK, KEY_BLOCK)
            kblk = k_ref[pl.ds(ks, KEY_BLOCK), :]
            vblk = v_ref[pl.ds(ks, KEY_BLOCK), :]
            _, mask, _, lb, lkm = _sb_scores(q, kblk, qpos, ks)
            acc = _split_dot(lkm, upper) + run
            w = jnp.where(mask, jnp.exp(lb + acc), 0.0)
            o_acc = o_acc + _mm(_bf(w), vblk)
            run = run + jnp.sum(lkm, axis=1, keepdims=True)
            return o_acc, run

        o_acc, run = lax.fori_loop(0, nkb, step, (jnp.zeros((tq, HEAD), F32), jnp.zeros((tq, 1), F32)))
        o_ref[...] = _bf(o_acc)
        t_ref[...] = run

    nh = d // HEAD
    return _pcall(
        body, name="sb_fwd", grid=(nh, nq),
        out_shape=(jax.ShapeDtypeStruct((s, d), BF16), jax.ShapeDtypeStruct((nh, s, 1), F32)),
        in_specs=[pl.BlockSpec((tq, HEAD), lambda h, i: (i, h)),
                  pl.BlockSpec((s, HEAD), lambda h, i: (0, nh + h)),
                  pl.BlockSpec((s, HEAD), lambda h, i: (0, 2 * nh + h))],
        out_specs=(pl.BlockSpec((tq, HEAD), lambda h, i: (i, h)),
                   pl.BlockSpec((None, tq, 1), lambda h, i: (h, i, 0))),
        compiler_params=_params(2),
    )(qkv, qkv, qkv)


def _sb_bwd(qkv, do, tot, tq):
    s = qkv.shape[0]
    d = qkv.shape[1] // 3
    nq = s // tq
    per = tq // KEY_BLOCK
    scale = HEAD ** -0.5

    def body(q_ref, k_ref, v_ref, do_ref, t_ref, dq_ref, dk_ref, dv_ref):
        qi = pl.program_id(1)

        @pl.when(qi == 0)
        def _():
            dk_ref[...] = jnp.zeros_like(dk_ref)
            dv_ref[...] = jnp.zeros_like(dv_ref)
        q = q_ref[...]
        dob = do_ref[...]
        total = t_ref[...]
        qpos = qi * tq + lax.broadcasted_iota(jnp.int32, (tq, 1), 0)
        ri = lax.broadcasted_iota(jnp.int32, (KEY_BLOCK, KEY_BLOCK), 0)
        ci = lax.broadcasted_iota(jnp.int32, (KEY_BLOCK, KEY_BLOCK), 1)
        upper = (ri > ci).astype(BF16)
        lower = (ri < ci).astype(BF16)
        nkb = (qi + 1) * per

        def step(jb, carry):
            dq_acc, pk, pd = carry
            ks = pl.multiple_of(jb * KEY_BLOCK, KEY_BLOCK)
            kblk = k_ref[pl.ds(ks, KEY_BLOCK), :]
            vblk = v_ref[pl.ds(ks, KEY_BLOCK), :]
            z, mask, e, lb, lkm = _sb_scores(q, kblk, qpos, ks)
            pk = pk + jnp.sum(lkm, axis=1, keepdims=True)
            acc = _split_dot(lkm, upper) + (total - pk)
            w = jnp.where(mask, jnp.exp(lb + acc), 0.0)
            wb = _bf(w)
            dv_ref[pl.ds(ks, KEY_BLOCK), :] += _mm_tn(wb, dob)
            de = _mm_nt(dob, vblk) * w
            dlk = _split_dot(de, lower) + pd
            pd = pd + jnp.sum(de, axis=1, keepdims=True)
            inv = 1.0 / (1.0 + e)
            sig = jnp.where(z >= 0.0, inv, e * inv)
            dz = jnp.where(mask, de * (1.0 - sig) - dlk * sig, 0.0) * scale
            dzb = _bf(dz)
            dq_acc = dq_acc + _mm(dzb, kblk)
            dk_ref[pl.ds(ks, KEY_BLOCK), :] += _mm_tn(dzb, q)
            return dq_acc, pk, pd

        zero1 = jnp.zeros((tq, 1), F32)
        dq_acc, _, _ = lax.fori_loop(0, nkb, step, (jnp.zeros((tq, HEAD), F32), zero1, zero1))
        dq_ref[...] = dq_acc

    nh = d // HEAD
    shp = jax.ShapeDtypeStruct((s, d), F32)
    return _pcall(
        body, name="sb_bwd", grid=(nh, nq), out_shape=(shp, shp, shp),
        in_specs=[pl.BlockSpec((tq, HEAD), lambda h, i: (i, h)),
                  pl.BlockSpec((s, HEAD), lambda h, i: (0, nh + h)),
                  pl.BlockSpec((s, HEAD), lambda h, i: (0, 2 * nh + h)),
                  pl.BlockSpec((tq, HEAD), lambda h, i: (i, h)),
                  pl.BlockSpec((None, tq, 1), lambda h, i: (h, i, 0))],
        out_specs=(pl.BlockSpec((tq, HEAD), lambda h, i: (i, h)),
                   pl.BlockSpec((s, HEAD), lambda h, i: (0, h)),
                   pl.BlockSpec((s, HEAD), lambda h, i: (0, h))),
        compiler_params=_params(2),
    )(qkv, qkv, qkv, do, tot)


def _odd_out_fwd(o, x, modp, w128, slot, ts):
    s, d = x.shape

    def body(o_ref, x_ref, mp_ref, w_ref, x1_ref, y_ref):
        y = _mm(o_ref[...], w_ref[...].reshape(d, d))
        y_ref[...] = y
        x1_ref[...] = x_ref[...] + mp_ref[2:3, :] * y

    tile = pl.BlockSpec((ts, d), lambda i: (i, 0))
    shp = jax.ShapeDtypeStruct((s, d), F32)
    return _pcall(
        body, name="odd_out_fwd", grid=(s // ts,), out_shape=(shp, shp),
        in_specs=[tile, tile, _full(modp.shape), _wspec4(w128, slot)],
        out_specs=(tile, tile), compiler_params=_params(1),
    )(o, x, modp, w128)


def _odd_out_bwd(dx1, y, modp, w128, slot, ts):
    s, d = dx1.shape

    def body(dx1_ref, y_ref, mp_ref, w_ref, do_ref, dy_ref, sg_ref):
        @pl.when(pl.program_id(0) == 0)
        def _():
            sg_ref[...] = jnp.zeros_like(sg_ref)
        dx1v = dx1_ref[...]
        dyb = _bf(mp_ref[2:3, :] * dx1v)
        dy_ref[...] = dyb
        do_ref[...] = _bf(_mm_nt(dyb, w_ref[...].reshape(d, d)))
        sg_ref[2:3, :] += _sum0(dx1v * y_ref[...])

    tile = pl.BlockSpec((ts, d), lambda i: (i, 0))
    bft = jax.ShapeDtypeStruct((s, d), BF16)
    return _pcall(
        body, name="odd_out_bwd", grid=(s // ts,),
        out_shape=(bft, bft, jax.ShapeDtypeStruct((8, d), F32)),
        in_specs=[tile, tile, _full(modp.shape), _wspec4(w128, slot)],
        out_specs=(tile, tile, _full((8, d))), compiler_params=_params(1),
    )(dx1, y, modp, w128)


def _odd_qkv_bwd(x, dx1, dq, dk, dv, sg_in, modp, w384, j, qkg, ts):
    s, d = x.shape
    n3 = N_DEV * w384.shape[2]

    def body(x_ref, dx1_ref, dq_ref, dk_ref, dv_ref, sgi_ref, mp_ref, w_ref, g_ref,
             dx_ref, dqkv_ref, h_ref, sg_ref):
        @pl.when(pl.program_id(0) == 0)
        def _():
            sg_ref[...] = sgi_ref[...]
        gmix, sc1 = mp_ref[6:7, :], mp_ref[1:2, :]
        n, rstd, h = _normmod_fwd(x_ref[...], gmix, sc1, mp_ref[0:1, :])
        hb = _bf(h)
        h_ref[...] = hb
        w = w_ref[...].reshape(n3, d)
        qkv = _mm_nt(hb, w)
        parts_q, parts_k = [], []
        gq, gk = g_ref[0:1, :], g_ref[1:2, :]
        dgq = jnp.zeros((1, HEAD), F32)
        dgk = jnp.zeros((1, HEAD), F32)
        for hh in range(SB_HEADS):
            lo = hh * HEAD
            qn, qr = _rms_fwd(qkv[:, lo:lo + HEAD])
            kn, kr = _rms_fwd(qkv[:, d + lo:d + lo + HEAD])
            dqn = dq_ref[:, lo:lo + HEAD]
            dkn = dk_ref[:, lo:lo + HEAD]
            dgq = dgq + _sum0(dqn * qn)
            dgk = dgk + _sum0(dkn * kn)
            parts_q.append(_rms_bwd(dqn * gq, qn, qr))
            parts_k.append(_rms_bwd(dkn * gk, kn, kr))
        dqkv = _bf(jnp.concatenate(parts_q + parts_k + [dv_ref[...]], axis=1))
        dqkv_ref[...] = dqkv
        dh = _mm(dqkv, w)
        dxn, dsh, dsc, dg = _normmod_bwd(dh, n, rstd, gmix, sc1)
        dx_ref[...] = dx1_ref[...] + dxn
        sg_ref[0:1, :] += dsh
        sg_ref[1:2, :] += dsc
        sg_ref[3:4, :] += dg
        sg_ref[4:5, 0:HEAD] += dgq
        sg_ref[5:6, 0:HEAD] += dgk

    tile = pl.BlockSpec((ts, d), lambda i: (i, 0))
    return _pcall(
        body, name="odd_qkv_bwd", grid=(s // ts,),
        out_shape=(jax.ShapeDtypeStruct((s, d), F32), jax.ShapeDtypeStruct((s, n3), BF16),
                   jax.ShapeDtypeStruct((s, d), BF16), jax.ShapeDtypeStruct((8, d), F32)),
        in_specs=[tile, tile, tile, tile, tile, _full((8, d)), _full(modp.shape), _wspec4(w384, j),
                  _full(qkg.shape)],
        out_specs=(tile, pl.BlockSpec((ts, n3), lambda i: (i, 0)), tile, _full((8, d))),
        compiler_params=_params(1),
    )(x, dx1, dq, dk, dv, sg_in, modp, w384, qkg)


def _pad_rows(a, rows):
    return jnp.concatenate([a, jnp.zeros((rows - a.shape[0],) + a.shape[1:], a.dtype)], axis=0)


def kernel(x, c, ada_w, ada_b, norm_mix_g, norm_ffn_g, ev_w_in, ev_conv_w, ev_ret_norm_g, ev_w_out, od_w_qkv, od_q_norm_g, od_k_norm_g, od_w_out, ffn_w_gate, ffn_w_up, ffn_w_down, loss_target, m_ada_w, m_ada_b, m_norm_mix_g, m_norm_ffn_g, m_ev_w_in, m_ev_conv_w, m_ev_ret_norm_g, m_ev_w_out, m_od_w_qkv, m_od_q_norm_g, m_od_k_norm_g, m_od_w_out, m_ffn_w_gate, m_ffn_w_up, m_ffn_w_down, v_ada_w, v_ada_b, v_norm_mix_g, v_norm_ffn_g, v_ev_w_in, v_ev_conv_w, v_ev_ret_norm_g, v_ev_w_out, v_od_w_qkv, v_od_q_norm_g, v_od_k_norm_g, v_od_w_out, v_ffn_w_gate, v_ffn_w_up, v_ffn_w_down):
    me = 4 * lax.axis_index("x") + 2 * lax.axis_index("y") + lax.axis_index("c")
    xs = x[0]
    tgt = loss_target[0]
    s, d = xs.shape
    depth = ada_w.shape[0]
    n_even, n_odd = ev_w_in.shape[0], od_w_qkv.shape[0]
    ts = 256
    tq = 256
    cd = 4 * HEAD
    cc = ev_conv_w.shape[2]

    pack0 = jnp.zeros((8, d), F32).at[0].set(c[0]).at[1, :n_even * 3 * cc].set(ev_conv_w.reshape(-1))
    got0 = _all_gather(pack0, "gather_cond", False).reshape(N_DEV, 8, d)
    c_all = got0[:, 0, :]
    conv_all = got0[:, 1, :n_even * 3 * cc].reshape(N_DEV, n_even, 3, cc).transpose(1, 2, 0, 3)
    conv_all = conv_all.reshape(n_even, 3, N_DEV * cc)
    cols = ada_w.shape[2]
    ada_b_cols = lax.dynamic_slice(ada_b, (0, me * cols), (depth, cols))
    mod_cols = _ada_fwd(c_all, ada_w, ada_b_cols)
    got1 = _all_gather(mod_cols.reshape(depth * N_DEV, cols), "gather_mod", False)
    got1 = got1.reshape(N_DEV, depth, N_DEV, cols)
    mod = lax.dynamic_index_in_dim(got1, me, axis=2, keepdims=False)
    mod = mod.transpose(1, 0, 2).reshape(depth, 6, d)
    modps = [jnp.concatenate([mod[l], norm_mix_g[l][None], norm_ffn_g[l][None]], axis=0) for l in range(depth)]

    tr = lambda w: _bf(w.transpose(0, 2, 1))
    grp352 = jnp.concatenate([tr(ffn_w_gate), tr(ffn_w_up), _bf(ffn_w_down)], axis=0)
    grp448 = tr(ev_w_in)
    grp384 = tr(od_w_qkv)
    grp128 = jnp.concatenate([_bf(ev_w_out), _bf(od_w_out)], axis=0)

    def gather_w(grp, name):
        n_s, n_p, k = grp.shape
        return _all_gather(grp.reshape(n_s * n_p, k), name, True).reshape(N_DEV, n_s, n_p, k)

    w448 = gather_w(grp448, "gather_w448")
    w128 = gather_w(grp128, "gather_w128")
    w352 = gather_w(grp352, "gather_w352")
    w384 = gather_w(grp384, "gather_w384")

    cos, sin = _rope_tables(s)
    consts = _retention_consts(ts)
    cws = [_pad_rows(jnp.concatenate([conv_all[j], ev_ret_norm_g[j][None]], axis=0), 8) for j in range(n_even)]
    qkgs = [_pad_rows(jnp.stack([od_q_norm_g[j], od_k_norm_g[j]]), 8) for j in range(n_odd)]

    saved = []
    cur = xs
    for l in range(depth):
        j = l // 2
        if l % 2 == 0:
            x1, y, states, zhalo = _even_fwd(cur, modps[l], w448, w128, j, cws[j], cos, sin, consts, ts)
            mix = (states, zhalo)
        else:
            qkv = _odd_qkv_fwd(cur, modps[l], w384, j, qkgs[j], ts)
            o, tot = _sb_fwd(qkv, tq)
            x1, y = _odd_out_fwd(o, cur, modps[l], w128, n_even + j, ts)
            mix = (qkv, o, tot)
        x2, f = _ffn_fwd(x1, modps[l], w352, l, ts)
        saved.append((cur, x1, y, f, mix))
        cur = x2

    dx, loss_part = _loss_grad(cur, tgt, ts)
    loss = lax.psum(loss_part[0, 0], ("x", "y", "c"))

    g352 = jnp.zeros(w352.shape, BF16)
    g448 = jnp.zeros(w448.shape, BF16)
    g384 = jnp.zeros(w384.shape, BF16)
    g128 = jnp.zeros(w128.shape, BF16)
    dmod = [None] * depth
    d_gmix = [None] * depth
    d_gffn = [None] * depth
    d_conv = [None] * n_even
    d_retg = [None] * n_even
    d_qg = [None] * n_odd
    d_kg = [None] * n_odd
    f_dim = N_DEV * w352.shape[2]
    for l in reversed(range(depth)):
        j = l // 2
        x0, x1, y, f, mix = saved[l]
        dx1, dab, h2, sv, df, sg2 = _ffn_bwd(x1, f, dx, modps[l], w352, l, ts)
        g352 = _tn_matmul(dab, 0, h2, g352, l, "tn_gate")
        g352 = _tn_matmul(dab, 1, h2, g352, depth + l, "tn_up")
        g352 = _tn_matmul(sv, 0, df, g352, 2 * depth + l, "tn_down")
        if l % 2 == 0:
            states, zhalo = mix
            dx, dproj, hb, mb, dyb, sg1 = _even_bwd(x0, dx1, y, states, zhalo, modps[l], w448, w128, j,
                                                    cws[j], cos, sin, consts, ts)
            g448 = _tn_matmul(dproj, 0, hb, g448, j, "tn_ev_in")
            g128 = _tn_matmul(mb, 0, dyb, g128, j, "tn_ev_out")
            d_conv[j] = sg1[4:7, :cd]
            d_retg[j] = sg1[7, :cd]
        else:
            qkv, o, tot = mix
            do, dyb, sg0 = _odd_out_bwd(dx1, y, modps[l], w128, n_even + j, ts)
            dq, dk, dv = _sb_bwd(qkv, do, tot, tq)
            dx, dqkv, hb, sg1 = _odd_qkv_bwd(x0, dx1, dq, dk, dv, sg0, modps[l], w384, j, qkgs[j], ts)
            g384 = _tn_matmul(dqkv, 0, hb, g384, j, "tn_od_qkv")
            g128 = _tn_matmul(o, 0, dyb, g128, n_even + j, "tn_od_out")
            d_qg[j] = sg1[4, :HEAD]
            d_kg[j] = sg1[5, :HEAD]
        dmod[l] = jnp.concatenate([sg1[0:3], sg2[0:3]], axis=0).reshape(-1)
        d_gmix[l] = sg1[3]
        d_gffn[l] = sg2[3]

    small = jnp.concatenate(
        [jnp.stack(dmod).reshape(-1), jnp.stack(d_gmix).reshape(-1), jnp.stack(d_gffn).reshape(-1),
         jnp.stack(d_retg).reshape(-1), jnp.stack(d_qg).reshape(-1), jnp.stack(d_kg).reshape(-1),
         jnp.stack(d_conv).reshape(-1)])
    n_small = small.shape[0]
    rows_small = -(-n_small // (8 * 128)) * 8
    small = jnp.concatenate([small, jnp.zeros((rows_small * 128 - n_small,), F32)]).reshape(rows_small, 128)
    got2 = _all_gather(small, "gather_small", False).reshape(N_DEV, rows_small, 128)
    tot_small = _sum_small(got2).reshape(-1)
    n_mod = depth * 6 * d
    dmod_all = got2.reshape(N_DEV, -1)[:, :n_mod].reshape(N_DEV, depth, 6 * d)
    dmod_cols = lax.dynamic_slice(dmod_all, (0, 0, me * cols), (N_DEV, depth, cols)).transpose(1, 0, 2)
    g_ada_w = _ada_bwd(c_all.T, dmod_cols)

    off = [0]

    def take(shape):
        n = int(np.prod(shape))
        out = tot_small[off[0]:off[0] + n].reshape(shape)
        off[0] += n
        return out

    g_ada_b = take((depth, 6 * d))
    g_norm_mix = take((depth, d))
    g_norm_ffn = take((depth, d))
    g_ret_norm = take((n_even, cd))
    g_q_norm = take((n_odd, HEAD))
    g_k_norm = take((n_odd, HEAD))
    g_conv_full = take((n_even, 3, cd))
    g_conv = lax.dynamic_slice(g_conv_full, (0, 0, me * cc), (n_even, 3, cc))

    def exchange(buf, name):
        n_s, n_p, k = buf.shape[1:]
        recv = _all_to_all(buf.reshape(N_DEV, n_s * n_p, k), "a2a_" + name)
        return _sum_slots(recv, "sum_" + name).reshape(n_s, n_p, k)

    s352 = exchange(g352, "g352")
    s448 = exchange(g448, "g448")
    s384 = exchange(g384, "g384")
    s128 = exchange(g128, "g128")
    un = lambda a: a.transpose(0, 2, 1)
    g_gate, g_up, g_down = un(s352[0:depth]), un(s352[depth:2 * depth]), s352[2 * depth:3 * depth]
    g_ev_in, g_od_qkv = un(s448), un(s384)
    g_ev_out, g_od_out = s128[0:n_even], s128[n_even:n_even + n_odd]

    big = [("ada_w", ada_w, g_ada_w, m_ada_w, v_ada_w), ("ev_w_in", ev_w_in, g_ev_in, m_ev_w_in, v_ev_w_in),
           ("ev_w_out", ev_w_out, g_ev_out, m_ev_w_out, v_ev_w_out),
           ("od_w_qkv", od_w_qkv, g_od_qkv, m_od_w_qkv, v_od_w_qkv),
           ("od_w_out", od_w_out, g_od_out, m_od_w_out, v_od_w_out),
           ("ffn_w_gate", ffn_w_gate, g_gate, m_ffn_w_gate, v_ffn_w_gate),
           ("ffn_w_up", ffn_w_up, g_up, m_ffn_w_up, v_ffn_w_up),
           ("ffn_w_down", ffn_w_down, g_down, m_ffn_w_down, v_ffn_w_down)]
    res = {}
    for name, w, g, m, v in big:
        res[name] = (g,) + _adamw_nd(w, g, m, v, "adamw_" + name)

    smalls = [("ada_b", ada_b, g_ada_b, m_ada_b, v_ada_b), ("norm_mix_g", norm_mix_g, g_norm_mix, m_norm_mix_g, v_norm_mix_g),
              ("norm_ffn_g", norm_ffn_g, g_norm_ffn, m_norm_ffn_g, v_norm_ffn_g),
              ("ev_conv_w", ev_conv_w, g_conv, m_ev_conv_w, v_ev_conv_w),
              ("ev_ret_norm_g", ev_ret_norm_g, g_ret_norm, m_ev_ret_norm_g, v_ev_ret_norm_g),
              ("od_q_norm_g", od_q_norm_g, g_q_norm, m_od_q_norm_g, v_od_q_norm_g),
              ("od_k_norm_g", od_k_norm_g, g_k_norm, m_od_k_norm_g, v_od_k_norm_g)]

    def pack(arrs):
        flat = jnp.concatenate([a.reshape(-1) for a in arrs])
        rows = -(-flat.shape[0] // (8 * 128)) * 8
        return jnp.concatenate([flat, jnp.zeros((rows * 128 - flat.shape[0],), F32)]).reshape(rows, 128)

    sd, sm, sv_ = _adamw(pack([t[1] for t in smalls]), pack([t[2] for t in smalls]),
                         pack([t[3] for t in smalls]), pack([t[4] for t in smalls]), "adamw_small")
    sd, sm, sv_ = sd.reshape(-1), sm.reshape(-1), sv_.reshape(-1)
    pos = 0
    for name, w, g, m, v in smalls:
        n = int(np.prod(w.shape))
        res[name] = (g, sd[pos:pos + n].reshape(w.shape), sm[pos:pos + n].reshape(w.shape),
                     sv_[pos:pos + n].reshape(w.shape))
        pos += n

    order = ["ada_w", "ada_b", "norm_mix_g", "norm_ffn_g", "ev_w_in", "ev_conv_w", "ev_ret_norm_g", "ev_w_out",
             "od_w_qkv", "od_q_norm_g", "od_k_norm_g", "od_w_out", "ffn_w_gate", "ffn_w_up", "ffn_w_down"]
    outs = [loss, dx[None]]
    for k in range(4):
        outs += [res[name][k] for name in order]
    return tuple(outs)
```

```python
import functools
import math

import numpy as np
import jax
import jax.numpy as jnp
from jax import lax
from jax.experimental import pallas as pl
from jax.experimental.pallas import tpu as pltpu

F32 = jnp.float32
BF16 = jnp.bfloat16
MESH = pl.DeviceIdType.MESH

N_DEV = 8
EPS = 1e-6
CHUNK = 64
HEAD = 128
RET_HEADS = 4
SB_HEADS = 8
ROPE_THETA = 10000.0
KEY_BLOCK = 128
ADAM_LR, ADAM_B1, ADAM_B2, ADAM_EPS, ADAM_WD, ADAM_STEP = 0.001, 0.9, 0.999, 1e-08, 0.01, 10
VMEM_LIMIT = 56 * 1024 * 1024


def _pcall(body, **kw):
    return pl.pallas_call(body, **kw)


def _params(n_grid=1, vmem=VMEM_LIMIT):
    return pltpu.CompilerParams(dimension_semantics=("arbitrary",) * n_grid, vmem_limit_bytes=vmem)


def _mm(a, b):
    return jnp.dot(a, b, preferred_element_type=F32)


def _mm_nt(a, b):
    return lax.dot_general(a, b, (((1,), (1,)), ((), ())), preferred_element_type=F32)


def _mm_tn(a, b):
    return lax.dot_general(a, b, (((0,), (0,)), ((), ())), preferred_element_type=F32)


def _bf(a):
    return a.astype(BF16)


def _sigmoid(a):
    return 1.0 / (1.0 + jnp.exp(-a))


def _sum0(a):
    return jnp.sum(a, axis=0, keepdims=True)


def _full(shape):
    nd = len(shape)
    return pl.BlockSpec(shape, lambda *_: (0,) * nd)


def _normmod_fwd(x, g, sc, sh):
    rstd = lax.rsqrt(jnp.mean(x * x, axis=-1, keepdims=True) + EPS)
    n = x * rstd
    return n, rstd, (n * g) * (1.0 + sc) + sh


def _normmod_bwd(dh, n, rstd, g, sc):
    dsh = _sum0(dh)
    dsc = _sum0(dh * (n * g))
    dg = _sum0(dh * n * (1.0 + sc))
    dn = dh * (g * (1.0 + sc))
    dx = rstd * (dn - n * jnp.mean(dn * n, axis=-1, keepdims=True))
    return dx, dsh, dsc, dg


def _rms_fwd(o):
    rstd = lax.rsqrt(jnp.mean(o * o, axis=-1, keepdims=True) + EPS)
    return o * rstd, rstd


def _rms_bwd(dn, n, rstd):
    return rstd * (dn - n * jnp.mean(dn * n, axis=-1, keepdims=True))


def _all_gather(x2d, name, in_hbm):
    m_per, n = x2d.shape
    space = pltpu.HBM if in_hbm else pltpu.VMEM

    def body(x_ref, out_ref, send_sems, recv_sems, local_sem):
        x, y, c = lax.axis_index("x"), lax.axis_index("y"), lax.axis_index("c")
        me, sibling = (x, y, c), (x, y, 1 - c)
        chips = [(1 - x, y), (x, 1 - y), (1 - x, 1 - y)]

        def rows(px, py, pc):
            return out_ref.at[pl.ds((4 * px + 2 * py + pc) * m_per, m_per), :]

        def copy(k, block, to, src=None):
            return pltpu.make_async_remote_copy(
                src_ref=rows(*block) if src is None else src, dst_ref=rows(*block),
                send_sem=send_sems.at[k], recv_sem=recv_sems.at[k],
                device_id=to, device_id_type=MESH)

        mine = pltpu.make_async_copy(x_ref, rows(*me), local_sem)
        mine.start()
        first = [copy(1 + j, me, (*chip, c), src=x_ref) for j, chip in enumerate(chips)]
        first += [copy(0, me, sibling, src=x_ref)]
        for cp in first:
            cp.start()
        passed = [copy(4 + j, (*chip, c), sibling) for j, chip in enumerate(chips)]
        for j, chip in enumerate(chips):
            copy(1 + j, (*chip, c), me).wait_recv()
            passed[j].start()
        copy(0, sibling, me).wait_recv()
        for j, chip in enumerate(chips):
            copy(4 + j, (*chip, 1 - c), me).wait_recv()
        for cp in first + passed:
            cp.wait_send()
        mine.wait()

    return _pcall(
        body, name=name,
        out_shape=jax.ShapeDtypeStruct((N_DEV * m_per, n), x2d.dtype),
        in_specs=[pl.BlockSpec(memory_space=space)],
        out_specs=pl.BlockSpec(memory_space=space),
        scratch_shapes=[pltpu.SemaphoreType.DMA((7,)), pltpu.SemaphoreType.DMA((7,)),
                        pltpu.SemaphoreType.DMA],
    )(x2d)


def _all_to_all(g, name):
    _, r, n = g.shape

    def body(g_ref, out_ref, send_sems, recv_sems, local_sem):
        x, y, c = lax.axis_index("x"), lax.axis_index("y"), lax.axis_index("c")
        me = 4 * x + 2 * y + c

        def peer(k):
            px = (1 - x) if (k >> 2) & 1 else x
            py = (1 - y) if (k >> 1) & 1 else y
            pc = (1 - c) if k & 1 else c
            return px, py, pc

        def copy(k):
            px, py, pc = peer(k)
            return pltpu.make_async_remote_copy(
                src_ref=g_ref.at[4 * px + 2 * py + pc], dst_ref=out_ref.at[me],
                send_sem=send_sems.at[k - 1], recv_sem=recv_sems.at[k - 1],
                device_id=(px, py, pc), device_id_type=MESH)

        def landed(k):
            px, py, pc = peer(k)
            q = 4 * px + 2 * py + pc
            return pltpu.make_async_remote_copy(
                src_ref=g_ref.at[q], dst_ref=out_ref.at[q],
                send_sem=send_sems.at[k - 1], recv_sem=recv_sems.at[k - 1],
                device_id=(px, py, pc), device_id_type=MESH)

        mine = pltpu.make_async_copy(g_ref.at[me], out_ref.at[me], local_sem)
        mine.start()
        order = [2, 4, 6, 3, 5, 7, 1]
        sent = [copy(k) for k in order]
        for cp in sent:
            cp.start()
        for k in order:
            landed(k).wait_recv()
        for cp in sent:
            cp.wait_send()
        mine.wait()

    return _pcall(
        body, name=name,
        out_shape=jax.ShapeDtypeStruct(g.shape, g.dtype),
        in_specs=[pl.BlockSpec(memory_space=pltpu.HBM)],
        out_specs=pl.BlockSpec(memory_space=pltpu.HBM),
        scratch_shapes=[pltpu.SemaphoreType.DMA((7,)), pltpu.SemaphoreType.DMA((7,)),
                        pltpu.SemaphoreType.DMA],
    )(g)


def _sum_slots(recv, name):
    _, r, n = recv.shape
    tr = r
    for cand in (512, 448, 384, 352, 256, 128, 64, 32, 16, 8):
        if r % cand == 0:
            tr = cand
            break

    def body(r_ref, o_ref):
        acc = r_ref[0].astype(F32)
        for p in range(1, N_DEV):
            acc = acc + r_ref[p].astype(F32)
        o_ref[...] = acc

    return _pcall(
        body, name=name, grid=(r // tr,),
        out_shape=jax.ShapeDtypeStruct((r, n), F32),
        in_specs=[pl.BlockSpec((N_DEV, tr, n), lambda i: (0, i, 0))],
        out_specs=pl.BlockSpec((tr, n), lambda i: (i, 0)),
        compiler_params=_params(1),
    )(recv)


def _adamw(w, g, m, v, name):
    r, n = w.shape
    tr = r
    for cand in (512, 256, 128, 64, 32, 16, 8):
        if r % cand == 0:
            tr = cand
            break
    bc1 = 1.0 / (1.0 - ADAM_B1 ** ADAM_STEP)
    bc2 = 1.0 / (1.0 - ADAM_B2 ** ADAM_STEP)

    def body(w_ref, g_ref, m_ref, v_ref, d_ref, nm_ref, nv_ref):
        gv = g_ref[...]
        nm = ADAM_B1 * m_ref[...] + (1.0 - ADAM_B1) * gv
        nv = ADAM_B2 * v_ref[...] + (1.0 - ADAM_B2) * (gv * gv)
        d_ref[...] = -ADAM_LR * ((nm * bc1) / (jnp.sqrt(nv * bc2) + ADAM_EPS) + ADAM_WD * w_ref[...])
        nm_ref[...] = nm
        nv_ref[...] = nv

    spec = pl.BlockSpec((tr, n), lambda i: (i, 0))
    shp = jax.ShapeDtypeStruct((r, n), F32)
    return _pcall(
        body, name=name, grid=(r // tr,), out_shape=(shp, shp, shp),
        in_specs=[spec] * 4, out_specs=(spec, spec, spec), compiler_params=_params(1),
    )(w, g, m, v)


def _adamw_nd(w, g, m, v, name):
    shp = w.shape
    f = lambda a: a.reshape(-1, shp[-1])
    d, nm, nv = _adamw(f(w), f(g), f(m), f(v), name)
    return d.reshape(shp), nm.reshape(shp), nv.reshape(shp)


def _ada_fwd(c_all, ada_w, ada_b_cols):
    n_l, d, cols = ada_w.shape

    def body(c_ref, w_ref, b_ref, o_ref):
        cv = c_ref[...]
        ca = cv * _sigmoid(cv)
        o_ref[...] = _mm(_bf(ca), _bf(w_ref[...])) + b_ref[...]

    return _pcall(
        body, name="ada_fwd", grid=(n_l,),
        out_shape=jax.ShapeDtypeStruct((n_l, N_DEV, cols), F32),
        in_specs=[_full((N_DEV, d)), pl.BlockSpec((None, d, cols), lambda l: (l, 0, 0)),
                  pl.BlockSpec((None, 1, cols), lambda l: (l, 0, 0))],
        out_specs=pl.BlockSpec((None, N_DEV, cols), lambda l: (l, 0, 0)),
        compiler_params=_params(1),
    )(c_all, ada_w, ada_b_cols.reshape(n_l, 1, cols))


def _ada_bwd(c_all_t, dmod_cols):
    d = c_all_t.shape[0]
    n_l, _, cols = dmod_cols.shape

    def body(ct_ref, dm_ref, o_ref):
        cv = ct_ref[...]
        ca = cv * _sigmoid(cv)
        dm = dm_ref[...]
        acc = ca[:, 0:1] * dm[0:1, :]
        for b in range(1, N_DEV):
            acc = acc + ca[:, b:b + 1] * dm[b:b + 1, :]
        o_ref[...] = acc

    return _pcall(
        body, name="ada_bwd", grid=(n_l,),
        out_shape=jax.ShapeDtypeStruct((n_l, d, cols), F32),
        in_specs=[_full((d, N_DEV)), pl.BlockSpec((None, N_DEV, cols), lambda l: (l, 0, 0))],
        out_specs=pl.BlockSpec((None, d, cols), lambda l: (l, 0, 0)),
        compiler_params=_params(1),
    )(c_all_t, dmod_cols)


def _sum_small(gathered):
    _, r, n = gathered.shape

    def body(g_ref, o_ref):
        acc = g_ref[0]
        for p in range(1, N_DEV):
            acc = acc + g_ref[p]
        o_ref[...] = acc

    return _pcall(
        body, name="sum_small", out_shape=jax.ShapeDtypeStruct((r, n), F32),
        in_specs=[_full((N_DEV, r, n))], out_specs=_full((r, n)),
    )(gathered)


def _loss_grad(xf, tgt, ts):
    s, d = xf.shape

    def body(x_ref, t_ref, dx_ref, l_ref):
        @pl.when(pl.program_id(0) == 0)
        def _():
            l_ref[...] = jnp.zeros_like(l_ref)
        e = x_ref[...] - t_ref[...]
        dx_ref[...] = e * (1.0 / d)
        l_ref[...] += (0.5 / d) * jnp.sum(jnp.sum(e * e, axis=1, keepdims=True), axis=0, keepdims=True)

    spec = pl.BlockSpec((ts, d), lambda i: (i, 0))
    return _pcall(
        body, name="loss_grad", grid=(s // ts,),
        out_shape=(jax.ShapeDtypeStruct((s, d), F32), jax.ShapeDtypeStruct((1, 1), F32)),
        in_specs=[spec, spec], out_specs=(spec, _full((1, 1))), compiler_params=_params(1),
    )(xf, tgt)


def _tn_matmul(a, col_block, b, buf, slot, name):
    s = a.shape[0]
    k = b.shape[1]
    n_p = buf.shape[2]
    mcols = N_DEV * n_p
    ts = 512 if s % 512 == 0 else 256
    nt = s // ts

    def body(a_ref, b_ref, buf_ref, o_ref, acc):
        i = pl.program_id(0)

        @pl.when(i == 0)
        def _():
            acc[...] = jnp.zeros_like(acc)
        acc[...] += _mm_tn(a_ref[...], b_ref[...])

        @pl.when(i == nt - 1)
        def _():
            o_ref[...] = acc[...].reshape(N_DEV, n_p, k).astype(BF16)

    return _pcall(
        body, name=name, grid=(nt,),
        out_shape=jax.ShapeDtypeStruct(buf.shape, BF16),
        in_specs=[pl.BlockSpec((ts, mcols), lambda i: (i, col_block)),
                  pl.BlockSpec((ts, k), lambda i: (i, 0)),
                  pl.BlockSpec(memory_space=pl.ANY)],
        out_specs=pl.BlockSpec((N_DEV, None, n_p, k), lambda i: (0, slot, 0, 0)),
        scratch_shapes=[pltpu.VMEM((mcols, k), F32)],
        input_output_aliases={2: 0},
        compiler_params=_params(1),
    )(a, b, buf)


def _wspec4(w, slot):
    _, _, n_p, k = w.shape
    return pl.BlockSpec((N_DEV, None, n_p, k), lambda i: (0, slot, 0, 0), pipeline_mode=pl.Buffered(1))


def _ffn_fwd(x1, modp, w352, l, ts):
    s, d = x1.shape
    n_l = w352.shape[1] // 3
    f_dim = N_DEV * w352.shape[2]

    def body(x_ref, mp_ref, wg_ref, wu_ref, wd_ref, x2_ref, f_ref):
        x = x_ref[...]
        _, _, h2 = _normmod_fwd(x, mp_ref[7:8, :], mp_ref[4:5, :], mp_ref[3:4, :])
        hb = _bf(h2)
        f = jnp.zeros((ts, d), F32)
        half_dev, fc = N_DEV // 2, f_dim // 2
        for part in range(2):
            dev0 = part * half_dev
            a = _mm_nt(hb, wg_ref[dev0:dev0 + half_dev].reshape(fc, d))
            b = _mm_nt(hb, wu_ref[dev0:dev0 + half_dev].reshape(fc, d))
            sv = (a * _sigmoid(a)) * b
            f = f + _mm(_bf(sv), wd_ref[dev0:dev0 + half_dev].reshape(fc, d))
        f_ref[...] = f
        x2_ref[...] = x + mp_ref[5:6, :] * f

    tile = pl.BlockSpec((ts, d), lambda i: (i, 0))
    shp = jax.ShapeDtypeStruct((s, d), F32)
    return _pcall(
        body, name="ffn_fwd", grid=(s // ts,), out_shape=(shp, shp),
        in_specs=[tile, _full(modp.shape), _wspec4(w352, l), _wspec4(w352, n_l + l),
                  _wspec4(w352, 2 * n_l + l)],
        out_specs=(tile, tile), compiler_params=_params(1),
    )(x1, modp, w352, w352, w352)


def _ffn_bwd(x1, f, dx2, modp, w352, l, ts):
    s, d = x1.shape
    n_l = w352.shape[1] // 3
    f_dim = N_DEV * w352.shape[2]

    def body(x_ref, f_ref, dx2_ref, mp_ref, wg_ref, wu_ref, wd_ref,
             dx1_ref, dab_ref, h2_ref, s_ref, df_ref, sg_ref):
        @pl.when(pl.program_id(0) == 0)
        def _():
            sg_ref[...] = jnp.zeros_like(sg_ref)
        x = x_ref[...]
        gffn, sc2, g2 = mp_ref[7:8, :], mp_ref[4:5, :], mp_ref[5:6, :]
        n, rstd, h2 = _normmod_fwd(x, gffn, sc2, mp_ref[3:4, :])
        hb = _bf(h2)
        dx2 = dx2_ref[...]
        dfb = _bf(g2 * dx2)
        dh2 = jnp.zeros((ts, d), F32)
        half_dev, fc = N_DEV // 2, f_dim // 2
        for part in range(2):
            dev0, c0 = part * half_dev, part * fc
            wg = wg_ref[dev0:dev0 + half_dev].reshape(fc, d)
            wu = wu_ref[dev0:dev0 + half_dev].reshape(fc, d)
            a = _mm_nt(hb, wg)
            b = _mm_nt(hb, wu)
            sig = _sigmoid(a)
            sa = a * sig
            s_ref[:, c0:c0 + fc] = _bf(sa * b)
            ds = _mm_nt(dfb, wd_ref[dev0:dev0 + half_dev].reshape(fc, d))
            dab = _bf(ds * b * (sig * (1.0 + a * (1.0 - sig))))
            dbb = _bf(ds * sa)
            dab_ref[:, c0:c0 + fc] = dab
            dab_ref[:, f_dim + c0:f_dim + c0 + fc] = dbb
            dh2 = dh2 + _mm(dab, wg) + _mm(dbb, wu)
        dxn, dsh, dsc, dg = _normmod_bwd(dh2, n, rstd, gffn, sc2)
        dx1_ref[...] = dx2 + dxn
        h2_ref[...] = hb
        df_ref[...] = dfb
        sg_ref[0:1, :] += dsh
        sg_ref[1:2, :] += dsc
        sg_ref[2:3, :] += _sum0(dx2 * f_ref[...])
        sg_ref[3:4, :] += dg

    tile = pl.BlockSpec((ts, d), lambda i: (i, 0))
    f32t = jax.ShapeDtypeStruct((s, d), F32)
    bft = jax.ShapeDtypeStruct((s, d), BF16)
    return _pcall(
        body, name="ffn_bwd", grid=(s // ts,),
        out_shape=(f32t, jax.ShapeDtypeStruct((s, 2 * f_dim), BF16), bft,
                   jax.ShapeDtypeStruct((s, f_dim), BF16), bft, jax.ShapeDtypeStruct((8, d), F32)),
        in_specs=[tile, tile, tile, _full(modp.shape), _wspec4(w352, l), _wspec4(w352, n_l + l),
                  _wspec4(w352, 2 * n_l + l)],
        out_specs=(tile, pl.BlockSpec((ts, 2 * f_dim), lambda i: (i, 0)), tile,
                   pl.BlockSpec((ts, f_dim), lambda i: (i, 0)), tile, _full((8, d))),
        compiler_params=_params(1),
    )(x1, f, dx2, modp, w352, w352, w352)


def _retention_consts(ts):
    h = np.arange(RET_HEADS, dtype=np.float64)
    log_g = np.log1p(-np.exp2(-5.0 - h))
    t = np.arange(ts)
    diff = t[:, None] - t[None, :]
    same = (t[:, None] // CHUNK) == (t[None, :] // CHUNK)
    later = (t[:, None] // CHUNK) > (t[None, :] // CHUNK)
    dm = np.where(same, np.abs(diff), np.where(later, diff, 0))[None] * log_g[:, None, None]
    dm = np.where((same | later)[None], np.exp(dm), 0.0)
    qd = np.exp((t[:, None] + 1.0) * log_g[None, :])
    kd = np.exp((ts - 1.0 - t[:, None]) * log_g[None, :])
    qd = np.repeat(qd, HEAD, axis=1)
    kd = np.repeat(kd, HEAD, axis=1)
    tdec = [float(np.exp(ts * lg)) for lg in log_g]
    return (jnp.asarray(dm, F32), jnp.asarray(qd, F32), jnp.asarray(kd, F32), tdec)


def _rope_tables(s):
    inv_freq = 1.0 / (ROPE_THETA ** (jnp.arange(0, HEAD, 2, dtype=F32) / HEAD))
    ang = jnp.arange(s, dtype=F32)[:, None] * inv_freq[None, :]
    cos, sin = jnp.cos(ang), jnp.sin(ang)
    return jnp.concatenate([cos, cos], axis=1), jnp.concatenate([-sin, sin], axis=1)


def _rope(v, cos, sin):
    return v * cos + pltpu.roll(v, HEAD // 2, 1) * sin


def _rope_t(dv, cos, sin):
    return dv * cos + pltpu.roll(dv * sin, HEAD // 2, 1)


def _shift_down(z, k, halo_ref):
    r = pltpu.roll(z, k, 0)
    rows = lax.broadcasted_iota(jnp.int32, z.shape, 0)
    for j in range(k):
        r = jnp.where(rows == j, halo_ref[8 - k + j:8 - k + j + 1, :], r)
    return r


def _shift_up(z, k, halo_ref):
    n = z.shape[0]
    r = pltpu.roll(z, n - k, 0)
    rows = lax.broadcasted_iota(jnp.int32, z.shape, 0)
    for j in range(k):
        r = jnp.where(rows == n - k + j, halo_ref[j:j + 1, :], r)
    return r


def _even_recompute(x, mp_ref, win, cw_ref, cos, sin, dm_ref, qd_ref, kd_ref, halo_ref, state_of):
    cd = 4 * HEAD
    n, rstd, h = _normmod_fwd(x, mp_ref[6:7, :], mp_ref[1:2, :], mp_ref[0:1, :])
    proj = _mm_nt(_bf(h), win)
    bg, cg, u = proj[:, 0:cd], proj[:, cd:2 * cd], proj[:, 2 * cd:3 * cd]
    z = cg * u
    z1 = _shift_down(z, 1, halo_ref)
    z2 = _shift_down(z, 2, halo_ref)
    conv = cw_ref[0:1, :] * z2 + cw_ref[1:2, :] * z1 + cw_ref[2:3, :] * z
    heads = []
    scale = HEAD ** -0.5
    for hh in range(RET_HEADS):
        lo = hh * HEAD
        q = proj[:, 3 * cd + lo:3 * cd + lo + HEAD]
        k = proj[:, 4 * cd + lo:4 * cd + lo + HEAD]
        v = proj[:, 5 * cd + lo:5 * cd + lo + HEAD]
        gate = proj[:, 6 * cd + lo:6 * cd + lo + HEAD]
        qr = _rope(q, cos, sin)
        kr = _rope(k, cos, sin) * scale
        sc = _mm_nt(_bf(qr), _bf(kr)) * dm_ref[hh]
        qs = qr * qd_ref[:, lo:lo + HEAD]
        ks = kr * kd_ref[:, lo:lo + HEAD]
        o = _mm(_bf(sc), _bf(v)) + _mm(_bf(qs), _bf(state_of(hh)))
        on, orstd = _rms_fwd(o)
        sig = _sigmoid(gate)
        heads.append(dict(qr=qr, kr=kr, v=v, gate=gate, sc=sc, qs=qs, ks=ks, on=on, orstd=orstd, sig=sig))
    return dict(n=n, rstd=rstd, h=h, bg=bg, cg=cg, u=u, z=z, z1=z1, z2=z2, conv=conv, heads=heads)


def _even_fwd(x, modp, w448, w128, l, cw, cos, sin, consts, ts):
    s, d = x.shape
    nt = s // ts
    dm, qd, kd, tdec = consts
    cd = 4 * HEAD
    e_in = N_DEV * w448.shape[2]

    def body(x_ref, mp_ref, win_ref, cw_ref, cos_ref, sin_ref, dm_ref, qd_ref, kd_ref, wout_ref,
             x1_ref, y_ref, st_ref, zh_ref, state, halo):
        @pl.when(pl.program_id(0) == 0)
        def _():
            state[...] = jnp.zeros_like(state)
            halo[...] = jnp.zeros_like(halo)
        xv = x_ref[...]
        st_ref[...] = state[...]
        zh_ref[...] = halo[...]
        r = _even_recompute(xv, mp_ref, win_ref[...].reshape(e_in, d), cw_ref, cos_ref[...], sin_ref[...],
                            dm_ref, qd_ref, kd_ref, halo, lambda hh: state[hh])
        halo[...] = r["z"][ts - 8:ts, :]
        parts = [r["bg"] * r["conv"]]
        for hh, hd in enumerate(r["heads"]):
            state[hh] = state[hh] * tdec[hh] + _mm_tn(_bf(hd["ks"]), _bf(hd["v"]))
            rg = cw_ref[3:4, hh * HEAD:(hh + 1) * HEAD]
            parts.append((hd["gate"] * hd["sig"]) * (hd["on"] * rg))
        mcat = jnp.concatenate(parts, axis=1)
        y = _mm(_bf(mcat), wout_ref[...].reshape(d, d))
        y_ref[...] = y
        x1_ref[...] = xv + mp_ref[2:3, :] * y

    tile = pl.BlockSpec((ts, d), lambda i: (i, 0))
    rt = pl.BlockSpec((ts, HEAD), lambda i: (i, 0))
    shp = jax.ShapeDtypeStruct((s, d), F32)
    return _pcall(
        body, name="even_fwd", grid=(nt,),
        out_shape=(shp, shp, jax.ShapeDtypeStruct((nt, RET_HEADS, HEAD, HEAD), F32),
                   jax.ShapeDtypeStruct((nt, 8, cd), F32)),
        in_specs=[tile, _full(modp.shape), _wspec4(w448, l), _full(cw.shape), rt, rt,
                  _full(dm.shape), _full(qd.shape), _full(kd.shape), _wspec4(w128, l)],
        out_specs=(tile, tile, pl.BlockSpec((None, RET_HEADS, HEAD, HEAD), lambda i: (i, 0, 0, 0)),
                   pl.BlockSpec((None, 8, cd), lambda i: (i, 0, 0))),
        scratch_shapes=[pltpu.VMEM((RET_HEADS, HEAD, HEAD), F32), pltpu.VMEM((8, cd), F32)],
        compiler_params=_params(1),
    )(x, modp, w448, cw, cos, sin, dm, qd, kd, w128)


def _even_bwd(x, dx1, y, states, zhalo, modp, w448, w128, l, cw, cos, sin, consts, ts):
    s, d = x.shape
    nt = s // ts
    dm, qd, kd, tdec = consts
    cd = 4 * HEAD
    e_in = N_DEV * w448.shape[2]
    scale = HEAD ** -0.5

    def body(x_ref, dx1_ref, y_ref, st_ref, zh_ref, mp_ref, win_ref, cw_ref, cos_ref, sin_ref,
             dm_ref, qd_ref, kd_ref, wout_ref,
             dx_ref, dproj_ref, h_ref, m_ref, dy_ref, sg_ref, gstate, halo_d):
        @pl.when(pl.program_id(0) == 0)
        def _():
            gstate[...] = jnp.zeros_like(gstate)
            halo_d[...] = jnp.zeros_like(halo_d)
            sg_ref[...] = jnp.zeros_like(sg_ref)
        xv = x_ref[...]
        cos, sin = cos_ref[...], sin_ref[...]
        win = win_ref[...].reshape(e_in, d)
        r = _even_recompute(xv, mp_ref, win, cw_ref, cos, sin, dm_ref, qd_ref, kd_ref, zh_ref,
                            lambda hh: st_ref[hh])
        parts = [r["bg"] * r["conv"]]
        for hh, hd in enumerate(r["heads"]):
            rg = cw_ref[3:4, hh * HEAD:(hh + 1) * HEAD]
            parts.append((hd["gate"] * hd["sig"]) * (hd["on"] * rg))
        m_ref[...] = _bf(jnp.concatenate(parts, axis=1))
        h_ref[...] = _bf(r["h"])

        dx1 = dx1_ref[...]
        dy = mp_ref[2:3, :] * dx1
        dyb = _bf(dy)
        dy_ref[...] = dyb
        sg_ref[2:3, :] += _sum0(dx1 * y_ref[...])
        dmix = _mm_nt(dyb, wout_ref[...].reshape(d, d))

        da_out = dmix[:, 0:cd]
        dbg = da_out * r["conv"]
        dconv = da_out * r["bg"]
        dc1 = _shift_up(dconv, 1, halo_d)
        dc2 = _shift_up(dconv, 2, halo_d)
        dz = cw_ref[2:3, :] * dconv + cw_ref[1:2, :] * dc1 + cw_ref[0:1, :] * dc2
        halo_d[...] = dconv[0:8, :]
        sg_ref[4:5, 0:cd] += _sum0(dconv * r["z2"])
        sg_ref[5:6, 0:cd] += _sum0(dconv * r["z1"])
        sg_ref[6:7, 0:cd] += _sum0(dconv * r["z"])
        dcg = dz * r["u"]
        du = dz * r["cg"]

        dqs, dks, dvs, dgs = [], [], [], []
        for hh, hd in enumerate(r["heads"]):
            lo = hh * HEAD
            rg = cw_ref[3:4, lo:lo + HEAD]
            dr = dmix[:, cd + lo:cd + lo + HEAD]
            sig, gate, on = hd["sig"], hd["gate"], hd["on"]
            rn = on * rg
            dgate = dr * rn * (sig * (1.0 + gate * (1.0 - sig)))
            drn = dr * (gate * sig)
            sg_ref[7:8, lo:lo + HEAD] += _sum0(drn * on)
            do = _rms_bwd(drn * rg, on, hd["orstd"])
            dob = _bf(do)
            gst = _bf(gstate[hh])
            scb = _bf(hd["sc"])
            vb = _bf(hd["v"])
            qrb, krb = _bf(hd["qr"]), _bf(hd["kr"])
            dv = _mm_tn(scb, dob) + _mm(_bf(hd["ks"]), gst)
            dsc = _bf(_mm_nt(dob, vb) * dm_ref[hh])
            dqr = _mm(dsc, krb) + _mm_nt(dob, _bf(st_ref[hh])) * qd_ref[:, lo:lo + HEAD]
            dkr = _mm_tn(dsc, qrb) + _mm_nt(vb, gst) * kd_ref[:, lo:lo + HEAD]
            gstate[hh] = gstate[hh] * tdec[hh] + _mm_tn(_bf(hd["qs"]), dob)
            dqs.append(_rope_t(dqr, cos, sin))
            dks.append(_rope_t(dkr * scale, cos, sin))
            dvs.append(dv)
            dgs.append(dgate)

        dproj = _bf(jnp.concatenate([dbg, dcg, du] + dqs + dks + dvs + dgs, axis=1))
        dproj_ref[...] = dproj
        dh = _mm(dproj, win)
        dxn, dsh, dsc1, dg = _normmod_bwd(dh, r["n"], r["rstd"], mp_ref[6:7, :], mp_ref[1:2, :])
        dx_ref[...] = dx1 + dxn
        sg_ref[0:1, :] += dsh
        sg_ref[1:2, :] += dsc1
        sg_ref[3:4, :] += dg

    rev = lambda i: (nt - 1 - i, 0)
    tile = pl.BlockSpec((ts, d), rev)
    rt = pl.BlockSpec((ts, HEAD), rev)
    bft = jax.ShapeDtypeStruct((s, d), BF16)
    return _pcall(
        body, name="even_bwd", grid=(nt,),
        out_shape=(jax.ShapeDtypeStruct((s, d), F32), jax.ShapeDtypeStruct((s, e_in), BF16), bft, bft, bft,
                   jax.ShapeDtypeStruct((8, d), F32)),
        in_specs=[tile, tile, tile,
                  pl.BlockSpec((None, RET_HEADS, HEAD, HEAD), lambda i: (nt - 1 - i, 0, 0, 0)),
                  pl.BlockSpec((None, 8, cd), lambda i: (nt - 1 - i, 0, 0)),
                  _full(modp.shape), _wspec4(w448, l), _full(cw.shape), rt, rt,
                  _full(dm.shape), _full(qd.shape), _full(kd.shape), _wspec4(w128, l)],
        out_specs=(tile, pl.BlockSpec((ts, e_in), rev), tile, tile, tile, _full((8, d))),
        scratch_shapes=[pltpu.VMEM((RET_HEADS, HEAD, HEAD), F32), pltpu.VMEM((8, cd), F32)],
        compiler_params=_params(1),
    )(x, dx1, y, states, zhalo, modp, w448, cw, cos, sin, dm, qd, kd, w128)


def _odd_qkv_fwd(x, modp, w384, j, qkg, ts):
    s, d = x.shape
    n3 = N_DEV * w384.shape[2]

    def body(x_ref, mp_ref, w_ref, g_ref, o_ref):
        _, _, h = _normmod_fwd(x_ref[...], mp_ref[6:7, :], mp_ref[1:2, :], mp_ref[0:1, :])
        qkv = _mm_nt(_bf(h), w_ref[...].reshape(n3, d))
        for hh in range(SB_HEADS):
            lo = hh * HEAD
            qn, _ = _rms_fwd(qkv[:, lo:lo + HEAD])
            kn, _ = _rms_fwd(qkv[:, d + lo:d + lo + HEAD])
            o_ref[:, lo:lo + HEAD] = _bf(qn * g_ref[0:1, :])
            o_ref[:, d + lo:d + lo + HEAD] = _bf(kn * g_ref[1:2, :])
        o_ref[:, 2 * d:3 * d] = _bf(qkv[:, 2 * d:3 * d])

    return _pcall(
        body, name="odd_qkv_fwd", grid=(s // ts,),
        out_shape=jax.ShapeDtypeStruct((s, n3), BF16),
        in_specs=[pl.BlockSpec((ts, d), lambda i: (i, 0)), _full(modp.shape), _wspec4(w384, j),
                  _full(qkg.shape)],
        out_specs=pl.BlockSpec((ts, n3), lambda i: (i, 0)), compiler_params=_params(1),
    )(x, modp, w384, qkg)


SB_QUERIES = 512
SB_WIDE = 256


def _sb_logits(q, kw, mask):
    z = _mm_nt(q, kw) * (HEAD ** -0.5)
    e = jnp.exp(-jnp.abs(z))
    lb = jnp.minimum(z, 0.0) - jnp.log(1.0 + e)
    lk = lb - z
    if mask is not None:
        lk = jnp.where(mask, lk, 0.0)
    return lb, lk


def _tri(n, above):
    ri = lax.broadcasted_iota(jnp.int32, (n, n), 0)
    ci = lax.broadcasted_iota(jnp.int32, (n, n), 1)
    return ((ri > ci) if above else (ri < ci)).astype(BF16)


def _split_dot(a, tri):
    hi = _bf(a)
    lo = _bf(a - hi.astype(F32))
    return _mm(hi, tri) + _mm(lo, tri)


def _sb_fwd(qkv, tq):
    s = qkv.shape[0]
    d = qkv.shape[1] // 3
    nq = s // tq
    assert tq % SB_WIDE == 0
    parts = tq // SB_WIDE

    def body(q_ref, k_ref, v_ref, o_ref, t_ref, o_acc, run):
        qi = pl.program_id(1)
        base = qi * tq
        upper = _tri(SB_WIDE, True)
        o_acc[...] = jnp.zeros_like(o_acc)
        run[...] = jnp.zeros_like(run)

        def wide_step(ks, row0, masked):
            rows = slice(row0, tq)
            mask = None
            if masked:
                qpos = base + row0 + lax.broadcasted_iota(jnp.int32, (tq - row0, SB_WIDE), 0)
                mask = qpos > ks + lax.broadcasted_iota(jnp.int32, (tq - row0, SB_WIDE), 1)
            lb, lk = _sb_logits(q_ref[rows, :], k_ref[pl.ds(ks, SB_WIDE), :], mask)
            w = jnp.exp(lb + (_split_dot(lk, upper) + run[rows, :]))
            if masked:
                w = jnp.where(mask, w, 0.0)
            o_acc[rows, :] += _mm(_bf(w), v_ref[pl.ds(ks, SB_WIDE), :])
            run[rows, :] += jnp.sum(lk, axis=1, keepdims=True)

        for part in reversed(range(parts)):
            wide_step(pl.multiple_of(base + part * SB_WIDE, SB_WIDE), part * SB_WIDE, True)
        nsteps = qi * parts

        def step(it, carry):
            wide_step(pl.multiple_of((nsteps - 1 - it) * SB_WIDE, SB_WIDE), 0, False)
            return carry

        lax.fori_loop(0, nsteps, step, 0)
        o_ref[...] = _bf(o_acc[...])
        t_ref[...] = run[...]

    nh = d // HEAD
    return _pcall(
        body, name="sb_fwd", grid=(nh, nq),
        out_shape=(jax.ShapeDtypeStruct((s, d), BF16), jax.ShapeDtypeStruct((nh, s, 1), F32)),
        in_specs=[pl.BlockSpec((tq, HEAD), lambda h, i: (i, h)),
                  pl.BlockSpec((s, HEAD), lambda h, i: (0, nh + h)),
                  pl.BlockSpec((s, HEAD), lambda h, i: (0, 2 * nh + h))],
        out_specs=(pl.BlockSpec((tq, HEAD), lambda h, i: (i, h)),
                   pl.BlockSpec((None, tq, 1), lambda h, i: (h, i, 0))),
        scratch_shapes=[pltpu.VMEM((tq, HEAD), F32), pltpu.VMEM((tq, 1), F32)],
        compiler_params=_params(2),
    )(qkv, qkv, qkv)


def _sb_bwd(qkv, do, tot, tq):
    s = qkv.shape[0]
    d = qkv.shape[1] // 3
    nq = s // tq
    scale = HEAD ** -0.5
    assert tq % SB_WIDE == 0
    parts = tq // SB_WIDE

    def body(q_ref, k_ref, v_ref, do_ref, t_ref, dq_ref, dk_ref, dv_ref, pk, pd):
        qi = pl.program_id(1)

        @pl.when(qi == 0)
        def _():
            dk_ref[...] = jnp.zeros_like(dk_ref)
            dv_ref[...] = jnp.zeros_like(dv_ref)
        base = qi * tq
        upper = _tri(SB_WIDE, True)
        lower = _tri(SB_WIDE, False)
        dq_ref[...] = jnp.zeros_like(dq_ref)
        pk[...] = jnp.zeros_like(pk)
        pd[...] = jnp.zeros_like(pd)

        def wide_step(ks, row0, masked):
            rows = slice(row0, tq)
            mask = None
            if masked:
                qpos = base + row0 + lax.broadcasted_iota(jnp.int32, (tq - row0, SB_WIDE), 0)
                mask = qpos > ks + lax.broadcasted_iota(jnp.int32, (tq - row0, SB_WIDE), 1)
            kw = k_ref[pl.ds(ks, SB_WIDE), :]
            lb, lk = _sb_logits(q_ref[rows, :], kw, mask)
            pk_new = pk[rows, :] + jnp.sum(lk, axis=1, keepdims=True)
            pk[rows, :] = pk_new
            w = jnp.exp(lb + (_split_dot(lk, upper) + (t_ref[rows, :] - pk_new)))
            if masked:
                w = jnp.where(mask, w, 0.0)
            de = _mm_nt(do_ref[rows, :], v_ref[pl.ds(ks, SB_WIDE), :]) * w
            dlk = _split_dot(de, lower) + pd[rows, :]
            pd[rows, :] += jnp.sum(de, axis=1, keepdims=True)
            dz = (de - jnp.exp(lb) * (de + dlk)) * scale
            if masked:
                dz = jnp.where(mask, dz, 0.0)
            dzb = _bf(dz)
            dq_ref[rows, :] += _mm(dzb, kw)
            dv_ref[pl.ds(ks, SB_WIDE), :] += _mm_tn(_bf(w), do_ref[rows, :])
            dk_ref[pl.ds(ks, SB_WIDE), :] += _mm_tn(dzb, q_ref[rows, :])

        def step(jb, carry):
            wide_step(pl.multiple_of(jb * SB_WIDE, SB_WIDE), 0, False)
            return carry

        lax.fori_loop(0, qi * parts, step, 0)
        for part in range(parts):
            wide_step(pl.multiple_of(base + part * SB_WIDE, SB_WIDE), part * SB_WIDE, True)

    nh = d // HEAD
    shp = jax.ShapeDtypeStruct((s, d), F32)
    return _pcall(
        body, name="sb_bwd", grid=(nh, nq), out_shape=(shp, shp, shp),
        in_specs=[pl.BlockSpec((tq, HEAD), lambda h, i: (i, h)),
                  pl.BlockSpec((s, HEAD), lambda h, i: (0, nh + h)),
                  pl.BlockSpec((s, HEAD), lambda h, i: (0, 2 * nh + h)),
                  pl.BlockSpec((tq, HEAD), lambda h, i: (i, h)),
                  pl.BlockSpec((None, tq, 1), lambda h, i: (h, i, 0))],
        out_specs=(pl.BlockSpec((tq, HEAD), lambda h, i: (i, h)),
                   pl.BlockSpec((s, HEAD), lambda h, i: (0, h)),
                   pl.BlockSpec((s, HEAD), lambda h, i: (0, h))),
        scratch_shapes=[pltpu.VMEM((tq, 1), F32), pltpu.VMEM((tq, 1), F32)],
        compiler_params=_params(2),
    )(qkv, qkv, qkv, do, tot)


def _odd_out_fwd(o, x, modp, w128, slot, ts):
    s, d = x.shape

    def body(o_ref, x_ref, mp_ref, w_ref, x1_ref, y_ref):
        y = _mm(o_ref[...], w_ref[...].reshape(d, d))
        y_ref[...] = y
        x1_ref[...] = x_ref[...] + mp_ref[2:3, :] * y

    tile = pl.BlockSpec((ts, d), lambda i: (i, 0))
    shp = jax.ShapeDtypeStruct((s, d), F32)
    return _pcall(
        body, name="odd_out_fwd", grid=(s // ts,), out_shape=(shp, shp),
        in_specs=[tile, tile, _full(modp.shape), _wspec4(w128, slot)],
        out_specs=(tile, tile), compiler_params=_params(1),
    )(o, x, modp, w128)


def _odd_out_bwd(dx1, y, modp, w128, slot, ts):
    s, d = dx1.shape

    def body(dx1_ref, y_ref, mp_ref, w_ref, do_ref, dy_ref, sg_ref):
        @pl.when(pl.program_id(0) == 0)
        def _():
            sg_ref[...] = jnp.zeros_like(sg_ref)
        dx1v = dx1_ref[...]
        dyb = _bf(mp_ref[2:3, :] * dx1v)
        dy_ref[...] = dyb
        do_ref[...] = _bf(_mm_nt(dyb, w_ref[...].reshape(d, d)))
        sg_ref[2:3, :] += _sum0(dx1v * y_ref[...])

    tile = pl.BlockSpec((ts, d), lambda i: (i, 0))
    bft = jax.ShapeDtypeStruct((s, d), BF16)
    return _pcall(
        body, name="odd_out_bwd", grid=(s // ts,),
        out_shape=(bft, bft, jax.ShapeDtypeStruct((8, d), F32)),
        in_specs=[tile, tile, _full(modp.shape), _wspec4(w128, slot)],
        out_specs=(tile, tile, _full((8, d))), compiler_params=_params(1),
    )(dx1, y, modp, w128)


def _odd_qkv_bwd(x, dx1, dq, dk, dv, sg_in, modp, w384, j, qkg, ts):
    s, d = x.shape
    n3 = N_DEV * w384.shape[2]

    def body(x_ref, dx1_ref, dq_ref, dk_ref, dv_ref, sgi_ref, mp_ref, w_ref, g_ref,
             dx_ref, dqkv_ref, h_ref, sg_ref):
        @pl.when(pl.program_id(0) == 0)
        def _():
            sg_ref[...] = sgi_ref[...]
        gmix, sc1 = mp_ref[6:7, :], mp_ref[1:2, :]
        n, rstd, h = _normmod_fwd(x_ref[...], gmix, sc1, mp_ref[0:1, :])
        hb = _bf(h)
        h_ref[...] = hb
        w = w_ref[...].reshape(n3, d)
        qkv = _mm_nt(hb, w)
        parts_q, parts_k = [], []
        gq, gk = g_ref[0:1, :], g_ref[1:2, :]
        dgq = jnp.zeros((1, HEAD), F32)
        dgk = jnp.zeros((1, HEAD), F32)
        for hh in range(SB_HEADS):
            lo = hh * HEAD
            qn, qr = _rms_fwd(qkv[:, lo:lo + HEAD])
            kn, kr = _rms_fwd(qkv[:, d + lo:d + lo + HEAD])
            dqn = dq_ref[:, lo:lo + HEAD]
            dkn = dk_ref[:, lo:lo + HEAD]
            dgq = dgq + _sum0(dqn * qn)
            dgk = dgk + _sum0(dkn * kn)
            parts_q.append(_rms_bwd(dqn * gq, qn, qr))
            parts_k.append(_rms_bwd(dkn * gk, kn, kr))
        dqkv = _bf(jnp.concatenate(parts_q + parts_k + [dv_ref[...]], axis=1))
        dqkv_ref[...] = dqkv
        dh = _mm(dqkv, w)
        dxn, dsh, dsc, dg = _normmod_bwd(dh, n, rstd, gmix, sc1)
        dx_ref[...] = dx1_ref[...] + dxn
        sg_ref[0:1, :] += dsh
        sg_ref[1:2, :] += dsc
        sg_ref[3:4, :] += dg
        sg_ref[4:5, 0:HEAD] += dgq
        sg_ref[5:6, 0:HEAD] += dgk

    tile = pl.BlockSpec((ts, d), lambda i: (i, 0))
    return _pcall(
        body, name="odd_qkv_bwd", grid=(s // ts,),
        out_shape=(jax.ShapeDtypeStruct((s, d), F32), jax.ShapeDtypeStruct((s, n3), BF16),
                   jax.ShapeDtypeStruct((s, d), BF16), jax.ShapeDtypeStruct((8, d), F32)),
        in_specs=[tile, tile, tile, tile, tile, _full((8, d)), _full(modp.shape), _wspec4(w384, j),
                  _full(qkg.shape)],
        out_specs=(tile, pl.BlockSpec((ts, n3), lambda i: (i, 0)), tile, _full((8, d))),
        compiler_params=_params(1),
    )(x, dx1, dq, dk, dv, sg_in, modp, w384, qkg)


def _pad_rows(a, rows):
    return jnp.concatenate([a, jnp.zeros((rows - a.shape[0],) + a.shape[1:], a.dtype)], axis=0)


def kernel(x, c, ada_w, ada_b, norm_mix_g, norm_ffn_g, ev_w_in, ev_conv_w, ev_ret_norm_g, ev_w_out, od_w_qkv, od_q_norm_g, od_k_norm_g, od_w_out, ffn_w_gate, ffn_w_up, ffn_w_down, loss_target, m_ada_w, m_ada_b, m_norm_mix_g, m_norm_ffn_g, m_ev_w_in, m_ev_conv_w, m_ev_ret_norm_g, m_ev_w_out, m_od_w_qkv, m_od_q_norm_g, m_od_k_norm_g, m_od_w_out, m_ffn_w_gate, m_ffn_w_up, m_ffn_w_down, v_ada_w, v_ada_b, v_norm_mix_g, v_norm_ffn_g, v_ev_w_in, v_ev_conv_w, v_ev_ret_norm_g, v_ev_w_out, v_od_w_qkv, v_od_q_norm_g, v_od_k_norm_g, v_od_w_out, v_ffn_w_gate, v_ffn_w_up, v_ffn_w_down):
    me = 4 * lax.axis_index("x") + 2 * lax.axis_index("y") + lax.axis_index("c")
    xs = x[0]
    tgt = loss_target[0]
    s, d = xs.shape
    depth = ada_w.shape[0]
    n_even, n_odd = ev_w_in.shape[0], od_w_qkv.shape[0]
    ts = 256
    tq = SB_QUERIES
    cd = 4 * HEAD
    cc = ev_conv_w.shape[2]

    pack0 = jnp.zeros((8, d), F32).at[0].set(c[0]).at[1, :n_even * 3 * cc].set(ev_conv_w.reshape(-1))
    got0 = _all_gather(pack0, "gather_cond", False).reshape(N_DEV, 8, d)
    c_all = got0[:, 0, :]
    conv_all = got0[:, 1, :n_even * 3 * cc].reshape(N_DEV, n_even, 3, cc).transpose(1, 2, 0, 3)
    conv_all = conv_all.reshape(n_even, 3, N_DEV * cc)
    cols = ada_w.shape[2]
    ada_b_cols = lax.dynamic_slice(ada_b, (0, me * cols), (depth, cols))
    mod_cols = _ada_fwd(c_all, ada_w, ada_b_cols)
    got1 = _all_gather(mod_cols.reshape(depth * N_DEV, cols), "gather_mod", False)
    got1 = got1.reshape(N_DEV, depth, N_DEV, cols)
    mod = lax.dynamic_index_in_dim(got1, me, axis=2, keepdims=False)
    mod = mod.transpose(1, 0, 2).reshape(depth, 6, d)
    modps = [jnp.concatenate([mod[l], norm_mix_g[l][None], norm_ffn_g[l][None]], axis=0) for l in range(depth)]

    tr = lambda w: _bf(w.transpose(0, 2, 1))
    grp352 = jnp.concatenate([tr(ffn_w_gate), tr(ffn_w_up), _bf(ffn_w_down)], axis=0)
    grp448 = tr(ev_w_in)
    grp384 = tr(od_w_qkv)
    grp128 = jnp.concatenate([_bf(ev_w_out), _bf(od_w_out)], axis=0)

    def gather_w(grp, name):
        n_s, n_p, k = grp.shape
        return _all_gather(grp.reshape(n_s * n_p, k), name, True).reshape(N_DEV, n_s, n_p, k)

    w448 = gather_w(grp448, "gather_w448")
    w128 = gather_w(grp128, "gather_w128")
    w352 = gather_w(grp352, "gather_w352")
    w384 = gather_w(grp384, "gather_w384")

    cos, sin = _rope_tables(s)
    consts = _retention_consts(ts)
    cws = [_pad_rows(jnp.concatenate([conv_all[j], ev_ret_norm_g[j][None]], axis=0), 8) for j in range(n_even)]
    qkgs = [_pad_rows(jnp.stack([od_q_norm_g[j], od_k_norm_g[j]]), 8) for j in range(n_odd)]

    saved = []
    cur = xs
    for l in range(depth):
        j = l // 2
        if l % 2 == 0:
            x1, y, states, zhalo = _even_fwd(cur, modps[l], w448, w128, j, cws[j], cos, sin, consts, ts)
            mix = (states, zhalo)
        else:
            qkv = _odd_qkv_fwd(cur, modps[l], w384, j, qkgs[j], ts)
            o, tot = _sb_fwd(qkv, tq)
            x1, y = _odd_out_fwd(o, cur, modps[l], w128, n_even + j, ts)
            mix = (qkv, o, tot)
        x2, f = _ffn_fwd(x1, modps[l], w352, l, ts)
        saved.append((cur, x1, y, f, mix))
        cur = x2

    dx, loss_part = _loss_grad(cur, tgt, ts)
    loss = lax.psum(loss_part[0, 0], ("x", "y", "c"))

    g352 = jnp.zeros(w352.shape, BF16)
    g448 = jnp.zeros(w448.shape, BF16)
    g384 = jnp.zeros(w384.shape, BF16)
    g128 = jnp.zeros(w128.shape, BF16)
    dmod = [None] * depth
    d_gmix = [None] * depth
    d_gffn = [None] * depth
    d_conv = [None] * n_even
    d_retg = [None] * n_even
    d_qg = [None] * n_odd
    d_kg = [None] * n_odd
    f_dim = N_DEV * w352.shape[2]
    for l in reversed(range(depth)):
        j = l // 2
        x0, x1, y, f, mix = saved[l]
        dx1, dab, h2, sv, df, sg2 = _ffn_bwd(x1, f, dx, modps[l], w352, l, ts)
        g352 = _tn_matmul(dab, 0, h2, g352, l, "tn_gate")
        g352 = _tn_matmul(dab, 1, h2, g352, depth + l, "tn_up")
        g352 = _tn_matmul(sv, 0, df, g352, 2 * depth + l, "tn_down")
        if l % 2 == 0:
            states, zhalo = mix
            dx, dproj, hb, mb, dyb, sg1 = _even_bwd(x0, dx1, y, states, zhalo, modps[l], w448, w128, j,
                                                    cws[j], cos, sin, consts, ts)
            g448 = _tn_matmul(dproj, 0, hb, g448, j, "tn_ev_in")
            g128 = _tn_matmul(mb, 0, dyb, g128, j, "tn_ev_out")
            d_conv[j] = sg1[4:7, :cd]
            d_retg[j] = sg1[7, :cd]
        else:
            qkv, o, tot = mix
            do, dyb, sg0 = _odd_out_bwd(dx1, y, modps[l], w128, n_even + j, ts)
            dq, dk, dv = _sb_bwd(qkv, do, tot, tq)
            dx, dqkv, hb, sg1 = _odd_qkv_bwd(x0, dx1, dq, dk, dv, sg0, modps[l], w384, j, qkgs[j], ts)
            g384 = _tn_matmul(dqkv, 0, hb, g384, j, "tn_od_qkv")
            g128 = _tn_matmul(o, 0, dyb, g128, n_even + j, "tn_od_out")
            d_qg[j] = sg1[4, :HEAD]
            d_kg[j] = sg1[5, :HEAD]
        dmod[l] = jnp.concatenate([sg1[0:3], sg2[0:3]], axis=0).reshape(-1)
        d_gmix[l] = sg1[3]
        d_gffn[l] = sg2[3]

    small = jnp.concatenate(
        [jnp.stack(dmod).reshape(-1), jnp.stack(d_gmix).reshape(-1), jnp.stack(d_gffn).reshape(-1),
         jnp.stack(d_retg).reshape(-1), jnp.stack(d_qg).reshape(-1), jnp.stack(d_kg).reshape(-1),
         jnp.stack(d_conv).reshape(-1)])
    n_small = small.shape[0]
    rows_small = -(-n_small // (8 * 128)) * 8
    small = jnp.concatenate([small, jnp.zeros((rows_small * 128 - n_small,), F32)]).reshape(rows_small, 128)
    got2 = _all_gather(small, "gather_small", False).reshape(N_DEV, rows_small, 128)
    tot_small = _sum_small(got2).reshape(-1)
    n_mod = depth * 6 * d
    dmod_all = got2.reshape(N_DEV, -1)[:, :n_mod].reshape(N_DEV, depth, 6 * d)
    dmod_cols = lax.dynamic_slice(dmod_all, (0, 0, me * cols), (N_DEV, depth, cols)).transpose(1, 0, 2)
    g_ada_w = _ada_bwd(c_all.T, dmod_cols)

    off = [0]

    def take(shape):
        n = int(np.prod(shape))
        out = tot_small[off[0]:off[0] + n].reshape(shape)
        off[0] += n
        return out

    g_ada_b = take((depth, 6 * d))
    g_norm_mix = take((depth, d))
    g_norm_ffn = take((depth, d))
    g_ret_norm = take((n_even, cd))
    g_q_norm = take((n_odd, HEAD))
    g_k_norm = take((n_odd, HEAD))
    g_conv_full = take((n_even, 3, cd))
    g_conv = lax.dynamic_slice(g_conv_full, (0, 0, me * cc), (n_even, 3, cc))

    def exchange(buf, name):
        n_s, n_p, k = buf.shape[1:]
        recv = _all_to_all(buf.reshape(N_DEV, n_s * n_p, k), "a2a_" + name)
        return _sum_slots(recv, "sum_" + name).reshape(n_s, n_p, k)

    s352 = exchange(g352, "g352")
    s448 = exchange(g448, "g448")
    s384 = exchange(g384, "g384")
    s128 = exchange(g128, "g128")
    un = lambda a: a.transpose(0, 2, 1)
    g_gate, g_up, g_down = un(s352[0:depth]), un(s352[depth:2 * depth]), s352[2 * depth:3 * depth]
    g_ev_in, g_od_qkv = un(s448), un(s384)
    g_ev_out, g_od_out = s128[0:n_even], s128[n_even:n_even + n_odd]

    big = [("ada_w", ada_w, g_ada_w, m_ada_w, v_ada_w), ("ev_w_in", ev_w_in, g_ev_in, m_ev_w_in, v_ev_w_in),
           ("ev_w_out", ev_w_out, g_ev_out, m_ev_w_out, v_ev_w_out),
           ("od_w_qkv", od_w_qkv, g_od_qkv, m_od_w_qkv, v_od_w_qkv),
           ("od_w_out", od_w_out, g_od_out, m_od_w_out, v_od_w_out),
           ("ffn_w_gate", ffn_w_gate, g_gate, m_ffn_w_gate, v_ffn_w_gate),
           ("ffn_w_up", ffn_w_up, g_up, m_ffn_w_up, v_ffn_w_up),
           ("ffn_w_down", ffn_w_down, g_down, m_ffn_w_down, v_ffn_w_down)]
    res = {}
    for name, w, g, m, v in big:
        res[name] = (g,) + _adamw_nd(w, g, m, v, "adamw_" + name)

    smalls = [("ada_b", ada_b, g_ada_b, m_ada_b, v_ada_b), ("norm_mix_g", norm_mix_g, g_norm_mix, m_norm_mix_g, v_norm_mix_g),
              ("norm_ffn_g", norm_ffn_g, g_norm_ffn, m_norm_ffn_g, v_norm_ffn_g),
              ("ev_conv_w", ev_conv_w, g_conv, m_ev_conv_w, v_ev_conv_w),
              ("ev_ret_norm_g", ev_ret_norm_g, g_ret_norm, m_ev_ret_norm_g, v_ev_ret_norm_g),
              ("od_q_norm_g", od_q_norm_g, g_q_norm, m_od_q_norm_g, v_od_q_norm_g),
              ("od_k_norm_g", od_k_norm_g, g_k_norm, m_od_k_norm_g, v_od_k_norm_g)]

    def pack(arrs):
        flat = jnp.concatenate([a.reshape(-1) for a in arrs])
        rows = -(-flat.shape[0] // (8 * 128)) * 8
        return jnp.concatenate([flat, jnp.zeros((rows * 128 - flat.shape[0],), F32)]).reshape(rows, 128)

    sd, sm, sv_ = _adamw(pack([t[1] for t in smalls]), pack([t[2] for t in smalls]),
                         pack([t[3] for t in smalls]), pack([t[4] for t in smalls]), "adamw_small")
    sd, sm, sv_ = sd.reshape(-1), sm.reshape(-1), sv_.reshape(-1)
    pos = 0
    for name, w, g, m, v in smalls:
        n = int(np.prod(w.shape))
        res[name] = (g, sd[pos:pos + n].reshape(w.shape), sm[pos:pos + n].reshape(w.shape),
                     sv_[pos:pos + n].reshape(w.shape))
        pos += n

    order = ["ada_w", "ada_b", "norm_mix_g", "norm_ffn_g", "ev_w_in", "ev_conv_w", "ev_ret_norm_g", "ev_w_out",
             "od_w_qkv", "od_q_norm_g", "od_k_norm_g", "od_w_out", "ffn_w_gate", "ffn_w_up", "ffn_w_down"]
    outs = [loss, dx[None]]
    for k in range(4):
        outs += [res[name][k] for name in order]
    return tuple(outs)
```

```python
import functools
import math

import numpy as np
import jax
import jax.numpy as jnp
from jax import lax
from jax.experimental import pallas as pl
from jax.experimental.pallas import tpu as pltpu

F32 = jnp.float32
BF16 = jnp.bfloat16
MESH = pl.DeviceIdType.MESH

N_DEV = 8
EPS = 1e-6
CHUNK = 64
HEAD = 128
RET_HEADS = 4
SB_HEADS = 8
ROPE_THETA = 10000.0
KEY_BLOCK = 128
ADAM_LR, ADAM_B1, ADAM_B2, ADAM_EPS, ADAM_WD, ADAM_STEP = 0.001, 0.9, 0.999, 1e-08, 0.01, 10
VMEM_LIMIT = 56 * 1024 * 1024


def _pcall(body, **kw):
    return pl.pallas_call(body, **kw)


def _params(n_grid=1, vmem=VMEM_LIMIT):
    return pltpu.CompilerParams(dimension_semantics=("arbitrary",) * n_grid, vmem_limit_bytes=vmem)


def _mm(a, b):
    return jnp.dot(a, b, preferred_element_type=F32)


def _mm_nt(a, b):
    return lax.dot_general(a, b, (((1,), (1,)), ((), ())), preferred_element_type=F32)


def _mm_tn(a, b):
    return lax.dot_general(a, b, (((0,), (0,)), ((), ())), preferred_element_type=F32)


def _bf(a):
    return a.astype(BF16)


def _sigmoid(a):
    return 1.0 / (1.0 + jnp.exp(-a))


def _sum0(a):
    return jnp.sum(a, axis=0, keepdims=True)


def _full(shape):
    nd = len(shape)
    return pl.BlockSpec(shape, lambda *_: (0,) * nd)


def _normmod_fwd(x, g, sc, sh):
    rstd = lax.rsqrt(jnp.mean(x * x, axis=-1, keepdims=True) + EPS)
    n = x * rstd
    return n, rstd, (n * g) * (1.0 + sc) + sh


def _normmod_bwd(dh, n, rstd, g, sc):
    dsh = _sum0(dh)
    dsc = _sum0(dh * (n * g))
    dg = _sum0(dh * n * (1.0 + sc))
    dn = dh * (g * (1.0 + sc))
    dx = rstd * (dn - n * jnp.mean(dn * n, axis=-1, keepdims=True))
    return dx, dsh, dsc, dg


def _rms_fwd(o):
    rstd = lax.rsqrt(jnp.mean(o * o, axis=-1, keepdims=True) + EPS)
    return o * rstd, rstd


def _rms_bwd(dn, n, rstd):
    return rstd * (dn - n * jnp.mean(dn * n, axis=-1, keepdims=True))


def _all_gather(x2d, name, in_hbm):
    m_per, n = x2d.shape
    space = pltpu.HBM if in_hbm else pltpu.VMEM

    def body(x_ref, out_ref, send_sems, recv_sems, local_sem):
        x, y, c = lax.axis_index("x"), lax.axis_index("y"), lax.axis_index("c")
        me, sibling = (x, y, c), (x, y, 1 - c)
        chips = [(1 - x, y), (x, 1 - y), (1 - x, 1 - y)]

        def rows(px, py, pc):
            return out_ref.at[pl.ds((4 * px + 2 * py + pc) * m_per, m_per), :]

        def copy(k, block, to, src=None):
            return pltpu.make_async_remote_copy(
                src_ref=rows(*block) if src is None else src, dst_ref=rows(*block),
                send_sem=send_sems.at[k], recv_sem=recv_sems.at[k],
                device_id=to, device_id_type=MESH)

        mine = pltpu.make_async_copy(x_ref, rows(*me), local_sem)
        mine.start()
        first = [copy(1 + j, me, (*chip, c), src=x_ref) for j, chip in enumerate(chips)]
        first += [copy(0, me, sibling, src=x_ref)]
        for cp in first:
            cp.start()
        passed = [copy(4 + j, (*chip, c), sibling) for j, chip in enumerate(chips)]
        for j, chip in enumerate(chips):
            copy(1 + j, (*chip, c), me).wait_recv()
            passed[j].start()
        copy(0, sibling, me).wait_recv()
        for j, chip in enumerate(chips):
            copy(4 + j, (*chip, 1 - c), me).wait_recv()
        for cp in first + passed:
            cp.wait_send()
        mine.wait()

    return _pcall(
        body, name=name,
        out_shape=jax.ShapeDtypeStruct((N_DEV * m_per, n), x2d.dtype),
        in_specs=[pl.BlockSpec(memory_space=space)],
        out_specs=pl.BlockSpec(memory_space=space),
        scratch_shapes=[pltpu.SemaphoreType.DMA((7,)), pltpu.SemaphoreType.DMA((7,)),
                        pltpu.SemaphoreType.DMA],
    )(x2d)


def _all_to_all(g, name):
    _, r, n = g.shape

    def body(g_ref, out_ref, send_sems, recv_sems, local_sem):
        x, y, c = lax.axis_index("x"), lax.axis_index("y"), lax.axis_index("c")
        me = 4 * x + 2 * y + c

        def peer(k):
            px = (1 - x) if (k >> 2) & 1 else x
            py = (1 - y) if (k >> 1) & 1 else y
            pc = (1 - c) if k & 1 else c
            return px, py, pc

        def copy(k):
            px, py, pc = peer(k)
            return pltpu.make_async_remote_copy(
                src_ref=g_ref.at[4 * px + 2 * py + pc], dst_ref=out_ref.at[me],
                send_sem=send_sems.at[k - 1], recv_sem=recv_sems.at[k - 1],
                device_id=(px, py, pc), device_id_type=MESH)

        def landed(k):
            px, py, pc = peer(k)
            q = 4 * px + 2 * py + pc
            return pltpu.make_async_remote_copy(
                src_ref=g_ref.at[q], dst_ref=out_ref.at[q],
                send_sem=send_sems.at[k - 1], recv_sem=recv_sems.at[k - 1],
                device_id=(px, py, pc), device_id_type=MESH)

        mine = pltpu.make_async_copy(g_ref.at[me], out_ref.at[me], local_sem)
        mine.start()
        order = [2, 4, 6, 3, 5, 7, 1]
        sent = [copy(k) for k in order]
        for cp in sent:
            cp.start()
        for k in order:
            landed(k).wait_recv()
        for cp in sent:
            cp.wait_send()
        mine.wait()

    return _pcall(
        body, name=name,
        out_shape=jax.ShapeDtypeStruct(g.shape, g.dtype),
        in_specs=[pl.BlockSpec(memory_space=pltpu.HBM)],
        out_specs=pl.BlockSpec(memory_space=pltpu.HBM),
        scratch_shapes=[pltpu.SemaphoreType.DMA((7,)), pltpu.SemaphoreType.DMA((7,)),
                        pltpu.SemaphoreType.DMA],
    )(g)


_HBM = pl.BlockSpec(memory_space=pltpu.HBM)
_SEM = pl.BlockSpec(memory_space=pltpu.SEMAPHORE)
_EFFECT = pltpu.SideEffectType.DATAFLOW_SIDE_EFFECTING


def _exchange_copies(src_refs, land_refs, send_sems, recv_sems, scatter):
    x, y, c = lax.axis_index("x"), lax.axis_index("y"), lax.axis_index("c")
    me = 4 * x + 2 * y + c
    out = []
    for i, (s_ref, l_ref) in enumerate(zip(src_refs, land_refs)):
        for k in (2, 4, 6, 3, 5, 7, 1):
            px = (1 - x) if (k >> 2) & 1 else x
            py = (1 - y) if (k >> 1) & 1 else y
            pc = (1 - c) if k & 1 else c
            out.append(pltpu.make_async_remote_copy(
                src_ref=s_ref.at[4 * px + 2 * py + pc] if scatter else s_ref, dst_ref=l_ref.at[me],
                send_sem=send_sems.at[7 * i + k - 1], recv_sem=recv_sems.at[7 * i + k - 1],
                device_id=(px, py, pc), device_id_type=MESH))
    return out


def _exchange_start(srcs, lands, scatter, name):
    n = len(srcs)

    def body(*refs):
        for cp in _exchange_copies(refs[:n], refs[n:2 * n], refs[2 * n], refs[2 * n + 1], scatter):
            cp.start()

    arrays = list(srcs) + list(lands)
    outs = _pcall(
        body, name=name,
        out_shape=(pltpu.SemaphoreType.DMA((7 * n,)), pltpu.SemaphoreType.DMA((7 * n,)),
                   *[pltpu.HBM(a.shape, a.dtype) for a in arrays]),
        in_specs=[_HBM] * (2 * n), out_specs=(_SEM, _SEM, *[_HBM] * (2 * n)),
        input_output_aliases={i: 2 + i for i in range(2 * n)},
        compiler_params=pltpu.CompilerParams(has_side_effects=_EFFECT),
    )(*[pltpu.with_memory_space_constraint(a, pltpu.HBM) for a in arrays])
    return outs[0], outs[1], list(outs[2:2 + n]), list(outs[2 + n:])


def _exchange_wait(send_sems, recv_sems, srcs, lands, scatter, after, name):
    n = len(srcs)

    def body(*refs):
        for cp in _exchange_copies(refs[:n], refs[n:2 * n], refs[2 * n], refs[2 * n + 1], scatter):
            cp.wait_send()
            cp.wait_recv()

    arrays = list(srcs) + list(lands)
    outs = _pcall(
        body, name=name,
        out_shape=tuple(pltpu.HBM(a.shape, a.dtype) for a in arrays),
        in_specs=[_HBM] * (2 * n) + [_SEM, _SEM, pl.BlockSpec(memory_space=pl.ANY)],
        out_specs=tuple([_HBM] * (2 * n)),
        input_output_aliases={i: i for i in range(2 * n)},
        compiler_params=pltpu.CompilerParams(has_side_effects=_EFFECT),
    )(*arrays, send_sems, recv_sems, after)
    return list(outs[n:])


def _landing(src, me, scatter):
    own = lax.dynamic_index_in_dim(src, me, 0, keepdims=True) if scatter else src[None]
    shape = src.shape if scatter else (N_DEV,) + src.shape
    return lax.dynamic_update_slice(lax.empty(shape, src.dtype), own, (me, 0, 0))


def _sum_slots(recv, name):
    _, r, n = recv.shape
    tr = r
    for cand in (512, 448, 384, 352, 256, 128, 64, 32, 16, 8):
        if r % cand == 0:
            tr = cand
            break

    def body(r_ref, o_ref):
        acc = r_ref[0].astype(F32)
        for p in range(1, N_DEV):
            acc = acc + r_ref[p].astype(F32)
        o_ref[...] = acc

    return _pcall(
        body, name=name, grid=(r // tr,),
        out_shape=jax.ShapeDtypeStruct((r, n), F32),
        in_specs=[pl.BlockSpec((N_DEV, tr, n), lambda i: (0, i, 0))],
        out_specs=pl.BlockSpec((tr, n), lambda i: (i, 0)),
        compiler_params=_params(1),
    )(recv)


def _adamw(w, g, m, v, name):
    r, n = w.shape
    tr = r
    for cand in (512, 256, 128, 64, 32, 16, 8):
        if r % cand == 0:
            tr = cand
            break
    bc1 = 1.0 / (1.0 - ADAM_B1 ** ADAM_STEP)
    bc2 = 1.0 / (1.0 - ADAM_B2 ** ADAM_STEP)

    def body(w_ref, g_ref, m_ref, v_ref, d_ref, nm_ref, nv_ref):
        gv = g_ref[...]
        nm = ADAM_B1 * m_ref[...] + (1.0 - ADAM_B1) * gv
        nv = ADAM_B2 * v_ref[...] + (1.0 - ADAM_B2) * (gv * gv)
        d_ref[...] = -ADAM_LR * ((nm * bc1) / (jnp.sqrt(nv * bc2) + ADAM_EPS) + ADAM_WD * w_ref[...])
        nm_ref[...] = nm
        nv_ref[...] = nv

    spec = pl.BlockSpec((tr, n), lambda i: (i, 0))
    shp = jax.ShapeDtypeStruct((r, n), F32)
    return _pcall(
        body, name=name, grid=(r // tr,), out_shape=(shp, shp, shp),
        in_specs=[spec] * 4, out_specs=(spec, spec, spec), compiler_params=_params(1),
    )(w, g, m, v)


def _adamw_nd(w, g, m, v, name):
    shp = w.shape
    f = lambda a: a.reshape(-1, shp[-1])
    d, nm, nv = _adamw(f(w), f(g), f(m), f(v), name)
    return d.reshape(shp), nm.reshape(shp), nv.reshape(shp)


def _ada_fwd(c_all, ada_w, ada_b_cols):
    n_l, d, cols = ada_w.shape

    def body(c_ref, w_ref, b_ref, o_ref):
        cv = c_ref[...]
        ca = cv * _sigmoid(cv)
        o_ref[...] = _mm(_bf(ca), _bf(w_ref[...])) + b_ref[...]

    return _pcall(
        body, name="ada_fwd", grid=(n_l,),
        out_shape=jax.ShapeDtypeStruct((n_l, N_DEV, cols), F32),
        in_specs=[_full((N_DEV, d)), pl.BlockSpec((None, d, cols), lambda l: (l, 0, 0)),
                  pl.BlockSpec((None, 1, cols), lambda l: (l, 0, 0))],
        out_specs=pl.BlockSpec((None, N_DEV, cols), lambda l: (l, 0, 0)),
        compiler_params=_params(1),
    )(c_all, ada_w, ada_b_cols.reshape(n_l, 1, cols))


def _ada_bwd(c_all_t, dmod_cols):
    d = c_all_t.shape[0]
    n_l, _, cols = dmod_cols.shape

    def body(ct_ref, dm_ref, o_ref):
        cv = ct_ref[...]
        ca = cv * _sigmoid(cv)
        dm = dm_ref[...]
        acc = ca[:, 0:1] * dm[0:1, :]
        for b in range(1, N_DEV):
            acc = acc + ca[:, b:b + 1] * dm[b:b + 1, :]
        o_ref[...] = acc

    return _pcall(
        body, name="ada_bwd", grid=(n_l,),
        out_shape=jax.ShapeDtypeStruct((n_l, d, cols), F32),
        in_specs=[_full((d, N_DEV)), pl.BlockSpec((None, N_DEV, cols), lambda l: (l, 0, 0))],
        out_specs=pl.BlockSpec((None, d, cols), lambda l: (l, 0, 0)),
        compiler_params=_params(1),
    )(c_all_t, dmod_cols)


def _sum_small(gathered):
    _, r, n = gathered.shape

    def body(g_ref, o_ref):
        acc = g_ref[0]
        for p in range(1, N_DEV):
            acc = acc + g_ref[p]
        o_ref[...] = acc

    return _pcall(
        body, name="sum_small", out_shape=jax.ShapeDtypeStruct((r, n), F32),
        in_specs=[_full((N_DEV, r, n))], out_specs=_full((r, n)),
    )(gathered)


def _loss_grad(xf, tgt, ts):
    s, d = xf.shape

    def body(x_ref, t_ref, dx_ref, l_ref):
        @pl.when(pl.program_id(0) == 0)
        def _():
            l_ref[...] = jnp.zeros_like(l_ref)
        e = x_ref[...] - t_ref[...]
        dx_ref[...] = e * (1.0 / d)
        l_ref[...] += (0.5 / d) * jnp.sum(jnp.sum(e * e, axis=1, keepdims=True), axis=0, keepdims=True)

    spec = pl.BlockSpec((ts, d), lambda i: (i, 0))
    return _pcall(
        body, name="loss_grad", grid=(s // ts,),
        out_shape=(jax.ShapeDtypeStruct((s, d), F32), jax.ShapeDtypeStruct((1, 1), F32)),
        in_specs=[spec, spec], out_specs=(spec, _full((1, 1))), compiler_params=_params(1),
    )(xf, tgt)


def _tn_matmul(a, col_block, b, buf, slot, name):
    s = a.shape[0]
    k = b.shape[1]
    n_p = buf.shape[2]
    mcols = N_DEV * n_p
    ts = 512 if s % 512 == 0 else 256
    nt = s // ts

    def body(a_ref, b_ref, buf_ref, o_ref, acc):
        i = pl.program_id(0)

        @pl.when(i == 0)
        def _():
            acc[...] = jnp.zeros_like(acc)
        acc[...] += _mm_tn(a_ref[...], b_ref[...])

        @pl.when(i == nt - 1)
        def _():
            o_ref[...] = acc[...].reshape(N_DEV, n_p, k).astype(BF16)

    return _pcall(
        body, name=name, grid=(nt,),
        out_shape=jax.ShapeDtypeStruct(buf.shape, BF16),
        in_specs=[pl.BlockSpec((ts, mcols), lambda i: (i, col_block)),
                  pl.BlockSpec((ts, k), lambda i: (i, 0)),
                  pl.BlockSpec(memory_space=pl.ANY)],
        out_specs=pl.BlockSpec((N_DEV, None, n_p, k), lambda i: (0, slot, 0, 0)),
        scratch_shapes=[pltpu.VMEM((mcols, k), F32)],
        input_output_aliases={2: 0},
        compiler_params=_params(1),
    )(a, b, buf)


def _wspec4(w, slot):
    _, _, n_p, k = w.shape
    return pl.BlockSpec((N_DEV, None, n_p, k), lambda i: (0, slot, 0, 0), pipeline_mode=pl.Buffered(1))


def _ffn_fwd(x1, modp, w352, l, ts):
    s, d = x1.shape
    n_l = w352.shape[1] // 3
    f_dim = N_DEV * w352.shape[2]

    def body(x_ref, mp_ref, wg_ref, wu_ref, wd_ref, x2_ref, f_ref):
        x = x_ref[...]
        _, _, h2 = _normmod_fwd(x, mp_ref[7:8, :], mp_ref[4:5, :], mp_ref[3:4, :])
        hb = _bf(h2)
        f = jnp.zeros((ts, d), F32)
        half_dev, fc = N_DEV // 2, f_dim // 2
        for part in range(2):
            dev0 = part * half_dev
            a = _mm_nt(hb, wg_ref[dev0:dev0 + half_dev].reshape(fc, d))
            b = _mm_nt(hb, wu_ref[dev0:dev0 + half_dev].reshape(fc, d))
            sv = (a * _sigmoid(a)) * b
            f = f + _mm(_bf(sv), wd_ref[dev0:dev0 + half_dev].reshape(fc, d))
        f_ref[...] = f
        x2_ref[...] = x + mp_ref[5:6, :] * f

    tile = pl.BlockSpec((ts, d), lambda i: (i, 0))
    shp = jax.ShapeDtypeStruct((s, d), F32)
    return _pcall(
        body, name="ffn_fwd", grid=(s // ts,), out_shape=(shp, shp),
        in_specs=[tile, _full(modp.shape), _wspec4(w352, l), _wspec4(w352, n_l + l),
                  _wspec4(w352, 2 * n_l + l)],
        out_specs=(tile, tile), compiler_params=_params(1),
    )(x1, modp, w352, w352, w352)


def _ffn_bwd(x1, f, dx2, modp, w352, l, ts):
    s, d = x1.shape
    n_l = w352.shape[1] // 3
    f_dim = N_DEV * w352.shape[2]

    def body(x_ref, f_ref, dx2_ref, mp_ref, wg_ref, wu_ref, wd_ref,
             dx1_ref, dab_ref, h2_ref, s_ref, df_ref, sg_ref):
        @pl.when(pl.program_id(0) == 0)
        def _():
            sg_ref[...] = jnp.zeros_like(sg_ref)
        x = x_ref[...]
        gffn, sc2, g2 = mp_ref[7:8, :], mp_ref[4:5, :], mp_ref[5:6, :]
        n, rstd, h2 = _normmod_fwd(x, gffn, sc2, mp_ref[3:4, :])
        hb = _bf(h2)
        dx2 = dx2_ref[...]
        dfb = _bf(g2 * dx2)
        dh2 = jnp.zeros((ts, d), F32)
        half_dev, fc = N_DEV // 2, f_dim // 2
        for part in range(2):
            dev0, c0 = part * half_dev, part * fc
            wg = wg_ref[dev0:dev0 + half_dev].reshape(fc, d)
            wu = wu_ref[dev0:dev0 + half_dev].reshape(fc, d)
            a = _mm_nt(hb, wg)
            b = _mm_nt(hb, wu)
            sig = _sigmoid(a)
            sa = a * sig
            s_ref[:, c0:c0 + fc] = _bf(sa * b)
            ds = _mm_nt(dfb, wd_ref[dev0:dev0 + half_dev].reshape(fc, d))
            dab = _bf(ds * b * (sig * (1.0 + a * (1.0 - sig))))
            dbb = _bf(ds * sa)
            dab_ref[:, c0:c0 + fc] = dab
            dab_ref[:, f_dim + c0:f_dim + c0 + fc] = dbb
            dh2 = dh2 + _mm(dab, wg) + _mm(dbb, wu)
        dxn, dsh, dsc, dg = _normmod_bwd(dh2, n, rstd, gffn, sc2)
        dx1_ref[...] = dx2 + dxn
        h2_ref[...] = hb
        df_ref[...] = dfb
        sg_ref[0:1, :] += dsh
        sg_ref[1:2, :] += dsc
        sg_ref[2:3, :] += _sum0(dx2 * f_ref[...])
        sg_ref[3:4, :] += dg

    tile = pl.BlockSpec((ts, d), lambda i: (i, 0))
    f32t = jax.ShapeDtypeStruct((s, d), F32)
    bft = jax.ShapeDtypeStruct((s, d), BF16)
    return _pcall(
        body, name="ffn_bwd", grid=(s // ts,),
        out_shape=(f32t, jax.ShapeDtypeStruct((s, 2 * f_dim), BF16), bft,
                   jax.ShapeDtypeStruct((s, f_dim), BF16), bft, jax.ShapeDtypeStruct((8, d), F32)),
        in_specs=[tile, tile, tile, _full(modp.shape), _wspec4(w352, l), _wspec4(w352, n_l + l),
                  _wspec4(w352, 2 * n_l + l)],
        out_specs=(tile, pl.BlockSpec((ts, 2 * f_dim), lambda i: (i, 0)), tile,
                   pl.BlockSpec((ts, f_dim), lambda i: (i, 0)), tile, _full((8, d))),
        compiler_params=_params(1),
    )(x1, f, dx2, modp, w352, w352, w352)


def _retention_consts(ts):
    h = np.arange(RET_HEADS, dtype=np.float64)
    log_g = np.log1p(-np.exp2(-5.0 - h))
    t = np.arange(ts)
    diff = t[:, None] - t[None, :]
    same = (t[:, None] // CHUNK) == (t[None, :] // CHUNK)
    later = (t[:, None] // CHUNK) > (t[None, :] // CHUNK)
    dm = np.where(same, np.abs(diff), np.where(later, diff, 0))[None] * log_g[:, None, None]
    dm = np.where((same | later)[None], np.exp(dm), 0.0)
    qd = np.exp((t[:, None] + 1.0) * log_g[None, :])
    kd = np.exp((ts - 1.0 - t[:, None]) * log_g[None, :])
    qd = np.repeat(qd, HEAD, axis=1)
    kd = np.repeat(kd, HEAD, axis=1)
    tdec = [float(np.exp(ts * lg)) for lg in log_g]
    return (jnp.asarray(dm, F32), jnp.asarray(qd, F32), jnp.asarray(kd, F32), tdec)


def _rope_tables(s):
    inv_freq = 1.0 / (ROPE_THETA ** (jnp.arange(0, HEAD, 2, dtype=F32) / HEAD))
    ang = jnp.arange(s, dtype=F32)[:, None] * inv_freq[None, :]
    cos, sin = jnp.cos(ang), jnp.sin(ang)
    return jnp.concatenate([cos, cos], axis=1), jnp.concatenate([-sin, sin], axis=1)


def _rope(v, cos, sin):
    return v * cos + pltpu.roll(v, HEAD // 2, 1) * sin


def _rope_t(dv, cos, sin):
    return dv * cos + pltpu.roll(dv * sin, HEAD // 2, 1)


def _shift_down(z, k, halo_ref):
    r = pltpu.roll(z, k, 0)
    rows = lax.broadcasted_iota(jnp.int32, z.shape, 0)
    for j in range(k):
        r = jnp.where(rows == j, halo_ref[8 - k + j:8 - k + j + 1, :], r)
    return r


def _shift_up(z, k, halo_ref):
    n = z.shape[0]
    r = pltpu.roll(z, n - k, 0)
    rows = lax.broadcasted_iota(jnp.int32, z.shape, 0)
    for j in range(k):
        r = jnp.where(rows == n - k + j, halo_ref[j:j + 1, :], r)
    return r


def _even_recompute(x, mp_ref, win, cw_ref, cos, sin, dm_ref, qd_ref, kd_ref, halo_ref, state_of):
    cd = 4 * HEAD
    n, rstd, h = _normmod_fwd(x, mp_ref[6:7, :], mp_ref[1:2, :], mp_ref[0:1, :])
    proj = _mm_nt(_bf(h), win)
    bg, cg, u = proj[:, 0:cd], proj[:, cd:2 * cd], proj[:, 2 * cd:3 * cd]
    z = cg * u
    z1 = _shift_down(z, 1, halo_ref)
    z2 = _shift_down(z, 2, halo_ref)
    conv = cw_ref[0:1, :] * z2 + cw_ref[1:2, :] * z1 + cw_ref[2:3, :] * z
    heads = []
    scale = HEAD ** -0.5
    for hh in range(RET_HEADS):
        lo = hh * HEAD
        q = proj[:, 3 * cd + lo:3 * cd + lo + HEAD]
        k = proj[:, 4 * cd + lo:4 * cd + lo + HEAD]
        v = proj[:, 5 * cd + lo:5 * cd + lo + HEAD]
        gate = proj[:, 6 * cd + lo:6 * cd + lo + HEAD]
        qr = _rope(q, cos, sin)
        kr = _rope(k, cos, sin) * scale
        sc = _mm_nt(_bf(qr), _bf(kr)) * dm_ref[hh]
        qs = qr * qd_ref[:, lo:lo + HEAD]
        ks = kr * kd_ref[:, lo:lo + HEAD]
        o = _mm(_bf(sc), _bf(v)) + _mm(_bf(qs), _bf(state_of(hh)))
        on, orstd = _rms_fwd(o)
        sig = _sigmoid(gate)
        heads.append(dict(qr=qr, kr=kr, v=v, gate=gate, sc=sc, qs=qs, ks=ks, on=on, orstd=orstd, sig=sig))
    return dict(n=n, rstd=rstd, h=h, bg=bg, cg=cg, u=u, z=z, z1=z1, z2=z2, conv=conv, heads=heads)


def _even_fwd(x, modp, w448, w128, l, cw, cos, sin, consts, ts):
    s, d = x.shape
    nt = s // ts
    dm, qd, kd, tdec = consts
    cd = 4 * HEAD
    e_in = N_DEV * w448.shape[2]

    def body(x_ref, mp_ref, win_ref, cw_ref, cos_ref, sin_ref, dm_ref, qd_ref, kd_ref, wout_ref,
             x1_ref, y_ref, st_ref, zh_ref, state, halo):
        @pl.when(pl.program_id(0) == 0)
        def _():
            state[...] = jnp.zeros_like(state)
            halo[...] = jnp.zeros_like(halo)
        xv = x_ref[...]
        st_ref[...] = state[...]
        zh_ref[...] = halo[...]
        r = _even_recompute(xv, mp_ref, win_ref[...].reshape(e_in, d), cw_ref, cos_ref[...], sin_ref[...],
                            dm_ref, qd_ref, kd_ref, halo, lambda hh: state[hh])
        halo[...] = r["z"][ts - 8:ts, :]
        parts = [r["bg"] * r["conv"]]
        for hh, hd in enumerate(r["heads"]):
            state[hh] = state[hh] * tdec[hh] + _mm_tn(_bf(hd["ks"]), _bf(hd["v"]))
            rg = cw_ref[3:4, hh * HEAD:(hh + 1) * HEAD]
            parts.append((hd["gate"] * hd["sig"]) * (hd["on"] * rg))
        mcat = jnp.concatenate(parts, axis=1)
        y = _mm(_bf(mcat), wout_ref[...].reshape(d, d))
        y_ref[...] = y
        x1_ref[...] = xv + mp_ref[2:3, :] * y

    tile = pl.BlockSpec((ts, d), lambda i: (i, 0))
    rt = pl.BlockSpec((ts, HEAD), lambda i: (i, 0))
    shp = jax.ShapeDtypeStruct((s, d), F32)
    return _pcall(
        body, name="even_fwd", grid=(nt,),
        out_shape=(shp, shp, jax.ShapeDtypeStruct((nt, RET_HEADS, HEAD, HEAD), F32),
                   jax.ShapeDtypeStruct((nt, 8, cd), F32)),
        in_specs=[tile, _full(modp.shape), _wspec4(w448, l), _full(cw.shape), rt, rt,
                  _full(dm.shape), _full(qd.shape), _full(kd.shape), _wspec4(w128, l)],
        out_specs=(tile, tile, pl.BlockSpec((None, RET_HEADS, HEAD, HEAD), lambda i: (i, 0, 0, 0)),
                   pl.BlockSpec((None, 8, cd), lambda i: (i, 0, 0))),
        scratch_shapes=[pltpu.VMEM((RET_HEADS, HEAD, HEAD), F32), pltpu.VMEM((8, cd), F32)],
        compiler_params=_params(1),
    )(x, modp, w448, cw, cos, sin, dm, qd, kd, w128)


def _even_bwd(x, dx1, y, states, zhalo, modp, w448, w128, l, cw, cos, sin, consts, ts):
    s, d = x.shape
    nt = s // ts
    dm, qd, kd, tdec = consts
    cd = 4 * HEAD
    e_in = N_DEV * w448.shape[2]
    scale = HEAD ** -0.5

    def body(x_ref, dx1_ref, y_ref, st_ref, zh_ref, mp_ref, win_ref, cw_ref, cos_ref, sin_ref,
             dm_ref, qd_ref, kd_ref, wout_ref,
             dx_ref, dproj_ref, h_ref, m_ref, dy_ref, sg_ref, gstate, halo_d):
        @pl.when(pl.program_id(0) == 0)
        def _():
            gstate[...] = jnp.zeros_like(gstate)
            halo_d[...] = jnp.zeros_like(halo_d)
            sg_ref[...] = jnp.zeros_like(sg_ref)
        xv = x_ref[...]
        cos, sin = cos_ref[...], sin_ref[...]
        win = win_ref[...].reshape(e_in, d)
        r = _even_recompute(xv, mp_ref, win, cw_ref, cos, sin, dm_ref, qd_ref, kd_ref, zh_ref,
                            lambda hh: st_ref[hh])
        parts = [r["bg"] * r["conv"]]
        for hh, hd in enumerate(r["heads"]):
            rg = cw_ref[3:4, hh * HEAD:(hh + 1) * HEAD]
            parts.append((hd["gate"] * hd["sig"]) * (hd["on"] * rg))
        m_ref[...] = _bf(jnp.concatenate(parts, axis=1))
        h_ref[...] = _bf(r["h"])

        dx1 = dx1_ref[...]
        dy = mp_ref[2:3, :] * dx1
        dyb = _bf(dy)
        dy_ref[...] = dyb
        sg_ref[2:3, :] += _sum0(dx1 * y_ref[...])
        dmix = _mm_nt(dyb, wout_ref[...].reshape(d, d))

        da_out = dmix[:, 0:cd]
        dbg = da_out * r["conv"]
        dconv = da_out * r["bg"]
        dc1 = _shift_up(dconv, 1, halo_d)
        dc2 = _shift_up(dconv, 2, halo_d)
        dz = cw_ref[2:3, :] * dconv + cw_ref[1:2, :] * dc1 + cw_ref[0:1, :] * dc2
        halo_d[...] = dconv[0:8, :]
        sg_ref[4:5, 0:cd] += _sum0(dconv * r["z2"])
        sg_ref[5:6, 0:cd] += _sum0(dconv * r["z1"])
        sg_ref[6:7, 0:cd] += _sum0(dconv * r["z"])
        dcg = dz * r["u"]
        du = dz * r["cg"]

        dqs, dks, dvs, dgs = [], [], [], []
        for hh, hd in enumerate(r["heads"]):
            lo = hh * HEAD
            rg = cw_ref[3:4, lo:lo + HEAD]
            dr = dmix[:, cd + lo:cd + lo + HEAD]
            sig, gate, on = hd["sig"], hd["gate"], hd["on"]
            rn = on * rg
            dgate = dr * rn * (sig * (1.0 + gate * (1.0 - sig)))
            drn = dr * (gate * sig)
            sg_ref[7:8, lo:lo + HEAD] += _sum0(drn * on)
            do = _rms_bwd(drn * rg, on, hd["orstd"])
            dob = _bf(do)
            gst = _bf(gstate[hh])
            scb = _bf(hd["sc"])
            vb = _bf(hd["v"])
            qrb, krb = _bf(hd["qr"]), _bf(hd["kr"])
            dv = _mm_tn(scb, dob) + _mm(_bf(hd["ks"]), gst)
            dsc = _bf(_mm_nt(dob, vb) * dm_ref[hh])
            dqr = _mm(dsc, krb) + _mm_nt(dob, _bf(st_ref[hh])) * qd_ref[:, lo:lo + HEAD]
            dkr = _mm_tn(dsc, qrb) + _mm_nt(vb, gst) * kd_ref[:, lo:lo + HEAD]
            gstate[hh] = gstate[hh] * tdec[hh] + _mm_tn(_bf(hd["qs"]), dob)
            dqs.append(_rope_t(dqr, cos, sin))
            dks.append(_rope_t(dkr * scale, cos, sin))
            dvs.append(dv)
            dgs.append(dgate)

        dproj = _bf(jnp.concatenate([dbg, dcg, du] + dqs + dks + dvs + dgs, axis=1))
        dproj_ref[...] = dproj
        dh = _mm(dproj, win)
        dxn, dsh, dsc1, dg = _normmod_bwd(dh, r["n"], r["rstd"], mp_ref[6:7, :], mp_ref[1:2, :])
        dx_ref[...] = dx1 + dxn
        sg_ref[0:1, :] += dsh
        sg_ref[1:2, :] += dsc1
        sg_ref[3:4, :] += dg

    rev = lambda i: (nt - 1 - i, 0)
    tile = pl.BlockSpec((ts, d), rev)
    rt = pl.BlockSpec((ts, HEAD), rev)
    bft = jax.ShapeDtypeStruct((s, d), BF16)
    return _pcall(
        body, name="even_bwd", grid=(nt,),
        out_shape=(jax.ShapeDtypeStruct((s, d), F32), jax.ShapeDtypeStruct((s, e_in), BF16), bft, bft, bft,
                   jax.ShapeDtypeStruct((8, d), F32)),
        in_specs=[tile, tile, tile,
                  pl.BlockSpec((None, RET_HEADS, HEAD, HEAD), lambda i: (nt - 1 - i, 0, 0, 0)),
                  pl.BlockSpec((None, 8, cd), lambda i: (nt - 1 - i, 0, 0)),
                  _full(modp.shape), _wspec4(w448, l), _full(cw.shape), rt, rt,
                  _full(dm.shape), _full(qd.shape), _full(kd.shape), _wspec4(w128, l)],
        out_specs=(tile, pl.BlockSpec((ts, e_in), rev), tile, tile, tile, _full((8, d))),
        scratch_shapes=[pltpu.VMEM((RET_HEADS, HEAD, HEAD), F32), pltpu.VMEM((8, cd), F32)],
        compiler_params=_params(1),
    )(x, dx1, y, states, zhalo, modp, w448, cw, cos, sin, dm, qd, kd, w128)


def _odd_qkv_fwd(x, modp, w384, j, qkg, ts):
    s, d = x.shape
    n3 = N_DEV * w384.shape[2]

    def body(x_ref, mp_ref, w_ref, g_ref, o_ref):
        _, _, h = _normmod_fwd(x_ref[...], mp_ref[6:7, :], mp_ref[1:2, :], mp_ref[0:1, :])
        qkv = _mm_nt(_bf(h), w_ref[...].reshape(n3, d))
        for hh in range(SB_HEADS):
            lo = hh * HEAD
            qn, _ = _rms_fwd(qkv[:, lo:lo + HEAD])
            kn, _ = _rms_fwd(qkv[:, d + lo:d + lo + HEAD])
            o_ref[:, lo:lo + HEAD] = _bf(qn * g_ref[0:1, :])
            o_ref[:, d + lo:d + lo + HEAD] = _bf(kn * g_ref[1:2, :])
        o_ref[:, 2 * d:3 * d] = _bf(qkv[:, 2 * d:3 * d])

    return _pcall(
        body, name="odd_qkv_fwd", grid=(s // ts,),
        out_shape=jax.ShapeDtypeStruct((s, n3), BF16),
        in_specs=[pl.BlockSpec((ts, d), lambda i: (i, 0)), _full(modp.shape), _wspec4(w384, j),
                  _full(qkg.shape)],
        out_specs=pl.BlockSpec((ts, n3), lambda i: (i, 0)), compiler_params=_params(1),
    )(x, modp, w384, qkg)


SB_QUERIES = 512
SB_WIDE = 256


def _sb_logits(q, kw, mask):
    z = _mm_nt(q, kw) * (HEAD ** -0.5)
    e = jnp.exp(-jnp.abs(z))
    lb = jnp.minimum(z, 0.0) - jnp.log(1.0 + e)
    lk = lb - z
    if mask is not None:
        lk = jnp.where(mask, lk, 0.0)
    return lb, lk


def _tri(n, above):
    ri = lax.broadcasted_iota(jnp.int32, (n, n), 0)
    ci = lax.broadcasted_iota(jnp.int32, (n, n), 1)
    return ((ri > ci) if above else (ri < ci)).astype(BF16)


def _split_dot(a, tri):
    hi = _bf(a)
    lo = _bf(a - hi.astype(F32))
    return _mm(hi, tri) + _mm(lo, tri)


def _sb_fwd(qkv, tq):
    s = qkv.shape[0]
    d = qkv.shape[1] // 3
    nq = s // tq
    assert tq % SB_WIDE == 0
    parts = tq // SB_WIDE

    def body(q_ref, k_ref, v_ref, o_ref, t_ref, o_acc, run):
        qi = pl.program_id(1)
        base = qi * tq
        upper = _tri(SB_WIDE, True)
        o_acc[...] = jnp.zeros_like(o_acc)
        run[...] = jnp.zeros_like(run)

        def wide_step(ks, row0, masked):
            rows = slice(row0, tq)
            mask = None
            if masked:
                qpos = base + row0 + lax.broadcasted_iota(jnp.int32, (tq - row0, SB_WIDE), 0)
                mask = qpos > ks + lax.broadcasted_iota(jnp.int32, (tq - row0, SB_WIDE), 1)
            lb, lk = _sb_logits(q_ref[rows, :], k_ref[pl.ds(ks, SB_WIDE), :], mask)
            w = jnp.exp(lb + (_split_dot(lk, upper) + run[rows, :]))
            if masked:
                w = jnp.where(mask, w, 0.0)
            o_acc[rows, :] += _mm(_bf(w), v_ref[pl.ds(ks, SB_WIDE), :])
            run[rows, :] += jnp.sum(lk, axis=1, keepdims=True)

        for part in reversed(range(parts)):
            wide_step(pl.multiple_of(base + part * SB_WIDE, SB_WIDE), part * SB_WIDE, True)
        nsteps = qi * parts

        def step(it, carry):
            wide_step(pl.multiple_of((nsteps - 1 - it) * SB_WIDE, SB_WIDE), 0, False)
            return carry

        lax.fori_loop(0, nsteps, step, 0)
        o_ref[...] = _bf(o_acc[...])
        t_ref[...] = run[...]

    nh = d // HEAD
    return _pcall(
        body, name="sb_fwd", grid=(nh, nq),
        out_shape=(jax.ShapeDtypeStruct((s, d), BF16), jax.ShapeDtypeStruct((nh, s, 1), F32)),
        in_specs=[pl.BlockSpec((tq, HEAD), lambda h, i: (i, h)),
                  pl.BlockSpec((s, HEAD), lambda h, i: (0, nh + h)),
                  pl.BlockSpec((s, HEAD), lambda h, i: (0, 2 * nh + h))],
        out_specs=(pl.BlockSpec((tq, HEAD), lambda h, i: (i, h)),
                   pl.BlockSpec((None, tq, 1), lambda h, i: (h, i, 0))),
        scratch_shapes=[pltpu.VMEM((tq, HEAD), F32), pltpu.VMEM((tq, 1), F32)],
        compiler_params=_params(2),
    )(qkv, qkv, qkv)


def _sb_bwd(qkv, do, tot, tq):
    s = qkv.shape[0]
    d = qkv.shape[1] // 3
    nq = s // tq
    scale = HEAD ** -0.5
    assert tq % SB_WIDE == 0
    parts = tq // SB_WIDE

    def body(q_ref, k_ref, v_ref, do_ref, t_ref, dq_ref, dk_ref, dv_ref, pk, pd):
        qi = pl.program_id(1)

        @pl.when(qi == 0)
        def _():
            dk_ref[...] = jnp.zeros_like(dk_ref)
            dv_ref[...] = jnp.zeros_like(dv_ref)
        base = qi * tq
        upper = _tri(SB_WIDE, True)
        lower = _tri(SB_WIDE, False)
        dq_ref[...] = jnp.zeros_like(dq_ref)
        pk[...] = jnp.zeros_like(pk)
        pd[...] = jnp.zeros_like(pd)

        def wide_step(ks, row0, masked):
            rows = slice(row0, tq)
            mask = None
            if masked:
                qpos = base + row0 + lax.broadcasted_iota(jnp.int32, (tq - row0, SB_WIDE), 0)
                mask = qpos > ks + lax.broadcasted_iota(jnp.int32, (tq - row0, SB_WIDE), 1)
            kw = k_ref[pl.ds(ks, SB_WIDE), :]
            lb, lk = _sb_logits(q_ref[rows, :], kw, mask)
            pk_new = pk[rows, :] + jnp.sum(lk, axis=1, keepdims=True)
            pk[rows, :] = pk_new
            w = jnp.exp(lb + (_split_dot(lk, upper) + (t_ref[rows, :] - pk_new)))
            if masked:
                w = jnp.where(mask, w, 0.0)
            de = _mm_nt(do_ref[rows, :], v_ref[pl.ds(ks, SB_WIDE), :]) * w
            dlk = _split_dot(de, lower) + pd[rows, :]
            pd[rows, :] += jnp.sum(de, axis=1, keepdims=True)
            dz = (de - jnp.exp(lb) * (de + dlk)) * scale
            if masked:
                dz = jnp.where(mask, dz, 0.0)
            dzb = _bf(dz)
            dq_ref[rows, :] += _mm(dzb, kw)
            dv_ref[pl.ds(ks, SB_WIDE), :] += _mm_tn(_bf(w), do_ref[rows, :])
            dk_ref[pl.ds(ks, SB_WIDE), :] += _mm_tn(dzb, q_ref[rows, :])

        def step(jb, carry):
            wide_step(pl.multiple_of(jb * SB_WIDE, SB_WIDE), 0, False)
            return carry

        lax.fori_loop(0, qi * parts, step, 0)
        for part in range(parts):
            wide_step(pl.multiple_of(base + part * SB_WIDE, SB_WIDE), part * SB_WIDE, True)

    nh = d // HEAD
    shp = jax.ShapeDtypeStruct((s, d), F32)
    return _pcall(
        body, name="sb_bwd", grid=(nh, nq), out_shape=(shp, shp, shp),
        in_specs=[pl.BlockSpec((tq, HEAD), lambda h, i: (i, h)),
                  pl.BlockSpec((s, HEAD), lambda h, i: (0, nh + h)),
                  pl.BlockSpec((s, HEAD), lambda h, i: (0, 2 * nh + h)),
                  pl.BlockSpec((tq, HEAD), lambda h, i: (i, h)),
                  pl.BlockSpec((None, tq, 1), lambda h, i: (h, i, 0))],
        out_specs=(pl.BlockSpec((tq, HEAD), lambda h, i: (i, h)),
                   pl.BlockSpec((s, HEAD), lambda h, i: (0, h)),
                   pl.BlockSpec((s, HEAD), lambda h, i: (0, h))),
        scratch_shapes=[pltpu.VMEM((tq, 1), F32), pltpu.VMEM((tq, 1), F32)],
        compiler_params=_params(2),
    )(qkv, qkv, qkv, do, tot)


def _odd_out_fwd(o, x, modp, w128, slot, ts):
    s, d = x.shape

    def body(o_ref, x_ref, mp_ref, w_ref, x1_ref, y_ref):
        y = _mm(o_ref[...], w_ref[...].reshape(d, d))
        y_ref[...] = y
        x1_ref[...] = x_ref[...] + mp_ref[2:3, :] * y

    tile = pl.BlockSpec((ts, d), lambda i: (i, 0))
    shp = jax.ShapeDtypeStruct((s, d), F32)
    return _pcall(
        body, name="odd_out_fwd", grid=(s // ts,), out_shape=(shp, shp),
        in_specs=[tile, tile, _full(modp.shape), _wspec4(w128, slot)],
        out_specs=(tile, tile), compiler_params=_params(1),
    )(o, x, modp, w128)


def _odd_out_bwd(dx1, y, modp, w128, slot, ts):
    s, d = dx1.shape

    def body(dx1_ref, y_ref, mp_ref, w_ref, do_ref, dy_ref, sg_ref):
        @pl.when(pl.program_id(0) == 0)
        def _():
            sg_ref[...] = jnp.zeros_like(sg_ref)
        dx1v = dx1_ref[...]
        dyb = _bf(mp_ref[2:3, :] * dx1v)
        dy_ref[...] = dyb
        do_ref[...] = _bf(_mm_nt(dyb, w_ref[...].reshape(d, d)))
        sg_ref[2:3, :] += _sum0(dx1v * y_ref[...])

    tile = pl.BlockSpec((ts, d), lambda i: (i, 0))
    bft = jax.ShapeDtypeStruct((s, d), BF16)
    return _pcall(
        body, name="odd_out_bwd", grid=(s // ts,),
        out_shape=(bft, bft, jax.ShapeDtypeStruct((8, d), F32)),
        in_specs=[tile, tile, _full(modp.shape), _wspec4(w128, slot)],
        out_specs=(tile, tile, _full((8, d))), compiler_params=_params(1),
    )(dx1, y, modp, w128)


def _odd_qkv_bwd(x, dx1, dq, dk, dv, sg_in, modp, w384, j, qkg, ts):
    s, d = x.shape
    n3 = N_DEV * w384.shape[2]

    def body(x_ref, dx1_ref, dq_ref, dk_ref, dv_ref, sgi_ref, mp_ref, w_ref, g_ref,
             dx_ref, dqkv_ref, h_ref, sg_ref):
        @pl.when(pl.program_id(0) == 0)
        def _():
            sg_ref[...] = sgi_ref[...]
        gmix, sc1 = mp_ref[6:7, :], mp_ref[1:2, :]
        n, rstd, h = _normmod_fwd(x_ref[...], gmix, sc1, mp_ref[0:1, :])
        hb = _bf(h)
        h_ref[...] = hb
        w = w_ref[...].reshape(n3, d)
        qkv = _mm_nt(hb, w)
        parts_q, parts_k = [], []
        gq, gk = g_ref[0:1, :], g_ref[1:2, :]
        dgq = jnp.zeros((1, HEAD), F32)
        dgk = jnp.zeros((1, HEAD), F32)
        for hh in range(SB_HEADS):
            lo = hh * HEAD
            qn, qr = _rms_fwd(qkv[:, lo:lo + HEAD])
            kn, kr = _rms_fwd(qkv[:, d + lo:d + lo + HEAD])
            dqn = dq_ref[:, lo:lo + HEAD]
            dkn = dk_ref[:, lo:lo + HEAD]
            dgq = dgq + _sum0(dqn * qn)
            dgk = dgk + _sum0(dkn * kn)
            parts_q.append(_rms_bwd(dqn * gq, qn, qr))
            parts_k.append(_rms_bwd(dkn * gk, kn, kr))
        dqkv = _bf(jnp.concatenate(parts_q + parts_k + [dv_ref[...]], axis=1))
        dqkv_ref[...] = dqkv
        dh = _mm(dqkv, w)
        dxn, dsh, dsc, dg = _normmod_bwd(dh, n, rstd, gmix, sc1)
        dx_ref[...] = dx1_ref[...] + dxn
        sg_ref[0:1, :] += dsh
        sg_ref[1:2, :] += dsc
        sg_ref[3:4, :] += dg
        sg_ref[4:5, 0:HEAD] += dgq
        sg_ref[5:6, 0:HEAD] += dgk

    tile = pl.BlockSpec((ts, d), lambda i: (i, 0))
    return _pcall(
        body, name="odd_qkv_bwd", grid=(s // ts,),
        out_shape=(jax.ShapeDtypeStruct((s, d), F32), jax.ShapeDtypeStruct((s, n3), BF16),
                   jax.ShapeDtypeStruct((s, d), BF16), jax.ShapeDtypeStruct((8, d), F32)),
        in_specs=[tile, tile, tile, tile, tile, _full((8, d)), _full(modp.shape), _wspec4(w384, j),
                  _full(qkg.shape)],
        out_specs=(tile, pl.BlockSpec((ts, n3), lambda i: (i, 0)), tile, _full((8, d))),
        compiler_params=_params(1),
    )(x, dx1, dq, dk, dv, sg_in, modp, w384, qkg)


def _pad_rows(a, rows):
    return jnp.concatenate([a, jnp.zeros((rows - a.shape[0],) + a.shape[1:], a.dtype)], axis=0)


def kernel(x, c, ada_w, ada_b, norm_mix_g, norm_ffn_g, ev_w_in, ev_conv_w, ev_ret_norm_g, ev_w_out, od_w_qkv, od_q_norm_g, od_k_norm_g, od_w_out, ffn_w_gate, ffn_w_up, ffn_w_down, loss_target, m_ada_w, m_ada_b, m_norm_mix_g, m_norm_ffn_g, m_ev_w_in, m_ev_conv_w, m_ev_ret_norm_g, m_ev_w_out, m_od_w_qkv, m_od_q_norm_g, m_od_k_norm_g, m_od_w_out, m_ffn_w_gate, m_ffn_w_up, m_ffn_w_down, v_ada_w, v_ada_b, v_norm_mix_g, v_norm_ffn_g, v_ev_w_in, v_ev_conv_w, v_ev_ret_norm_g, v_ev_w_out, v_od_w_qkv, v_od_q_norm_g, v_od_k_norm_g, v_od_w_out, v_ffn_w_gate, v_ffn_w_up, v_ffn_w_down):
    me = 4 * lax.axis_index("x") + 2 * lax.axis_index("y") + lax.axis_index("c")
    xs = x[0]
    tgt = loss_target[0]
    s, d = xs.shape
    depth = ada_w.shape[0]
    n_even, n_odd = ev_w_in.shape[0], od_w_qkv.shape[0]
    ts = 256
    tq = SB_QUERIES
    cd = 4 * HEAD
    cc = ev_conv_w.shape[2]

    pack0 = jnp.zeros((8, d), F32).at[0].set(c[0]).at[1, :n_even * 3 * cc].set(ev_conv_w.reshape(-1))
    got0 = _all_gather(pack0, "gather_cond", False).reshape(N_DEV, 8, d)
    c_all = got0[:, 0, :]
    conv_all = got0[:, 1, :n_even * 3 * cc].reshape(N_DEV, n_even, 3, cc).transpose(1, 2, 0, 3)
    conv_all = conv_all.reshape(n_even, 3, N_DEV * cc)
    cols = ada_w.shape[2]
    ada_b_cols = lax.dynamic_slice(ada_b, (0, me * cols), (depth, cols))
    mod_cols = _ada_fwd(c_all, ada_w, ada_b_cols)
    got1 = _all_gather(mod_cols.reshape(depth * N_DEV, cols), "gather_mod", False)
    got1 = got1.reshape(N_DEV, depth, N_DEV, cols)
    mod = lax.dynamic_index_in_dim(got1, me, axis=2, keepdims=False)
    mod = mod.transpose(1, 0, 2).reshape(depth, 6, d)
    modps = [jnp.concatenate([mod[l], norm_mix_g[l][None], norm_ffn_g[l][None]], axis=0) for l in range(depth)]

    tr = lambda w: _bf(w.T)
    in_flight = []
    for l in range(depth):
        j = l // 2
        blocks = [tr(ev_w_in[j]), _bf(ev_w_out[j])] if l % 2 == 0 else [tr(od_w_qkv[j]), _bf(od_w_out[j])]
        blocks.append(jnp.concatenate([tr(ffn_w_gate[l]), tr(ffn_w_up[l]), _bf(ffn_w_down[l])], axis=0))
        lands = [_landing(b, me, False) for b in blocks]
        in_flight.append(_exchange_start(blocks, lands, False, f"gather_start_{l}"))
    n_ffn = ffn_w_down.shape[1]

    def layer_weights(l, after):
        got = _exchange_wait(*in_flight[l], False, after, f"gather_wait_{l}")
        w_in = got[0].reshape(N_DEV, 1, -1, d)
        w_out = got[1].reshape(N_DEV, 1, -1, d)
        return w_in, w_out, got[2].reshape(N_DEV, 3, n_ffn, d)

    cos, sin = _rope_tables(s)
    consts = _retention_consts(ts)
    cws = [_pad_rows(jnp.concatenate([conv_all[j], ev_ret_norm_g[j][None]], axis=0), 8) for j in range(n_even)]
    qkgs = [_pad_rows(jnp.stack([od_q_norm_g[j], od_k_norm_g[j]]), 8) for j in range(n_odd)]

    saved = []
    weights = []
    cur = xs
    for l in range(depth):
        j = l // 2
        w_in, w_out, w_ffn = layer_weights(l, cur)
        weights.append((w_in, w_out, w_ffn))
        if l % 2 == 0:
            x1, y, states, zhalo = _even_fwd(cur, modps[l], w_in, w_out, 0, cws[j], cos, sin, consts, ts)
            mix = (states, zhalo)
        else:
            qkv = _odd_qkv_fwd(cur, modps[l], w_in, 0, qkgs[j], ts)
            o, tot = _sb_fwd(qkv, tq)
            x1, y = _odd_out_fwd(o, cur, modps[l], w_out, 0, ts)
            mix = (qkv, o, tot)
        x2, f = _ffn_fwd(x1, modps[l], w_ffn, 0, ts)
        saved.append((cur, x1, y, f, mix))
        cur = x2

    dx, loss_part = _loss_grad(cur, tgt, ts)
    loss = lax.psum(loss_part[0, 0], ("x", "y", "c"))

    dmod = [None] * depth
    d_gmix = [None] * depth
    d_gffn = [None] * depth
    d_conv = [None] * n_even
    d_retg = [None] * n_even
    d_qg = [None] * n_odd
    d_kg = [None] * n_odd
    grads_in_flight = [None] * depth
    for l in reversed(range(depth)):
        j = l // 2
        x0, x1, y, f, mix = saved[l]
        w_in, w_out, w_ffn = weights[l]
        g_ffn = lax.empty(w_ffn.shape, BF16)
        g_in = lax.empty(w_in.shape, BF16)
        g_out = lax.empty(w_out.shape, BF16)
        dx1, dab, h2, sv, df, sg2 = _ffn_bwd(x1, f, dx, modps[l], w_ffn, 0, ts)
        g_ffn = _tn_matmul(dab, 0, h2, g_ffn, 0, "tn_gate")
        g_ffn = _tn_matmul(dab, 1, h2, g_ffn, 1, "tn_up")
        g_ffn = _tn_matmul(sv, 0, df, g_ffn, 2, "tn_down")
        if l % 2 == 0:
            states, zhalo = mix
            dx, dproj, hb, mb, dyb, sg1 = _even_bwd(x0, dx1, y, states, zhalo, modps[l], w_in, w_out, 0,
                                                    cws[j], cos, sin, consts, ts)
            g_in = _tn_matmul(dproj, 0, hb, g_in, 0, "tn_ev_in")
            g_out = _tn_matmul(mb, 0, dyb, g_out, 0, "tn_ev_out")
            d_conv[j] = sg1[4:7, :cd]
            d_retg[j] = sg1[7, :cd]
        else:
            qkv, o, tot = mix
            do, dyb, sg0 = _odd_out_bwd(dx1, y, modps[l], w_out, 0, ts)
            dq, dk, dv = _sb_bwd(qkv, do, tot, tq)
            dx, dqkv, hb, sg1 = _odd_qkv_bwd(x0, dx1, dq, dk, dv, sg0, modps[l], w_in, 0, qkgs[j], ts)
            g_in = _tn_matmul(dqkv, 0, hb, g_in, 0, "tn_od_qkv")
            g_out = _tn_matmul(o, 0, dyb, g_out, 0, "tn_od_out")
            d_qg[j] = sg1[4, :HEAD]
            d_kg[j] = sg1[5, :HEAD]
        pieces = [g.reshape(N_DEV, -1, d) for g in (g_ffn, g_in, g_out)]
        grads_in_flight[l] = _exchange_start(pieces, [_landing(p, me, True) for p in pieces], True,
                                             f"grads_start_{l}")
        dmod[l] = jnp.concatenate([sg1[0:3], sg2[0:3]], axis=0).reshape(-1)
        d_gmix[l] = sg1[3]
        d_gffn[l] = sg2[3]

    small = jnp.concatenate(
        [jnp.stack(dmod).reshape(-1), jnp.stack(d_gmix).reshape(-1), jnp.stack(d_gffn).reshape(-1),
         jnp.stack(d_retg).reshape(-1), jnp.stack(d_qg).reshape(-1), jnp.stack(d_kg).reshape(-1),
         jnp.stack(d_conv).reshape(-1)])
    n_small = small.shape[0]
    rows_small = -(-n_small // (8 * 128)) * 8
    small = jnp.concatenate([small, jnp.zeros((rows_small * 128 - n_small,), F32)]).reshape(rows_small, 128)
    got2 = _all_gather(small, "gather_small", False).reshape(N_DEV, rows_small, 128)
    tot_small = _sum_small(got2).reshape(-1)
    n_mod = depth * 6 * d
    dmod_all = got2.reshape(N_DEV, -1)[:, :n_mod].reshape(N_DEV, depth, 6 * d)
    dmod_cols = lax.dynamic_slice(dmod_all, (0, 0, me * cols), (N_DEV, depth, cols)).transpose(1, 0, 2)
    g_ada_w = _ada_bwd(c_all.T, dmod_cols)

    off = [0]

    def take(shape):
        n = int(np.prod(shape))
        out = tot_small[off[0]:off[0] + n].reshape(shape)
        off[0] += n
        return out

    g_ada_b = take((depth, 6 * d))
    g_norm_mix = take((depth, d))
    g_norm_ffn = take((depth, d))
    g_ret_norm = take((n_even, cd))
    g_q_norm = take((n_odd, HEAD))
    g_k_norm = take((n_odd, HEAD))
    g_conv_full = take((n_even, 3, cd))
    g_conv = lax.dynamic_slice(g_conv_full, (0, 0, me * cc), (n_even, 3, cc))

    sums = []
    for l in reversed(range(depth)):
        recv = _exchange_wait(*grads_in_flight[l], True, g_ada_w, f"grads_wait_{l}")
        sums.append([_sum_slots(r, f"sum_{i}") for i, r in enumerate(recv)])
    sums = sums[::-1]
    un = lambda rows: jnp.stack(rows).transpose(0, 2, 1)
    g_gate = un([sums[l][0][0:n_ffn] for l in range(depth)])
    g_up = un([sums[l][0][n_ffn:2 * n_ffn] for l in range(depth)])
    g_down = jnp.stack([sums[l][0][2 * n_ffn:3 * n_ffn] for l in range(depth)])
    g_ev_in = un([sums[l][1] for l in range(0, depth, 2)])
    g_od_qkv = un([sums[l][1] for l in range(1, depth, 2)])
    g_ev_out = jnp.stack([sums[l][2] for l in range(0, depth, 2)])
    g_od_out = jnp.stack([sums[l][2] for l in range(1, depth, 2)])

    big = [("ada_w", ada_w, g_ada_w, m_ada_w, v_ada_w), ("ev_w_in", ev_w_in, g_ev_in, m_ev_w_in, v_ev_w_in),
           ("ev_w_out", ev_w_out, g_ev_out, m_ev_w_out, v_ev_w_out),
           ("od_w_qkv", od_w_qkv, g_od_qkv, m_od_w_qkv, v_od_w_qkv),
           ("od_w_out", od_w_out, g_od_out, m_od_w_out, v_od_w_out),
           ("ffn_w_gate", ffn_w_gate, g_gate, m_ffn_w_gate, v_ffn_w_gate),
           ("ffn_w_up", ffn_w_up, g_up, m_ffn_w_up, v_ffn_w_up),
           ("ffn_w_down", ffn_w_down, g_down, m_ffn_w_down, v_ffn_w_down)]
    res = {}
    for name, w, g, m, v in big:
        res[name] = (g,) + _adamw_nd(w, g, m, v, "adamw_" + name)

    smalls = [("ada_b", ada_b, g_ada_b, m_ada_b, v_ada_b), ("norm_mix_g", norm_mix_g, g_norm_mix, m_norm_mix_g, v_norm_mix_g),
              ("norm_ffn_g", norm_ffn_g, g_norm_ffn, m_norm_ffn_g, v_norm_ffn_g),
              ("ev_conv_w", ev_conv_w, g_conv, m_ev_conv_w, v_ev_conv_w),
              ("ev_ret_norm_g", ev_ret_norm_g, g_ret_norm, m_ev_ret_norm_g, v_ev_ret_norm_g),
              ("od_q_norm_g", od_q_norm_g, g_q_norm, m_od_q_norm_g, v_od_q_norm_g),
              ("od_k_norm_g", od_k_norm_g, g_k_norm, m_od_k_norm_g, v_od_k_norm_g)]

    def pack(arrs):
        flat = jnp.concatenate([a.reshape(-1) for a in arrs])
        rows = -(-flat.shape[0] // (8 * 128)) * 8
        return jnp.concatenate([flat, jnp.zeros((rows * 128 - flat.shape[0],), F32)]).reshape(rows, 128)

    sd, sm, sv_ = _adamw(pack([t[1] for t in smalls]), pack([t[2] for t in smalls]),
                         pack([t[3] for t in smalls]), pack([t[4] for t in smalls]), "adamw_small")
    sd, sm, sv_ = sd.reshape(-1), sm.reshape(-1), sv_.reshape(-1)
    pos = 0
    for name, w, g, m, v in smalls:
        n = int(np.prod(w.shape))
        res[name] = (g, sd[pos:pos + n].reshape(w.shape), sm[pos:pos + n].reshape(w.shape),
                     sv_[pos:pos + n].reshape(w.shape))
        pos += n

    order = ["ada_w", "ada_b", "norm_mix_g", "norm_ffn_g", "ev_w_in", "ev_conv_w", "ev_ret_norm_g", "ev_w_out",
             "od_w_qkv", "od_q_norm_g", "od_k_norm_g", "od_w_out", "ffn_w_gate", "ffn_w_up", "ffn_w_down"]
    outs = [loss, dx[None]]
    for k in range(4):
        outs += [res[name][k] for name in order]
    return tuple(outs)
```

```python
import functools
import math

import numpy as np
import jax
import jax.numpy as jnp
from jax import lax
from jax.experimental import pallas as pl
from jax.experimental.pallas import tpu as pltpu

F32 = jnp.float32
BF16 = jnp.bfloat16
MESH = pl.DeviceIdType.MESH

N_DEV = 8
EPS = 1e-6
CHUNK = 64
HEAD = 128
RET_HEADS = 4
SB_HEADS = 8
ROPE_THETA = 10000.0
KEY_BLOCK = 128
ADAM_LR, ADAM_B1, ADAM_B2, ADAM_EPS, ADAM_WD, ADAM_STEP = 0.001, 0.9, 0.999, 1e-08, 0.01, 10
VMEM_LIMIT = 56 * 1024 * 1024


def _pcall(body, **kw):
    return pl.pallas_call(body, **kw)


def _params(n_grid=1, vmem=VMEM_LIMIT):
    return pltpu.CompilerParams(dimension_semantics=("arbitrary",) * n_grid, vmem_limit_bytes=vmem)


def _mm(a, b):
    return jnp.dot(a, b, preferred_element_type=F32)


def _mm_nt(a, b):
    return lax.dot_general(a, b, (((1,), (1,)), ((), ())), preferred_element_type=F32)


def _mm_tn(a, b):
    return lax.dot_general(a, b, (((0,), (0,)), ((), ())), preferred_element_type=F32)


def _bf(a):
    return a.astype(BF16)


def _sigmoid(a):
    return 1.0 / (1.0 + jnp.exp(-a))


def _sum0(a):
    return jnp.sum(a, axis=0, keepdims=True)


def _full(shape):
    nd = len(shape)
    return pl.BlockSpec(shape, lambda *_: (0,) * nd)


def _normmod_fwd(x, g, sc, sh):
    rstd = lax.rsqrt(jnp.mean(x * x, axis=-1, keepdims=True) + EPS)
    n = x * rstd
    return n, rstd, (n * g) * (1.0 + sc) + sh


def _normmod_bwd(dh, n, rstd, g, sc):
    dsh = _sum0(dh)
    dsc = _sum0(dh * (n * g))
    dg = _sum0(dh * n * (1.0 + sc))
    dn = dh * (g * (1.0 + sc))
    dx = rstd * (dn - n * jnp.mean(dn * n, axis=-1, keepdims=True))
    return dx, dsh, dsc, dg


def _rms_fwd(o):
    rstd = lax.rsqrt(jnp.mean(o * o, axis=-1, keepdims=True) + EPS)
    return o * rstd, rstd


def _rms_bwd(dn, n, rstd):
    return rstd * (dn - n * jnp.mean(dn * n, axis=-1, keepdims=True))


def _all_gather(x2d, name, in_hbm):
    m_per, n = x2d.shape
    space = pltpu.HBM if in_hbm else pltpu.VMEM

    def body(x_ref, out_ref, send_sems, recv_sems, local_sem):
        x, y, c = lax.axis_index("x"), lax.axis_index("y"), lax.axis_index("c")
        me, sibling = (x, y, c), (x, y, 1 - c)
        chips = [(1 - x, y), (x, 1 - y), (1 - x, 1 - y)]

        def rows(px, py, pc):
            return out_ref.at[pl.ds((4 * px + 2 * py + pc) * m_per, m_per), :]

        def copy(k, block, to, src=None):
            return pltpu.make_async_remote_copy(
                src_ref=rows(*block) if src is None else src, dst_ref=rows(*block),
                send_sem=send_sems.at[k], recv_sem=recv_sems.at[k],
                device_id=to, device_id_type=MESH)

        mine = pltpu.make_async_copy(x_ref, rows(*me), local_sem)
        mine.start()
        first = [copy(1 + j, me, (*chip, c), src=x_ref) for j, chip in enumerate(chips)]
        first += [copy(0, me, sibling, src=x_ref)]
        for cp in first:
            cp.start()
        passed = [copy(4 + j, (*chip, c), sibling) for j, chip in enumerate(chips)]
        for j, chip in enumerate(chips):
            copy(1 + j, (*chip, c), me).wait_recv()
            passed[j].start()
        copy(0, sibling, me).wait_recv()
        for j, chip in enumerate(chips):
            copy(4 + j, (*chip, 1 - c), me).wait_recv()
        for cp in first + passed:
            cp.wait_send()
        mine.wait()

    return _pcall(
        body, name=name,
        out_shape=jax.ShapeDtypeStruct((N_DEV * m_per, n), x2d.dtype),
        in_specs=[pl.BlockSpec(memory_space=space)],
        out_specs=pl.BlockSpec(memory_space=space),
        scratch_shapes=[pltpu.SemaphoreType.DMA((7,)), pltpu.SemaphoreType.DMA((7,)),
                        pltpu.SemaphoreType.DMA],
    )(x2d)


def _all_to_all(g, name):
    _, r, n = g.shape

    def body(g_ref, out_ref, send_sems, recv_sems, local_sem):
        x, y, c = lax.axis_index("x"), lax.axis_index("y"), lax.axis_index("c")
        me = 4 * x + 2 * y + c

        def peer(k):
            px = (1 - x) if (k >> 2) & 1 else x
            py = (1 - y) if (k >> 1) & 1 else y
            pc = (1 - c) if k & 1 else c
            return px, py, pc

        def copy(k):
            px, py, pc = peer(k)
            return pltpu.make_async_remote_copy(
                src_ref=g_ref.at[4 * px + 2 * py + pc], dst_ref=out_ref.at[me],
                send_sem=send_sems.at[k - 1], recv_sem=recv_sems.at[k - 1],
                device_id=(px, py, pc), device_id_type=MESH)

        def landed(k):
            px, py, pc = peer(k)
            q = 4 * px + 2 * py + pc
            return pltpu.make_async_remote_copy(
                src_ref=g_ref.at[q], dst_ref=out_ref.at[q],
                send_sem=send_sems.at[k - 1], recv_sem=recv_sems.at[k - 1],
                device_id=(px, py, pc), device_id_type=MESH)

        mine = pltpu.make_async_copy(g_ref.at[me], out_ref.at[me], local_sem)
        mine.start()
        order = [2, 4, 6, 3, 5, 7, 1]
        sent = [copy(k) for k in order]
        for cp in sent:
            cp.start()
        for k in order:
            landed(k).wait_recv()
        for cp in sent:
            cp.wait_send()
        mine.wait()

    return _pcall(
        body, name=name,
        out_shape=jax.ShapeDtypeStruct(g.shape, g.dtype),
        in_specs=[pl.BlockSpec(memory_space=pltpu.HBM)],
        out_specs=pl.BlockSpec(memory_space=pltpu.HBM),
        scratch_shapes=[pltpu.SemaphoreType.DMA((7,)), pltpu.SemaphoreType.DMA((7,)),
                        pltpu.SemaphoreType.DMA],
    )(g)


_HBM = pl.BlockSpec(memory_space=pltpu.HBM)
_SEM = pl.BlockSpec(memory_space=pltpu.SEMAPHORE)
_EFFECT = pltpu.SideEffectType.DATAFLOW_SIDE_EFFECTING


def _exchange_copies(src_refs, land_refs, send_sems, recv_sems, scatter):
    x, y, c = lax.axis_index("x"), lax.axis_index("y"), lax.axis_index("c")
    me = 4 * x + 2 * y + c
    out = []
    for i, (s_ref, l_ref) in enumerate(zip(src_refs, land_refs)):
        for k in (2, 4, 6, 3, 5, 7, 1):
            px = (1 - x) if (k >> 2) & 1 else x
            py = (1 - y) if (k >> 1) & 1 else y
            pc = (1 - c) if k & 1 else c
            out.append(pltpu.make_async_remote_copy(
                src_ref=s_ref.at[4 * px + 2 * py + pc] if scatter else s_ref, dst_ref=l_ref.at[me],
                send_sem=send_sems.at[7 * i + k - 1], recv_sem=recv_sems.at[7 * i + k - 1],
                device_id=(px, py, pc), device_id_type=MESH))
    return out


def _exchange_start(srcs, lands, scatter, name):
    n = len(srcs)

    def body(*refs):
        for cp in _exchange_copies(refs[:n], refs[n:2 * n], refs[2 * n], refs[2 * n + 1], scatter):
            cp.start()
        refs[-1][...] = jnp.zeros_like(refs[-1])

    arrays = list(srcs) + list(lands)
    outs = _pcall(
        body, name=name,
        out_shape=(pltpu.SemaphoreType.DMA((7 * n,)), pltpu.SemaphoreType.DMA((7 * n,)),
                   *[pltpu.HBM(a.shape, a.dtype) for a in arrays], jax.ShapeDtypeStruct((8, 128), F32)),
        in_specs=[_HBM] * (2 * n),
        out_specs=(_SEM, _SEM, *[_HBM] * (2 * n), pl.BlockSpec(memory_space=pltpu.VMEM)),
        input_output_aliases={i: 2 + i for i in range(2 * n)},
        compiler_params=pltpu.CompilerParams(has_side_effects=_EFFECT),
    )(*[pltpu.with_memory_space_constraint(a, pltpu.HBM) for a in arrays])
    return (outs[0], outs[1], list(outs[2:2 + n]), list(outs[2 + n:2 + 2 * n])), outs[-1][0, 0]


def _exchange_wait(send_sems, recv_sems, srcs, lands, scatter, after, name):
    n = len(srcs)

    def body(*refs):
        for cp in _exchange_copies(refs[:n], refs[n:2 * n], refs[2 * n], refs[2 * n + 1], scatter):
            cp.wait_send()
            cp.wait_recv()

    arrays = list(srcs) + list(lands)
    outs = _pcall(
        body, name=name,
        out_shape=tuple(pltpu.HBM(a.shape, a.dtype) for a in arrays),
        in_specs=[_HBM] * (2 * n) + [_SEM, _SEM, pl.BlockSpec(memory_space=pl.ANY)],
        out_specs=tuple([_HBM] * (2 * n)),
        input_output_aliases={i: i for i in range(2 * n)},
        compiler_params=pltpu.CompilerParams(has_side_effects=_EFFECT),
    )(*arrays, send_sems, recv_sems, after)
    return list(outs[n:])


def _landing(src, me, scatter):
    own = lax.dynamic_index_in_dim(src, me, 0, keepdims=True) if scatter else src[None]
    shape = src.shape if scatter else (N_DEV,) + src.shape
    return lax.dynamic_update_slice(lax.empty(shape, src.dtype), own, (me, 0, 0))


def _sum_slots(recv, name):
    _, r, n = recv.shape
    tr = r
    for cand in (512, 448, 384, 352, 256, 128, 64, 32, 16, 8):
        if r % cand == 0:
            tr = cand
            break

    def body(r_ref, o_ref):
        acc = r_ref[0].astype(F32)
        for p in range(1, N_DEV):
            acc = acc + r_ref[p].astype(F32)
        o_ref[...] = acc

    return _pcall(
        body, name=name, grid=(r // tr,),
        out_shape=jax.ShapeDtypeStruct((r, n), F32),
        in_specs=[pl.BlockSpec((N_DEV, tr, n), lambda i: (0, i, 0))],
        out_specs=pl.BlockSpec((tr, n), lambda i: (i, 0)),
        compiler_params=_params(1),
    )(recv)


def _adamw(w, g, m, v, name):
    r, n = w.shape
    tr = r
    for cand in (512, 256, 128, 64, 32, 16, 8):
        if r % cand == 0:
            tr = cand
            break
    bc1 = 1.0 / (1.0 - ADAM_B1 ** ADAM_STEP)
    bc2 = 1.0 / (1.0 - ADAM_B2 ** ADAM_STEP)

    def body(w_ref, g_ref, m_ref, v_ref, d_ref, nm_ref, nv_ref):
        gv = g_ref[...]
        nm = ADAM_B1 * m_ref[...] + (1.0 - ADAM_B1) * gv
        nv = ADAM_B2 * v_ref[...] + (1.0 - ADAM_B2) * (gv * gv)
        d_ref[...] = -ADAM_LR * ((nm * bc1) / (jnp.sqrt(nv * bc2) + ADAM_EPS) + ADAM_WD * w_ref[...])
        nm_ref[...] = nm
        nv_ref[...] = nv

    spec = pl.BlockSpec((tr, n), lambda i: (i, 0))
    shp = jax.ShapeDtypeStruct((r, n), F32)
    return _pcall(
        body, name=name, grid=(r // tr,), out_shape=(shp, shp, shp),
        in_specs=[spec] * 4, out_specs=(spec, spec, spec), compiler_params=_params(1),
    )(w, g, m, v)


def _adamw_nd(w, g, m, v, name):
    shp = w.shape
    f = lambda a: a.reshape(-1, shp[-1])
    d, nm, nv = _adamw(f(w), f(g), f(m), f(v), name)
    return d.reshape(shp), nm.reshape(shp), nv.reshape(shp)


def _ada_fwd(c_all, ada_w, ada_b_cols):
    n_l, d, cols = ada_w.shape

    def body(c_ref, w_ref, b_ref, o_ref):
        cv = c_ref[...]
        ca = cv * _sigmoid(cv)
        o_ref[...] = _mm(_bf(ca), _bf(w_ref[...])) + b_ref[...]

    return _pcall(
        body, name="ada_fwd", grid=(n_l,),
        out_shape=jax.ShapeDtypeStruct((n_l, N_DEV, cols), F32),
        in_specs=[_full((N_DEV, d)), pl.BlockSpec((None, d, cols), lambda l: (l, 0, 0)),
                  pl.BlockSpec((None, 1, cols), lambda l: (l, 0, 0))],
        out_specs=pl.BlockSpec((None, N_DEV, cols), lambda l: (l, 0, 0)),
        compiler_params=_params(1),
    )(c_all, ada_w, ada_b_cols.reshape(n_l, 1, cols))


def _ada_bwd(c_all_t, dmod_cols):
    d = c_all_t.shape[0]
    n_l, _, cols = dmod_cols.shape

    def body(ct_ref, dm_ref, o_ref):
        cv = ct_ref[...]
        ca = cv * _sigmoid(cv)
        dm = dm_ref[...]
        acc = ca[:, 0:1] * dm[0:1, :]
        for b in range(1, N_DEV):
            acc = acc + ca[:, b:b + 1] * dm[b:b + 1, :]
        o_ref[...] = acc

    return _pcall(
        body, name="ada_bwd", grid=(n_l,),
        out_shape=jax.ShapeDtypeStruct((n_l, d, cols), F32),
        in_specs=[_full((d, N_DEV)), pl.BlockSpec((None, N_DEV, cols), lambda l: (l, 0, 0))],
        out_specs=pl.BlockSpec((None, d, cols), lambda l: (l, 0, 0)),
        compiler_params=_params(1),
    )(c_all_t, dmod_cols)


def _sum_small(gathered):
    _, r, n = gathered.shape

    def body(g_ref, o_ref):
        acc = g_ref[0]
        for p in range(1, N_DEV):
            acc = acc + g_ref[p]
        o_ref[...] = acc

    return _pcall(
        body, name="sum_small", out_shape=jax.ShapeDtypeStruct((r, n), F32),
        in_specs=[_full((N_DEV, r, n))], out_specs=_full((r, n)),
    )(gathered)


def _loss_grad(xf, tgt, ts):
    s, d = xf.shape

    def body(x_ref, t_ref, dx_ref, l_ref):
        @pl.when(pl.program_id(0) == 0)
        def _():
            l_ref[...] = jnp.zeros_like(l_ref)
        e = x_ref[...] - t_ref[...]
        dx_ref[...] = e * (1.0 / d)
        l_ref[...] += (0.5 / d) * jnp.sum(jnp.sum(e * e, axis=1, keepdims=True), axis=0, keepdims=True)

    spec = pl.BlockSpec((ts, d), lambda i: (i, 0))
    return _pcall(
        body, name="loss_grad", grid=(s // ts,),
        out_shape=(jax.ShapeDtypeStruct((s, d), F32), jax.ShapeDtypeStruct((1, 1), F32)),
        in_specs=[spec, spec], out_specs=(spec, _full((1, 1))), compiler_params=_params(1),
    )(xf, tgt)


def _tn_matmul(a, col_block, b, buf, slot, name):
    s = a.shape[0]
    k = b.shape[1]
    n_p = buf.shape[2]
    mcols = N_DEV * n_p
    ts = 512 if s % 512 == 0 else 256
    nt = s // ts

    def body(a_ref, b_ref, buf_ref, o_ref, acc):
        i = pl.program_id(0)

        @pl.when(i == 0)
        def _():
            acc[...] = jnp.zeros_like(acc)
        acc[...] += _mm_tn(a_ref[...], b_ref[...])

        @pl.when(i == nt - 1)
        def _():
            o_ref[...] = acc[...].reshape(N_DEV, n_p, k).astype(BF16)

    return _pcall(
        body, name=name, grid=(nt,),
        out_shape=jax.ShapeDtypeStruct(buf.shape, BF16),
        in_specs=[pl.BlockSpec((ts, mcols), lambda i: (i, col_block)),
                  pl.BlockSpec((ts, k), lambda i: (i, 0)),
                  pl.BlockSpec(memory_space=pl.ANY)],
        out_specs=pl.BlockSpec((N_DEV, None, n_p, k), lambda i: (0, slot, 0, 0)),
        scratch_shapes=[pltpu.VMEM((mcols, k), F32)],
        input_output_aliases={2: 0},
        compiler_params=_params(1),
    )(a, b, buf)


def _wspec4(w, slot):
    _, _, n_p, k = w.shape
    return pl.BlockSpec((N_DEV, None, n_p, k), lambda i: (0, slot, 0, 0), pipeline_mode=pl.Buffered(1))


def _ffn_fwd(x1, modp, w352, l, ts):
    s, d = x1.shape
    n_l = w352.shape[1] // 3
    f_dim = N_DEV * w352.shape[2]

    def body(x_ref, mp_ref, wg_ref, wu_ref, wd_ref, x2_ref, f_ref):
        x = x_ref[...]
        _, _, h2 = _normmod_fwd(x, mp_ref[7:8, :], mp_ref[4:5, :], mp_ref[3:4, :])
        hb = _bf(h2)
        f = jnp.zeros((ts, d), F32)
        half_dev, fc = N_DEV // 2, f_dim // 2
        for part in range(2):
            dev0 = part * half_dev
            a = _mm_nt(hb, wg_ref[dev0:dev0 + half_dev].reshape(fc, d))
            b = _mm_nt(hb, wu_ref[dev0:dev0 + half_dev].reshape(fc, d))
            sv = (a * _sigmoid(a)) * b
            f = f + _mm(_bf(sv), wd_ref[dev0:dev0 + half_dev].reshape(fc, d))
        f_ref[...] = f
        x2_ref[...] = x + mp_ref[5:6, :] * f

    tile = pl.BlockSpec((ts, d), lambda i: (i, 0))
    shp = jax.ShapeDtypeStruct((s, d), F32)
    return _pcall(
        body, name="ffn_fwd", grid=(s // ts,), out_shape=(shp, shp),
        in_specs=[tile, _full(modp.shape), _wspec4(w352, l), _wspec4(w352, n_l + l),
                  _wspec4(w352, 2 * n_l + l)],
        out_specs=(tile, tile), compiler_params=_params(1),
    )(x1, modp, w352, w352, w352)


def _ffn_bwd(x1, f, dx2, modp, w352, l, ts):
    s, d = x1.shape
    n_l = w352.shape[1] // 3
    f_dim = N_DEV * w352.shape[2]

    def body(x_ref, f_ref, dx2_ref, mp_ref, wg_ref, wu_ref, wd_ref,
             dx1_ref, dab_ref, h2_ref, s_ref, df_ref, sg_ref):
        @pl.when(pl.program_id(0) == 0)
        def _():
            sg_ref[...] = jnp.zeros_like(sg_ref)
        x = x_ref[...]
        gffn, sc2, g2 = mp_ref[7:8, :], mp_ref[4:5, :], mp_ref[5:6, :]
        n, rstd, h2 = _normmod_fwd(x, gffn, sc2, mp_ref[3:4, :])
        hb = _bf(h2)
        dx2 = dx2_ref[...]
        dfb = _bf(g2 * dx2)
        dh2 = jnp.zeros((ts, d), F32)
        half_dev, fc = N_DEV // 2, f_dim // 2
        for part in range(2):
            dev0, c0 = part * half_dev, part * fc
            wg = wg_ref[dev0:dev0 + half_dev].reshape(fc, d)
            wu = wu_ref[dev0:dev0 + half_dev].reshape(fc, d)
            a = _mm_nt(hb, wg)
            b = _mm_nt(hb, wu)
            sig = _sigmoid(a)
            sa = a * sig
            s_ref[:, c0:c0 + fc] = _bf(sa * b)
            ds = _mm_nt(dfb, wd_ref[dev0:dev0 + half_dev].reshape(fc, d))
            dab = _bf(ds * b * (sig * (1.0 + a * (1.0 - sig))))
            dbb = _bf(ds * sa)
            dab_ref[:, c0:c0 + fc] = dab
            dab_ref[:, f_dim + c0:f_dim + c0 + fc] = dbb
            dh2 = dh2 + _mm(dab, wg) + _mm(dbb, wu)
        dxn, dsh, dsc, dg = _normmod_bwd(dh2, n, rstd, gffn, sc2)
        dx1_ref[...] = dx2 + dxn
        h2_ref[...] = hb
        df_ref[...] = dfb
        sg_ref[0:1, :] += dsh
        sg_ref[1:2, :] += dsc
        sg_ref[2:3, :] += _sum0(dx2 * f_ref[...])
        sg_ref[3:4, :] += dg

    tile = pl.BlockSpec((ts, d), lambda i: (i, 0))
    f32t = jax.ShapeDtypeStruct((s, d), F32)
    bft = jax.ShapeDtypeStruct((s, d), BF16)
    return _pcall(
        body, name="ffn_bwd", grid=(s // ts,),
        out_shape=(f32t, jax.ShapeDtypeStruct((s, 2 * f_dim), BF16), bft,
                   jax.ShapeDtypeStruct((s, f_dim), BF16), bft, jax.ShapeDtypeStruct((8, d), F32)),
        in_specs=[tile, tile, tile, _full(modp.shape), _wspec4(w352, l), _wspec4(w352, n_l + l),
                  _wspec4(w352, 2 * n_l + l)],
        out_specs=(tile, pl.BlockSpec((ts, 2 * f_dim), lambda i: (i, 0)), tile,
                   pl.BlockSpec((ts, f_dim), lambda i: (i, 0)), tile, _full((8, d))),
        compiler_params=_params(1),
    )(x1, f, dx2, modp, w352, w352, w352)


def _retention_consts(ts):
    h = np.arange(RET_HEADS, dtype=np.float64)
    log_g = np.log1p(-np.exp2(-5.0 - h))
    t = np.arange(ts)
    diff = t[:, None] - t[None, :]
    same = (t[:, None] // CHUNK) == (t[None, :] // CHUNK)
    later = (t[:, None] // CHUNK) > (t[None, :] // CHUNK)
    dm = np.where(same, np.abs(diff), np.where(later, diff, 0))[None] * log_g[:, None, None]
    dm = np.where((same | later)[None], np.exp(dm), 0.0)
    qd = np.exp((t[:, None] + 1.0) * log_g[None, :])
    kd = np.exp((ts - 1.0 - t[:, None]) * log_g[None, :])
    qd = np.repeat(qd, HEAD, axis=1)
    kd = np.repeat(kd, HEAD, axis=1)
    tdec = [float(np.exp(ts * lg)) for lg in log_g]
    return (jnp.asarray(dm, F32), jnp.asarray(qd, F32), jnp.asarray(kd, F32), tdec)


def _rope_tables(s):
    inv_freq = 1.0 / (ROPE_THETA ** (jnp.arange(0, HEAD, 2, dtype=F32) / HEAD))
    ang = jnp.arange(s, dtype=F32)[:, None] * inv_freq[None, :]
    cos, sin = jnp.cos(ang), jnp.sin(ang)
    return jnp.concatenate([cos, cos], axis=1), jnp.concatenate([-sin, sin], axis=1)


def _rope(v, cos, sin):
    return v * cos + pltpu.roll(v, HEAD // 2, 1) * sin


def _rope_t(dv, cos, sin):
    return dv * cos + pltpu.roll(dv * sin, HEAD // 2, 1)


def _shift_down(z, k, halo_ref):
    r = pltpu.roll(z, k, 0)
    rows = lax.broadcasted_iota(jnp.int32, z.shape, 0)
    for j in range(k):
        r = jnp.where(rows == j, halo_ref[8 - k + j:8 - k + j + 1, :], r)
    return r


def _shift_up(z, k, halo_ref):
    n = z.shape[0]
    r = pltpu.roll(z, n - k, 0)
    rows = lax.broadcasted_iota(jnp.int32, z.shape, 0)
    for j in range(k):
        r = jnp.where(rows == n - k + j, halo_ref[j:j + 1, :], r)
    return r


def _even_recompute(x, mp_ref, win, cw_ref, cos, sin, dm_ref, qd_ref, kd_ref, halo_ref, state_of):
    cd = 4 * HEAD
    n, rstd, h = _normmod_fwd(x, mp_ref[6:7, :], mp_ref[1:2, :], mp_ref[0:1, :])
    proj = _mm_nt(_bf(h), win)
    bg, cg, u = proj[:, 0:cd], proj[:, cd:2 * cd], proj[:, 2 * cd:3 * cd]
    z = cg * u
    z1 = _shift_down(z, 1, halo_ref)
    z2 = _shift_down(z, 2, halo_ref)
    conv = cw_ref[0:1, :] * z2 + cw_ref[1:2, :] * z1 + cw_ref[2:3, :] * z
    heads = []
    scale = HEAD ** -0.5
    for hh in range(RET_HEADS):
        lo = hh * HEAD
        q = proj[:, 3 * cd + lo:3 * cd + lo + HEAD]
        k = proj[:, 4 * cd + lo:4 * cd + lo + HEAD]
        v = proj[:, 5 * cd + lo:5 * cd + lo + HEAD]
        gate = proj[:, 6 * cd + lo:6 * cd + lo + HEAD]
        qr = _rope(q, cos, sin)
        kr = _rope(k, cos, sin) * scale
        sc = _mm_nt(_bf(qr), _bf(kr)) * dm_ref[hh]
        qs = qr * qd_ref[:, lo:lo + HEAD]
        ks = kr * kd_ref[:, lo:lo + HEAD]
        o = _mm(_bf(sc), _bf(v)) + _mm(_bf(qs), _bf(state_of(hh)))
        on, orstd = _rms_fwd(o)
        sig = _sigmoid(gate)
        heads.append(dict(qr=qr, kr=kr, v=v, gate=gate, sc=sc, qs=qs, ks=ks, on=on, orstd=orstd, sig=sig))
    return dict(n=n, rstd=rstd, h=h, bg=bg, cg=cg, u=u, z=z, z1=z1, z2=z2, conv=conv, heads=heads)


def _even_fwd(x, modp, w448, w128, l, cw, cos, sin, consts, ts):
    s, d = x.shape
    nt = s // ts
    dm, qd, kd, tdec = consts
    cd = 4 * HEAD
    e_in = N_DEV * w448.shape[2]

    def body(x_ref, mp_ref, win_ref, cw_ref, cos_ref, sin_ref, dm_ref, qd_ref, kd_ref, wout_ref,
             x1_ref, y_ref, st_ref, zh_ref, state, halo):
        @pl.when(pl.program_id(0) == 0)
        def _():
            state[...] = jnp.zeros_like(state)
            halo[...] = jnp.zeros_like(halo)
        xv = x_ref[...]
        st_ref[...] = state[...]
        zh_ref[...] = halo[...]
        r = _even_recompute(xv, mp_ref, win_ref[...].reshape(e_in, d), cw_ref, cos_ref[...], sin_ref[...],
                            dm_ref, qd_ref, kd_ref, halo, lambda hh: state[hh])
        halo[...] = r["z"][ts - 8:ts, :]
        parts = [r["bg"] * r["conv"]]
        for hh, hd in enumerate(r["heads"]):
            state[hh] = state[hh] * tdec[hh] + _mm_tn(_bf(hd["ks"]), _bf(hd["v"]))
            rg = cw_ref[3:4, hh * HEAD:(hh + 1) * HEAD]
            parts.append((hd["gate"] * hd["sig"]) * (hd["on"] * rg))
        mcat = jnp.concatenate(parts, axis=1)
        y = _mm(_bf(mcat), wout_ref[...].reshape(d, d))
        y_ref[...] = y
        x1_ref[...] = xv + mp_ref[2:3, :] * y

    tile = pl.BlockSpec((ts, d), lambda i: (i, 0))
    rt = pl.BlockSpec((ts, HEAD), lambda i: (i, 0))
    shp = jax.ShapeDtypeStruct((s, d), F32)
    return _pcall(
        body, name="even_fwd", grid=(nt,),
        out_shape=(shp, shp, jax.ShapeDtypeStruct((nt, RET_HEADS, HEAD, HEAD), F32),
                   jax.ShapeDtypeStruct((nt, 8, cd), F32)),
        in_specs=[tile, _full(modp.shape), _wspec4(w448, l), _full(cw.shape), rt, rt,
                  _full(dm.shape), _full(qd.shape), _full(kd.shape), _wspec4(w128, l)],
        out_specs=(tile, tile, pl.BlockSpec((None, RET_HEADS, HEAD, HEAD), lambda i: (i, 0, 0, 0)),
                   pl.BlockSpec((None, 8, cd), lambda i: (i, 0, 0))),
        scratch_shapes=[pltpu.VMEM((RET_HEADS, HEAD, HEAD), F32), pltpu.VMEM((8, cd), F32)],
        compiler_params=_params(1),
    )(x, modp, w448, cw, cos, sin, dm, qd, kd, w128)


def _even_bwd(x, dx1, y, states, zhalo, modp, w448, w128, l, cw, cos, sin, consts, ts):
    s, d = x.shape
    nt = s // ts
    dm, qd, kd, tdec = consts
    cd = 4 * HEAD
    e_in = N_DEV * w448.shape[2]
    scale = HEAD ** -0.5

    def body(x_ref, dx1_ref, y_ref, st_ref, zh_ref, mp_ref, win_ref, cw_ref, cos_ref, sin_ref,
             dm_ref, qd_ref, kd_ref, wout_ref,
             dx_ref, dproj_ref, h_ref, m_ref, dy_ref, sg_ref, gstate, halo_d):
        @pl.when(pl.program_id(0) == 0)
        def _():
            gstate[...] = jnp.zeros_like(gstate)
            halo_d[...] = jnp.zeros_like(halo_d)
            sg_ref[...] = jnp.zeros_like(sg_ref)
        xv = x_ref[...]
        cos, sin = cos_ref[...], sin_ref[...]
        win = win_ref[...].reshape(e_in, d)
        r = _even_recompute(xv, mp_ref, win, cw_ref, cos, sin, dm_ref, qd_ref, kd_ref, zh_ref,
                            lambda hh: st_ref[hh])
        parts = [r["bg"] * r["conv"]]
        for hh, hd in enumerate(r["heads"]):
            rg = cw_ref[3:4, hh * HEAD:(hh + 1) * HEAD]
            parts.append((hd["gate"] * hd["sig"]) * (hd["on"] * rg))
        m_ref[...] = _bf(jnp.concatenate(parts, axis=1))
        h_ref[...] = _bf(r["h"])

        dx1 = dx1_ref[...]
        dy = mp_ref[2:3, :] * dx1
        dyb = _bf(dy)
        dy_ref[...] = dyb
        sg_ref[2:3, :] += _sum0(dx1 * y_ref[...])
        dmix = _mm_nt(dyb, wout_ref[...].reshape(d, d))

        da_out = dmix[:, 0:cd]
        dbg = da_out * r["conv"]
        dconv = da_out * r["bg"]
        dc1 = _shift_up(dconv, 1, halo_d)
        dc2 = _shift_up(dconv, 2, halo_d)
        dz = cw_ref[2:3, :] * dconv + cw_ref[1:2, :] * dc1 + cw_ref[0:1, :] * dc2
        halo_d[...] = dconv[0:8, :]
        sg_ref[4:5, 0:cd] += _sum0(dconv * r["z2"])
        sg_ref[5:6, 0:cd] += _sum0(dconv * r["z1"])
        sg_ref[6:7, 0:cd] += _sum0(dconv * r["z"])
        dcg = dz * r["u"]
        du = dz * r["cg"]

        dqs, dks, dvs, dgs = [], [], [], []
        for hh, hd in enumerate(r["heads"]):
            lo = hh * HEAD
            rg = cw_ref[3:4, lo:lo + HEAD]
            dr = dmix[:, cd + lo:cd + lo + HEAD]
            sig, gate, on = hd["sig"], hd["gate"], hd["on"]
            rn = on * rg
            dgate = dr * rn * (sig * (1.0 + gate * (1.0 - sig)))
            drn = dr * (gate * sig)
            sg_ref[7:8, lo:lo + HEAD] += _sum0(drn * on)
            do = _rms_bwd(drn * rg, on, hd["orstd"])
            dob = _bf(do)
            gst = _bf(gstate[hh])
            scb = _bf(hd["sc"])
            vb = _bf(hd["v"])
            qrb, krb = _bf(hd["qr"]), _bf(hd["kr"])
            dv = _mm_tn(scb, dob) + _mm(_bf(hd["ks"]), gst)
            dsc = _bf(_mm_nt(dob, vb) * dm_ref[hh])
            dqr = _mm(dsc, krb) + _mm_nt(dob, _bf(st_ref[hh])) * qd_ref[:, lo:lo + HEAD]
            dkr = _mm_tn(dsc, qrb) + _mm_nt(vb, gst) * kd_ref[:, lo:lo + HEAD]
            gstate[hh] = gstate[hh] * tdec[hh] + _mm_tn(_bf(hd["qs"]), dob)
            dqs.append(_rope_t(dqr, cos, sin))
            dks.append(_rope_t(dkr * scale, cos, sin))
            dvs.append(dv)
            dgs.append(dgate)

        dproj = _bf(jnp.concatenate([dbg, dcg, du] + dqs + dks + dvs + dgs, axis=1))
        dproj_ref[...] = dproj
        dh = _mm(dproj, win)
        dxn, dsh, dsc1, dg = _normmod_bwd(dh, r["n"], r["rstd"], mp_ref[6:7, :], mp_ref[1:2, :])
        dx_ref[...] = dx1 + dxn
        sg_ref[0:1, :] += dsh
        sg_ref[1:2, :] += dsc1
        sg_ref[3:4, :] += dg

    rev = lambda i: (nt - 1 - i, 0)
    tile = pl.BlockSpec((ts, d), rev)
    rt = pl.BlockSpec((ts, HEAD), rev)
    bft = jax.ShapeDtypeStruct((s, d), BF16)
    return _pcall(
        body, name="even_bwd", grid=(nt,),
        out_shape=(jax.ShapeDtypeStruct((s, d), F32), jax.ShapeDtypeStruct((s, e_in), BF16), bft, bft, bft,
                   jax.ShapeDtypeStruct((8, d), F32)),
        in_specs=[tile, tile, tile,
                  pl.BlockSpec((None, RET_HEADS, HEAD, HEAD), lambda i: (nt - 1 - i, 0, 0, 0)),
                  pl.BlockSpec((None, 8, cd), lambda i: (nt - 1 - i, 0, 0)),
                  _full(modp.shape), _wspec4(w448, l), _full(cw.shape), rt, rt,
                  _full(dm.shape), _full(qd.shape), _full(kd.shape), _wspec4(w128, l)],
        out_specs=(tile, pl.BlockSpec((ts, e_in), rev), tile, tile, tile, _full((8, d))),
        scratch_shapes=[pltpu.VMEM((RET_HEADS, HEAD, HEAD), F32), pltpu.VMEM((8, cd), F32)],
        compiler_params=_params(1),
    )(x, dx1, y, states, zhalo, modp, w448, cw, cos, sin, dm, qd, kd, w128)


def _odd_qkv_fwd(x, modp, w384, j, qkg, ts):
    s, d = x.shape
    n3 = N_DEV * w384.shape[2]

    def body(x_ref, mp_ref, w_ref, g_ref, o_ref):
        _, _, h = _normmod_fwd(x_ref[...], mp_ref[6:7, :], mp_ref[1:2, :], mp_ref[0:1, :])
        qkv = _mm_nt(_bf(h), w_ref[...].reshape(n3, d))
        for hh in range(SB_HEADS):
            lo = hh * HEAD
            qn, _ = _rms_fwd(qkv[:, lo:lo + HEAD])
            kn, _ = _rms_fwd(qkv[:, d + lo:d + lo + HEAD])
            o_ref[:, lo:lo + HEAD] = _bf(qn * g_ref[0:1, :])
            o_ref[:, d + lo:d + lo + HEAD] = _bf(kn * g_ref[1:2, :])
        o_ref[:, 2 * d:3 * d] = _bf(qkv[:, 2 * d:3 * d])

    return _pcall(
        body, name="odd_qkv_fwd", grid=(s // ts,),
        out_shape=jax.ShapeDtypeStruct((s, n3), BF16),
        in_specs=[pl.BlockSpec((ts, d), lambda i: (i, 0)), _full(modp.shape), _wspec4(w384, j),
                  _full(qkg.shape)],
        out_specs=pl.BlockSpec((ts, n3), lambda i: (i, 0)), compiler_params=_params(1),
    )(x, modp, w384, qkg)


SB_QUERIES = 512
SB_WIDE = 256


def _sb_logits(q, kw, mask):
    z = _mm_nt(q, kw) * (HEAD ** -0.5)
    e = jnp.exp(-jnp.abs(z))
    lb = jnp.minimum(z, 0.0) - jnp.log(1.0 + e)
    lk = lb - z
    if mask is not None:
        lk = jnp.where(mask, lk, 0.0)
    return lb, lk


def _tri(n, above):
    ri = lax.broadcasted_iota(jnp.int32, (n, n), 0)
    ci = lax.broadcasted_iota(jnp.int32, (n, n), 1)
    return ((ri > ci) if above else (ri < ci)).astype(BF16)


def _split_dot(a, tri):
    hi = _bf(a)
    lo = _bf(a - hi.astype(F32))
    return _mm(hi, tri) + _mm(lo, tri)


def _sb_fwd(qkv, tq):
    s = qkv.shape[0]
    d = qkv.shape[1] // 3
    nq = s // tq
    assert tq % SB_WIDE == 0
    parts = tq // SB_WIDE

    def body(q_ref, k_ref, v_ref, o_ref, t_ref, o_acc, run):
        qi = pl.program_id(1)
        base = qi * tq
        upper = _tri(SB_WIDE, True)
        o_acc[...] = jnp.zeros_like(o_acc)
        run[...] = jnp.zeros_like(run)

        def wide_step(ks, row0, masked):
            rows = slice(row0, tq)
            mask = None
            if masked:
                qpos = base + row0 + lax.broadcasted_iota(jnp.int32, (tq - row0, SB_WIDE), 0)
                mask = qpos > ks + lax.broadcasted_iota(jnp.int32, (tq - row0, SB_WIDE), 1)
            lb, lk = _sb_logits(q_ref[rows, :], k_ref[pl.ds(ks, SB_WIDE), :], mask)
            w = jnp.exp(lb + (_split_dot(lk, upper) + run[rows, :]))
            if masked:
                w = jnp.where(mask, w, 0.0)
            o_acc[rows, :] += _mm(_bf(w), v_ref[pl.ds(ks, SB_WIDE), :])
            run[rows, :] += jnp.sum(lk, axis=1, keepdims=True)

        for part in reversed(range(parts)):
            wide_step(pl.multiple_of(base + part * SB_WIDE, SB_WIDE), part * SB_WIDE, True)
        nsteps = qi * parts

        def step(it, carry):
            wide_step(pl.multiple_of((nsteps - 1 - it) * SB_WIDE, SB_WIDE), 0, False)
            return carry

        lax.fori_loop(0, nsteps, step, 0)
        o_ref[...] = _bf(o_acc[...])
        t_ref[...] = run[...]

    nh = d // HEAD
    return _pcall(
        body, name="sb_fwd", grid=(nh, nq),
        out_shape=(jax.ShapeDtypeStruct((s, d), BF16), jax.ShapeDtypeStruct((nh, s, 1), F32)),
        in_specs=[pl.BlockSpec((tq, HEAD), lambda h, i: (i, h)),
                  pl.BlockSpec((s, HEAD), lambda h, i: (0, nh + h)),
                  pl.BlockSpec((s, HEAD), lambda h, i: (0, 2 * nh + h))],
        out_specs=(pl.BlockSpec((tq, HEAD), lambda h, i: (i, h)),
                   pl.BlockSpec((None, tq, 1), lambda h, i: (h, i, 0))),
        scratch_shapes=[pltpu.VMEM((tq, HEAD), F32), pltpu.VMEM((tq, 1), F32)],
        compiler_params=_params(2),
    )(qkv, qkv, qkv)


def _sb_bwd(qkv, do, tot, tq):
    s = qkv.shape[0]
    d = qkv.shape[1] // 3
    nq = s // tq
    scale = HEAD ** -0.5
    assert tq % SB_WIDE == 0
    parts = tq // SB_WIDE

    def body(q_ref, k_ref, v_ref, do_ref, t_ref, dq_ref, dk_ref, dv_ref, pk, pd):
        qi = pl.program_id(1)

        @pl.when(qi == 0)
        def _():
            dk_ref[...] = jnp.zeros_like(dk_ref)
            dv_ref[...] = jnp.zeros_like(dv_ref)
        base = qi * tq
        upper = _tri(SB_WIDE, True)
        lower = _tri(SB_WIDE, False)
        dq_ref[...] = jnp.zeros_like(dq_ref)
        pk[...] = jnp.zeros_like(pk)
        pd[...] = jnp.zeros_like(pd)

        def wide_step(ks, row0, masked):
            rows = slice(row0, tq)
            mask = None
            if masked:
                qpos = base + row0 + lax.broadcasted_iota(jnp.int32, (tq - row0, SB_WIDE), 0)
                mask = qpos > ks + lax.broadcasted_iota(jnp.int32, (tq - row0, SB_WIDE), 1)
            kw = k_ref[pl.ds(ks, SB_WIDE), :]
            lb, lk = _sb_logits(q_ref[rows, :], kw, mask)
            pk_new = pk[rows, :] + jnp.sum(lk, axis=1, keepdims=True)
            pk[rows, :] = pk_new
            w = jnp.exp(lb + (_split_dot(lk, upper) + (t_ref[rows, :] - pk_new)))
            if masked:
                w = jnp.where(mask, w, 0.0)
            de = _mm_nt(do_ref[rows, :], v_ref[pl.ds(ks, SB_WIDE), :]) * w
            dlk = _split_dot(de, lower) + pd[rows, :]
            pd[rows, :] += jnp.sum(de, axis=1, keepdims=True)
            dz = (de - jnp.exp(lb) * (de + dlk)) * scale
            if masked:
                dz = jnp.where(mask, dz, 0.0)
            dzb = _bf(dz)
            dq_ref[rows, :] += _mm(dzb, kw)
            dv_ref[pl.ds(ks, SB_WIDE), :] += _mm_tn(_bf(w), do_ref[rows, :])
            dk_ref[pl.ds(ks, SB_WIDE), :] += _mm_tn(dzb, q_ref[rows, :])

        def step(jb, carry):
            wide_step(pl.multiple_of(jb * SB_WIDE, SB_WIDE), 0, False)
            return carry

        lax.fori_loop(0, qi * parts, step, 0)
        for part in range(parts):
            wide_step(pl.multiple_of(base + part * SB_WIDE, SB_WIDE), part * SB_WIDE, True)

    nh = d // HEAD
    shp = jax.ShapeDtypeStruct((s, d), F32)
    return _pcall(
        body, name="sb_bwd", grid=(nh, nq), out_shape=(shp, shp, shp),
        in_specs=[pl.BlockSpec((tq, HEAD), lambda h, i: (i, h)),
                  pl.BlockSpec((s, HEAD), lambda h, i: (0, nh + h)),
                  pl.BlockSpec((s, HEAD), lambda h, i: (0, 2 * nh + h)),
                  pl.BlockSpec((tq, HEAD), lambda h, i: (i, h)),
                  pl.BlockSpec((None, tq, 1), lambda h, i: (h, i, 0))],
        out_specs=(pl.BlockSpec((tq, HEAD), lambda h, i: (i, h)),
                   pl.BlockSpec((s, HEAD), lambda h, i: (0, h)),
                   pl.BlockSpec((s, HEAD), lambda h, i: (0, h))),
        scratch_shapes=[pltpu.VMEM((tq, 1), F32), pltpu.VMEM((tq, 1), F32)],
        compiler_params=_params(2),
    )(qkv, qkv, qkv, do, tot)


def _odd_out_fwd(o, x, modp, w128, slot, ts):
    s, d = x.shape

    def body(o_ref, x_ref, mp_ref, w_ref, x1_ref, y_ref):
        y = _mm(o_ref[...], w_ref[...].reshape(d, d))
        y_ref[...] = y
        x1_ref[...] = x_ref[...] + mp_ref[2:3, :] * y

    tile = pl.BlockSpec((ts, d), lambda i: (i, 0))
    shp = jax.ShapeDtypeStruct((s, d), F32)
    return _pcall(
        body, name="odd_out_fwd", grid=(s // ts,), out_shape=(shp, shp),
        in_specs=[tile, tile, _full(modp.shape), _wspec4(w128, slot)],
        out_specs=(tile, tile), compiler_params=_params(1),
    )(o, x, modp, w128)


def _odd_out_bwd(dx1, y, modp, w128, slot, ts):
    s, d = dx1.shape

    def body(dx1_ref, y_ref, mp_ref, w_ref, do_ref, dy_ref, sg_ref):
        @pl.when(pl.program_id(0) == 0)
        def _():
            sg_ref[...] = jnp.zeros_like(sg_ref)
        dx1v = dx1_ref[...]
        dyb = _bf(mp_ref[2:3, :] * dx1v)
        dy_ref[...] = dyb
        do_ref[...] = _bf(_mm_nt(dyb, w_ref[...].reshape(d, d)))
        sg_ref[2:3, :] += _sum0(dx1v * y_ref[...])

    tile = pl.BlockSpec((ts, d), lambda i: (i, 0))
    bft = jax.ShapeDtypeStruct((s, d), BF16)
    return _pcall(
        body, name="odd_out_bwd", grid=(s // ts,),
        out_shape=(bft, bft, jax.ShapeDtypeStruct((8, d), F32)),
        in_specs=[tile, tile, _full(modp.shape), _wspec4(w128, slot)],
        out_specs=(tile, tile, _full((8, d))), compiler_params=_params(1),
    )(dx1, y, modp, w128)


def _odd_qkv_bwd(x, dx1, dq, dk, dv, sg_in, modp, w384, j, qkg, ts):
    s, d = x.shape
    n3 = N_DEV * w384.shape[2]

    def body(x_ref, dx1_ref, dq_ref, dk_ref, dv_ref, sgi_ref, mp_ref, w_ref, g_ref,
             dx_ref, dqkv_ref, h_ref, sg_ref):
        @pl.when(pl.program_id(0) == 0)
        def _():
            sg_ref[...] = sgi_ref[...]
        gmix, sc1 = mp_ref[6:7, :], mp_ref[1:2, :]
        n, rstd, h = _normmod_fwd(x_ref[...], gmix, sc1, mp_ref[0:1, :])
        hb = _bf(h)
        h_ref[...] = hb
        w = w_ref[...].reshape(n3, d)
        qkv = _mm_nt(hb, w)
        parts_q, parts_k = [], []
        gq, gk = g_ref[0:1, :], g_ref[1:2, :]
        dgq = jnp.zeros((1, HEAD), F32)
        dgk = jnp.zeros((1, HEAD), F32)
        for hh in range(SB_HEADS):
            lo = hh * HEAD
            qn, qr = _rms_fwd(qkv[:, lo:lo + HEAD])
            kn, kr = _rms_fwd(qkv[:, d + lo:d + lo + HEAD])
            dqn = dq_ref[:, lo:lo + HEAD]
            dkn = dk_ref[:, lo:lo + HEAD]
            dgq = dgq + _sum0(dqn * qn)
            dgk = dgk + _sum0(dkn * kn)
            parts_q.append(_rms_bwd(dqn * gq, qn, qr))
            parts_k.append(_rms_bwd(dkn * gk, kn, kr))
        dqkv = _bf(jnp.concatenate(parts_q + parts_k + [dv_ref[...]], axis=1))
        dqkv_ref[...] = dqkv
        dh = _mm(dqkv, w)
        dxn, dsh, dsc, dg = _normmod_bwd(dh, n, rstd, gmix, sc1)
        dx_ref[...] = dx1_ref[...] + dxn
        sg_ref[0:1, :] += dsh
        sg_ref[1:2, :] += dsc
        sg_ref[3:4, :] += dg
        sg_ref[4:5, 0:HEAD] += dgq
        sg_ref[5:6, 0:HEAD] += dgk

    tile = pl.BlockSpec((ts, d), lambda i: (i, 0))
    return _pcall(
        body, name="odd_qkv_bwd", grid=(s // ts,),
        out_shape=(jax.ShapeDtypeStruct((s, d), F32), jax.ShapeDtypeStruct((s, n3), BF16),
                   jax.ShapeDtypeStruct((s, d), BF16), jax.ShapeDtypeStruct((8, d), F32)),
        in_specs=[tile, tile, tile, tile, tile, _full((8, d)), _full(modp.shape), _wspec4(w384, j),
                  _full(qkg.shape)],
        out_specs=(tile, pl.BlockSpec((ts, n3), lambda i: (i, 0)), tile, _full((8, d))),
        compiler_params=_params(1),
    )(x, dx1, dq, dk, dv, sg_in, modp, w384, qkg)


def _pad_rows(a, rows):
    return jnp.concatenate([a, jnp.zeros((rows - a.shape[0],) + a.shape[1:], a.dtype)], axis=0)


def kernel(x, c, ada_w, ada_b, norm_mix_g, norm_ffn_g, ev_w_in, ev_conv_w, ev_ret_norm_g, ev_w_out, od_w_qkv, od_q_norm_g, od_k_norm_g, od_w_out, ffn_w_gate, ffn_w_up, ffn_w_down, loss_target, m_ada_w, m_ada_b, m_norm_mix_g, m_norm_ffn_g, m_ev_w_in, m_ev_conv_w, m_ev_ret_norm_g, m_ev_w_out, m_od_w_qkv, m_od_q_norm_g, m_od_k_norm_g, m_od_w_out, m_ffn_w_gate, m_ffn_w_up, m_ffn_w_down, v_ada_w, v_ada_b, v_norm_mix_g, v_norm_ffn_g, v_ev_w_in, v_ev_conv_w, v_ev_ret_norm_g, v_ev_w_out, v_od_w_qkv, v_od_q_norm_g, v_od_k_norm_g, v_od_w_out, v_ffn_w_gate, v_ffn_w_up, v_ffn_w_down):
    me = 4 * lax.axis_index("x") + 2 * lax.axis_index("y") + lax.axis_index("c")
    xs = x[0]
    tgt = loss_target[0]
    s, d = xs.shape
    depth = ada_w.shape[0]
    n_even, n_odd = ev_w_in.shape[0], od_w_qkv.shape[0]
    ts = 256
    tq = SB_QUERIES
    cd = 4 * HEAD
    cc = ev_conv_w.shape[2]

    pack0 = jnp.zeros((8, d), F32).at[0].set(c[0]).at[1, :n_even * 3 * cc].set(ev_conv_w.reshape(-1))
    got0 = _all_gather(pack0, "gather_cond", False).reshape(N_DEV, 8, d)
    c_all = got0[:, 0, :]
    conv_all = got0[:, 1, :n_even * 3 * cc].reshape(N_DEV, n_even, 3, cc).transpose(1, 2, 0, 3)
    conv_all = conv_all.reshape(n_even, 3, N_DEV * cc)
    cols = ada_w.shape[2]
    ada_b_cols = lax.dynamic_slice(ada_b, (0, me * cols), (depth, cols))
    mod_cols = _ada_fwd(c_all, ada_w, ada_b_cols)
    got1 = _all_gather(mod_cols.reshape(depth * N_DEV, cols), "gather_mod", False)
    got1 = got1.reshape(N_DEV, depth, N_DEV, cols)
    mod = lax.dynamic_index_in_dim(got1, me, axis=2, keepdims=False)
    mod = mod.transpose(1, 0, 2).reshape(depth, 6, d)
    modps = [jnp.concatenate([mod[l], norm_mix_g[l][None], norm_ffn_g[l][None]], axis=0) for l in range(depth)]

    in_flight = []
    started = jnp.zeros((), F32)
    for l in range(depth):
        j = l // 2
        plain = lambda w: _bf(w + started)
        tr = lambda w: plain(w).T
        blocks = [tr(ev_w_in[j]), plain(ev_w_out[j])] if l % 2 == 0 else [tr(od_w_qkv[j]), plain(od_w_out[j])]
        blocks.append(jnp.concatenate([tr(ffn_w_gate[l]), tr(ffn_w_up[l]), plain(ffn_w_down[l])], axis=0))
        lands = [_landing(b, me, False) for b in blocks]
        flight, started = _exchange_start(blocks, lands, False, f"gather_start_{l}")
        in_flight.append(flight)
        modps[0] = modps[0] + started
    n_ffn = ffn_w_down.shape[1]

    def layer_weights(l, after):
        got = _exchange_wait(*in_flight[l], False, after, f"gather_wait_{l}")
        w_in = got[0].reshape(N_DEV, 1, -1, d)
        w_out = got[1].reshape(N_DEV, 1, -1, d)
        return w_in, w_out, got[2].reshape(N_DEV, 3, n_ffn, d)

    cos, sin = _rope_tables(s)
    consts = _retention_consts(ts)
    cws = [_pad_rows(jnp.concatenate([conv_all[j], ev_ret_norm_g[j][None]], axis=0), 8) for j in range(n_even)]
    qkgs = [_pad_rows(jnp.stack([od_q_norm_g[j], od_k_norm_g[j]]), 8) for j in range(n_odd)]

    saved = []
    weights = []
    cur = xs
    for l in range(depth):
        j = l // 2
        w_in, w_out, w_ffn = layer_weights(l, cur)
        weights.append((w_in, w_out, w_ffn))
        if l % 2 == 0:
            x1, y, states, zhalo = _even_fwd(cur, modps[l], w_in, w_out, 0, cws[j], cos, sin, consts, ts)
            mix = (states, zhalo)
        else:
            qkv = _odd_qkv_fwd(cur, modps[l], w_in, 0, qkgs[j], ts)
            o, tot = _sb_fwd(qkv, tq)
            x1, y = _odd_out_fwd(o, cur, modps[l], w_out, 0, ts)
            mix = (qkv, o, tot)
        x2, f = _ffn_fwd(x1, modps[l], w_ffn, 0, ts)
        saved.append((cur, x1, y, f, mix))
        cur = x2

    dx, loss_part = _loss_grad(cur, tgt, ts)
    loss = lax.psum(loss_part[0, 0], ("x", "y", "c"))

    dmod = [None] * depth
    d_gmix = [None] * depth
    d_gffn = [None] * depth
    d_conv = [None] * n_even
    d_retg = [None] * n_even
    d_qg = [None] * n_odd
    d_kg = [None] * n_odd
    grads_in_flight = [None] * depth
    for l in reversed(range(depth)):
        j = l // 2
        x0, x1, y, f, mix = saved[l]
        w_in, w_out, w_ffn = weights[l]
        g_ffn = lax.empty(w_ffn.shape, BF16)
        g_in = lax.empty(w_in.shape, BF16)
        g_out = lax.empty(w_out.shape, BF16)
        dx1, dab, h2, sv, df, sg2 = _ffn_bwd(x1, f, dx, modps[l], w_ffn, 0, ts)
        g_ffn = _tn_matmul(dab, 0, h2, g_ffn, 0, "tn_gate")
        g_ffn = _tn_matmul(dab, 1, h2, g_ffn, 1, "tn_up")
        g_ffn = _tn_matmul(sv, 0, df, g_ffn, 2, "tn_down")
        if l % 2 == 0:
            states, zhalo = mix
            dx, dproj, hb, mb, dyb, sg1 = _even_bwd(x0, dx1, y, states, zhalo, modps[l], w_in, w_out, 0,
                                                    cws[j], cos, sin, consts, ts)
            g_in = _tn_matmul(dproj, 0, hb, g_in, 0, "tn_ev_in")
            g_out = _tn_matmul(mb, 0, dyb, g_out, 0, "tn_ev_out")
            d_conv[j] = sg1[4:7, :cd]
            d_retg[j] = sg1[7, :cd]
        else:
            qkv, o, tot = mix
            do, dyb, sg0 = _odd_out_bwd(dx1, y, modps[l], w_out, 0, ts)
            dq, dk, dv = _sb_bwd(qkv, do, tot, tq)
            dx, dqkv, hb, sg1 = _odd_qkv_bwd(x0, dx1, dq, dk, dv, sg0, modps[l], w_in, 0, qkgs[j], ts)
            g_in = _tn_matmul(dqkv, 0, hb, g_in, 0, "tn_od_qkv")
            g_out = _tn_matmul(o, 0, dyb, g_out, 0, "tn_od_out")
            d_qg[j] = sg1[4, :HEAD]
            d_kg[j] = sg1[5, :HEAD]
        pieces = [g.reshape(N_DEV, -1, d) for g in (g_ffn, g_in, g_out)]
        grads_in_flight[l], started = _exchange_start(pieces, [_landing(p, me, True) for p in pieces], True,
                                                      f"grads_start_{l}")
        if l > 0:
            modps[l - 1] = modps[l - 1] + started
        dmod[l] = jnp.concatenate([sg1[0:3], sg2[0:3]], axis=0).reshape(-1)
        d_gmix[l] = sg1[3]
        d_gffn[l] = sg2[3]

    small = jnp.concatenate(
        [jnp.stack(dmod).reshape(-1), jnp.stack(d_gmix).reshape(-1), jnp.stack(d_gffn).reshape(-1),
         jnp.stack(d_retg).reshape(-1), jnp.stack(d_qg).reshape(-1), jnp.stack(d_kg).reshape(-1),
         jnp.stack(d_conv).reshape(-1)])
    n_small = small.shape[0]
    rows_small = -(-n_small // (8 * 128)) * 8
    small = jnp.concatenate([small, jnp.zeros((rows_small * 128 - n_small,), F32)]).reshape(rows_small, 128)
    small = small + started
    got2 = _all_gather(small, "gather_small", False).reshape(N_DEV, rows_small, 128)
    tot_small = _sum_small(got2).reshape(-1)
    n_mod = depth * 6 * d
    dmod_all = got2.reshape(N_DEV, -1)[:, :n_mod].reshape(N_DEV, depth, 6 * d)
    dmod_cols = lax.dynamic_slice(dmod_all, (0, 0, me * cols), (N_DEV, depth, cols)).transpose(1, 0, 2)
    g_ada_w = _ada_bwd(c_all.T, dmod_cols)

    off = [0]

    def take(shape):
        n = int(np.prod(shape))
        out = tot_small[off[0]:off[0] + n].reshape(shape)
        off[0] += n
        return out

    g_ada_b = take((depth, 6 * d))
    g_norm_mix = take((depth, d))
    g_norm_ffn = take((depth, d))
    g_ret_norm = take((n_even, cd))
    g_q_norm = take((n_odd, HEAD))
    g_k_norm = take((n_odd, HEAD))
    g_conv_full = take((n_even, 3, cd))
    g_conv = lax.dynamic_slice(g_conv_full, (0, 0, me * cc), (n_even, 3, cc))

    res = {"ada_w": (g_ada_w,) + _adamw_nd(ada_w, g_ada_w, m_ada_w, v_ada_w, "adamw_ada_w")}
    sums = []
    for l in reversed(range(depth)):
        recv = _exchange_wait(*grads_in_flight[l], True, res["ada_w"][1], f"grads_wait_{l}")
        sums.append([_sum_slots(r, f"sum_{i}") for i, r in enumerate(recv)])
    sums = sums[::-1]
    un = lambda rows: jnp.stack(rows).transpose(0, 2, 1)
    g_gate = un([sums[l][0][0:n_ffn] for l in range(depth)])
    g_up = un([sums[l][0][n_ffn:2 * n_ffn] for l in range(depth)])
    g_down = jnp.stack([sums[l][0][2 * n_ffn:3 * n_ffn] for l in range(depth)])
    g_ev_in = un([sums[l][1] for l in range(0, depth, 2)])
    g_od_qkv = un([sums[l][1] for l in range(1, depth, 2)])
    g_ev_out = jnp.stack([sums[l][2] for l in range(0, depth, 2)])
    g_od_out = jnp.stack([sums[l][2] for l in range(1, depth, 2)])

    big = [("ev_w_in", ev_w_in, g_ev_in, m_ev_w_in, v_ev_w_in),
           ("ev_w_out", ev_w_out, g_ev_out, m_ev_w_out, v_ev_w_out),
           ("od_w_qkv", od_w_qkv, g_od_qkv, m_od_w_qkv, v_od_w_qkv),
           ("od_w_out", od_w_out, g_od_out, m_od_w_out, v_od_w_out),
           ("ffn_w_gate", ffn_w_gate, g_gate, m_ffn_w_gate, v_ffn_w_gate),
           ("ffn_w_up", ffn_w_up, g_up, m_ffn_w_up, v_ffn_w_up),
           ("ffn_w_down", ffn_w_down, g_down, m_ffn_w_down, v_ffn_w_down)]
    for name, w, g, m, v in big:
        res[name] = (g,) + _adamw_nd(w, g, m, v, "adamw_" + name)

    smalls = [("ada_b", ada_b, g_ada_b, m_ada_b, v_ada_b), ("norm_mix_g", norm_mix_g, g_norm_mix, m_norm_mix_g, v_norm_mix_g),
              ("norm_ffn_g", norm_ffn_g, g_norm_ffn, m_norm_ffn_g, v_norm_ffn_g),
              ("ev_conv_w", ev_conv_w, g_conv, m_ev_conv_w, v_ev_conv_w),
              ("ev_ret_norm_g", ev_ret_norm_g, g_ret_norm, m_ev_ret_norm_g, v_ev_ret_norm_g),
              ("od_q_norm_g", od_q_norm_g, g_q_norm, m_od_q_norm_g, v_od_q_norm_g),
              ("od_k_norm_g", od_k_norm_g, g_k_norm, m_od_k_norm_g, v_od_k_norm_g)]

    def pack(arrs):
        flat = jnp.concatenate([a.reshape(-1) for a in arrs])
        rows = -(-flat.shape[0] // (8 * 128)) * 8
        return jnp.concatenate([flat, jnp.zeros((rows * 128 - flat.shape[0],), F32)]).reshape(rows, 128)

    sd, sm, sv_ = _adamw(pack([t[1] for t in smalls]), pack([t[2] for t in smalls]),
                         pack([t[3] for t in smalls]), pack([t[4] for t in smalls]), "adamw_small")
    sd, sm, sv_ = sd.reshape(-1), sm.reshape(-1), sv_.reshape(-1)
    pos = 0
    for name, w, g, m, v in smalls:
        n = int(np.prod(w.shape))
        res[name] = (g, sd[pos:pos + n].reshape(w.shape), sm[pos:pos + n].reshape(w.shape),
                     sv_[pos:pos + n].reshape(w.shape))
        pos += n

    order = ["ada_w", "ada_b", "norm_mix_g", "norm_ffn_g", "ev_w_in", "ev_conv_w", "ev_ret_norm_g", "ev_w_out",
             "od_w_qkv", "od_q_norm_g", "od_k_norm_g", "od_w_out", "ffn_w_gate", "ffn_w_up", "ffn_w_down"]
    outs = [loss, dx[None]]
    for k in range(4):
        outs += [res[name][k] for name in order]
    return tuple(outs)
```

```python
import functools
import math

import numpy as np
import jax
import jax.numpy as jnp
from jax import lax
from jax.experimental import pallas as pl
from jax.experimental.pallas import tpu as pltpu

F32 = jnp.float32
BF16 = jnp.bfloat16
MESH = pl.DeviceIdType.MESH

N_DEV = 8
EPS = 1e-6
CHUNK = 64
HEAD = 128
RET_HEADS = 4
SB_HEADS = 8
ROPE_THETA = 10000.0
KEY_BLOCK = 128
ADAM_LR, ADAM_B1, ADAM_B2, ADAM_EPS, ADAM_WD, ADAM_STEP = 0.001, 0.9, 0.999, 1e-08, 0.01, 10
VMEM_LIMIT = 56 * 1024 * 1024


def _pcall(body, **kw):
    return pl.pallas_call(body, **kw)


def _params(n_grid=1, vmem=VMEM_LIMIT):
    return pltpu.CompilerParams(dimension_semantics=("arbitrary",) * n_grid, vmem_limit_bytes=vmem)


def _mm(a, b):
    return jnp.dot(a, b, preferred_element_type=F32)


def _mm_nt(a, b):
    return lax.dot_general(a, b, (((1,), (1,)), ((), ())), preferred_element_type=F32)


def _mm_tn(a, b):
    return lax.dot_general(a, b, (((0,), (0,)), ((), ())), preferred_element_type=F32)


def _bf(a):
    return a.astype(BF16)


def _sigmoid(a):
    return 1.0 / (1.0 + jnp.exp(-a))


def _sum0(a):
    return jnp.sum(a, axis=0, keepdims=True)


def _full(shape):
    nd = len(shape)
    return pl.BlockSpec(shape, lambda *_: (0,) * nd)


def _normmod_fwd(x, g, sc, sh):
    rstd = lax.rsqrt(jnp.mean(x * x, axis=-1, keepdims=True) + EPS)
    n = x * rstd
    return n, rstd, (n * g) * (1.0 + sc) + sh


def _normmod_bwd(dh, n, rstd, g, sc):
    dsh = _sum0(dh)
    dsc = _sum0(dh * (n * g))
    dg = _sum0(dh * n * (1.0 + sc))
    dn = dh * (g * (1.0 + sc))
    dx = rstd * (dn - n * jnp.mean(dn * n, axis=-1, keepdims=True))
    return dx, dsh, dsc, dg


def _rms_fwd(o):
    rstd = lax.rsqrt(jnp.mean(o * o, axis=-1, keepdims=True) + EPS)
    return o * rstd, rstd


def _rms_bwd(dn, n, rstd):
    return rstd * (dn - n * jnp.mean(dn * n, axis=-1, keepdims=True))


def _all_gather(x2d, name):
    m_per, n = x2d.shape
    space = pltpu.VMEM

    def body(x_ref, out_ref, done_ref, send_sems, recv_sems, local_sem):
        x, y, c = lax.axis_index("x"), lax.axis_index("y"), lax.axis_index("c")
        me, sibling = (x, y, c), (x, y, 1 - c)
        chips = [(1 - x, y), (x, 1 - y), (1 - x, 1 - y)]

        def rows(px, py, pc):
            return out_ref.at[pl.ds((4 * px + 2 * py + pc) * m_per, m_per), :]

        def copy(k, block, to, src=None):
            return pltpu.make_async_remote_copy(
                src_ref=rows(*block) if src is None else src, dst_ref=rows(*block),
                send_sem=send_sems.at[k], recv_sem=recv_sems.at[k],
                device_id=to, device_id_type=MESH)

        mine = pltpu.make_async_copy(x_ref, rows(*me), local_sem)
        mine.start()
        first = [copy(1 + j, me, (*chip, c), src=x_ref) for j, chip in enumerate(chips)]
        first += [copy(0, me, sibling, src=x_ref)]
        for cp in first:
            cp.start()
        passed = [copy(4 + j, (*chip, c), sibling) for j, chip in enumerate(chips)]
        for j, chip in enumerate(chips):
            copy(1 + j, (*chip, c), me).wait_recv()
            passed[j].start()
        copy(0, sibling, me).wait_recv()
        for j, chip in enumerate(chips):
            copy(4 + j, (*chip, 1 - c), me).wait_recv()
        for cp in first + passed:
            cp.wait_send()
        mine.wait()
        done_ref[...] = jnp.zeros_like(done_ref)

    out, done = _pcall(
        body, name=name,
        out_shape=(jax.ShapeDtypeStruct((N_DEV * m_per, n), x2d.dtype), jax.ShapeDtypeStruct((8, 128), F32)),
        in_specs=[pl.BlockSpec(memory_space=space)],
        out_specs=(pl.BlockSpec(memory_space=space), pl.BlockSpec(memory_space=pltpu.VMEM)),
        scratch_shapes=[pltpu.SemaphoreType.DMA((7,)), pltpu.SemaphoreType.DMA((7,)),
                        pltpu.SemaphoreType.DMA],
    )(x2d)
    return out, done[0, 0]


_HBM = pl.BlockSpec(memory_space=pltpu.HBM)
_SEM = pl.BlockSpec(memory_space=pltpu.SEMAPHORE)
_EFFECT = pltpu.SideEffectType.DATAFLOW_SIDE_EFFECTING


def _exchange_copies(src_refs, land_refs, send_sems, recv_sems, scatter):
    x, y, c = lax.axis_index("x"), lax.axis_index("y"), lax.axis_index("c")
    me = 4 * x + 2 * y + c
    out = []
    for i, (s_ref, l_ref) in enumerate(zip(src_refs, land_refs)):
        for k in (2, 4, 6, 3, 5, 7, 1):
            px = (1 - x) if (k >> 2) & 1 else x
            py = (1 - y) if (k >> 1) & 1 else y
            pc = (1 - c) if k & 1 else c
            out.append(pltpu.make_async_remote_copy(
                src_ref=s_ref.at[4 * px + 2 * py + pc] if scatter else s_ref, dst_ref=l_ref.at[me],
                send_sem=send_sems.at[7 * i + k - 1], recv_sem=recv_sems.at[7 * i + k - 1],
                device_id=(px, py, pc), device_id_type=MESH))
    return out


def _exchange_start(srcs, lands, scatter, name):
    n = len(srcs)

    def body(*refs):
        for cp in _exchange_copies(refs[:n], refs[n:2 * n], refs[2 * n], refs[2 * n + 1], scatter):
            cp.start()
        refs[-1][...] = jnp.zeros_like(refs[-1])

    arrays = list(srcs) + list(lands)
    outs = _pcall(
        body, name=name,
        out_shape=(pltpu.SemaphoreType.DMA((7 * n,)), pltpu.SemaphoreType.DMA((7 * n,)),
                   *[pltpu.HBM(a.shape, a.dtype) for a in arrays], jax.ShapeDtypeStruct((8, 128), F32)),
        in_specs=[_HBM] * (2 * n),
        out_specs=(_SEM, _SEM, *[_HBM] * (2 * n), pl.BlockSpec(memory_space=pltpu.VMEM)),
        input_output_aliases={i: 2 + i for i in range(2 * n)},
        compiler_params=pltpu.CompilerParams(has_side_effects=_EFFECT),
    )(*[pltpu.with_memory_space_constraint(a, pltpu.HBM) for a in arrays])
    return (outs[0], outs[1], list(outs[2:2 + n]), list(outs[2 + n:2 + 2 * n])), outs[-1][0, 0]


def _exchange_wait(send_sems, recv_sems, srcs, lands, scatter, after, name):
    n = len(srcs)
    after = list(after)

    def body(*refs):
        for cp in _exchange_copies(refs[:n], refs[n:2 * n], refs[2 * n], refs[2 * n + 1], scatter):
            cp.wait_send()
            cp.wait_recv()

    arrays = list(srcs) + list(lands)
    outs = _pcall(
        body, name=name,
        out_shape=tuple(pltpu.HBM(a.shape, a.dtype) for a in arrays),
        in_specs=[_HBM] * (2 * n) + [_SEM, _SEM] + [pl.BlockSpec(memory_space=pl.ANY)] * len(after),
        out_specs=tuple([_HBM] * (2 * n)),
        input_output_aliases={i: i for i in range(2 * n)},
        compiler_params=pltpu.CompilerParams(has_side_effects=_EFFECT),
    )(*arrays, send_sems, recv_sems, *after)
    return list(outs[n:])


def _landing(src, me, scatter, after=None):
    own = lax.dynamic_index_in_dim(src, me, 0, keepdims=True) if scatter else src[None]
    if after is not None:
        own = own + after.astype(own.dtype)
    shape = src.shape if scatter else (N_DEV,) + src.shape
    return lax.dynamic_update_slice(lax.empty(shape, src.dtype), own, (me, 0, 0))


def _sum_slots(recv, name):
    _, r, n = recv.shape
    tr = r
    for cand in (512, 448, 384, 352, 256, 128, 64, 32, 16, 8):
        if r % cand == 0:
            tr = cand
            break

    def body(r_ref, o_ref):
        acc = r_ref[0].astype(F32)
        for p in range(1, N_DEV):
            acc = acc + r_ref[p].astype(F32)
        o_ref[...] = acc

    return _pcall(
        body, name=name, grid=(r // tr,),
        out_shape=jax.ShapeDtypeStruct((r, n), F32),
        in_specs=[pl.BlockSpec((N_DEV, tr, n), lambda i: (0, i, 0))],
        out_specs=pl.BlockSpec((tr, n), lambda i: (i, 0)),
        compiler_params=_params(1),
    )(recv)


def _adamw(w, g, m, v, name):
    r, n = w.shape
    tr = r
    for cand in (512, 256, 128, 64, 32, 16, 8):
        if r % cand == 0:
            tr = cand
            break

    def body(w_ref, g_ref, m_ref, v_ref, d_ref, nm_ref, nv_ref):
        d_ref[...], nm_ref[...], nv_ref[...] = _adam_update(w_ref[...], g_ref[...], m_ref[...], v_ref[...])

    spec = pl.BlockSpec((tr, n), lambda i: (i, 0))
    shp = jax.ShapeDtypeStruct((r, n), F32)
    return _pcall(
        body, name=name, grid=(r // tr,), out_shape=(shp, shp, shp),
        in_specs=[spec] * 4, out_specs=(spec, spec, spec), compiler_params=_params(1),
    )(w, g, m, v)


def _adam_update(wv, gv, mv, vv):
    bc1 = 1.0 / (1.0 - ADAM_B1 ** ADAM_STEP)
    bc2 = 1.0 / (1.0 - ADAM_B2 ** ADAM_STEP)
    nm = ADAM_B1 * mv + (1.0 - ADAM_B1) * gv
    nv = ADAM_B2 * vv + (1.0 - ADAM_B2) * (gv * gv)
    return -ADAM_LR * ((nm * bc1) / (jnp.sqrt(nv * bc2) + ADAM_EPS) + ADAM_WD * wv), nm, nv


def _adamw_layer(w, g_layer, m, v, outs, idx, name):
    _, a, b = w.shape
    tr = a
    for cand in (512, 256, 128, 64, 32, 16, 8):
        if a % cand == 0:
            tr = cand
            break

    def body(w_ref, g_ref, m_ref, v_ref, o0, o1, o2, o3, go_ref, d_ref, nm_ref, nv_ref):
        gv = g_ref[...]
        go_ref[...] = gv
        d_ref[...], nm_ref[...], nv_ref[...] = _adam_update(w_ref[...], gv, m_ref[...], v_ref[...])

    layer = pl.BlockSpec((None, tr, b), lambda i: (idx, i, 0))
    anyw = pl.BlockSpec(memory_space=pl.ANY)
    shp = jax.ShapeDtypeStruct(w.shape, F32)
    return tuple(_pcall(
        body, name=name, grid=(a // tr,), out_shape=(shp,) * 4,
        in_specs=[layer, pl.BlockSpec((tr, b), lambda i: (i, 0)), layer, layer, anyw, anyw, anyw, anyw],
        out_specs=(layer,) * 4, input_output_aliases={4: 0, 5: 1, 6: 2, 7: 3},
        compiler_params=_params(1),
    )(w, g_layer, m, v, *outs))


def _adamw_nd(w, g, m, v, name):
    shp = w.shape
    f = lambda a: a.reshape(-1, shp[-1])
    d, nm, nv = _adamw(f(w), f(g), f(m), f(v), name)
    return d.reshape(shp), nm.reshape(shp), nv.reshape(shp)


def _ada_fwd(c_all, ada_w, ada_b_cols):
    n_l, d, cols = ada_w.shape

    def body(c_ref, w_ref, b_ref, o_ref):
        cv = c_ref[...]
        ca = cv * _sigmoid(cv)
        o_ref[...] = _mm(_bf(ca), _bf(w_ref[...])) + b_ref[...]

    return _pcall(
        body, name="ada_fwd", grid=(n_l,),
        out_shape=jax.ShapeDtypeStruct((n_l, N_DEV, cols), F32),
        in_specs=[_full((N_DEV, d)), pl.BlockSpec((None, d, cols), lambda l: (l, 0, 0)),
                  pl.BlockSpec((None, 1, cols), lambda l: (l, 0, 0))],
        out_specs=pl.BlockSpec((None, N_DEV, cols), lambda l: (l, 0, 0)),
        compiler_params=_params(1),
    )(c_all, ada_w, ada_b_cols.reshape(n_l, 1, cols))


def _ada_bwd(c_all_t, dmod_cols):
    d = c_all_t.shape[0]
    n_l, _, cols = dmod_cols.shape

    def body(ct_ref, dm_ref, o_ref):
        cv = ct_ref[...]
        ca = cv * _sigmoid(cv)
        dm = dm_ref[...]
        acc = ca[:, 0:1] * dm[0:1, :]
        for b in range(1, N_DEV):
            acc = acc + ca[:, b:b + 1] * dm[b:b + 1, :]
        o_ref[...] = acc

    return _pcall(
        body, name="ada_bwd", grid=(n_l,),
        out_shape=jax.ShapeDtypeStruct((n_l, d, cols), F32),
        in_specs=[_full((d, N_DEV)), pl.BlockSpec((None, N_DEV, cols), lambda l: (l, 0, 0))],
        out_specs=pl.BlockSpec((None, d, cols), lambda l: (l, 0, 0)),
        compiler_params=_params(1),
    )(c_all_t, dmod_cols)


def _sum_small(gathered):
    _, r, n = gathered.shape

    def body(g_ref, o_ref):
        acc = g_ref[0]
        for p in range(1, N_DEV):
            acc = acc + g_ref[p]
        o_ref[...] = acc

    return _pcall(
        body, name="sum_small", out_shape=jax.ShapeDtypeStruct((r, n), F32),
        in_specs=[_full((N_DEV, r, n))], out_specs=_full((r, n)),
    )(gathered)


def _loss_grad(xf, tgt, ts):
    s, d = xf.shape

    def body(x_ref, t_ref, dx_ref, l_ref):
        @pl.when(pl.program_id(0) == 0)
        def _():
            l_ref[...] = jnp.zeros_like(l_ref)
        e = x_ref[...] - t_ref[...]
        dx_ref[...] = e * (1.0 / d)
        l_ref[...] += (0.5 / d) * jnp.sum(jnp.sum(e * e, axis=1, keepdims=True), axis=0, keepdims=True)

    spec = pl.BlockSpec((ts, d), lambda i: (i, 0))
    return _pcall(
        body, name="loss_grad", grid=(s // ts,),
        out_shape=(jax.ShapeDtypeStruct((s, d), F32), jax.ShapeDtypeStruct((1, 1), F32)),
        in_specs=[spec, spec], out_specs=(spec, _full((1, 1))), compiler_params=_params(1),
    )(xf, tgt)


def _tn_matmul(a, col_block, b, buf, slot, name):
    s = a.shape[0]
    k = b.shape[1]
    n_p = buf.shape[2]
    mcols = N_DEV * n_p
    ts = 512 if s % 512 == 0 else 256
    nt = s // ts

    def body(a_ref, b_ref, buf_ref, o_ref, acc):
        i = pl.program_id(0)

        @pl.when(i == 0)
        def _():
            acc[...] = jnp.zeros_like(acc)
        acc[...] += _mm_tn(a_ref[...], b_ref[...])

        @pl.when(i == nt - 1)
        def _():
            o_ref[...] = acc[...].reshape(N_DEV, n_p, k).astype(BF16)

    return _pcall(
        body, name=name, grid=(nt,),
        out_shape=jax.ShapeDtypeStruct(buf.shape, BF16),
        in_specs=[pl.BlockSpec((ts, mcols), lambda i: (i, col_block)),
                  pl.BlockSpec((ts, k), lambda i: (i, 0)),
                  pl.BlockSpec(memory_space=pl.ANY)],
        out_specs=pl.BlockSpec((N_DEV, None, n_p, k), lambda i: (0, slot, 0, 0)),
        scratch_shapes=[pltpu.VMEM((mcols, k), F32)],
        input_output_aliases={2: 0},
        compiler_params=_params(1),
    )(a, b, buf)


def _wspec4(w, slot):
    _, _, n_p, k = w.shape
    return pl.BlockSpec((N_DEV, None, n_p, k), lambda i: (0, slot, 0, 0), pipeline_mode=pl.Buffered(1))


def _ffn_fwd(x1, modp, w352, l, ts):
    s, d = x1.shape
    n_l = w352.shape[1] // 3
    f_dim = N_DEV * w352.shape[2]

    def body(x_ref, mp_ref, wg_ref, wu_ref, wd_ref, x2_ref, f_ref):
        x = x_ref[...]
        _, _, h2 = _normmod_fwd(x, mp_ref[7:8, :], mp_ref[4:5, :], mp_ref[3:4, :])
        hb = _bf(h2)
        f = jnp.zeros((ts, d), F32)
        half_dev, fc = N_DEV // 2, f_dim // 2
        for part in range(2):
            dev0 = part * half_dev
            a = _mm_nt(hb, wg_ref[dev0:dev0 + half_dev].reshape(fc, d))
            b = _mm_nt(hb, wu_ref[dev0:dev0 + half_dev].reshape(fc, d))
            sv = (a * _sigmoid(a)) * b
            f = f + _mm(_bf(sv), wd_ref[dev0:dev0 + half_dev].reshape(fc, d))
        f_ref[...] = f
        x2_ref[...] = x + mp_ref[5:6, :] * f

    tile = pl.BlockSpec((ts, d), lambda i: (i, 0))
    shp = jax.ShapeDtypeStruct((s, d), F32)
    return _pcall(
        body, name="ffn_fwd", grid=(s // ts,), out_shape=(shp, shp),
        in_specs=[tile, _full(modp.shape), _wspec4(w352, l), _wspec4(w352, n_l + l),
                  _wspec4(w352, 2 * n_l + l)],
        out_specs=(tile, tile), compiler_params=_params(1),
    )(x1, modp, w352, w352, w352)


def _ffn_bwd(x1, f, dx2, modp, w352, l, ts):
    s, d = x1.shape
    n_l = w352.shape[1] // 3
    f_dim = N_DEV * w352.shape[2]

    def body(x_ref, f_ref, dx2_ref, mp_ref, wg_ref, wu_ref, wd_ref,
             dx1_ref, dab_ref, h2_ref, s_ref, df_ref, sg_ref):
        @pl.when(pl.program_id(0) == 0)
        def _():
            sg_ref[...] = jnp.zeros_like(sg_ref)
        x = x_ref[...]
        gffn, sc2, g2 = mp_ref[7:8, :], mp_ref[4:5, :], mp_ref[5:6, :]
        n, rstd, h2 = _normmod_fwd(x, gffn, sc2, mp_ref[3:4, :])
        hb = _bf(h2)
        dx2 = dx2_ref[...]
        dfb = _bf(g2 * dx2)
        dh2 = jnp.zeros((ts, d), F32)
        half_dev, fc = N_DEV // 2, f_dim // 2
        for part in range(2):
            dev0, c0 = part * half_dev, part * fc
            wg = wg_ref[dev0:dev0 + half_dev].reshape(fc, d)
            wu = wu_ref[dev0:dev0 + half_dev].reshape(fc, d)
            a = _mm_nt(hb, wg)
            b = _mm_nt(hb, wu)
            sig = _sigmoid(a)
            sa = a * sig
            s_ref[:, c0:c0 + fc] = _bf(sa * b)
            ds = _mm_nt(dfb, wd_ref[dev0:dev0 + half_dev].reshape(fc, d))
            dab = _bf(ds * b * (sig * (1.0 + a * (1.0 - sig))))
            dbb = _bf(ds * sa)
            dab_ref[:, c0:c0 + fc] = dab
            dab_ref[:, f_dim + c0:f_dim + c0 + fc] = dbb
            dh2 = dh2 + _mm(dab, wg) + _mm(dbb, wu)
        dxn, dsh, dsc, dg = _normmod_bwd(dh2, n, rstd, gffn, sc2)
        dx1_ref[...] = dx2 + dxn
        h2_ref[...] = hb
        df_ref[...] = dfb
        sg_ref[0:1, :] += dsh
        sg_ref[1:2, :] += dsc
        sg_ref[2:3, :] += _sum0(dx2 * f_ref[...])
        sg_ref[3:4, :] += dg

    tile = pl.BlockSpec((ts, d), lambda i: (i, 0))
    f32t = jax.ShapeDtypeStruct((s, d), F32)
    bft = jax.ShapeDtypeStruct((s, d), BF16)
    return _pcall(
        body, name="ffn_bwd", grid=(s // ts,),
        out_shape=(f32t, jax.ShapeDtypeStruct((s, 2 * f_dim), BF16), bft,
                   jax.ShapeDtypeStruct((s, f_dim), BF16), bft, jax.ShapeDtypeStruct((8, d), F32)),
        in_specs=[tile, tile, tile, _full(modp.shape), _wspec4(w352, l), _wspec4(w352, n_l + l),
                  _wspec4(w352, 2 * n_l + l)],
        out_specs=(tile, pl.BlockSpec((ts, 2 * f_dim), lambda i: (i, 0)), tile,
                   pl.BlockSpec((ts, f_dim), lambda i: (i, 0)), tile, _full((8, d))),
        compiler_params=_params(1),
    )(x1, f, dx2, modp, w352, w352, w352)


def _retention_consts(ts):
    h = np.arange(RET_HEADS, dtype=np.float64)
    log_g = np.log1p(-np.exp2(-5.0 - h))
    t = np.arange(ts)
    diff = t[:, None] - t[None, :]
    same = (t[:, None] // CHUNK) == (t[None, :] // CHUNK)
    later = (t[:, None] // CHUNK) > (t[None, :] // CHUNK)
    dm = np.where(same, np.abs(diff), np.where(later, diff, 0))[None] * log_g[:, None, None]
    dm = np.where((same | later)[None], np.exp(dm), 0.0)
    qd = np.exp((t[:, None] + 1.0) * log_g[None, :])
    kd = np.exp((ts - 1.0 - t[:, None]) * log_g[None, :])
    qd = np.repeat(qd, HEAD, axis=1)
    kd = np.repeat(kd, HEAD, axis=1)
    tdec = [float(np.exp(ts * lg)) for lg in log_g]
    return (jnp.asarray(dm, F32), jnp.asarray(qd, F32), jnp.asarray(kd, F32), tdec)


def _rope_tables(s):
    inv_freq = 1.0 / (ROPE_THETA ** (jnp.arange(0, HEAD, 2, dtype=F32) / HEAD))
    ang = jnp.arange(s, dtype=F32)[:, None] * inv_freq[None, :]
    cos, sin = jnp.cos(ang), jnp.sin(ang)
    return jnp.concatenate([cos, cos], axis=1), jnp.concatenate([-sin, sin], axis=1)


def _rope(v, cos, sin):
    return v * cos + pltpu.roll(v, HEAD // 2, 1) * sin


def _rope_t(dv, cos, sin):
    return dv * cos + pltpu.roll(dv * sin, HEAD // 2, 1)


def _shift_down(z, k, halo_ref):
    r = pltpu.roll(z, k, 0)
    rows = lax.broadcasted_iota(jnp.int32, z.shape, 0)
    for j in range(k):
        r = jnp.where(rows == j, halo_ref[8 - k + j:8 - k + j + 1, :], r)
    return r


def _shift_up(z, k, halo_ref):
    n = z.shape[0]
    r = pltpu.roll(z, n - k, 0)
    rows = lax.broadcasted_iota(jnp.int32, z.shape, 0)
    for j in range(k):
        r = jnp.where(rows == n - k + j, halo_ref[j:j + 1, :], r)
    return r


def _even_recompute(x, mp_ref, win, cw_ref, cos, sin, dm_ref, qd_ref, kd_ref, halo_ref, state_of):
    cd = 4 * HEAD
    n, rstd, h = _normmod_fwd(x, mp_ref[6:7, :], mp_ref[1:2, :], mp_ref[0:1, :])
    proj = _mm_nt(_bf(h), win)
    bg, cg, u = proj[:, 0:cd], proj[:, cd:2 * cd], proj[:, 2 * cd:3 * cd]
    z = cg * u
    z1 = _shift_down(z, 1, halo_ref)
    z2 = _shift_down(z, 2, halo_ref)
    conv = cw_ref[0:1, :] * z2 + cw_ref[1:2, :] * z1 + cw_ref[2:3, :] * z
    heads = []
    scale = HEAD ** -0.5
    for hh in range(RET_HEADS):
        lo = hh * HEAD
        q = proj[:, 3 * cd + lo:3 * cd + lo + HEAD]
        k = proj[:, 4 * cd + lo:4 * cd + lo + HEAD]
        v = proj[:, 5 * cd + lo:5 * cd + lo + HEAD]
        gate = proj[:, 6 * cd + lo:6 * cd + lo + HEAD]
        qr = _rope(q, cos, sin)
        kr = _rope(k, cos, sin) * scale
        sc = _mm_nt(_bf(qr), _bf(kr)) * dm_ref[hh]
        qs = qr * qd_ref[:, lo:lo + HEAD]
        ks = kr * kd_ref[:, lo:lo + HEAD]
        o = _mm(_bf(sc), _bf(v)) + _mm(_bf(qs), _bf(state_of(hh)))
        on, orstd = _rms_fwd(o)
        sig = _sigmoid(gate)
        heads.append(dict(qr=qr, kr=kr, v=v, gate=gate, sc=sc, qs=qs, ks=ks, on=on, orstd=orstd, sig=sig))
    return dict(n=n, rstd=rstd, h=h, bg=bg, cg=cg, u=u, z=z, z1=z1, z2=z2, conv=conv, heads=heads)


def _even_fwd(x, modp, w448, w128, l, cw, cos, sin, consts, ts):
    s, d = x.shape
    nt = s // ts
    dm, qd, kd, tdec = consts
    cd = 4 * HEAD
    e_in = N_DEV * w448.shape[2]

    def body(x_ref, mp_ref, win_ref, cw_ref, cos_ref, sin_ref, dm_ref, qd_ref, kd_ref, wout_ref,
             x1_ref, y_ref, st_ref, zh_ref, state, halo):
        @pl.when(pl.program_id(0) == 0)
        def _():
            state[...] = jnp.zeros_like(state)
            halo[...] = jnp.zeros_like(halo)
        xv = x_ref[...]
        st_ref[...] = state[...]
        zh_ref[...] = halo[...]
        r = _even_recompute(xv, mp_ref, win_ref[...].reshape(e_in, d), cw_ref, cos_ref[...], sin_ref[...],
                            dm_ref, qd_ref, kd_ref, halo, lambda hh: state[hh])
        halo[...] = r["z"][ts - 8:ts, :]
        parts = [r["bg"] * r["conv"]]
        for hh, hd in enumerate(r["heads"]):
            state[hh] = state[hh] * tdec[hh] + _mm_tn(_bf(hd["ks"]), _bf(hd["v"]))
            rg = cw_ref[3:4, hh * HEAD:(hh + 1) * HEAD]
            parts.append((hd["gate"] * hd["sig"]) * (hd["on"] * rg))
        mcat = jnp.concatenate(parts, axis=1)
        y = _mm(_bf(mcat), wout_ref[...].reshape(d, d))
        y_ref[...] = y
        x1_ref[...] = xv + mp_ref[2:3, :] * y

    tile = pl.BlockSpec((ts, d), lambda i: (i, 0))
    rt = pl.BlockSpec((ts, HEAD), lambda i: (i, 0))
    shp = jax.ShapeDtypeStruct((s, d), F32)
    return _pcall(
        body, name="even_fwd", grid=(nt,),
        out_shape=(shp, shp, jax.ShapeDtypeStruct((nt, RET_HEADS, HEAD, HEAD), F32),
                   jax.ShapeDtypeStruct((nt, 8, cd), F32)),
        in_specs=[tile, _full(modp.shape), _wspec4(w448, l), _full(cw.shape), rt, rt,
                  _full(dm.shape), _full(qd.shape), _full(kd.shape), _wspec4(w128, l)],
        out_specs=(tile, tile, pl.BlockSpec((None, RET_HEADS, HEAD, HEAD), lambda i: (i, 0, 0, 0)),
                   pl.BlockSpec((None, 8, cd), lambda i: (i, 0, 0))),
        scratch_shapes=[pltpu.VMEM((RET_HEADS, HEAD, HEAD), F32), pltpu.VMEM((8, cd), F32)],
        compiler_params=_params(1),
    )(x, modp, w448, cw, cos, sin, dm, qd, kd, w128)


def _even_bwd(x, dx1, y, states, zhalo, modp, w448, w128, l, cw, cos, sin, consts, ts):
    s, d = x.shape
    nt = s // ts
    dm, qd, kd, tdec = consts
    cd = 4 * HEAD
    e_in = N_DEV * w448.shape[2]
    scale = HEAD ** -0.5

    def body(x_ref, dx1_ref, y_ref, st_ref, zh_ref, mp_ref, win_ref, cw_ref, cos_ref, sin_ref,
             dm_ref, qd_ref, kd_ref, wout_ref,
             dx_ref, dproj_ref, h_ref, m_ref, dy_ref, sg_ref, gstate, halo_d):
        @pl.when(pl.program_id(0) == 0)
        def _():
            gstate[...] = jnp.zeros_like(gstate)
            halo_d[...] = jnp.zeros_like(halo_d)
            sg_ref[...] = jnp.zeros_like(sg_ref)
        xv = x_ref[...]
        cos, sin = cos_ref[...], sin_ref[...]
        win = win_ref[...].reshape(e_in, d)
        r = _even_recompute(xv, mp_ref, win, cw_ref, cos, sin, dm_ref, qd_ref, kd_ref, zh_ref,
                            lambda hh: st_ref[hh])
        parts = [r["bg"] * r["conv"]]
        for hh, hd in enumerate(r["heads"]):
            rg = cw_ref[3:4, hh * HEAD:(hh + 1) * HEAD]
            parts.append((hd["gate"] * hd["sig"]) * (hd["on"] * rg))
        m_ref[...] = _bf(jnp.concatenate(parts, axis=1))
        h_ref[...] = _bf(r["h"])

        dx1 = dx1_ref[...]
        dy = mp_ref[2:3, :] * dx1
        dyb = _bf(dy)
        dy_ref[...] = dyb
        sg_ref[2:3, :] += _sum0(dx1 * y_ref[...])
        dmix = _mm_nt(dyb, wout_ref[...].reshape(d, d))

        da_out = dmix[:, 0:cd]
        dbg = da_out * r["conv"]
        dconv = da_out * r["bg"]
        dc1 = _shift_up(dconv, 1, halo_d)
        dc2 = _shift_up(dconv, 2, halo_d)
        dz = cw_ref[2:3, :] * dconv + cw_ref[1:2, :] * dc1 + cw_ref[0:1, :] * dc2
        halo_d[...] = dconv[0:8, :]
        sg_ref[4:5, 0:cd] += _sum0(dconv * r["z2"])
        sg_ref[5:6, 0:cd] += _sum0(dconv * r["z1"])
        sg_ref[6:7, 0:cd] += _sum0(dconv * r["z"])
        dcg = dz * r["u"]
        du = dz * r["cg"]

        dqs, dks, dvs, dgs = [], [], [], []
        for hh, hd in enumerate(r["heads"]):
            lo = hh * HEAD
            rg = cw_ref[3:4, lo:lo + HEAD]
            dr = dmix[:, cd + lo:cd + lo + HEAD]
            sig, gate, on = hd["sig"], hd["gate"], hd["on"]
            rn = on * rg
            dgate = dr * rn * (sig * (1.0 + gate * (1.0 - sig)))
            drn = dr * (gate * sig)
            sg_ref[7:8, lo:lo + HEAD] += _sum0(drn * on)
            do = _rms_bwd(drn * rg, on, hd["orstd"])
            dob = _bf(do)
            gst = _bf(gstate[hh])
            scb = _bf(hd["sc"])
            vb = _bf(hd["v"])
            qrb, krb = _bf(hd["qr"]), _bf(hd["kr"])
            dv = _mm_tn(scb, dob) + _mm(_bf(hd["ks"]), gst)
            dsc = _bf(_mm_nt(dob, vb) * dm_ref[hh])
            dqr = _mm(dsc, krb) + _mm_nt(dob, _bf(st_ref[hh])) * qd_ref[:, lo:lo + HEAD]
            dkr = _mm_tn(dsc, qrb) + _mm_nt(vb, gst) * kd_ref[:, lo:lo + HEAD]
            gstate[hh] = gstate[hh] * tdec[hh] + _mm_tn(_bf(hd["qs"]), dob)
            dqs.append(_rope_t(dqr, cos, sin))
            dks.append(_rope_t(dkr * scale, cos, sin))
            dvs.append(dv)
            dgs.append(dgate)

        dproj = _bf(jnp.concatenate([dbg, dcg, du] + dqs + dks + dvs + dgs, axis=1))
        dproj_ref[...] = dproj
        dh = _mm(dproj, win)
        dxn, dsh, dsc1, dg = _normmod_bwd(dh, r["n"], r["rstd"], mp_ref[6:7, :], mp_ref[1:2, :])
        dx_ref[...] = dx1 + dxn
        sg_ref[0:1, :] += dsh
        sg_ref[1:2, :] += dsc1
        sg_ref[3:4, :] += dg

    rev = lambda i: (nt - 1 - i, 0)
    tile = pl.BlockSpec((ts, d), rev)
    rt = pl.BlockSpec((ts, HEAD), rev)
    bft = jax.ShapeDtypeStruct((s, d), BF16)
    return _pcall(
        body, name="even_bwd", grid=(nt,),
        out_shape=(jax.ShapeDtypeStruct((s, d), F32), jax.ShapeDtypeStruct((s, e_in), BF16), bft, bft, bft,
                   jax.ShapeDtypeStruct((8, d), F32)),
        in_specs=[tile, tile, tile,
                  pl.BlockSpec((None, RET_HEADS, HEAD, HEAD), lambda i: (nt - 1 - i, 0, 0, 0)),
                  pl.BlockSpec((None, 8, cd), lambda i: (nt - 1 - i, 0, 0)),
                  _full(modp.shape), _wspec4(w448, l), _full(cw.shape), rt, rt,
                  _full(dm.shape), _full(qd.shape), _full(kd.shape), _wspec4(w128, l)],
        out_specs=(tile, pl.BlockSpec((ts, e_in), rev), tile, tile, tile, _full((8, d))),
        scratch_shapes=[pltpu.VMEM((RET_HEADS, HEAD, HEAD), F32), pltpu.VMEM((8, cd), F32)],
        compiler_params=_params(1),
    )(x, dx1, y, states, zhalo, modp, w448, cw, cos, sin, dm, qd, kd, w128)


def _odd_qkv_fwd(x, modp, w384, j, qkg, ts):
    s, d = x.shape
    n3 = N_DEV * w384.shape[2]

    def body(x_ref, mp_ref, w_ref, g_ref, o_ref):
        _, _, h = _normmod_fwd(x_ref[...], mp_ref[6:7, :], mp_ref[1:2, :], mp_ref[0:1, :])
        qkv = _mm_nt(_bf(h), w_ref[...].reshape(n3, d))
        for hh in range(SB_HEADS):
            lo = hh * HEAD
            qn, _ = _rms_fwd(qkv[:, lo:lo + HEAD])
            kn, _ = _rms_fwd(qkv[:, d + lo:d + lo + HEAD])
            o_ref[:, lo:lo + HEAD] = _bf(qn * g_ref[0:1, :])
            o_ref[:, d + lo:d + lo + HEAD] = _bf(kn * g_ref[1:2, :])
        o_ref[:, 2 * d:3 * d] = _bf(qkv[:, 2 * d:3 * d])

    return _pcall(
        body, name="odd_qkv_fwd", grid=(s // ts,),
        out_shape=jax.ShapeDtypeStruct((s, n3), BF16),
        in_specs=[pl.BlockSpec((ts, d), lambda i: (i, 0)), _full(modp.shape), _wspec4(w384, j),
                  _full(qkg.shape)],
        out_specs=pl.BlockSpec((ts, n3), lambda i: (i, 0)), compiler_params=_params(1),
    )(x, modp, w384, qkg)


SB_QUERIES = 512
SB_WIDE = 256


def _sb_logits(q, kw, mask):
    z = _mm_nt(q, kw) * (HEAD ** -0.5)
    e = jnp.exp(-jnp.abs(z))
    lb = jnp.minimum(z, 0.0) - jnp.log(1.0 + e)
    lk = lb - z
    if mask is not None:
        lk = jnp.where(mask, lk, 0.0)
    return lb, lk


def _tri(n, above):
    ri = lax.broadcasted_iota(jnp.int32, (n, n), 0)
    ci = lax.broadcasted_iota(jnp.int32, (n, n), 1)
    return ((ri > ci) if above else (ri < ci)).astype(BF16)


def _split_dot(a, tri):
    hi = _bf(a)
    lo = _bf(a - hi.astype(F32))
    return _mm(hi, tri) + _mm(lo, tri)


def _sb_fwd(qkv, tq):
    s = qkv.shape[0]
    d = qkv.shape[1] // 3
    nq = s // tq
    assert tq % SB_WIDE == 0
    parts = tq // SB_WIDE

    def body(q_ref, k_ref, v_ref, o_ref, t_ref, o_acc, run):
        qi = pl.program_id(1)
        base = qi * tq
        upper = _tri(SB_WIDE, True)
        o_acc[...] = jnp.zeros_like(o_acc)
        run[...] = jnp.zeros_like(run)

        def wide_step(ks, row0, masked):
            rows = slice(row0, tq)
            mask = None
            if masked:
                qpos = base + row0 + lax.broadcasted_iota(jnp.int32, (tq - row0, SB_WIDE), 0)
                mask = qpos > ks + lax.broadcasted_iota(jnp.int32, (tq - row0, SB_WIDE), 1)
            lb, lk = _sb_logits(q_ref[rows, :], k_ref[pl.ds(ks, SB_WIDE), :], mask)
            w = jnp.exp(lb + (_split_dot(lk, upper) + run[rows, :]))
            if masked:
                w = jnp.where(mask, w, 0.0)
            o_acc[rows, :] += _mm(_bf(w), v_ref[pl.ds(ks, SB_WIDE), :])
            run[rows, :] += jnp.sum(lk, axis=1, keepdims=True)

        for part in reversed(range(parts)):
            wide_step(pl.multiple_of(base + part * SB_WIDE, SB_WIDE), part * SB_WIDE, True)
        nsteps = qi * parts

        def step(it, carry):
            wide_step(pl.multiple_of((nsteps - 1 - it) * SB_WIDE, SB_WIDE), 0, False)
            return carry

        lax.fori_loop(0, nsteps, step, 0)
        o_ref[...] = _bf(o_acc[...])
        t_ref[...] = run[...]

    nh = d // HEAD
    return _pcall(
        body, name="sb_fwd", grid=(nh, nq),
        out_shape=(jax.ShapeDtypeStruct((s, d), BF16), jax.ShapeDtypeStruct((nh, s, 1), F32)),
        in_specs=[pl.BlockSpec((tq, HEAD), lambda h, i: (i, h)),
                  pl.BlockSpec((s, HEAD), lambda h, i: (0, nh + h)),
                  pl.BlockSpec((s, HEAD), lambda h, i: (0, 2 * nh + h))],
        out_specs=(pl.BlockSpec((tq, HEAD), lambda h, i: (i, h)),
                   pl.BlockSpec((None, tq, 1), lambda h, i: (h, i, 0))),
        scratch_shapes=[pltpu.VMEM((tq, HEAD), F32), pltpu.VMEM((tq, 1), F32)],
        compiler_params=_params(2),
    )(qkv, qkv, qkv)


def _sb_bwd(qkv, do, tot, tq):
    s = qkv.shape[0]
    d = qkv.shape[1] // 3
    nq = s // tq
    scale = HEAD ** -0.5
    assert tq % SB_WIDE == 0
    parts = tq // SB_WIDE

    def body(q_ref, k_ref, v_ref, do_ref, t_ref, dq_ref, dk_ref, dv_ref, pk, pd):
        qi = pl.program_id(1)

        @pl.when(qi == 0)
        def _():
            dk_ref[...] = jnp.zeros_like(dk_ref)
            dv_ref[...] = jnp.zeros_like(dv_ref)
        base = qi * tq
        upper = _tri(SB_WIDE, True)
        lower = _tri(SB_WIDE, False)
        dq_ref[...] = jnp.zeros_like(dq_ref)
        pk[...] = jnp.zeros_like(pk)
        pd[...] = jnp.zeros_like(pd)

        def wide_step(ks, row0, masked):
            rows = slice(row0, tq)
            mask = None
            if masked:
                qpos = base + row0 + lax.broadcasted_iota(jnp.int32, (tq - row0, SB_WIDE), 0)
                mask = qpos > ks + lax.broadcasted_iota(jnp.int32, (tq - row0, SB_WIDE), 1)
            kw = k_ref[pl.ds(ks, SB_WIDE), :]
            lb, lk = _sb_logits(q_ref[rows, :], kw, mask)
            pk_new = pk[rows, :] + jnp.sum(lk, axis=1, keepdims=True)
            pk[rows, :] = pk_new
            w = jnp.exp(lb + (_split_dot(lk, upper) + (t_ref[rows, :] - pk_new)))
            if masked:
                w = jnp.where(mask, w, 0.0)
            de = _mm_nt(do_ref[rows, :], v_ref[pl.ds(ks, SB_WIDE), :]) * w
            dlk = _split_dot(de, lower) + pd[rows, :]
            pd[rows, :] += jnp.sum(de, axis=1, keepdims=True)
            dz = (de - jnp.exp(lb) * (de + dlk)) * scale
            if masked:
                dz = jnp.where(mask, dz, 0.0)
            dzb = _bf(dz)
            dq_ref[rows, :] += _mm(dzb, kw)
            dv_ref[pl.ds(ks, SB_WIDE), :] += _mm_tn(_bf(w), do_ref[rows, :])
            dk_ref[pl.ds(ks, SB_WIDE), :] += _mm_tn(dzb, q_ref[rows, :])

        def step(jb, carry):
            wide_step(pl.multiple_of(jb * SB_WIDE, SB_WIDE), 0, False)
            return carry

        lax.fori_loop(0, qi * parts, step, 0)
        for part in range(parts):
            wide_step(pl.multiple_of(base + part * SB_WIDE, SB_WIDE), part * SB_WIDE, True)

    nh = d // HEAD
    shp = jax.ShapeDtypeStruct((s, d), F32)
    return _pcall(
        body, name="sb_bwd", grid=(nh, nq), out_shape=(shp, shp, shp),
        in_specs=[pl.BlockSpec((tq, HEAD), lambda h, i: (i, h)),
                  pl.BlockSpec((s, HEAD), lambda h, i: (0, nh + h)),
                  pl.BlockSpec((s, HEAD), lambda h, i: (0, 2 * nh + h)),
                  pl.BlockSpec((tq, HEAD), lambda h, i: (i, h)),
                  pl.BlockSpec((None, tq, 1), lambda h, i: (h, i, 0))],
        out_specs=(pl.BlockSpec((tq, HEAD), lambda h, i: (i, h)),
                   pl.BlockSpec((s, HEAD), lambda h, i: (0, h)),
                   pl.BlockSpec((s, HEAD), lambda h, i: (0, h))),
        scratch_shapes=[pltpu.VMEM((tq, 1), F32), pltpu.VMEM((tq, 1), F32)],
        compiler_params=_params(2),
    )(qkv, qkv, qkv, do, tot)


def _odd_out_fwd(o, x, modp, w128, slot, ts):
    s, d = x.shape

    def body(o_ref, x_ref, mp_ref, w_ref, x1_ref, y_ref):
        y = _mm(o_ref[...], w_ref[...].reshape(d, d))
        y_ref[...] = y
        x1_ref[...] = x_ref[...] + mp_ref[2:3, :] * y

    tile = pl.BlockSpec((ts, d), lambda i: (i, 0))
    shp = jax.ShapeDtypeStruct((s, d), F32)
    return _pcall(
        body, name="odd_out_fwd", grid=(s // ts,), out_shape=(shp, shp),
        in_specs=[tile, tile, _full(modp.shape), _wspec4(w128, slot)],
        out_specs=(tile, tile), compiler_params=_params(1),
    )(o, x, modp, w128)


def _odd_out_bwd(dx1, y, modp, w128, slot, ts):
    s, d = dx1.shape

    def body(dx1_ref, y_ref, mp_ref, w_ref, do_ref, dy_ref, sg_ref):
        @pl.when(pl.program_id(0) == 0)
        def _():
            sg_ref[...] = jnp.zeros_like(sg_ref)
        dx1v = dx1_ref[...]
        dyb = _bf(mp_ref[2:3, :] * dx1v)
        dy_ref[...] = dyb
        do_ref[...] = _bf(_mm_nt(dyb, w_ref[...].reshape(d, d)))
        sg_ref[2:3, :] += _sum0(dx1v * y_ref[...])

    tile = pl.BlockSpec((ts, d), lambda i: (i, 0))
    bft = jax.ShapeDtypeStruct((s, d), BF16)
    return _pcall(
        body, name="odd_out_bwd", grid=(s // ts,),
        out_shape=(bft, bft, jax.ShapeDtypeStruct((8, d), F32)),
        in_specs=[tile, tile, _full(modp.shape), _wspec4(w128, slot)],
        out_specs=(tile, tile, _full((8, d))), compiler_params=_params(1),
    )(dx1, y, modp, w128)


def _odd_qkv_bwd(x, dx1, dq, dk, dv, sg_in, modp, w384, j, qkg, ts):
    s, d = x.shape
    n3 = N_DEV * w384.shape[2]

    def body(x_ref, dx1_ref, dq_ref, dk_ref, dv_ref, sgi_ref, mp_ref, w_ref, g_ref,
             dx_ref, dqkv_ref, h_ref, sg_ref):
        @pl.when(pl.program_id(0) == 0)
        def _():
            sg_ref[...] = sgi_ref[...]
        gmix, sc1 = mp_ref[6:7, :], mp_ref[1:2, :]
        n, rstd, h = _normmod_fwd(x_ref[...], gmix, sc1, mp_ref[0:1, :])
        hb = _bf(h)
        h_ref[...] = hb
        w = w_ref[...].reshape(n3, d)
        qkv = _mm_nt(hb, w)
        parts_q, parts_k = [], []
        gq, gk = g_ref[0:1, :], g_ref[1:2, :]
        dgq = jnp.zeros((1, HEAD), F32)
        dgk = jnp.zeros((1, HEAD), F32)
        for hh in range(SB_HEADS):
            lo = hh * HEAD
            qn, qr = _rms_fwd(qkv[:, lo:lo + HEAD])
            kn, kr = _rms_fwd(qkv[:, d + lo:d + lo + HEAD])
            dqn = dq_ref[:, lo:lo + HEAD]
            dkn = dk_ref[:, lo:lo + HEAD]
            dgq = dgq + _sum0(dqn * qn)
            dgk = dgk + _sum0(dkn * kn)
            parts_q.append(_rms_bwd(dqn * gq, qn, qr))
            parts_k.append(_rms_bwd(dkn * gk, kn, kr))
        dqkv = _bf(jnp.concatenate(parts_q + parts_k + [dv_ref[...]], axis=1))
        dqkv_ref[...] = dqkv
        dh = _mm(dqkv, w)
        dxn, dsh, dsc, dg = _normmod_bwd(dh, n, rstd, gmix, sc1)
        dx_ref[...] = dx1_ref[...] + dxn
        sg_ref[0:1, :] += dsh
        sg_ref[1:2, :] += dsc
        sg_ref[3:4, :] += dg
        sg_ref[4:5, 0:HEAD] += dgq
        sg_ref[5:6, 0:HEAD] += dgk

    tile = pl.BlockSpec((ts, d), lambda i: (i, 0))
    return _pcall(
        body, name="odd_qkv_bwd", grid=(s // ts,),
        out_shape=(jax.ShapeDtypeStruct((s, d), F32), jax.ShapeDtypeStruct((s, n3), BF16),
                   jax.ShapeDtypeStruct((s, d), BF16), jax.ShapeDtypeStruct((8, d), F32)),
        in_specs=[tile, tile, tile, tile, tile, _full((8, d)), _full(modp.shape), _wspec4(w384, j),
                  _full(qkg.shape)],
        out_specs=(tile, pl.BlockSpec((ts, n3), lambda i: (i, 0)), tile, _full((8, d))),
        compiler_params=_params(1),
    )(x, dx1, dq, dk, dv, sg_in, modp, w384, qkg)


def _pad_rows(a, rows):
    return jnp.concatenate([a, jnp.zeros((rows - a.shape[0],) + a.shape[1:], a.dtype)], axis=0)


def kernel(x, c, ada_w, ada_b, norm_mix_g, norm_ffn_g, ev_w_in, ev_conv_w, ev_ret_norm_g, ev_w_out, od_w_qkv, od_q_norm_g, od_k_norm_g, od_w_out, ffn_w_gate, ffn_w_up, ffn_w_down, loss_target, m_ada_w, m_ada_b, m_norm_mix_g, m_norm_ffn_g, m_ev_w_in, m_ev_conv_w, m_ev_ret_norm_g, m_ev_w_out, m_od_w_qkv, m_od_q_norm_g, m_od_k_norm_g, m_od_w_out, m_ffn_w_gate, m_ffn_w_up, m_ffn_w_down, v_ada_w, v_ada_b, v_norm_mix_g, v_norm_ffn_g, v_ev_w_in, v_ev_conv_w, v_ev_ret_norm_g, v_ev_w_out, v_od_w_qkv, v_od_q_norm_g, v_od_k_norm_g, v_od_w_out, v_ffn_w_gate, v_ffn_w_up, v_ffn_w_down):
    me = 4 * lax.axis_index("x") + 2 * lax.axis_index("y") + lax.axis_index("c")
    xs = x[0]
    tgt = loss_target[0]
    s, d = xs.shape
    depth = ada_w.shape[0]
    n_even, n_odd = ev_w_in.shape[0], od_w_qkv.shape[0]
    ts = 256
    tq = SB_QUERIES
    cd = 4 * HEAD
    cc = ev_conv_w.shape[2]

    pack0 = jnp.zeros((8, d), F32).at[0].set(c[0]).at[1, :n_even * 3 * cc].set(ev_conv_w.reshape(-1))
    got0, _ = _all_gather(pack0, "gather_cond")
    got0 = got0.reshape(N_DEV, 8, d)
    c_all = got0[:, 0, :]
    conv_all = got0[:, 1, :n_even * 3 * cc].reshape(N_DEV, n_even, 3, cc).transpose(1, 2, 0, 3)
    conv_all = conv_all.reshape(n_even, 3, N_DEV * cc)
    cols = ada_w.shape[2]
    ada_b_cols = lax.dynamic_slice(ada_b, (0, me * cols), (depth, cols))
    mod_cols = _ada_fwd(c_all, ada_w, ada_b_cols)
    got1, cond_done = _all_gather(mod_cols.reshape(depth * N_DEV, cols), "gather_mod")
    got1 = got1.reshape(N_DEV, depth, N_DEV, cols)
    mod = lax.dynamic_index_in_dim(got1, me, axis=2, keepdims=False)
    mod = mod.transpose(1, 0, 2).reshape(depth, 6, d)
    modps = [jnp.concatenate([mod[l], norm_mix_g[l][None], norm_ffn_g[l][None]], axis=0) for l in range(depth)]

    in_flight = []
    started = cond_done
    for l in range(depth):
        j = l // 2
        plain = lambda w: _bf(w + started)
        tr = lambda w: plain(w).T
        blocks = [tr(ev_w_in[j]), plain(ev_w_out[j])] if l % 2 == 0 else [tr(od_w_qkv[j]), plain(od_w_out[j])]
        blocks.append(jnp.concatenate([tr(ffn_w_gate[l]), tr(ffn_w_up[l]), plain(ffn_w_down[l])], axis=0))
        lands = [_landing(b, me, False) for b in blocks]
        flight, started = _exchange_start(blocks, lands, False, f"gather_start_{l}")
        in_flight.append(flight)
        modps[0] = modps[0] + started
    n_ffn = ffn_w_down.shape[1]

    def layer_weights(l, after):
        got = _exchange_wait(*in_flight[l], False, [after], f"gather_wait_{l}")
        w_in = got[0].reshape(N_DEV, 1, -1, d)
        w_out = got[1].reshape(N_DEV, 1, -1, d)
        return w_in, w_out, got[2].reshape(N_DEV, 3, n_ffn, d)

    cos, sin = _rope_tables(s)
    consts = _retention_consts(ts)
    cws = [_pad_rows(jnp.concatenate([conv_all[j], ev_ret_norm_g[j][None]], axis=0), 8) for j in range(n_even)]
    qkgs = [_pad_rows(jnp.stack([od_q_norm_g[j], od_k_norm_g[j]]), 8) for j in range(n_odd)]

    saved = []
    weights = []
    cur = xs
    for l in range(depth):
        j = l // 2
        w_in, w_out, w_ffn = layer_weights(l, cur)
        weights.append((w_in, w_out, w_ffn))
        if l % 2 == 0:
            x1, y, states, zhalo = _even_fwd(cur, modps[l], w_in, w_out, 0, cws[j], cos, sin, consts, ts)
            mix = (states, zhalo)
        else:
            qkv = _odd_qkv_fwd(cur, modps[l], w_in, 0, qkgs[j], ts)
            o, tot = _sb_fwd(qkv, tq)
            x1, y = _odd_out_fwd(o, cur, modps[l], w_out, 0, ts)
            mix = (qkv, o, tot)
        x2, f = _ffn_fwd(x1, modps[l], w_ffn, 0, ts)
        saved.append((cur, x1, y, f, mix))
        cur = x2

    dx, loss_part = _loss_grad(cur, tgt, ts)
    loss = lax.psum(loss_part[0, 0], ("x", "y", "c"))

    dmod = [None] * depth
    d_gmix = [None] * depth
    d_gffn = [None] * depth
    d_conv = [None] * n_even
    d_retg = [None] * n_even
    d_qg = [None] * n_odd
    d_kg = [None] * n_odd
    grads_in_flight = [None] * depth
    for l in reversed(range(depth)):
        j = l // 2
        x0, x1, y, f, mix = saved[l]
        w_in, w_out, w_ffn = weights[l]
        g_ffn = lax.empty(w_ffn.shape, BF16)
        g_in = lax.empty(w_in.shape, BF16)
        g_out = lax.empty(w_out.shape, BF16)
        dx1, dab, h2, sv, df, sg2 = _ffn_bwd(x1, f, dx, modps[l], w_ffn, 0, ts)
        g_ffn = _tn_matmul(dab, 0, h2, g_ffn, 0, "tn_gate")
        g_ffn = _tn_matmul(dab, 1, h2, g_ffn, 1, "tn_up")
        g_ffn = _tn_matmul(sv, 0, df, g_ffn, 2, "tn_down")
        if l % 2 == 0:
            states, zhalo = mix
            dx, dproj, hb, mb, dyb, sg1 = _even_bwd(x0, dx1, y, states, zhalo, modps[l], w_in, w_out, 0,
                                                    cws[j], cos, sin, consts, ts)
            g_in = _tn_matmul(dproj, 0, hb, g_in, 0, "tn_ev_in")
            g_out = _tn_matmul(mb, 0, dyb, g_out, 0, "tn_ev_out")
            d_conv[j] = sg1[4:7, :cd]
            d_retg[j] = sg1[7, :cd]
        else:
            qkv, o, tot = mix
            do, dyb, sg0 = _odd_out_bwd(dx1, y, modps[l], w_out, 0, ts)
            dq, dk, dv = _sb_bwd(qkv, do, tot, tq)
            dx, dqkv, hb, sg1 = _odd_qkv_bwd(x0, dx1, dq, dk, dv, sg0, modps[l], w_in, 0, qkgs[j], ts)
            g_in = _tn_matmul(dqkv, 0, hb, g_in, 0, "tn_od_qkv")
            g_out = _tn_matmul(o, 0, dyb, g_out, 0, "tn_od_out")
            d_qg[j] = sg1[4, :HEAD]
            d_kg[j] = sg1[5, :HEAD]
        pieces = [g.reshape(N_DEV, -1, d) for g in (g_ffn, g_in, g_out)]
        if l > 0:
            grads_in_flight[l], started = _exchange_start(pieces, [_landing(p, me, True) for p in pieces], True,
                                                          f"grads_start_{l}")
            modps[l - 1] = modps[l - 1] + started
        dmod[l] = jnp.concatenate([sg1[0:3], sg2[0:3]], axis=0).reshape(-1)
        d_gmix[l] = sg1[3]
        d_gffn[l] = sg2[3]

    small = jnp.concatenate(
        [jnp.stack(dmod).reshape(-1), jnp.stack(d_gmix).reshape(-1), jnp.stack(d_gffn).reshape(-1),
         jnp.stack(d_retg).reshape(-1), jnp.stack(d_qg).reshape(-1), jnp.stack(d_kg).reshape(-1),
         jnp.stack(d_conv).reshape(-1)])
    n_small = small.shape[0]
    rows_small = -(-n_small // (8 * 128)) * 8
    small = jnp.concatenate([small, jnp.zeros((rows_small * 128 - n_small,), F32)]).reshape(rows_small, 128)
    got2, small_done = _all_gather(small, "gather_small")
    got2 = got2.reshape(N_DEV, rows_small, 128)
    grads_in_flight[0], started = _exchange_start(pieces, [_landing(p, me, True, small_done) for p in pieces],
                                                  True, "grads_start_0")
    tot_small = _sum_small(got2).reshape(-1)
    n_mod = depth * 6 * d
    dmod_all = got2.reshape(N_DEV, -1)[:, :n_mod].reshape(N_DEV, depth, 6 * d)
    dmod_cols = lax.dynamic_slice(dmod_all, (0, 0, me * cols), (N_DEV, depth, cols)).transpose(1, 0, 2)
    g_ada_w = _ada_bwd(c_all.T, dmod_cols + started)

    off = [0]

    def take(shape):
        n = int(np.prod(shape))
        out = tot_small[off[0]:off[0] + n].reshape(shape)
        off[0] += n
        return out

    g_ada_b = take((depth, 6 * d))
    g_norm_mix = take((depth, d))
    g_norm_ffn = take((depth, d))
    g_ret_norm = take((n_even, cd))
    g_q_norm = take((n_odd, HEAD))
    g_k_norm = take((n_odd, HEAD))
    g_conv_full = take((n_even, 3, cd))
    g_conv = lax.dynamic_slice(g_conv_full, (0, 0, me * cc), (n_even, 3, cc))

    res = {"ada_w": (g_ada_w,) + _adamw_nd(ada_w, g_ada_w, m_ada_w, v_ada_w, "adamw_ada_w")}
    big = {"ev_w_in": (ev_w_in, m_ev_w_in, v_ev_w_in), "ev_w_out": (ev_w_out, m_ev_w_out, v_ev_w_out),
           "od_w_qkv": (od_w_qkv, m_od_w_qkv, v_od_w_qkv), "od_w_out": (od_w_out, m_od_w_out, v_od_w_out),
           "ffn_w_gate": (ffn_w_gate, m_ffn_w_gate, v_ffn_w_gate), "ffn_w_up": (ffn_w_up, m_ffn_w_up, v_ffn_w_up),
           "ffn_w_down": (ffn_w_down, m_ffn_w_down, v_ffn_w_down)}
    for name, (w, _, _) in big.items():
        res[name] = tuple(lax.empty(w.shape, F32) for _ in range(4))

    def update(name, idx, g_layer):
        w, m, v = big[name]
        res[name] = _adamw_layer(w, g_layer, m, v, res[name], idx, "adamw_" + name)

    after = [res["ada_w"][1]]
    for l in reversed(range(depth)):
        recv = _exchange_wait(*grads_in_flight[l], True, after, f"grads_wait_{l}")
        s_ffn, s_in, s_out = [_sum_slots(r, f"sum_{i}") for i, r in enumerate(recv)]
        j = l // 2
        update("ffn_w_gate", l, s_ffn[0:n_ffn].T)
        update("ffn_w_up", l, s_ffn[n_ffn:2 * n_ffn].T)
        update("ffn_w_down", l, s_ffn[2 * n_ffn:3 * n_ffn])
        update("ev_w_in" if l % 2 == 0 else "od_w_qkv", j, s_in.T)
        update("ev_w_out" if l % 2 == 0 else "od_w_out", j, s_out)
        after = [res[name][1] for name in big]

    smalls = [("ada_b", ada_b, g_ada_b, m_ada_b, v_ada_b), ("norm_mix_g", norm_mix_g, g_norm_mix, m_norm_mix_g, v_norm_mix_g),
              ("norm_ffn_g", norm_ffn_g, g_norm_ffn, m_norm_ffn_g, v_norm_ffn_g),
              ("ev_conv_w", ev_conv_w, g_conv, m_ev_conv_w, v_ev_conv_w),
              ("ev_ret_norm_g", ev_ret_norm_g, g_ret_norm, m_ev_ret_norm_g, v_ev_ret_norm_g),
              ("od_q_norm_g", od_q_norm_g, g_q_norm, m_od_q_norm_g, v_od_q_norm_g),
              ("od_k_norm_g", od_k_norm_g, g_k_norm, m_od_k_norm_g, v_od_k_norm_g)]

    def pack(arrs):
        flat = jnp.concatenate([a.reshape(-1) for a in arrs])
        rows = -(-flat.shape[0] // (8 * 128)) * 8
        return jnp.concatenate([flat, jnp.zeros((rows * 128 - flat.shape[0],), F32)]).reshape(rows, 128)

    sd, sm, sv_ = _adamw(pack([t[1] for t in smalls]), pack([t[2] for t in smalls]),
                         pack([t[3] for t in smalls]), pack([t[4] for t in smalls]), "adamw_small")
    sd, sm, sv_ = sd.reshape(-1), sm.reshape(-1), sv_.reshape(-1)
    pos = 0
    for name, w, g, m, v in smalls:
        n = int(np.prod(w.shape))
        res[name] = (g, sd[pos:pos + n].reshape(w.shape), sm[pos:pos + n].reshape(w.shape),
                     sv_[pos:pos + n].reshape(w.shape))
        pos += n

    order = ["ada_w", "ada_b", "norm_mix_g", "norm_ffn_g", "ev_w_in", "ev_conv_w", "ev_ret_norm_g", "ev_w_out",
             "od_w_qkv", "od_q_norm_g", "od_k_norm_g", "od_w_out", "ffn_w_gate", "ffn_w_up", "ffn_w_down"]
    outs = [loss, dx[None]]
    for k in range(4):
        outs += [res[name][k] for name in order]
    return tuple(outs)
```

```python
import functools
import math

import numpy as np
import jax
import jax.numpy as jnp
from jax import lax
from jax.experimental import pallas as pl
from jax.experimental.pallas import tpu as pltpu

F32 = jnp.float32
BF16 = jnp.bfloat16
MESH = pl.DeviceIdType.MESH

N_DEV = 8
EPS = 1e-6
CHUNK = 64
HEAD = 128
RET_HEADS = 4
SB_HEADS = 8
ROPE_THETA = 10000.0
KEY_BLOCK = 128
ADAM_LR, ADAM_B1, ADAM_B2, ADAM_EPS, ADAM_WD, ADAM_STEP = 0.001, 0.9, 0.999, 1e-08, 0.01, 10
VMEM_LIMIT = 56 * 1024 * 1024


def _pcall(body, **kw):
    return pl.pallas_call(body, **kw)


def _params(n_grid=1, vmem=VMEM_LIMIT):
    return pltpu.CompilerParams(dimension_semantics=("arbitrary",) * n_grid, vmem_limit_bytes=vmem)


def _mm(a, b):
    return jnp.dot(a, b, preferred_element_type=F32)


def _mm_nt(a, b):
    return lax.dot_general(a, b, (((1,), (1,)), ((), ())), preferred_element_type=F32)


def _mm_tn(a, b):
    return lax.dot_general(a, b, (((0,), (0,)), ((), ())), preferred_element_type=F32)


def _bf(a):
    return a.astype(BF16)


def _sigmoid(a):
    return 1.0 / (1.0 + jnp.exp(-a))


def _sum0(a):
    return jnp.sum(a, axis=0, keepdims=True)


def _full(shape):
    nd = len(shape)
    return pl.BlockSpec(shape, lambda *_: (0,) * nd)


def _normmod_fwd(x, g, sc, sh):
    rstd = lax.rsqrt(jnp.mean(x * x, axis=-1, keepdims=True) + EPS)
    n = x * rstd
    return n, rstd, (n * g) * (1.0 + sc) + sh


def _normmod_bwd(dh, n, rstd, g, sc):
    dsh = _sum0(dh)
    dsc = _sum0(dh * (n * g))
    dg = _sum0(dh * n * (1.0 + sc))
    dn = dh * (g * (1.0 + sc))
    dx = rstd * (dn - n * jnp.mean(dn * n, axis=-1, keepdims=True))
    return dx, dsh, dsc, dg


def _rms_fwd(o):
    rstd = lax.rsqrt(jnp.mean(o * o, axis=-1, keepdims=True) + EPS)
    return o * rstd, rstd


def _rms_bwd(dn, n, rstd):
    return rstd * (dn - n * jnp.mean(dn * n, axis=-1, keepdims=True))


def _all_gather(x2d, name):
    m_per, n = x2d.shape
    space = pltpu.VMEM

    def body(x_ref, out_ref, done_ref, send_sems, recv_sems, local_sem):
        x, y, c = lax.axis_index("x"), lax.axis_index("y"), lax.axis_index("c")
        me, sibling = (x, y, c), (x, y, 1 - c)
        chips = [(1 - x, y), (x, 1 - y), (1 - x, 1 - y)]

        def rows(px, py, pc):
            return out_ref.at[pl.ds((4 * px + 2 * py + pc) * m_per, m_per), :]

        def copy(k, block, to, src=None):
            return pltpu.make_async_remote_copy(
                src_ref=rows(*block) if src is None else src, dst_ref=rows(*block),
                send_sem=send_sems.at[k], recv_sem=recv_sems.at[k],
                device_id=to, device_id_type=MESH)

        mine = pltpu.make_async_copy(x_ref, rows(*me), local_sem)
        mine.start()
        first = [copy(1 + j, me, (*chip, c), src=x_ref) for j, chip in enumerate(chips)]
        first += [copy(0, me, sibling, src=x_ref)]
        for cp in first:
            cp.start()
        passed = [copy(4 + j, (*chip, c), sibling) for j, chip in enumerate(chips)]
        for j, chip in enumerate(chips):
            copy(1 + j, (*chip, c), me).wait_recv()
            passed[j].start()
        copy(0, sibling, me).wait_recv()
        for j, chip in enumerate(chips):
            copy(4 + j, (*chip, 1 - c), me).wait_recv()
        for cp in first + passed:
            cp.wait_send()
        mine.wait()
        done_ref[...] = jnp.zeros_like(done_ref)

    out, done = _pcall(
        body, name=name,
        out_shape=(jax.ShapeDtypeStruct((N_DEV * m_per, n), x2d.dtype), jax.ShapeDtypeStruct((8, 128), F32)),
        in_specs=[pl.BlockSpec(memory_space=space)],
        out_specs=(pl.BlockSpec(memory_space=space), pl.BlockSpec(memory_space=pltpu.VMEM)),
        scratch_shapes=[pltpu.SemaphoreType.DMA((7,)), pltpu.SemaphoreType.DMA((7,)),
                        pltpu.SemaphoreType.DMA],
    )(x2d)
    return out, done[0, 0]


_HBM = pl.BlockSpec(memory_space=pltpu.HBM)
_SEM = pl.BlockSpec(memory_space=pltpu.SEMAPHORE)
_EFFECT = pltpu.SideEffectType.DATAFLOW_SIDE_EFFECTING


def _exchange_copies(src_refs, land_refs, send_sems, recv_sems, scatter):
    x, y, c = lax.axis_index("x"), lax.axis_index("y"), lax.axis_index("c")
    me = 4 * x + 2 * y + c
    out = []
    for i, (s_ref, l_ref) in enumerate(zip(src_refs, land_refs)):
        for k in (2, 4, 6, 3, 5, 7, 1):
            px = (1 - x) if (k >> 2) & 1 else x
            py = (1 - y) if (k >> 1) & 1 else y
            pc = (1 - c) if k & 1 else c
            out.append(pltpu.make_async_remote_copy(
                src_ref=s_ref.at[4 * px + 2 * py + pc] if scatter else s_ref, dst_ref=l_ref.at[me],
                send_sem=send_sems.at[7 * i + k - 1], recv_sem=recv_sems.at[7 * i + k - 1],
                device_id=(px, py, pc), device_id_type=MESH))
    return out


def _exchange_start(srcs, lands, scatter, name):
    n = len(srcs)

    def body(*refs):
        for cp in _exchange_copies(refs[:n], refs[n:2 * n], refs[2 * n], refs[2 * n + 1], scatter):
            cp.start()
        refs[-1][...] = jnp.zeros_like(refs[-1])

    arrays = list(srcs) + list(lands)
    outs = _pcall(
        body, name=name,
        out_shape=(pltpu.SemaphoreType.DMA((7 * n,)), pltpu.SemaphoreType.DMA((7 * n,)),
                   *[pltpu.HBM(a.shape, a.dtype) for a in arrays], jax.ShapeDtypeStruct((8, 128), F32)),
        in_specs=[_HBM] * (2 * n),
        out_specs=(_SEM, _SEM, *[_HBM] * (2 * n), pl.BlockSpec(memory_space=pltpu.VMEM)),
        input_output_aliases={i: 2 + i for i in range(2 * n)},
        compiler_params=pltpu.CompilerParams(has_side_effects=_EFFECT),
    )(*[pltpu.with_memory_space_constraint(a, pltpu.HBM) for a in arrays])
    return (outs[0], outs[1], list(outs[2:2 + n]), list(outs[2 + n:2 + 2 * n])), outs[-1][0, 0]


def _exchange_wait(send_sems, recv_sems, srcs, lands, scatter, after, name):
    n = len(srcs)
    after = list(after)

    def body(*refs):
        for cp in _exchange_copies(refs[:n], refs[n:2 * n], refs[2 * n], refs[2 * n + 1], scatter):
            cp.wait_send()
            cp.wait_recv()

    arrays = list(srcs) + list(lands)
    outs = _pcall(
        body, name=name,
        out_shape=tuple(pltpu.HBM(a.shape, a.dtype) for a in arrays),
        in_specs=[_HBM] * (2 * n) + [_SEM, _SEM] + [pl.BlockSpec(memory_space=pl.ANY)] * len(after),
        out_specs=tuple([_HBM] * (2 * n)),
        input_output_aliases={i: i for i in range(2 * n)},
        compiler_params=pltpu.CompilerParams(has_side_effects=_EFFECT),
    )(*arrays, send_sems, recv_sems, *after)
    return list(outs[n:])


def _landing(src, me, scatter, after=None):
    own = lax.dynamic_index_in_dim(src, me, 0, keepdims=True) if scatter else src[None]
    if after is not None:
        own = own + after.astype(own.dtype)
    shape = src.shape if scatter else (N_DEV,) + src.shape
    return lax.dynamic_update_slice(lax.empty(shape, src.dtype), own, (me, 0, 0))


def _sum_slots(recv, name):
    _, r, n = recv.shape
    tr = r
    for cand in (512, 448, 384, 352, 256, 128, 64, 32, 16, 8):
        if r % cand == 0:
            tr = cand
            break

    def body(r_ref, o_ref):
        acc = r_ref[0].astype(F32)
        for p in range(1, N_DEV):
            acc = acc + r_ref[p].astype(F32)
        o_ref[...] = acc

    return _pcall(
        body, name=name, grid=(r // tr,),
        out_shape=jax.ShapeDtypeStruct((r, n), F32),
        in_specs=[pl.BlockSpec((N_DEV, tr, n), lambda i: (0, i, 0))],
        out_specs=pl.BlockSpec((tr, n), lambda i: (i, 0)),
        compiler_params=_params(1),
    )(recv)


def _adamw(w, g, m, v, name):
    r, n = w.shape
    tr = r
    for cand in (512, 256, 128, 64, 32, 16, 8):
        if r % cand == 0:
            tr = cand
            break

    def body(w_ref, g_ref, m_ref, v_ref, d_ref, nm_ref, nv_ref):
        d_ref[...], nm_ref[...], nv_ref[...] = _adam_update(w_ref[...], g_ref[...], m_ref[...], v_ref[...])

    spec = pl.BlockSpec((tr, n), lambda i: (i, 0))
    shp = jax.ShapeDtypeStruct((r, n), F32)
    return _pcall(
        body, name=name, grid=(r // tr,), out_shape=(shp, shp, shp),
        in_specs=[spec] * 4, out_specs=(spec, spec, spec), compiler_params=_params(1),
    )(w, g, m, v)


def _adam_update(wv, gv, mv, vv):
    bc1 = 1.0 / (1.0 - ADAM_B1 ** ADAM_STEP)
    bc2 = 1.0 / (1.0 - ADAM_B2 ** ADAM_STEP)
    nm = ADAM_B1 * mv + (1.0 - ADAM_B1) * gv
    nv = ADAM_B2 * vv + (1.0 - ADAM_B2) * (gv * gv)
    return -ADAM_LR * ((nm * bc1) / (jnp.sqrt(nv * bc2) + ADAM_EPS) + ADAM_WD * wv), nm, nv


def _adamw_layer(w, g_layer, m, v, outs, idx, name):
    _, a, b = w.shape
    tr = a
    for cand in (512, 256, 128, 64, 32, 16, 8):
        if a % cand == 0:
            tr = cand
            break

    def body(w_ref, g_ref, m_ref, v_ref, o0, o1, o2, o3, go_ref, d_ref, nm_ref, nv_ref):
        gv = g_ref[...]
        go_ref[...] = gv
        d_ref[...], nm_ref[...], nv_ref[...] = _adam_update(w_ref[...], gv, m_ref[...], v_ref[...])

    layer = pl.BlockSpec((None, tr, b), lambda i: (idx, i, 0))
    anyw = pl.BlockSpec(memory_space=pl.ANY)
    shp = jax.ShapeDtypeStruct(w.shape, F32)
    return tuple(_pcall(
        body, name=name, grid=(a // tr,), out_shape=(shp,) * 4,
        in_specs=[layer, pl.BlockSpec((tr, b), lambda i: (i, 0)), layer, layer, anyw, anyw, anyw, anyw],
        out_specs=(layer,) * 4, input_output_aliases={4: 0, 5: 1, 6: 2, 7: 3},
        compiler_params=_params(1),
    )(w, g_layer, m, v, *outs))


def _adamw_nd(w, g, m, v, name):
    shp = w.shape
    f = lambda a: a.reshape(-1, shp[-1])
    d, nm, nv = _adamw(f(w), f(g), f(m), f(v), name)
    return d.reshape(shp), nm.reshape(shp), nv.reshape(shp)


def _ada_fwd(c_all, ada_w, ada_b_cols):
    n_l, d, cols = ada_w.shape

    def body(c_ref, w_ref, b_ref, o_ref):
        cv = c_ref[...]
        ca = cv * _sigmoid(cv)
        o_ref[...] = _mm(_bf(ca), _bf(w_ref[...])) + b_ref[...]

    return _pcall(
        body, name="ada_fwd", grid=(n_l,),
        out_shape=jax.ShapeDtypeStruct((n_l, N_DEV, cols), F32),
        in_specs=[_full((N_DEV, d)), pl.BlockSpec((None, d, cols), lambda l: (l, 0, 0)),
                  pl.BlockSpec((None, 1, cols), lambda l: (l, 0, 0))],
        out_specs=pl.BlockSpec((None, N_DEV, cols), lambda l: (l, 0, 0)),
        compiler_params=_params(1),
    )(c_all, ada_w, ada_b_cols.reshape(n_l, 1, cols))


def _ada_bwd(c_all_t, dmod_cols):
    d = c_all_t.shape[0]
    n_l, _, cols = dmod_cols.shape

    def body(ct_ref, dm_ref, o_ref):
        cv = ct_ref[...]
        ca = cv * _sigmoid(cv)
        dm = dm_ref[...]
        acc = ca[:, 0:1] * dm[0:1, :]
        for b in range(1, N_DEV):
            acc = acc + ca[:, b:b + 1] * dm[b:b + 1, :]
        o_ref[...] = acc

    return _pcall(
        body, name="ada_bwd", grid=(n_l,),
        out_shape=jax.ShapeDtypeStruct((n_l, d, cols), F32),
        in_specs=[_full((d, N_DEV)), pl.BlockSpec((None, N_DEV, cols), lambda l: (l, 0, 0))],
        out_specs=pl.BlockSpec((None, d, cols), lambda l: (l, 0, 0)),
        compiler_params=_params(1),
    )(c_all_t, dmod_cols)


def _sum_small(gathered):
    _, r, n = gathered.shape

    def body(g_ref, o_ref):
        acc = g_ref[0]
        for p in range(1, N_DEV):
            acc = acc + g_ref[p]
        o_ref[...] = acc

    return _pcall(
        body, name="sum_small", out_shape=jax.ShapeDtypeStruct((r, n), F32),
        in_specs=[_full((N_DEV, r, n))], out_specs=_full((r, n)),
    )(gathered)


def _loss_grad(xf, tgt, ts):
    s, d = xf.shape

    def body(x_ref, t_ref, dx_ref, l_ref):
        @pl.when(pl.program_id(0) == 0)
        def _():
            l_ref[...] = jnp.zeros_like(l_ref)
        e = x_ref[...] - t_ref[...]
        dx_ref[...] = e * (1.0 / d)
        l_ref[...] += (0.5 / d) * jnp.sum(jnp.sum(e * e, axis=1, keepdims=True), axis=0, keepdims=True)

    spec = pl.BlockSpec((ts, d), lambda i: (i, 0))
    return _pcall(
        body, name="loss_grad", grid=(s // ts,),
        out_shape=(jax.ShapeDtypeStruct((s, d), F32), jax.ShapeDtypeStruct((1, 1), F32)),
        in_specs=[spec, spec], out_specs=(spec, _full((1, 1))), compiler_params=_params(1),
    )(xf, tgt)


def _tn_matmul(a, col_block, b, buf, slot, name):
    s = a.shape[0]
    k = b.shape[1]
    n_p = buf.shape[2]
    mcols = N_DEV * n_p
    ts = 512 if s % 512 == 0 else 256
    nt = s // ts

    def body(a_ref, b_ref, buf_ref, o_ref, acc):
        i = pl.program_id(0)

        @pl.when(i == 0)
        def _():
            acc[...] = jnp.zeros_like(acc)
        acc[...] += _mm_tn(a_ref[...], b_ref[...])

        @pl.when(i == nt - 1)
        def _():
            o_ref[...] = acc[...].reshape(N_DEV, n_p, k).astype(BF16)

    return _pcall(
        body, name=name, grid=(nt,),
        out_shape=jax.ShapeDtypeStruct(buf.shape, BF16),
        in_specs=[pl.BlockSpec((ts, mcols), lambda i: (i, col_block)),
                  pl.BlockSpec((ts, k), lambda i: (i, 0)),
                  pl.BlockSpec(memory_space=pl.ANY)],
        out_specs=pl.BlockSpec((N_DEV, None, n_p, k), lambda i: (0, slot, 0, 0)),
        scratch_shapes=[pltpu.VMEM((mcols, k), F32)],
        input_output_aliases={2: 0},
        compiler_params=_params(1),
    )(a, b, buf)


def _wspec4(w, slot):
    _, _, n_p, k = w.shape
    return pl.BlockSpec((N_DEV, None, n_p, k), lambda i: (0, slot, 0, 0), pipeline_mode=pl.Buffered(1))


def _ffn_fwd(x1, modp, w352, l, ts):
    s, d = x1.shape
    n_l = w352.shape[1] // 3
    f_dim = N_DEV * w352.shape[2]

    def body(x_ref, mp_ref, wg_ref, wu_ref, wd_ref, x2_ref, f_ref):
        x = x_ref[...]
        _, _, h2 = _normmod_fwd(x, mp_ref[7:8, :], mp_ref[4:5, :], mp_ref[3:4, :])
        hb = _bf(h2)
        f = jnp.zeros((ts, d), F32)
        half_dev, fc = N_DEV // 2, f_dim // 2
        for part in range(2):
            dev0 = part * half_dev
            a = _mm_nt(hb, wg_ref[dev0:dev0 + half_dev].reshape(fc, d))
            b = _mm_nt(hb, wu_ref[dev0:dev0 + half_dev].reshape(fc, d))
            sv = (a * _sigmoid(a)) * b
            f = f + _mm(_bf(sv), wd_ref[dev0:dev0 + half_dev].reshape(fc, d))
        f_ref[...] = f
        x2_ref[...] = x + mp_ref[5:6, :] * f

    tile = pl.BlockSpec((ts, d), lambda i: (i, 0))
    shp = jax.ShapeDtypeStruct((s, d), F32)
    return _pcall(
        body, name="ffn_fwd", grid=(s // ts,), out_shape=(shp, shp),
        in_specs=[tile, _full(modp.shape), _wspec4(w352, l), _wspec4(w352, n_l + l),
                  _wspec4(w352, 2 * n_l + l)],
        out_specs=(tile, tile), compiler_params=_params(1),
    )(x1, modp, w352, w352, w352)


def _ffn_bwd(x1, f, dx2, modp, w352, l, ts):
    s, d = x1.shape
    n_l = w352.shape[1] // 3
    f_dim = N_DEV * w352.shape[2]

    def body(x_ref, f_ref, dx2_ref, mp_ref, wg_ref, wu_ref, wd_ref,
             dx1_ref, dab_ref, h2_ref, s_ref, df_ref, sg_ref):
        @pl.when(pl.program_id(0) == 0)
        def _():
            sg_ref[...] = jnp.zeros_like(sg_ref)
        x = x_ref[...]
        gffn, sc2, g2 = mp_ref[7:8, :], mp_ref[4:5, :], mp_ref[5:6, :]
        n, rstd, h2 = _normmod_fwd(x, gffn, sc2, mp_ref[3:4, :])
        hb = _bf(h2)
        dx2 = dx2_ref[...]
        dfb = _bf(g2 * dx2)
        dh2 = jnp.zeros((ts, d), F32)
        half_dev, fc = N_DEV // 2, f_dim // 2
        for part in range(2):
            dev0, c0 = part * half_dev, part * fc
            wg = wg_ref[dev0:dev0 + half_dev].reshape(fc, d)
            wu = wu_ref[dev0:dev0 + half_dev].reshape(fc, d)
            a = _mm_nt(hb, wg)
            b = _mm_nt(hb, wu)
            sig = _sigmoid(a)
            sa = a * sig
            s_ref[:, c0:c0 + fc] = _bf(sa * b)
            ds = _mm_nt(dfb, wd_ref[dev0:dev0 + half_dev].reshape(fc, d))
            dab = _bf(ds * b * (sig * (1.0 + a * (1.0 - sig))))
            dbb = _bf(ds * sa)
            dab_ref[:, c0:c0 + fc] = dab
            dab_ref[:, f_dim + c0:f_dim + c0 + fc] = dbb
            dh2 = dh2 + _mm(dab, wg) + _mm(dbb, wu)
        dxn, dsh, dsc, dg = _normmod_bwd(dh2, n, rstd, gffn, sc2)
        dx1_ref[...] = dx2 + dxn
        h2_ref[...] = hb
        df_ref[...] = dfb
        sg_ref[0:1, :] += dsh
        sg_ref[1:2, :] += dsc
        sg_ref[2:3, :] += _sum0(dx2 * f_ref[...])
        sg_ref[3:4, :] += dg

    tile = pl.BlockSpec((ts, d), lambda i: (i, 0))
    f32t = jax.ShapeDtypeStruct((s, d), F32)
    bft = jax.ShapeDtypeStruct((s, d), BF16)
    return _pcall(
        body, name="ffn_bwd", grid=(s // ts,),
        out_shape=(f32t, jax.ShapeDtypeStruct((s, 2 * f_dim), BF16), bft,
                   jax.ShapeDtypeStruct((s, f_dim), BF16), bft, jax.ShapeDtypeStruct((8, d), F32)),
        in_specs=[tile, tile, tile, _full(modp.shape), _wspec4(w352, l), _wspec4(w352, n_l + l),
                  _wspec4(w352, 2 * n_l + l)],
        out_specs=(tile, pl.BlockSpec((ts, 2 * f_dim), lambda i: (i, 0)), tile,
                   pl.BlockSpec((ts, f_dim), lambda i: (i, 0)), tile, _full((8, d))),
        compiler_params=_params(1),
    )(x1, f, dx2, modp, w352, w352, w352)


def _retention_consts(ts):
    h = np.arange(RET_HEADS, dtype=np.float64)
    log_g = np.log1p(-np.exp2(-5.0 - h))
    t = np.arange(ts)
    diff = t[:, None] - t[None, :]
    same = (t[:, None] // CHUNK) == (t[None, :] // CHUNK)
    later = (t[:, None] // CHUNK) > (t[None, :] // CHUNK)
    dm = np.where(same, np.abs(diff), np.where(later, diff, 0))[None] * log_g[:, None, None]
    dm = np.where((same | later)[None], np.exp(dm), 0.0)
    qd = np.exp((t[:, None] + 1.0) * log_g[None, :])
    kd = np.exp((ts - 1.0 - t[:, None]) * log_g[None, :])
    qd = np.repeat(qd, HEAD, axis=1)
    kd = np.repeat(kd, HEAD, axis=1)
    tdec = [float(np.exp(ts * lg)) for lg in log_g]
    return (jnp.asarray(dm, F32), jnp.asarray(qd, F32), jnp.asarray(kd, F32), tdec)


def _rope_tables(s):
    inv_freq = 1.0 / (ROPE_THETA ** (jnp.arange(0, HEAD, 2, dtype=F32) / HEAD))
    ang = jnp.arange(s, dtype=F32)[:, None] * inv_freq[None, :]
    cos, sin = jnp.cos(ang), jnp.sin(ang)
    return jnp.concatenate([cos, cos], axis=1), jnp.concatenate([-sin, sin], axis=1)


def _rope(v, cos, sin):
    return v * cos + pltpu.roll(v, HEAD // 2, 1) * sin


def _rope_t(dv, cos, sin):
    return dv * cos + pltpu.roll(dv * sin, HEAD // 2, 1)


def _shift_down(z, k, halo_ref):
    r = pltpu.roll(z, k, 0)
    rows = lax.broadcasted_iota(jnp.int32, z.shape, 0)
    for j in range(k):
        r = jnp.where(rows == j, halo_ref[8 - k + j:8 - k + j + 1, :], r)
    return r


def _shift_up(z, k, halo_ref):
    n = z.shape[0]
    r = pltpu.roll(z, n - k, 0)
    rows = lax.broadcasted_iota(jnp.int32, z.shape, 0)
    for j in range(k):
        r = jnp.where(rows == n - k + j, halo_ref[j:j + 1, :], r)
    return r


def _even_recompute(x, mp_ref, win, cw_ref, cos, sin, dm_ref, qd_ref, kd_ref, halo_ref, state_of):
    cd = 4 * HEAD
    n, rstd, h = _normmod_fwd(x, mp_ref[6:7, :], mp_ref[1:2, :], mp_ref[0:1, :])
    proj = _mm_nt(_bf(h), win)
    bg, cg, u = proj[:, 0:cd], proj[:, cd:2 * cd], proj[:, 2 * cd:3 * cd]
    z = cg * u
    z1 = _shift_down(z, 1, halo_ref)
    z2 = _shift_down(z, 2, halo_ref)
    conv = cw_ref[0:1, :] * z2 + cw_ref[1:2, :] * z1 + cw_ref[2:3, :] * z
    heads = []
    scale = HEAD ** -0.5
    for hh in range(RET_HEADS):
        lo = hh * HEAD
        q = proj[:, 3 * cd + lo:3 * cd + lo + HEAD]
        k = proj[:, 4 * cd + lo:4 * cd + lo + HEAD]
        v = proj[:, 5 * cd + lo:5 * cd + lo + HEAD]
        gate = proj[:, 6 * cd + lo:6 * cd + lo + HEAD]
        qr = _rope(q, cos, sin)
        kr = _rope(k, cos, sin) * scale
        sc = _mm_nt(_bf(qr), _bf(kr)) * dm_ref[hh]
        qs = qr * qd_ref[:, lo:lo + HEAD]
        ks = kr * kd_ref[:, lo:lo + HEAD]
        o = _mm(_bf(sc), _bf(v)) + _mm(_bf(qs), _bf(state_of(hh)))
        on, orstd = _rms_fwd(o)
        sig = _sigmoid(gate)
        heads.append(dict(qr=qr, kr=kr, v=v, gate=gate, sc=sc, qs=qs, ks=ks, on=on, orstd=orstd, sig=sig))
    return dict(n=n, rstd=rstd, h=h, bg=bg, cg=cg, u=u, z=z, z1=z1, z2=z2, conv=conv, heads=heads)


def _even_fwd(x, modp, w448, w128, l, cw, cos, sin, consts, ts):
    s, d = x.shape
    nt = s // ts
    dm, qd, kd, tdec = consts
    cd = 4 * HEAD
    e_in = N_DEV * w448.shape[2]

    def body(x_ref, mp_ref, win_ref, cw_ref, cos_ref, sin_ref, dm_ref, qd_ref, kd_ref, wout_ref,
             x1_ref, y_ref, st_ref, zh_ref, state, halo):
        @pl.when(pl.program_id(0) == 0)
        def _():
            state[...] = jnp.zeros_like(state)
            halo[...] = jnp.zeros_like(halo)
        xv = x_ref[...]
        st_ref[...] = state[...]
        zh_ref[...] = halo[...]
        r = _even_recompute(xv, mp_ref, win_ref[...].reshape(e_in, d), cw_ref, cos_ref[...], sin_ref[...],
                            dm_ref, qd_ref, kd_ref, halo, lambda hh: state[hh])
        halo[...] = r["z"][ts - 8:ts, :]
        parts = [r["bg"] * r["conv"]]
        for hh, hd in enumerate(r["heads"]):
            state[hh] = state[hh] * tdec[hh] + _mm_tn(_bf(hd["ks"]), _bf(hd["v"]))
            rg = cw_ref[3:4, hh * HEAD:(hh + 1) * HEAD]
            parts.append((hd["gate"] * hd["sig"]) * (hd["on"] * rg))
        mcat = jnp.concatenate(parts, axis=1)
        y = _mm(_bf(mcat), wout_ref[...].reshape(d, d))
        y_ref[...] = y
        x1_ref[...] = xv + mp_ref[2:3, :] * y

    tile = pl.BlockSpec((ts, d), lambda i: (i, 0))
    rt = pl.BlockSpec((ts, HEAD), lambda i: (i, 0))
    shp = jax.ShapeDtypeStruct((s, d), F32)
    return _pcall(
        body, name="even_fwd", grid=(nt,),
        out_shape=(shp, shp, jax.ShapeDtypeStruct((nt, RET_HEADS, HEAD, HEAD), F32),
                   jax.ShapeDtypeStruct((nt, 8, cd), F32)),
        in_specs=[tile, _full(modp.shape), _wspec4(w448, l), _full(cw.shape), rt, rt,
                  _full(dm.shape), _full(qd.shape), _full(kd.shape), _wspec4(w128, l)],
        out_specs=(tile, tile, pl.BlockSpec((None, RET_HEADS, HEAD, HEAD), lambda i: (i, 0, 0, 0)),
                   pl.BlockSpec((None, 8, cd), lambda i: (i, 0, 0))),
        scratch_shapes=[pltpu.VMEM((RET_HEADS, HEAD, HEAD), F32), pltpu.VMEM((8, cd), F32)],
        compiler_params=_params(1),
    )(x, modp, w448, cw, cos, sin, dm, qd, kd, w128)


def _even_bwd(x, dx1, y, states, zhalo, modp, w448, w128, l, cw, cos, sin, consts, ts):
    s, d = x.shape
    nt = s // ts
    dm, qd, kd, tdec = consts
    cd = 4 * HEAD
    e_in = N_DEV * w448.shape[2]
    scale = HEAD ** -0.5

    def body(x_ref, dx1_ref, y_ref, st_ref, zh_ref, mp_ref, win_ref, cw_ref, cos_ref, sin_ref,
             dm_ref, qd_ref, kd_ref, wout_ref,
             dx_ref, dproj_ref, h_ref, m_ref, dy_ref, sg_ref, gstate, halo_d):
        @pl.when(pl.program_id(0) == 0)
        def _():
            gstate[...] = jnp.zeros_like(gstate)
            halo_d[...] = jnp.zeros_like(halo_d)
            sg_ref[...] = jnp.zeros_like(sg_ref)
        xv = x_ref[...]
        cos, sin = cos_ref[...], sin_ref[...]
        win = win_ref[...].reshape(e_in, d)
        r = _even_recompute(xv, mp_ref, win, cw_ref, cos, sin, dm_ref, qd_ref, kd_ref, zh_ref,
                            lambda hh: st_ref[hh])
        parts = [r["bg"] * r["conv"]]
        for hh, hd in enumerate(r["heads"]):
            rg = cw_ref[3:4, hh * HEAD:(hh + 1) * HEAD]
            parts.append((hd["gate"] * hd["sig"]) * (hd["on"] * rg))
        m_ref[...] = _bf(jnp.concatenate(parts, axis=1))
        h_ref[...] = _bf(r["h"])

        dx1 = dx1_ref[...]
        dy = mp_ref[2:3, :] * dx1
        dyb = _bf(dy)
        dy_ref[...] = dyb
        sg_ref[2:3, :] += _sum0(dx1 * y_ref[...])
        dmix = _mm_nt(dyb, wout_ref[...].reshape(d, d))

        da_out = dmix[:, 0:cd]
        dbg = da_out * r["conv"]
        dconv = da_out * r["bg"]
        dc1 = _shift_up(dconv, 1, halo_d)
        dc2 = _shift_up(dconv, 2, halo_d)
        dz = cw_ref[2:3, :] * dconv + cw_ref[1:2, :] * dc1 + cw_ref[0:1, :] * dc2
        halo_d[...] = dconv[0:8, :]
        sg_ref[4:5, 0:cd] += _sum0(dconv * r["z2"])
        sg_ref[5:6, 0:cd] += _sum0(dconv * r["z1"])
        sg_ref[6:7, 0:cd] += _sum0(dconv * r["z"])
        dcg = dz * r["u"]
        du = dz * r["cg"]

        dqs, dks, dvs, dgs = [], [], [], []
        for hh, hd in enumerate(r["heads"]):
            lo = hh * HEAD
            rg = cw_ref[3:4, lo:lo + HEAD]
            dr = dmix[:, cd + lo:cd + lo + HEAD]
            sig, gate, on = hd["sig"], hd["gate"], hd["on"]
            rn = on * rg
            dgate = dr * rn * (sig * (1.0 + gate * (1.0 - sig)))
            drn = dr * (gate * sig)
            sg_ref[7:8, lo:lo + HEAD] += _sum0(drn * on)
            do = _rms_bwd(drn * rg, on, hd["orstd"])
            dob = _bf(do)
            gst = _bf(gstate[hh])
            scb = _bf(hd["sc"])
            vb = _bf(hd["v"])
            qrb, krb = _bf(hd["qr"]), _bf(hd["kr"])
            dv = _mm_tn(scb, dob) + _mm(_bf(hd["ks"]), gst)
            dsc = _bf(_mm_nt(dob, vb) * dm_ref[hh])
            dqr = _mm(dsc, krb) + _mm_nt(dob, _bf(st_ref[hh])) * qd_ref[:, lo:lo + HEAD]
            dkr = _mm_tn(dsc, qrb) + _mm_nt(vb, gst) * kd_ref[:, lo:lo + HEAD]
            gstate[hh] = gstate[hh] * tdec[hh] + _mm_tn(_bf(hd["qs"]), dob)
            dqs.append(_rope_t(dqr, cos, sin))
            dks.append(_rope_t(dkr * scale, cos, sin))
            dvs.append(dv)
            dgs.append(dgate)

        dproj = _bf(jnp.concatenate([dbg, dcg, du] + dqs + dks + dvs + dgs, axis=1))
        dproj_ref[...] = dproj
        dh = _mm(dproj, win)
        dxn, dsh, dsc1, dg = _normmod_bwd(dh, r["n"], r["rstd"], mp_ref[6:7, :], mp_ref[1:2, :])
        dx_ref[...] = dx1 + dxn
        sg_ref[0:1, :] += dsh
        sg_ref[1:2, :] += dsc1
        sg_ref[3:4, :] += dg

    rev = lambda i: (nt - 1 - i, 0)
    tile = pl.BlockSpec((ts, d), rev)
    rt = pl.BlockSpec((ts, HEAD), rev)
    bft = jax.ShapeDtypeStruct((s, d), BF16)
    return _pcall(
        body, name="even_bwd", grid=(nt,),
        out_shape=(jax.ShapeDtypeStruct((s, d), F32), jax.ShapeDtypeStruct((s, e_in), BF16), bft, bft, bft,
                   jax.ShapeDtypeStruct((8, d), F32)),
        in_specs=[tile, tile, tile,
                  pl.BlockSpec((None, RET_HEADS, HEAD, HEAD), lambda i: (nt - 1 - i, 0, 0, 0)),
                  pl.BlockSpec((None, 8, cd), lambda i: (nt - 1 - i, 0, 0)),
                  _full(modp.shape), _wspec4(w448, l), _full(cw.shape), rt, rt,
                  _full(dm.shape), _full(qd.shape), _full(kd.shape), _wspec4(w128, l)],
        out_specs=(tile, pl.BlockSpec((ts, e_in), rev), tile, tile, tile, _full((8, d))),
        scratch_shapes=[pltpu.VMEM((RET_HEADS, HEAD, HEAD), F32), pltpu.VMEM((8, cd), F32)],
        compiler_params=_params(1),
    )(x, dx1, y, states, zhalo, modp, w448, cw, cos, sin, dm, qd, kd, w128)


def _odd_qkv_fwd(x, modp, w384, j, qkg, ts):
    s, d = x.shape
    n3 = N_DEV * w384.shape[2]

    def body(x_ref, mp_ref, w_ref, g_ref, o_ref):
        _, _, h = _normmod_fwd(x_ref[...], mp_ref[6:7, :], mp_ref[1:2, :], mp_ref[0:1, :])
        qkv = _mm_nt(_bf(h), w_ref[...].reshape(n3, d))
        for hh in range(SB_HEADS):
            lo = hh * HEAD
            qn, _ = _rms_fwd(qkv[:, lo:lo + HEAD])
            kn, _ = _rms_fwd(qkv[:, d + lo:d + lo + HEAD])
            o_ref[:, lo:lo + HEAD] = _bf(qn * g_ref[0:1, :])
            o_ref[:, d + lo:d + lo + HEAD] = _bf(kn * g_ref[1:2, :])
        o_ref[:, 2 * d:3 * d] = _bf(qkv[:, 2 * d:3 * d])

    return _pcall(
        body, name="odd_qkv_fwd", grid=(s // ts,),
        out_shape=jax.ShapeDtypeStruct((s, n3), BF16),
        in_specs=[pl.BlockSpec((ts, d), lambda i: (i, 0)), _full(modp.shape), _wspec4(w384, j),
                  _full(qkg.shape)],
        out_specs=pl.BlockSpec((ts, n3), lambda i: (i, 0)), compiler_params=_params(1),
    )(x, modp, w384, qkg)


SB_QUERIES = 512
SB_WIDE = 256
SB_LOOP_BLOCKS = 2


def _sb_logits(q, kw, mask):
    z = _mm_nt(q, kw) * (HEAD ** -0.5)
    e = jnp.exp(-jnp.abs(z))
    lb = jnp.minimum(z, 0.0) - jnp.log(1.0 + e)
    lk = lb - z
    if mask is not None:
        lk = jnp.where(mask, lk, 0.0)
    return lb, lk


def _tri(n, above):
    ri = lax.broadcasted_iota(jnp.int32, (n, n), 0)
    ci = lax.broadcasted_iota(jnp.int32, (n, n), 1)
    return ((ri > ci) if above else (ri < ci)).astype(BF16)


def _split_dot(a, tri):
    hi = _bf(a)
    lo = _bf(a - hi.astype(F32))
    return _mm(hi, tri) + _mm(lo, tri)


def _sb_fwd(qkv, tq):
    s = qkv.shape[0]
    d = qkv.shape[1] // 3
    nq = s // tq
    assert tq % SB_WIDE == 0
    parts = tq // SB_WIDE

    def body(q_ref, k_ref, v_ref, o_ref, t_ref, o_acc, run):
        qi = pl.program_id(1)
        base = qi * tq
        upper = _tri(SB_WIDE, True)
        o_acc[...] = jnp.zeros_like(o_acc)
        run[...] = jnp.zeros_like(run)

        def wide_step(ks, row0, masked, nblk):
            rows = slice(row0, tq)
            width = nblk * SB_WIDE
            mask = None
            if masked:
                qpos = base + row0 + lax.broadcasted_iota(jnp.int32, (tq - row0, width), 0)
                mask = qpos > ks + lax.broadcasted_iota(jnp.int32, (tq - row0, width), 1)
            lb, lk = _sb_logits(q_ref[rows, :], k_ref[pl.ds(ks, width), :], mask)
            blocks = [lk[:, b * SB_WIDE:(b + 1) * SB_WIDE] for b in range(nblk)]
            right = run[rows, :]
            accs = [None] * nblk
            for b in reversed(range(nblk)):
                accs[b] = _split_dot(blocks[b], upper) + right
                right = right + jnp.sum(blocks[b], axis=1, keepdims=True)
            w = jnp.exp(lb + (accs[0] if nblk == 1 else jnp.concatenate(accs, axis=1)))
            if masked:
                w = jnp.where(mask, w, 0.0)
            o_acc[rows, :] += _mm(_bf(w), v_ref[pl.ds(ks, width), :])
            run[rows, :] = right

        for part in reversed(range(parts)):
            wide_step(pl.multiple_of(base + part * SB_WIDE, SB_WIDE), part * SB_WIDE, True, 1)
        loop_width = SB_LOOP_BLOCKS * SB_WIDE
        nsteps = qi * (tq // loop_width)

        def step(it, carry):
            wide_step(pl.multiple_of((nsteps - 1 - it) * loop_width, loop_width), 0, False, SB_LOOP_BLOCKS)
            return carry

        lax.fori_loop(0, nsteps, step, 0)
        o_ref[...] = _bf(o_acc[...])
        t_ref[...] = run[...]

    nh = d // HEAD
    return _pcall(
        body, name="sb_fwd", grid=(nh, nq),
        out_shape=(jax.ShapeDtypeStruct((s, d), BF16), jax.ShapeDtypeStruct((nh, s, 1), F32)),
        in_specs=[pl.BlockSpec((tq, HEAD), lambda h, i: (i, h)),
                  pl.BlockSpec((s, HEAD), lambda h, i: (0, nh + h)),
                  pl.BlockSpec((s, HEAD), lambda h, i: (0, 2 * nh + h))],
        out_specs=(pl.BlockSpec((tq, HEAD), lambda h, i: (i, h)),
                   pl.BlockSpec((None, tq, 1), lambda h, i: (h, i, 0))),
        scratch_shapes=[pltpu.VMEM((tq, HEAD), F32), pltpu.VMEM((tq, 1), F32)],
        compiler_params=_params(2),
    )(qkv, qkv, qkv)


def _sb_bwd(qkv, do, tot, tq):
    s = qkv.shape[0]
    d = qkv.shape[1] // 3
    nq = s // tq
    scale = HEAD ** -0.5
    assert tq % SB_WIDE == 0
    parts = tq // SB_WIDE

    def body(q_ref, k_ref, v_ref, do_ref, t_ref, dq_ref, dk_ref, dv_ref, pk, pd):
        qi = pl.program_id(1)

        @pl.when(qi == 0)
        def _():
            dk_ref[...] = jnp.zeros_like(dk_ref)
            dv_ref[...] = jnp.zeros_like(dv_ref)
        base = qi * tq
        upper = _tri(SB_WIDE, True)
        lower = _tri(SB_WIDE, False)
        dq_ref[...] = jnp.zeros_like(dq_ref)
        pk[...] = jnp.zeros_like(pk)
        pd[...] = jnp.zeros_like(pd)

        def wide_step(ks, row0, masked, nblk):
            rows = slice(row0, tq)
            width = nblk * SB_WIDE
            cut = lambda a: [a[:, b * SB_WIDE:(b + 1) * SB_WIDE] for b in range(nblk)]
            join = lambda parts_: parts_[0] if nblk == 1 else jnp.concatenate(parts_, axis=1)
            mask = None
            if masked:
                qpos = base + row0 + lax.broadcasted_iota(jnp.int32, (tq - row0, width), 0)
                mask = qpos > ks + lax.broadcasted_iota(jnp.int32, (tq - row0, width), 1)
            kw = k_ref[pl.ds(ks, width), :]
            lb, lk = _sb_logits(q_ref[rows, :], kw, mask)
            left = pk[rows, :]
            total = t_ref[rows, :]
            accs = []
            for blk in cut(lk):
                left = left + jnp.sum(blk, axis=1, keepdims=True)
                accs.append(_split_dot(blk, upper) + (total - left))
            pk[rows, :] = left
            w = jnp.exp(lb + join(accs))
            if masked:
                w = jnp.where(mask, w, 0.0)
            de = _mm_nt(do_ref[rows, :], v_ref[pl.ds(ks, width), :]) * w
            before = pd[rows, :]
            dlks = []
            for blk in cut(de):
                dlks.append(_split_dot(blk, lower) + before)
                before = before + jnp.sum(blk, axis=1, keepdims=True)
            pd[rows, :] = before
            dz = (de - jnp.exp(lb) * (de + join(dlks))) * scale
            if masked:
                dz = jnp.where(mask, dz, 0.0)
            dzb = _bf(dz)
            dq_ref[rows, :] += _mm(dzb, kw)
            dv_ref[pl.ds(ks, width), :] += _mm_tn(_bf(w), do_ref[rows, :])
            dk_ref[pl.ds(ks, width), :] += _mm_tn(dzb, q_ref[rows, :])

        loop_width = SB_LOOP_BLOCKS * SB_WIDE

        def step(jb, carry):
            wide_step(pl.multiple_of(jb * loop_width, loop_width), 0, False, SB_LOOP_BLOCKS)
            return carry

        lax.fori_loop(0, qi * (tq // loop_width), step, 0)
        for part in range(parts):
            wide_step(pl.multiple_of(base + part * SB_WIDE, SB_WIDE), part * SB_WIDE, True, 1)

    nh = d // HEAD
    shp = jax.ShapeDtypeStruct((s, d), F32)
    return _pcall(
        body, name="sb_bwd", grid=(nh, nq), out_shape=(shp, shp, shp),
        in_specs=[pl.BlockSpec((tq, HEAD), lambda h, i: (i, h)),
                  pl.BlockSpec((s, HEAD), lambda h, i: (0, nh + h)),
                  pl.BlockSpec((s, HEAD), lambda h, i: (0, 2 * nh + h)),
                  pl.BlockSpec((tq, HEAD), lambda h, i: (i, h)),
                  pl.BlockSpec((None, tq, 1), lambda h, i: (h, i, 0))],
        out_specs=(pl.BlockSpec((tq, HEAD), lambda h, i: (i, h)),
                   pl.BlockSpec((s, HEAD), lambda h, i: (0, h)),
                   pl.BlockSpec((s, HEAD), lambda h, i: (0, h))),
        scratch_shapes=[pltpu.VMEM((tq, 1), F32), pltpu.VMEM((tq, 1), F32)],
        compiler_params=_params(2),
    )(qkv, qkv, qkv, do, tot)


def _odd_out_fwd(o, x, modp, w128, slot, ts):
    s, d = x.shape

    def body(o_ref, x_ref, mp_ref, w_ref, x1_ref, y_ref):
        y = _mm(o_ref[...], w_ref[...].reshape(d, d))
        y_ref[...] = y
        x1_ref[...] = x_ref[...] + mp_ref[2:3, :] * y

    tile = pl.BlockSpec((ts, d), lambda i: (i, 0))
    shp = jax.ShapeDtypeStruct((s, d), F32)
    return _pcall(
        body, name="odd_out_fwd", grid=(s // ts,), out_shape=(shp, shp),
        in_specs=[tile, tile, _full(modp.shape), _wspec4(w128, slot)],
        out_specs=(tile, tile), compiler_params=_params(1),
    )(o, x, modp, w128)


def _odd_out_bwd(dx1, y, modp, w128, slot, ts):
    s, d = dx1.shape

    def body(dx1_ref, y_ref, mp_ref, w_ref, do_ref, dy_ref, sg_ref):
        @pl.when(pl.program_id(0) == 0)
        def _():
            sg_ref[...] = jnp.zeros_like(sg_ref)
        dx1v = dx1_ref[...]
        dyb = _bf(mp_ref[2:3, :] * dx1v)
        dy_ref[...] = dyb
        do_ref[...] = _bf(_mm_nt(dyb, w_ref[...].reshape(d, d)))
        sg_ref[2:3, :] += _sum0(dx1v * y_ref[...])

    tile = pl.BlockSpec((ts, d), lambda i: (i, 0))
    bft = jax.ShapeDtypeStruct((s, d), BF16)
    return _pcall(
        body, name="odd_out_bwd", grid=(s // ts,),
        out_shape=(bft, bft, jax.ShapeDtypeStruct((8, d), F32)),
        in_specs=[tile, tile, _full(modp.shape), _wspec4(w128, slot)],
        out_specs=(tile, tile, _full((8, d))), compiler_params=_params(1),
    )(dx1, y, modp, w128)


def _odd_qkv_bwd(x, dx1, dq, dk, dv, sg_in, modp, w384, j, qkg, ts):
    s, d = x.shape
    n3 = N_DEV * w384.shape[2]

    def body(x_ref, dx1_ref, dq_ref, dk_ref, dv_ref, sgi_ref, mp_ref, w_ref, g_ref,
             dx_ref, dqkv_ref, h_ref, sg_ref):
        @pl.when(pl.program_id(0) == 0)
        def _():
            sg_ref[...] = sgi_ref[...]
        gmix, sc1 = mp_ref[6:7, :], mp_ref[1:2, :]
        n, rstd, h = _normmod_fwd(x_ref[...], gmix, sc1, mp_ref[0:1, :])
        hb = _bf(h)
        h_ref[...] = hb
        w = w_ref[...].reshape(n3, d)
        qkv = _mm_nt(hb, w)
        parts_q, parts_k = [], []
        gq, gk = g_ref[0:1, :], g_ref[1:2, :]
        dgq = jnp.zeros((1, HEAD), F32)
        dgk = jnp.zeros((1, HEAD), F32)
        for hh in range(SB_HEADS):
            lo = hh * HEAD
            qn, qr = _rms_fwd(qkv[:, lo:lo + HEAD])
            kn, kr = _rms_fwd(qkv[:, d + lo:d + lo + HEAD])
            dqn = dq_ref[:, lo:lo + HEAD]
            dkn = dk_ref[:, lo:lo + HEAD]
            dgq = dgq + _sum0(dqn * qn)
            dgk = dgk + _sum0(dkn * kn)
            parts_q.append(_rms_bwd(dqn * gq, qn, qr))
            parts_k.append(_rms_bwd(dkn * gk, kn, kr))
        dqkv = _bf(jnp.concatenate(parts_q + parts_k + [dv_ref[...]], axis=1))
        dqkv_ref[...] = dqkv
        dh = _mm(dqkv, w)
        dxn, dsh, dsc, dg = _normmod_bwd(dh, n, rstd, gmix, sc1)
        dx_ref[...] = dx1_ref[...] + dxn
        sg_ref[0:1, :] += dsh
        sg_ref[1:2, :] += dsc
        sg_ref[3:4, :] += dg
        sg_ref[4:5, 0:HEAD] += dgq
        sg_ref[5:6, 0:HEAD] += dgk

    tile = pl.BlockSpec((ts, d), lambda i: (i, 0))
    return _pcall(
        body, name="odd_qkv_bwd", grid=(s // ts,),
        out_shape=(jax.ShapeDtypeStruct((s, d), F32), jax.ShapeDtypeStruct((s, n3), BF16),
                   jax.ShapeDtypeStruct((s, d), BF16), jax.ShapeDtypeStruct((8, d), F32)),
        in_specs=[tile, tile, tile, tile, tile, _full((8, d)), _full(modp.shape), _wspec4(w384, j),
                  _full(qkg.shape)],
        out_specs=(tile, pl.BlockSpec((ts, n3), lambda i: (i, 0)), tile, _full((8, d))),
        compiler_params=_params(1),
    )(x, dx1, dq, dk, dv, sg_in, modp, w384, qkg)


def _pad_rows(a, rows):
    return jnp.concatenate([a, jnp.zeros((rows - a.shape[0],) + a.shape[1:], a.dtype)], axis=0)


def kernel(x, c, ada_w, ada_b, norm_mix_g, norm_ffn_g, ev_w_in, ev_conv_w, ev_ret_norm_g, ev_w_out, od_w_qkv, od_q_norm_g, od_k_norm_g, od_w_out, ffn_w_gate, ffn_w_up, ffn_w_down, loss_target, m_ada_w, m_ada_b, m_norm_mix_g, m_norm_ffn_g, m_ev_w_in, m_ev_conv_w, m_ev_ret_norm_g, m_ev_w_out, m_od_w_qkv, m_od_q_norm_g, m_od_k_norm_g, m_od_w_out, m_ffn_w_gate, m_ffn_w_up, m_ffn_w_down, v_ada_w, v_ada_b, v_norm_mix_g, v_norm_ffn_g, v_ev_w_in, v_ev_conv_w, v_ev_ret_norm_g, v_ev_w_out, v_od_w_qkv, v_od_q_norm_g, v_od_k_norm_g, v_od_w_out, v_ffn_w_gate, v_ffn_w_up, v_ffn_w_down):
    me = 4 * lax.axis_index("x") + 2 * lax.axis_index("y") + lax.axis_index("c")
    xs = x[0]
    tgt = loss_target[0]
    s, d = xs.shape
    depth = ada_w.shape[0]
    n_even, n_odd = ev_w_in.shape[0], od_w_qkv.shape[0]
    ts = 256
    tq = SB_QUERIES
    cd = 4 * HEAD
    cc = ev_conv_w.shape[2]

    pack0 = jnp.zeros((8, d), F32).at[0].set(c[0]).at[1, :n_even * 3 * cc].set(ev_conv_w.reshape(-1))
    got0, _ = _all_gather(pack0, "gather_cond")
    got0 = got0.reshape(N_DEV, 8, d)
    c_all = got0[:, 0, :]
    conv_all = got0[:, 1, :n_even * 3 * cc].reshape(N_DEV, n_even, 3, cc).transpose(1, 2, 0, 3)
    conv_all = conv_all.reshape(n_even, 3, N_DEV * cc)
    cols = ada_w.shape[2]
    ada_b_cols = lax.dynamic_slice(ada_b, (0, me * cols), (depth, cols))
    mod_cols = _ada_fwd(c_all, ada_w, ada_b_cols)
    got1, cond_done = _all_gather(mod_cols.reshape(depth * N_DEV, cols), "gather_mod")
    got1 = got1.reshape(N_DEV, depth, N_DEV, cols)
    mod = lax.dynamic_index_in_dim(got1, me, axis=2, keepdims=False)
    mod = mod.transpose(1, 0, 2).reshape(depth, 6, d)
    modps = [jnp.concatenate([mod[l], norm_mix_g[l][None], norm_ffn_g[l][None]], axis=0) for l in range(depth)]

    in_flight = []
    started = cond_done
    for l in range(depth):
        j = l // 2
        plain = lambda w: _bf(w + started)
        tr = lambda w: plain(w).T
        blocks = [tr(ev_w_in[j]), plain(ev_w_out[j])] if l % 2 == 0 else [tr(od_w_qkv[j]), plain(od_w_out[j])]
        blocks.append(jnp.concatenate([tr(ffn_w_gate[l]), tr(ffn_w_up[l]), plain(ffn_w_down[l])], axis=0))
        lands = [_landing(b, me, False) for b in blocks]
        flight, started = _exchange_start(blocks, lands, False, f"gather_start_{l}")
        in_flight.append(flight)
        modps[0] = modps[0] + started
    n_ffn = ffn_w_down.shape[1]

    def layer_weights(l, after):
        got = _exchange_wait(*in_flight[l], False, [after], f"gather_wait_{l}")
        w_in = got[0].reshape(N_DEV, 1, -1, d)
        w_out = got[1].reshape(N_DEV, 1, -1, d)
        return w_in, w_out, got[2].reshape(N_DEV, 3, n_ffn, d)

    cos, sin = _rope_tables(s)
    consts = _retention_consts(ts)
    cws = [_pad_rows(jnp.concatenate([conv_all[j], ev_ret_norm_g[j][None]], axis=0), 8) for j in range(n_even)]
    qkgs = [_pad_rows(jnp.stack([od_q_norm_g[j], od_k_norm_g[j]]), 8) for j in range(n_odd)]

    saved = []
    weights = []
    cur = xs
    for l in range(depth):
        j = l // 2
        w_in, w_out, w_ffn = layer_weights(l, cur)
        weights.append((w_in, w_out, w_ffn))
        if l % 2 == 0:
            x1, y, states, zhalo = _even_fwd(cur, modps[l], w_in, w_out, 0, cws[j], cos, sin, consts, ts)
            mix = (states, zhalo)
        else:
            qkv = _odd_qkv_fwd(cur, modps[l], w_in, 0, qkgs[j], ts)
            o, tot = _sb_fwd(qkv, tq)
            x1, y = _odd_out_fwd(o, cur, modps[l], w_out, 0, ts)
            mix = (qkv, o, tot)
        x2, f = _ffn_fwd(x1, modps[l], w_ffn, 0, ts)
        saved.append((cur, x1, y, f, mix))
        cur = x2

    dx, loss_part = _loss_grad(cur, tgt, ts)
    loss = lax.psum(loss_part[0, 0], ("x", "y", "c"))

    dmod = [None] * depth
    d_gmix = [None] * depth
    d_gffn = [None] * depth
    d_conv = [None] * n_even
    d_retg = [None] * n_even
    d_qg = [None] * n_odd
    d_kg = [None] * n_odd
    grads_in_flight = [None] * depth
    for l in reversed(range(depth)):
        j = l // 2
        x0, x1, y, f, mix = saved[l]
        w_in, w_out, w_ffn = weights[l]
        g_ffn = lax.empty(w_ffn.shape, BF16)
        g_in = lax.empty(w_in.shape, BF16)
        g_out = lax.empty(w_out.shape, BF16)
        dx1, dab, h2, sv, df, sg2 = _ffn_bwd(x1, f, dx, modps[l], w_ffn, 0, ts)
        g_ffn = _tn_matmul(dab, 0, h2, g_ffn, 0, "tn_gate")
        g_ffn = _tn_matmul(dab, 1, h2, g_ffn, 1, "tn_up")
        g_ffn = _tn_matmul(sv, 0, df, g_ffn, 2, "tn_down")
        if l % 2 == 0:
            states, zhalo = mix
            dx, dproj, hb, mb, dyb, sg1 = _even_bwd(x0, dx1, y, states, zhalo, modps[l], w_in, w_out, 0,
                                                    cws[j], cos, sin, consts, ts)
            g_in = _tn_matmul(dproj, 0, hb, g_in, 0, "tn_ev_in")
            g_out = _tn_matmul(mb, 0, dyb, g_out, 0, "tn_ev_out")
            d_conv[j] = sg1[4:7, :cd]
            d_retg[j] = sg1[7, :cd]
        else:
            qkv, o, tot = mix
            do, dyb, sg0 = _odd_out_bwd(dx1, y, modps[l], w_out, 0, ts)
            dq, dk, dv = _sb_bwd(qkv, do, tot, tq)
            dx, dqkv, hb, sg1 = _odd_qkv_bwd(x0, dx1, dq, dk, dv, sg0, modps[l], w_in, 0, qkgs[j], ts)
            g_in = _tn_matmul(dqkv, 0, hb, g_in, 0, "tn_od_qkv")
            g_out = _tn_matmul(o, 0, dyb, g_out, 0, "tn_od_out")
            d_qg[j] = sg1[4, :HEAD]
            d_kg[j] = sg1[5, :HEAD]
        pieces = [g.reshape(N_DEV, -1, d) for g in (g_ffn, g_in, g_out)]
        if l > 0:
            grads_in_flight[l], started = _exchange_start(pieces, [_landing(p, me, True) for p in pieces], True,
                                                          f"grads_start_{l}")
            modps[l - 1] = modps[l - 1] + started
        dmod[l] = jnp.concatenate([sg1[0:3], sg2[0:3]], axis=0).reshape(-1)
        d_gmix[l] = sg1[3]
        d_gffn[l] = sg2[3]

    small = jnp.concatenate(
        [jnp.stack(dmod).reshape(-1), jnp.stack(d_gmix).reshape(-1), jnp.stack(d_gffn).reshape(-1),
         jnp.stack(d_retg).reshape(-1), jnp.stack(d_qg).reshape(-1), jnp.stack(d_kg).reshape(-1),
         jnp.stack(d_conv).reshape(-1)])
    n_small = small.shape[0]
    rows_small = -(-n_small // (8 * 128)) * 8
    small = jnp.concatenate([small, jnp.zeros((rows_small * 128 - n_small,), F32)]).reshape(rows_small, 128)
    got2, small_done = _all_gather(small, "gather_small")
    got2 = got2.reshape(N_DEV, rows_small, 128)
    grads_in_flight[0], started = _exchange_start(pieces, [_landing(p, me, True, small_done) for p in pieces],
                                                  True, "grads_start_0")
    tot_small = _sum_small(got2).reshape(-1)
    n_mod = depth * 6 * d
    dmod_all = got2.reshape(N_DEV, -1)[:, :n_mod].reshape(N_DEV, depth, 6 * d)
    dmod_cols = lax.dynamic_slice(dmod_all, (0, 0, me * cols), (N_DEV, depth, cols)).transpose(1, 0, 2)
    g_ada_w = _ada_bwd(c_all.T, dmod_cols + started)

    off = [0]

    def take(shape):
        n = int(np.prod(shape))
        out = tot_small[off[0]:off[0] + n].reshape(shape)
        off[0] += n
        return out

    g_ada_b = take((depth, 6 * d))
    g_norm_mix = take((depth, d))
    g_norm_ffn = take((depth, d))
    g_ret_norm = take((n_even, cd))
    g_q_norm = take((n_odd, HEAD))
    g_k_norm = take((n_odd, HEAD))
    g_conv_full = take((n_even, 3, cd))
    g_conv = lax.dynamic_slice(g_conv_full, (0, 0, me * cc), (n_even, 3, cc))

    res = {"ada_w": (g_ada_w,) + _adamw_nd(ada_w, g_ada_w, m_ada_w, v_ada_w, "adamw_ada_w")}
    big = {"ev_w_in": (ev_w_in, m_ev_w_in, v_ev_w_in), "ev_w_out": (ev_w_out, m_ev_w_out, v_ev_w_out),
           "od_w_qkv": (od_w_qkv, m_od_w_qkv, v_od_w_qkv), "od_w_out": (od_w_out, m_od_w_out, v_od_w_out),
           "ffn_w_gate": (ffn_w_gate, m_ffn_w_gate, v_ffn_w_gate), "ffn_w_up": (ffn_w_up, m_ffn_w_up, v_ffn_w_up),
           "ffn_w_down": (ffn_w_down, m_ffn_w_down, v_ffn_w_down)}
    for name, (w, _, _) in big.items():
        res[name] = tuple(lax.empty(w.shape, F32) for _ in range(4))

    def update(name, idx, g_layer):
        w, m, v = big[name]
        res[name] = _adamw_layer(w, g_layer, m, v, res[name], idx, "adamw_" + name)

    after = [res["ada_w"][1]]
    for l in reversed(range(depth)):
        recv = _exchange_wait(*grads_in_flight[l], True, after, f"grads_wait_{l}")
        s_ffn, s_in, s_out = [_sum_slots(r, f"sum_{i}") for i, r in enumerate(recv)]
        j = l // 2
        update("ffn_w_gate", l, s_ffn[0:n_ffn].T)
        update("ffn_w_up", l, s_ffn[n_ffn:2 * n_ffn].T)
        update("ffn_w_down", l, s_ffn[2 * n_ffn:3 * n_ffn])
        update("ev_w_in" if l % 2 == 0 else "od_w_qkv", j, s_in.T)
        update("ev_w_out" if l % 2 == 0 else "od_w_out", j, s_out)
        after = [res[name][1] for name in big]

    smalls = [("ada_b", ada_b, g_ada_b, m_ada_b, v_ada_b), ("norm_mix_g", norm_mix_g, g_norm_mix, m_norm_mix_g, v_norm_mix_g),
              ("norm_ffn_g", norm_ffn_g, g_norm_ffn, m_norm_ffn_g, v_norm_ffn_g),
              ("ev_conv_w", ev_conv_w, g_conv, m_ev_conv_w, v_ev_conv_w),
              ("ev_ret_norm_g", ev_ret_norm_g, g_ret_norm, m_ev_ret_norm_g, v_ev_ret_norm_g),
              ("od_q_norm_g", od_q_norm_g, g_q_norm, m_od_q_norm_g, v_od_q_norm_g),
              ("od_k_norm_g", od_k_norm_g, g_k_norm, m_od_k_norm_g, v_od_k_norm_g)]

    def pack(arrs):
        flat = jnp.concatenate([a.reshape(-1) for a in arrs])
        rows = -(-flat.shape[0] // (8 * 128)) * 8
        return jnp.concatenate([flat, jnp.zeros((rows * 128 - flat.shape[0],), F32)]).reshape(rows, 128)

    sd, sm, sv_ = _adamw(pack([t[1] for t in smalls]), pack([t[2] for t in smalls]),
                         pack([t[3] for t in smalls]), pack([t[4] for t in smalls]), "adamw_small")
    sd, sm, sv_ = sd.reshape(-1), sm.reshape(-1), sv_.reshape(-1)
    pos = 0
    for name, w, g, m, v in smalls:
        n = int(np.prod(w.shape))
        res[name] = (g, sd[pos:pos + n].reshape(w.shape), sm[pos:pos + n].reshape(w.shape),
                     sv_[pos:pos + n].reshape(w.shape))
        pos += n

    order = ["ada_w", "ada_b", "norm_mix_g", "norm_ffn_g", "ev_w_in", "ev_conv_w", "ev_ret_norm_g", "ev_w_out",
             "od_w_qkv", "od_q_norm_g", "od_k_norm_g", "od_w_out", "ffn_w_gate", "ffn_w_up", "ffn_w_down"]
    outs = [loss, dx[None]]
    for k in range(4):
        outs += [res[name][k] for name in order]
    return tuple(outs)
```

```python
import functools
import math

import numpy as np
import jax
import jax.numpy as jnp
from jax import lax
from jax.experimental import pallas as pl
from jax.experimental.pallas import tpu as pltpu

F32 = jnp.float32
BF16 = jnp.bfloat16
MESH = pl.DeviceIdType.MESH

N_DEV = 8
EPS = 1e-6
CHUNK = 64
HEAD = 128
RET_HEADS = 4
SB_HEADS = 8
ROPE_THETA = 10000.0
KEY_BLOCK = 128
ADAM_LR, ADAM_B1, ADAM_B2, ADAM_EPS, ADAM_WD, ADAM_STEP = 0.001, 0.9, 0.999, 1e-08, 0.01, 10
VMEM_LIMIT = 56 * 1024 * 1024


def _pcall(body, **kw):
    return pl.pallas_call(body, **kw)


def _params(n_grid=1, vmem=VMEM_LIMIT):
    return pltpu.CompilerParams(dimension_semantics=("arbitrary",) * n_grid, vmem_limit_bytes=vmem)


def _mm(a, b):
    return jnp.dot(a, b, preferred_element_type=F32)


def _mm_nt(a, b):
    return lax.dot_general(a, b, (((1,), (1,)), ((), ())), preferred_element_type=F32)


def _mm_tn(a, b):
    return lax.dot_general(a, b, (((0,), (0,)), ((), ())), preferred_element_type=F32)


def _bf(a):
    return a.astype(BF16)


def _sigmoid(a):
    return 1.0 / (1.0 + jnp.exp(-a))


def _sum0(a):
    return jnp.sum(a, axis=0, keepdims=True)


def _full(shape):
    nd = len(shape)
    return pl.BlockSpec(shape, lambda *_: (0,) * nd)


def _normmod_fwd(x, g, sc, sh):
    rstd = lax.rsqrt(jnp.mean(x * x, axis=-1, keepdims=True) + EPS)
    n = x * rstd
    return n, rstd, (n * g) * (1.0 + sc) + sh


def _normmod_bwd(dh, n, rstd, g, sc):
    dsh = _sum0(dh)
    dsc = _sum0(dh * (n * g))
    dg = _sum0(dh * n * (1.0 + sc))
    dn = dh * (g * (1.0 + sc))
    dx = rstd * (dn - n * jnp.mean(dn * n, axis=-1, keepdims=True))
    return dx, dsh, dsc, dg


def _rms_fwd(o):
    rstd = lax.rsqrt(jnp.mean(o * o, axis=-1, keepdims=True) + EPS)
    return o * rstd, rstd


def _rms_bwd(dn, n, rstd):
    return rstd * (dn - n * jnp.mean(dn * n, axis=-1, keepdims=True))


def _all_gather(x2d, name):
    m_per, n = x2d.shape
    space = pltpu.VMEM

    def body(x_ref, out_ref, done_ref, send_sems, recv_sems, local_sem):
        x, y, c = lax.axis_index("x"), lax.axis_index("y"), lax.axis_index("c")
        me, sibling = (x, y, c), (x, y, 1 - c)
        chips = [(1 - x, y), (x, 1 - y), (1 - x, 1 - y)]

        def rows(px, py, pc):
            return out_ref.at[pl.ds((4 * px + 2 * py + pc) * m_per, m_per), :]

        def copy(k, block, to, src=None):
            return pltpu.make_async_remote_copy(
                src_ref=rows(*block) if src is None else src, dst_ref=rows(*block),
                send_sem=send_sems.at[k], recv_sem=recv_sems.at[k],
                device_id=to, device_id_type=MESH)

        mine = pltpu.make_async_copy(x_ref, rows(*me), local_sem)
        mine.start()
        first = [copy(1 + j, me, (*chip, c), src=x_ref) for j, chip in enumerate(chips)]
        first += [copy(0, me, sibling, src=x_ref)]
        for cp in first:
            cp.start()
        passed = [copy(4 + j, (*chip, c), sibling) for j, chip in enumerate(chips)]
        for j, chip in enumerate(chips):
            copy(1 + j, (*chip, c), me).wait_recv()
            passed[j].start()
        copy(0, sibling, me).wait_recv()
        for j, chip in enumerate(chips):
            copy(4 + j, (*chip, 1 - c), me).wait_recv()
        for cp in first + passed:
            cp.wait_send()
        mine.wait()
        done_ref[...] = jnp.zeros_like(done_ref)

    out, done = _pcall(
        body, name=name,
        out_shape=(jax.ShapeDtypeStruct((N_DEV * m_per, n), x2d.dtype), jax.ShapeDtypeStruct((8, 128), F32)),
        in_specs=[pl.BlockSpec(memory_space=space)],
        out_specs=(pl.BlockSpec(memory_space=space), pl.BlockSpec(memory_space=pltpu.VMEM)),
        scratch_shapes=[pltpu.SemaphoreType.DMA((7,)), pltpu.SemaphoreType.DMA((7,)),
                        pltpu.SemaphoreType.DMA],
    )(x2d)
    return out, done[0, 0]


_HBM = pl.BlockSpec(memory_space=pltpu.HBM)
_SEM = pl.BlockSpec(memory_space=pltpu.SEMAPHORE)
_EFFECT = pltpu.SideEffectType.DATAFLOW_SIDE_EFFECTING


def _exchange_copies(src_refs, land_refs, send_sems, recv_sems, scatter):
    x, y, c = lax.axis_index("x"), lax.axis_index("y"), lax.axis_index("c")
    me = 4 * x + 2 * y + c
    out = []
    for i, (s_ref, l_ref) in enumerate(zip(src_refs, land_refs)):
        for k in (2, 4, 6, 3, 5, 7, 1):
            px = (1 - x) if (k >> 2) & 1 else x
            py = (1 - y) if (k >> 1) & 1 else y
            pc = (1 - c) if k & 1 else c
            out.append(pltpu.make_async_remote_copy(
                src_ref=s_ref.at[4 * px + 2 * py + pc] if scatter else s_ref, dst_ref=l_ref.at[me],
                send_sem=send_sems.at[7 * i + k - 1], recv_sem=recv_sems.at[7 * i + k - 1],
                device_id=(px, py, pc), device_id_type=MESH))
    return out


def _exchange_start(srcs, lands, scatter, name):
    n = len(srcs)

    def body(*refs):
        for cp in _exchange_copies(refs[:n], refs[n:2 * n], refs[2 * n], refs[2 * n + 1], scatter):
            cp.start()
        refs[-1][...] = jnp.zeros_like(refs[-1])

    arrays = list(srcs) + list(lands)
    outs = _pcall(
        body, name=name,
        out_shape=(pltpu.SemaphoreType.DMA((7 * n,)), pltpu.SemaphoreType.DMA((7 * n,)),
                   *[pltpu.HBM(a.shape, a.dtype) for a in arrays], jax.ShapeDtypeStruct((8, 128), F32)),
        in_specs=[_HBM] * (2 * n),
        out_specs=(_SEM, _SEM, *[_HBM] * (2 * n), pl.BlockSpec(memory_space=pltpu.VMEM)),
        input_output_aliases={i: 2 + i for i in range(2 * n)},
        compiler_params=pltpu.CompilerParams(has_side_effects=_EFFECT),
    )(*[pltpu.with_memory_space_constraint(a, pltpu.HBM) for a in arrays])
    return (outs[0], outs[1], list(outs[2:2 + n]), list(outs[2 + n:2 + 2 * n])), outs[-1][0, 0]


def _exchange_wait(send_sems, recv_sems, srcs, lands, scatter, after, name):
    n = len(srcs)
    after = list(after)

    def body(*refs):
        for cp in _exchange_copies(refs[:n], refs[n:2 * n], refs[2 * n], refs[2 * n + 1], scatter):
            cp.wait_send()
            cp.wait_recv()

    arrays = list(srcs) + list(lands)
    outs = _pcall(
        body, name=name,
        out_shape=tuple(pltpu.HBM(a.shape, a.dtype) for a in arrays),
        in_specs=[_HBM] * (2 * n) + [_SEM, _SEM] + [pl.BlockSpec(memory_space=pl.ANY)] * len(after),
        out_specs=tuple([_HBM] * (2 * n)),
        input_output_aliases={i: i for i in range(2 * n)},
        compiler_params=pltpu.CompilerParams(has_side_effects=_EFFECT),
    )(*arrays, send_sems, recv_sems, *after)
    return list(outs[n:])


def _landing(src, me, scatter, after=None):
    own = lax.dynamic_index_in_dim(src, me, 0, keepdims=True) if scatter else src[None]
    if after is not None:
        own = own + after.astype(own.dtype)
    shape = src.shape if scatter else (N_DEV,) + src.shape
    return lax.dynamic_update_slice(lax.empty(shape, src.dtype), own, (me, 0, 0))


def _sum_slots(recv, name):
    _, r, n = recv.shape
    tr = r
    for cand in (512, 448, 384, 352, 256, 128, 64, 32, 16, 8):
        if r % cand == 0:
            tr = cand
            break

    def body(r_ref, o_ref):
        acc = r_ref[0].astype(F32)
        for p in range(1, N_DEV):
            acc = acc + r_ref[p].astype(F32)
        o_ref[...] = acc

    return _pcall(
        body, name=name, grid=(r // tr,),
        out_shape=jax.ShapeDtypeStruct((r, n), F32),
        in_specs=[pl.BlockSpec((N_DEV, tr, n), lambda i: (0, i, 0))],
        out_specs=pl.BlockSpec((tr, n), lambda i: (i, 0)),
        compiler_params=_params(1),
    )(recv)


def _adamw(w, g, m, v, name):
    r, n = w.shape
    tr = r
    for cand in (512, 256, 128, 64, 32, 16, 8):
        if r % cand == 0:
            tr = cand
            break

    def body(w_ref, g_ref, m_ref, v_ref, d_ref, nm_ref, nv_ref):
        d_ref[...], nm_ref[...], nv_ref[...] = _adam_update(w_ref[...], g_ref[...], m_ref[...], v_ref[...])

    spec = pl.BlockSpec((tr, n), lambda i: (i, 0))
    shp = jax.ShapeDtypeStruct((r, n), F32)
    return _pcall(
        body, name=name, grid=(r // tr,), out_shape=(shp, shp, shp),
        in_specs=[spec] * 4, out_specs=(spec, spec, spec), compiler_params=_params(1),
    )(w, g, m, v)


def _adam_update(wv, gv, mv, vv):
    bc1 = 1.0 / (1.0 - ADAM_B1 ** ADAM_STEP)
    bc2 = 1.0 / (1.0 - ADAM_B2 ** ADAM_STEP)
    nm = ADAM_B1 * mv + (1.0 - ADAM_B1) * gv
    nv = ADAM_B2 * vv + (1.0 - ADAM_B2) * (gv * gv)
    return -ADAM_LR * ((nm * bc1) / (jnp.sqrt(nv * bc2) + ADAM_EPS) + ADAM_WD * wv), nm, nv


def _adamw_layer(w, g_layer, m, v, outs, idx, name):
    _, a, b = w.shape
    tr = a
    for cand in (512, 256, 128, 64, 32, 16, 8):
        if a % cand == 0:
            tr = cand
            break

    def body(w_ref, g_ref, m_ref, v_ref, o0, o1, o2, o3, go_ref, d_ref, nm_ref, nv_ref):
        gv = g_ref[...]
        go_ref[...] = gv
        d_ref[...], nm_ref[...], nv_ref[...] = _adam_update(w_ref[...], gv, m_ref[...], v_ref[...])

    layer = pl.BlockSpec((None, tr, b), lambda i: (idx, i, 0))
    anyw = pl.BlockSpec(memory_space=pl.ANY)
    shp = jax.ShapeDtypeStruct(w.shape, F32)
    return tuple(_pcall(
        body, name=name, grid=(a // tr,), out_shape=(shp,) * 4,
        in_specs=[layer, pl.BlockSpec((tr, b), lambda i: (i, 0)), layer, layer, anyw, anyw, anyw, anyw],
        out_specs=(layer,) * 4, input_output_aliases={4: 0, 5: 1, 6: 2, 7: 3},
        compiler_params=_params(1),
    )(w, g_layer, m, v, *outs))


def _adamw_nd(w, g, m, v, name):
    shp = w.shape
    f = lambda a: a.reshape(-1, shp[-1])
    d, nm, nv = _adamw(f(w), f(g), f(m), f(v), name)
    return d.reshape(shp), nm.reshape(shp), nv.reshape(shp)


def _ada_fwd(c_all, ada_w, ada_b_cols):
    n_l, d, cols = ada_w.shape

    def body(c_ref, w_ref, b_ref, o_ref):
        cv = c_ref[...]
        ca = cv * _sigmoid(cv)
        o_ref[...] = _mm(_bf(ca), _bf(w_ref[...])) + b_ref[...]

    return _pcall(
        body, name="ada_fwd", grid=(n_l,),
        out_shape=jax.ShapeDtypeStruct((n_l, N_DEV, cols), F32),
        in_specs=[_full((N_DEV, d)), pl.BlockSpec((None, d, cols), lambda l: (l, 0, 0)),
                  pl.BlockSpec((None, 1, cols), lambda l: (l, 0, 0))],
        out_specs=pl.BlockSpec((None, N_DEV, cols), lambda l: (l, 0, 0)),
        compiler_params=_params(1),
    )(c_all, ada_w, ada_b_cols.reshape(n_l, 1, cols))


def _ada_bwd(c_all_t, dmod_cols):
    d = c_all_t.shape[0]
    n_l, _, cols = dmod_cols.shape

    def body(ct_ref, dm_ref, o_ref):
        cv = ct_ref[...]
        ca = cv * _sigmoid(cv)
        dm = dm_ref[...]
        acc = ca[:, 0:1] * dm[0:1, :]
        for b in range(1, N_DEV):
            acc = acc + ca[:, b:b + 1] * dm[b:b + 1, :]
        o_ref[...] = acc

    return _pcall(
        body, name="ada_bwd", grid=(n_l,),
        out_shape=jax.ShapeDtypeStruct((n_l, d, cols), F32),
        in_specs=[_full((d, N_DEV)), pl.BlockSpec((None, N_DEV, cols), lambda l: (l, 0, 0))],
        out_specs=pl.BlockSpec((None, d, cols), lambda l: (l, 0, 0)),
        compiler_params=_params(1),
    )(c_all_t, dmod_cols)


def _sum_small(gathered):
    _, r, n = gathered.shape

    def body(g_ref, o_ref):
        acc = g_ref[0]
        for p in range(1, N_DEV):
            acc = acc + g_ref[p]
        o_ref[...] = acc

    return _pcall(
        body, name="sum_small", out_shape=jax.ShapeDtypeStruct((r, n), F32),
        in_specs=[_full((N_DEV, r, n))], out_specs=_full((r, n)),
    )(gathered)


def _loss_grad(xf, tgt, ts):
    s, d = xf.shape

    def body(x_ref, t_ref, dx_ref, l_ref):
        @pl.when(pl.program_id(0) == 0)
        def _():
            l_ref[...] = jnp.zeros_like(l_ref)
        e = x_ref[...] - t_ref[...]
        dx_ref[...] = e * (1.0 / d)
        l_ref[...] += (0.5 / d) * jnp.sum(jnp.sum(e * e, axis=1, keepdims=True), axis=0, keepdims=True)

    spec = pl.BlockSpec((ts, d), lambda i: (i, 0))
    return _pcall(
        body, name="loss_grad", grid=(s // ts,),
        out_shape=(jax.ShapeDtypeStruct((s, d), F32), jax.ShapeDtypeStruct((1, 1), F32)),
        in_specs=[spec, spec], out_specs=(spec, _full((1, 1))), compiler_params=_params(1),
    )(xf, tgt)


def _tn_matmul(a, col_block, b, buf, slot, name):
    s = a.shape[0]
    k = b.shape[1]
    n_p = buf.shape[2]
    mcols = N_DEV * n_p
    ts = 512 if s % 512 == 0 else 256
    nt = s // ts

    def body(a_ref, b_ref, buf_ref, o_ref, acc):
        i = pl.program_id(0)

        @pl.when(i == 0)
        def _():
            acc[...] = jnp.zeros_like(acc)
        acc[...] += _mm_tn(a_ref[...], b_ref[...])

        @pl.when(i == nt - 1)
        def _():
            o_ref[...] = acc[...].reshape(N_DEV, n_p, k).astype(BF16)

    return _pcall(
        body, name=name, grid=(nt,),
        out_shape=jax.ShapeDtypeStruct(buf.shape, BF16),
        in_specs=[pl.BlockSpec((ts, mcols), lambda i: (i, col_block)),
                  pl.BlockSpec((ts, k), lambda i: (i, 0)),
                  pl.BlockSpec(memory_space=pl.ANY)],
        out_specs=pl.BlockSpec((N_DEV, None, n_p, k), lambda i: (0, slot, 0, 0)),
        scratch_shapes=[pltpu.VMEM((mcols, k), F32)],
        input_output_aliases={2: 0},
        compiler_params=_params(1),
    )(a, b, buf)


def _wspec4(w, slot):
    _, _, n_p, k = w.shape
    return pl.BlockSpec((N_DEV, None, n_p, k), lambda i: (0, slot, 0, 0), pipeline_mode=pl.Buffered(1))


def _ffn_fwd(x1, modp, w352, l, ts):
    s, d = x1.shape
    n_l = w352.shape[1] // 3
    f_dim = N_DEV * w352.shape[2]

    def body(x_ref, mp_ref, wg_ref, wu_ref, wd_ref, x2_ref, f_ref, ab_ref):
        x = x_ref[...]
        _, _, h2 = _normmod_fwd(x, mp_ref[7:8, :], mp_ref[4:5, :], mp_ref[3:4, :])
        hb = _bf(h2)
        f = jnp.zeros((ts, d), F32)
        half_dev, fc = N_DEV // 2, f_dim // 2
        for part in range(2):
            dev0, c0 = part * half_dev, part * fc
            a = _mm_nt(hb, wg_ref[dev0:dev0 + half_dev].reshape(fc, d))
            b = _mm_nt(hb, wu_ref[dev0:dev0 + half_dev].reshape(fc, d))
            ab_ref[:, c0:c0 + fc] = a
            ab_ref[:, f_dim + c0:f_dim + c0 + fc] = b
            sv = (a * _sigmoid(a)) * b
            f = f + _mm(_bf(sv), wd_ref[dev0:dev0 + half_dev].reshape(fc, d))
        f_ref[...] = f
        x2_ref[...] = x + mp_ref[5:6, :] * f

    tile = pl.BlockSpec((ts, d), lambda i: (i, 0))
    shp = jax.ShapeDtypeStruct((s, d), F32)
    return _pcall(
        body, name="ffn_fwd", grid=(s // ts,),
        out_shape=(shp, shp, jax.ShapeDtypeStruct((s, 2 * f_dim), F32)),
        in_specs=[tile, _full(modp.shape), _wspec4(w352, l), _wspec4(w352, n_l + l),
                  _wspec4(w352, 2 * n_l + l)],
        out_specs=(tile, tile, pl.BlockSpec((ts, 2 * f_dim), lambda i: (i, 0))), compiler_params=_params(1),
    )(x1, modp, w352, w352, w352)


def _ffn_bwd(x1, f, ab, dx2, modp, w352, l, ts):
    s, d = x1.shape
    n_l = w352.shape[1] // 3
    f_dim = N_DEV * w352.shape[2]

    def body(x_ref, f_ref, ab_ref, dx2_ref, mp_ref, wg_ref, wu_ref, wd_ref,
             dx1_ref, dab_ref, h2_ref, s_ref, df_ref, sg_ref):
        @pl.when(pl.program_id(0) == 0)
        def _():
            sg_ref[...] = jnp.zeros_like(sg_ref)
        x = x_ref[...]
        gffn, sc2, g2 = mp_ref[7:8, :], mp_ref[4:5, :], mp_ref[5:6, :]
        n, rstd, h2 = _normmod_fwd(x, gffn, sc2, mp_ref[3:4, :])
        hb = _bf(h2)
        dx2 = dx2_ref[...]
        dfb = _bf(g2 * dx2)
        dh2 = jnp.zeros((ts, d), F32)
        half_dev, fc = N_DEV // 2, f_dim // 2
        for part in range(2):
            dev0, c0 = part * half_dev, part * fc
            wg = wg_ref[dev0:dev0 + half_dev].reshape(fc, d)
            wu = wu_ref[dev0:dev0 + half_dev].reshape(fc, d)
            a = ab_ref[:, c0:c0 + fc]
            b = ab_ref[:, f_dim + c0:f_dim + c0 + fc]
            sig = _sigmoid(a)
            sa = a * sig
            s_ref[:, c0:c0 + fc] = _bf(sa * b)
            ds = _mm_nt(dfb, wd_ref[dev0:dev0 + half_dev].reshape(fc, d))
            dab = _bf(ds * b * (sig * (1.0 + a * (1.0 - sig))))
            dbb = _bf(ds * sa)
            dab_ref[:, c0:c0 + fc] = dab
            dab_ref[:, f_dim + c0:f_dim + c0 + fc] = dbb
            dh2 = dh2 + _mm(dab, wg) + _mm(dbb, wu)
        dxn, dsh, dsc, dg = _normmod_bwd(dh2, n, rstd, gffn, sc2)
        dx1_ref[...] = dx2 + dxn
        h2_ref[...] = hb
        df_ref[...] = dfb
        sg_ref[0:1, :] += dsh
        sg_ref[1:2, :] += dsc
        sg_ref[2:3, :] += _sum0(dx2 * f_ref[...])
        sg_ref[3:4, :] += dg

    tile = pl.BlockSpec((ts, d), lambda i: (i, 0))
    f32t = jax.ShapeDtypeStruct((s, d), F32)
    bft = jax.ShapeDtypeStruct((s, d), BF16)
    return _pcall(
        body, name="ffn_bwd", grid=(s // ts,),
        out_shape=(f32t, jax.ShapeDtypeStruct((s, 2 * f_dim), BF16), bft,
                   jax.ShapeDtypeStruct((s, f_dim), BF16), bft, jax.ShapeDtypeStruct((8, d), F32)),
        in_specs=[tile, tile, pl.BlockSpec((ts, 2 * f_dim), lambda i: (i, 0)), tile, _full(modp.shape),
                  _wspec4(w352, l), _wspec4(w352, n_l + l), _wspec4(w352, 2 * n_l + l)],
        out_specs=(tile, pl.BlockSpec((ts, 2 * f_dim), lambda i: (i, 0)), tile,
                   pl.BlockSpec((ts, f_dim), lambda i: (i, 0)), tile, _full((8, d))),
        compiler_params=_params(1),
    )(x1, f, ab, dx2, modp, w352, w352, w352)


def _retention_consts(ts):
    h = np.arange(RET_HEADS, dtype=np.float64)
    log_g = np.log1p(-np.exp2(-5.0 - h))
    t = np.arange(ts)
    diff = t[:, None] - t[None, :]
    same = (t[:, None] // CHUNK) == (t[None, :] // CHUNK)
    later = (t[:, None] // CHUNK) > (t[None, :] // CHUNK)
    dm = np.where(same, np.abs(diff), np.where(later, diff, 0))[None] * log_g[:, None, None]
    dm = np.where((same | later)[None], np.exp(dm), 0.0)
    qd = np.exp((t[:, None] + 1.0) * log_g[None, :])
    kd = np.exp((ts - 1.0 - t[:, None]) * log_g[None, :])
    qd = np.repeat(qd, HEAD, axis=1)
    kd = np.repeat(kd, HEAD, axis=1)
    tdec = [float(np.exp(ts * lg)) for lg in log_g]
    return (jnp.asarray(dm, F32), jnp.asarray(qd, F32), jnp.asarray(kd, F32), tdec)


def _rope_tables(s):
    inv_freq = 1.0 / (ROPE_THETA ** (jnp.arange(0, HEAD, 2, dtype=F32) / HEAD))
    ang = jnp.arange(s, dtype=F32)[:, None] * inv_freq[None, :]
    cos, sin = jnp.cos(ang), jnp.sin(ang)
    return jnp.concatenate([cos, cos], axis=1), jnp.concatenate([-sin, sin], axis=1)


def _rope(v, cos, sin):
    return v * cos + pltpu.roll(v, HEAD // 2, 1) * sin


def _rope_t(dv, cos, sin):
    return dv * cos + pltpu.roll(dv * sin, HEAD // 2, 1)


def _shift_down(z, k, halo_ref):
    r = pltpu.roll(z, k, 0)
    rows = lax.broadcasted_iota(jnp.int32, z.shape, 0)
    for j in range(k):
        r = jnp.where(rows == j, halo_ref[8 - k + j:8 - k + j + 1, :], r)
    return r


def _shift_up(z, k, halo_ref):
    n = z.shape[0]
    r = pltpu.roll(z, n - k, 0)
    rows = lax.broadcasted_iota(jnp.int32, z.shape, 0)
    for j in range(k):
        r = jnp.where(rows == n - k + j, halo_ref[j:j + 1, :], r)
    return r


def _even_recompute(x, mp_ref, win, cw_ref, cos, sin, dm_ref, qd_ref, kd_ref, halo_ref, state_of, proj=None):
    cd = 4 * HEAD
    n, rstd, h = _normmod_fwd(x, mp_ref[6:7, :], mp_ref[1:2, :], mp_ref[0:1, :])
    if proj is None:
        proj = _mm_nt(_bf(h), win)
    bg, cg, u = proj[:, 0:cd], proj[:, cd:2 * cd], proj[:, 2 * cd:3 * cd]
    z = cg * u
    z1 = _shift_down(z, 1, halo_ref)
    z2 = _shift_down(z, 2, halo_ref)
    conv = cw_ref[0:1, :] * z2 + cw_ref[1:2, :] * z1 + cw_ref[2:3, :] * z
    heads = []
    scale = HEAD ** -0.5
    for hh in range(RET_HEADS):
        lo = hh * HEAD
        q = proj[:, 3 * cd + lo:3 * cd + lo + HEAD]
        k = proj[:, 4 * cd + lo:4 * cd + lo + HEAD]
        v = proj[:, 5 * cd + lo:5 * cd + lo + HEAD]
        gate = proj[:, 6 * cd + lo:6 * cd + lo + HEAD]
        qr = _rope(q, cos, sin)
        kr = _rope(k, cos, sin) * scale
        sc = _mm_nt(_bf(qr), _bf(kr)) * dm_ref[hh]
        qs = qr * qd_ref[:, lo:lo + HEAD]
        ks = kr * kd_ref[:, lo:lo + HEAD]
        o = _mm(_bf(sc), _bf(v)) + _mm(_bf(qs), _bf(state_of(hh)))
        on, orstd = _rms_fwd(o)
        sig = _sigmoid(gate)
        heads.append(dict(qr=qr, kr=kr, v=v, gate=gate, sc=sc, qs=qs, ks=ks, on=on, orstd=orstd, sig=sig))
    return dict(n=n, rstd=rstd, h=h, proj=proj, bg=bg, cg=cg, u=u, z=z, z1=z1, z2=z2, conv=conv, heads=heads)


def _even_fwd(x, modp, w448, w128, l, cw, cos, sin, consts, ts):
    s, d = x.shape
    nt = s // ts
    dm, qd, kd, tdec = consts
    cd = 4 * HEAD
    e_in = N_DEV * w448.shape[2]

    def body(x_ref, mp_ref, win_ref, cw_ref, cos_ref, sin_ref, dm_ref, qd_ref, kd_ref, wout_ref,
             x1_ref, y_ref, st_ref, zh_ref, proj_ref, state, halo):
        @pl.when(pl.program_id(0) == 0)
        def _():
            state[...] = jnp.zeros_like(state)
            halo[...] = jnp.zeros_like(halo)
        xv = x_ref[...]
        st_ref[...] = state[...]
        zh_ref[...] = halo[...]
        r = _even_recompute(xv, mp_ref, win_ref[...].reshape(e_in, d), cw_ref, cos_ref[...], sin_ref[...],
                            dm_ref, qd_ref, kd_ref, halo, lambda hh: state[hh])
        proj_ref[...] = r["proj"]
        halo[...] = r["z"][ts - 8:ts, :]
        parts = [r["bg"] * r["conv"]]
        for hh, hd in enumerate(r["heads"]):
            state[hh] = state[hh] * tdec[hh] + _mm_tn(_bf(hd["ks"]), _bf(hd["v"]))
            rg = cw_ref[3:4, hh * HEAD:(hh + 1) * HEAD]
            parts.append((hd["gate"] * hd["sig"]) * (hd["on"] * rg))
        mcat = jnp.concatenate(parts, axis=1)
        y = _mm(_bf(mcat), wout_ref[...].reshape(d, d))
        y_ref[...] = y
        x1_ref[...] = xv + mp_ref[2:3, :] * y

    tile = pl.BlockSpec((ts, d), lambda i: (i, 0))
    rt = pl.BlockSpec((ts, HEAD), lambda i: (i, 0))
    shp = jax.ShapeDtypeStruct((s, d), F32)
    return _pcall(
        body, name="even_fwd", grid=(nt,),
        out_shape=(shp, shp, jax.ShapeDtypeStruct((nt, RET_HEADS, HEAD, HEAD), F32),
                   jax.ShapeDtypeStruct((nt, 8, cd), F32), jax.ShapeDtypeStruct((s, e_in), F32)),
        in_specs=[tile, _full(modp.shape), _wspec4(w448, l), _full(cw.shape), rt, rt,
                  _full(dm.shape), _full(qd.shape), _full(kd.shape), _wspec4(w128, l)],
        out_specs=(tile, tile, pl.BlockSpec((None, RET_HEADS, HEAD, HEAD), lambda i: (i, 0, 0, 0)),
                   pl.BlockSpec((None, 8, cd), lambda i: (i, 0, 0)), pl.BlockSpec((ts, e_in), lambda i: (i, 0))),
        scratch_shapes=[pltpu.VMEM((RET_HEADS, HEAD, HEAD), F32), pltpu.VMEM((8, cd), F32)],
        compiler_params=_params(1),
    )(x, modp, w448, cw, cos, sin, dm, qd, kd, w128)


def _even_bwd(x, dx1, y, states, zhalo, proj, modp, w448, w128, l, cw, cos, sin, consts, ts):
    s, d = x.shape
    nt = s // ts
    dm, qd, kd, tdec = consts
    cd = 4 * HEAD
    e_in = N_DEV * w448.shape[2]
    scale = HEAD ** -0.5

    def body(x_ref, dx1_ref, y_ref, st_ref, zh_ref, proj_ref, mp_ref, win_ref, cw_ref, cos_ref, sin_ref,
             dm_ref, qd_ref, kd_ref, wout_ref,
             dx_ref, dproj_ref, h_ref, m_ref, dy_ref, sg_ref, gstate, halo_d):
        @pl.when(pl.program_id(0) == 0)
        def _():
            gstate[...] = jnp.zeros_like(gstate)
            halo_d[...] = jnp.zeros_like(halo_d)
            sg_ref[...] = jnp.zeros_like(sg_ref)
        xv = x_ref[...]
        cos, sin = cos_ref[...], sin_ref[...]
        win = win_ref[...].reshape(e_in, d)
        r = _even_recompute(xv, mp_ref, win, cw_ref, cos, sin, dm_ref, qd_ref, kd_ref, zh_ref,
                            lambda hh: st_ref[hh], proj_ref[...])
        parts = [r["bg"] * r["conv"]]
        for hh, hd in enumerate(r["heads"]):
            rg = cw_ref[3:4, hh * HEAD:(hh + 1) * HEAD]
            parts.append((hd["gate"] * hd["sig"]) * (hd["on"] * rg))
        m_ref[...] = _bf(jnp.concatenate(parts, axis=1))
        h_ref[...] = _bf(r["h"])

        dx1 = dx1_ref[...]
        dy = mp_ref[2:3, :] * dx1
        dyb = _bf(dy)
        dy_ref[...] = dyb
        sg_ref[2:3, :] += _sum0(dx1 * y_ref[...])
        dmix = _mm_nt(dyb, wout_ref[...].reshape(d, d))

        da_out = dmix[:, 0:cd]
        dbg = da_out * r["conv"]
        dconv = da_out * r["bg"]
        dc1 = _shift_up(dconv, 1, halo_d)
        dc2 = _shift_up(dconv, 2, halo_d)
        dz = cw_ref[2:3, :] * dconv + cw_ref[1:2, :] * dc1 + cw_ref[0:1, :] * dc2
        halo_d[...] = dconv[0:8, :]
        sg_ref[4:5, 0:cd] += _sum0(dconv * r["z2"])
        sg_ref[5:6, 0:cd] += _sum0(dconv * r["z1"])
        sg_ref[6:7, 0:cd] += _sum0(dconv * r["z"])
        dcg = dz * r["u"]
        du = dz * r["cg"]

        dqs, dks, dvs, dgs = [], [], [], []
        for hh, hd in enumerate(r["heads"]):
            lo = hh * HEAD
            rg = cw_ref[3:4, lo:lo + HEAD]
            dr = dmix[:, cd + lo:cd + lo + HEAD]
            sig, gate, on = hd["sig"], hd["gate"], hd["on"]
            rn = on * rg
            dgate = dr * rn * (sig * (1.0 + gate * (1.0 - sig)))
            drn = dr * (gate * sig)
            sg_ref[7:8, lo:lo + HEAD] += _sum0(drn * on)
            do = _rms_bwd(drn * rg, on, hd["orstd"])
            dob = _bf(do)
            gst = _bf(gstate[hh])
            scb = _bf(hd["sc"])
            vb = _bf(hd["v"])
            qrb, krb = _bf(hd["qr"]), _bf(hd["kr"])
            dv = _mm_tn(scb, dob) + _mm(_bf(hd["ks"]), gst)
            dsc = _bf(_mm_nt(dob, vb) * dm_ref[hh])
            dqr = _mm(dsc, krb) + _mm_nt(dob, _bf(st_ref[hh])) * qd_ref[:, lo:lo + HEAD]
            dkr = _mm_tn(dsc, qrb) + _mm_nt(vb, gst) * kd_ref[:, lo:lo + HEAD]
            gstate[hh] = gstate[hh] * tdec[hh] + _mm_tn(_bf(hd["qs"]), dob)
            dqs.append(_rope_t(dqr, cos, sin))
            dks.append(_rope_t(dkr * scale, cos, sin))
            dvs.append(dv)
            dgs.append(dgate)

        dproj = _bf(jnp.concatenate([dbg, dcg, du] + dqs + dks + dvs + dgs, axis=1))
        dproj_ref[...] = dproj
        dh = _mm(dproj, win)
        dxn, dsh, dsc1, dg = _normmod_bwd(dh, r["n"], r["rstd"], mp_ref[6:7, :], mp_ref[1:2, :])
        dx_ref[...] = dx1 + dxn
        sg_ref[0:1, :] += dsh
        sg_ref[1:2, :] += dsc1
        sg_ref[3:4, :] += dg

    rev = lambda i: (nt - 1 - i, 0)
    tile = pl.BlockSpec((ts, d), rev)
    rt = pl.BlockSpec((ts, HEAD), rev)
    bft = jax.ShapeDtypeStruct((s, d), BF16)
    return _pcall(
        body, name="even_bwd", grid=(nt,),
        out_shape=(jax.ShapeDtypeStruct((s, d), F32), jax.ShapeDtypeStruct((s, e_in), BF16), bft, bft, bft,
                   jax.ShapeDtypeStruct((8, d), F32)),
        in_specs=[tile, tile, tile,
                  pl.BlockSpec((None, RET_HEADS, HEAD, HEAD), lambda i: (nt - 1 - i, 0, 0, 0)),
                  pl.BlockSpec((None, 8, cd), lambda i: (nt - 1 - i, 0, 0)), pl.BlockSpec((ts, e_in), rev),
                  _full(modp.shape), _wspec4(w448, l), _full(cw.shape), rt, rt,
                  _full(dm.shape), _full(qd.shape), _full(kd.shape), _wspec4(w128, l)],
        out_specs=(tile, pl.BlockSpec((ts, e_in), rev), tile, tile, tile, _full((8, d))),
        scratch_shapes=[pltpu.VMEM((RET_HEADS, HEAD, HEAD), F32), pltpu.VMEM((8, cd), F32)],
        compiler_params=_params(1),
    )(x, dx1, y, states, zhalo, proj, modp, w448, cw, cos, sin, dm, qd, kd, w128)


def _odd_qkv_fwd(x, modp, w384, j, qkg, ts):
    s, d = x.shape
    n3 = N_DEV * w384.shape[2]

    def body(x_ref, mp_ref, w_ref, g_ref, o_ref):
        _, _, h = _normmod_fwd(x_ref[...], mp_ref[6:7, :], mp_ref[1:2, :], mp_ref[0:1, :])
        qkv = _mm_nt(_bf(h), w_ref[...].reshape(n3, d))
        for hh in range(SB_HEADS):
            lo = hh * HEAD
            qn, _ = _rms_fwd(qkv[:, lo:lo + HEAD])
            kn, _ = _rms_fwd(qkv[:, d + lo:d + lo + HEAD])
            o_ref[:, lo:lo + HEAD] = _bf(qn * g_ref[0:1, :])
            o_ref[:, d + lo:d + lo + HEAD] = _bf(kn * g_ref[1:2, :])
        o_ref[:, 2 * d:3 * d] = _bf(qkv[:, 2 * d:3 * d])

    return _pcall(
        body, name="odd_qkv_fwd", grid=(s // ts,),
        out_shape=jax.ShapeDtypeStruct((s, n3), BF16),
        in_specs=[pl.BlockSpec((ts, d), lambda i: (i, 0)), _full(modp.shape), _wspec4(w384, j),
                  _full(qkg.shape)],
        out_specs=pl.BlockSpec((ts, n3), lambda i: (i, 0)), compiler_params=_params(1),
    )(x, modp, w384, qkg)


SB_QUERIES = 512
SB_WIDE = 256
SB_LOOP_BLOCKS = 2


def _sb_logits(q, kw, mask):
    z = _mm_nt(q, kw) * (HEAD ** -0.5)
    e = jnp.exp(-jnp.abs(z))
    lb = jnp.minimum(z, 0.0) - jnp.log(1.0 + e)
    lk = lb - z
    if mask is not None:
        lk = jnp.where(mask, lk, 0.0)
    return lb, lk


def _tri(n, above):
    ri = lax.broadcasted_iota(jnp.int32, (n, n), 0)
    ci = lax.broadcasted_iota(jnp.int32, (n, n), 1)
    return ((ri > ci) if above else (ri < ci)).astype(BF16)


def _split_dot(a, tri):
    hi = _bf(a)
    lo = _bf(a - hi.astype(F32))
    return _mm(hi, tri) + _mm(lo, tri)


def _sb_fwd(qkv, tq):
    s = qkv.shape[0]
    d = qkv.shape[1] // 3
    nq = s // tq
    assert tq % SB_WIDE == 0
    parts = tq // SB_WIDE

    def body(q_ref, k_ref, v_ref, o_ref, t_ref, o_acc, run):
        qi = pl.program_id(1)
        base = qi * tq
        upper = _tri(SB_WIDE, True)
        o_acc[...] = jnp.zeros_like(o_acc)
        run[...] = jnp.zeros_like(run)

        def wide_step(ks, row0, masked, nblk):
            rows = slice(row0, tq)
            width = nblk * SB_WIDE
            mask = None
            if masked:
                qpos = base + row0 + lax.broadcasted_iota(jnp.int32, (tq - row0, width), 0)
                mask = qpos > ks + lax.broadcasted_iota(jnp.int32, (tq - row0, width), 1)
            lb, lk = _sb_logits(q_ref[rows, :], k_ref[pl.ds(ks, width), :], mask)
            blocks = [lk[:, b * SB_WIDE:(b + 1) * SB_WIDE] for b in range(nblk)]
            right = run[rows, :]
            accs = [None] * nblk
            for b in reversed(range(nblk)):
                accs[b] = _split_dot(blocks[b], upper) + right
                right = right + jnp.sum(blocks[b], axis=1, keepdims=True)
            w = jnp.exp(lb + (accs[0] if nblk == 1 else jnp.concatenate(accs, axis=1)))
            if masked:
                w = jnp.where(mask, w, 0.0)
            o_acc[rows, :] += _mm(_bf(w), v_ref[pl.ds(ks, width), :])
            run[rows, :] = right

        for part in reversed(range(parts)):
            wide_step(pl.multiple_of(base + part * SB_WIDE, SB_WIDE), part * SB_WIDE, True, 1)
        loop_width = SB_LOOP_BLOCKS * SB_WIDE
        nsteps = qi * (tq // loop_width)

        def step(it, carry):
            wide_step(pl.multiple_of((nsteps - 1 - it) * loop_width, loop_width), 0, False, SB_LOOP_BLOCKS)
            return carry

        lax.fori_loop(0, nsteps, step, 0)
        o_ref[...] = _bf(o_acc[...])
        t_ref[...] = run[...]

    nh = d // HEAD
    return _pcall(
        body, name="sb_fwd", grid=(nh, nq),
        out_shape=(jax.ShapeDtypeStruct((s, d), BF16), jax.ShapeDtypeStruct((nh, s, 1), F32)),
        in_specs=[pl.BlockSpec((tq, HEAD), lambda h, i: (i, h)),
                  pl.BlockSpec((s, HEAD), lambda h, i: (0, nh + h)),
                  pl.BlockSpec((s, HEAD), lambda h, i: (0, 2 * nh + h))],
        out_specs=(pl.BlockSpec((tq, HEAD), lambda h, i: (i, h)),
                   pl.BlockSpec((None, tq, 1), lambda h, i: (h, i, 0))),
        scratch_shapes=[pltpu.VMEM((tq, HEAD), F32), pltpu.VMEM((tq, 1), F32)],
        compiler_params=_params(2),
    )(qkv, qkv, qkv)


def _sb_bwd(qkv, do, tot, tq):
    s = qkv.shape[0]
    d = qkv.shape[1] // 3
    nq = s // tq
    scale = HEAD ** -0.5
    assert tq % SB_WIDE == 0
    parts = tq // SB_WIDE

    def body(q_ref, k_ref, v_ref, do_ref, t_ref, dq_ref, dk_ref, dv_ref, pk, pd):
        qi = pl.program_id(1)

        @pl.when(qi == 0)
        def _():
            dk_ref[...] = jnp.zeros_like(dk_ref)
            dv_ref[...] = jnp.zeros_like(dv_ref)
        base = qi * tq
        upper = _tri(SB_WIDE, True)
        lower = _tri(SB_WIDE, False)
        dq_ref[...] = jnp.zeros_like(dq_ref)
        pk[...] = jnp.zeros_like(pk)
        pd[...] = jnp.zeros_like(pd)

        def wide_step(ks, row0, masked, nblk):
            rows = slice(row0, tq)
            width = nblk * SB_WIDE
            cut = lambda a: [a[:, b * SB_WIDE:(b + 1) * SB_WIDE] for b in range(nblk)]
            join = lambda parts_: parts_[0] if nblk == 1 else jnp.concatenate(parts_, axis=1)
            mask = None
            if masked:
                qpos = base + row0 + lax.broadcasted_iota(jnp.int32, (tq - row0, width), 0)
                mask = qpos > ks + lax.broadcasted_iota(jnp.int32, (tq - row0, width), 1)
            kw = k_ref[pl.ds(ks, width), :]
            lb, lk = _sb_logits(q_ref[rows, :], kw, mask)
            left = pk[rows, :]
            total = t_ref[rows, :]
            accs = []
            for blk in cut(lk):
                left = left + jnp.sum(blk, axis=1, keepdims=True)
                accs.append(_split_dot(blk, upper) + (total - left))
            pk[rows, :] = left
            w = jnp.exp(lb + join(accs))
            if masked:
                w = jnp.where(mask, w, 0.0)
            de = _mm_nt(do_ref[rows, :], v_ref[pl.ds(ks, width), :]) * w
            before = pd[rows, :]
            dlks = []
            for blk in cut(de):
                dlks.append(_split_dot(blk, lower) + before)
                before = before + jnp.sum(blk, axis=1, keepdims=True)
            pd[rows, :] = before
            dz = (de - jnp.exp(lb) * (de + join(dlks))) * scale
            if masked:
                dz = jnp.where(mask, dz, 0.0)
            dzb = _bf(dz)
            dq_ref[rows, :] += _mm(dzb, kw)
            dv_ref[pl.ds(ks, width), :] += _mm_tn(_bf(w), do_ref[rows, :])
            dk_ref[pl.ds(ks, width), :] += _mm_tn(dzb, q_ref[rows, :])

        loop_width = SB_LOOP_BLOCKS * SB_WIDE

        def step(jb, carry):
            wide_step(pl.multiple_of(jb * loop_width, loop_width), 0, False, SB_LOOP_BLOCKS)
            return carry

        lax.fori_loop(0, qi * (tq // loop_width), step, 0)
        for part in range(parts):
            wide_step(pl.multiple_of(base + part * SB_WIDE, SB_WIDE), part * SB_WIDE, True, 1)

    nh = d // HEAD
    shp = jax.ShapeDtypeStruct((s, d), F32)
    return _pcall(
        body, name="sb_bwd", grid=(nh, nq), out_shape=(shp, shp, shp),
        in_specs=[pl.BlockSpec((tq, HEAD), lambda h, i: (i, h)),
                  pl.BlockSpec((s, HEAD), lambda h, i: (0, nh + h)),
                  pl.BlockSpec((s, HEAD), lambda h, i: (0, 2 * nh + h)),
                  pl.BlockSpec((tq, HEAD), lambda h, i: (i, h)),
                  pl.BlockSpec((None, tq, 1), lambda h, i: (h, i, 0))],
        out_specs=(pl.BlockSpec((tq, HEAD), lambda h, i: (i, h)),
                   pl.BlockSpec((s, HEAD), lambda h, i: (0, h)),
                   pl.BlockSpec((s, HEAD), lambda h, i: (0, h))),
        scratch_shapes=[pltpu.VMEM((tq, 1), F32), pltpu.VMEM((tq, 1), F32)],
        compiler_params=_params(2),
    )(qkv, qkv, qkv, do, tot)


def _odd_out_fwd(o, x, modp, w128, slot, ts):
    s, d = x.shape

    def body(o_ref, x_ref, mp_ref, w_ref, x1_ref, y_ref):
        y = _mm(o_ref[...], w_ref[...].reshape(d, d))
        y_ref[...] = y
        x1_ref[...] = x_ref[...] + mp_ref[2:3, :] * y

    tile = pl.BlockSpec((ts, d), lambda i: (i, 0))
    shp = jax.ShapeDtypeStruct((s, d), F32)
    return _pcall(
        body, name="odd_out_fwd", grid=(s // ts,), out_shape=(shp, shp),
        in_specs=[tile, tile, _full(modp.shape), _wspec4(w128, slot)],
        out_specs=(tile, tile), compiler_params=_params(1),
    )(o, x, modp, w128)


def _odd_out_bwd(dx1, y, modp, w128, slot, ts):
    s, d = dx1.shape

    def body(dx1_ref, y_ref, mp_ref, w_ref, do_ref, dy_ref, sg_ref):
        @pl.when(pl.program_id(0) == 0)
        def _():
            sg_ref[...] = jnp.zeros_like(sg_ref)
        dx1v = dx1_ref[...]
        dyb = _bf(mp_ref[2:3, :] * dx1v)
        dy_ref[...] = dyb
        do_ref[...] = _bf(_mm_nt(dyb, w_ref[...].reshape(d, d)))
        sg_ref[2:3, :] += _sum0(dx1v * y_ref[...])

    tile = pl.BlockSpec((ts, d), lambda i: (i, 0))
    bft = jax.ShapeDtypeStruct((s, d), BF16)
    return _pcall(
        body, name="odd_out_bwd", grid=(s // ts,),
        out_shape=(bft, bft, jax.ShapeDtypeStruct((8, d), F32)),
        in_specs=[tile, tile, _full(modp.shape), _wspec4(w128, slot)],
        out_specs=(tile, tile, _full((8, d))), compiler_params=_params(1),
    )(dx1, y, modp, w128)


def _odd_qkv_bwd(x, dx1, dq, dk, dv, sg_in, modp, w384, j, qkg, ts):
    s, d = x.shape
    n3 = N_DEV * w384.shape[2]

    def body(x_ref, dx1_ref, dq_ref, dk_ref, dv_ref, sgi_ref, mp_ref, w_ref, g_ref,
             dx_ref, dqkv_ref, h_ref, sg_ref):
        @pl.when(pl.program_id(0) == 0)
        def _():
            sg_ref[...] = sgi_ref[...]
        gmix, sc1 = mp_ref[6:7, :], mp_ref[1:2, :]
        n, rstd, h = _normmod_fwd(x_ref[...], gmix, sc1, mp_ref[0:1, :])
        hb = _bf(h)
        h_ref[...] = hb
        w = w_ref[...].reshape(n3, d)
        qkv = _mm_nt(hb, w)
        parts_q, parts_k = [], []
        gq, gk = g_ref[0:1, :], g_ref[1:2, :]
        dgq = jnp.zeros((1, HEAD), F32)
        dgk = jnp.zeros((1, HEAD), F32)
        for hh in range(SB_HEADS):
            lo = hh * HEAD
            qn, qr = _rms_fwd(qkv[:, lo:lo + HEAD])
            kn, kr = _rms_fwd(qkv[:, d + lo:d + lo + HEAD])
            dqn = dq_ref[:, lo:lo + HEAD]
            dkn = dk_ref[:, lo:lo + HEAD]
            dgq = dgq + _sum0(dqn * qn)
            dgk = dgk + _sum0(dkn * kn)
            parts_q.append(_rms_bwd(dqn * gq, qn, qr))
            parts_k.append(_rms_bwd(dkn * gk, kn, kr))
        dqkv = _bf(jnp.concatenate(parts_q + parts_k + [dv_ref[...]], axis=1))
        dqkv_ref[...] = dqkv
        dh = _mm(dqkv, w)
        dxn, dsh, dsc, dg = _normmod_bwd(dh, n, rstd, gmix, sc1)
        dx_ref[...] = dx1_ref[...] + dxn
        sg_ref[0:1, :] += dsh
        sg_ref[1:2, :] += dsc
        sg_ref[3:4, :] += dg
        sg_ref[4:5, 0:HEAD] += dgq
        sg_ref[5:6, 0:HEAD] += dgk

    tile = pl.BlockSpec((ts, d), lambda i: (i, 0))
    return _pcall(
        body, name="odd_qkv_bwd", grid=(s // ts,),
        out_shape=(jax.ShapeDtypeStruct((s, d), F32), jax.ShapeDtypeStruct((s, n3), BF16),
                   jax.ShapeDtypeStruct((s, d), BF16), jax.ShapeDtypeStruct((8, d), F32)),
        in_specs=[tile, tile, tile, tile, tile, _full((8, d)), _full(modp.shape), _wspec4(w384, j),
                  _full(qkg.shape)],
        out_specs=(tile, pl.BlockSpec((ts, n3), lambda i: (i, 0)), tile, _full((8, d))),
        compiler_params=_params(1),
    )(x, dx1, dq, dk, dv, sg_in, modp, w384, qkg)


def _pad_rows(a, rows):
    return jnp.concatenate([a, jnp.zeros((rows - a.shape[0],) + a.shape[1:], a.dtype)], axis=0)


def kernel(x, c, ada_w, ada_b, norm_mix_g, norm_ffn_g, ev_w_in, ev_conv_w, ev_ret_norm_g, ev_w_out, od_w_qkv, od_q_norm_g, od_k_norm_g, od_w_out, ffn_w_gate, ffn_w_up, ffn_w_down, loss_target, m_ada_w, m_ada_b, m_norm_mix_g, m_norm_ffn_g, m_ev_w_in, m_ev_conv_w, m_ev_ret_norm_g, m_ev_w_out, m_od_w_qkv, m_od_q_norm_g, m_od_k_norm_g, m_od_w_out, m_ffn_w_gate, m_ffn_w_up, m_ffn_w_down, v_ada_w, v_ada_b, v_norm_mix_g, v_norm_ffn_g, v_ev_w_in, v_ev_conv_w, v_ev_ret_norm_g, v_ev_w_out, v_od_w_qkv, v_od_q_norm_g, v_od_k_norm_g, v_od_w_out, v_ffn_w_gate, v_ffn_w_up, v_ffn_w_down):
    me = 4 * lax.axis_index("x") + 2 * lax.axis_index("y") + lax.axis_index("c")
    xs = x[0]
    tgt = loss_target[0]
    s, d = xs.shape
    depth = ada_w.shape[0]
    n_even, n_odd = ev_w_in.shape[0], od_w_qkv.shape[0]
    ts = 256
    tq = SB_QUERIES
    cd = 4 * HEAD
    cc = ev_conv_w.shape[2]

    pack0 = jnp.zeros((8, d), F32).at[0].set(c[0]).at[1, :n_even * 3 * cc].set(ev_conv_w.reshape(-1))
    got0, _ = _all_gather(pack0, "gather_cond")
    got0 = got0.reshape(N_DEV, 8, d)
    c_all = got0[:, 0, :]
    conv_all = got0[:, 1, :n_even * 3 * cc].reshape(N_DEV, n_even, 3, cc).transpose(1, 2, 0, 3)
    conv_all = conv_all.reshape(n_even, 3, N_DEV * cc)
    cols = ada_w.shape[2]
    ada_b_cols = lax.dynamic_slice(ada_b, (0, me * cols), (depth, cols))
    mod_cols = _ada_fwd(c_all, ada_w, ada_b_cols)
    got1, cond_done = _all_gather(mod_cols.reshape(depth * N_DEV, cols), "gather_mod")
    got1 = got1.reshape(N_DEV, depth, N_DEV, cols)
    mod = lax.dynamic_index_in_dim(got1, me, axis=2, keepdims=False)
    mod = mod.transpose(1, 0, 2).reshape(depth, 6, d)
    modps = [jnp.concatenate([mod[l], norm_mix_g[l][None], norm_ffn_g[l][None]], axis=0) for l in range(depth)]

    in_flight = []
    started = cond_done
    for l in range(depth):
        j = l // 2
        plain = lambda w: _bf(w + started)
        tr = lambda w: plain(w).T
        blocks = [tr(ev_w_in[j]), plain(ev_w_out[j])] if l % 2 == 0 else [tr(od_w_qkv[j]), plain(od_w_out[j])]
        blocks.append(jnp.concatenate([tr(ffn_w_gate[l]), tr(ffn_w_up[l]), plain(ffn_w_down[l])], axis=0))
        lands = [_landing(b, me, False) for b in blocks]
        flight, started = _exchange_start(blocks, lands, False, f"gather_start_{l}")
        in_flight.append(flight)
        modps[0] = modps[0] + started
    n_ffn = ffn_w_down.shape[1]

    def layer_weights(l, after):
        got = _exchange_wait(*in_flight[l], False, [after], f"gather_wait_{l}")
        w_in = got[0].reshape(N_DEV, 1, -1, d)
        w_out = got[1].reshape(N_DEV, 1, -1, d)
        return w_in, w_out, got[2].reshape(N_DEV, 3, n_ffn, d)

    cos, sin = _rope_tables(s)
    consts = _retention_consts(ts)
    cws = [_pad_rows(jnp.concatenate([conv_all[j], ev_ret_norm_g[j][None]], axis=0), 8) for j in range(n_even)]
    qkgs = [_pad_rows(jnp.stack([od_q_norm_g[j], od_k_norm_g[j]]), 8) for j in range(n_odd)]

    saved = []
    weights = []
    cur = xs
    for l in range(depth):
        j = l // 2
        w_in, w_out, w_ffn = layer_weights(l, cur)
        weights.append((w_in, w_out, w_ffn))
        if l % 2 == 0:
            x1, y, states, zhalo, proj = _even_fwd(cur, modps[l], w_in, w_out, 0, cws[j], cos, sin, consts, ts)
            mix = (states, zhalo, proj)
        else:
            qkv = _odd_qkv_fwd(cur, modps[l], w_in, 0, qkgs[j], ts)
            o, tot = _sb_fwd(qkv, tq)
            x1, y = _odd_out_fwd(o, cur, modps[l], w_out, 0, ts)
            mix = (qkv, o, tot)
        x2, f, ab = _ffn_fwd(x1, modps[l], w_ffn, 0, ts)
        saved.append((cur, x1, y, (f, ab), mix))
        cur = x2

    dx, loss_part = _loss_grad(cur, tgt, ts)
    loss = lax.psum(loss_part[0, 0], ("x", "y", "c"))

    dmod = [None] * depth
    d_gmix = [None] * depth
    d_gffn = [None] * depth
    d_conv = [None] * n_even
    d_retg = [None] * n_even
    d_qg = [None] * n_odd
    d_kg = [None] * n_odd
    grads_in_flight = [None] * depth
    for l in reversed(range(depth)):
        j = l // 2
        x0, x1, y, (f, ab), mix = saved[l]
        w_in, w_out, w_ffn = weights[l]
        g_ffn = lax.empty(w_ffn.shape, BF16)
        g_in = lax.empty(w_in.shape, BF16)
        g_out = lax.empty(w_out.shape, BF16)
        dx1, dab, h2, sv, df, sg2 = _ffn_bwd(x1, f, ab, dx, modps[l], w_ffn, 0, ts)
        g_ffn = _tn_matmul(dab, 0, h2, g_ffn, 0, "tn_gate")
        g_ffn = _tn_matmul(dab, 1, h2, g_ffn, 1, "tn_up")
        g_ffn = _tn_matmul(sv, 0, df, g_ffn, 2, "tn_down")
        if l % 2 == 0:
            states, zhalo, proj = mix
            dx, dproj, hb, mb, dyb, sg1 = _even_bwd(x0, dx1, y, states, zhalo, proj, modps[l], w_in, w_out, 0,
                                                    cws[j], cos, sin, consts, ts)
            g_in = _tn_matmul(dproj, 0, hb, g_in, 0, "tn_ev_in")
            g_out = _tn_matmul(mb, 0, dyb, g_out, 0, "tn_ev_out")
            d_conv[j] = sg1[4:7, :cd]
            d_retg[j] = sg1[7, :cd]
        else:
            qkv, o, tot = mix
            do, dyb, sg0 = _odd_out_bwd(dx1, y, modps[l], w_out, 0, ts)
            dq, dk, dv = _sb_bwd(qkv, do, tot, tq)
            dx, dqkv, hb, sg1 = _odd_qkv_bwd(x0, dx1, dq, dk, dv, sg0, modps[l], w_in, 0, qkgs[j], ts)
            g_in = _tn_matmul(dqkv, 0, hb, g_in, 0, "tn_od_qkv")
            g_out = _tn_matmul(o, 0, dyb, g_out, 0, "tn_od_out")
            d_qg[j] = sg1[4, :HEAD]
            d_kg[j] = sg1[5, :HEAD]
        pieces = [g.reshape(N_DEV, -1, d) for g in (g_ffn, g_in, g_out)]
        if l > 0:
            grads_in_flight[l], started = _exchange_start(pieces, [_landing(p, me, True) for p in pieces], True,
                                                          f"grads_start_{l}")
            modps[l - 1] = modps[l - 1] + started
        dmod[l] = jnp.concatenate([sg1[0:3], sg2[0:3]], axis=0).reshape(-1)
        d_gmix[l] = sg1[3]
        d_gffn[l] = sg2[3]

    small = jnp.concatenate(
        [jnp.stack(dmod).reshape(-1), jnp.stack(d_gmix).reshape(-1), jnp.stack(d_gffn).reshape(-1),
         jnp.stack(d_retg).reshape(-1), jnp.stack(d_qg).reshape(-1), jnp.stack(d_kg).reshape(-1),
         jnp.stack(d_conv).reshape(-1)])
    n_small = small.shape[0]
    rows_small = -(-n_small // (8 * 128)) * 8
    small = jnp.concatenate([small, jnp.zeros((rows_small * 128 - n_small,), F32)]).reshape(rows_small, 128)
    got2, small_done = _all_gather(small, "gather_small")
    got2 = got2.reshape(N_DEV, rows_small, 128)
    grads_in_flight[0], started = _exchange_start(pieces, [_landing(p, me, True, small_done) for p in pieces],
                                                  True, "grads_start_0")
    tot_small = _sum_small(got2).reshape(-1)
    n_mod = depth * 6 * d
    dmod_all = got2.reshape(N_DEV, -1)[:, :n_mod].reshape(N_DEV, depth, 6 * d)
    dmod_cols = lax.dynamic_slice(dmod_all, (0, 0, me * cols), (N_DEV, depth, cols)).transpose(1, 0, 2)
    g_ada_w = _ada_bwd(c_all.T, dmod_cols + started)

    off = [0]

    def take(shape):
        n = int(np.prod(shape))
        out = tot_small[off[0]:off[0] + n].reshape(shape)
        off[0] += n
        return out

    g_ada_b = take((depth, 6 * d))
    g_norm_mix = take((depth, d))
    g_norm_ffn = take((depth, d))
    g_ret_norm = take((n_even, cd))
    g_q_norm = take((n_odd, HEAD))
    g_k_norm = take((n_odd, HEAD))
    g_conv_full = take((n_even, 3, cd))
    g_conv = lax.dynamic_slice(g_conv_full, (0, 0, me * cc), (n_even, 3, cc))

    res = {"ada_w": (g_ada_w,) + _adamw_nd(ada_w, g_ada_w, m_ada_w, v_ada_w, "adamw_ada_w")}
    big = {"ev_w_in": (ev_w_in, m_ev_w_in, v_ev_w_in), "ev_w_out": (ev_w_out, m_ev_w_out, v_ev_w_out),
           "od_w_qkv": (od_w_qkv, m_od_w_qkv, v_od_w_qkv), "od_w_out": (od_w_out, m_od_w_out, v_od_w_out),
           "ffn_w_gate": (ffn_w_gate, m_ffn_w_gate, v_ffn_w_gate), "ffn_w_up": (ffn_w_up, m_ffn_w_up, v_ffn_w_up),
           "ffn_w_down": (ffn_w_down, m_ffn_w_down, v_ffn_w_down)}
    for name, (w, _, _) in big.items():
        res[name] = tuple(lax.empty(w.shape, F32) for _ in range(4))

    def update(name, idx, g_layer):
        w, m, v = big[name]
        res[name] = _adamw_layer(w, g_layer, m, v, res[name], idx, "adamw_" + name)

    after = [res["ada_w"][1]]
    for l in reversed(range(depth)):
        recv = _exchange_wait(*grads_in_flight[l], True, after, f"grads_wait_{l}")
        s_ffn, s_in, s_out = [_sum_slots(r, f"sum_{i}") for i, r in enumerate(recv)]
        j = l // 2
        update("ffn_w_gate", l, s_ffn[0:n_ffn].T)
        update("ffn_w_up", l, s_ffn[n_ffn:2 * n_ffn].T)
        update("ffn_w_down", l, s_ffn[2 * n_ffn:3 * n_ffn])
        update("ev_w_in" if l % 2 == 0 else "od_w_qkv", j, s_in.T)
        update("ev_w_out" if l % 2 == 0 else "od_w_out", j, s_out)
        after = [res[name][1] for name in big]

    smalls = [("ada_b", ada_b, g_ada_b, m_ada_b, v_ada_b), ("norm_mix_g", norm_mix_g, g_norm_mix, m_norm_mix_g, v_norm_mix_g),
              ("norm_ffn_g", norm_ffn_g, g_norm_ffn, m_norm_ffn_g, v_norm_ffn_g),
              ("ev_conv_w", ev_conv_w, g_conv, m_ev_conv_w, v_ev_conv_w),
              ("ev_ret_norm_g", ev_ret_norm_g, g_ret_norm, m_ev_ret_norm_g, v_ev_ret_norm_g),
              ("od_q_norm_g", od_q_norm_g, g_q_norm, m_od_q_norm_g, v_od_q_norm_g),
              ("od_k_norm_g", od_k_norm_g, g_k_norm, m_od_k_norm_g, v_od_k_norm_g)]

    def pack(arrs):
        flat = jnp.concatenate([a.reshape(-1) for a in arrs])
        rows = -(-flat.shape[0] // (8 * 128)) * 8
        return jnp.concatenate([flat, jnp.zeros((rows * 128 - flat.shape[0],), F32)]).reshape(rows, 128)

    sd, sm, sv_ = _adamw(pack([t[1] for t in smalls]), pack([t[2] for t in smalls]),
                         pack([t[3] for t in smalls]), pack([t[4] for t in smalls]), "adamw_small")
    sd, sm, sv_ = sd.reshape(-1), sm.reshape(-1), sv_.reshape(-1)
    pos = 0
    for name, w, g, m, v in smalls:
        n = int(np.prod(w.shape))
        res[name] = (g, sd[pos:pos + n].reshape(w.shape), sm[pos:pos + n].reshape(w.shape),
                     sv_[pos:pos + n].reshape(w.shape))
        pos += n

    order = ["ada_w", "ada_b", "norm_mix_g", "norm_ffn_g", "ev_w_in", "ev_conv_w", "ev_ret_norm_g", "ev_w_out",
             "od_w_qkv", "od_q_norm_g", "od_k_norm_g", "od_w_out", "ffn_w_gate", "ffn_w_up", "ffn_w_down"]
    outs = [loss, dx[None]]
    for k in range(4):
        outs += [res[name][k] for name in order]
    return tuple(outs)
```

```python
import functools
import math

import numpy as np
import jax
import jax.numpy as jnp
from jax import lax
from jax.experimental import pallas as pl
from jax.experimental.pallas import tpu as pltpu

F32 = jnp.float32
BF16 = jnp.bfloat16
MESH = pl.DeviceIdType.MESH

N_DEV = 8
EPS = 1e-6
CHUNK = 64
HEAD = 128
RET_HEADS = 4
SB_HEADS = 8
ROPE_THETA = 10000.0
KEY_BLOCK = 128
ADAM_LR, ADAM_B1, ADAM_B2, ADAM_EPS, ADAM_WD, ADAM_STEP = 0.001, 0.9, 0.999, 1e-08, 0.01, 10
VMEM_LIMIT = 56 * 1024 * 1024


def _pcall(body, **kw):
    return pl.pallas_call(body, **kw)


def _params(n_grid=1, vmem=VMEM_LIMIT):
    return pltpu.CompilerParams(dimension_semantics=("arbitrary",) * n_grid, vmem_limit_bytes=vmem)


def _mm(a, b):
    return jnp.dot(a, b, preferred_element_type=F32)


def _mm_nt(a, b):
    return lax.dot_general(a, b, (((1,), (1,)), ((), ())), preferred_element_type=F32)


def _mm_tn(a, b):
    return lax.dot_general(a, b, (((0,), (0,)), ((), ())), preferred_element_type=F32)


def _bf(a):
    return a.astype(BF16)


def _sigmoid(a):
    return 1.0 / (1.0 + jnp.exp(-a))


def _sum0(a):
    return jnp.sum(a, axis=0, keepdims=True)


def _full(shape):
    nd = len(shape)
    return pl.BlockSpec(shape, lambda *_: (0,) * nd)


def _normmod_fwd(x, g, sc, sh):
    rstd = lax.rsqrt(jnp.mean(x * x, axis=-1, keepdims=True) + EPS)
    n = x * rstd
    return n, rstd, (n * g) * (1.0 + sc) + sh


def _normmod_bwd(dh, n, rstd, g, sc):
    dsh = _sum0(dh)
    dsc = _sum0(dh * (n * g))
    dg = _sum0(dh * n * (1.0 + sc))
    dn = dh * (g * (1.0 + sc))
    dx = rstd * (dn - n * jnp.mean(dn * n, axis=-1, keepdims=True))
    return dx, dsh, dsc, dg


def _rms_fwd(o):
    rstd = lax.rsqrt(jnp.mean(o * o, axis=-1, keepdims=True) + EPS)
    return o * rstd, rstd


def _rms_bwd(dn, n, rstd):
    return rstd * (dn - n * jnp.mean(dn * n, axis=-1, keepdims=True))


def _all_gather(x2d, name):
    m_per, n = x2d.shape
    space = pltpu.VMEM

    def body(x_ref, out_ref, done_ref, send_sems, recv_sems, local_sem):
        x, y, c = lax.axis_index("x"), lax.axis_index("y"), lax.axis_index("c")
        me, sibling = (x, y, c), (x, y, 1 - c)
        chips = [(1 - x, y), (x, 1 - y), (1 - x, 1 - y)]

        def rows(px, py, pc):
            return out_ref.at[pl.ds((4 * px + 2 * py + pc) * m_per, m_per), :]

        def copy(k, block, to, src=None):
            return pltpu.make_async_remote_copy(
                src_ref=rows(*block) if src is None else src, dst_ref=rows(*block),
                send_sem=send_sems.at[k], recv_sem=recv_sems.at[k],
                device_id=to, device_id_type=MESH)

        mine = pltpu.make_async_copy(x_ref, rows(*me), local_sem)
        mine.start()
        first = [copy(1 + j, me, (*chip, c), src=x_ref) for j, chip in enumerate(chips)]
        first += [copy(0, me, sibling, src=x_ref)]
        for cp in first:
            cp.start()
        passed = [copy(4 + j, (*chip, c), sibling) for j, chip in enumerate(chips)]
        for j, chip in enumerate(chips):
            copy(1 + j, (*chip, c), me).wait_recv()
            passed[j].start()
        copy(0, sibling, me).wait_recv()
        for j, chip in enumerate(chips):
            copy(4 + j, (*chip, 1 - c), me).wait_recv()
        for cp in first + passed:
            cp.wait_send()
        mine.wait()
        done_ref[...] = jnp.zeros_like(done_ref)

    out, done = _pcall(
        body, name=name,
        out_shape=(jax.ShapeDtypeStruct((N_DEV * m_per, n), x2d.dtype), jax.ShapeDtypeStruct((8, 128), F32)),
        in_specs=[pl.BlockSpec(memory_space=space)],
        out_specs=(pl.BlockSpec(memory_space=space), pl.BlockSpec(memory_space=pltpu.VMEM)),
        scratch_shapes=[pltpu.SemaphoreType.DMA((7,)), pltpu.SemaphoreType.DMA((7,)),
                        pltpu.SemaphoreType.DMA],
    )(x2d)
    return out, done[0, 0]


_HBM = pl.BlockSpec(memory_space=pltpu.HBM)
_SEM = pl.BlockSpec(memory_space=pltpu.SEMAPHORE)
_EFFECT = pltpu.SideEffectType.DATAFLOW_SIDE_EFFECTING


def _exchange_copies(src_refs, land_refs, send_sems, recv_sems, scatter):
    x, y, c = lax.axis_index("x"), lax.axis_index("y"), lax.axis_index("c")
    me = 4 * x + 2 * y + c
    out = []
    for i, (s_ref, l_ref) in enumerate(zip(src_refs, land_refs)):
        for k in (2, 4, 6, 3, 5, 7, 1):
            px = (1 - x) if (k >> 2) & 1 else x
            py = (1 - y) if (k >> 1) & 1 else y
            pc = (1 - c) if k & 1 else c
            out.append(pltpu.make_async_remote_copy(
                src_ref=s_ref.at[4 * px + 2 * py + pc] if scatter else s_ref, dst_ref=l_ref.at[me],
                send_sem=send_sems.at[7 * i + k - 1], recv_sem=recv_sems.at[7 * i + k - 1],
                device_id=(px, py, pc), device_id_type=MESH))
    return out


def _exchange_start(srcs, lands, scatter, name):
    n = len(srcs)

    def body(*refs):
        for cp in _exchange_copies(refs[:n], refs[n:2 * n], refs[2 * n], refs[2 * n + 1], scatter):
            cp.start()
        refs[-1][...] = jnp.zeros_like(refs[-1])

    arrays = list(srcs) + list(lands)
    outs = _pcall(
        body, name=name,
        out_shape=(pltpu.SemaphoreType.DMA((7 * n,)), pltpu.SemaphoreType.DMA((7 * n,)),
                   *[pltpu.HBM(a.shape, a.dtype) for a in arrays], jax.ShapeDtypeStruct((8, 128), F32)),
        in_specs=[_HBM] * (2 * n),
        out_specs=(_SEM, _SEM, *[_HBM] * (2 * n), pl.BlockSpec(memory_space=pltpu.VMEM)),
        input_output_aliases={i: 2 + i for i in range(2 * n)},
        compiler_params=pltpu.CompilerParams(has_side_effects=_EFFECT),
    )(*[pltpu.with_memory_space_constraint(a, pltpu.HBM) for a in arrays])
    return (outs[0], outs[1], list(outs[2:2 + n]), list(outs[2 + n:2 + 2 * n])), outs[-1][0, 0]


def _exchange_wait(send_sems, recv_sems, srcs, lands, scatter, after, name):
    n = len(srcs)
    after = list(after)

    def body(*refs):
        for cp in _exchange_copies(refs[:n], refs[n:2 * n], refs[2 * n], refs[2 * n + 1], scatter):
            cp.wait_send()
            cp.wait_recv()

    arrays = list(srcs) + list(lands)
    outs = _pcall(
        body, name=name,
        out_shape=tuple(pltpu.HBM(a.shape, a.dtype) for a in arrays),
        in_specs=[_HBM] * (2 * n) + [_SEM, _SEM] + [pl.BlockSpec(memory_space=pl.ANY)] * len(after),
        out_specs=tuple([_HBM] * (2 * n)),
        input_output_aliases={i: i for i in range(2 * n)},
        compiler_params=pltpu.CompilerParams(has_side_effects=_EFFECT),
    )(*arrays, send_sems, recv_sems, *after)
    return list(outs[n:])


def _landing(src, me, scatter, after=None):
    own = lax.dynamic_index_in_dim(src, me, 0, keepdims=True) if scatter else src[None]
    if after is not None:
        own = own + after.astype(own.dtype)
    shape = src.shape if scatter else (N_DEV,) + src.shape
    return lax.dynamic_update_slice(lax.empty(shape, src.dtype), own, (me, 0, 0))


def _sum_slots(recv, name):
    _, r, n = recv.shape
    tr = r
    for cand in (512, 448, 384, 352, 256, 128, 64, 32, 16, 8):
        if r % cand == 0:
            tr = cand
            break

    def body(r_ref, o_ref):
        acc = r_ref[0].astype(F32)
        for p in range(1, N_DEV):
            acc = acc + r_ref[p].astype(F32)
        o_ref[...] = acc

    return _pcall(
        body, name=name, grid=(r // tr,),
        out_shape=jax.ShapeDtypeStruct((r, n), F32),
        in_specs=[pl.BlockSpec((N_DEV, tr, n), lambda i: (0, i, 0))],
        out_specs=pl.BlockSpec((tr, n), lambda i: (i, 0)),
        compiler_params=_params(1),
    )(recv)


def _adamw(w, g, m, v, name):
    r, n = w.shape
    tr = r
    for cand in (512, 256, 128, 64, 32, 16, 8):
        if r % cand == 0:
            tr = cand
            break

    def body(w_ref, g_ref, m_ref, v_ref, d_ref, nm_ref, nv_ref):
        d_ref[...], nm_ref[...], nv_ref[...] = _adam_update(w_ref[...], g_ref[...], m_ref[...], v_ref[...])

    spec = pl.BlockSpec((tr, n), lambda i: (i, 0))
    shp = jax.ShapeDtypeStruct((r, n), F32)
    return _pcall(
        body, name=name, grid=(r // tr,), out_shape=(shp, shp, shp),
        in_specs=[spec] * 4, out_specs=(spec, spec, spec), compiler_params=_params(1),
    )(w, g, m, v)


def _adam_update(wv, gv, mv, vv):
    bc1 = 1.0 / (1.0 - ADAM_B1 ** ADAM_STEP)
    bc2 = 1.0 / (1.0 - ADAM_B2 ** ADAM_STEP)
    nm = ADAM_B1 * mv + (1.0 - ADAM_B1) * gv
    nv = ADAM_B2 * vv + (1.0 - ADAM_B2) * (gv * gv)
    return -ADAM_LR * ((nm * bc1) / (jnp.sqrt(nv * bc2) + ADAM_EPS) + ADAM_WD * wv), nm, nv


def _adamw_layer(w, g_layer, m, v, outs, idx, name):
    _, a, b = w.shape
    tr = a
    for cand in (512, 256, 128, 64, 32, 16, 8):
        if a % cand == 0:
            tr = cand
            break

    def body(w_ref, g_ref, m_ref, v_ref, o0, o1, o2, o3, go_ref, d_ref, nm_ref, nv_ref):
        gv = g_ref[...]
        go_ref[...] = gv
        d_ref[...], nm_ref[...], nv_ref[...] = _adam_update(w_ref[...], gv, m_ref[...], v_ref[...])

    layer = pl.BlockSpec((None, tr, b), lambda i: (idx, i, 0))
    anyw = pl.BlockSpec(memory_space=pl.ANY)
    shp = jax.ShapeDtypeStruct(w.shape, F32)
    return tuple(_pcall(
        body, name=name, grid=(a // tr,), out_shape=(shp,) * 4,
        in_specs=[layer, pl.BlockSpec((tr, b), lambda i: (i, 0)), layer, layer, anyw, anyw, anyw, anyw],
        out_specs=(layer,) * 4, input_output_aliases={4: 0, 5: 1, 6: 2, 7: 3},
        compiler_params=_params(1),
    )(w, g_layer, m, v, *outs))


def _adamw_nd(w, g, m, v, name):
    shp = w.shape
    f = lambda a: a.reshape(-1, shp[-1])
    d, nm, nv = _adamw(f(w), f(g), f(m), f(v), name)
    return d.reshape(shp), nm.reshape(shp), nv.reshape(shp)


def _ada_fwd(c_all, ada_w, ada_b_cols):
    n_l, d, cols = ada_w.shape

    def body(c_ref, w_ref, b_ref, o_ref):
        cv = c_ref[...]
        ca = cv * _sigmoid(cv)
        o_ref[...] = _mm(_bf(ca), _bf(w_ref[...])) + b_ref[...]

    return _pcall(
        body, name="ada_fwd", grid=(n_l,),
        out_shape=jax.ShapeDtypeStruct((n_l, N_DEV, cols), F32),
        in_specs=[_full((N_DEV, d)), pl.BlockSpec((None, d, cols), lambda l: (l, 0, 0)),
                  pl.BlockSpec((None, 1, cols), lambda l: (l, 0, 0))],
        out_specs=pl.BlockSpec((None, N_DEV, cols), lambda l: (l, 0, 0)),
        compiler_params=_params(1),
    )(c_all, ada_w, ada_b_cols.reshape(n_l, 1, cols))


def _ada_bwd(c_all_t, dmod_cols):
    d = c_all_t.shape[0]
    n_l, _, cols = dmod_cols.shape

    def body(ct_ref, dm_ref, o_ref):
        cv = ct_ref[...]
        ca = cv * _sigmoid(cv)
        dm = dm_ref[...]
        acc = ca[:, 0:1] * dm[0:1, :]
        for b in range(1, N_DEV):
            acc = acc + ca[:, b:b + 1] * dm[b:b + 1, :]
        o_ref[...] = acc

    return _pcall(
        body, name="ada_bwd", grid=(n_l,),
        out_shape=jax.ShapeDtypeStruct((n_l, d, cols), F32),
        in_specs=[_full((d, N_DEV)), pl.BlockSpec((None, N_DEV, cols), lambda l: (l, 0, 0))],
        out_specs=pl.BlockSpec((None, d, cols), lambda l: (l, 0, 0)),
        compiler_params=_params(1),
    )(c_all_t, dmod_cols)


def _sum_small(gathered):
    _, r, n = gathered.shape

    def body(g_ref, o_ref):
        acc = g_ref[0]
        for p in range(1, N_DEV):
            acc = acc + g_ref[p]
        o_ref[...] = acc

    return _pcall(
        body, name="sum_small", out_shape=jax.ShapeDtypeStruct((r, n), F32),
        in_specs=[_full((N_DEV, r, n))], out_specs=_full((r, n)),
    )(gathered)


def _loss_grad(xf, tgt, ts):
    s, d = xf.shape

    def body(x_ref, t_ref, dx_ref, l_ref):
        @pl.when(pl.program_id(0) == 0)
        def _():
            l_ref[...] = jnp.zeros_like(l_ref)
        e = x_ref[...] - t_ref[...]
        dx_ref[...] = e * (1.0 / d)
        l_ref[...] += (0.5 / d) * jnp.sum(jnp.sum(e * e, axis=1, keepdims=True), axis=0, keepdims=True)

    spec = pl.BlockSpec((ts, d), lambda i: (i, 0))
    return _pcall(
        body, name="loss_grad", grid=(s // ts,),
        out_shape=(jax.ShapeDtypeStruct((s, d), F32), jax.ShapeDtypeStruct((1, 1), F32)),
        in_specs=[spec, spec], out_specs=(spec, _full((1, 1))), compiler_params=_params(1),
    )(xf, tgt)


def _tn_matmul(a, col_block, b, buf, slot, name):
    s = a.shape[0]
    k = b.shape[1]
    n_p = buf.shape[2]
    mcols = N_DEV * n_p
    ts = 512 if s % 512 == 0 else 256
    nt = s // ts

    def body(a_ref, b_ref, buf_ref, o_ref, acc):
        i = pl.program_id(0)

        @pl.when(i == 0)
        def _():
            acc[...] = jnp.zeros_like(acc)
        acc[...] += _mm_tn(a_ref[...], b_ref[...])

        @pl.when(i == nt - 1)
        def _():
            o_ref[...] = acc[...].reshape(N_DEV, n_p, k).astype(BF16)

    return _pcall(
        body, name=name, grid=(nt,),
        out_shape=jax.ShapeDtypeStruct(buf.shape, BF16),
        in_specs=[pl.BlockSpec((ts, mcols), lambda i: (i, col_block)),
                  pl.BlockSpec((ts, k), lambda i: (i, 0)),
                  pl.BlockSpec(memory_space=pl.ANY)],
        out_specs=pl.BlockSpec((N_DEV, None, n_p, k), lambda i: (0, slot, 0, 0)),
        scratch_shapes=[pltpu.VMEM((mcols, k), F32)],
        input_output_aliases={2: 0},
        compiler_params=_params(1),
    )(a, b, buf)


def _wspec4(w, slot):
    _, _, n_p, k = w.shape
    return pl.BlockSpec((N_DEV, None, n_p, k), lambda i: (0, slot, 0, 0), pipeline_mode=pl.Buffered(1))


def _ffn_fwd(x1, modp, w352, l, ts):
    s, d = x1.shape
    n_l = w352.shape[1] // 3
    f_dim = N_DEV * w352.shape[2]

    def body(x_ref, mp_ref, wg_ref, wu_ref, wd_ref, x2_ref, f_ref, ab_ref):
        x = x_ref[...]
        _, _, h2 = _normmod_fwd(x, mp_ref[7:8, :], mp_ref[4:5, :], mp_ref[3:4, :])
        hb = _bf(h2)
        f = jnp.zeros((ts, d), F32)
        half_dev, fc = N_DEV // 2, f_dim // 2
        for part in range(2):
            dev0, c0 = part * half_dev, part * fc
            a = _mm_nt(hb, wg_ref[dev0:dev0 + half_dev].reshape(fc, d))
            b = _mm_nt(hb, wu_ref[dev0:dev0 + half_dev].reshape(fc, d))
            ab_ref[:, c0:c0 + fc] = a
            ab_ref[:, f_dim + c0:f_dim + c0 + fc] = b
            sv = (a * _sigmoid(a)) * b
            f = f + _mm(_bf(sv), wd_ref[dev0:dev0 + half_dev].reshape(fc, d))
        f_ref[...] = f
        x2_ref[...] = x + mp_ref[5:6, :] * f

    tile = pl.BlockSpec((ts, d), lambda i: (i, 0))
    shp = jax.ShapeDtypeStruct((s, d), F32)
    return _pcall(
        body, name="ffn_fwd", grid=(s // ts,),
        out_shape=(shp, shp, jax.ShapeDtypeStruct((s, 2 * f_dim), F32)),
        in_specs=[tile, _full(modp.shape), _wspec4(w352, l), _wspec4(w352, n_l + l),
                  _wspec4(w352, 2 * n_l + l)],
        out_specs=(tile, tile, pl.BlockSpec((ts, 2 * f_dim), lambda i: (i, 0))), compiler_params=_params(1),
    )(x1, modp, w352, w352, w352)


def _ffn_bwd(x1, f, ab, dx2, modp, w352, l, ts):
    s, d = x1.shape
    n_l = w352.shape[1] // 3
    f_dim = N_DEV * w352.shape[2]

    def body(x_ref, f_ref, ab_ref, dx2_ref, mp_ref, wg_ref, wu_ref, wd_ref,
             dx1_ref, dab_ref, h2_ref, s_ref, df_ref, sg_ref):
        @pl.when(pl.program_id(0) == 0)
        def _():
            sg_ref[...] = jnp.zeros_like(sg_ref)
        x = x_ref[...]
        gffn, sc2, g2 = mp_ref[7:8, :], mp_ref[4:5, :], mp_ref[5:6, :]
        n, rstd, h2 = _normmod_fwd(x, gffn, sc2, mp_ref[3:4, :])
        hb = _bf(h2)
        dx2 = dx2_ref[...]
        dfb = _bf(g2 * dx2)
        dh2 = jnp.zeros((ts, d), F32)
        half_dev, fc = N_DEV // 2, f_dim // 2
        for part in range(2):
            dev0, c0 = part * half_dev, part * fc
            wg = wg_ref[dev0:dev0 + half_dev].reshape(fc, d)
            wu = wu_ref[dev0:dev0 + half_dev].reshape(fc, d)
            a = ab_ref[:, c0:c0 + fc]
            b = ab_ref[:, f_dim + c0:f_dim + c0 + fc]
            sig = _sigmoid(a)
            sa = a * sig
            s_ref[:, c0:c0 + fc] = _bf(sa * b)
            ds = _mm_nt(dfb, wd_ref[dev0:dev0 + half_dev].reshape(fc, d))
            dab = _bf(ds * b * (sig * (1.0 + a * (1.0 - sig))))
            dbb = _bf(ds * sa)
            dab_ref[:, c0:c0 + fc] = dab
            dab_ref[:, f_dim + c0:f_dim + c0 + fc] = dbb
            dh2 = dh2 + _mm(dab, wg) + _mm(dbb, wu)
        dxn, dsh, dsc, dg = _normmod_bwd(dh2, n, rstd, gffn, sc2)
        dx1_ref[...] = dx2 + dxn
        h2_ref[...] = hb
        df_ref[...] = dfb
        sg_ref[0:1, :] += dsh
        sg_ref[1:2, :] += dsc
        sg_ref[2:3, :] += _sum0(dx2 * f_ref[...])
        sg_ref[3:4, :] += dg

    tile = pl.BlockSpec((ts, d), lambda i: (i, 0))
    f32t = jax.ShapeDtypeStruct((s, d), F32)
    bft = jax.ShapeDtypeStruct((s, d), BF16)
    return _pcall(
        body, name="ffn_bwd", grid=(s // ts,),
        out_shape=(f32t, jax.ShapeDtypeStruct((s, 2 * f_dim), BF16), bft,
                   jax.ShapeDtypeStruct((s, f_dim), BF16), bft, jax.ShapeDtypeStruct((8, d), F32)),
        in_specs=[tile, tile, pl.BlockSpec((ts, 2 * f_dim), lambda i: (i, 0)), tile, _full(modp.shape),
                  _wspec4(w352, l), _wspec4(w352, n_l + l), _wspec4(w352, 2 * n_l + l)],
        out_specs=(tile, pl.BlockSpec((ts, 2 * f_dim), lambda i: (i, 0)), tile,
                   pl.BlockSpec((ts, f_dim), lambda i: (i, 0)), tile, _full((8, d))),
        compiler_params=_params(1),
    )(x1, f, ab, dx2, modp, w352, w352, w352)


def _retention_consts(ts):
    h = np.arange(RET_HEADS, dtype=np.float64)
    log_g = np.log1p(-np.exp2(-5.0 - h))
    t = np.arange(ts)
    diff = t[:, None] - t[None, :]
    same = (t[:, None] // CHUNK) == (t[None, :] // CHUNK)
    later = (t[:, None] // CHUNK) > (t[None, :] // CHUNK)
    dm = np.where(same, np.abs(diff), np.where(later, diff, 0))[None] * log_g[:, None, None]
    dm = np.where((same | later)[None], np.exp(dm), 0.0)
    qd = np.exp((t[:, None] + 1.0) * log_g[None, :])
    kd = np.exp((ts - 1.0 - t[:, None]) * log_g[None, :])
    qd = np.repeat(qd, HEAD, axis=1)
    kd = np.repeat(kd, HEAD, axis=1)
    tdec = [float(np.exp(ts * lg)) for lg in log_g]
    return (jnp.asarray(dm, F32), jnp.asarray(qd, F32), jnp.asarray(kd, F32), tdec)


def _rope_tables(s):
    inv_freq = 1.0 / (ROPE_THETA ** (jnp.arange(0, HEAD, 2, dtype=F32) / HEAD))
    ang = jnp.arange(s, dtype=F32)[:, None] * inv_freq[None, :]
    cos, sin = jnp.cos(ang), jnp.sin(ang)
    return jnp.concatenate([cos, cos], axis=1), jnp.concatenate([-sin, sin], axis=1)


def _rope(v, cos, sin):
    return v * cos + pltpu.roll(v, HEAD // 2, 1) * sin


def _rope_t(dv, cos, sin):
    return dv * cos + pltpu.roll(dv * sin, HEAD // 2, 1)


def _shift_down(z, k, halo_ref):
    r = pltpu.roll(z, k, 0)
    rows = lax.broadcasted_iota(jnp.int32, z.shape, 0)
    for j in range(k):
        r = jnp.where(rows == j, halo_ref[8 - k + j:8 - k + j + 1, :], r)
    return r


def _shift_up(z, k, halo_ref):
    n = z.shape[0]
    r = pltpu.roll(z, n - k, 0)
    rows = lax.broadcasted_iota(jnp.int32, z.shape, 0)
    for j in range(k):
        r = jnp.where(rows == n - k + j, halo_ref[j:j + 1, :], r)
    return r


def _even_recompute(x, mp_ref, win, cw_ref, cos, sin, dm_ref, qd_ref, kd_ref, halo_ref, state_of, proj=None):
    cd = 4 * HEAD
    n, rstd, h = _normmod_fwd(x, mp_ref[6:7, :], mp_ref[1:2, :], mp_ref[0:1, :])
    if proj is None:
        proj = _mm_nt(_bf(h), win)
    bg, cg, u = proj[:, 0:cd], proj[:, cd:2 * cd], proj[:, 2 * cd:3 * cd]
    z = cg * u
    z1 = _shift_down(z, 1, halo_ref)
    z2 = _shift_down(z, 2, halo_ref)
    conv = cw_ref[0:1, :] * z2 + cw_ref[1:2, :] * z1 + cw_ref[2:3, :] * z
    heads = []
    scale = HEAD ** -0.5
    for hh in range(RET_HEADS):
        lo = hh * HEAD
        q = proj[:, 3 * cd + lo:3 * cd + lo + HEAD]
        k = proj[:, 4 * cd + lo:4 * cd + lo + HEAD]
        v = proj[:, 5 * cd + lo:5 * cd + lo + HEAD]
        gate = proj[:, 6 * cd + lo:6 * cd + lo + HEAD]
        qr = _rope(q, cos, sin)
        kr = _rope(k, cos, sin) * scale
        sc = _mm_nt(_bf(qr), _bf(kr)) * dm_ref[hh]
        qs = qr * qd_ref[:, lo:lo + HEAD]
        ks = kr * kd_ref[:, lo:lo + HEAD]
        o = _mm(_bf(sc), _bf(v)) + _mm(_bf(qs), _bf(state_of(hh)))
        on, orstd = _rms_fwd(o)
        sig = _sigmoid(gate)
        heads.append(dict(qr=qr, kr=kr, v=v, gate=gate, sc=sc, qs=qs, ks=ks, on=on, orstd=orstd, sig=sig))
    return dict(n=n, rstd=rstd, h=h, proj=proj, bg=bg, cg=cg, u=u, z=z, z1=z1, z2=z2, conv=conv, heads=heads)


def _even_fwd(x, modp, w448, w128, l, cw, cos, sin, consts, ts):
    s, d = x.shape
    nt = s // ts
    dm, qd, kd, tdec = consts
    cd = 4 * HEAD
    e_in = N_DEV * w448.shape[2]

    def body(x_ref, mp_ref, win_ref, cw_ref, cos_ref, sin_ref, dm_ref, qd_ref, kd_ref, wout_ref,
             x1_ref, y_ref, st_ref, zh_ref, proj_ref, state, halo):
        @pl.when(pl.program_id(0) == 0)
        def _():
            state[...] = jnp.zeros_like(state)
            halo[...] = jnp.zeros_like(halo)
        xv = x_ref[...]
        st_ref[...] = state[...]
        zh_ref[...] = halo[...]
        r = _even_recompute(xv, mp_ref, win_ref[...].reshape(e_in, d), cw_ref, cos_ref[...], sin_ref[...],
                            dm_ref, qd_ref, kd_ref, halo, lambda hh: state[hh])
        proj_ref[...] = r["proj"]
        halo[...] = r["z"][ts - 8:ts, :]
        parts = [r["bg"] * r["conv"]]
        for hh, hd in enumerate(r["heads"]):
            state[hh] = state[hh] * tdec[hh] + _mm_tn(_bf(hd["ks"]), _bf(hd["v"]))
            rg = cw_ref[3:4, hh * HEAD:(hh + 1) * HEAD]
            parts.append((hd["gate"] * hd["sig"]) * (hd["on"] * rg))
        mcat = jnp.concatenate(parts, axis=1)
        y = _mm(_bf(mcat), wout_ref[...].reshape(d, d))
        y_ref[...] = y
        x1_ref[...] = xv + mp_ref[2:3, :] * y

    tile = pl.BlockSpec((ts, d), lambda i: (i, 0))
    rt = pl.BlockSpec((ts, HEAD), lambda i: (i, 0))
    shp = jax.ShapeDtypeStruct((s, d), F32)
    return _pcall(
        body, name="even_fwd", grid=(nt,),
        out_shape=(shp, shp, jax.ShapeDtypeStruct((nt, RET_HEADS, HEAD, HEAD), F32),
                   jax.ShapeDtypeStruct((nt, 8, cd), F32), jax.ShapeDtypeStruct((s, e_in), F32)),
        in_specs=[tile, _full(modp.shape), _wspec4(w448, l), _full(cw.shape), rt, rt,
                  _full(dm.shape), _full(qd.shape), _full(kd.shape), _wspec4(w128, l)],
        out_specs=(tile, tile, pl.BlockSpec((None, RET_HEADS, HEAD, HEAD), lambda i: (i, 0, 0, 0)),
                   pl.BlockSpec((None, 8, cd), lambda i: (i, 0, 0)), pl.BlockSpec((ts, e_in), lambda i: (i, 0))),
        scratch_shapes=[pltpu.VMEM((RET_HEADS, HEAD, HEAD), F32), pltpu.VMEM((8, cd), F32)],
        compiler_params=_params(1),
    )(x, modp, w448, cw, cos, sin, dm, qd, kd, w128)


def _even_bwd(x, dx1, y, states, zhalo, proj, modp, w448, w128, l, cw, cos, sin, consts, ts):
    s, d = x.shape
    nt = s // ts
    dm, qd, kd, tdec = consts
    cd = 4 * HEAD
    e_in = N_DEV * w448.shape[2]
    scale = HEAD ** -0.5

    def body(x_ref, dx1_ref, y_ref, st_ref, zh_ref, proj_ref, mp_ref, win_ref, cw_ref, cos_ref, sin_ref,
             dm_ref, qd_ref, kd_ref, wout_ref,
             dx_ref, dproj_ref, h_ref, m_ref, dy_ref, sg_ref, gstate, halo_d):
        @pl.when(pl.program_id(0) == 0)
        def _():
            gstate[...] = jnp.zeros_like(gstate)
            halo_d[...] = jnp.zeros_like(halo_d)
            sg_ref[...] = jnp.zeros_like(sg_ref)
        xv = x_ref[...]
        cos, sin = cos_ref[...], sin_ref[...]
        win = win_ref[...].reshape(e_in, d)
        r = _even_recompute(xv, mp_ref, win, cw_ref, cos, sin, dm_ref, qd_ref, kd_ref, zh_ref,
                            lambda hh: st_ref[hh], proj_ref[...])
        parts = [r["bg"] * r["conv"]]
        for hh, hd in enumerate(r["heads"]):
            rg = cw_ref[3:4, hh * HEAD:(hh + 1) * HEAD]
            parts.append((hd["gate"] * hd["sig"]) * (hd["on"] * rg))
        m_ref[...] = _bf(jnp.concatenate(parts, axis=1))
        h_ref[...] = _bf(r["h"])

        dx1 = dx1_ref[...]
        dy = mp_ref[2:3, :] * dx1
        dyb = _bf(dy)
        dy_ref[...] = dyb
        sg_ref[2:3, :] += _sum0(dx1 * y_ref[...])
        dmix = _mm_nt(dyb, wout_ref[...].reshape(d, d))

        da_out = dmix[:, 0:cd]
        dbg = da_out * r["conv"]
        dconv = da_out * r["bg"]
        dc1 = _shift_up(dconv, 1, halo_d)
        dc2 = _shift_up(dconv, 2, halo_d)
        dz = cw_ref[2:3, :] * dconv + cw_ref[1:2, :] * dc1 + cw_ref[0:1, :] * dc2
        halo_d[...] = dconv[0:8, :]
        sg_ref[4:5, 0:cd] += _sum0(dconv * r["z2"])
        sg_ref[5:6, 0:cd] += _sum0(dconv * r["z1"])
        sg_ref[6:7, 0:cd] += _sum0(dconv * r["z"])
        dcg = dz * r["u"]
        du = dz * r["cg"]

        dqs, dks, dvs, dgs = [], [], [], []
        for hh, hd in enumerate(r["heads"]):
            lo = hh * HEAD
            rg = cw_ref[3:4, lo:lo + HEAD]
            dr = dmix[:, cd + lo:cd + lo + HEAD]
            sig, gate, on = hd["sig"], hd["gate"], hd["on"]
            rn = on * rg
            dgate = dr * rn * (sig * (1.0 + gate * (1.0 - sig)))
            drn = dr * (gate * sig)
            sg_ref[7:8, lo:lo + HEAD] += _sum0(drn * on)
            do = _rms_bwd(drn * rg, on, hd["orstd"])
            dob = _bf(do)
            gst = _bf(gstate[hh])
            scb = _bf(hd["sc"])
            vb = _bf(hd["v"])
            qrb, krb = _bf(hd["qr"]), _bf(hd["kr"])
            dv = _mm_tn(scb, dob) + _mm(_bf(hd["ks"]), gst)
            dsc = _bf(_mm_nt(dob, vb) * dm_ref[hh])
            dqr = _mm(dsc, krb) + _mm_nt(dob, _bf(st_ref[hh])) * qd_ref[:, lo:lo + HEAD]
            dkr = _mm_tn(dsc, qrb) + _mm_nt(vb, gst) * kd_ref[:, lo:lo + HEAD]
            gstate[hh] = gstate[hh] * tdec[hh] + _mm_tn(_bf(hd["qs"]), dob)
            dqs.append(_rope_t(dqr, cos, sin))
            dks.append(_rope_t(dkr * scale, cos, sin))
            dvs.append(dv)
            dgs.append(dgate)

        dproj = _bf(jnp.concatenate([dbg, dcg, du] + dqs + dks + dvs + dgs, axis=1))
        dproj_ref[...] = dproj
        dh = _mm(dproj, win)
        dxn, dsh, dsc1, dg = _normmod_bwd(dh, r["n"], r["rstd"], mp_ref[6:7, :], mp_ref[1:2, :])
        dx_ref[...] = dx1 + dxn
        sg_ref[0:1, :] += dsh
        sg_ref[1:2, :] += dsc1
        sg_ref[3:4, :] += dg

    rev = lambda i: (nt - 1 - i, 0)
    tile = pl.BlockSpec((ts, d), rev)
    rt = pl.BlockSpec((ts, HEAD), rev)
    bft = jax.ShapeDtypeStruct((s, d), BF16)
    return _pcall(
        body, name="even_bwd", grid=(nt,),
        out_shape=(jax.ShapeDtypeStruct((s, d), F32), jax.ShapeDtypeStruct((s, e_in), BF16), bft, bft, bft,
                   jax.ShapeDtypeStruct((8, d), F32)),
        in_specs=[tile, tile, tile,
                  pl.BlockSpec((None, RET_HEADS, HEAD, HEAD), lambda i: (nt - 1 - i, 0, 0, 0)),
                  pl.BlockSpec((None, 8, cd), lambda i: (nt - 1 - i, 0, 0)), pl.BlockSpec((ts, e_in), rev),
                  _full(modp.shape), _wspec4(w448, l), _full(cw.shape), rt, rt,
                  _full(dm.shape), _full(qd.shape), _full(kd.shape), _wspec4(w128, l)],
        out_specs=(tile, pl.BlockSpec((ts, e_in), rev), tile, tile, tile, _full((8, d))),
        scratch_shapes=[pltpu.VMEM((RET_HEADS, HEAD, HEAD), F32), pltpu.VMEM((8, cd), F32)],
        compiler_params=_params(1),
    )(x, dx1, y, states, zhalo, proj, modp, w448, cw, cos, sin, dm, qd, kd, w128)


def _odd_qkv_fwd(x, modp, w384, j, qkg, ts):
    s, d = x.shape
    n3 = N_DEV * w384.shape[2]

    def body(x_ref, mp_ref, w_ref, g_ref, o_ref):
        _, _, h = _normmod_fwd(x_ref[...], mp_ref[6:7, :], mp_ref[1:2, :], mp_ref[0:1, :])
        qkv = _mm_nt(_bf(h), w_ref[...].reshape(n3, d))
        for hh in range(SB_HEADS):
            lo = hh * HEAD
            qn, _ = _rms_fwd(qkv[:, lo:lo + HEAD])
            kn, _ = _rms_fwd(qkv[:, d + lo:d + lo + HEAD])
            o_ref[:, lo:lo + HEAD] = _bf(qn * g_ref[0:1, :])
            o_ref[:, d + lo:d + lo + HEAD] = _bf(kn * g_ref[1:2, :])
        o_ref[:, 2 * d:3 * d] = _bf(qkv[:, 2 * d:3 * d])

    return _pcall(
        body, name="odd_qkv_fwd", grid=(s // ts,),
        out_shape=jax.ShapeDtypeStruct((s, n3), BF16),
        in_specs=[pl.BlockSpec((ts, d), lambda i: (i, 0)), _full(modp.shape), _wspec4(w384, j),
                  _full(qkg.shape)],
        out_specs=pl.BlockSpec((ts, n3), lambda i: (i, 0)), compiler_params=_params(1),
    )(x, modp, w384, qkg)


SB_QUERIES = 512
SB_WIDE = 256
SB_LOOP_BLOCKS = 2


def _sb_logits(q, kw, mask):
    z = _mm_nt(q, kw) * (HEAD ** -0.5)
    e = jnp.exp(-jnp.abs(z))
    lb = jnp.minimum(z, 0.0) - jnp.log(1.0 + e)
    lk = lb - z
    if mask is not None:
        lk = jnp.where(mask, lk, 0.0)
    return lb, lk


def _tri(n, above):
    ri = lax.broadcasted_iota(jnp.int32, (n, n), 0)
    ci = lax.broadcasted_iota(jnp.int32, (n, n), 1)
    return ((ri > ci) if above else (ri < ci)).astype(BF16)


def _split_dot(a, tri):
    hi = _bf(a)
    lo = _bf(a - hi.astype(F32))
    return _mm(hi, tri) + _mm(lo, tri)


def _sb_fwd(qkv, tq):
    s = qkv.shape[0]
    d = qkv.shape[1] // 3
    nq = s // tq
    assert tq % SB_WIDE == 0
    parts = tq // SB_WIDE

    def body(q_ref, k_ref, v_ref, o_ref, t_ref, o_acc, run):
        qi = pl.program_id(1)
        base = qi * tq
        upper = _tri(SB_WIDE, True)
        o_acc[...] = jnp.zeros_like(o_acc)
        run[...] = jnp.zeros_like(run)

        def wide_step(ks, row0, masked, nblk):
            rows = slice(row0, tq)
            width = nblk * SB_WIDE
            mask = None
            if masked:
                qpos = base + row0 + lax.broadcasted_iota(jnp.int32, (tq - row0, width), 0)
                mask = qpos > ks + lax.broadcasted_iota(jnp.int32, (tq - row0, width), 1)
            lb, lk = _sb_logits(q_ref[rows, :], k_ref[pl.ds(ks, width), :], mask)
            blocks = [lk[:, b * SB_WIDE:(b + 1) * SB_WIDE] for b in range(nblk)]
            right = run[rows, :]
            accs = [None] * nblk
            for b in reversed(range(nblk)):
                accs[b] = _split_dot(blocks[b], upper) + right
                right = right + jnp.sum(blocks[b], axis=1, keepdims=True)
            w = jnp.exp(lb + (accs[0] if nblk == 1 else jnp.concatenate(accs, axis=1)))
            if masked:
                w = jnp.where(mask, w, 0.0)
            o_acc[rows, :] += _mm(_bf(w), v_ref[pl.ds(ks, width), :])
            run[rows, :] = right

        for part in reversed(range(parts)):
            wide_step(pl.multiple_of(base + part * SB_WIDE, SB_WIDE), part * SB_WIDE, True, 1)
        loop_width = SB_LOOP_BLOCKS * SB_WIDE
        nsteps = qi * (tq // loop_width)

        def step(it, carry):
            wide_step(pl.multiple_of((nsteps - 1 - it) * loop_width, loop_width), 0, False, SB_LOOP_BLOCKS)
            return carry

        lax.fori_loop(0, nsteps, step, 0)
        o_ref[...] = _bf(o_acc[...])
        t_ref[...] = run[...]

    nh = d // HEAD
    return _pcall(
        body, name="sb_fwd", grid=(nh, nq),
        out_shape=(jax.ShapeDtypeStruct((s, d), BF16), jax.ShapeDtypeStruct((nh, s, 1), F32)),
        in_specs=[pl.BlockSpec((tq, HEAD), lambda h, i: (i, h)),
                  pl.BlockSpec((s, HEAD), lambda h, i: (0, nh + h)),
                  pl.BlockSpec((s, HEAD), lambda h, i: (0, 2 * nh + h))],
        out_specs=(pl.BlockSpec((tq, HEAD), lambda h, i: (i, h)),
                   pl.BlockSpec((None, tq, 1), lambda h, i: (h, i, 0))),
        scratch_shapes=[pltpu.VMEM((tq, HEAD), F32), pltpu.VMEM((tq, 1), F32)],
        compiler_params=_params(2),
    )(qkv, qkv, qkv)


def _sb_bwd(qkv, do, tot, tq):
    s = qkv.shape[0]
    d = qkv.shape[1] // 3
    nq = s // tq
    scale = HEAD ** -0.5
    assert tq % SB_WIDE == 0
    parts = tq // SB_WIDE

    def body(q_ref, k_ref, v_ref, do_ref, t_ref, dq_ref, dk_ref, dv_ref, pk, pd):
        qi = pl.program_id(1)

        @pl.when(qi == 0)
        def _():
            dk_ref[...] = jnp.zeros_like(dk_ref)
            dv_ref[...] = jnp.zeros_like(dv_ref)
        base = qi * tq
        upper = _tri(SB_WIDE, True)
        lower = _tri(SB_WIDE, False)
        dq_ref[...] = jnp.zeros_like(dq_ref)
        pk[...] = jnp.zeros_like(pk)
        pd[...] = jnp.zeros_like(pd)

        def wide_step(ks, row0, masked, nblk):
            rows = slice(row0, tq)
            width = nblk * SB_WIDE
            cut = lambda a: [a[:, b * SB_WIDE:(b + 1) * SB_WIDE] for b in range(nblk)]
            join = lambda parts_: parts_[0] if nblk == 1 else jnp.concatenate(parts_, axis=1)
            mask = None
            if masked:
                qpos = base + row0 + lax.broadcasted_iota(jnp.int32, (tq - row0, width), 0)
                mask = qpos > ks + lax.broadcasted_iota(jnp.int32, (tq - row0, width), 1)
            kw = k_ref[pl.ds(ks, width), :]
            lb, lk = _sb_logits(q_ref[rows, :], kw, mask)
            left = pk[rows, :]
            total = t_ref[rows, :]
            accs = []
            for blk in cut(lk):
                left = left + jnp.sum(blk, axis=1, keepdims=True)
                accs.append(_split_dot(blk, upper) + (total - left))
            pk[rows, :] = left
            w = jnp.exp(lb + join(accs))
            if masked:
                w = jnp.where(mask, w, 0.0)
            de = _mm_nt(do_ref[rows, :], v_ref[pl.ds(ks, width), :]) * w
            before = pd[rows, :]
            dlks = []
            for blk in cut(de):
                dlks.append(_split_dot(blk, lower) + before)
                before = before + jnp.sum(blk, axis=1, keepdims=True)
            pd[rows, :] = before
            dz = (de - jnp.exp(lb) * (de + join(dlks))) * scale
            if masked:
                dz = jnp.where(mask, dz, 0.0)
            dzb = _bf(dz)
            dq_ref[rows, :] += _mm(dzb, kw)
            dv_ref[pl.ds(ks, width), :] += _mm_tn(_bf(w), do_ref[rows, :])
            dk_ref[pl.ds(ks, width), :] += _mm_tn(dzb, q_ref[rows, :])

        loop_width = SB_LOOP_BLOCKS * SB_WIDE

        def step(jb, carry):
            wide_step(pl.multiple_of(jb * loop_width, loop_width), 0, False, SB_LOOP_BLOCKS)
            return carry

        lax.fori_loop(0, qi * (tq // loop_width), step, 0)
        for part in range(parts):
            wide_step(pl.multiple_of(base + part * SB_WIDE, SB_WIDE), part * SB_WIDE, True, 1)

    nh = d // HEAD
    shp = jax.ShapeDtypeStruct((s, d), F32)
    return _pcall(
        body, name="sb_bwd", grid=(nh, nq), out_shape=(shp, shp, shp),
        in_specs=[pl.BlockSpec((tq, HEAD), lambda h, i: (i, h)),
                  pl.BlockSpec((s, HEAD), lambda h, i: (0, nh + h)),
                  pl.BlockSpec((s, HEAD), lambda h, i: (0, 2 * nh + h)),
                  pl.BlockSpec((tq, HEAD), lambda h, i: (i, h)),
                  pl.BlockSpec((None, tq, 1), lambda h, i: (h, i, 0))],
        out_specs=(pl.BlockSpec((tq, HEAD), lambda h, i: (i, h)),
                   pl.BlockSpec((s, HEAD), lambda h, i: (0, h)),
                   pl.BlockSpec((s, HEAD), lambda h, i: (0, h))),
        scratch_shapes=[pltpu.VMEM((tq, 1), F32), pltpu.VMEM((tq, 1), F32)],
        compiler_params=_params(2),
    )(qkv, qkv, qkv, do, tot)


def _odd_out_fwd(o, x, modp, w128, slot, ts):
    s, d = x.shape

    def body(o_ref, x_ref, mp_ref, w_ref, x1_ref, y_ref):
        y = _mm(o_ref[...], w_ref[...].reshape(d, d))
        y_ref[...] = y
        x1_ref[...] = x_ref[...] + mp_ref[2:3, :] * y

    tile = pl.BlockSpec((ts, d), lambda i: (i, 0))
    shp = jax.ShapeDtypeStruct((s, d), F32)
    return _pcall(
        body, name="odd_out_fwd", grid=(s // ts,), out_shape=(shp, shp),
        in_specs=[tile, tile, _full(modp.shape), _wspec4(w128, slot)],
        out_specs=(tile, tile), compiler_params=_params(1),
    )(o, x, modp, w128)


def _odd_out_bwd(dx1, y, modp, w128, slot, ts):
    s, d = dx1.shape

    def body(dx1_ref, y_ref, mp_ref, w_ref, do_ref, dy_ref, sg_ref):
        @pl.when(pl.program_id(0) == 0)
        def _():
            sg_ref[...] = jnp.zeros_like(sg_ref)
        dx1v = dx1_ref[...]
        dyb = _bf(mp_ref[2:3, :] * dx1v)
        dy_ref[...] = dyb
        do_ref[...] = _bf(_mm_nt(dyb, w_ref[...].reshape(d, d)))
        sg_ref[2:3, :] += _sum0(dx1v * y_ref[...])

    tile = pl.BlockSpec((ts, d), lambda i: (i, 0))
    bft = jax.ShapeDtypeStruct((s, d), BF16)
    return _pcall(
        body, name="odd_out_bwd", grid=(s // ts,),
        out_shape=(bft, bft, jax.ShapeDtypeStruct((8, d), F32)),
        in_specs=[tile, tile, _full(modp.shape), _wspec4(w128, slot)],
        out_specs=(tile, tile, _full((8, d))), compiler_params=_params(1),
    )(dx1, y, modp, w128)


def _odd_qkv_bwd(x, dx1, dq, dk, dv, sg_in, modp, w384, j, qkg, ts):
    s, d = x.shape
    n3 = N_DEV * w384.shape[2]

    def body(x_ref, dx1_ref, dq_ref, dk_ref, dv_ref, sgi_ref, mp_ref, w_ref, g_ref,
             dx_ref, dqkv_ref, h_ref, sg_ref):
        @pl.when(pl.program_id(0) == 0)
        def _():
            sg_ref[...] = sgi_ref[...]
        gmix, sc1 = mp_ref[6:7, :], mp_ref[1:2, :]
        n, rstd, h = _normmod_fwd(x_ref[...], gmix, sc1, mp_ref[0:1, :])
        hb = _bf(h)
        h_ref[...] = hb
        w = w_ref[...].reshape(n3, d)
        qkv = _mm_nt(hb, w)
        parts_q, parts_k = [], []
        gq, gk = g_ref[0:1, :], g_ref[1:2, :]
        dgq = jnp.zeros((1, HEAD), F32)
        dgk = jnp.zeros((1, HEAD), F32)
        for hh in range(SB_HEADS):
            lo = hh * HEAD
            qn, qr = _rms_fwd(qkv[:, lo:lo + HEAD])
            kn, kr = _rms_fwd(qkv[:, d + lo:d + lo + HEAD])
            dqn = dq_ref[:, lo:lo + HEAD]
            dkn = dk_ref[:, lo:lo + HEAD]
            dgq = dgq + _sum0(dqn * qn)
            dgk = dgk + _sum0(dkn * kn)
            parts_q.append(_rms_bwd(dqn * gq, qn, qr))
            parts_k.append(_rms_bwd(dkn * gk, kn, kr))
        dqkv = _bf(jnp.concatenate(parts_q + parts_k + [dv_ref[...]], axis=1))
        dqkv_ref[...] = dqkv
        dh = _mm(dqkv, w)
        dxn, dsh, dsc, dg = _normmod_bwd(dh, n, rstd, gmix, sc1)
        dx_ref[...] = dx1_ref[...] + dxn
        sg_ref[0:1, :] += dsh
        sg_ref[1:2, :] += dsc
        sg_ref[3:4, :] += dg
        sg_ref[4:5, 0:HEAD] += dgq
        sg_ref[5:6, 0:HEAD] += dgk

    tile = pl.BlockSpec((ts, d), lambda i: (i, 0))
    return _pcall(
        body, name="odd_qkv_bwd", grid=(s // ts,),
        out_shape=(jax.ShapeDtypeStruct((s, d), F32), jax.ShapeDtypeStruct((s, n3), BF16),
                   jax.ShapeDtypeStruct((s, d), BF16), jax.ShapeDtypeStruct((8, d), F32)),
        in_specs=[tile, tile, tile, tile, tile, _full((8, d)), _full(modp.shape), _wspec4(w384, j),
                  _full(qkg.shape)],
        out_specs=(tile, pl.BlockSpec((ts, n3), lambda i: (i, 0)), tile, _full((8, d))),
        compiler_params=_params(1),
    )(x, dx1, dq, dk, dv, sg_in, modp, w384, qkg)


def _pad_rows(a, rows):
    return jnp.concatenate([a, jnp.zeros((rows - a.shape[0],) + a.shape[1:], a.dtype)], axis=0)


def kernel(x, c, ada_w, ada_b, norm_mix_g, norm_ffn_g, ev_w_in, ev_conv_w, ev_ret_norm_g, ev_w_out, od_w_qkv, od_q_norm_g, od_k_norm_g, od_w_out, ffn_w_gate, ffn_w_up, ffn_w_down, loss_target, m_ada_w, m_ada_b, m_norm_mix_g, m_norm_ffn_g, m_ev_w_in, m_ev_conv_w, m_ev_ret_norm_g, m_ev_w_out, m_od_w_qkv, m_od_q_norm_g, m_od_k_norm_g, m_od_w_out, m_ffn_w_gate, m_ffn_w_up, m_ffn_w_down, v_ada_w, v_ada_b, v_norm_mix_g, v_norm_ffn_g, v_ev_w_in, v_ev_conv_w, v_ev_ret_norm_g, v_ev_w_out, v_od_w_qkv, v_od_q_norm_g, v_od_k_norm_g, v_od_w_out, v_ffn_w_gate, v_ffn_w_up, v_ffn_w_down):
    me = 4 * lax.axis_index("x") + 2 * lax.axis_index("y") + lax.axis_index("c")
    xs = x[0]
    tgt = loss_target[0]
    s, d = xs.shape
    depth = ada_w.shape[0]
    n_even, n_odd = ev_w_in.shape[0], od_w_qkv.shape[0]
    ts = 256
    tq = SB_QUERIES
    cd = 4 * HEAD
    cc = ev_conv_w.shape[2]

    pack0 = jnp.zeros((8, d), F32).at[0].set(c[0]).at[1, :n_even * 3 * cc].set(ev_conv_w.reshape(-1))
    got0, _ = _all_gather(pack0, "gather_cond")
    got0 = got0.reshape(N_DEV, 8, d)
    c_all = got0[:, 0, :]
    conv_all = got0[:, 1, :n_even * 3 * cc].reshape(N_DEV, n_even, 3, cc).transpose(1, 2, 0, 3)
    conv_all = conv_all.reshape(n_even, 3, N_DEV * cc)
    cols = ada_w.shape[2]
    ada_b_cols = lax.dynamic_slice(ada_b, (0, me * cols), (depth, cols))
    mod_cols = _ada_fwd(c_all, ada_w, ada_b_cols)
    got1, cond_done = _all_gather(mod_cols.reshape(depth * N_DEV, cols), "gather_mod")
    got1 = got1.reshape(N_DEV, depth, N_DEV, cols)
    mod = lax.dynamic_index_in_dim(got1, me, axis=2, keepdims=False)
    mod = mod.transpose(1, 0, 2).reshape(depth, 6, d)
    modps = [jnp.concatenate([mod[l], norm_mix_g[l][None], norm_ffn_g[l][None]], axis=0) for l in range(depth)]

    in_flight = []
    started = cond_done
    for l in range(depth):
        j = l // 2
        plain = lambda w: _bf(w + started)
        tr = lambda w: plain(w).T
        blocks = [tr(ev_w_in[j]), plain(ev_w_out[j])] if l % 2 == 0 else [tr(od_w_qkv[j]), plain(od_w_out[j])]
        mixer, started = _exchange_start(blocks, [_landing(b, me, False) for b in blocks], False,
                                         f"gather_start_mixer_{l}")
        blocks = [jnp.concatenate([tr(ffn_w_gate[l]), tr(ffn_w_up[l]), plain(ffn_w_down[l])], axis=0)]
        ffn, started = _exchange_start(blocks, [_landing(b, me, False) for b in blocks], False,
                                       f"gather_start_ffn_{l}")
        in_flight.append((mixer, ffn))
        modps[0] = modps[0] + started
    n_ffn = ffn_w_down.shape[1]

    def mixer_weights(l, after):
        got = _exchange_wait(*in_flight[l][0], False, [after], f"gather_wait_mixer_{l}")
        return got[0].reshape(N_DEV, 1, -1, d), got[1].reshape(N_DEV, 1, -1, d)

    def ffn_weights(l, after):
        got = _exchange_wait(*in_flight[l][1], False, [after], f"gather_wait_ffn_{l}")
        return got[0].reshape(N_DEV, 3, n_ffn, d)

    cos, sin = _rope_tables(s)
    consts = _retention_consts(ts)
    cws = [_pad_rows(jnp.concatenate([conv_all[j], ev_ret_norm_g[j][None]], axis=0), 8) for j in range(n_even)]
    qkgs = [_pad_rows(jnp.stack([od_q_norm_g[j], od_k_norm_g[j]]), 8) for j in range(n_odd)]

    saved = []
    weights = []
    cur = xs
    for l in range(depth):
        j = l // 2
        w_in, w_out = mixer_weights(l, cur)
        if l % 2 == 0:
            x1, y, states, zhalo, proj = _even_fwd(cur, modps[l], w_in, w_out, 0, cws[j], cos, sin, consts, ts)
            mix = (states, zhalo, proj)
        else:
            qkv = _odd_qkv_fwd(cur, modps[l], w_in, 0, qkgs[j], ts)
            o, tot = _sb_fwd(qkv, tq)
            x1, y = _odd_out_fwd(o, cur, modps[l], w_out, 0, ts)
            mix = (qkv, o, tot)
        w_ffn = ffn_weights(l, x1)
        weights.append((w_in, w_out, w_ffn))
        x2, f, ab = _ffn_fwd(x1, modps[l], w_ffn, 0, ts)
        saved.append((cur, x1, y, (f, ab), mix))
        cur = x2

    dx, loss_part = _loss_grad(cur, tgt, ts)
    loss = lax.psum(loss_part[0, 0], ("x", "y", "c"))

    dmod = [None] * depth
    d_gmix = [None] * depth
    d_gffn = [None] * depth
    d_conv = [None] * n_even
    d_retg = [None] * n_even
    d_qg = [None] * n_odd
    d_kg = [None] * n_odd
    grads_in_flight = [None] * depth
    for l in reversed(range(depth)):
        j = l // 2
        x0, x1, y, (f, ab), mix = saved[l]
        w_in, w_out, w_ffn = weights[l]
        g_ffn = lax.empty(w_ffn.shape, BF16)
        g_in = lax.empty(w_in.shape, BF16)
        g_out = lax.empty(w_out.shape, BF16)
        dx1, dab, h2, sv, df, sg2 = _ffn_bwd(x1, f, ab, dx, modps[l], w_ffn, 0, ts)
        g_ffn = _tn_matmul(dab, 0, h2, g_ffn, 0, "tn_gate")
        g_ffn = _tn_matmul(dab, 1, h2, g_ffn, 1, "tn_up")
        g_ffn = _tn_matmul(sv, 0, df, g_ffn, 2, "tn_down")
        if l % 2 == 0:
            states, zhalo, proj = mix
            dx, dproj, hb, mb, dyb, sg1 = _even_bwd(x0, dx1, y, states, zhalo, proj, modps[l], w_in, w_out, 0,
                                                    cws[j], cos, sin, consts, ts)
            g_in = _tn_matmul(dproj, 0, hb, g_in, 0, "tn_ev_in")
            g_out = _tn_matmul(mb, 0, dyb, g_out, 0, "tn_ev_out")
            d_conv[j] = sg1[4:7, :cd]
            d_retg[j] = sg1[7, :cd]
        else:
            qkv, o, tot = mix
            do, dyb, sg0 = _odd_out_bwd(dx1, y, modps[l], w_out, 0, ts)
            dq, dk, dv = _sb_bwd(qkv, do, tot, tq)
            dx, dqkv, hb, sg1 = _odd_qkv_bwd(x0, dx1, dq, dk, dv, sg0, modps[l], w_in, 0, qkgs[j], ts)
            g_in = _tn_matmul(dqkv, 0, hb, g_in, 0, "tn_od_qkv")
            g_out = _tn_matmul(o, 0, dyb, g_out, 0, "tn_od_out")
            d_qg[j] = sg1[4, :HEAD]
            d_kg[j] = sg1[5, :HEAD]
        pieces = [g.reshape(N_DEV, -1, d) for g in (g_ffn, g_in, g_out)]
        if l > 0:
            grads_in_flight[l], started = _exchange_start(pieces, [_landing(p, me, True) for p in pieces], True,
                                                          f"grads_start_{l}")
            modps[l - 1] = modps[l - 1] + started
        dmod[l] = jnp.concatenate([sg1[0:3], sg2[0:3]], axis=0).reshape(-1)
        d_gmix[l] = sg1[3]
        d_gffn[l] = sg2[3]

    small = jnp.concatenate(
        [jnp.stack(dmod).reshape(-1), jnp.stack(d_gmix).reshape(-1), jnp.stack(d_gffn).reshape(-1),
         jnp.stack(d_retg).reshape(-1), jnp.stack(d_qg).reshape(-1), jnp.stack(d_kg).reshape(-1),
         jnp.stack(d_conv).reshape(-1)])
    n_small = small.shape[0]
    rows_small = -(-n_small // (8 * 128)) * 8
    small = jnp.concatenate([small, jnp.zeros((rows_small * 128 - n_small,), F32)]).reshape(rows_small, 128)
    got2, small_done = _all_gather(small, "gather_small")
    got2 = got2.reshape(N_DEV, rows_small, 128)
    grads_in_flight[0], started = _exchange_start(pieces, [_landing(p, me, True, small_done) for p in pieces],
                                                  True, "grads_start_0")
    tot_small = _sum_small(got2).reshape(-1)
    n_mod = depth * 6 * d
    dmod_all = got2.reshape(N_DEV, -1)[:, :n_mod].reshape(N_DEV, depth, 6 * d)
    dmod_cols = lax.dynamic_slice(dmod_all, (0, 0, me * cols), (N_DEV, depth, cols)).transpose(1, 0, 2)
    g_ada_w = _ada_bwd(c_all.T, dmod_cols + started)

    off = [0]

    def take(shape):
        n = int(np.prod(shape))
        out = tot_small[off[0]:off[0] + n].reshape(shape)
        off[0] += n
        return out

    g_ada_b = take((depth, 6 * d))
    g_norm_mix = take((depth, d))
    g_norm_ffn = take((depth, d))
    g_ret_norm = take((n_even, cd))
    g_q_norm = take((n_odd, HEAD))
    g_k_norm = take((n_odd, HEAD))
    g_conv_full = take((n_even, 3, cd))
    g_conv = lax.dynamic_slice(g_conv_full, (0, 0, me * cc), (n_even, 3, cc))

    res = {"ada_w": (g_ada_w,) + _adamw_nd(ada_w, g_ada_w, m_ada_w, v_ada_w, "adamw_ada_w")}
    big = {"ev_w_in": (ev_w_in, m_ev_w_in, v_ev_w_in), "ev_w_out": (ev_w_out, m_ev_w_out, v_ev_w_out),
           "od_w_qkv": (od_w_qkv, m_od_w_qkv, v_od_w_qkv), "od_w_out": (od_w_out, m_od_w_out, v_od_w_out),
           "ffn_w_gate": (ffn_w_gate, m_ffn_w_gate, v_ffn_w_gate), "ffn_w_up": (ffn_w_up, m_ffn_w_up, v_ffn_w_up),
           "ffn_w_down": (ffn_w_down, m_ffn_w_down, v_ffn_w_down)}
    flipped = ("ev_w_in", "ffn_w_gate", "ffn_w_up")
    flip = lambda a: a.transpose(0, 2, 1)
    for name in flipped:
        big[name] = tuple(flip(a) for a in big[name])
    for name, (w, _, _) in big.items():
        res[name] = tuple(lax.empty(w.shape, F32) for _ in range(4))

    def update(name, idx, g_layer):
        w, m, v = big[name]
        res[name] = _adamw_layer(w, g_layer, m, v, res[name], idx, "adamw_" + name)

    after = [res["ada_w"][1]]
    for l in reversed(range(depth)):
        recv = _exchange_wait(*grads_in_flight[l], True, after, f"grads_wait_{l}")
        s_ffn, s_in, s_out = [_sum_slots(r, f"sum_{i}") for i, r in enumerate(recv)]
        j = l // 2
        update("ffn_w_gate", l, s_ffn[0:n_ffn])
        update("ffn_w_up", l, s_ffn[n_ffn:2 * n_ffn])
        update("ffn_w_down", l, s_ffn[2 * n_ffn:3 * n_ffn])
        if l % 2 == 0:
            update("ev_w_in", j, s_in)
        else:
            update("od_w_qkv", j, s_in.T)
        update("ev_w_out" if l % 2 == 0 else "od_w_out", j, s_out)
        after = [res[name][1] for name in big]
    for name in flipped:
        res[name] = tuple(flip(a) for a in res[name])

    smalls = [("ada_b", ada_b, g_ada_b, m_ada_b, v_ada_b), ("norm_mix_g", norm_mix_g, g_norm_mix, m_norm_mix_g, v_norm_mix_g),
              ("norm_ffn_g", norm_ffn_g, g_norm_ffn, m_norm_ffn_g, v_norm_ffn_g),
              ("ev_conv_w", ev_conv_w, g_conv, m_ev_conv_w, v_ev_conv_w),
              ("ev_ret_norm_g", ev_ret_norm_g, g_ret_norm, m_ev_ret_norm_g, v_ev_ret_norm_g),
              ("od_q_norm_g", od_q_norm_g, g_q_norm, m_od_q_norm_g, v_od_q_norm_g),
              ("od_k_norm_g", od_k_norm_g, g_k_norm, m_od_k_norm_g, v_od_k_norm_g)]

    def pack(arrs):
        flat = jnp.concatenate([a.reshape(-1) for a in arrs])
        rows = -(-flat.shape[0] // (8 * 128)) * 8
        return jnp.concatenate([flat, jnp.zeros((rows * 128 - flat.shape[0],), F32)]).reshape(rows, 128)

    sd, sm, sv_ = _adamw(pack([t[1] for t in smalls]), pack([t[2] for t in smalls]),
                         pack([t[3] for t in smalls]), pack([t[4] for t in smalls]), "adamw_small")
    sd, sm, sv_ = sd.reshape(-1), sm.reshape(-1), sv_.reshape(-1)
    pos = 0
    for name, w, g, m, v in smalls:
        n = int(np.prod(w.shape))
        res[name] = (g, sd[pos:pos + n].reshape(w.shape), sm[pos:pos + n].reshape(w.shape),
                     sv_[pos:pos + n].reshape(w.shape))
        pos += n

    order = ["ada_w", "ada_b", "norm_mix_g", "norm_ffn_g", "ev_w_in", "ev_conv_w", "ev_ret_norm_g", "ev_w_out",
             "od_w_qkv", "od_q_norm_g", "od_k_norm_g", "od_w_out", "ffn_w_gate", "ffn_w_up", "ffn_w_down"]
    outs = [loss, dx[None]]
    for k in range(4):
        outs += [res[name][k] for name in order]
    return tuple(outs)
```

```python
import functools
import math

import numpy as np
import jax
import jax.numpy as jnp
from jax import lax
from jax.experimental import pallas as pl
from jax.experimental.pallas import tpu as pltpu

F32 = jnp.float32
BF16 = jnp.bfloat16
MESH = pl.DeviceIdType.MESH

N_DEV = 8
EPS = 1e-6
CHUNK = 64
HEAD = 128
RET_HEADS = 4
SB_HEADS = 8
ROPE_THETA = 10000.0
KEY_BLOCK = 128
ADAM_LR, ADAM_B1, ADAM_B2, ADAM_EPS, ADAM_WD, ADAM_STEP = 0.001, 0.9, 0.999, 1e-08, 0.01, 10
VMEM_LIMIT = 56 * 1024 * 1024


def _pcall(body, **kw):
    return pl.pallas_call(body, **kw)


def _params(n_grid=1, vmem=VMEM_LIMIT):
    return pltpu.CompilerParams(dimension_semantics=("arbitrary",) * n_grid, vmem_limit_bytes=vmem)


def _mm(a, b):
    return jnp.dot(a, b, preferred_element_type=F32)


def _mm_nt(a, b):
    return lax.dot_general(a, b, (((1,), (1,)), ((), ())), preferred_element_type=F32)


def _mm_tn(a, b):
    return lax.dot_general(a, b, (((0,), (0,)), ((), ())), preferred_element_type=F32)


def _bf(a):
    return a.astype(BF16)


def _sigmoid(a):
    return 1.0 / (1.0 + jnp.exp(-a))


def _sum0(a):
    return jnp.sum(a, axis=0, keepdims=True)


def _full(shape):
    nd = len(shape)
    return pl.BlockSpec(shape, lambda *_: (0,) * nd)


def _normmod_fwd(x, g, sc, sh):
    rstd = lax.rsqrt(jnp.mean(x * x, axis=-1, keepdims=True) + EPS)
    n = x * rstd
    return n, rstd, (n * g) * (1.0 + sc) + sh


def _normmod_bwd(dh, n, rstd, g, sc):
    dsh = _sum0(dh)
    dsc = _sum0(dh * (n * g))
    dg = _sum0(dh * n * (1.0 + sc))
    dn = dh * (g * (1.0 + sc))
    dx = rstd * (dn - n * jnp.mean(dn * n, axis=-1, keepdims=True))
    return dx, dsh, dsc, dg


def _rms_fwd(o):
    rstd = lax.rsqrt(jnp.mean(o * o, axis=-1, keepdims=True) + EPS)
    return o * rstd, rstd


def _rms_bwd(dn, n, rstd):
    return rstd * (dn - n * jnp.mean(dn * n, axis=-1, keepdims=True))


def _all_gather(x2d, name):
    m_per, n = x2d.shape
    space = pltpu.VMEM

    def body(x_ref, out_ref, done_ref, send_sems, recv_sems, local_sem):
        x, y, c = lax.axis_index("x"), lax.axis_index("y"), lax.axis_index("c")
        me, sibling = (x, y, c), (x, y, 1 - c)
        chips = [(1 - x, y), (x, 1 - y), (1 - x, 1 - y)]

        def rows(px, py, pc):
            return out_ref.at[pl.ds((4 * px + 2 * py + pc) * m_per, m_per), :]

        def copy(k, block, to, src=None):
            return pltpu.make_async_remote_copy(
                src_ref=rows(*block) if src is None else src, dst_ref=rows(*block),
                send_sem=send_sems.at[k], recv_sem=recv_sems.at[k],
                device_id=to, device_id_type=MESH)

        mine = pltpu.make_async_copy(x_ref, rows(*me), local_sem)
        mine.start()
        first = [copy(1 + j, me, (*chip, c), src=x_ref) for j, chip in enumerate(chips)]
        first += [copy(0, me, sibling, src=x_ref)]
        for cp in first:
            cp.start()
        passed = [copy(4 + j, (*chip, c), sibling) for j, chip in enumerate(chips)]
        for j, chip in enumerate(chips):
            copy(1 + j, (*chip, c), me).wait_recv()
            passed[j].start()
        copy(0, sibling, me).wait_recv()
        for j, chip in enumerate(chips):
            copy(4 + j, (*chip, 1 - c), me).wait_recv()
        for cp in first + passed:
            cp.wait_send()
        mine.wait()
        done_ref[...] = jnp.zeros_like(done_ref)

    out, done = _pcall(
        body, name=name,
        out_shape=(jax.ShapeDtypeStruct((N_DEV * m_per, n), x2d.dtype), jax.ShapeDtypeStruct((8, 128), F32)),
        in_specs=[pl.BlockSpec(memory_space=space)],
        out_specs=(pl.BlockSpec(memory_space=space), pl.BlockSpec(memory_space=pltpu.VMEM)),
        scratch_shapes=[pltpu.SemaphoreType.DMA((7,)), pltpu.SemaphoreType.DMA((7,)),
                        pltpu.SemaphoreType.DMA],
    )(x2d)
    return out, done[0, 0]


_HBM = pl.BlockSpec(memory_space=pltpu.HBM)
_SEM = pl.BlockSpec(memory_space=pltpu.SEMAPHORE)
_EFFECT = pltpu.SideEffectType.DATAFLOW_SIDE_EFFECTING


def _exchange_copies(src_refs, land_refs, send_sems, recv_sems, scatter):
    x, y, c = lax.axis_index("x"), lax.axis_index("y"), lax.axis_index("c")
    me = 4 * x + 2 * y + c
    out = []
    for i, (s_ref, l_ref) in enumerate(zip(src_refs, land_refs)):
        for k in (2, 4, 6, 3, 5, 7, 1):
            px = (1 - x) if (k >> 2) & 1 else x
            py = (1 - y) if (k >> 1) & 1 else y
            pc = (1 - c) if k & 1 else c
            out.append(pltpu.make_async_remote_copy(
                src_ref=s_ref.at[4 * px + 2 * py + pc] if scatter else s_ref, dst_ref=l_ref.at[me],
                send_sem=send_sems.at[7 * i + k - 1], recv_sem=recv_sems.at[7 * i + k - 1],
                device_id=(px, py, pc), device_id_type=MESH))
    return out


def _exchange_start(srcs, lands, scatter, name):
    n = len(srcs)

    def body(*refs):
        for cp in _exchange_copies(refs[:n], refs[n:2 * n], refs[2 * n], refs[2 * n + 1], scatter):
            cp.start()
        refs[-1][...] = jnp.zeros_like(refs[-1])

    arrays = list(srcs) + list(lands)
    outs = _pcall(
        body, name=name,
        out_shape=(pltpu.SemaphoreType.DMA((7 * n,)), pltpu.SemaphoreType.DMA((7 * n,)),
                   *[pltpu.HBM(a.shape, a.dtype) for a in arrays], jax.ShapeDtypeStruct((8, 128), F32)),
        in_specs=[_HBM] * (2 * n),
        out_specs=(_SEM, _SEM, *[_HBM] * (2 * n), pl.BlockSpec(memory_space=pltpu.VMEM)),
        input_output_aliases={i: 2 + i for i in range(2 * n)},
        compiler_params=pltpu.CompilerParams(has_side_effects=_EFFECT),
    )(*[pltpu.with_memory_space_constraint(a, pltpu.HBM) for a in arrays])
    return (outs[0], outs[1], list(outs[2:2 + n]), list(outs[2 + n:2 + 2 * n])), outs[-1][0, 0]


def _exchange_wait(send_sems, recv_sems, srcs, lands, scatter, after, name):
    n = len(srcs)
    after = list(after)

    def body(*refs):
        for cp in _exchange_copies(refs[:n], refs[n:2 * n], refs[2 * n], refs[2 * n + 1], scatter):
            cp.wait_send()
            cp.wait_recv()

    arrays = list(srcs) + list(lands)
    outs = _pcall(
        body, name=name,
        out_shape=tuple(pltpu.HBM(a.shape, a.dtype) for a in arrays),
        in_specs=[_HBM] * (2 * n) + [_SEM, _SEM] + [pl.BlockSpec(memory_space=pl.ANY)] * len(after),
        out_specs=tuple([_HBM] * (2 * n)),
        input_output_aliases={i: i for i in range(2 * n)},
        compiler_params=pltpu.CompilerParams(has_side_effects=_EFFECT),
    )(*arrays, send_sems, recv_sems, *after)
    return list(outs[n:])


def _landing(src, me, scatter, after=None):
    own = lax.dynamic_index_in_dim(src, me, 0, keepdims=True) if scatter else src[None]
    if after is not None:
        own = own + after.astype(own.dtype)
    shape = src.shape if scatter else (N_DEV,) + src.shape
    return lax.dynamic_update_slice(lax.empty(shape, src.dtype), own, (me, 0, 0))


def _sum_slots(recv, name):
    _, r, n = recv.shape
    tr = r
    for cand in (512, 448, 384, 352, 256, 128, 64, 32, 16, 8):
        if r % cand == 0:
            tr = cand
            break

    def body(r_ref, o_ref):
        acc = r_ref[0].astype(F32)
        for p in range(1, N_DEV):
            acc = acc + r_ref[p].astype(F32)
        o_ref[...] = acc

    return _pcall(
        body, name=name, grid=(r // tr,),
        out_shape=jax.ShapeDtypeStruct((r, n), F32),
        in_specs=[pl.BlockSpec((N_DEV, tr, n), lambda i: (0, i, 0))],
        out_specs=pl.BlockSpec((tr, n), lambda i: (i, 0)),
        compiler_params=_params(1),
    )(recv)


def _row_tile(rows, limit=512):
    for cand in range(min(limit, rows) // 8 * 8, 7, -8):
        if rows % cand == 0:
            return cand
    return rows


def _adamw(w, g, m, v, name):
    r, n = w.shape
    tr = _row_tile(r)

    def body(w_ref, g_ref, m_ref, v_ref, d_ref, nm_ref, nv_ref):
        d_ref[...], nm_ref[...], nv_ref[...] = _adam_update(w_ref[...], g_ref[...], m_ref[...], v_ref[...])

    spec = pl.BlockSpec((tr, n), lambda i: (i, 0))
    shp = jax.ShapeDtypeStruct((r, n), F32)
    return _pcall(
        body, name=name, grid=(r // tr,), out_shape=(shp, shp, shp),
        in_specs=[spec] * 4, out_specs=(spec, spec, spec), compiler_params=_params(1),
    )(w, g, m, v)


def _adam_update(wv, gv, mv, vv):
    bc1 = 1.0 / (1.0 - ADAM_B1 ** ADAM_STEP)
    bc2 = 1.0 / (1.0 - ADAM_B2 ** ADAM_STEP)
    nm = ADAM_B1 * mv + (1.0 - ADAM_B1) * gv
    nv = ADAM_B2 * vv + (1.0 - ADAM_B2) * (gv * gv)
    return -ADAM_LR * ((nm * bc1) / (jnp.sqrt(nv * bc2) + ADAM_EPS) + ADAM_WD * wv), nm, nv


def _adamw_layer(w, g_layer, m, v, outs, idx, name):
    _, a, b = w.shape
    tr = _row_tile(a)

    def body(w_ref, g_ref, m_ref, v_ref, o0, o1, o2, o3, go_ref, d_ref, nm_ref, nv_ref):
        gv = g_ref[...]
        go_ref[...] = gv
        d_ref[...], nm_ref[...], nv_ref[...] = _adam_update(w_ref[...], gv, m_ref[...], v_ref[...])

    layer = pl.BlockSpec((None, tr, b), lambda i: (idx, i, 0))
    anyw = pl.BlockSpec(memory_space=pl.ANY)
    shp = jax.ShapeDtypeStruct(w.shape, F32)
    return tuple(_pcall(
        body, name=name, grid=(a // tr,), out_shape=(shp,) * 4,
        in_specs=[layer, pl.BlockSpec((tr, b), lambda i: (i, 0)), layer, layer, anyw, anyw, anyw, anyw],
        out_specs=(layer,) * 4, input_output_aliases={4: 0, 5: 1, 6: 2, 7: 3},
        compiler_params=_params(1),
    )(w, g_layer, m, v, *outs))


def _adamw_nd(w, g, m, v, name):
    shp = w.shape
    f = lambda a: a.reshape(-1, shp[-1])
    d, nm, nv = _adamw(f(w), f(g), f(m), f(v), name)
    return d.reshape(shp), nm.reshape(shp), nv.reshape(shp)


def _ada_fwd(c_all, ada_w, ada_b_cols):
    n_l, d, cols = ada_w.shape

    def body(c_ref, w_ref, b_ref, o_ref):
        cv = c_ref[...]
        ca = cv * _sigmoid(cv)
        o_ref[...] = _mm(_bf(ca), _bf(w_ref[...])) + b_ref[...]

    return _pcall(
        body, name="ada_fwd", grid=(n_l,),
        out_shape=jax.ShapeDtypeStruct((n_l, N_DEV, cols), F32),
        in_specs=[_full((N_DEV, d)), pl.BlockSpec((None, d, cols), lambda l: (l, 0, 0)),
                  pl.BlockSpec((None, 1, cols), lambda l: (l, 0, 0))],
        out_specs=pl.BlockSpec((None, N_DEV, cols), lambda l: (l, 0, 0)),
        compiler_params=_params(1),
    )(c_all, ada_w, ada_b_cols.reshape(n_l, 1, cols))


def _ada_bwd(c_all_t, dmod_cols):
    d = c_all_t.shape[0]
    n_l, _, cols = dmod_cols.shape

    def body(ct_ref, dm_ref, o_ref):
        cv = ct_ref[...]
        ca = cv * _sigmoid(cv)
        dm = dm_ref[...]
        acc = ca[:, 0:1] * dm[0:1, :]
        for b in range(1, N_DEV):
            acc = acc + ca[:, b:b + 1] * dm[b:b + 1, :]
        o_ref[...] = acc

    return _pcall(
        body, name="ada_bwd", grid=(n_l,),
        out_shape=jax.ShapeDtypeStruct((n_l, d, cols), F32),
        in_specs=[_full((d, N_DEV)), pl.BlockSpec((None, N_DEV, cols), lambda l: (l, 0, 0))],
        out_specs=pl.BlockSpec((None, d, cols), lambda l: (l, 0, 0)),
        compiler_params=_params(1),
    )(c_all_t, dmod_cols)


def _sum_small(gathered):
    _, r, n = gathered.shape

    def body(g_ref, o_ref):
        acc = g_ref[0]
        for p in range(1, N_DEV):
            acc = acc + g_ref[p]
        o_ref[...] = acc

    return _pcall(
        body, name="sum_small", out_shape=jax.ShapeDtypeStruct((r, n), F32),
        in_specs=[_full((N_DEV, r, n))], out_specs=_full((r, n)),
    )(gathered)


def _loss_grad(xf, tgt, ts):
    s, d = xf.shape

    def body(x_ref, t_ref, dx_ref, l_ref):
        @pl.when(pl.program_id(0) == 0)
        def _():
            l_ref[...] = jnp.zeros_like(l_ref)
        e = x_ref[...] - t_ref[...]
        dx_ref[...] = e * (1.0 / d)
        l_ref[...] += (0.5 / d) * jnp.sum(jnp.sum(e * e, axis=1, keepdims=True), axis=0, keepdims=True)

    spec = pl.BlockSpec((ts, d), lambda i: (i, 0))
    return _pcall(
        body, name="loss_grad", grid=(s // ts,),
        out_shape=(jax.ShapeDtypeStruct((s, d), F32), jax.ShapeDtypeStruct((1, 1), F32)),
        in_specs=[spec, spec], out_specs=(spec, _full((1, 1))), compiler_params=_params(1),
    )(xf, tgt)


def _tn_matmul(a, col_block, b, buf, slot, name):
    s = a.shape[0]
    k = b.shape[1]
    n_p = buf.shape[2]
    mcols = N_DEV * n_p
    ts = _row_tile(s, 1024)
    nt = s // ts

    def body(a_ref, b_ref, buf_ref, o_ref, acc):
        i = pl.program_id(0)

        @pl.when(i == 0)
        def _():
            acc[...] = jnp.zeros_like(acc)
        acc[...] += _mm_tn(a_ref[...], b_ref[...])

        @pl.when(i == nt - 1)
        def _():
            o_ref[...] = acc[...].reshape(N_DEV, n_p, k).astype(BF16)

    return _pcall(
        body, name=name, grid=(nt,),
        out_shape=jax.ShapeDtypeStruct(buf.shape, BF16),
        in_specs=[pl.BlockSpec((ts, mcols), lambda i: (i, col_block)),
                  pl.BlockSpec((ts, k), lambda i: (i, 0)),
                  pl.BlockSpec(memory_space=pl.ANY)],
        out_specs=pl.BlockSpec((N_DEV, None, n_p, k), lambda i: (0, slot, 0, 0)),
        scratch_shapes=[pltpu.VMEM((mcols, k), F32)],
        input_output_aliases={2: 0},
        compiler_params=_params(1),
    )(a, b, buf)


def _wspec4(w, slot):
    _, _, n_p, k = w.shape
    return pl.BlockSpec((N_DEV, None, n_p, k), lambda i: (0, slot, 0, 0), pipeline_mode=pl.Buffered(1))


def _ffn_fwd(x1, modp, w352, l, ts):
    s, d = x1.shape
    n_l = w352.shape[1] // 3
    f_dim = N_DEV * w352.shape[2]

    def body(x_ref, mp_ref, wg_ref, wu_ref, wd_ref, x2_ref, f_ref, ab_ref):
        x = x_ref[...]
        _, _, h2 = _normmod_fwd(x, mp_ref[7:8, :], mp_ref[4:5, :], mp_ref[3:4, :])
        hb = _bf(h2)
        f = jnp.zeros((ts, d), F32)
        half_dev, fc = N_DEV // 2, f_dim // 2
        for part in range(2):
            dev0, c0 = part * half_dev, part * fc
            a = _mm_nt(hb, wg_ref[dev0:dev0 + half_dev].reshape(fc, d))
            b = _mm_nt(hb, wu_ref[dev0:dev0 + half_dev].reshape(fc, d))
            ab_ref[:, c0:c0 + fc] = a
            ab_ref[:, f_dim + c0:f_dim + c0 + fc] = b
            sv = (a * _sigmoid(a)) * b
            f = f + _mm(_bf(sv), wd_ref[dev0:dev0 + half_dev].reshape(fc, d))
        f_ref[...] = f
        x2_ref[...] = x + mp_ref[5:6, :] * f

    tile = pl.BlockSpec((ts, d), lambda i: (i, 0))
    shp = jax.ShapeDtypeStruct((s, d), F32)
    return _pcall(
        body, name="ffn_fwd", grid=(s // ts,),
        out_shape=(shp, shp, jax.ShapeDtypeStruct((s, 2 * f_dim), F32)),
        in_specs=[tile, _full(modp.shape), _wspec4(w352, l), _wspec4(w352, n_l + l),
                  _wspec4(w352, 2 * n_l + l)],
        out_specs=(tile, tile, pl.BlockSpec((ts, 2 * f_dim), lambda i: (i, 0))), compiler_params=_params(1),
    )(x1, modp, w352, w352, w352)


def _ffn_bwd(x1, f, ab, dx2, modp, w352, l, ts):
    s, d = x1.shape
    n_l = w352.shape[1] // 3
    f_dim = N_DEV * w352.shape[2]

    def body(x_ref, f_ref, ab_ref, dx2_ref, mp_ref, wg_ref, wu_ref, wd_ref,
             dx1_ref, dab_ref, h2_ref, s_ref, df_ref, sg_ref):
        @pl.when(pl.program_id(0) == 0)
        def _():
            sg_ref[...] = jnp.zeros_like(sg_ref)
        x = x_ref[...]
        gffn, sc2, g2 = mp_ref[7:8, :], mp_ref[4:5, :], mp_ref[5:6, :]
        n, rstd, h2 = _normmod_fwd(x, gffn, sc2, mp_ref[3:4, :])
        hb = _bf(h2)
        dx2 = dx2_ref[...]
        dfb = _bf(g2 * dx2)
        dh2 = jnp.zeros((ts, d), F32)
        half_dev, fc = N_DEV // 2, f_dim // 2
        for part in range(2):
            dev0, c0 = part * half_dev, part * fc
            wg = wg_ref[dev0:dev0 + half_dev].reshape(fc, d)
            wu = wu_ref[dev0:dev0 + half_dev].reshape(fc, d)
            a = ab_ref[:, c0:c0 + fc]
            b = ab_ref[:, f_dim + c0:f_dim + c0 + fc]
            sig = _sigmoid(a)
            sa = a * sig
            s_ref[:, c0:c0 + fc] = _bf(sa * b)
            ds = _mm_nt(dfb, wd_ref[dev0:dev0 + half_dev].reshape(fc, d))
            dab = _bf(ds * b * (sig * (1.0 + a * (1.0 - sig))))
            dbb = _bf(ds * sa)
            dab_ref[:, c0:c0 + fc] = dab
            dab_ref[:, f_dim + c0:f_dim + c0 + fc] = dbb
            dh2 = dh2 + _mm(dab, wg) + _mm(dbb, wu)
        dxn, dsh, dsc, dg = _normmod_bwd(dh2, n, rstd, gffn, sc2)
        dx1_ref[...] = dx2 + dxn
        h2_ref[...] = hb
        df_ref[...] = dfb
        sg_ref[0:1, :] += dsh
        sg_ref[1:2, :] += dsc
        sg_ref[2:3, :] += _sum0(dx2 * f_ref[...])
        sg_ref[3:4, :] += dg

    tile = pl.BlockSpec((ts, d), lambda i: (i, 0))
    f32t = jax.ShapeDtypeStruct((s, d), F32)
    bft = jax.ShapeDtypeStruct((s, d), BF16)
    return _pcall(
        body, name="ffn_bwd", grid=(s // ts,),
        out_shape=(f32t, jax.ShapeDtypeStruct((s, 2 * f_dim), BF16), bft,
                   jax.ShapeDtypeStruct((s, f_dim), BF16), bft, jax.ShapeDtypeStruct((8, d), F32)),
        in_specs=[tile, tile, pl.BlockSpec((ts, 2 * f_dim), lambda i: (i, 0)), tile, _full(modp.shape),
                  _wspec4(w352, l), _wspec4(w352, n_l + l), _wspec4(w352, 2 * n_l + l)],
        out_specs=(tile, pl.BlockSpec((ts, 2 * f_dim), lambda i: (i, 0)), tile,
                   pl.BlockSpec((ts, f_dim), lambda i: (i, 0)), tile, _full((8, d))),
        compiler_params=_params(1),
    )(x1, f, ab, dx2, modp, w352, w352, w352)


def _retention_consts(ts):
    h = np.arange(RET_HEADS, dtype=np.float64)
    log_g = np.log1p(-np.exp2(-5.0 - h))
    t = np.arange(ts)
    diff = t[:, None] - t[None, :]
    same = (t[:, None] // CHUNK) == (t[None, :] // CHUNK)
    later = (t[:, None] // CHUNK) > (t[None, :] // CHUNK)
    dm = np.where(same, np.abs(diff), np.where(later, diff, 0))[None] * log_g[:, None, None]
    dm = np.where((same | later)[None], np.exp(dm), 0.0)
    qd = np.exp((t[:, None] + 1.0) * log_g[None, :])
    kd = np.exp((ts - 1.0 - t[:, None]) * log_g[None, :])
    qd = np.repeat(qd, HEAD, axis=1)
    kd = np.repeat(kd, HEAD, axis=1)
    tdec = [float(np.exp(ts * lg)) for lg in log_g]
    return (jnp.asarray(dm, F32), jnp.asarray(qd, F32), jnp.asarray(kd, F32), tdec)


def _rope_tables(s):
    inv_freq = 1.0 / (ROPE_THETA ** (jnp.arange(0, HEAD, 2, dtype=F32) / HEAD))
    ang = jnp.arange(s, dtype=F32)[:, None] * inv_freq[None, :]
    cos, sin = jnp.cos(ang), jnp.sin(ang)
    return jnp.concatenate([cos, cos], axis=1), jnp.concatenate([-sin, sin], axis=1)


def _rope(v, cos, sin):
    return v * cos + pltpu.roll(v, HEAD // 2, 1) * sin


def _rope_t(dv, cos, sin):
    return dv * cos + pltpu.roll(dv * sin, HEAD // 2, 1)


def _shift_down(z, k, halo_ref):
    r = pltpu.roll(z, k, 0)
    rows = lax.broadcasted_iota(jnp.int32, z.shape, 0)
    for j in range(k):
        r = jnp.where(rows == j, halo_ref[8 - k + j:8 - k + j + 1, :], r)
    return r


def _shift_up(z, k, halo_ref):
    n = z.shape[0]
    r = pltpu.roll(z, n - k, 0)
    rows = lax.broadcasted_iota(jnp.int32, z.shape, 0)
    for j in range(k):
        r = jnp.where(rows == n - k + j, halo_ref[j:j + 1, :], r)
    return r


def _even_recompute(x, mp_ref, win, cw_ref, cos, sin, dm_ref, qd_ref, kd_ref, halo_ref, state_of, proj=None):
    cd = 4 * HEAD
    n, rstd, h = _normmod_fwd(x, mp_ref[6:7, :], mp_ref[1:2, :], mp_ref[0:1, :])
    if proj is None:
        proj = _mm_nt(_bf(h), win)
    bg, cg, u = proj[:, 0:cd], proj[:, cd:2 * cd], proj[:, 2 * cd:3 * cd]
    z = cg * u
    z1 = _shift_down(z, 1, halo_ref)
    z2 = _shift_down(z, 2, halo_ref)
    conv = cw_ref[0:1, :] * z2 + cw_ref[1:2, :] * z1 + cw_ref[2:3, :] * z
    heads = []
    scale = HEAD ** -0.5
    for hh in range(RET_HEADS):
        lo = hh * HEAD
        q = proj[:, 3 * cd + lo:3 * cd + lo + HEAD]
        k = proj[:, 4 * cd + lo:4 * cd + lo + HEAD]
        v = proj[:, 5 * cd + lo:5 * cd + lo + HEAD]
        gate = proj[:, 6 * cd + lo:6 * cd + lo + HEAD]
        qr = _rope(q, cos, sin)
        kr = _rope(k, cos, sin) * scale
        sc = _mm_nt(_bf(qr), _bf(kr)) * dm_ref[hh]
        qs = qr * qd_ref[:, lo:lo + HEAD]
        ks = kr * kd_ref[:, lo:lo + HEAD]
        o = _mm(_bf(sc), _bf(v)) + _mm(_bf(qs), _bf(state_of(hh)))
        on, orstd = _rms_fwd(o)
        sig = _sigmoid(gate)
        heads.append(dict(qr=qr, kr=kr, v=v, gate=gate, sc=sc, qs=qs, ks=ks, on=on, orstd=orstd, sig=sig))
    return dict(n=n, rstd=rstd, h=h, proj=proj, bg=bg, cg=cg, u=u, z=z, z1=z1, z2=z2, conv=conv, heads=heads)


def _even_fwd(x, modp, w448, w128, l, cw, cos, sin, consts, ts):
    s, d = x.shape
    nt = s // ts
    dm, qd, kd, tdec = consts
    cd = 4 * HEAD
    e_in = N_DEV * w448.shape[2]

    def body(x_ref, mp_ref, win_ref, cw_ref, cos_ref, sin_ref, dm_ref, qd_ref, kd_ref, wout_ref,
             x1_ref, y_ref, st_ref, zh_ref, proj_ref, state, halo):
        @pl.when(pl.program_id(0) == 0)
        def _():
            state[...] = jnp.zeros_like(state)
            halo[...] = jnp.zeros_like(halo)
        xv = x_ref[...]
        st_ref[...] = state[...]
        zh_ref[...] = halo[...]
        r = _even_recompute(xv, mp_ref, win_ref[...].reshape(e_in, d), cw_ref, cos_ref[...], sin_ref[...],
                            dm_ref, qd_ref, kd_ref, halo, lambda hh: state[hh])
        proj_ref[...] = r["proj"]
        halo[...] = r["z"][ts - 8:ts, :]
        parts = [r["bg"] * r["conv"]]
        for hh, hd in enumerate(r["heads"]):
            state[hh] = state[hh] * tdec[hh] + _mm_tn(_bf(hd["ks"]), _bf(hd["v"]))
            rg = cw_ref[3:4, hh * HEAD:(hh + 1) * HEAD]
            parts.append((hd["gate"] * hd["sig"]) * (hd["on"] * rg))
        mcat = jnp.concatenate(parts, axis=1)
        y = _mm(_bf(mcat), wout_ref[...].reshape(d, d))
        y_ref[...] = y
        x1_ref[...] = xv + mp_ref[2:3, :] * y

    tile = pl.BlockSpec((ts, d), lambda i: (i, 0))
    rt = pl.BlockSpec((ts, HEAD), lambda i: (i, 0))
    shp = jax.ShapeDtypeStruct((s, d), F32)
    return _pcall(
        body, name="even_fwd", grid=(nt,),
        out_shape=(shp, shp, jax.ShapeDtypeStruct((nt, RET_HEADS, HEAD, HEAD), F32),
                   jax.ShapeDtypeStruct((nt, 8, cd), F32), jax.ShapeDtypeStruct((s, e_in), F32)),
        in_specs=[tile, _full(modp.shape), _wspec4(w448, l), _full(cw.shape), rt, rt,
                  _full(dm.shape), _full(qd.shape), _full(kd.shape), _wspec4(w128, l)],
        out_specs=(tile, tile, pl.BlockSpec((None, RET_HEADS, HEAD, HEAD), lambda i: (i, 0, 0, 0)),
                   pl.BlockSpec((None, 8, cd), lambda i: (i, 0, 0)), pl.BlockSpec((ts, e_in), lambda i: (i, 0))),
        scratch_shapes=[pltpu.VMEM((RET_HEADS, HEAD, HEAD), F32), pltpu.VMEM((8, cd), F32)],
        compiler_params=_params(1),
    )(x, modp, w448, cw, cos, sin, dm, qd, kd, w128)


def _even_bwd(x, dx1, y, states, zhalo, proj, modp, w448, w128, l, cw, cos, sin, consts, ts):
    s, d = x.shape
    nt = s // ts
    dm, qd, kd, tdec = consts
    cd = 4 * HEAD
    e_in = N_DEV * w448.shape[2]
    scale = HEAD ** -0.5

    def body(x_ref, dx1_ref, y_ref, st_ref, zh_ref, proj_ref, mp_ref, win_ref, cw_ref, cos_ref, sin_ref,
             dm_ref, qd_ref, kd_ref, wout_ref,
             dx_ref, dproj_ref, h_ref, m_ref, dy_ref, sg_ref, gstate, halo_d):
        @pl.when(pl.program_id(0) == 0)
        def _():
            gstate[...] = jnp.zeros_like(gstate)
            halo_d[...] = jnp.zeros_like(halo_d)
            sg_ref[...] = jnp.zeros_like(sg_ref)
        xv = x_ref[...]
        cos, sin = cos_ref[...], sin_ref[...]
        win = win_ref[...].reshape(e_in, d)
        r = _even_recompute(xv, mp_ref, win, cw_ref, cos, sin, dm_ref, qd_ref, kd_ref, zh_ref,
                            lambda hh: st_ref[hh], proj_ref[...])
        parts = [r["bg"] * r["conv"]]
        for hh, hd in enumerate(r["heads"]):
            rg = cw_ref[3:4, hh * HEAD:(hh + 1) * HEAD]
            parts.append((hd["gate"] * hd["sig"]) * (hd["on"] * rg))
        m_ref[...] = _bf(jnp.concatenate(parts, axis=1))
        h_ref[...] = _bf(r["h"])

        dx1 = dx1_ref[...]
        dy = mp_ref[2:3, :] * dx1
        dyb = _bf(dy)
        dy_ref[...] = dyb
        sg_ref[2:3, :] += _sum0(dx1 * y_ref[...])
        dmix = _mm_nt(dyb, wout_ref[...].reshape(d, d))

        da_out = dmix[:, 0:cd]
        dbg = da_out * r["conv"]
        dconv = da_out * r["bg"]
        dc1 = _shift_up(dconv, 1, halo_d)
        dc2 = _shift_up(dconv, 2, halo_d)
        dz = cw_ref[2:3, :] * dconv + cw_ref[1:2, :] * dc1 + cw_ref[0:1, :] * dc2
        halo_d[...] = dconv[0:8, :]
        sg_ref[4:5, 0:cd] += _sum0(dconv * r["z2"])
        sg_ref[5:6, 0:cd] += _sum0(dconv * r["z1"])
        sg_ref[6:7, 0:cd] += _sum0(dconv * r["z"])
        dcg = dz * r["u"]
        du = dz * r["cg"]

        dqs, dks, dvs, dgs = [], [], [], []
        for hh, hd in enumerate(r["heads"]):
            lo = hh * HEAD
            rg = cw_ref[3:4, lo:lo + HEAD]
            dr = dmix[:, cd + lo:cd + lo + HEAD]
            sig, gate, on = hd["sig"], hd["gate"], hd["on"]
            rn = on * rg
            dgate = dr * rn * (sig * (1.0 + gate * (1.0 - sig)))
            drn = dr * (gate * sig)
            sg_ref[7:8, lo:lo + HEAD] += _sum0(drn * on)
            do = _rms_bwd(drn * rg, on, hd["orstd"])
            dob = _bf(do)
            gst = _bf(gstate[hh])
            scb = _bf(hd["sc"])
            vb = _bf(hd["v"])
            qrb, krb = _bf(hd["qr"]), _bf(hd["kr"])
            dv = _mm_tn(scb, dob) + _mm(_bf(hd["ks"]), gst)
            dsc = _bf(_mm_nt(dob, vb) * dm_ref[hh])
            dqr = _mm(dsc, krb) + _mm_nt(dob, _bf(st_ref[hh])) * qd_ref[:, lo:lo + HEAD]
            dkr = _mm_tn(dsc, qrb) + _mm_nt(vb, gst) * kd_ref[:, lo:lo + HEAD]
            gstate[hh] = gstate[hh] * tdec[hh] + _mm_tn(_bf(hd["qs"]), dob)
            dqs.append(_rope_t(dqr, cos, sin))
            dks.append(_rope_t(dkr * scale, cos, sin))
            dvs.append(dv)
            dgs.append(dgate)

        dproj = _bf(jnp.concatenate([dbg, dcg, du] + dqs + dks + dvs + dgs, axis=1))
        dproj_ref[...] = dproj
        dh = _mm(dproj, win)
        dxn, dsh, dsc1, dg = _normmod_bwd(dh, r["n"], r["rstd"], mp_ref[6:7, :], mp_ref[1:2, :])
        dx_ref[...] = dx1 + dxn
        sg_ref[0:1, :] += dsh
        sg_ref[1:2, :] += dsc1
        sg_ref[3:4, :] += dg

    rev = lambda i: (nt - 1 - i, 0)
    tile = pl.BlockSpec((ts, d), rev)
    rt = pl.BlockSpec((ts, HEAD), rev)
    bft = jax.ShapeDtypeStruct((s, d), BF16)
    return _pcall(
        body, name="even_bwd", grid=(nt,),
        out_shape=(jax.ShapeDtypeStruct((s, d), F32), jax.ShapeDtypeStruct((s, e_in), BF16), bft, bft, bft,
                   jax.ShapeDtypeStruct((8, d), F32)),
        in_specs=[tile, tile, tile,
                  pl.BlockSpec((None, RET_HEADS, HEAD, HEAD), lambda i: (nt - 1 - i, 0, 0, 0)),
                  pl.BlockSpec((None, 8, cd), lambda i: (nt - 1 - i, 0, 0)), pl.BlockSpec((ts, e_in), rev),
                  _full(modp.shape), _wspec4(w448, l), _full(cw.shape), rt, rt,
                  _full(dm.shape), _full(qd.shape), _full(kd.shape), _wspec4(w128, l)],
        out_specs=(tile, pl.BlockSpec((ts, e_in), rev), tile, tile, tile, _full((8, d))),
        scratch_shapes=[pltpu.VMEM((RET_HEADS, HEAD, HEAD), F32), pltpu.VMEM((8, cd), F32)],
        compiler_params=_params(1),
    )(x, dx1, y, states, zhalo, proj, modp, w448, cw, cos, sin, dm, qd, kd, w128)


def _odd_qkv_fwd(x, modp, w384, j, qkg, ts):
    s, d = x.shape
    n3 = N_DEV * w384.shape[2]

    def body(x_ref, mp_ref, w_ref, g_ref, o_ref, pre_ref):
        _, _, h = _normmod_fwd(x_ref[...], mp_ref[6:7, :], mp_ref[1:2, :], mp_ref[0:1, :])
        qkv = _mm_nt(_bf(h), w_ref[...].reshape(n3, d))
        pre_ref[...] = qkv[:, 0:2 * d]
        for hh in range(SB_HEADS):
            lo = hh * HEAD
            qn, _ = _rms_fwd(qkv[:, lo:lo + HEAD])
            kn, _ = _rms_fwd(qkv[:, d + lo:d + lo + HEAD])
            o_ref[:, lo:lo + HEAD] = _bf(qn * g_ref[0:1, :])
            o_ref[:, d + lo:d + lo + HEAD] = _bf(kn * g_ref[1:2, :])
        o_ref[:, 2 * d:3 * d] = _bf(qkv[:, 2 * d:3 * d])

    return _pcall(
        body, name="odd_qkv_fwd", grid=(s // ts,),
        out_shape=(jax.ShapeDtypeStruct((s, n3), BF16), jax.ShapeDtypeStruct((s, 2 * d), F32)),
        in_specs=[pl.BlockSpec((ts, d), lambda i: (i, 0)), _full(modp.shape), _wspec4(w384, j),
                  _full(qkg.shape)],
        out_specs=(pl.BlockSpec((ts, n3), lambda i: (i, 0)), pl.BlockSpec((ts, 2 * d), lambda i: (i, 0))),
        compiler_params=_params(1),
    )(x, modp, w384, qkg)


SB_QUERIES = 512
SB_WIDE = 256
SB_LOOP_BLOCKS = 2


def _sb_logits(q, kw, mask):
    z = _mm_nt(q, kw) * (HEAD ** -0.5)
    e = jnp.exp(-jnp.abs(z))
    lb = jnp.minimum(z, 0.0) - jnp.log(1.0 + e)
    lk = lb - z
    if mask is not None:
        lk = jnp.where(mask, lk, 0.0)
    return lb, lk


def _tri(n, above):
    ri = lax.broadcasted_iota(jnp.int32, (n, n), 0)
    ci = lax.broadcasted_iota(jnp.int32, (n, n), 1)
    return ((ri > ci) if above else (ri < ci)).astype(BF16)


def _split_dot(a, tri):
    hi = _bf(a)
    lo = _bf(a - hi.astype(F32))
    return _mm(hi, tri) + _mm(lo, tri)


def _sb_fwd(qkv, tq):
    s = qkv.shape[0]
    d = qkv.shape[1] // 3
    nq = s // tq
    assert tq % SB_WIDE == 0
    parts = tq // SB_WIDE

    def body(q_ref, k_ref, v_ref, o_ref, t_ref, o_acc, run):
        qi = pl.program_id(1)
        base = qi * tq
        upper = _tri(SB_WIDE, True)
        o_acc[...] = jnp.zeros_like(o_acc)
        run[...] = jnp.zeros_like(run)

        def wide_step(ks, row0, masked, nblk):
            rows = slice(row0, tq)
            width = nblk * SB_WIDE
            mask = None
            if masked:
                qpos = base + row0 + lax.broadcasted_iota(jnp.int32, (tq - row0, width), 0)
                mask = qpos > ks + lax.broadcasted_iota(jnp.int32, (tq - row0, width), 1)
            lb, lk = _sb_logits(q_ref[rows, :], k_ref[pl.ds(ks, width), :], mask)
            blocks = [lk[:, b * SB_WIDE:(b + 1) * SB_WIDE] for b in range(nblk)]
            right = run[rows, :]
            accs = [None] * nblk
            for b in reversed(range(nblk)):
                accs[b] = _split_dot(blocks[b], upper) + right
                right = right + jnp.sum(blocks[b], axis=1, keepdims=True)
            w = jnp.exp(lb + (accs[0] if nblk == 1 else jnp.concatenate(accs, axis=1)))
            if masked:
                w = jnp.where(mask, w, 0.0)
            o_acc[rows, :] += _mm(_bf(w), v_ref[pl.ds(ks, width), :])
            run[rows, :] = right

        for part in reversed(range(parts)):
            wide_step(pl.multiple_of(base + part * SB_WIDE, SB_WIDE), part * SB_WIDE, True, 1)
        loop_width = SB_LOOP_BLOCKS * SB_WIDE
        nsteps = qi * (tq // loop_width)

        def step(it, carry):
            wide_step(pl.multiple_of((nsteps - 1 - it) * loop_width, loop_width), 0, False, SB_LOOP_BLOCKS)
            return carry

        lax.fori_loop(0, nsteps, step, 0)
        o_ref[...] = _bf(o_acc[...])
        t_ref[...] = run[...]

    nh = d // HEAD
    return _pcall(
        body, name="sb_fwd", grid=(nh, nq),
        out_shape=(jax.ShapeDtypeStruct((s, d), BF16), jax.ShapeDtypeStruct((nh, s, 1), F32)),
        in_specs=[pl.BlockSpec((tq, HEAD), lambda h, i: (i, h)),
                  pl.BlockSpec((s, HEAD), lambda h, i: (0, nh + h)),
                  pl.BlockSpec((s, HEAD), lambda h, i: (0, 2 * nh + h))],
        out_specs=(pl.BlockSpec((tq, HEAD), lambda h, i: (i, h)),
                   pl.BlockSpec((None, tq, 1), lambda h, i: (h, i, 0))),
        scratch_shapes=[pltpu.VMEM((tq, HEAD), F32), pltpu.VMEM((tq, 1), F32)],
        compiler_params=_params(2),
    )(qkv, qkv, qkv)


def _sb_bwd(qkv, do, tot, tq):
    s = qkv.shape[0]
    d = qkv.shape[1] // 3
    nq = s // tq
    scale = HEAD ** -0.5
    assert tq % SB_WIDE == 0
    parts = tq // SB_WIDE

    def body(q_ref, k_ref, v_ref, do_ref, t_ref, dq_ref, dk_ref, dv_ref, pk, pd):
        qi = pl.program_id(1)

        @pl.when(qi == 0)
        def _():
            dk_ref[...] = jnp.zeros_like(dk_ref)
            dv_ref[...] = jnp.zeros_like(dv_ref)
        base = qi * tq
        upper = _tri(SB_WIDE, True)
        lower = _tri(SB_WIDE, False)
        dq_ref[...] = jnp.zeros_like(dq_ref)
        pk[...] = jnp.zeros_like(pk)
        pd[...] = jnp.zeros_like(pd)

        def wide_step(ks, row0, masked, nblk):
            rows = slice(row0, tq)
            width = nblk * SB_WIDE
            cut = lambda a: [a[:, b * SB_WIDE:(b + 1) * SB_WIDE] for b in range(nblk)]
            join = lambda parts_: parts_[0] if nblk == 1 else jnp.concatenate(parts_, axis=1)
            mask = None
            if masked:
                qpos = base + row0 + lax.broadcasted_iota(jnp.int32, (tq - row0, width), 0)
                mask = qpos > ks + lax.broadcasted_iota(jnp.int32, (tq - row0, width), 1)
            kw = k_ref[pl.ds(ks, width), :]
            lb, lk = _sb_logits(q_ref[rows, :], kw, mask)
            left = pk[rows, :]
            total = t_ref[rows, :]
            accs = []
            for blk in cut(lk):
                left = left + jnp.sum(blk, axis=1, keepdims=True)
                accs.append(_split_dot(blk, upper) + (total - left))
            pk[rows, :] = left
            w = jnp.exp(lb + join(accs))
            if masked:
                w = jnp.where(mask, w, 0.0)
            de = _mm_nt(do_ref[rows, :], v_ref[pl.ds(ks, width), :]) * w
            before = pd[rows, :]
            dlks = []
            for blk in cut(de):
                dlks.append(_split_dot(blk, lower) + before)
                before = before + jnp.sum(blk, axis=1, keepdims=True)
            pd[rows, :] = before
            dz = (de - jnp.exp(lb) * (de + join(dlks))) * scale
            if masked:
                dz = jnp.where(mask, dz, 0.0)
            dzb = _bf(dz)
            dq_ref[rows, :] += _mm(dzb, kw)
            dv_ref[pl.ds(ks, width), :] += _mm_tn(_bf(w), do_ref[rows, :])
            dk_ref[pl.ds(ks, width), :] += _mm_tn(dzb, q_ref[rows, :])

        loop_width = SB_LOOP_BLOCKS * SB_WIDE

        def step(jb, carry):
            wide_step(pl.multiple_of(jb * loop_width, loop_width), 0, False, SB_LOOP_BLOCKS)
            return carry

        lax.fori_loop(0, qi * (tq // loop_width), step, 0)
        for part in range(parts):
            wide_step(pl.multiple_of(base + part * SB_WIDE, SB_WIDE), part * SB_WIDE, True, 1)

    nh = d // HEAD
    shp = jax.ShapeDtypeStruct((s, d), F32)
    return _pcall(
        body, name="sb_bwd", grid=(nh, nq), out_shape=(shp, shp, shp),
        in_specs=[pl.BlockSpec((tq, HEAD), lambda h, i: (i, h)),
                  pl.BlockSpec((s, HEAD), lambda h, i: (0, nh + h)),
                  pl.BlockSpec((s, HEAD), lambda h, i: (0, 2 * nh + h)),
                  pl.BlockSpec((tq, HEAD), lambda h, i: (i, h)),
                  pl.BlockSpec((None, tq, 1), lambda h, i: (h, i, 0))],
        out_specs=(pl.BlockSpec((tq, HEAD), lambda h, i: (i, h)),
                   pl.BlockSpec((s, HEAD), lambda h, i: (0, h)),
                   pl.BlockSpec((s, HEAD), lambda h, i: (0, h))),
        scratch_shapes=[pltpu.VMEM((tq, 1), F32), pltpu.VMEM((tq, 1), F32)],
        compiler_params=_params(2),
    )(qkv, qkv, qkv, do, tot)


def _odd_out_fwd(o, x, modp, w128, slot, ts):
    s, d = x.shape

    def body(o_ref, x_ref, mp_ref, w_ref, x1_ref, y_ref):
        y = _mm(o_ref[...], w_ref[...].reshape(d, d))
        y_ref[...] = y
        x1_ref[...] = x_ref[...] + mp_ref[2:3, :] * y

    tile = pl.BlockSpec((ts, d), lambda i: (i, 0))
    shp = jax.ShapeDtypeStruct((s, d), F32)
    return _pcall(
        body, name="odd_out_fwd", grid=(s // ts,), out_shape=(shp, shp),
        in_specs=[tile, tile, _full(modp.shape), _wspec4(w128, slot)],
        out_specs=(tile, tile), compiler_params=_params(1),
    )(o, x, modp, w128)


def _odd_out_bwd(dx1, y, modp, w128, slot, ts):
    s, d = dx1.shape

    def body(dx1_ref, y_ref, mp_ref, w_ref, do_ref, dy_ref, sg_ref):
        @pl.when(pl.program_id(0) == 0)
        def _():
            sg_ref[...] = jnp.zeros_like(sg_ref)
        dx1v = dx1_ref[...]
        dyb = _bf(mp_ref[2:3, :] * dx1v)
        dy_ref[...] = dyb
        do_ref[...] = _bf(_mm_nt(dyb, w_ref[...].reshape(d, d)))
        sg_ref[2:3, :] += _sum0(dx1v * y_ref[...])

    tile = pl.BlockSpec((ts, d), lambda i: (i, 0))
    bft = jax.ShapeDtypeStruct((s, d), BF16)
    return _pcall(
        body, name="odd_out_bwd", grid=(s // ts,),
        out_shape=(bft, bft, jax.ShapeDtypeStruct((8, d), F32)),
        in_specs=[tile, tile, _full(modp.shape), _wspec4(w128, slot)],
        out_specs=(tile, tile, _full((8, d))), compiler_params=_params(1),
    )(dx1, y, modp, w128)


def _odd_qkv_bwd(x, dx1, dq, dk, dv, pre, sg_in, modp, w384, j, qkg, ts):
    s, d = x.shape
    n3 = N_DEV * w384.shape[2]

    def body(x_ref, dx1_ref, dq_ref, dk_ref, dv_ref, pre_ref, sgi_ref, mp_ref, w_ref, g_ref,
             dx_ref, dqkv_ref, h_ref, sg_ref):
        @pl.when(pl.program_id(0) == 0)
        def _():
            sg_ref[...] = sgi_ref[...]
        gmix, sc1 = mp_ref[6:7, :], mp_ref[1:2, :]
        n, rstd, h = _normmod_fwd(x_ref[...], gmix, sc1, mp_ref[0:1, :])
        hb = _bf(h)
        h_ref[...] = hb
        w = w_ref[...].reshape(n3, d)
        qkv = pre_ref[...]
        parts_q, parts_k = [], []
        gq, gk = g_ref[0:1, :], g_ref[1:2, :]
        dgq = jnp.zeros((1, HEAD), F32)
        dgk = jnp.zeros((1, HEAD), F32)
        for hh in range(SB_HEADS):
            lo = hh * HEAD
            qn, qr = _rms_fwd(qkv[:, lo:lo + HEAD])
            kn, kr = _rms_fwd(qkv[:, d + lo:d + lo + HEAD])
            dqn = dq_ref[:, lo:lo + HEAD]
            dkn = dk_ref[:, lo:lo + HEAD]
            dgq = dgq + _sum0(dqn * qn)
            dgk = dgk + _sum0(dkn * kn)
            parts_q.append(_rms_bwd(dqn * gq, qn, qr))
            parts_k.append(_rms_bwd(dkn * gk, kn, kr))
        dqkv = _bf(jnp.concatenate(parts_q + parts_k + [dv_ref[...]], axis=1))
        dqkv_ref[...] = dqkv
        dh = _mm(dqkv, w)
        dxn, dsh, dsc, dg = _normmod_bwd(dh, n, rstd, gmix, sc1)
        dx_ref[...] = dx1_ref[...] + dxn
        sg_ref[0:1, :] += dsh
        sg_ref[1:2, :] += dsc
        sg_ref[3:4, :] += dg
        sg_ref[4:5, 0:HEAD] += dgq
        sg_ref[5:6, 0:HEAD] += dgk

    tile = pl.BlockSpec((ts, d), lambda i: (i, 0))
    return _pcall(
        body, name="odd_qkv_bwd", grid=(s // ts,),
        out_shape=(jax.ShapeDtypeStruct((s, d), F32), jax.ShapeDtypeStruct((s, n3), BF16),
                   jax.ShapeDtypeStruct((s, d), BF16), jax.ShapeDtypeStruct((8, d), F32)),
        in_specs=[tile, tile, tile, tile, tile, pl.BlockSpec((ts, 2 * d), lambda i: (i, 0)), _full((8, d)),
                  _full(modp.shape), _wspec4(w384, j), _full(qkg.shape)],
        out_specs=(tile, pl.BlockSpec((ts, n3), lambda i: (i, 0)), tile, _full((8, d))),
        compiler_params=_params(1),
    )(x, dx1, dq, dk, dv, pre, sg_in, modp, w384, qkg)


def _pad_rows(a, rows):
    return jnp.concatenate([a, jnp.zeros((rows - a.shape[0],) + a.shape[1:], a.dtype)], axis=0)


def kernel(x, c, ada_w, ada_b, norm_mix_g, norm_ffn_g, ev_w_in, ev_conv_w, ev_ret_norm_g, ev_w_out, od_w_qkv, od_q_norm_g, od_k_norm_g, od_w_out, ffn_w_gate, ffn_w_up, ffn_w_down, loss_target, m_ada_w, m_ada_b, m_norm_mix_g, m_norm_ffn_g, m_ev_w_in, m_ev_conv_w, m_ev_ret_norm_g, m_ev_w_out, m_od_w_qkv, m_od_q_norm_g, m_od_k_norm_g, m_od_w_out, m_ffn_w_gate, m_ffn_w_up, m_ffn_w_down, v_ada_w, v_ada_b, v_norm_mix_g, v_norm_ffn_g, v_ev_w_in, v_ev_conv_w, v_ev_ret_norm_g, v_ev_w_out, v_od_w_qkv, v_od_q_norm_g, v_od_k_norm_g, v_od_w_out, v_ffn_w_gate, v_ffn_w_up, v_ffn_w_down):
    me = 4 * lax.axis_index("x") + 2 * lax.axis_index("y") + lax.axis_index("c")
    xs = x[0]
    tgt = loss_target[0]
    s, d = xs.shape
    depth = ada_w.shape[0]
    n_even, n_odd = ev_w_in.shape[0], od_w_qkv.shape[0]
    ts = 256
    tq = SB_QUERIES
    cd = 4 * HEAD
    cc = ev_conv_w.shape[2]

    pack0 = jnp.zeros((8, d), F32).at[0].set(c[0]).at[1, :n_even * 3 * cc].set(ev_conv_w.reshape(-1))
    got0, _ = _all_gather(pack0, "gather_cond")
    got0 = got0.reshape(N_DEV, 8, d)
    c_all = got0[:, 0, :]
    conv_all = got0[:, 1, :n_even * 3 * cc].reshape(N_DEV, n_even, 3, cc).transpose(1, 2, 0, 3)
    conv_all = conv_all.reshape(n_even, 3, N_DEV * cc)
    cols = ada_w.shape[2]
    ada_b_cols = lax.dynamic_slice(ada_b, (0, me * cols), (depth, cols))
    mod_cols = _ada_fwd(c_all, ada_w, ada_b_cols)
    got1, cond_done = _all_gather(mod_cols.reshape(depth * N_DEV, cols), "gather_mod")
    got1 = got1.reshape(N_DEV, depth, N_DEV, cols)
    mod = lax.dynamic_index_in_dim(got1, me, axis=2, keepdims=False)
    mod = mod.transpose(1, 0, 2).reshape(depth, 6, d)
    modps = [jnp.concatenate([mod[l], norm_mix_g[l][None], norm_ffn_g[l][None]], axis=0) for l in range(depth)]

    in_flight = []
    started = cond_done
    for l in range(depth):
        j = l // 2
        plain = lambda w: _bf(w + started)
        tr = lambda w: plain(w).T
        blocks = [tr(ev_w_in[j]), plain(ev_w_out[j])] if l % 2 == 0 else [tr(od_w_qkv[j]), plain(od_w_out[j])]
        mixer, started = _exchange_start(blocks, [_landing(b, me, False) for b in blocks], False,
                                         f"gather_start_mixer_{l}")
        blocks = [jnp.concatenate([tr(ffn_w_gate[l]), tr(ffn_w_up[l]), plain(ffn_w_down[l])], axis=0)]
        ffn, started = _exchange_start(blocks, [_landing(b, me, False) for b in blocks], False,
                                       f"gather_start_ffn_{l}")
        in_flight.append((mixer, ffn))
        modps[0] = modps[0] + started
    n_ffn = ffn_w_down.shape[1]

    def mixer_weights(l, after):
        got = _exchange_wait(*in_flight[l][0], False, [after], f"gather_wait_mixer_{l}")
        return got[0].reshape(N_DEV, 1, -1, d), got[1].reshape(N_DEV, 1, -1, d)

    def ffn_weights(l, after):
        got = _exchange_wait(*in_flight[l][1], False, [after], f"gather_wait_ffn_{l}")
        return got[0].reshape(N_DEV, 3, n_ffn, d)

    cos, sin = _rope_tables(s)
    consts = _retention_consts(ts)
    cws = [_pad_rows(jnp.concatenate([conv_all[j], ev_ret_norm_g[j][None]], axis=0), 8) for j in range(n_even)]
    qkgs = [_pad_rows(jnp.stack([od_q_norm_g[j], od_k_norm_g[j]]), 8) for j in range(n_odd)]

    saved = []
    weights = []
    cur = xs
    for l in range(depth):
        j = l // 2
        w_in, w_out = mixer_weights(l, cur)
        if l % 2 == 0:
            x1, y, states, zhalo, proj = _even_fwd(cur, modps[l], w_in, w_out, 0, cws[j], cos, sin, consts, ts)
            mix = (states, zhalo, proj)
        else:
            qkv, pre = _odd_qkv_fwd(cur, modps[l], w_in, 0, qkgs[j], ts)
            o, tot = _sb_fwd(qkv, tq)
            x1, y = _odd_out_fwd(o, cur, modps[l], w_out, 0, ts)
            mix = (qkv, o, tot, pre)
        w_ffn = ffn_weights(l, x1)
        weights.append((w_in, w_out, w_ffn))
        x2, f, ab = _ffn_fwd(x1, modps[l], w_ffn, 0, ts)
        saved.append((cur, x1, y, (f, ab), mix))
        cur = x2

    dx, loss_part = _loss_grad(cur, tgt, ts)
    loss = lax.psum(loss_part[0, 0], ("x", "y", "c"))

    dmod = [None] * depth
    d_gmix = [None] * depth
    d_gffn = [None] * depth
    d_conv = [None] * n_even
    d_retg = [None] * n_even
    d_qg = [None] * n_odd
    d_kg = [None] * n_odd
    grads_in_flight = [None] * depth
    for l in reversed(range(depth)):
        j = l // 2
        x0, x1, y, (f, ab), mix = saved[l]
        w_in, w_out, w_ffn = weights[l]
        g_ffn = lax.empty(w_ffn.shape, BF16)
        g_in = lax.empty(w_in.shape, BF16)
        g_out = lax.empty(w_out.shape, BF16)
        dx1, dab, h2, sv, df, sg2 = _ffn_bwd(x1, f, ab, dx, modps[l], w_ffn, 0, ts)
        g_ffn = _tn_matmul(dab, 0, h2, g_ffn, 0, "tn_gate")
        g_ffn = _tn_matmul(dab, 1, h2, g_ffn, 1, "tn_up")
        g_ffn = _tn_matmul(sv, 0, df, g_ffn, 2, "tn_down")
        if l % 2 == 0:
            states, zhalo, proj = mix
            dx, dproj, hb, mb, dyb, sg1 = _even_bwd(x0, dx1, y, states, zhalo, proj, modps[l], w_in, w_out, 0,
                                                    cws[j], cos, sin, consts, ts)
            g_in = _tn_matmul(dproj, 0, hb, g_in, 0, "tn_ev_in")
            g_out = _tn_matmul(mb, 0, dyb, g_out, 0, "tn_ev_out")
            d_conv[j] = sg1[4:7, :cd]
            d_retg[j] = sg1[7, :cd]
        else:
            qkv, o, tot, pre = mix
            do, dyb, sg0 = _odd_out_bwd(dx1, y, modps[l], w_out, 0, ts)
            dq, dk, dv = _sb_bwd(qkv, do, tot, tq)
            dx, dqkv, hb, sg1 = _odd_qkv_bwd(x0, dx1, dq, dk, dv, pre, sg0, modps[l], w_in, 0, qkgs[j], ts)
            g_in = _tn_matmul(dqkv, 0, hb, g_in, 0, "tn_od_qkv")
            g_out = _tn_matmul(o, 0, dyb, g_out, 0, "tn_od_out")
            d_qg[j] = sg1[4, :HEAD]
            d_kg[j] = sg1[5, :HEAD]
        pieces = [g.reshape(N_DEV, -1, d) for g in (g_ffn, g_in, g_out)]
        if l > 0:
            grads_in_flight[l], started = _exchange_start(pieces, [_landing(p, me, True) for p in pieces], True,
                                                          f"grads_start_{l}")
            modps[l - 1] = modps[l - 1] + started
        dmod[l] = jnp.concatenate([sg1[0:3], sg2[0:3]], axis=0).reshape(-1)
        d_gmix[l] = sg1[3]
        d_gffn[l] = sg2[3]

    small = jnp.concatenate(
        [jnp.stack(dmod).reshape(-1), jnp.stack(d_gmix).reshape(-1), jnp.stack(d_gffn).reshape(-1),
         jnp.stack(d_retg).reshape(-1), jnp.stack(d_qg).reshape(-1), jnp.stack(d_kg).reshape(-1),
         jnp.stack(d_conv).reshape(-1)])
    n_small = small.shape[0]
    rows_small = -(-n_small // (8 * 128)) * 8
    small = jnp.concatenate([small, jnp.zeros((rows_small * 128 - n_small,), F32)]).reshape(rows_small, 128)
    got2, small_done = _all_gather(small, "gather_small")
    got2 = got2.reshape(N_DEV, rows_small, 128)
    grads_in_flight[0], started = _exchange_start(pieces, [_landing(p, me, True, small_done) for p in pieces],
                                                  True, "grads_start_0")
    tot_small = _sum_small(got2).reshape(-1)
    n_mod = depth * 6 * d
    dmod_all = got2.reshape(N_DEV, -1)[:, :n_mod].reshape(N_DEV, depth, 6 * d)
    dmod_cols = lax.dynamic_slice(dmod_all, (0, 0, me * cols), (N_DEV, depth, cols)).transpose(1, 0, 2)
    g_ada_w = _ada_bwd(c_all.T, dmod_cols + started)

    off = [0]

    def take(shape):
        n = int(np.prod(shape))
        out = tot_small[off[0]:off[0] + n].reshape(shape)
        off[0] += n
        return out

    g_ada_b = take((depth, 6 * d))
    g_norm_mix = take((depth, d))
    g_norm_ffn = take((depth, d))
    g_ret_norm = take((n_even, cd))
    g_q_norm = take((n_odd, HEAD))
    g_k_norm = take((n_odd, HEAD))
    g_conv_full = take((n_even, 3, cd))
    g_conv = lax.dynamic_slice(g_conv_full, (0, 0, me * cc), (n_even, 3, cc))

    res = {"ada_w": (g_ada_w,) + _adamw_nd(ada_w, g_ada_w, m_ada_w, v_ada_w, "adamw_ada_w")}
    big = {"ev_w_in": (ev_w_in, m_ev_w_in, v_ev_w_in), "ev_w_out": (ev_w_out, m_ev_w_out, v_ev_w_out),
           "od_w_qkv": (od_w_qkv, m_od_w_qkv, v_od_w_qkv), "od_w_out": (od_w_out, m_od_w_out, v_od_w_out),
           "ffn_w_gate": (ffn_w_gate, m_ffn_w_gate, v_ffn_w_gate), "ffn_w_up": (ffn_w_up, m_ffn_w_up, v_ffn_w_up),
           "ffn_w_down": (ffn_w_down, m_ffn_w_down, v_ffn_w_down)}
    flipped = ("ev_w_in", "ffn_w_gate", "ffn_w_up")
    flip = lambda a: a.transpose(0, 2, 1)
    for name in flipped:
        big[name] = tuple(flip(a) for a in big[name])
    for name, (w, _, _) in big.items():
        res[name] = tuple(lax.empty(w.shape, F32) for _ in range(4))

    def update(name, idx, g_layer):
        w, m, v = big[name]
        res[name] = _adamw_layer(w, g_layer, m, v, res[name], idx, "adamw_" + name)

    after = [res["ada_w"][1]]
    for l in reversed(range(depth)):
        recv = _exchange_wait(*grads_in_flight[l], True, after, f"grads_wait_{l}")
        s_ffn, s_in, s_out = [_sum_slots(r, f"sum_{i}") for i, r in enumerate(recv)]
        j = l // 2
        update("ffn_w_gate", l, s_ffn[0:n_ffn])
        update("ffn_w_up", l, s_ffn[n_ffn:2 * n_ffn])
        update("ffn_w_down", l, s_ffn[2 * n_ffn:3 * n_ffn])
        if l % 2 == 0:
            update("ev_w_in", j, s_in)
        else:
            update("od_w_qkv", j, s_in.T)
        update("ev_w_out" if l % 2 == 0 else "od_w_out", j, s_out)
        after = [res[name][1] for name in big]
    for name in flipped:
        res[name] = tuple(flip(a) for a in res[name])

    smalls = [("ada_b", ada_b, g_ada_b, m_ada_b, v_ada_b), ("norm_mix_g", norm_mix_g, g_norm_mix, m_norm_mix_g, v_norm_mix_g),
              ("norm_ffn_g", norm_ffn_g, g_norm_ffn, m_norm_ffn_g, v_norm_ffn_g),
              ("ev_conv_w", ev_conv_w, g_conv, m_ev_conv_w, v_ev_conv_w),
              ("ev_ret_norm_g", ev_ret_norm_g, g_ret_norm, m_ev_ret_norm_g, v_ev_ret_norm_g),
              ("od_q_norm_g", od_q_norm_g, g_q_norm, m_od_q_norm_g, v_od_q_norm_g),
              ("od_k_norm_g", od_k_norm_g, g_k_norm, m_od_k_norm_g, v_od_k_norm_g)]

    def pack(arrs):
        flat = jnp.concatenate([a.reshape(-1) for a in arrs])
        rows = -(-flat.shape[0] // (8 * 128)) * 8
        return jnp.concatenate([flat, jnp.zeros((rows * 128 - flat.shape[0],), F32)]).reshape(rows, 128)

    sd, sm, sv_ = _adamw(pack([t[1] for t in smalls]), pack([t[2] for t in smalls]),
                         pack([t[3] for t in smalls]), pack([t[4] for t in smalls]), "adamw_small")
    sd, sm, sv_ = sd.reshape(-1), sm.reshape(-1), sv_.reshape(-1)
    pos = 0
    for name, w, g, m, v in smalls:
        n = int(np.prod(w.shape))
        res[name] = (g, sd[pos:pos + n].reshape(w.shape), sm[pos:pos + n].reshape(w.shape),
                     sv_[pos:pos + n].reshape(w.shape))
        pos += n

    order = ["ada_w", "ada_b", "norm_mix_g", "norm_ffn_g", "ev_w_in", "ev_conv_w", "ev_ret_norm_g", "ev_w_out",
             "od_w_qkv", "od_q_norm_g", "od_k_norm_g", "od_w_out", "ffn_w_gate", "ffn_w_up", "ffn_w_down"]
    outs = [loss, dx[None]]
    for k in range(4):
        outs += [res[name][k] for name in order]
    return tuple(outs)
```

```python
import functools
import math

import numpy as np
import jax
import jax.numpy as jnp
from jax import lax
from jax.experimental import pallas as pl
from jax.experimental.pallas import tpu as pltpu

F32 = jnp.float32
BF16 = jnp.bfloat16
MESH = pl.DeviceIdType.MESH

N_DEV = 8
EPS = 1e-6
CHUNK = 64
HEAD = 128
RET_HEADS = 4
SB_HEADS = 8
ROPE_THETA = 10000.0
KEY_BLOCK = 128
ADAM_LR, ADAM_B1, ADAM_B2, ADAM_EPS, ADAM_WD, ADAM_STEP = 0.001, 0.9, 0.999, 1e-08, 0.01, 10
VMEM_LIMIT = 56 * 1024 * 1024


def _pcall(body, **kw):
    return pl.pallas_call(body, **kw)


def _params(n_grid=1, vmem=VMEM_LIMIT):
    return pltpu.CompilerParams(dimension_semantics=("arbitrary",) * n_grid, vmem_limit_bytes=vmem)


def _mm(a, b):
    return jnp.dot(a, b, preferred_element_type=F32)


def _mm_nt(a, b):
    return lax.dot_general(a, b, (((1,), (1,)), ((), ())), preferred_element_type=F32)


def _mm_tn(a, b):
    return lax.dot_general(a, b, (((0,), (0,)), ((), ())), preferred_element_type=F32)


def _bf(a):
    return a.astype(BF16)


def _sigmoid(a):
    return 1.0 / (1.0 + jnp.exp(-a))


def _sum0(a):
    return jnp.sum(a, axis=0, keepdims=True)


def _full(shape):
    nd = len(shape)
    return pl.BlockSpec(shape, lambda *_: (0,) * nd)


def _normmod_fwd(x, g, sc, sh):
    rstd = lax.rsqrt(jnp.mean(x * x, axis=-1, keepdims=True) + EPS)
    n = x * rstd
    return n, rstd, (n * g) * (1.0 + sc) + sh


def _normmod_bwd(dh, n, rstd, g, sc):
    dsh = _sum0(dh)
    dsc = _sum0(dh * (n * g))
    dg = _sum0(dh * n * (1.0 + sc))
    dn = dh * (g * (1.0 + sc))
    dx = rstd * (dn - n * jnp.mean(dn * n, axis=-1, keepdims=True))
    return dx, dsh, dsc, dg


def _rms_fwd(o):
    rstd = lax.rsqrt(jnp.mean(o * o, axis=-1, keepdims=True) + EPS)
    return o * rstd, rstd


def _rms_bwd(dn, n, rstd):
    return rstd * (dn - n * jnp.mean(dn * n, axis=-1, keepdims=True))


def _all_gather(x2d, name):
    m_per, n = x2d.shape
    space = pltpu.VMEM

    def body(x_ref, out_ref, done_ref, send_sems, recv_sems, local_sem):
        x, y, c = lax.axis_index("x"), lax.axis_index("y"), lax.axis_index("c")
        me, sibling = (x, y, c), (x, y, 1 - c)
        chips = [(1 - x, y), (x, 1 - y), (1 - x, 1 - y)]

        def rows(px, py, pc):
            return out_ref.at[pl.ds((4 * px + 2 * py + pc) * m_per, m_per), :]

        def copy(k, block, to, src=None):
            return pltpu.make_async_remote_copy(
                src_ref=rows(*block) if src is None else src, dst_ref=rows(*block),
                send_sem=send_sems.at[k], recv_sem=recv_sems.at[k],
                device_id=to, device_id_type=MESH)

        mine = pltpu.make_async_copy(x_ref, rows(*me), local_sem)
        mine.start()
        first = [copy(1 + j, me, (*chip, c), src=x_ref) for j, chip in enumerate(chips)]
        first += [copy(0, me, sibling, src=x_ref)]
        for cp in first:
            cp.start()
        passed = [copy(4 + j, (*chip, c), sibling) for j, chip in enumerate(chips)]
        for j, chip in enumerate(chips):
            copy(1 + j, (*chip, c), me).wait_recv()
            passed[j].start()
        copy(0, sibling, me).wait_recv()
        for j, chip in enumerate(chips):
            copy(4 + j, (*chip, 1 - c), me).wait_recv()
        for cp in first + passed:
            cp.wait_send()
        mine.wait()
        done_ref[...] = jnp.zeros_like(done_ref)

    out, done = _pcall(
        body, name=name,
        out_shape=(jax.ShapeDtypeStruct((N_DEV * m_per, n), x2d.dtype), jax.ShapeDtypeStruct((8, 128), F32)),
        in_specs=[pl.BlockSpec(memory_space=space)],
        out_specs=(pl.BlockSpec(memory_space=space), pl.BlockSpec(memory_space=pltpu.VMEM)),
        scratch_shapes=[pltpu.SemaphoreType.DMA((7,)), pltpu.SemaphoreType.DMA((7,)),
                        pltpu.SemaphoreType.DMA],
    )(x2d)
    return out, done[0, 0]


_HBM = pl.BlockSpec(memory_space=pltpu.HBM)
_SEM = pl.BlockSpec(memory_space=pltpu.SEMAPHORE)
_EFFECT = pltpu.SideEffectType.DATAFLOW_SIDE_EFFECTING


def _exchange_copies(src_refs, land_refs, send_sems, recv_sems, scatter):
    x, y, c = lax.axis_index("x"), lax.axis_index("y"), lax.axis_index("c")
    me = 4 * x + 2 * y + c
    out = []
    for i, (s_ref, l_ref) in enumerate(zip(src_refs, land_refs)):
        for k in (2, 4, 6, 3, 5, 7, 1):
            px = (1 - x) if (k >> 2) & 1 else x
            py = (1 - y) if (k >> 1) & 1 else y
            pc = (1 - c) if k & 1 else c
            out.append(pltpu.make_async_remote_copy(
                src_ref=s_ref.at[4 * px + 2 * py + pc] if scatter else s_ref, dst_ref=l_ref.at[me],
                send_sem=send_sems.at[7 * i + k - 1], recv_sem=recv_sems.at[7 * i + k - 1],
                device_id=(px, py, pc), device_id_type=MESH))
    return out


def _exchange_start(srcs, lands, scatter, name):
    n = len(srcs)

    def body(*refs):
        for cp in _exchange_copies(refs[:n], refs[n:2 * n], refs[2 * n], refs[2 * n + 1], scatter):
            cp.start()
        refs[-1][...] = jnp.zeros_like(refs[-1])

    arrays = list(srcs) + list(lands)
    outs = _pcall(
        body, name=name,
        out_shape=(pltpu.SemaphoreType.DMA((7 * n,)), pltpu.SemaphoreType.DMA((7 * n,)),
                   *[pltpu.HBM(a.shape, a.dtype) for a in arrays], jax.ShapeDtypeStruct((8, 128), F32)),
        in_specs=[_HBM] * (2 * n),
        out_specs=(_SEM, _SEM, *[_HBM] * (2 * n), pl.BlockSpec(memory_space=pltpu.VMEM)),
        input_output_aliases={i: 2 + i for i in range(2 * n)},
        compiler_params=pltpu.CompilerParams(has_side_effects=_EFFECT),
    )(*[pltpu.with_memory_space_constraint(a, pltpu.HBM) for a in arrays])
    return (outs[0], outs[1], list(outs[2:2 + n]), list(outs[2 + n:2 + 2 * n])), outs[-1][0, 0]


def _exchange_wait(send_sems, recv_sems, srcs, lands, scatter, after, name):
    n = len(srcs)
    after = list(after)

    def body(*refs):
        for cp in _exchange_copies(refs[:n], refs[n:2 * n], refs[2 * n], refs[2 * n + 1], scatter):
            cp.wait_send()
            cp.wait_recv()

    arrays = list(srcs) + list(lands)
    outs = _pcall(
        body, name=name,
        out_shape=tuple(pltpu.HBM(a.shape, a.dtype) for a in arrays),
        in_specs=[_HBM] * (2 * n) + [_SEM, _SEM] + [pl.BlockSpec(memory_space=pl.ANY)] * len(after),
        out_specs=tuple([_HBM] * (2 * n)),
        input_output_aliases={i: i for i in range(2 * n)},
        compiler_params=pltpu.CompilerParams(has_side_effects=_EFFECT),
    )(*arrays, send_sems, recv_sems, *after)
    return list(outs[n:])


def _landing(src, me, scatter, after=None):
    own = lax.dynamic_index_in_dim(src, me, 0, keepdims=True) if scatter else src[None]
    if after is not None:
        own = own + after.astype(own.dtype)
    shape = src.shape if scatter else (N_DEV,) + src.shape
    return lax.dynamic_update_slice(lax.empty(shape, src.dtype), own, (me, 0, 0))


def _sum_slots(recv, name):
    _, r, n = recv.shape
    tr = r
    for cand in (512, 448, 384, 352, 256, 128, 64, 32, 16, 8):
        if r % cand == 0:
            tr = cand
            break

    def body(r_ref, o_ref):
        acc = r_ref[0].astype(F32)
        for p in range(1, N_DEV):
            acc = acc + r_ref[p].astype(F32)
        o_ref[...] = acc

    return _pcall(
        body, name=name, grid=(r // tr,),
        out_shape=jax.ShapeDtypeStruct((r, n), F32),
        in_specs=[pl.BlockSpec((N_DEV, tr, n), lambda i: (0, i, 0))],
        out_specs=pl.BlockSpec((tr, n), lambda i: (i, 0)),
        compiler_params=_params(1),
    )(recv)


def _row_tile(rows, limit=512):
    for cand in range(min(limit, rows) // 8 * 8, 7, -8):
        if rows % cand == 0:
            return cand
    return rows


def _adamw(w, g, m, v, name):
    r, n = w.shape
    tr = _row_tile(r)

    def body(w_ref, g_ref, m_ref, v_ref, d_ref, nm_ref, nv_ref):
        d_ref[...], nm_ref[...], nv_ref[...] = _adam_update(w_ref[...], g_ref[...], m_ref[...], v_ref[...])

    spec = pl.BlockSpec((tr, n), lambda i: (i, 0))
    shp = jax.ShapeDtypeStruct((r, n), F32)
    return _pcall(
        body, name=name, grid=(r // tr,), out_shape=(shp, shp, shp),
        in_specs=[spec] * 4, out_specs=(spec, spec, spec), compiler_params=_params(1),
    )(w, g, m, v)


def _adam_update(wv, gv, mv, vv):
    bc1 = 1.0 / (1.0 - ADAM_B1 ** ADAM_STEP)
    bc2 = 1.0 / (1.0 - ADAM_B2 ** ADAM_STEP)
    nm = ADAM_B1 * mv + (1.0 - ADAM_B1) * gv
    nv = ADAM_B2 * vv + (1.0 - ADAM_B2) * (gv * gv)
    return -ADAM_LR * ((nm * bc1) / (jnp.sqrt(nv * bc2) + ADAM_EPS) + ADAM_WD * wv), nm, nv


def _adamw_layer(w, g_layer, m, v, outs, idx, name):
    _, a, b = w.shape
    tr = _row_tile(a)

    def body(w_ref, g_ref, m_ref, v_ref, o0, o1, o2, o3, go_ref, d_ref, nm_ref, nv_ref):
        gv = g_ref[...]
        go_ref[...] = gv
        d_ref[...], nm_ref[...], nv_ref[...] = _adam_update(w_ref[...], gv, m_ref[...], v_ref[...])

    layer = pl.BlockSpec((None, tr, b), lambda i: (idx, i, 0))
    anyw = pl.BlockSpec(memory_space=pl.ANY)
    shp = jax.ShapeDtypeStruct(w.shape, F32)
    return tuple(_pcall(
        body, name=name, grid=(a // tr,), out_shape=(shp,) * 4,
        in_specs=[layer, pl.BlockSpec((tr, b), lambda i: (i, 0)), layer, layer, anyw, anyw, anyw, anyw],
        out_specs=(layer,) * 4, input_output_aliases={4: 0, 5: 1, 6: 2, 7: 3},
        compiler_params=_params(1),
    )(w, g_layer, m, v, *outs))


def _adamw_nd(w, g, m, v, name):
    shp = w.shape
    f = lambda a: a.reshape(-1, shp[-1])
    d, nm, nv = _adamw(f(w), f(g), f(m), f(v), name)
    return d.reshape(shp), nm.reshape(shp), nv.reshape(shp)


def _ada_fwd(c_all, ada_w, ada_b_cols):
    n_l, d, cols = ada_w.shape

    def body(c_ref, w_ref, b_ref, o_ref):
        cv = c_ref[...]
        ca = cv * _sigmoid(cv)
        o_ref[...] = _mm(_bf(ca), _bf(w_ref[...])) + b_ref[...]

    return _pcall(
        body, name="ada_fwd", grid=(n_l,),
        out_shape=jax.ShapeDtypeStruct((n_l, N_DEV, cols), F32),
        in_specs=[_full((N_DEV, d)), pl.BlockSpec((None, d, cols), lambda l: (l, 0, 0)),
                  pl.BlockSpec((None, 1, cols), lambda l: (l, 0, 0))],
        out_specs=pl.BlockSpec((None, N_DEV, cols), lambda l: (l, 0, 0)),
        compiler_params=_params(1),
    )(c_all, ada_w, ada_b_cols.reshape(n_l, 1, cols))


def _ada_bwd(c_all_t, dmod_cols):
    d = c_all_t.shape[0]
    n_l, _, cols = dmod_cols.shape

    def body(ct_ref, dm_ref, o_ref):
        cv = ct_ref[...]
        ca = cv * _sigmoid(cv)
        dm = dm_ref[...]
        acc = ca[:, 0:1] * dm[0:1, :]
        for b in range(1, N_DEV):
            acc = acc + ca[:, b:b + 1] * dm[b:b + 1, :]
        o_ref[...] = acc

    return _pcall(
        body, name="ada_bwd", grid=(n_l,),
        out_shape=jax.ShapeDtypeStruct((n_l, d, cols), F32),
        in_specs=[_full((d, N_DEV)), pl.BlockSpec((None, N_DEV, cols), lambda l: (l, 0, 0))],
        out_specs=pl.BlockSpec((None, d, cols), lambda l: (l, 0, 0)),
        compiler_params=_params(1),
    )(c_all_t, dmod_cols)


def _sum_small(gathered):
    _, r, n = gathered.shape

    def body(g_ref, o_ref):
        acc = g_ref[0]
        for p in range(1, N_DEV):
            acc = acc + g_ref[p]
        o_ref[...] = acc

    return _pcall(
        body, name="sum_small", out_shape=jax.ShapeDtypeStruct((r, n), F32),
        in_specs=[_full((N_DEV, r, n))], out_specs=_full((r, n)),
    )(gathered)


def _loss_grad(xf, tgt, ts):
    s, d = xf.shape

    def body(x_ref, t_ref, dx_ref, l_ref):
        @pl.when(pl.program_id(0) == 0)
        def _():
            l_ref[...] = jnp.zeros_like(l_ref)
        e = x_ref[...] - t_ref[...]
        dx_ref[...] = e * (1.0 / d)
        l_ref[...] += (0.5 / d) * jnp.sum(jnp.sum(e * e, axis=1, keepdims=True), axis=0, keepdims=True)

    spec = pl.BlockSpec((ts, d), lambda i: (i, 0))
    return _pcall(
        body, name="loss_grad", grid=(s // ts,),
        out_shape=(jax.ShapeDtypeStruct((s, d), F32), jax.ShapeDtypeStruct((1, 1), F32)),
        in_specs=[spec, spec], out_specs=(spec, _full((1, 1))), compiler_params=_params(1),
    )(xf, tgt)


def _tn_matmul(a, col_block, b, buf, slot, name):
    s = a.shape[0]
    k = b.shape[1]
    n_p = buf.shape[2]
    mcols = N_DEV * n_p
    ts = _row_tile(s, 1024)
    nt = s // ts

    def body(a_ref, b_ref, buf_ref, o_ref, acc):
        i = pl.program_id(0)

        @pl.when(i == 0)
        def _():
            acc[...] = jnp.zeros_like(acc)
        acc[...] += _mm_tn(a_ref[...], b_ref[...])

        @pl.when(i == nt - 1)
        def _():
            o_ref[...] = acc[...].reshape(N_DEV, n_p, k).astype(BF16)

    return _pcall(
        body, name=name, grid=(nt,),
        out_shape=jax.ShapeDtypeStruct(buf.shape, BF16),
        in_specs=[pl.BlockSpec((ts, mcols), lambda i: (i, col_block)),
                  pl.BlockSpec((ts, k), lambda i: (i, 0)),
                  pl.BlockSpec(memory_space=pl.ANY)],
        out_specs=pl.BlockSpec((N_DEV, None, n_p, k), lambda i: (0, slot, 0, 0)),
        scratch_shapes=[pltpu.VMEM((mcols, k), F32)],
        input_output_aliases={2: 0},
        compiler_params=_params(1),
    )(a, b, buf)


def _wspec4(w, slot):
    _, _, n_p, k = w.shape
    return pl.BlockSpec((N_DEV, None, n_p, k), lambda i: (0, slot, 0, 0), pipeline_mode=pl.Buffered(1))


def _ffn_fwd(x1, modp, w352, l, ts):
    s, d = x1.shape
    n_l = w352.shape[1] // 3
    f_dim = N_DEV * w352.shape[2]

    def body(x_ref, mp_ref, wg_ref, wu_ref, wd_ref, x2_ref, f_ref, ab_ref):
        x = x_ref[...]
        _, _, h2 = _normmod_fwd(x, mp_ref[7:8, :], mp_ref[4:5, :], mp_ref[3:4, :])
        hb = _bf(h2)
        f = jnp.zeros((ts, d), F32)
        half_dev, fc = N_DEV // 2, f_dim // 2
        for part in range(2):
            dev0, c0 = part * half_dev, part * fc
            a = _mm_nt(hb, wg_ref[dev0:dev0 + half_dev].reshape(fc, d))
            b = _mm_nt(hb, wu_ref[dev0:dev0 + half_dev].reshape(fc, d))
            ab_ref[:, c0:c0 + fc] = a
            ab_ref[:, f_dim + c0:f_dim + c0 + fc] = b
            sv = (a * _sigmoid(a)) * b
            f = f + _mm(_bf(sv), wd_ref[dev0:dev0 + half_dev].reshape(fc, d))
        f_ref[...] = f
        x2_ref[...] = x + mp_ref[5:6, :] * f

    tile = pl.BlockSpec((ts, d), lambda i: (i, 0))
    shp = jax.ShapeDtypeStruct((s, d), F32)
    return _pcall(
        body, name="ffn_fwd", grid=(s // ts,),
        out_shape=(shp, shp, jax.ShapeDtypeStruct((s, 2 * f_dim), F32)),
        in_specs=[tile, _full(modp.shape), _wspec4(w352, l), _wspec4(w352, n_l + l),
                  _wspec4(w352, 2 * n_l + l)],
        out_specs=(tile, tile, pl.BlockSpec((ts, 2 * f_dim), lambda i: (i, 0))), compiler_params=_params(1),
    )(x1, modp, w352, w352, w352)


def _ffn_bwd(x1, f, ab, dx2, modp, w352, l, ts):
    s, d = x1.shape
    n_l = w352.shape[1] // 3
    f_dim = N_DEV * w352.shape[2]

    def body(x_ref, f_ref, ab_ref, dx2_ref, mp_ref, wg_ref, wu_ref, wd_ref,
             dx1_ref, dab_ref, h2_ref, s_ref, df_ref, sg_ref):
        @pl.when(pl.program_id(0) == 0)
        def _():
            sg_ref[...] = jnp.zeros_like(sg_ref)
        x = x_ref[...]
        gffn, sc2, g2 = mp_ref[7:8, :], mp_ref[4:5, :], mp_ref[5:6, :]
        n, rstd, h2 = _normmod_fwd(x, gffn, sc2, mp_ref[3:4, :])
        hb = _bf(h2)
        dx2 = dx2_ref[...]
        dfb = _bf(g2 * dx2)
        dh2 = jnp.zeros((ts, d), F32)
        half_dev, fc = N_DEV // 2, f_dim // 2
        for part in range(2):
            dev0, c0 = part * half_dev, part * fc
            wg = wg_ref[dev0:dev0 + half_dev].reshape(fc, d)
            wu = wu_ref[dev0:dev0 + half_dev].reshape(fc, d)
            a = ab_ref[:, c0:c0 + fc]
            b = ab_ref[:, f_dim + c0:f_dim + c0 + fc]
            sig = _sigmoid(a)
            sa = a * sig
            s_ref[:, c0:c0 + fc] = _bf(sa * b)
            ds = _mm_nt(dfb, wd_ref[dev0:dev0 + half_dev].reshape(fc, d))
            dab = _bf(ds * b * (sig * (1.0 + a * (1.0 - sig))))
            dbb = _bf(ds * sa)
            dab_ref[:, c0:c0 + fc] = dab
            dab_ref[:, f_dim + c0:f_dim + c0 + fc] = dbb
            dh2 = dh2 + _mm(dab, wg) + _mm(dbb, wu)
        dxn, dsh, dsc, dg = _normmod_bwd(dh2, n, rstd, gffn, sc2)
        dx1_ref[...] = dx2 + dxn
        h2_ref[...] = hb
        df_ref[...] = dfb
        sg_ref[0:1, :] += dsh
        sg_ref[1:2, :] += dsc
        sg_ref[2:3, :] += _sum0(dx2 * f_ref[...])
        sg_ref[3:4, :] += dg

    tile = pl.BlockSpec((ts, d), lambda i: (i, 0))
    f32t = jax.ShapeDtypeStruct((s, d), F32)
    bft = jax.ShapeDtypeStruct((s, d), BF16)
    return _pcall(
        body, name="ffn_bwd", grid=(s // ts,),
        out_shape=(f32t, jax.ShapeDtypeStruct((s, 2 * f_dim), BF16), bft,
                   jax.ShapeDtypeStruct((s, f_dim), BF16), bft, jax.ShapeDtypeStruct((8, d), F32)),
        in_specs=[tile, tile, pl.BlockSpec((ts, 2 * f_dim), lambda i: (i, 0)), tile, _full(modp.shape),
                  _wspec4(w352, l), _wspec4(w352, n_l + l), _wspec4(w352, 2 * n_l + l)],
        out_specs=(tile, pl.BlockSpec((ts, 2 * f_dim), lambda i: (i, 0)), tile,
                   pl.BlockSpec((ts, f_dim), lambda i: (i, 0)), tile, _full((8, d))),
        compiler_params=_params(1),
    )(x1, f, ab, dx2, modp, w352, w352, w352)


def _retention_consts(ts):
    h = np.arange(RET_HEADS, dtype=np.float64)
    log_g = np.log1p(-np.exp2(-5.0 - h))
    t = np.arange(ts)
    diff = t[:, None] - t[None, :]
    same = (t[:, None] // CHUNK) == (t[None, :] // CHUNK)
    later = (t[:, None] // CHUNK) > (t[None, :] // CHUNK)
    dm = np.where(same, np.abs(diff), np.where(later, diff, 0))[None] * log_g[:, None, None]
    dm = np.where((same | later)[None], np.exp(dm), 0.0)
    qd = np.exp((t[:, None] + 1.0) * log_g[None, :])
    kd = np.exp((ts - 1.0 - t[:, None]) * log_g[None, :])
    qd = np.repeat(qd, HEAD, axis=1)
    kd = np.repeat(kd, HEAD, axis=1)
    tdec = [float(np.exp(ts * lg)) for lg in log_g]
    return (jnp.asarray(dm, F32), jnp.asarray(qd, F32), jnp.asarray(kd, F32), tdec)


def _rope_tables(s):
    inv_freq = 1.0 / (ROPE_THETA ** (jnp.arange(0, HEAD, 2, dtype=F32) / HEAD))
    ang = jnp.arange(s, dtype=F32)[:, None] * inv_freq[None, :]
    cos, sin = jnp.cos(ang), jnp.sin(ang)
    return jnp.concatenate([cos, cos], axis=1), jnp.concatenate([-sin, sin], axis=1)


def _rope(v, cos, sin):
    return v * cos + pltpu.roll(v, HEAD // 2, 1) * sin


def _rope_t(dv, cos, sin):
    return dv * cos + pltpu.roll(dv * sin, HEAD // 2, 1)


def _shift_down(z, k, halo_ref):
    r = pltpu.roll(z, k, 0)
    rows = lax.broadcasted_iota(jnp.int32, z.shape, 0)
    for j in range(k):
        r = jnp.where(rows == j, halo_ref[8 - k + j:8 - k + j + 1, :], r)
    return r


def _shift_up(z, k, halo_ref):
    n = z.shape[0]
    r = pltpu.roll(z, n - k, 0)
    rows = lax.broadcasted_iota(jnp.int32, z.shape, 0)
    for j in range(k):
        r = jnp.where(rows == n - k + j, halo_ref[j:j + 1, :], r)
    return r


def _even_recompute(x, mp_ref, win, cw_ref, cos, sin, dm_ref, qd_ref, kd_ref, halo_ref, state_of, proj=None):
    cd = 4 * HEAD
    n, rstd, h = _normmod_fwd(x, mp_ref[6:7, :], mp_ref[1:2, :], mp_ref[0:1, :])
    if proj is None:
        proj = _mm_nt(_bf(h), win)
    bg, cg, u = proj[:, 0:cd], proj[:, cd:2 * cd], proj[:, 2 * cd:3 * cd]
    z = cg * u
    z1 = _shift_down(z, 1, halo_ref)
    z2 = _shift_down(z, 2, halo_ref)
    conv = cw_ref[0:1, :] * z2 + cw_ref[1:2, :] * z1 + cw_ref[2:3, :] * z
    heads = []
    scale = HEAD ** -0.5
    for hh in range(RET_HEADS):
        lo = hh * HEAD
        q = proj[:, 3 * cd + lo:3 * cd + lo + HEAD]
        k = proj[:, 4 * cd + lo:4 * cd + lo + HEAD]
        v = proj[:, 5 * cd + lo:5 * cd + lo + HEAD]
        gate = proj[:, 6 * cd + lo:6 * cd + lo + HEAD]
        qr = _rope(q, cos, sin)
        kr = _rope(k, cos, sin) * scale
        sc = _mm_nt(_bf(qr), _bf(kr)) * dm_ref[hh]
        qs = qr * qd_ref[:, lo:lo + HEAD]
        ks = kr * kd_ref[:, lo:lo + HEAD]
        o = _mm(_bf(sc), _bf(v)) + _mm(_bf(qs), _bf(state_of(hh)))
        on, orstd = _rms_fwd(o)
        sig = _sigmoid(gate)
        heads.append(dict(qr=qr, kr=kr, v=v, gate=gate, sc=sc, qs=qs, ks=ks, on=on, orstd=orstd, sig=sig))
    return dict(n=n, rstd=rstd, h=h, proj=proj, bg=bg, cg=cg, u=u, z=z, z1=z1, z2=z2, conv=conv, heads=heads)


def _even_fwd(x, modp, w448, w128, l, cw, cos, sin, consts, ts):
    s, d = x.shape
    nt = s // ts
    dm, qd, kd, tdec = consts
    cd = 4 * HEAD
    e_in = N_DEV * w448.shape[2]

    def body(x_ref, mp_ref, win_ref, cw_ref, cos_ref, sin_ref, dm_ref, qd_ref, kd_ref, wout_ref,
             x1_ref, y_ref, st_ref, zh_ref, proj_ref, state, halo):
        @pl.when(pl.program_id(0) == 0)
        def _():
            state[...] = jnp.zeros_like(state)
            halo[...] = jnp.zeros_like(halo)
        xv = x_ref[...]
        st_ref[...] = state[...]
        zh_ref[...] = halo[...]
        r = _even_recompute(xv, mp_ref, win_ref[...].reshape(e_in, d), cw_ref, cos_ref[...], sin_ref[...],
                            dm_ref, qd_ref, kd_ref, halo, lambda hh: state[hh])
        proj_ref[...] = r["proj"]
        halo[...] = r["z"][ts - 8:ts, :]
        parts = [r["bg"] * r["conv"]]
        for hh, hd in enumerate(r["heads"]):
            state[hh] = state[hh] * tdec[hh] + _mm_tn(_bf(hd["ks"]), _bf(hd["v"]))
            rg = cw_ref[3:4, hh * HEAD:(hh + 1) * HEAD]
            parts.append((hd["gate"] * hd["sig"]) * (hd["on"] * rg))
        mcat = jnp.concatenate(parts, axis=1)
        y = _mm(_bf(mcat), wout_ref[...].reshape(d, d))
        y_ref[...] = y
        x1_ref[...] = xv + mp_ref[2:3, :] * y

    tile = pl.BlockSpec((ts, d), lambda i: (i, 0))
    rt = pl.BlockSpec((ts, HEAD), lambda i: (i, 0))
    shp = jax.ShapeDtypeStruct((s, d), F32)
    return _pcall(
        body, name="even_fwd", grid=(nt,),
        out_shape=(shp, shp, jax.ShapeDtypeStruct((nt, RET_HEADS, HEAD, HEAD), F32),
                   jax.ShapeDtypeStruct((nt, 8, cd), F32), jax.ShapeDtypeStruct((s, e_in), F32)),
        in_specs=[tile, _full(modp.shape), _wspec4(w448, l), _full(cw.shape), rt, rt,
                  _full(dm.shape), _full(qd.shape), _full(kd.shape), _wspec4(w128, l)],
        out_specs=(tile, tile, pl.BlockSpec((None, RET_HEADS, HEAD, HEAD), lambda i: (i, 0, 0, 0)),
                   pl.BlockSpec((None, 8, cd), lambda i: (i, 0, 0)), pl.BlockSpec((ts, e_in), lambda i: (i, 0))),
        scratch_shapes=[pltpu.VMEM((RET_HEADS, HEAD, HEAD), F32), pltpu.VMEM((8, cd), F32)],
        compiler_params=_params(1),
    )(x, modp, w448, cw, cos, sin, dm, qd, kd, w128)


def _even_bwd(x, dx1, y, states, zhalo, proj, modp, w448, w128, l, cw, cos, sin, consts, ts):
    s, d = x.shape
    nt = s // ts
    dm, qd, kd, tdec = consts
    cd = 4 * HEAD
    e_in = N_DEV * w448.shape[2]
    scale = HEAD ** -0.5

    def body(x_ref, dx1_ref, y_ref, st_ref, zh_ref, proj_ref, mp_ref, win_ref, cw_ref, cos_ref, sin_ref,
             dm_ref, qd_ref, kd_ref, wout_ref,
             dx_ref, dproj_ref, h_ref, m_ref, dy_ref, sg_ref, gstate, halo_d):
        @pl.when(pl.program_id(0) == 0)
        def _():
            gstate[...] = jnp.zeros_like(gstate)
            halo_d[...] = jnp.zeros_like(halo_d)
            sg_ref[...] = jnp.zeros_like(sg_ref)
        xv = x_ref[...]
        cos, sin = cos_ref[...], sin_ref[...]
        win = win_ref[...].reshape(e_in, d)
        r = _even_recompute(xv, mp_ref, win, cw_ref, cos, sin, dm_ref, qd_ref, kd_ref, zh_ref,
                            lambda hh: st_ref[hh], proj_ref[...])
        parts = [r["bg"] * r["conv"]]
        for hh, hd in enumerate(r["heads"]):
            rg = cw_ref[3:4, hh * HEAD:(hh + 1) * HEAD]
            parts.append((hd["gate"] * hd["sig"]) * (hd["on"] * rg))
        m_ref[...] = _bf(jnp.concatenate(parts, axis=1))
        h_ref[...] = _bf(r["h"])

        dx1 = dx1_ref[...]
        dy = mp_ref[2:3, :] * dx1
        dyb = _bf(dy)
        dy_ref[...] = dyb
        sg_ref[2:3, :] += _sum0(dx1 * y_ref[...])
        dmix = _mm_nt(dyb, wout_ref[...].reshape(d, d))

        da_out = dmix[:, 0:cd]
        dbg = da_out * r["conv"]
        dconv = da_out * r["bg"]
        dc1 = _shift_up(dconv, 1, halo_d)
        dc2 = _shift_up(dconv, 2, halo_d)
        dz = cw_ref[2:3, :] * dconv + cw_ref[1:2, :] * dc1 + cw_ref[0:1, :] * dc2
        halo_d[...] = dconv[0:8, :]
        sg_ref[4:5, 0:cd] += _sum0(dconv * r["z2"])
        sg_ref[5:6, 0:cd] += _sum0(dconv * r["z1"])
        sg_ref[6:7, 0:cd] += _sum0(dconv * r["z"])
        dcg = dz * r["u"]
        du = dz * r["cg"]

        dqs, dks, dvs, dgs = [], [], [], []
        for hh, hd in enumerate(r["heads"]):
            lo = hh * HEAD
            rg = cw_ref[3:4, lo:lo + HEAD]
            dr = dmix[:, cd + lo:cd + lo + HEAD]
            sig, gate, on = hd["sig"], hd["gate"], hd["on"]
            rn = on * rg
            dgate = dr * rn * (sig * (1.0 + gate * (1.0 - sig)))
            drn = dr * (gate * sig)
            sg_ref[7:8, lo:lo + HEAD] += _sum0(drn * on)
            do = _rms_bwd(drn * rg, on, hd["orstd"])
            dob = _bf(do)
            gst = _bf(gstate[hh])
            scb = _bf(hd["sc"])
            vb = _bf(hd["v"])
            qrb, krb = _bf(hd["qr"]), _bf(hd["kr"])
            dv = _mm_tn(scb, dob) + _mm(_bf(hd["ks"]), gst)
            dsc = _bf(_mm_nt(dob, vb) * dm_ref[hh])
            dqr = _mm(dsc, krb) + _mm_nt(dob, _bf(st_ref[hh])) * qd_ref[:, lo:lo + HEAD]
            dkr = _mm_tn(dsc, qrb) + _mm_nt(vb, gst) * kd_ref[:, lo:lo + HEAD]
            gstate[hh] = gstate[hh] * tdec[hh] + _mm_tn(_bf(hd["qs"]), dob)
            dqs.append(_rope_t(dqr, cos, sin))
            dks.append(_rope_t(dkr * scale, cos, sin))
            dvs.append(dv)
            dgs.append(dgate)

        dproj = _bf(jnp.concatenate([dbg, dcg, du] + dqs + dks + dvs + dgs, axis=1))
        dproj_ref[...] = dproj
        dh = _mm(dproj, win)
        dxn, dsh, dsc1, dg = _normmod_bwd(dh, r["n"], r["rstd"], mp_ref[6:7, :], mp_ref[1:2, :])
        dx_ref[...] = dx1 + dxn
        sg_ref[0:1, :] += dsh
        sg_ref[1:2, :] += dsc1
        sg_ref[3:4, :] += dg

    rev = lambda i: (nt - 1 - i, 0)
    tile = pl.BlockSpec((ts, d), rev)
    rt = pl.BlockSpec((ts, HEAD), rev)
    bft = jax.ShapeDtypeStruct((s, d), BF16)
    return _pcall(
        body, name="even_bwd", grid=(nt,),
        out_shape=(jax.ShapeDtypeStruct((s, d), F32), jax.ShapeDtypeStruct((s, e_in), BF16), bft, bft, bft,
                   jax.ShapeDtypeStruct((8, d), F32)),
        in_specs=[tile, tile, tile,
                  pl.BlockSpec((None, RET_HEADS, HEAD, HEAD), lambda i: (nt - 1 - i, 0, 0, 0)),
                  pl.BlockSpec((None, 8, cd), lambda i: (nt - 1 - i, 0, 0)), pl.BlockSpec((ts, e_in), rev),
                  _full(modp.shape), _wspec4(w448, l), _full(cw.shape), rt, rt,
                  _full(dm.shape), _full(qd.shape), _full(kd.shape), _wspec4(w128, l)],
        out_specs=(tile, pl.BlockSpec((ts, e_in), rev), tile, tile, tile, _full((8, d))),
        scratch_shapes=[pltpu.VMEM((RET_HEADS, HEAD, HEAD), F32), pltpu.VMEM((8, cd), F32)],
        compiler_params=_params(1),
    )(x, dx1, y, states, zhalo, proj, modp, w448, cw, cos, sin, dm, qd, kd, w128)


def _odd_qkv_fwd(x, modp, w384, j, qkg, ts):
    s, d = x.shape
    n3 = N_DEV * w384.shape[2]

    def body(x_ref, mp_ref, w_ref, g_ref, o_ref, pre_ref):
        _, _, h = _normmod_fwd(x_ref[...], mp_ref[6:7, :], mp_ref[1:2, :], mp_ref[0:1, :])
        qkv = _mm_nt(_bf(h), w_ref[...].reshape(n3, d))
        pre_ref[...] = qkv[:, 0:2 * d]
        for hh in range(SB_HEADS):
            lo = hh * HEAD
            qn, _ = _rms_fwd(qkv[:, lo:lo + HEAD])
            kn, _ = _rms_fwd(qkv[:, d + lo:d + lo + HEAD])
            o_ref[:, lo:lo + HEAD] = _bf(qn * g_ref[0:1, :])
            o_ref[:, d + lo:d + lo + HEAD] = _bf(kn * g_ref[1:2, :])
        o_ref[:, 2 * d:3 * d] = _bf(qkv[:, 2 * d:3 * d])

    return _pcall(
        body, name="odd_qkv_fwd", grid=(s // ts,),
        out_shape=(jax.ShapeDtypeStruct((s, n3), BF16), jax.ShapeDtypeStruct((s, 2 * d), F32)),
        in_specs=[pl.BlockSpec((ts, d), lambda i: (i, 0)), _full(modp.shape), _wspec4(w384, j),
                  _full(qkg.shape)],
        out_specs=(pl.BlockSpec((ts, n3), lambda i: (i, 0)), pl.BlockSpec((ts, 2 * d), lambda i: (i, 0))),
        compiler_params=_params(1),
    )(x, modp, w384, qkg)


SB_QUERIES = 512
SB_WIDE = 256
SB_LOOP_BLOCKS = 2


def _sb_logits(q, kw, mask):
    z = _mm_nt(q, kw) * (HEAD ** -0.5)
    e = jnp.exp(-jnp.abs(z))
    lb = jnp.minimum(z, 0.0) - jnp.log(1.0 + e)
    lk = lb - z
    if mask is not None:
        lk = jnp.where(mask, lk, 0.0)
    return lb, lk


def _tri(n, above):
    ri = lax.broadcasted_iota(jnp.int32, (n, n), 0)
    ci = lax.broadcasted_iota(jnp.int32, (n, n), 1)
    return ((ri > ci) if above else (ri < ci)).astype(BF16)


def _split_dot(a, tri):
    hi = _bf(a)
    lo = _bf(a - hi.astype(F32))
    return _mm(hi, tri) + _mm(lo, tri)


def _sb_fwd(qkv, tq):
    s = qkv.shape[0]
    d = qkv.shape[1] // 3
    nq = s // tq
    assert tq % SB_WIDE == 0
    parts = tq // SB_WIDE

    def body(q_ref, k_ref, v_ref, o_ref, t_ref, o_acc, run):
        qi = pl.program_id(1)
        base = qi * tq
        upper = _tri(SB_WIDE, True)
        o_acc[...] = jnp.zeros_like(o_acc)
        run[...] = jnp.zeros_like(run)

        def wide_step(ks, row0, masked, nblk):
            rows = slice(row0, tq)
            width = nblk * SB_WIDE
            mask = None
            if masked:
                qpos = base + row0 + lax.broadcasted_iota(jnp.int32, (tq - row0, width), 0)
                mask = qpos > ks + lax.broadcasted_iota(jnp.int32, (tq - row0, width), 1)
            lb, lk = _sb_logits(q_ref[rows, :], k_ref[pl.ds(ks, width), :], mask)
            blocks = [lk[:, b * SB_WIDE:(b + 1) * SB_WIDE] for b in range(nblk)]
            right = run[rows, :]
            accs = [None] * nblk
            for b in reversed(range(nblk)):
                accs[b] = _split_dot(blocks[b], upper) + right
                right = right + jnp.sum(blocks[b], axis=1, keepdims=True)
            w = jnp.exp(lb + (accs[0] if nblk == 1 else jnp.concatenate(accs, axis=1)))
            if masked:
                w = jnp.where(mask, w, 0.0)
            o_acc[rows, :] += _mm(_bf(w), v_ref[pl.ds(ks, width), :])
            run[rows, :] = right

        for part in reversed(range(parts)):
            wide_step(pl.multiple_of(base + part * SB_WIDE, SB_WIDE), part * SB_WIDE, True, 1)
        loop_width = SB_LOOP_BLOCKS * SB_WIDE
        nsteps = qi * (tq // loop_width)

        def step(it, carry):
            wide_step(pl.multiple_of((nsteps - 1 - it) * loop_width, loop_width), 0, False, SB_LOOP_BLOCKS)
            return carry

        lax.fori_loop(0, nsteps, step, 0)
        o_ref[...] = _bf(o_acc[...])
        t_ref[...] = run[...]

    nh = d // HEAD
    return _pcall(
        body, name="sb_fwd", grid=(nh, nq),
        out_shape=(jax.ShapeDtypeStruct((s, d), BF16), jax.ShapeDtypeStruct((nh, s, 1), F32)),
        in_specs=[pl.BlockSpec((tq, HEAD), lambda h, i: (i, h)),
                  pl.BlockSpec((s, HEAD), lambda h, i: (0, nh + h)),
                  pl.BlockSpec((s, HEAD), lambda h, i: (0, 2 * nh + h))],
        out_specs=(pl.BlockSpec((tq, HEAD), lambda h, i: (i, h)),
                   pl.BlockSpec((None, tq, 1), lambda h, i: (h, i, 0))),
        scratch_shapes=[pltpu.VMEM((tq, HEAD), F32), pltpu.VMEM((tq, 1), F32)],
        compiler_params=_params(2),
    )(qkv, qkv, qkv)


def _sb_bwd(qkv, do, tot, tq):
    s = qkv.shape[0]
    d = qkv.shape[1] // 3
    nq = s // tq
    scale = HEAD ** -0.5
    assert tq % SB_WIDE == 0
    parts = tq // SB_WIDE

    def body(q_ref, k_ref, v_ref, do_ref, t_ref, dq_ref, dk_ref, dv_ref, pk, pd):
        qi = pl.program_id(1)

        @pl.when(qi == 0)
        def _():
            dk_ref[...] = jnp.zeros_like(dk_ref)
            dv_ref[...] = jnp.zeros_like(dv_ref)
        base = qi * tq
        upper = _tri(SB_WIDE, True)
        lower = _tri(SB_WIDE, False)
        dq_ref[...] = jnp.zeros_like(dq_ref)
        pk[...] = jnp.zeros_like(pk)
        pd[...] = jnp.zeros_like(pd)

        def wide_step(ks, row0, masked, nblk):
            rows = slice(row0, tq)
            width = nblk * SB_WIDE
            cut = lambda a: [a[:, b * SB_WIDE:(b + 1) * SB_WIDE] for b in range(nblk)]
            join = lambda parts_: parts_[0] if nblk == 1 else jnp.concatenate(parts_, axis=1)
            mask = None
            if masked:
                qpos = base + row0 + lax.broadcasted_iota(jnp.int32, (tq - row0, width), 0)
                mask = qpos > ks + lax.broadcasted_iota(jnp.int32, (tq - row0, width), 1)
            kw = k_ref[pl.ds(ks, width), :]
            lb, lk = _sb_logits(q_ref[rows, :], kw, mask)
            left = pk[rows, :]
            total = t_ref[rows, :]
            accs = []
            for blk in cut(lk):
                left = left + jnp.sum(blk, axis=1, keepdims=True)
                accs.append(_split_dot(blk, upper) + (total - left))
            pk[rows, :] = left
            w = jnp.exp(lb + join(accs))
            if masked:
                w = jnp.where(mask, w, 0.0)
            de = _mm_nt(do_ref[rows, :], v_ref[pl.ds(ks, width), :]) * w
            before = pd[rows, :]
            dlks = []
            for blk in cut(de):
                dlks.append(_split_dot(blk, lower) + before)
                before = before + jnp.sum(blk, axis=1, keepdims=True)
            pd[rows, :] = before
            dz = (de - jnp.exp(lb) * (de + join(dlks))) * scale
            if masked:
                dz = jnp.where(mask, dz, 0.0)
            dzb = _bf(dz)
            dq_ref[rows, :] += _mm(dzb, kw)
            dv_ref[pl.ds(ks, width), :] += _mm_tn(_bf(w), do_ref[rows, :])
            dk_ref[pl.ds(ks, width), :] += _mm_tn(dzb, q_ref[rows, :])

        loop_width = SB_LOOP_BLOCKS * SB_WIDE

        def step(jb, carry):
            wide_step(pl.multiple_of(jb * loop_width, loop_width), 0, False, SB_LOOP_BLOCKS)
            return carry

        lax.fori_loop(0, qi * (tq // loop_width), step, 0)
        for part in range(parts):
            wide_step(pl.multiple_of(base + part * SB_WIDE, SB_WIDE), part * SB_WIDE, True, 1)

    nh = d // HEAD
    shp = jax.ShapeDtypeStruct((s, d), F32)
    return _pcall(
        body, name="sb_bwd", grid=(nh, nq), out_shape=(shp, shp, shp),
        in_specs=[pl.BlockSpec((tq, HEAD), lambda h, i: (i, h)),
                  pl.BlockSpec((s, HEAD), lambda h, i: (0, nh + h)),
                  pl.BlockSpec((s, HEAD), lambda h, i: (0, 2 * nh + h)),
                  pl.BlockSpec((tq, HEAD), lambda h, i: (i, h)),
                  pl.BlockSpec((None, tq, 1), lambda h, i: (h, i, 0))],
        out_specs=(pl.BlockSpec((tq, HEAD), lambda h, i: (i, h)),
                   pl.BlockSpec((s, HEAD), lambda h, i: (0, h)),
                   pl.BlockSpec((s, HEAD), lambda h, i: (0, h))),
        scratch_shapes=[pltpu.VMEM((tq, 1), F32), pltpu.VMEM((tq, 1), F32)],
        compiler_params=_params(2),
    )(qkv, qkv, qkv, do, tot)


def _odd_out_fwd(o, x, modp, w128, slot, ts):
    s, d = x.shape

    def body(o_ref, x_ref, mp_ref, w_ref, x1_ref, y_ref):
        y = _mm(o_ref[...], w_ref[...].reshape(d, d))
        y_ref[...] = y
        x1_ref[...] = x_ref[...] + mp_ref[2:3, :] * y

    tile = pl.BlockSpec((ts, d), lambda i: (i, 0))
    shp = jax.ShapeDtypeStruct((s, d), F32)
    return _pcall(
        body, name="odd_out_fwd", grid=(s // ts,), out_shape=(shp, shp),
        in_specs=[tile, tile, _full(modp.shape), _wspec4(w128, slot)],
        out_specs=(tile, tile), compiler_params=_params(1),
    )(o, x, modp, w128)


def _odd_out_bwd(dx1, y, modp, w128, slot, ts):
    s, d = dx1.shape

    def body(dx1_ref, y_ref, mp_ref, w_ref, do_ref, dy_ref, sg_ref):
        @pl.when(pl.program_id(0) == 0)
        def _():
            sg_ref[...] = jnp.zeros_like(sg_ref)
        dx1v = dx1_ref[...]
        dyb = _bf(mp_ref[2:3, :] * dx1v)
        dy_ref[...] = dyb
        do_ref[...] = _bf(_mm_nt(dyb, w_ref[...].reshape(d, d)))
        sg_ref[2:3, :] += _sum0(dx1v * y_ref[...])

    tile = pl.BlockSpec((ts, d), lambda i: (i, 0))
    bft = jax.ShapeDtypeStruct((s, d), BF16)
    return _pcall(
        body, name="odd_out_bwd", grid=(s // ts,),
        out_shape=(bft, bft, jax.ShapeDtypeStruct((8, d), F32)),
        in_specs=[tile, tile, _full(modp.shape), _wspec4(w128, slot)],
        out_specs=(tile, tile, _full((8, d))), compiler_params=_params(1),
    )(dx1, y, modp, w128)


def _odd_qkv_bwd(x, dx1, dq, dk, dv, pre, sg_in, modp, w384, j, qkg, ts):
    s, d = x.shape
    n3 = N_DEV * w384.shape[2]

    def body(x_ref, dx1_ref, dq_ref, dk_ref, dv_ref, pre_ref, sgi_ref, mp_ref, w_ref, g_ref,
             dx_ref, dqkv_ref, h_ref, sg_ref):
        @pl.when(pl.program_id(0) == 0)
        def _():
            sg_ref[...] = sgi_ref[...]
        gmix, sc1 = mp_ref[6:7, :], mp_ref[1:2, :]
        n, rstd, h = _normmod_fwd(x_ref[...], gmix, sc1, mp_ref[0:1, :])
        hb = _bf(h)
        h_ref[...] = hb
        w = w_ref[...].reshape(n3, d)
        qkv = pre_ref[...]
        parts_q, parts_k = [], []
        gq, gk = g_ref[0:1, :], g_ref[1:2, :]
        dgq = jnp.zeros((1, HEAD), F32)
        dgk = jnp.zeros((1, HEAD), F32)
        for hh in range(SB_HEADS):
            lo = hh * HEAD
            qn, qr = _rms_fwd(qkv[:, lo:lo + HEAD])
            kn, kr = _rms_fwd(qkv[:, d + lo:d + lo + HEAD])
            dqn = dq_ref[:, lo:lo + HEAD]
            dkn = dk_ref[:, lo:lo + HEAD]
            dgq = dgq + _sum0(dqn * qn)
            dgk = dgk + _sum0(dkn * kn)
            parts_q.append(_rms_bwd(dqn * gq, qn, qr))
            parts_k.append(_rms_bwd(dkn * gk, kn, kr))
        dqkv = _bf(jnp.concatenate(parts_q + parts_k + [dv_ref[...]], axis=1))
        dqkv_ref[...] = dqkv
        dh = _mm(dqkv, w)
        dxn, dsh, dsc, dg = _normmod_bwd(dh, n, rstd, gmix, sc1)
        dx_ref[...] = dx1_ref[...] + dxn
        sg_ref[0:1, :] += dsh
        sg_ref[1:2, :] += dsc
        sg_ref[3:4, :] += dg
        sg_ref[4:5, 0:HEAD] += dgq
        sg_ref[5:6, 0:HEAD] += dgk

    tile = pl.BlockSpec((ts, d), lambda i: (i, 0))
    return _pcall(
        body, name="odd_qkv_bwd", grid=(s // ts,),
        out_shape=(jax.ShapeDtypeStruct((s, d), F32), jax.ShapeDtypeStruct((s, n3), BF16),
                   jax.ShapeDtypeStruct((s, d), BF16), jax.ShapeDtypeStruct((8, d), F32)),
        in_specs=[tile, tile, tile, tile, tile, pl.BlockSpec((ts, 2 * d), lambda i: (i, 0)), _full((8, d)),
                  _full(modp.shape), _wspec4(w384, j), _full(qkg.shape)],
        out_specs=(tile, pl.BlockSpec((ts, n3), lambda i: (i, 0)), tile, _full((8, d))),
        compiler_params=_params(1),
    )(x, dx1, dq, dk, dv, pre, sg_in, modp, w384, qkg)


def _pad_rows(a, rows):
    return jnp.concatenate([a, jnp.zeros((rows - a.shape[0],) + a.shape[1:], a.dtype)], axis=0)


def kernel(x, c, ada_w, ada_b, norm_mix_g, norm_ffn_g, ev_w_in, ev_conv_w, ev_ret_norm_g, ev_w_out, od_w_qkv, od_q_norm_g, od_k_norm_g, od_w_out, ffn_w_gate, ffn_w_up, ffn_w_down, loss_target, m_ada_w, m_ada_b, m_norm_mix_g, m_norm_ffn_g, m_ev_w_in, m_ev_conv_w, m_ev_ret_norm_g, m_ev_w_out, m_od_w_qkv, m_od_q_norm_g, m_od_k_norm_g, m_od_w_out, m_ffn_w_gate, m_ffn_w_up, m_ffn_w_down, v_ada_w, v_ada_b, v_norm_mix_g, v_norm_ffn_g, v_ev_w_in, v_ev_conv_w, v_ev_ret_norm_g, v_ev_w_out, v_od_w_qkv, v_od_q_norm_g, v_od_k_norm_g, v_od_w_out, v_ffn_w_gate, v_ffn_w_up, v_ffn_w_down):
    me = 4 * lax.axis_index("x") + 2 * lax.axis_index("y") + lax.axis_index("c")
    xs = x[0]
    tgt = loss_target[0]
    s, d = xs.shape
    depth = ada_w.shape[0]
    n_even, n_odd = ev_w_in.shape[0], od_w_qkv.shape[0]
    ts = 256
    tq = SB_QUERIES
    cd = 4 * HEAD
    cc = ev_conv_w.shape[2]

    pack0 = jnp.zeros((8, d), F32).at[0].set(c[0]).at[1, :n_even * 3 * cc].set(ev_conv_w.reshape(-1))
    got0, _ = _all_gather(pack0, "gather_cond")
    got0 = got0.reshape(N_DEV, 8, d)
    c_all = got0[:, 0, :]
    conv_all = got0[:, 1, :n_even * 3 * cc].reshape(N_DEV, n_even, 3, cc).transpose(1, 2, 0, 3)
    conv_all = conv_all.reshape(n_even, 3, N_DEV * cc)
    cols = ada_w.shape[2]
    ada_b_cols = lax.dynamic_slice(ada_b, (0, me * cols), (depth, cols))
    mod_cols = _ada_fwd(c_all, ada_w, ada_b_cols)
    got1, cond_done = _all_gather(mod_cols.reshape(depth * N_DEV, cols), "gather_mod")
    got1 = got1.reshape(N_DEV, depth, N_DEV, cols)
    mod = lax.dynamic_index_in_dim(got1, me, axis=2, keepdims=False)
    mod = mod.transpose(1, 0, 2).reshape(depth, 6, d)
    modps = [jnp.concatenate([mod[l], norm_mix_g[l][None], norm_ffn_g[l][None]], axis=0) for l in range(depth)]

    in_flight = []
    started = cond_done
    for l in range(depth):
        j = l // 2
        plain = lambda w: _bf(w + started)
        tr = lambda w: plain(w).T
        blocks = [tr(ev_w_in[j]), plain(ev_w_out[j])] if l % 2 == 0 else [tr(od_w_qkv[j]), plain(od_w_out[j])]
        mixer, started = _exchange_start(blocks, [_landing(b, me, False) for b in blocks], False,
                                         f"gather_start_mixer_{l}")
        blocks = [jnp.concatenate([tr(ffn_w_gate[l]), tr(ffn_w_up[l]), plain(ffn_w_down[l])], axis=0)]
        ffn, started = _exchange_start(blocks, [_landing(b, me, False) for b in blocks], False,
                                       f"gather_start_ffn_{l}")
        in_flight.append((mixer, ffn))
        modps[0] = modps[0] + started
    n_ffn = ffn_w_down.shape[1]

    def mixer_weights(l, after):
        got = _exchange_wait(*in_flight[l][0], False, [after], f"gather_wait_mixer_{l}")
        return got[0].reshape(N_DEV, 1, -1, d), got[1].reshape(N_DEV, 1, -1, d)

    def ffn_weights(l, after):
        got = _exchange_wait(*in_flight[l][1], False, [after], f"gather_wait_ffn_{l}")
        return got[0].reshape(N_DEV, 3, n_ffn, d)

    cos, sin = _rope_tables(s)
    consts = _retention_consts(ts)
    cws = [_pad_rows(jnp.concatenate([conv_all[j], ev_ret_norm_g[j][None]], axis=0), 8) for j in range(n_even)]
    qkgs = [_pad_rows(jnp.stack([od_q_norm_g[j], od_k_norm_g[j]]), 8) for j in range(n_odd)]

    saved = []
    weights = []
    cur = xs
    for l in range(depth):
        j = l // 2
        w_in, w_out = mixer_weights(l, cur)
        if l % 2 == 0:
            x1, y, states, zhalo, proj = _even_fwd(cur, modps[l], w_in, w_out, 0, cws[j], cos, sin, consts, ts)
            mix = (states, zhalo, proj)
        else:
            qkv, pre = _odd_qkv_fwd(cur, modps[l], w_in, 0, qkgs[j], ts)
            o, tot = _sb_fwd(qkv, tq)
            x1, y = _odd_out_fwd(o, cur, modps[l], w_out, 0, ts)
            mix = (qkv, o, tot, pre)
        w_ffn = ffn_weights(l, x1)
        weights.append((w_in, w_out, w_ffn))
        x2, f, ab = _ffn_fwd(x1, modps[l], w_ffn, 0, ts)
        saved.append((cur, x1, y, (f, ab), mix))
        cur = x2

    dx, loss_part = _loss_grad(cur, tgt, ts)
    loss = lax.psum(loss_part[0, 0], ("x", "y", "c"))

    dmod = [None] * depth
    d_gmix = [None] * depth
    d_gffn = [None] * depth
    d_conv = [None] * n_even
    d_retg = [None] * n_even
    d_qg = [None] * n_odd
    d_kg = [None] * n_odd
    grads_in_flight = [None] * depth
    for l in reversed(range(depth)):
        j = l // 2
        x0, x1, y, (f, ab), mix = saved[l]
        w_in, w_out, w_ffn = weights[l]
        g_ffn = lax.empty(w_ffn.shape, BF16)
        g_in = lax.empty(w_in.shape, BF16)
        g_out = lax.empty(w_out.shape, BF16)
        dx1, dab, h2, sv, df, sg2 = _ffn_bwd(x1, f, ab, dx, modps[l], w_ffn, 0, ts)
        g_ffn = _tn_matmul(dab, 0, h2, g_ffn, 0, "tn_gate")
        g_ffn = _tn_matmul(dab, 1, h2, g_ffn, 1, "tn_up")
        g_ffn = _tn_matmul(sv, 0, df, g_ffn, 2, "tn_down")
        if l == 0:
            pieces = [g_ffn.reshape(N_DEV, -1, d)]
            last_ffn_flight, started = _exchange_start(pieces, [_landing(p, me, True) for p in pieces], True,
                                                       "grads_start_ffn_0")
            modps[0] = modps[0] + started
        if l % 2 == 0:
            states, zhalo, proj = mix
            dx, dproj, hb, mb, dyb, sg1 = _even_bwd(x0, dx1, y, states, zhalo, proj, modps[l], w_in, w_out, 0,
                                                    cws[j], cos, sin, consts, ts)
            g_in = _tn_matmul(dproj, 0, hb, g_in, 0, "tn_ev_in")
            g_out = _tn_matmul(mb, 0, dyb, g_out, 0, "tn_ev_out")
            d_conv[j] = sg1[4:7, :cd]
            d_retg[j] = sg1[7, :cd]
        else:
            qkv, o, tot, pre = mix
            do, dyb, sg0 = _odd_out_bwd(dx1, y, modps[l], w_out, 0, ts)
            dq, dk, dv = _sb_bwd(qkv, do, tot, tq)
            dx, dqkv, hb, sg1 = _odd_qkv_bwd(x0, dx1, dq, dk, dv, pre, sg0, modps[l], w_in, 0, qkgs[j], ts)
            g_in = _tn_matmul(dqkv, 0, hb, g_in, 0, "tn_od_qkv")
            g_out = _tn_matmul(o, 0, dyb, g_out, 0, "tn_od_out")
            d_qg[j] = sg1[4, :HEAD]
            d_kg[j] = sg1[5, :HEAD]
        pieces = [g.reshape(N_DEV, -1, d) for g in ((g_ffn, g_in, g_out) if l > 0 else (g_in, g_out))]
        if l > 0:
            grads_in_flight[l], started = _exchange_start(pieces, [_landing(p, me, True) for p in pieces], True,
                                                          f"grads_start_{l}")
            modps[l - 1] = modps[l - 1] + started
        dmod[l] = jnp.concatenate([sg1[0:3], sg2[0:3]], axis=0).reshape(-1)
        d_gmix[l] = sg1[3]
        d_gffn[l] = sg2[3]

    small = jnp.concatenate(
        [jnp.stack(dmod).reshape(-1), jnp.stack(d_gmix).reshape(-1), jnp.stack(d_gffn).reshape(-1),
         jnp.stack(d_retg).reshape(-1), jnp.stack(d_qg).reshape(-1), jnp.stack(d_kg).reshape(-1),
         jnp.stack(d_conv).reshape(-1)])
    n_small = small.shape[0]
    rows_small = -(-n_small // (8 * 128)) * 8
    small = jnp.concatenate([small, jnp.zeros((rows_small * 128 - n_small,), F32)]).reshape(rows_small, 128)
    got2, small_done = _all_gather(small, "gather_small")
    got2 = got2.reshape(N_DEV, rows_small, 128)
    grads_in_flight[0], started = _exchange_start(pieces, [_landing(p, me, True, small_done) for p in pieces],
                                                  True, "grads_start_0")
    tot_small = _sum_small(got2).reshape(-1)
    n_mod = depth * 6 * d
    dmod_all = got2.reshape(N_DEV, -1)[:, :n_mod].reshape(N_DEV, depth, 6 * d)
    dmod_cols = lax.dynamic_slice(dmod_all, (0, 0, me * cols), (N_DEV, depth, cols)).transpose(1, 0, 2)
    g_ada_w = _ada_bwd(c_all.T, dmod_cols + started)

    off = [0]

    def take(shape):
        n = int(np.prod(shape))
        out = tot_small[off[0]:off[0] + n].reshape(shape)
        off[0] += n
        return out

    g_ada_b = take((depth, 6 * d))
    g_norm_mix = take((depth, d))
    g_norm_ffn = take((depth, d))
    g_ret_norm = take((n_even, cd))
    g_q_norm = take((n_odd, HEAD))
    g_k_norm = take((n_odd, HEAD))
    g_conv_full = take((n_even, 3, cd))
    g_conv = lax.dynamic_slice(g_conv_full, (0, 0, me * cc), (n_even, 3, cc))

    res = {"ada_w": (g_ada_w,) + _adamw_nd(ada_w, g_ada_w, m_ada_w, v_ada_w, "adamw_ada_w")}
    big = {"ev_w_in": (ev_w_in, m_ev_w_in, v_ev_w_in), "ev_w_out": (ev_w_out, m_ev_w_out, v_ev_w_out),
           "od_w_qkv": (od_w_qkv, m_od_w_qkv, v_od_w_qkv), "od_w_out": (od_w_out, m_od_w_out, v_od_w_out),
           "ffn_w_gate": (ffn_w_gate, m_ffn_w_gate, v_ffn_w_gate), "ffn_w_up": (ffn_w_up, m_ffn_w_up, v_ffn_w_up),
           "ffn_w_down": (ffn_w_down, m_ffn_w_down, v_ffn_w_down)}
    flipped = ("ev_w_in", "ffn_w_gate", "ffn_w_up")
    flip = lambda a: a.transpose(0, 2, 1)
    for name in flipped:
        big[name] = tuple(flip(a) for a in big[name])
    for name, (w, _, _) in big.items():
        res[name] = tuple(lax.empty(w.shape, F32) for _ in range(4))

    def update(name, idx, g_layer):
        w, m, v = big[name]
        res[name] = _adamw_layer(w, g_layer, m, v, res[name], idx, "adamw_" + name)

    after = [res["ada_w"][1]]
    for l in reversed(range(depth)):
        j = l // 2
        if l > 0:
            recv = _exchange_wait(*grads_in_flight[l], True, after, f"grads_wait_{l}")
            s_ffn, s_in, s_out = [_sum_slots(r, f"sum_{i}") for i, r in enumerate(recv)]
        else:
            recv = _exchange_wait(*last_ffn_flight, True, after, "grads_wait_ffn_0")
            s_ffn = _sum_slots(recv[0], "sum_0")
        update("ffn_w_gate", l, s_ffn[0:n_ffn])
        update("ffn_w_up", l, s_ffn[n_ffn:2 * n_ffn])
        update("ffn_w_down", l, s_ffn[2 * n_ffn:3 * n_ffn])
        if l == 0:
            after = [res[name][1] for name in ("ffn_w_gate", "ffn_w_up", "ffn_w_down")]
            recv = _exchange_wait(*grads_in_flight[l], True, after, f"grads_wait_{l}")
            s_in, s_out = [_sum_slots(r, f"sum_{1 + i}") for i, r in enumerate(recv)]
        if l % 2 == 0:
            update("ev_w_in", j, s_in)
        else:
            update("od_w_qkv", j, s_in.T)
        update("ev_w_out" if l % 2 == 0 else "od_w_out", j, s_out)
        after = [res[name][1] for name in big]
    for name in flipped:
        res[name] = tuple(flip(a) for a in res[name])

    smalls = [("ada_b", ada_b, g_ada_b, m_ada_b, v_ada_b), ("norm_mix_g", norm_mix_g, g_norm_mix, m_norm_mix_g, v_norm_mix_g),
              ("norm_ffn_g", norm_ffn_g, g_norm_ffn, m_norm_ffn_g, v_norm_ffn_g),
              ("ev_conv_w", ev_conv_w, g_conv, m_ev_conv_w, v_ev_conv_w),
              ("ev_ret_norm_g", ev_ret_norm_g, g_ret_norm, m_ev_ret_norm_g, v_ev_ret_norm_g),
              ("od_q_norm_g", od_q_norm_g, g_q_norm, m_od_q_norm_g, v_od_q_norm_g),
              ("od_k_norm_g", od_k_norm_g, g_k_norm, m_od_k_norm_g, v_od_k_norm_g)]

    def pack(arrs):
        flat = jnp.concatenate([a.reshape(-1) for a in arrs])
        rows = -(-flat.shape[0] // (8 * 128)) * 8
        return jnp.concatenate([flat, jnp.zeros((rows * 128 - flat.shape[0],), F32)]).reshape(rows, 128)

    sd, sm, sv_ = _adamw(pack([t[1] for t in smalls]), pack([t[2] for t in smalls]),
                         pack([t[3] for t in smalls]), pack([t[4] for t in smalls]), "adamw_small")
    sd, sm, sv_ = sd.reshape(-1), sm.reshape(-1), sv_.reshape(-1)
    pos = 0
    for name, w, g, m, v in smalls:
        n = int(np.prod(w.shape))
        res[name] = (g, sd[pos:pos + n].reshape(w.shape), sm[pos:pos + n].reshape(w.shape),
                     sv_[pos:pos + n].reshape(w.shape))
        pos += n

    order = ["ada_w", "ada_b", "norm_mix_g", "norm_ffn_g", "ev_w_in", "ev_conv_w", "ev_ret_norm_g", "ev_w_out",
             "od_w_qkv", "od_q_norm_g", "od_k_norm_g", "od_w_out", "ffn_w_gate", "ffn_w_up", "ffn_w_down"]
    outs = [loss, dx[None]]
    for k in range(4):
        outs += [res[name][k] for name in order]
    return tuple(outs)
```

```python
import functools
import math

import numpy as np
import jax
import jax.numpy as jnp
from jax import lax
from jax.experimental import pallas as pl
from jax.experimental.pallas import tpu as pltpu

F32 = jnp.float32
BF16 = jnp.bfloat16
MESH = pl.DeviceIdType.MESH

N_DEV = 8
EPS = 1e-6
CHUNK = 64
HEAD = 128
RET_HEADS = 4
SB_HEADS = 8
ROPE_THETA = 10000.0
KEY_BLOCK = 128
ADAM_LR, ADAM_B1, ADAM_B2, ADAM_EPS, ADAM_WD, ADAM_STEP = 0.001, 0.9, 0.999, 1e-08, 0.01, 10
VMEM_LIMIT = 56 * 1024 * 1024


def _pcall(body, **kw):
    return pl.pallas_call(body, **kw)


def _params(n_grid=1, vmem=VMEM_LIMIT):
    return pltpu.CompilerParams(dimension_semantics=("arbitrary",) * n_grid, vmem_limit_bytes=vmem)


def _mm(a, b):
    return jnp.dot(a, b, preferred_element_type=F32)


def _mm_nt(a, b):
    return lax.dot_general(a, b, (((1,), (1,)), ((), ())), preferred_element_type=F32)


def _mm_tn(a, b):
    return lax.dot_general(a, b, (((0,), (0,)), ((), ())), preferred_element_type=F32)


def _bf(a):
    return a.astype(BF16)


def _sigmoid(a):
    return 1.0 / (1.0 + jnp.exp(-a))


def _sum0(a):
    return jnp.sum(a, axis=0, keepdims=True)


def _full(shape):
    nd = len(shape)
    return pl.BlockSpec(shape, lambda *_: (0,) * nd)


def _normmod_fwd(x, g, sc, sh):
    rstd = lax.rsqrt(jnp.mean(x * x, axis=-1, keepdims=True) + EPS)
    n = x * rstd
    return n, rstd, (n * g) * (1.0 + sc) + sh


def _normmod_bwd(dh, n, rstd, g, sc):
    dsh = _sum0(dh)
    dsc = _sum0(dh * (n * g))
    dg = _sum0(dh * n * (1.0 + sc))
    dn = dh * (g * (1.0 + sc))
    dx = rstd * (dn - n * jnp.mean(dn * n, axis=-1, keepdims=True))
    return dx, dsh, dsc, dg


def _rms_fwd(o):
    rstd = lax.rsqrt(jnp.mean(o * o, axis=-1, keepdims=True) + EPS)
    return o * rstd, rstd


def _rms_bwd(dn, n, rstd):
    return rstd * (dn - n * jnp.mean(dn * n, axis=-1, keepdims=True))


def _all_gather(x2d, name):
    m_per, n = x2d.shape
    space = pltpu.VMEM

    def body(x_ref, out_ref, done_ref, send_sems, recv_sems, local_sem):
        x, y, c = lax.axis_index("x"), lax.axis_index("y"), lax.axis_index("c")
        me, sibling = (x, y, c), (x, y, 1 - c)
        chips = [(1 - x, y), (x, 1 - y), (1 - x, 1 - y)]

        def rows(px, py, pc):
            return out_ref.at[pl.ds((4 * px + 2 * py + pc) * m_per, m_per), :]

        def copy(k, block, to, src=None):
            return pltpu.make_async_remote_copy(
                src_ref=rows(*block) if src is None else src, dst_ref=rows(*block),
                send_sem=send_sems.at[k], recv_sem=recv_sems.at[k],
                device_id=to, device_id_type=MESH)

        mine = pltpu.make_async_copy(x_ref, rows(*me), local_sem)
        mine.start()
        first = [copy(1 + j, me, (*chip, c), src=x_ref) for j, chip in enumerate(chips)]
        first += [copy(0, me, sibling, src=x_ref)]
        for cp in first:
            cp.start()
        passed = [copy(4 + j, (*chip, c), sibling) for j, chip in enumerate(chips)]
        for j, chip in enumerate(chips):
            copy(1 + j, (*chip, c), me).wait_recv()
            passed[j].start()
        copy(0, sibling, me).wait_recv()
        for j, chip in enumerate(chips):
            copy(4 + j, (*chip, 1 - c), me).wait_recv()
        for cp in first + passed:
            cp.wait_send()
        mine.wait()
        done_ref[...] = jnp.zeros_like(done_ref)

    out, done = _pcall(
        body, name=name,
        out_shape=(jax.ShapeDtypeStruct((N_DEV * m_per, n), x2d.dtype), jax.ShapeDtypeStruct((8, 128), F32)),
        in_specs=[pl.BlockSpec(memory_space=space)],
        out_specs=(pl.BlockSpec(memory_space=space), pl.BlockSpec(memory_space=pltpu.VMEM)),
        scratch_shapes=[pltpu.SemaphoreType.DMA((7,)), pltpu.SemaphoreType.DMA((7,)),
                        pltpu.SemaphoreType.DMA],
    )(x2d)
    return out, done[0, 0]


_HBM = pl.BlockSpec(memory_space=pltpu.HBM)
_SEM = pl.BlockSpec(memory_space=pltpu.SEMAPHORE)
_EFFECT = pltpu.SideEffectType.DATAFLOW_SIDE_EFFECTING


def _exchange_copies(src_refs, land_refs, send_sems, recv_sems, scatter):
    x, y, c = lax.axis_index("x"), lax.axis_index("y"), lax.axis_index("c")
    me = 4 * x + 2 * y + c
    out = []
    for i, (s_ref, l_ref) in enumerate(zip(src_refs, land_refs)):
        for k in (2, 4, 6, 3, 5, 7, 1):
            px = (1 - x) if (k >> 2) & 1 else x
            py = (1 - y) if (k >> 1) & 1 else y
            pc = (1 - c) if k & 1 else c
            out.append(pltpu.make_async_remote_copy(
                src_ref=s_ref.at[4 * px + 2 * py + pc] if scatter else s_ref, dst_ref=l_ref.at[me],
                send_sem=send_sems.at[7 * i + k - 1], recv_sem=recv_sems.at[7 * i + k - 1],
                device_id=(px, py, pc), device_id_type=MESH))
    return out


def _exchange_start(srcs, lands, scatter, name):
    n = len(srcs)

    def body(*refs):
        for cp in _exchange_copies(refs[:n], refs[n:2 * n], refs[2 * n], refs[2 * n + 1], scatter):
            cp.start()
        refs[-1][...] = jnp.zeros_like(refs[-1])

    arrays = list(srcs) + list(lands)
    outs = _pcall(
        body, name=name,
        out_shape=(pltpu.SemaphoreType.DMA((7 * n,)), pltpu.SemaphoreType.DMA((7 * n,)),
                   *[pltpu.HBM(a.shape, a.dtype) for a in arrays], jax.ShapeDtypeStruct((8, 128), F32)),
        in_specs=[_HBM] * (2 * n),
        out_specs=(_SEM, _SEM, *[_HBM] * (2 * n), pl.BlockSpec(memory_space=pltpu.VMEM)),
        input_output_aliases={i: 2 + i for i in range(2 * n)},
        compiler_params=pltpu.CompilerParams(has_side_effects=_EFFECT),
    )(*[pltpu.with_memory_space_constraint(a, pltpu.HBM) for a in arrays])
    return (outs[0], outs[1], list(outs[2:2 + n]), list(outs[2 + n:2 + 2 * n])), outs[-1][0, 0]


def _exchange_wait(send_sems, recv_sems, srcs, lands, scatter, after, name):
    n = len(srcs)
    after = list(after)

    def body(*refs):
        for cp in _exchange_copies(refs[:n], refs[n:2 * n], refs[2 * n], refs[2 * n + 1], scatter):
            cp.wait_send()
            cp.wait_recv()

    arrays = list(srcs) + list(lands)
    outs = _pcall(
        body, name=name,
        out_shape=tuple(pltpu.HBM(a.shape, a.dtype) for a in arrays),
        in_specs=[_HBM] * (2 * n) + [_SEM, _SEM] + [pl.BlockSpec(memory_space=pl.ANY)] * len(after),
        out_specs=tuple([_HBM] * (2 * n)),
        input_output_aliases={i: i for i in range(2 * n)},
        compiler_params=pltpu.CompilerParams(has_side_effects=_EFFECT),
    )(*arrays, send_sems, recv_sems, *after)
    return list(outs[n:])


def _landing(src, me, scatter, after=None):
    own = lax.dynamic_index_in_dim(src, me, 0, keepdims=True) if scatter else src[None]
    if after is not None:
        own = own + after.astype(own.dtype)
    shape = src.shape if scatter else (N_DEV,) + src.shape
    return lax.dynamic_update_slice(lax.empty(shape, src.dtype), own, (me, 0, 0))


def _sum_slots(recv, name):
    _, r, n = recv.shape
    tr = r
    for cand in (512, 448, 384, 352, 256, 128, 64, 32, 16, 8):
        if r % cand == 0:
            tr = cand
            break

    def body(r_ref, o_ref):
        acc = r_ref[0].astype(F32)
        for p in range(1, N_DEV):
            acc = acc + r_ref[p].astype(F32)
        o_ref[...] = acc

    return _pcall(
        body, name=name, grid=(r // tr,),
        out_shape=jax.ShapeDtypeStruct((r, n), F32),
        in_specs=[pl.BlockSpec((N_DEV, tr, n), lambda i: (0, i, 0))],
        out_specs=pl.BlockSpec((tr, n), lambda i: (i, 0)),
        compiler_params=_params(1),
    )(recv)


def _row_tile(rows, limit=512):
    for cand in range(min(limit, rows) // 8 * 8, 7, -8):
        if rows % cand == 0:
            return cand
    return rows


def _adamw(w, g, m, v, name):
    r, n = w.shape
    tr = _row_tile(r)

    def body(w_ref, g_ref, m_ref, v_ref, d_ref, nm_ref, nv_ref):
        d_ref[...], nm_ref[...], nv_ref[...] = _adam_update(w_ref[...], g_ref[...], m_ref[...], v_ref[...])

    spec = pl.BlockSpec((tr, n), lambda i: (i, 0))
    shp = jax.ShapeDtypeStruct((r, n), F32)
    return _pcall(
        body, name=name, grid=(r // tr,), out_shape=(shp, shp, shp),
        in_specs=[spec] * 4, out_specs=(spec, spec, spec), compiler_params=_params(1),
    )(w, g, m, v)


def _adam_update(wv, gv, mv, vv):
    bc1 = 1.0 / (1.0 - ADAM_B1 ** ADAM_STEP)
    bc2 = 1.0 / (1.0 - ADAM_B2 ** ADAM_STEP)
    nm = ADAM_B1 * mv + (1.0 - ADAM_B1) * gv
    nv = ADAM_B2 * vv + (1.0 - ADAM_B2) * (gv * gv)
    return -ADAM_LR * ((nm * bc1) / (jnp.sqrt(nv * bc2) + ADAM_EPS) + ADAM_WD * wv), nm, nv


def _adamw_layer(w, g_layer, m, v, outs, idx, name):
    _, a, b = w.shape
    tr = _row_tile(a)

    def body(w_ref, g_ref, m_ref, v_ref, o0, o1, o2, o3, go_ref, d_ref, nm_ref, nv_ref):
        gv = g_ref[...]
        go_ref[...] = gv
        d_ref[...], nm_ref[...], nv_ref[...] = _adam_update(w_ref[...], gv, m_ref[...], v_ref[...])

    layer = pl.BlockSpec((None, tr, b), lambda i: (idx, i, 0))
    anyw = pl.BlockSpec(memory_space=pl.ANY)
    shp = jax.ShapeDtypeStruct(w.shape, F32)
    return tuple(_pcall(
        body, name=name, grid=(a // tr,), out_shape=(shp,) * 4,
        in_specs=[layer, pl.BlockSpec((tr, b), lambda i: (i, 0)), layer, layer, anyw, anyw, anyw, anyw],
        out_specs=(layer,) * 4, input_output_aliases={4: 0, 5: 1, 6: 2, 7: 3},
        compiler_params=_params(1),
    )(w, g_layer, m, v, *outs))


def _adamw_nd(w, g, m, v, name):
    shp = w.shape
    f = lambda a: a.reshape(-1, shp[-1])
    d, nm, nv = _adamw(f(w), f(g), f(m), f(v), name)
    return d.reshape(shp), nm.reshape(shp), nv.reshape(shp)


def _ada_fwd(c_all, ada_w, ada_b_cols):
    n_l, d, cols = ada_w.shape

    def body(c_ref, w_ref, b_ref, o_ref):
        cv = c_ref[...]
        ca = cv * _sigmoid(cv)
        o_ref[...] = _mm(_bf(ca), _bf(w_ref[...])) + b_ref[...]

    return _pcall(
        body, name="ada_fwd", grid=(n_l,),
        out_shape=jax.ShapeDtypeStruct((n_l, N_DEV, cols), F32),
        in_specs=[_full((N_DEV, d)), pl.BlockSpec((None, d, cols), lambda l: (l, 0, 0)),
                  pl.BlockSpec((None, 1, cols), lambda l: (l, 0, 0))],
        out_specs=pl.BlockSpec((None, N_DEV, cols), lambda l: (l, 0, 0)),
        compiler_params=_params(1),
    )(c_all, ada_w, ada_b_cols.reshape(n_l, 1, cols))


def _ada_bwd(c_all_t, dmod_cols):
    d = c_all_t.shape[0]
    n_l, _, cols = dmod_cols.shape

    def body(ct_ref, dm_ref, o_ref):
        cv = ct_ref[...]
        ca = cv * _sigmoid(cv)
        dm = dm_ref[...]
        acc = ca[:, 0:1] * dm[0:1, :]
        for b in range(1, N_DEV):
            acc = acc + ca[:, b:b + 1] * dm[b:b + 1, :]
        o_ref[...] = acc

    return _pcall(
        body, name="ada_bwd", grid=(n_l,),
        out_shape=jax.ShapeDtypeStruct((n_l, d, cols), F32),
        in_specs=[_full((d, N_DEV)), pl.BlockSpec((None, N_DEV, cols), lambda l: (l, 0, 0))],
        out_specs=pl.BlockSpec((None, d, cols), lambda l: (l, 0, 0)),
        compiler_params=_params(1),
    )(c_all_t, dmod_cols)


def _sum_small(gathered):
    _, r, n = gathered.shape

    def body(g_ref, o_ref):
        acc = g_ref[0]
        for p in range(1, N_DEV):
            acc = acc + g_ref[p]
        o_ref[...] = acc

    return _pcall(
        body, name="sum_small", out_shape=jax.ShapeDtypeStruct((r, n), F32),
        in_specs=[_full((N_DEV, r, n))], out_specs=_full((r, n)),
    )(gathered)


def _loss_grad(xf, tgt, ts):
    s, d = xf.shape

    def body(x_ref, t_ref, dx_ref, l_ref):
        @pl.when(pl.program_id(0) == 0)
        def _():
            l_ref[...] = jnp.zeros_like(l_ref)
        e = x_ref[...] - t_ref[...]
        dx_ref[...] = e * (1.0 / d)
        l_ref[...] += (0.5 / d) * jnp.sum(jnp.sum(e * e, axis=1, keepdims=True), axis=0, keepdims=True)

    spec = pl.BlockSpec((ts, d), lambda i: (i, 0))
    return _pcall(
        body, name="loss_grad", grid=(s // ts,),
        out_shape=(jax.ShapeDtypeStruct((s, d), F32), jax.ShapeDtypeStruct((1, 1), F32)),
        in_specs=[spec, spec], out_specs=(spec, _full((1, 1))), compiler_params=_params(1),
    )(xf, tgt)


def _tn_matmul(a, col_block, b, buf, slot, name):
    s = a.shape[0]
    k = b.shape[1]
    n_p = buf.shape[2]
    mcols = N_DEV * n_p
    ts = _row_tile(s, 1024)
    nt = s // ts

    def body(a_ref, b_ref, buf_ref, o_ref, acc):
        i = pl.program_id(0)

        @pl.when(i == 0)
        def _():
            acc[...] = jnp.zeros_like(acc)
        acc[...] += _mm_tn(a_ref[...], b_ref[...])

        @pl.when(i == nt - 1)
        def _():
            o_ref[...] = acc[...].reshape(N_DEV, n_p, k).astype(BF16)

    return _pcall(
        body, name=name, grid=(nt,),
        out_shape=jax.ShapeDtypeStruct(buf.shape, BF16),
        in_specs=[pl.BlockSpec((ts, mcols), lambda i: (i, col_block)),
                  pl.BlockSpec((ts, k), lambda i: (i, 0)),
                  pl.BlockSpec(memory_space=pl.ANY)],
        out_specs=pl.BlockSpec((N_DEV, None, n_p, k), lambda i: (0, slot, 0, 0)),
        scratch_shapes=[pltpu.VMEM((mcols, k), F32)],
        input_output_aliases={2: 0},
        compiler_params=_params(1),
    )(a, b, buf)


def _wspec4(w, slot):
    _, _, n_p, k = w.shape
    return pl.BlockSpec((N_DEV, None, n_p, k), lambda i: (0, slot, 0, 0), pipeline_mode=pl.Buffered(1))


def _ffn_fwd(x1, modp, w352, l, ts):
    s, d = x1.shape
    n_l = w352.shape[1] // 3
    f_dim = N_DEV * w352.shape[2]

    def body(x_ref, mp_ref, wg_ref, wu_ref, wd_ref, x2_ref, f_ref, ab_ref):
        x = x_ref[...]
        _, _, h2 = _normmod_fwd(x, mp_ref[7:8, :], mp_ref[4:5, :], mp_ref[3:4, :])
        hb = _bf(h2)
        f = jnp.zeros((ts, d), F32)
        half_dev, fc = N_DEV // 2, f_dim // 2
        for part in range(2):
            dev0, c0 = part * half_dev, part * fc
            a = _mm_nt(hb, wg_ref[dev0:dev0 + half_dev].reshape(fc, d))
            b = _mm_nt(hb, wu_ref[dev0:dev0 + half_dev].reshape(fc, d))
            ab_ref[:, c0:c0 + fc] = a
            ab_ref[:, f_dim + c0:f_dim + c0 + fc] = b
            sv = (a * _sigmoid(a)) * b
            f = f + _mm(_bf(sv), wd_ref[dev0:dev0 + half_dev].reshape(fc, d))
        f_ref[...] = f
        x2_ref[...] = x + mp_ref[5:6, :] * f

    tile = pl.BlockSpec((ts, d), lambda i: (i, 0))
    shp = jax.ShapeDtypeStruct((s, d), F32)
    return _pcall(
        body, name="ffn_fwd", grid=(s // ts,),
        out_shape=(shp, shp, jax.ShapeDtypeStruct((s, 2 * f_dim), F32)),
        in_specs=[tile, _full(modp.shape), _wspec4(w352, l), _wspec4(w352, n_l + l),
                  _wspec4(w352, 2 * n_l + l)],
        out_specs=(tile, tile, pl.BlockSpec((ts, 2 * f_dim), lambda i: (i, 0))), compiler_params=_params(1),
    )(x1, modp, w352, w352, w352)


def _ffn_bwd(x1, f, ab, dx2, modp, w352, l, ts):
    s, d = x1.shape
    n_l = w352.shape[1] // 3
    f_dim = N_DEV * w352.shape[2]

    def body(x_ref, f_ref, ab_ref, dx2_ref, mp_ref, wg_ref, wu_ref, wd_ref,
             dx1_ref, dab_ref, h2_ref, s_ref, df_ref, sg_ref):
        @pl.when(pl.program_id(0) == 0)
        def _():
            sg_ref[...] = jnp.zeros_like(sg_ref)
        x = x_ref[...]
        gffn, sc2, g2 = mp_ref[7:8, :], mp_ref[4:5, :], mp_ref[5:6, :]
        n, rstd, h2 = _normmod_fwd(x, gffn, sc2, mp_ref[3:4, :])
        hb = _bf(h2)
        dx2 = dx2_ref[...]
        dfb = _bf(g2 * dx2)
        dh2 = jnp.zeros((ts, d), F32)
        half_dev, fc = N_DEV // 2, f_dim // 2
        for part in range(2):
            dev0, c0 = part * half_dev, part * fc
            wg = wg_ref[dev0:dev0 + half_dev].reshape(fc, d)
            wu = wu_ref[dev0:dev0 + half_dev].reshape(fc, d)
            a = ab_ref[:, c0:c0 + fc]
            b = ab_ref[:, f_dim + c0:f_dim + c0 + fc]
            sig = _sigmoid(a)
            sa = a * sig
            s_ref[:, c0:c0 + fc] = _bf(sa * b)
            ds = _mm_nt(dfb, wd_ref[dev0:dev0 + half_dev].reshape(fc, d))
            dab = _bf(ds * b * (sig * (1.0 + a * (1.0 - sig))))
            dbb = _bf(ds * sa)
            dab_ref[:, c0:c0 + fc] = dab
            dab_ref[:, f_dim + c0:f_dim + c0 + fc] = dbb
            dh2 = dh2 + _mm(dab, wg) + _mm(dbb, wu)
        dxn, dsh, dsc, dg = _normmod_bwd(dh2, n, rstd, gffn, sc2)
        dx1_ref[...] = dx2 + dxn
        h2_ref[...] = hb
        df_ref[...] = dfb
        sg_ref[0:1, :] += dsh
        sg_ref[1:2, :] += dsc
        sg_ref[2:3, :] += _sum0(dx2 * f_ref[...])
        sg_ref[3:4, :] += dg

    tile = pl.BlockSpec((ts, d), lambda i: (i, 0))
    f32t = jax.ShapeDtypeStruct((s, d), F32)
    bft = jax.ShapeDtypeStruct((s, d), BF16)
    return _pcall(
        body, name="ffn_bwd", grid=(s // ts,),
        out_shape=(f32t, jax.ShapeDtypeStruct((s, 2 * f_dim), BF16), bft,
                   jax.ShapeDtypeStruct((s, f_dim), BF16), bft, jax.ShapeDtypeStruct((8, d), F32)),
        in_specs=[tile, tile, pl.BlockSpec((ts, 2 * f_dim), lambda i: (i, 0)), tile, _full(modp.shape),
                  _wspec4(w352, l), _wspec4(w352, n_l + l), _wspec4(w352, 2 * n_l + l)],
        out_specs=(tile, pl.BlockSpec((ts, 2 * f_dim), lambda i: (i, 0)), tile,
                   pl.BlockSpec((ts, f_dim), lambda i: (i, 0)), tile, _full((8, d))),
        compiler_params=_params(1),
    )(x1, f, ab, dx2, modp, w352, w352, w352)


def _retention_consts(ts):
    h = np.arange(RET_HEADS, dtype=np.float64)
    log_g = np.log1p(-np.exp2(-5.0 - h))
    t = np.arange(ts)
    diff = t[:, None] - t[None, :]
    same = (t[:, None] // CHUNK) == (t[None, :] // CHUNK)
    later = (t[:, None] // CHUNK) > (t[None, :] // CHUNK)
    dm = np.where(same, np.abs(diff), np.where(later, diff, 0))[None] * log_g[:, None, None]
    dm = np.where((same | later)[None], np.exp(dm), 0.0)
    qd = np.exp((t[:, None] + 1.0) * log_g[None, :])
    kd = np.exp((ts - 1.0 - t[:, None]) * log_g[None, :])
    qd = np.repeat(qd, HEAD, axis=1)
    kd = np.repeat(kd, HEAD, axis=1)
    tdec = [float(np.exp(ts * lg)) for lg in log_g]
    return (jnp.asarray(dm, F32), jnp.asarray(qd, F32), jnp.asarray(kd, F32), tdec)


def _rope_tables(s):
    inv_freq = 1.0 / (ROPE_THETA ** (jnp.arange(0, HEAD, 2, dtype=F32) / HEAD))
    ang = jnp.arange(s, dtype=F32)[:, None] * inv_freq[None, :]
    cos, sin = jnp.cos(ang), jnp.sin(ang)
    return jnp.concatenate([cos, cos], axis=1), jnp.concatenate([-sin, sin], axis=1)


def _rope(v, cos, sin):
    return v * cos + pltpu.roll(v, HEAD // 2, 1) * sin


def _rope_t(dv, cos, sin):
    return dv * cos + pltpu.roll(dv * sin, HEAD // 2, 1)


def _shift_down(z, k, halo_ref):
    r = pltpu.roll(z, k, 0)
    rows = lax.broadcasted_iota(jnp.int32, z.shape, 0)
    for j in range(k):
        r = jnp.where(rows == j, halo_ref[8 - k + j:8 - k + j + 1, :], r)
    return r


def _shift_up(z, k, halo_ref):
    n = z.shape[0]
    r = pltpu.roll(z, n - k, 0)
    rows = lax.broadcasted_iota(jnp.int32, z.shape, 0)
    for j in range(k):
        r = jnp.where(rows == n - k + j, halo_ref[j:j + 1, :], r)
    return r


def _even_recompute(x, mp_ref, win, cw_ref, cos, sin, dm_ref, qd_ref, kd_ref, halo_ref, state_of, proj=None):
    cd = 4 * HEAD
    n, rstd, h = _normmod_fwd(x, mp_ref[6:7, :], mp_ref[1:2, :], mp_ref[0:1, :])
    if proj is None:
        proj = _mm_nt(_bf(h), win)
    bg, cg, u = proj[:, 0:cd], proj[:, cd:2 * cd], proj[:, 2 * cd:3 * cd]
    z = cg * u
    z1 = _shift_down(z, 1, halo_ref)
    z2 = _shift_down(z, 2, halo_ref)
    conv = cw_ref[0:1, :] * z2 + cw_ref[1:2, :] * z1 + cw_ref[2:3, :] * z
    heads = []
    scale = HEAD ** -0.5
    for hh in range(RET_HEADS):
        lo = hh * HEAD
        q = proj[:, 3 * cd + lo:3 * cd + lo + HEAD]
        k = proj[:, 4 * cd + lo:4 * cd + lo + HEAD]
        v = proj[:, 5 * cd + lo:5 * cd + lo + HEAD]
        gate = proj[:, 6 * cd + lo:6 * cd + lo + HEAD]
        qr = _rope(q, cos, sin)
        kr = _rope(k, cos, sin) * scale
        sc = _mm_nt(_bf(qr), _bf(kr)) * dm_ref[hh]
        qs = qr * qd_ref[:, lo:lo + HEAD]
        ks = kr * kd_ref[:, lo:lo + HEAD]
        o = _mm(_bf(sc), _bf(v)) + _mm(_bf(qs), _bf(state_of(hh)))
        on, orstd = _rms_fwd(o)
        sig = _sigmoid(gate)
        heads.append(dict(qr=qr, kr=kr, v=v, gate=gate, sc=sc, qs=qs, ks=ks, on=on, orstd=orstd, sig=sig))
    return dict(n=n, rstd=rstd, h=h, proj=proj, bg=bg, cg=cg, u=u, z=z, z1=z1, z2=z2, conv=conv, heads=heads)


def _even_fwd(x, modp, w448, w128, l, cw, cos, sin, consts, ts):
    s, d = x.shape
    nt = s // ts
    dm, qd, kd, tdec = consts
    cd = 4 * HEAD
    e_in = N_DEV * w448.shape[2]

    def body(x_ref, mp_ref, win_ref, cw_ref, cos_ref, sin_ref, dm_ref, qd_ref, kd_ref, wout_ref,
             x1_ref, y_ref, st_ref, zh_ref, proj_ref, state, halo):
        @pl.when(pl.program_id(0) == 0)
        def _():
            state[...] = jnp.zeros_like(state)
            halo[...] = jnp.zeros_like(halo)
        xv = x_ref[...]
        st_ref[...] = state[...]
        zh_ref[...] = halo[...]
        r = _even_recompute(xv, mp_ref, win_ref[...].reshape(e_in, d), cw_ref, cos_ref[...], sin_ref[...],
                            dm_ref, qd_ref, kd_ref, halo, lambda hh: state[hh])
        proj_ref[...] = r["proj"]
        halo[...] = r["z"][ts - 8:ts, :]
        parts = [r["bg"] * r["conv"]]
        for hh, hd in enumerate(r["heads"]):
            state[hh] = state[hh] * tdec[hh] + _mm_tn(_bf(hd["ks"]), _bf(hd["v"]))
            rg = cw_ref[3:4, hh * HEAD:(hh + 1) * HEAD]
            parts.append((hd["gate"] * hd["sig"]) * (hd["on"] * rg))
        mcat = jnp.concatenate(parts, axis=1)
        y = _mm(_bf(mcat), wout_ref[...].reshape(d, d))
        y_ref[...] = y
        x1_ref[...] = xv + mp_ref[2:3, :] * y

    tile = pl.BlockSpec((ts, d), lambda i: (i, 0))
    rt = pl.BlockSpec((ts, HEAD), lambda i: (i, 0))
    shp = jax.ShapeDtypeStruct((s, d), F32)
    return _pcall(
        body, name="even_fwd", grid=(nt,),
        out_shape=(shp, shp, jax.ShapeDtypeStruct((nt, RET_HEADS, HEAD, HEAD), F32),
                   jax.ShapeDtypeStruct((nt, 8, cd), F32), jax.ShapeDtypeStruct((s, e_in), F32)),
        in_specs=[tile, _full(modp.shape), _wspec4(w448, l), _full(cw.shape), rt, rt,
                  _full(dm.shape), _full(qd.shape), _full(kd.shape), _wspec4(w128, l)],
        out_specs=(tile, tile, pl.BlockSpec((None, RET_HEADS, HEAD, HEAD), lambda i: (i, 0, 0, 0)),
                   pl.BlockSpec((None, 8, cd), lambda i: (i, 0, 0)), pl.BlockSpec((ts, e_in), lambda i: (i, 0))),
        scratch_shapes=[pltpu.VMEM((RET_HEADS, HEAD, HEAD), F32), pltpu.VMEM((8, cd), F32)],
        compiler_params=_params(1),
    )(x, modp, w448, cw, cos, sin, dm, qd, kd, w128)


def _even_bwd(x, dx1, y, states, zhalo, proj, modp, w448, w128, l, cw, cos, sin, consts, ts):
    s, d = x.shape
    nt = s // ts
    dm, qd, kd, tdec = consts
    cd = 4 * HEAD
    e_in = N_DEV * w448.shape[2]
    scale = HEAD ** -0.5

    def body(x_ref, dx1_ref, y_ref, st_ref, zh_ref, proj_ref, mp_ref, win_ref, cw_ref, cos_ref, sin_ref,
             dm_ref, qd_ref, kd_ref, wout_ref,
             dx_ref, dproj_ref, h_ref, m_ref, dy_ref, sg_ref, gstate, halo_d):
        @pl.when(pl.program_id(0) == 0)
        def _():
            gstate[...] = jnp.zeros_like(gstate)
            halo_d[...] = jnp.zeros_like(halo_d)
            sg_ref[...] = jnp.zeros_like(sg_ref)
        xv = x_ref[...]
        cos, sin = cos_ref[...], sin_ref[...]
        win = win_ref[...].reshape(e_in, d)
        r = _even_recompute(xv, mp_ref, win, cw_ref, cos, sin, dm_ref, qd_ref, kd_ref, zh_ref,
                            lambda hh: st_ref[hh], proj_ref[...])
        parts = [r["bg"] * r["conv"]]
        for hh, hd in enumerate(r["heads"]):
            rg = cw_ref[3:4, hh * HEAD:(hh + 1) * HEAD]
            parts.append((hd["gate"] * hd["sig"]) * (hd["on"] * rg))
        m_ref[...] = _bf(jnp.concatenate(parts, axis=1))
        h_ref[...] = _bf(r["h"])

        dx1 = dx1_ref[...]
        dy = mp_ref[2:3, :] * dx1
        dyb = _bf(dy)
        dy_ref[...] = dyb
        sg_ref[2:3, :] += _sum0(dx1 * y_ref[...])
        dmix = _mm_nt(dyb, wout_ref[...].reshape(d, d))

        da_out = dmix[:, 0:cd]
        dbg = da_out * r["conv"]
        dconv = da_out * r["bg"]
        dc1 = _shift_up(dconv, 1, halo_d)
        dc2 = _shift_up(dconv, 2, halo_d)
        dz = cw_ref[2:3, :] * dconv + cw_ref[1:2, :] * dc1 + cw_ref[0:1, :] * dc2
        halo_d[...] = dconv[0:8, :]
        sg_ref[4:5, 0:cd] += _sum0(dconv * r["z2"])
        sg_ref[5:6, 0:cd] += _sum0(dconv * r["z1"])
        sg_ref[6:7, 0:cd] += _sum0(dconv * r["z"])
        dcg = dz * r["u"]
        du = dz * r["cg"]

        dqs, dks, dvs, dgs = [], [], [], []
        for hh, hd in enumerate(r["heads"]):
            lo = hh * HEAD
            rg = cw_ref[3:4, lo:lo + HEAD]
            dr = dmix[:, cd + lo:cd + lo + HEAD]
            sig, gate, on = hd["sig"], hd["gate"], hd["on"]
            rn = on * rg
            dgate = dr * rn * (sig * (1.0 + gate * (1.0 - sig)))
            drn = dr * (gate * sig)
            sg_ref[7:8, lo:lo + HEAD] += _sum0(drn * on)
            do = _rms_bwd(drn * rg, on, hd["orstd"])
            dob = _bf(do)
            gst = _bf(gstate[hh])
            scb = _bf(hd["sc"])
            vb = _bf(hd["v"])
            qrb, krb = _bf(hd["qr"]), _bf(hd["kr"])
            dv = _mm_tn(scb, dob) + _mm(_bf(hd["ks"]), gst)
            dsc = _bf(_mm_nt(dob, vb) * dm_ref[hh])
            dqr = _mm(dsc, krb) + _mm_nt(dob, _bf(st_ref[hh])) * qd_ref[:, lo:lo + HEAD]
            dkr = _mm_tn(dsc, qrb) + _mm_nt(vb, gst) * kd_ref[:, lo:lo + HEAD]
            gstate[hh] = gstate[hh] * tdec[hh] + _mm_tn(_bf(hd["qs"]), dob)
            dqs.append(_rope_t(dqr, cos, sin))
            dks.append(_rope_t(dkr * scale, cos, sin))
            dvs.append(dv)
            dgs.append(dgate)

        dproj = _bf(jnp.concatenate([dbg, dcg, du] + dqs + dks + dvs + dgs, axis=1))
        dproj_ref[...] = dproj
        dh = _mm(dproj, win)
        dxn, dsh, dsc1, dg = _normmod_bwd(dh, r["n"], r["rstd"], mp_ref[6:7, :], mp_ref[1:2, :])
        dx_ref[...] = dx1 + dxn
        sg_ref[0:1, :] += dsh
        sg_ref[1:2, :] += dsc1
        sg_ref[3:4, :] += dg

    rev = lambda i: (nt - 1 - i, 0)
    tile = pl.BlockSpec((ts, d), rev)
    rt = pl.BlockSpec((ts, HEAD), rev)
    bft = jax.ShapeDtypeStruct((s, d), BF16)
    return _pcall(
        body, name="even_bwd", grid=(nt,),
        out_shape=(jax.ShapeDtypeStruct((s, d), F32), jax.ShapeDtypeStruct((s, e_in), BF16), bft, bft, bft,
                   jax.ShapeDtypeStruct((8, d), F32)),
        in_specs=[tile, tile, tile,
                  pl.BlockSpec((None, RET_HEADS, HEAD, HEAD), lambda i: (nt - 1 - i, 0, 0, 0)),
                  pl.BlockSpec((None, 8, cd), lambda i: (nt - 1 - i, 0, 0)), pl.BlockSpec((ts, e_in), rev),
                  _full(modp.shape), _wspec4(w448, l), _full(cw.shape), rt, rt,
                  _full(dm.shape), _full(qd.shape), _full(kd.shape), _wspec4(w128, l)],
        out_specs=(tile, pl.BlockSpec((ts, e_in), rev), tile, tile, tile, _full((8, d))),
        scratch_shapes=[pltpu.VMEM((RET_HEADS, HEAD, HEAD), F32), pltpu.VMEM((8, cd), F32)],
        compiler_params=_params(1),
    )(x, dx1, y, states, zhalo, proj, modp, w448, cw, cos, sin, dm, qd, kd, w128)


def _odd_qkv_fwd(x, modp, w384, j, qkg, ts):
    s, d = x.shape
    n3 = N_DEV * w384.shape[2]

    def body(x_ref, mp_ref, w_ref, g_ref, o_ref, pre_ref):
        _, _, h = _normmod_fwd(x_ref[...], mp_ref[6:7, :], mp_ref[1:2, :], mp_ref[0:1, :])
        qkv = _mm_nt(_bf(h), w_ref[...].reshape(n3, d))
        pre_ref[...] = qkv[:, 0:2 * d]
        for hh in range(SB_HEADS):
            lo = hh * HEAD
            qn, _ = _rms_fwd(qkv[:, lo:lo + HEAD])
            kn, _ = _rms_fwd(qkv[:, d + lo:d + lo + HEAD])
            o_ref[:, lo:lo + HEAD] = _bf(qn * g_ref[0:1, :])
            o_ref[:, d + lo:d + lo + HEAD] = _bf(kn * g_ref[1:2, :])
        o_ref[:, 2 * d:3 * d] = _bf(qkv[:, 2 * d:3 * d])

    return _pcall(
        body, name="odd_qkv_fwd", grid=(s // ts,),
        out_shape=(jax.ShapeDtypeStruct((s, n3), BF16), jax.ShapeDtypeStruct((s, 2 * d), F32)),
        in_specs=[pl.BlockSpec((ts, d), lambda i: (i, 0)), _full(modp.shape), _wspec4(w384, j),
                  _full(qkg.shape)],
        out_specs=(pl.BlockSpec((ts, n3), lambda i: (i, 0)), pl.BlockSpec((ts, 2 * d), lambda i: (i, 0))),
        compiler_params=_params(1),
    )(x, modp, w384, qkg)


SB_QUERIES = 512
SB_WIDE = 256
SB_LOOP_BLOCKS = 2


def _sb_logits(q, kw, mask):
    z = _mm_nt(q, kw) * (HEAD ** -0.5)
    e = jnp.exp(-jnp.abs(z))
    lb = jnp.minimum(z, 0.0) - jnp.log(1.0 + e)
    lk = lb - z
    if mask is not None:
        lk = jnp.where(mask, lk, 0.0)
    return lb, lk


def _tri(n, above):
    ri = lax.broadcasted_iota(jnp.int32, (n, n), 0)
    ci = lax.broadcasted_iota(jnp.int32, (n, n), 1)
    return ((ri > ci) if above else (ri < ci)).astype(BF16)


def _split_dot(a, tri):
    hi = _bf(a)
    lo = _bf(a - hi.astype(F32))
    return _mm(hi, tri) + _mm(lo, tri)


def _sb_fwd(qkv, tq):
    s = qkv.shape[0]
    d = qkv.shape[1] // 3
    nq = s // tq
    assert tq % SB_WIDE == 0
    parts = tq // SB_WIDE

    def body(q_ref, k_ref, v_ref, o_ref, t_ref, o_acc, run):
        qi = pl.program_id(1)
        base = qi * tq
        upper = _tri(SB_WIDE, True)
        o_acc[...] = jnp.zeros_like(o_acc)
        run[...] = jnp.zeros_like(run)

        def wide_step(ks, row0, masked, nblk):
            rows = slice(row0, tq)
            width = nblk * SB_WIDE
            mask = None
            if masked:
                qpos = base + row0 + lax.broadcasted_iota(jnp.int32, (tq - row0, width), 0)
                mask = qpos > ks + lax.broadcasted_iota(jnp.int32, (tq - row0, width), 1)
            lb, lk = _sb_logits(q_ref[rows, :], k_ref[pl.ds(ks, width), :], mask)
            blocks = [lk[:, b * SB_WIDE:(b + 1) * SB_WIDE] for b in range(nblk)]
            right = run[rows, :]
            accs = [None] * nblk
            for b in reversed(range(nblk)):
                accs[b] = _split_dot(blocks[b], upper) + right
                right = right + jnp.sum(blocks[b], axis=1, keepdims=True)
            w = jnp.exp(lb + (accs[0] if nblk == 1 else jnp.concatenate(accs, axis=1)))
            if masked:
                w = jnp.where(mask, w, 0.0)
            o_acc[rows, :] += _mm(_bf(w), v_ref[pl.ds(ks, width), :])
            run[rows, :] = right

        for part in reversed(range(parts)):
            wide_step(pl.multiple_of(base + part * SB_WIDE, SB_WIDE), part * SB_WIDE, True, 1)
        loop_width = SB_LOOP_BLOCKS * SB_WIDE
        nsteps = qi * (tq // loop_width)

        def step(it, carry):
            wide_step(pl.multiple_of((nsteps - 1 - it) * loop_width, loop_width), 0, False, SB_LOOP_BLOCKS)
            return carry

        lax.fori_loop(0, nsteps, step, 0)
        o_ref[...] = _bf(o_acc[...])
        t_ref[...] = run[...]

    nh = d // HEAD
    return _pcall(
        body, name="sb_fwd", grid=(nh, nq),
        out_shape=(jax.ShapeDtypeStruct((s, d), BF16), jax.ShapeDtypeStruct((nh, s, 1), F32)),
        in_specs=[pl.BlockSpec((tq, HEAD), lambda h, i: (i, h)),
                  pl.BlockSpec((s, HEAD), lambda h, i: (0, nh + h)),
                  pl.BlockSpec((s, HEAD), lambda h, i: (0, 2 * nh + h))],
        out_specs=(pl.BlockSpec((tq, HEAD), lambda h, i: (i, h)),
                   pl.BlockSpec((None, tq, 1), lambda h, i: (h, i, 0))),
        scratch_shapes=[pltpu.VMEM((tq, HEAD), F32), pltpu.VMEM((tq, 1), F32)],
        compiler_params=_params(2),
    )(qkv, qkv, qkv)


def _sb_bwd(qkv, do, tot, tq):
    s = qkv.shape[0]
    d = qkv.shape[1] // 3
    nq = s // tq
    scale = HEAD ** -0.5
    assert tq % SB_WIDE == 0
    parts = tq // SB_WIDE

    def body(q_ref, k_ref, v_ref, do_ref, t_ref, dq_ref, dk_ref, dv_ref, pk, pd):
        qi = pl.program_id(1)

        @pl.when(qi == 0)
        def _():
            dk_ref[...] = jnp.zeros_like(dk_ref)
            dv_ref[...] = jnp.zeros_like(dv_ref)
        base = qi * tq
        upper = _tri(SB_WIDE, True)
        lower = _tri(SB_WIDE, False)
        dq_ref[...] = jnp.zeros_like(dq_ref)
        pk[...] = jnp.zeros_like(pk)
        pd[...] = jnp.zeros_like(pd)

        def wide_step(ks, row0, masked, nblk):
            rows = slice(row0, tq)
            width = nblk * SB_WIDE
            cut = lambda a: [a[:, b * SB_WIDE:(b + 1) * SB_WIDE] for b in range(nblk)]
            join = lambda parts_: parts_[0] if nblk == 1 else jnp.concatenate(parts_, axis=1)
            mask = None
            if masked:
                qpos = base + row0 + lax.broadcasted_iota(jnp.int32, (tq - row0, width), 0)
                mask = qpos > ks + lax.broadcasted_iota(jnp.int32, (tq - row0, width), 1)
            kw = k_ref[pl.ds(ks, width), :]
            lb, lk = _sb_logits(q_ref[rows, :], kw, mask)
            left = pk[rows, :]
            total = t_ref[rows, :]
            accs = []
            for blk in cut(lk):
                left = left + jnp.sum(blk, axis=1, keepdims=True)
                accs.append(_split_dot(blk, upper) + (total - left))
            pk[rows, :] = left
            w = jnp.exp(lb + join(accs))
            if masked:
                w = jnp.where(mask, w, 0.0)
            de = _mm_nt(do_ref[rows, :], v_ref[pl.ds(ks, width), :]) * w
            before = pd[rows, :]
            dlks = []
            for blk in cut(de):
                dlks.append(_split_dot(blk, lower) + before)
                before = before + jnp.sum(blk, axis=1, keepdims=True)
            pd[rows, :] = before
            dz = (de - jnp.exp(lb) * (de + join(dlks))) * scale
            if masked:
                dz = jnp.where(mask, dz, 0.0)
            dzb = _bf(dz)
            dq_ref[rows, :] += _mm(dzb, kw)
            dv_ref[pl.ds(ks, width), :] += _mm_tn(_bf(w), do_ref[rows, :])
            dk_ref[pl.ds(ks, width), :] += _mm_tn(dzb, q_ref[rows, :])

        loop_width = SB_LOOP_BLOCKS * SB_WIDE

        def step(jb, carry):
            wide_step(pl.multiple_of(jb * loop_width, loop_width), 0, False, SB_LOOP_BLOCKS)
            return carry

        lax.fori_loop(0, qi * (tq // loop_width), step, 0)
        for part in range(parts):
            wide_step(pl.multiple_of(base + part * SB_WIDE, SB_WIDE), part * SB_WIDE, True, 1)

    nh = d // HEAD
    shp = jax.ShapeDtypeStruct((s, d), F32)
    return _pcall(
        body, name="sb_bwd", grid=(nh, nq), out_shape=(shp, shp, shp),
        in_specs=[pl.BlockSpec((tq, HEAD), lambda h, i: (i, h)),
                  pl.BlockSpec((s, HEAD), lambda h, i: (0, nh + h)),
                  pl.BlockSpec((s, HEAD), lambda h, i: (0, 2 * nh + h)),
                  pl.BlockSpec((tq, HEAD), lambda h, i: (i, h)),
                  pl.BlockSpec((None, tq, 1), lambda h, i: (h, i, 0))],
        out_specs=(pl.BlockSpec((tq, HEAD), lambda h, i: (i, h)),
                   pl.BlockSpec((s, HEAD), lambda h, i: (0, h)),
                   pl.BlockSpec((s, HEAD), lambda h, i: (0, h))),
        scratch_shapes=[pltpu.VMEM((tq, 1), F32), pltpu.VMEM((tq, 1), F32)],
        compiler_params=_params(2),
    )(qkv, qkv, qkv, do, tot)


def _odd_out_fwd(o, x, modp, w128, slot, ts):
    s, d = x.shape

    def body(o_ref, x_ref, mp_ref, w_ref, x1_ref, y_ref):
        y = _mm(o_ref[...], w_ref[...].reshape(d, d))
        y_ref[...] = y
        x1_ref[...] = x_ref[...] + mp_ref[2:3, :] * y

    tile = pl.BlockSpec((ts, d), lambda i: (i, 0))
    shp = jax.ShapeDtypeStruct((s, d), F32)
    return _pcall(
        body, name="odd_out_fwd", grid=(s // ts,), out_shape=(shp, shp),
        in_specs=[tile, tile, _full(modp.shape), _wspec4(w128, slot)],
        out_specs=(tile, tile), compiler_params=_params(1),
    )(o, x, modp, w128)


def _odd_out_bwd(dx1, y, modp, w128, slot, ts):
    s, d = dx1.shape

    def body(dx1_ref, y_ref, mp_ref, w_ref, do_ref, dy_ref, sg_ref):
        @pl.when(pl.program_id(0) == 0)
        def _():
            sg_ref[...] = jnp.zeros_like(sg_ref)
        dx1v = dx1_ref[...]
        dyb = _bf(mp_ref[2:3, :] * dx1v)
        dy_ref[...] = dyb
        do_ref[...] = _bf(_mm_nt(dyb, w_ref[...].reshape(d, d)))
        sg_ref[2:3, :] += _sum0(dx1v * y_ref[...])

    tile = pl.BlockSpec((ts, d), lambda i: (i, 0))
    bft = jax.ShapeDtypeStruct((s, d), BF16)
    return _pcall(
        body, name="odd_out_bwd", grid=(s // ts,),
        out_shape=(bft, bft, jax.ShapeDtypeStruct((8, d), F32)),
        in_specs=[tile, tile, _full(modp.shape), _wspec4(w128, slot)],
        out_specs=(tile, tile, _full((8, d))), compiler_params=_params(1),
    )(dx1, y, modp, w128)


def _odd_qkv_bwd(x, dx1, dq, dk, dv, pre, sg_in, modp, w384, j, qkg, ts):
    s, d = x.shape
    n3 = N_DEV * w384.shape[2]

    def body(x_ref, dx1_ref, dq_ref, dk_ref, dv_ref, pre_ref, sgi_ref, mp_ref, w_ref, g_ref,
             dx_ref, dqkv_ref, h_ref, sg_ref):
        @pl.when(pl.program_id(0) == 0)
        def _():
            sg_ref[...] = sgi_ref[...]
        gmix, sc1 = mp_ref[6:7, :], mp_ref[1:2, :]
        n, rstd, h = _normmod_fwd(x_ref[...], gmix, sc1, mp_ref[0:1, :])
        hb = _bf(h)
        h_ref[...] = hb
        w = w_ref[...].reshape(n3, d)
        qkv = pre_ref[...]
        parts_q, parts_k = [], []
        gq, gk = g_ref[0:1, :], g_ref[1:2, :]
        dgq = jnp.zeros((1, HEAD), F32)
        dgk = jnp.zeros((1, HEAD), F32)
        for hh in range(SB_HEADS):
            lo = hh * HEAD
            qn, qr = _rms_fwd(qkv[:, lo:lo + HEAD])
            kn, kr = _rms_fwd(qkv[:, d + lo:d + lo + HEAD])
            dqn = dq_ref[:, lo:lo + HEAD]
            dkn = dk_ref[:, lo:lo + HEAD]
            dgq = dgq + _sum0(dqn * qn)
            dgk = dgk + _sum0(dkn * kn)
            parts_q.append(_rms_bwd(dqn * gq, qn, qr))
            parts_k.append(_rms_bwd(dkn * gk, kn, kr))
        dqkv = _bf(jnp.concatenate(parts_q + parts_k + [dv_ref[...]], axis=1))
        dqkv_ref[...] = dqkv
        dh = _mm(dqkv, w)
        dxn, dsh, dsc, dg = _normmod_bwd(dh, n, rstd, gmix, sc1)
        dx_ref[...] = dx1_ref[...] + dxn
        sg_ref[0:1, :] += dsh
        sg_ref[1:2, :] += dsc
        sg_ref[3:4, :] += dg
        sg_ref[4:5, 0:HEAD] += dgq
        sg_ref[5:6, 0:HEAD] += dgk

    tile = pl.BlockSpec((ts, d), lambda i: (i, 0))
    return _pcall(
        body, name="odd_qkv_bwd", grid=(s // ts,),
        out_shape=(jax.ShapeDtypeStruct((s, d), F32), jax.ShapeDtypeStruct((s, n3), BF16),
                   jax.ShapeDtypeStruct((s, d), BF16), jax.ShapeDtypeStruct((8, d), F32)),
        in_specs=[tile, tile, tile, tile, tile, pl.BlockSpec((ts, 2 * d), lambda i: (i, 0)), _full((8, d)),
                  _full(modp.shape), _wspec4(w384, j), _full(qkg.shape)],
        out_specs=(tile, pl.BlockSpec((ts, n3), lambda i: (i, 0)), tile, _full((8, d))),
        compiler_params=_params(1),
    )(x, dx1, dq, dk, dv, pre, sg_in, modp, w384, qkg)


def _pad_rows(a, rows):
    return jnp.concatenate([a, jnp.zeros((rows - a.shape[0],) + a.shape[1:], a.dtype)], axis=0)


def kernel(x, c, ada_w, ada_b, norm_mix_g, norm_ffn_g, ev_w_in, ev_conv_w, ev_ret_norm_g, ev_w_out, od_w_qkv, od_q_norm_g, od_k_norm_g, od_w_out, ffn_w_gate, ffn_w_up, ffn_w_down, loss_target, m_ada_w, m_ada_b, m_norm_mix_g, m_norm_ffn_g, m_ev_w_in, m_ev_conv_w, m_ev_ret_norm_g, m_ev_w_out, m_od_w_qkv, m_od_q_norm_g, m_od_k_norm_g, m_od_w_out, m_ffn_w_gate, m_ffn_w_up, m_ffn_w_down, v_ada_w, v_ada_b, v_norm_mix_g, v_norm_ffn_g, v_ev_w_in, v_ev_conv_w, v_ev_ret_norm_g, v_ev_w_out, v_od_w_qkv, v_od_q_norm_g, v_od_k_norm_g, v_od_w_out, v_ffn_w_gate, v_ffn_w_up, v_ffn_w_down):
    me = 4 * lax.axis_index("x") + 2 * lax.axis_index("y") + lax.axis_index("c")
    xs = x[0]
    tgt = loss_target[0]
    s, d = xs.shape
    depth = ada_w.shape[0]
    n_even, n_odd = ev_w_in.shape[0], od_w_qkv.shape[0]
    ts = 256
    tq = SB_QUERIES
    cd = 4 * HEAD
    cc = ev_conv_w.shape[2]

    pack0 = jnp.zeros((8, d), F32).at[0].set(c[0]).at[1, :n_even * 3 * cc].set(ev_conv_w.reshape(-1))
    got0, _ = _all_gather(pack0, "gather_cond")
    got0 = got0.reshape(N_DEV, 8, d)
    c_all = got0[:, 0, :]
    conv_all = got0[:, 1, :n_even * 3 * cc].reshape(N_DEV, n_even, 3, cc).transpose(1, 2, 0, 3)
    conv_all = conv_all.reshape(n_even, 3, N_DEV * cc)
    cols = ada_w.shape[2]
    ada_b_cols = lax.dynamic_slice(ada_b, (0, me * cols), (depth, cols))
    mod_cols = _ada_fwd(c_all, ada_w, ada_b_cols)
    got1, cond_done = _all_gather(mod_cols.reshape(depth * N_DEV, cols), "gather_mod")
    got1 = got1.reshape(N_DEV, depth, N_DEV, cols)
    mod = lax.dynamic_index_in_dim(got1, me, axis=2, keepdims=False)
    mod = mod.transpose(1, 0, 2).reshape(depth, 6, d)
    modps = [jnp.concatenate([mod[l], norm_mix_g[l][None], norm_ffn_g[l][None]], axis=0) for l in range(depth)]

    in_flight = []
    started = cond_done
    for l in range(depth):
        j = l // 2
        plain = lambda w: _bf(w + started)
        tr = lambda w: plain(w).T
        blocks = [tr(ev_w_in[j]), plain(ev_w_out[j])] if l % 2 == 0 else [tr(od_w_qkv[j]), plain(od_w_out[j])]
        mixer, started = _exchange_start(blocks, [_landing(b, me, False) for b in blocks], False,
                                         f"gather_start_mixer_{l}")
        blocks = [jnp.concatenate([tr(ffn_w_gate[l]), tr(ffn_w_up[l]), plain(ffn_w_down[l])], axis=0)]
        ffn, started = _exchange_start(blocks, [_landing(b, me, False) for b in blocks], False,
                                       f"gather_start_ffn_{l}")
        in_flight.append((mixer, ffn))
        modps[0] = modps[0] + started
    n_ffn = ffn_w_down.shape[1]

    def mixer_weights(l, after):
        got = _exchange_wait(*in_flight[l][0], False, [after], f"gather_wait_mixer_{l}")
        return got[0].reshape(N_DEV, 1, -1, d), got[1].reshape(N_DEV, 1, -1, d)

    def ffn_weights(l, after):
        got = _exchange_wait(*in_flight[l][1], False, [after], f"gather_wait_ffn_{l}")
        return got[0].reshape(N_DEV, 3, n_ffn, d)

    cos, sin = _rope_tables(s)
    consts = _retention_consts(ts)
    cws = [_pad_rows(jnp.concatenate([conv_all[j], ev_ret_norm_g[j][None]], axis=0), 8) for j in range(n_even)]
    qkgs = [_pad_rows(jnp.stack([od_q_norm_g[j], od_k_norm_g[j]]), 8) for j in range(n_odd)]

    saved = []
    weights = []
    cur = xs
    for l in range(depth):
        j = l // 2
        w_in, w_out = mixer_weights(l, cur)
        if l % 2 == 0:
            x1, y, states, zhalo, proj = _even_fwd(cur, modps[l], w_in, w_out, 0, cws[j], cos, sin, consts, ts)
            mix = (states, zhalo, proj)
        else:
            qkv, pre = _odd_qkv_fwd(cur, modps[l], w_in, 0, qkgs[j], ts)
            o, tot = _sb_fwd(qkv, tq)
            x1, y = _odd_out_fwd(o, cur, modps[l], w_out, 0, ts)
            mix = (qkv, o, tot, pre)
        w_ffn = ffn_weights(l, x1)
        weights.append((w_in, w_out, w_ffn))
        x2, f, ab = _ffn_fwd(x1, modps[l], w_ffn, 0, ts)
        saved.append((cur, x1, y, (f, ab), mix))
        cur = x2

    dx, loss_part = _loss_grad(cur, tgt, ts)
    loss = lax.psum(loss_part[0, 0], ("x", "y", "c"))

    dmod = [None] * depth
    d_gmix = [None] * depth
    d_gffn = [None] * depth
    d_conv = [None] * n_even
    d_retg = [None] * n_even
    d_qg = [None] * n_odd
    d_kg = [None] * n_odd
    grads_in_flight = [None] * depth
    for l in reversed(range(depth)):
        j = l // 2
        x0, x1, y, (f, ab), mix = saved[l]
        w_in, w_out, w_ffn = weights[l]
        g_ffn = lax.empty(w_ffn.shape, BF16)
        g_in = lax.empty(w_in.shape, BF16)
        g_out = lax.empty(w_out.shape, BF16)
        dx1, dab, h2, sv, df, sg2 = _ffn_bwd(x1, f, ab, dx, modps[l], w_ffn, 0, ts)
        g_ffn = _tn_matmul(dab, 0, h2, g_ffn, 0, "tn_gate")
        g_ffn = _tn_matmul(dab, 1, h2, g_ffn, 1, "tn_up")
        g_ffn = _tn_matmul(sv, 0, df, g_ffn, 2, "tn_down")
        if l == 0:
            pieces = [g_ffn.reshape(N_DEV, -1, d)]
            last_ffn_flight, started = _exchange_start(pieces, [_landing(p, me, True) for p in pieces], True,
                                                       "grads_start_ffn_0")
            modps[0] = modps[0] + started
        if l % 2 == 0:
            states, zhalo, proj = mix
            dx, dproj, hb, mb, dyb, sg1 = _even_bwd(x0, dx1, y, states, zhalo, proj, modps[l], w_in, w_out, 0,
                                                    cws[j], cos, sin, consts, ts)
            g_in = _tn_matmul(dproj, 0, hb, g_in, 0, "tn_ev_in")
            g_out = _tn_matmul(mb, 0, dyb, g_out, 0, "tn_ev_out")
            d_conv[j] = sg1[4:7, :cd]
            d_retg[j] = sg1[7, :cd]
        else:
            qkv, o, tot, pre = mix
            do, dyb, sg0 = _odd_out_bwd(dx1, y, modps[l], w_out, 0, ts)
            dq, dk, dv = _sb_bwd(qkv, do, tot, tq)
            dx, dqkv, hb, sg1 = _odd_qkv_bwd(x0, dx1, dq, dk, dv, pre, sg0, modps[l], w_in, 0, qkgs[j], ts)
            g_in = _tn_matmul(dqkv, 0, hb, g_in, 0, "tn_od_qkv")
            g_out = _tn_matmul(o, 0, dyb, g_out, 0, "tn_od_out")
            d_qg[j] = sg1[4, :HEAD]
            d_kg[j] = sg1[5, :HEAD]
        pieces = [g.reshape(N_DEV, -1, d) for g in ((g_ffn, g_in, g_out) if l > 0 else (g_in, g_out))]
        if l > 0:
            grads_in_flight[l], started = _exchange_start(pieces, [_landing(p, me, True) for p in pieces], True,
                                                          f"grads_start_{l}")
            modps[l - 1] = modps[l - 1] + started
        dmod[l] = jnp.concatenate([sg1[0:3], sg2[0:3]], axis=0).reshape(-1)
        d_gmix[l] = sg1[3]
        d_gffn[l] = sg2[3]

    small = jnp.concatenate(
        [jnp.stack(dmod).reshape(-1), jnp.stack(d_gmix).reshape(-1), jnp.stack(d_gffn).reshape(-1),
         jnp.stack(d_retg).reshape(-1), jnp.stack(d_qg).reshape(-1), jnp.stack(d_kg).reshape(-1),
         jnp.stack(d_conv).reshape(-1)])
    n_small = small.shape[0]
    rows_small = -(-n_small // (8 * 128)) * 8
    small = jnp.concatenate([small, jnp.zeros((rows_small * 128 - n_small,), F32)]).reshape(rows_small, 128)
    small_flight, started = _exchange_start([small], [_landing(small, me, False)], False, "small_start")
    grads_in_flight[0], started = _exchange_start(pieces, [_landing(p, me, True, started) for p in pieces],
                                                  True, "grads_start_0")

    res = {}
    big = {"ev_w_in": (ev_w_in, m_ev_w_in, v_ev_w_in), "ev_w_out": (ev_w_out, m_ev_w_out, v_ev_w_out),
           "od_w_qkv": (od_w_qkv, m_od_w_qkv, v_od_w_qkv), "od_w_out": (od_w_out, m_od_w_out, v_od_w_out),
           "ffn_w_gate": (ffn_w_gate, m_ffn_w_gate, v_ffn_w_gate), "ffn_w_up": (ffn_w_up, m_ffn_w_up, v_ffn_w_up),
           "ffn_w_down": (ffn_w_down, m_ffn_w_down, v_ffn_w_down)}
    flipped = ("ev_w_in", "ffn_w_gate", "ffn_w_up")
    flip = lambda a: a.transpose(0, 2, 1)
    for name in flipped:
        big[name] = tuple(flip(a) for a in big[name])
    for name, (w, _, _) in big.items():
        res[name] = tuple(lax.empty(w.shape, F32) for _ in range(4))

    def update(name, idx, g_layer):
        w, m, v = big[name]
        res[name] = _adamw_layer(w, g_layer, m, v, res[name], idx, "adamw_" + name)

    def ffn_updates(l, s_ffn):
        update("ffn_w_gate", l, s_ffn[0:n_ffn])
        update("ffn_w_up", l, s_ffn[n_ffn:2 * n_ffn])
        update("ffn_w_down", l, s_ffn[2 * n_ffn:3 * n_ffn])

    def mixer_updates(l, s_in, s_out):
        if l % 2 == 0:
            update("ev_w_in", l // 2, s_in)
        else:
            update("od_w_qkv", l // 2, s_in.T)
        update("ev_w_out" if l % 2 == 0 else "od_w_out", l // 2, s_out)

    after = [dx + started]
    for l in reversed(range(1, depth)):
        recv = _exchange_wait(*grads_in_flight[l], True, after, f"grads_wait_{l}")
        s_ffn, s_in, s_out = [_sum_slots(r, f"sum_{i}") for i, r in enumerate(recv)]
        ffn_updates(l, s_ffn)
        mixer_updates(l, s_in, s_out)
        after = [res[name][1] for name in big]

    got2 = _exchange_wait(*small_flight, False, after, "small_wait")[0]
    tot_small = _sum_small(got2).reshape(-1)
    n_mod = depth * 6 * d
    dmod_all = got2.reshape(N_DEV, -1)[:, :n_mod].reshape(N_DEV, depth, 6 * d)
    dmod_cols = lax.dynamic_slice(dmod_all, (0, 0, me * cols), (N_DEV, depth, cols)).transpose(1, 0, 2)
    g_ada_w = _ada_bwd(c_all.T, dmod_cols)
    res["ada_w"] = (g_ada_w,) + _adamw_nd(ada_w, g_ada_w, m_ada_w, v_ada_w, "adamw_ada_w")

    off = [0]

    def take(shape):
        n = int(np.prod(shape))
        out = tot_small[off[0]:off[0] + n].reshape(shape)
        off[0] += n
        return out

    g_ada_b = take((depth, 6 * d))
    g_norm_mix = take((depth, d))
    g_norm_ffn = take((depth, d))
    g_ret_norm = take((n_even, cd))
    g_q_norm = take((n_odd, HEAD))
    g_k_norm = take((n_odd, HEAD))
    g_conv_full = take((n_even, 3, cd))
    g_conv = lax.dynamic_slice(g_conv_full, (0, 0, me * cc), (n_even, 3, cc))

    recv = _exchange_wait(*last_ffn_flight, True, [res["ada_w"][1]], "grads_wait_ffn_0")
    ffn_updates(0, _sum_slots(recv[0], "sum_0"))
    after = [res[name][1] for name in ("ffn_w_gate", "ffn_w_up", "ffn_w_down")]
    recv = _exchange_wait(*grads_in_flight[0], True, after, "grads_wait_0")
    mixer_updates(0, *[_sum_slots(r, f"sum_{1 + i}") for i, r in enumerate(recv)])
    for name in flipped:
        res[name] = tuple(flip(a) for a in res[name])

    smalls = [("ada_b", ada_b, g_ada_b, m_ada_b, v_ada_b), ("norm_mix_g", norm_mix_g, g_norm_mix, m_norm_mix_g, v_norm_mix_g),
              ("norm_ffn_g", norm_ffn_g, g_norm_ffn, m_norm_ffn_g, v_norm_ffn_g),
              ("ev_conv_w", ev_conv_w, g_conv, m_ev_conv_w, v_ev_conv_w),
              ("ev_ret_norm_g", ev_ret_norm_g, g_ret_norm, m_ev_ret_norm_g, v_ev_ret_norm_g),
              ("od_q_norm_g", od_q_norm_g, g_q_norm, m_od_q_norm_g, v_od_q_norm_g),
              ("od_k_norm_g", od_k_norm_g, g_k_norm, m_od_k_norm_g, v_od_k_norm_g)]

    def pack(arrs):
        flat = jnp.concatenate([a.reshape(-1) for a in arrs])
        rows = -(-flat.shape[0] // (8 * 128)) * 8
        return jnp.concatenate([flat, jnp.zeros((rows * 128 - flat.shape[0],), F32)]).reshape(rows, 128)

    sd, sm, sv_ = _adamw(pack([t[1] for t in smalls]), pack([t[2] for t in smalls]),
                         pack([t[3] for t in smalls]), pack([t[4] for t in smalls]), "adamw_small")
    sd, sm, sv_ = sd.reshape(-1), sm.reshape(-1), sv_.reshape(-1)
    pos = 0
    for name, w, g, m, v in smalls:
        n = int(np.prod(w.shape))
        res[name] = (g, sd[pos:pos + n].reshape(w.shape), sm[pos:pos + n].reshape(w.shape),
                     sv_[pos:pos + n].reshape(w.shape))
        pos += n

    order = ["ada_w", "ada_b", "norm_mix_g", "norm_ffn_g", "ev_w_in", "ev_conv_w", "ev_ret_norm_g", "ev_w_out",
             "od_w_qkv", "od_q_norm_g", "od_k_norm_g", "od_w_out", "ffn_w_gate", "ffn_w_up", "ffn_w_down"]
    outs = [loss, dx[None]]
    for k in range(4):
        outs += [res[name][k] for name in order]
    return tuple(outs)
```

```python
import functools
import math

import numpy as np
import jax
import jax.numpy as jnp
from jax import lax
from jax.experimental import pallas as pl
from jax.experimental.pallas import tpu as pltpu

F32 = jnp.float32
BF16 = jnp.bfloat16
MESH = pl.DeviceIdType.MESH

N_DEV = 8
EPS = 1e-6
CHUNK = 64
HEAD = 128
RET_HEADS = 4
SB_HEADS = 8
ROPE_THETA = 10000.0
KEY_BLOCK = 128
ADAM_LR, ADAM_B1, ADAM_B2, ADAM_EPS, ADAM_WD, ADAM_STEP = 0.001, 0.9, 0.999, 1e-08, 0.01, 10
VMEM_LIMIT = 56 * 1024 * 1024


def _pcall(body, **kw):
    return pl.pallas_call(body, **kw)


def _params(n_grid=1, vmem=VMEM_LIMIT):
    return pltpu.CompilerParams(dimension_semantics=("arbitrary",) * n_grid, vmem_limit_bytes=vmem)


def _mm(a, b):
    return jnp.dot(a, b, preferred_element_type=F32)


def _mm_nt(a, b):
    return lax.dot_general(a, b, (((1,), (1,)), ((), ())), preferred_element_type=F32)


def _mm_tn(a, b):
    return lax.dot_general(a, b, (((0,), (0,)), ((), ())), preferred_element_type=F32)


def _bf(a):
    return a.astype(BF16)


def _sigmoid(a):
    return 1.0 / (1.0 + jnp.exp(-a))


def _sum0(a):
    return jnp.sum(a, axis=0, keepdims=True)


def _full(shape):
    nd = len(shape)
    return pl.BlockSpec(shape, lambda *_: (0,) * nd)


def _normmod_fwd(x, g, sc, sh):
    rstd = lax.rsqrt(jnp.mean(x * x, axis=-1, keepdims=True) + EPS)
    n = x * rstd
    return n, rstd, (n * g) * (1.0 + sc) + sh


def _normmod_bwd(dh, n, rstd, g, sc):
    dsh = _sum0(dh)
    dsc = _sum0(dh * (n * g))
    dg = _sum0(dh * n * (1.0 + sc))
    dn = dh * (g * (1.0 + sc))
    dx = rstd * (dn - n * jnp.mean(dn * n, axis=-1, keepdims=True))
    return dx, dsh, dsc, dg


def _rms_fwd(o):
    rstd = lax.rsqrt(jnp.mean(o * o, axis=-1, keepdims=True) + EPS)
    return o * rstd, rstd


def _rms_bwd(dn, n, rstd):
    return rstd * (dn - n * jnp.mean(dn * n, axis=-1, keepdims=True))


def _all_gather(x2d, name):
    m_per, n = x2d.shape
    space = pltpu.VMEM

    def body(x_ref, out_ref, done_ref, send_sems, recv_sems, local_sem):
        x, y, c = lax.axis_index("x"), lax.axis_index("y"), lax.axis_index("c")
        me, sibling = (x, y, c), (x, y, 1 - c)
        chips = [(1 - x, y), (x, 1 - y), (1 - x, 1 - y)]

        def rows(px, py, pc):
            return out_ref.at[pl.ds((4 * px + 2 * py + pc) * m_per, m_per), :]

        def copy(k, block, to, src=None):
            return pltpu.make_async_remote_copy(
                src_ref=rows(*block) if src is None else src, dst_ref=rows(*block),
                send_sem=send_sems.at[k], recv_sem=recv_sems.at[k],
                device_id=to, device_id_type=MESH)

        mine = pltpu.make_async_copy(x_ref, rows(*me), local_sem)
        mine.start()
        first = [copy(1 + j, me, (*chip, c), src=x_ref) for j, chip in enumerate(chips)]
        first += [copy(0, me, sibling, src=x_ref)]
        for cp in first:
            cp.start()
        passed = [copy(4 + j, (*chip, c), sibling) for j, chip in enumerate(chips)]
        for j, chip in enumerate(chips):
            copy(1 + j, (*chip, c), me).wait_recv()
            passed[j].start()
        copy(0, sibling, me).wait_recv()
        for j, chip in enumerate(chips):
            copy(4 + j, (*chip, 1 - c), me).wait_recv()
        for cp in first + passed:
            cp.wait_send()
        mine.wait()
        done_ref[...] = jnp.zeros_like(done_ref)

    out, done = _pcall(
        body, name=name,
        out_shape=(jax.ShapeDtypeStruct((N_DEV * m_per, n), x2d.dtype), jax.ShapeDtypeStruct((8, 128), F32)),
        in_specs=[pl.BlockSpec(memory_space=space)],
        out_specs=(pl.BlockSpec(memory_space=space), pl.BlockSpec(memory_space=pltpu.VMEM)),
        scratch_shapes=[pltpu.SemaphoreType.DMA((7,)), pltpu.SemaphoreType.DMA((7,)),
                        pltpu.SemaphoreType.DMA],
    )(x2d)
    return out, done[0, 0]


_HBM = pl.BlockSpec(memory_space=pltpu.HBM)
_SEM = pl.BlockSpec(memory_space=pltpu.SEMAPHORE)
_EFFECT = pltpu.SideEffectType.DATAFLOW_SIDE_EFFECTING


def _exchange_copies(src_refs, land_refs, send_sems, recv_sems, scatter):
    x, y, c = lax.axis_index("x"), lax.axis_index("y"), lax.axis_index("c")
    me = 4 * x + 2 * y + c
    out = []
    for i, (s_ref, l_ref) in enumerate(zip(src_refs, land_refs)):
        for k in (2, 4, 6, 3, 5, 7, 1):
            px = (1 - x) if (k >> 2) & 1 else x
            py = (1 - y) if (k >> 1) & 1 else y
            pc = (1 - c) if k & 1 else c
            out.append(pltpu.make_async_remote_copy(
                src_ref=s_ref.at[4 * px + 2 * py + pc] if scatter else s_ref, dst_ref=l_ref.at[me],
                send_sem=send_sems.at[7 * i + k - 1], recv_sem=recv_sems.at[7 * i + k - 1],
                device_id=(px, py, pc), device_id_type=MESH))
    return out


def _exchange_start(srcs, lands, scatter, name):
    n = len(srcs)

    def body(*refs):
        for cp in _exchange_copies(refs[:n], refs[n:2 * n], refs[2 * n], refs[2 * n + 1], scatter):
            cp.start()
        refs[-1][...] = jnp.zeros_like(refs[-1])

    arrays = list(srcs) + list(lands)
    outs = _pcall(
        body, name=name,
        out_shape=(pltpu.SemaphoreType.DMA((7 * n,)), pltpu.SemaphoreType.DMA((7 * n,)),
                   *[pltpu.HBM(a.shape, a.dtype) for a in arrays], jax.ShapeDtypeStruct((8, 128), F32)),
        in_specs=[_HBM] * (2 * n),
        out_specs=(_SEM, _SEM, *[_HBM] * (2 * n), pl.BlockSpec(memory_space=pltpu.VMEM)),
        input_output_aliases={i: 2 + i for i in range(2 * n)},
        compiler_params=pltpu.CompilerParams(has_side_effects=_EFFECT),
    )(*[pltpu.with_memory_space_constraint(a, pltpu.HBM) for a in arrays])
    return (outs[0], outs[1], list(outs[2:2 + n]), list(outs[2 + n:2 + 2 * n])), outs[-1][0, 0]


def _exchange_wait(send_sems, recv_sems, srcs, lands, scatter, after, name):
    n = len(srcs)
    after = list(after)

    def body(*refs):
        for cp in _exchange_copies(refs[:n], refs[n:2 * n], refs[2 * n], refs[2 * n + 1], scatter):
            cp.wait_send()
            cp.wait_recv()

    arrays = list(srcs) + list(lands)
    outs = _pcall(
        body, name=name,
        out_shape=tuple(pltpu.HBM(a.shape, a.dtype) for a in arrays),
        in_specs=[_HBM] * (2 * n) + [_SEM, _SEM] + [pl.BlockSpec(memory_space=pl.ANY)] * len(after),
        out_specs=tuple([_HBM] * (2 * n)),
        input_output_aliases={i: i for i in range(2 * n)},
        compiler_params=pltpu.CompilerParams(has_side_effects=_EFFECT),
    )(*arrays, send_sems, recv_sems, *after)
    return list(outs[n:])


def _landing(src, me, scatter, after=None):
    own = lax.dynamic_index_in_dim(src, me, 0, keepdims=True) if scatter else src[None]
    if after is not None:
        own = own + after.astype(own.dtype)
    shape = src.shape if scatter else (N_DEV,) + src.shape
    return lax.dynamic_update_slice(lax.empty(shape, src.dtype), own, (me, 0, 0))


def _sum_slots(recv, name):
    _, r, n = recv.shape
    tr = r
    for cand in (512, 448, 384, 352, 256, 128, 64, 32, 16, 8):
        if r % cand == 0:
            tr = cand
            break

    def body(r_ref, o_ref):
        acc = r_ref[0].astype(F32)
        for p in range(1, N_DEV):
            acc = acc + r_ref[p].astype(F32)
        o_ref[...] = acc

    return _pcall(
        body, name=name, grid=(r // tr,),
        out_shape=jax.ShapeDtypeStruct((r, n), F32),
        in_specs=[pl.BlockSpec((N_DEV, tr, n), lambda i: (0, i, 0))],
        out_specs=pl.BlockSpec((tr, n), lambda i: (i, 0)),
        compiler_params=_params(1),
    )(recv)


def _row_tile(rows, limit=512):
    for cand in range(min(limit, rows) // 8 * 8, 7, -8):
        if rows % cand == 0:
            return cand
    return rows


def _adamw(w, g, m, v, name):
    r, n = w.shape
    tr = _row_tile(r)

    def body(w_ref, g_ref, m_ref, v_ref, d_ref, nm_ref, nv_ref):
        d_ref[...], nm_ref[...], nv_ref[...] = _adam_update(w_ref[...], g_ref[...], m_ref[...], v_ref[...])

    spec = pl.BlockSpec((tr, n), lambda i: (i, 0))
    shp = jax.ShapeDtypeStruct((r, n), F32)
    return _pcall(
        body, name=name, grid=(r // tr,), out_shape=(shp, shp, shp),
        in_specs=[spec] * 4, out_specs=(spec, spec, spec), compiler_params=_params(1),
    )(w, g, m, v)


def _adam_update(wv, gv, mv, vv):
    bc1 = 1.0 / (1.0 - ADAM_B1 ** ADAM_STEP)
    bc2 = 1.0 / (1.0 - ADAM_B2 ** ADAM_STEP)
    nm = ADAM_B1 * mv + (1.0 - ADAM_B1) * gv
    nv = ADAM_B2 * vv + (1.0 - ADAM_B2) * (gv * gv)
    return -ADAM_LR * ((nm * bc1) / (jnp.sqrt(nv * bc2) + ADAM_EPS) + ADAM_WD * wv), nm, nv


def _adamw_layer(w, g_layer, m, v, outs, idx, name, part=0):
    _, a, b = w.shape
    tr = _row_tile(a)
    slots = g_layer.ndim == 3

    def body(w_ref, g_ref, m_ref, v_ref, o0, o1, o2, o3, go_ref, d_ref, nm_ref, nv_ref):
        if slots:
            gv = g_ref[0].astype(F32)
            for p in range(1, N_DEV):
                gv = gv + g_ref[p].astype(F32)
        else:
            gv = g_ref[...]
        go_ref[...] = gv
        d_ref[...], nm_ref[...], nv_ref[...] = _adam_update(w_ref[...], gv, m_ref[...], v_ref[...])

    layer = pl.BlockSpec((None, tr, b), lambda i: (idx, i, 0))
    steps = a // tr
    g_spec = (pl.BlockSpec((N_DEV, tr, b), lambda i: (0, part * steps + i, 0)) if slots
              else pl.BlockSpec((tr, b), lambda i: (i, 0)))
    anyw = pl.BlockSpec(memory_space=pl.ANY)
    shp = jax.ShapeDtypeStruct(w.shape, F32)
    return tuple(_pcall(
        body, name=name, grid=(steps,), out_shape=(shp,) * 4,
        in_specs=[layer, g_spec, layer, layer, anyw, anyw, anyw, anyw],
        out_specs=(layer,) * 4, input_output_aliases={4: 0, 5: 1, 6: 2, 7: 3},
        compiler_params=_params(1),
    )(w, g_layer, m, v, *outs))


def _adamw_nd(w, g, m, v, name):
    shp = w.shape
    f = lambda a: a.reshape(-1, shp[-1])
    d, nm, nv = _adamw(f(w), f(g), f(m), f(v), name)
    return d.reshape(shp), nm.reshape(shp), nv.reshape(shp)


def _ada_fwd(c_all, ada_w, ada_b_cols):
    n_l, d, cols = ada_w.shape

    def body(c_ref, w_ref, b_ref, o_ref):
        cv = c_ref[...]
        ca = cv * _sigmoid(cv)
        o_ref[...] = _mm(_bf(ca), _bf(w_ref[...])) + b_ref[...]

    return _pcall(
        body, name="ada_fwd", grid=(n_l,),
        out_shape=jax.ShapeDtypeStruct((n_l, N_DEV, cols), F32),
        in_specs=[_full((N_DEV, d)), pl.BlockSpec((None, d, cols), lambda l: (l, 0, 0)),
                  pl.BlockSpec((None, 1, cols), lambda l: (l, 0, 0))],
        out_specs=pl.BlockSpec((None, N_DEV, cols), lambda l: (l, 0, 0)),
        compiler_params=_params(1),
    )(c_all, ada_w, ada_b_cols.reshape(n_l, 1, cols))


def _ada_bwd(c_all_t, dmod_cols):
    d = c_all_t.shape[0]
    n_l, _, cols = dmod_cols.shape

    def body(ct_ref, dm_ref, o_ref):
        cv = ct_ref[...]
        ca = cv * _sigmoid(cv)
        dm = dm_ref[...]
        acc = ca[:, 0:1] * dm[0:1, :]
        for b in range(1, N_DEV):
            acc = acc + ca[:, b:b + 1] * dm[b:b + 1, :]
        o_ref[...] = acc

    return _pcall(
        body, name="ada_bwd", grid=(n_l,),
        out_shape=jax.ShapeDtypeStruct((n_l, d, cols), F32),
        in_specs=[_full((d, N_DEV)), pl.BlockSpec((None, N_DEV, cols), lambda l: (l, 0, 0))],
        out_specs=pl.BlockSpec((None, d, cols), lambda l: (l, 0, 0)),
        compiler_params=_params(1),
    )(c_all_t, dmod_cols)


def _sum_small(gathered):
    _, r, n = gathered.shape

    def body(g_ref, o_ref):
        acc = g_ref[0]
        for p in range(1, N_DEV):
            acc = acc + g_ref[p]
        o_ref[...] = acc

    return _pcall(
        body, name="sum_small", out_shape=jax.ShapeDtypeStruct((r, n), F32),
        in_specs=[_full((N_DEV, r, n))], out_specs=_full((r, n)),
    )(gathered)


def _loss_grad(xf, tgt, ts):
    s, d = xf.shape

    def body(x_ref, t_ref, dx_ref, l_ref):
        @pl.when(pl.program_id(0) == 0)
        def _():
            l_ref[...] = jnp.zeros_like(l_ref)
        e = x_ref[...] - t_ref[...]
        dx_ref[...] = e * (1.0 / d)
        l_ref[...] += (0.5 / d) * jnp.sum(jnp.sum(e * e, axis=1, keepdims=True), axis=0, keepdims=True)

    spec = pl.BlockSpec((ts, d), lambda i: (i, 0))
    return _pcall(
        body, name="loss_grad", grid=(s // ts,),
        out_shape=(jax.ShapeDtypeStruct((s, d), F32), jax.ShapeDtypeStruct((1, 1), F32)),
        in_specs=[spec, spec], out_specs=(spec, _full((1, 1))), compiler_params=_params(1),
    )(xf, tgt)


def _tn_matmul(a, col_block, b, buf, slot, name):
    s = a.shape[0]
    k = b.shape[1]
    n_p = buf.shape[2]
    mcols = N_DEV * n_p
    ts = _row_tile(s, 1024)
    nt = s // ts

    def body(a_ref, b_ref, buf_ref, o_ref, acc):
        i = pl.program_id(0)

        @pl.when(i == 0)
        def _():
            acc[...] = jnp.zeros_like(acc)
        acc[...] += _mm_tn(a_ref[...], b_ref[...])

        @pl.when(i == nt - 1)
        def _():
            o_ref[...] = acc[...].reshape(N_DEV, n_p, k).astype(BF16)

    return _pcall(
        body, name=name, grid=(nt,),
        out_shape=jax.ShapeDtypeStruct(buf.shape, BF16),
        in_specs=[pl.BlockSpec((ts, mcols), lambda i: (i, col_block)),
                  pl.BlockSpec((ts, k), lambda i: (i, 0)),
                  pl.BlockSpec(memory_space=pl.ANY)],
        out_specs=pl.BlockSpec((N_DEV, None, n_p, k), lambda i: (0, slot, 0, 0)),
        scratch_shapes=[pltpu.VMEM((mcols, k), F32)],
        input_output_aliases={2: 0},
        compiler_params=_params(1),
    )(a, b, buf)


def _wspec4(w, slot):
    _, _, n_p, k = w.shape
    return pl.BlockSpec((N_DEV, None, n_p, k), lambda i: (0, slot, 0, 0), pipeline_mode=pl.Buffered(1))


def _ffn_fwd(x1, modp, w352, l, ts):
    s, d = x1.shape
    n_l = w352.shape[1] // 3
    f_dim = N_DEV * w352.shape[2]

    def body(x_ref, mp_ref, wg_ref, wu_ref, wd_ref, x2_ref, f_ref, ab_ref):
        x = x_ref[...]
        _, _, h2 = _normmod_fwd(x, mp_ref[7:8, :], mp_ref[4:5, :], mp_ref[3:4, :])
        hb = _bf(h2)
        f = jnp.zeros((ts, d), F32)
        half_dev, fc = N_DEV // 2, f_dim // 2
        for part in range(2):
            dev0, c0 = part * half_dev, part * fc
            a = _mm_nt(hb, wg_ref[dev0:dev0 + half_dev].reshape(fc, d))
            b = _mm_nt(hb, wu_ref[dev0:dev0 + half_dev].reshape(fc, d))
            ab_ref[:, c0:c0 + fc] = a
            ab_ref[:, f_dim + c0:f_dim + c0 + fc] = b
            sv = (a * _sigmoid(a)) * b
            f = f + _mm(_bf(sv), wd_ref[dev0:dev0 + half_dev].reshape(fc, d))
        f_ref[...] = f
        x2_ref[...] = x + mp_ref[5:6, :] * f

    tile = pl.BlockSpec((ts, d), lambda i: (i, 0))
    shp = jax.ShapeDtypeStruct((s, d), F32)
    return _pcall(
        body, name="ffn_fwd", grid=(s // ts,),
        out_shape=(shp, shp, jax.ShapeDtypeStruct((s, 2 * f_dim), F32)),
        in_specs=[tile, _full(modp.shape), _wspec4(w352, l), _wspec4(w352, n_l + l),
                  _wspec4(w352, 2 * n_l + l)],
        out_specs=(tile, tile, pl.BlockSpec((ts, 2 * f_dim), lambda i: (i, 0))), compiler_params=_params(1),
    )(x1, modp, w352, w352, w352)


def _ffn_bwd(x1, f, ab, dx2, modp, w352, l, ts):
    s, d = x1.shape
    n_l = w352.shape[1] // 3
    f_dim = N_DEV * w352.shape[2]

    def body(x_ref, f_ref, ab_ref, dx2_ref, mp_ref, wg_ref, wu_ref, wd_ref,
             dx1_ref, dab_ref, h2_ref, s_ref, df_ref, sg_ref):
        @pl.when(pl.program_id(0) == 0)
        def _():
            sg_ref[...] = jnp.zeros_like(sg_ref)
        x = x_ref[...]
        gffn, sc2, g2 = mp_ref[7:8, :], mp_ref[4:5, :], mp_ref[5:6, :]
        n, rstd, h2 = _normmod_fwd(x, gffn, sc2, mp_ref[3:4, :])
        hb = _bf(h2)
        dx2 = dx2_ref[...]
        dfb = _bf(g2 * dx2)
        dh2 = jnp.zeros((ts, d), F32)
        half_dev, fc = N_DEV // 2, f_dim // 2
        for part in range(2):
            dev0, c0 = part * half_dev, part * fc
            wg = wg_ref[dev0:dev0 + half_dev].reshape(fc, d)
            wu = wu_ref[dev0:dev0 + half_dev].reshape(fc, d)
            a = ab_ref[:, c0:c0 + fc]
            b = ab_ref[:, f_dim + c0:f_dim + c0 + fc]
            sig = _sigmoid(a)
            sa = a * sig
            s_ref[:, c0:c0 + fc] = _bf(sa * b)
            ds = _mm_nt(dfb, wd_ref[dev0:dev0 + half_dev].reshape(fc, d))
            dab = _bf(ds * b * (sig * (1.0 + a * (1.0 - sig))))
            dbb = _bf(ds * sa)
            dab_ref[:, c0:c0 + fc] = dab
            dab_ref[:, f_dim + c0:f_dim + c0 + fc] = dbb
            dh2 = dh2 + _mm(dab, wg) + _mm(dbb, wu)
        dxn, dsh, dsc, dg = _normmod_bwd(dh2, n, rstd, gffn, sc2)
        dx1_ref[...] = dx2 + dxn
        h2_ref[...] = hb
        df_ref[...] = dfb
        sg_ref[0:1, :] += dsh
        sg_ref[1:2, :] += dsc
        sg_ref[2:3, :] += _sum0(dx2 * f_ref[...])
        sg_ref[3:4, :] += dg

    tile = pl.BlockSpec((ts, d), lambda i: (i, 0))
    f32t = jax.ShapeDtypeStruct((s, d), F32)
    bft = jax.ShapeDtypeStruct((s, d), BF16)
    return _pcall(
        body, name="ffn_bwd", grid=(s // ts,),
        out_shape=(f32t, jax.ShapeDtypeStruct((s, 2 * f_dim), BF16), bft,
                   jax.ShapeDtypeStruct((s, f_dim), BF16), bft, jax.ShapeDtypeStruct((8, d), F32)),
        in_specs=[tile, tile, pl.BlockSpec((ts, 2 * f_dim), lambda i: (i, 0)), tile, _full(modp.shape),
                  _wspec4(w352, l), _wspec4(w352, n_l + l), _wspec4(w352, 2 * n_l + l)],
        out_specs=(tile, pl.BlockSpec((ts, 2 * f_dim), lambda i: (i, 0)), tile,
                   pl.BlockSpec((ts, f_dim), lambda i: (i, 0)), tile, _full((8, d))),
        compiler_params=_params(1),
    )(x1, f, ab, dx2, modp, w352, w352, w352)


def _retention_consts(ts):
    h = np.arange(RET_HEADS, dtype=np.float64)
    log_g = np.log1p(-np.exp2(-5.0 - h))
    t = np.arange(ts)
    diff = t[:, None] - t[None, :]
    same = (t[:, None] // CHUNK) == (t[None, :] // CHUNK)
    later = (t[:, None] // CHUNK) > (t[None, :] // CHUNK)
    dm = np.where(same, np.abs(diff), np.where(later, diff, 0))[None] * log_g[:, None, None]
    dm = np.where((same | later)[None], np.exp(dm), 0.0)
    qd = np.exp((t[:, None] + 1.0) * log_g[None, :])
    kd = np.exp((ts - 1.0 - t[:, None]) * log_g[None, :])
    qd = np.repeat(qd, HEAD, axis=1)
    kd = np.repeat(kd, HEAD, axis=1)
    tdec = [float(np.exp(ts * lg)) for lg in log_g]
    return (jnp.asarray(dm, F32), jnp.asarray(qd, F32), jnp.asarray(kd, F32), tdec)


def _rope_tables(s):
    inv_freq = 1.0 / (ROPE_THETA ** (jnp.arange(0, HEAD, 2, dtype=F32) / HEAD))
    ang = jnp.arange(s, dtype=F32)[:, None] * inv_freq[None, :]
    cos, sin = jnp.cos(ang), jnp.sin(ang)
    return jnp.concatenate([cos, cos], axis=1), jnp.concatenate([-sin, sin], axis=1)


def _rope(v, cos, sin):
    return v * cos + pltpu.roll(v, HEAD // 2, 1) * sin


def _rope_t(dv, cos, sin):
    return dv * cos + pltpu.roll(dv * sin, HEAD // 2, 1)


def _shift_down(z, k, halo_ref):
    r = pltpu.roll(z, k, 0)
    rows = lax.broadcasted_iota(jnp.int32, z.shape, 0)
    for j in range(k):
        r = jnp.where(rows == j, halo_ref[8 - k + j:8 - k + j + 1, :], r)
    return r


def _shift_up(z, k, halo_ref):
    n = z.shape[0]
    r = pltpu.roll(z, n - k, 0)
    rows = lax.broadcasted_iota(jnp.int32, z.shape, 0)
    for j in range(k):
        r = jnp.where(rows == n - k + j, halo_ref[j:j + 1, :], r)
    return r


def _even_recompute(x, mp_ref, win, cw_ref, cos, sin, dm_ref, qd_ref, kd_ref, halo_ref, state_of, proj=None):
    cd = 4 * HEAD
    n, rstd, h = _normmod_fwd(x, mp_ref[6:7, :], mp_ref[1:2, :], mp_ref[0:1, :])
    if proj is None:
        proj = _mm_nt(_bf(h), win)
    bg, cg, u = proj[:, 0:cd], proj[:, cd:2 * cd], proj[:, 2 * cd:3 * cd]
    z = cg * u
    z1 = _shift_down(z, 1, halo_ref)
    z2 = _shift_down(z, 2, halo_ref)
    conv = cw_ref[0:1, :] * z2 + cw_ref[1:2, :] * z1 + cw_ref[2:3, :] * z
    heads = []
    scale = HEAD ** -0.5
    for hh in range(RET_HEADS):
        lo = hh * HEAD
        q = proj[:, 3 * cd + lo:3 * cd + lo + HEAD]
        k = proj[:, 4 * cd + lo:4 * cd + lo + HEAD]
        v = proj[:, 5 * cd + lo:5 * cd + lo + HEAD]
        gate = proj[:, 6 * cd + lo:6 * cd + lo + HEAD]
        qr = _rope(q, cos, sin)
        kr = _rope(k, cos, sin) * scale
        sc = _mm_nt(_bf(qr), _bf(kr)) * dm_ref[hh]
        qs = qr * qd_ref[:, lo:lo + HEAD]
        ks = kr * kd_ref[:, lo:lo + HEAD]
        o = _mm(_bf(sc), _bf(v)) + _mm(_bf(qs), _bf(state_of(hh)))
        on, orstd = _rms_fwd(o)
        sig = _sigmoid(gate)
        heads.append(dict(qr=qr, kr=kr, v=v, gate=gate, sc=sc, qs=qs, ks=ks, on=on, orstd=orstd, sig=sig))
    return dict(n=n, rstd=rstd, h=h, proj=proj, bg=bg, cg=cg, u=u, z=z, z1=z1, z2=z2, conv=conv, heads=heads)


def _even_fwd(x, modp, w448, w128, l, cw, cos, sin, consts, ts):
    s, d = x.shape
    nt = s // ts
    dm, qd, kd, tdec = consts
    cd = 4 * HEAD
    e_in = N_DEV * w448.shape[2]

    def body(x_ref, mp_ref, win_ref, cw_ref, cos_ref, sin_ref, dm_ref, qd_ref, kd_ref, wout_ref,
             x1_ref, y_ref, st_ref, zh_ref, proj_ref, state, halo):
        @pl.when(pl.program_id(0) == 0)
        def _():
            state[...] = jnp.zeros_like(state)
            halo[...] = jnp.zeros_like(halo)
        xv = x_ref[...]
        st_ref[...] = state[...]
        zh_ref[...] = halo[...]
        r = _even_recompute(xv, mp_ref, win_ref[...].reshape(e_in, d), cw_ref, cos_ref[...], sin_ref[...],
                            dm_ref, qd_ref, kd_ref, halo, lambda hh: state[hh])
        proj_ref[...] = r["proj"]
        halo[...] = r["z"][ts - 8:ts, :]
        parts = [r["bg"] * r["conv"]]
        for hh, hd in enumerate(r["heads"]):
            state[hh] = state[hh] * tdec[hh] + _mm_tn(_bf(hd["ks"]), _bf(hd["v"]))
            rg = cw_ref[3:4, hh * HEAD:(hh + 1) * HEAD]
            parts.append((hd["gate"] * hd["sig"]) * (hd["on"] * rg))
        mcat = jnp.concatenate(parts, axis=1)
        y = _mm(_bf(mcat), wout_ref[...].reshape(d, d))
        y_ref[...] = y
        x1_ref[...] = xv + mp_ref[2:3, :] * y

    tile = pl.BlockSpec((ts, d), lambda i: (i, 0))
    rt = pl.BlockSpec((ts, HEAD), lambda i: (i, 0))
    shp = jax.ShapeDtypeStruct((s, d), F32)
    return _pcall(
        body, name="even_fwd", grid=(nt,),
        out_shape=(shp, shp, jax.ShapeDtypeStruct((nt, RET_HEADS, HEAD, HEAD), F32),
                   jax.ShapeDtypeStruct((nt, 8, cd), F32), jax.ShapeDtypeStruct((s, e_in), F32)),
        in_specs=[tile, _full(modp.shape), _wspec4(w448, l), _full(cw.shape), rt, rt,
                  _full(dm.shape), _full(qd.shape), _full(kd.shape), _wspec4(w128, l)],
        out_specs=(tile, tile, pl.BlockSpec((None, RET_HEADS, HEAD, HEAD), lambda i: (i, 0, 0, 0)),
                   pl.BlockSpec((None, 8, cd), lambda i: (i, 0, 0)), pl.BlockSpec((ts, e_in), lambda i: (i, 0))),
        scratch_shapes=[pltpu.VMEM((RET_HEADS, HEAD, HEAD), F32), pltpu.VMEM((8, cd), F32)],
        compiler_params=_params(1),
    )(x, modp, w448, cw, cos, sin, dm, qd, kd, w128)


def _even_bwd(x, dx1, y, states, zhalo, proj, modp, w448, w128, l, cw, cos, sin, consts, ts):
    s, d = x.shape
    nt = s // ts
    dm, qd, kd, tdec = consts
    cd = 4 * HEAD
    e_in = N_DEV * w448.shape[2]
    scale = HEAD ** -0.5

    def body(x_ref, dx1_ref, y_ref, st_ref, zh_ref, proj_ref, mp_ref, win_ref, cw_ref, cos_ref, sin_ref,
             dm_ref, qd_ref, kd_ref, wout_ref,
             dx_ref, dproj_ref, h_ref, m_ref, dy_ref, sg_ref, gstate, halo_d):
        @pl.when(pl.program_id(0) == 0)
        def _():
            gstate[...] = jnp.zeros_like(gstate)
            halo_d[...] = jnp.zeros_like(halo_d)
            sg_ref[...] = jnp.zeros_like(sg_ref)
        xv = x_ref[...]
        cos, sin = cos_ref[...], sin_ref[...]
        win = win_ref[...].reshape(e_in, d)
        r = _even_recompute(xv, mp_ref, win, cw_ref, cos, sin, dm_ref, qd_ref, kd_ref, zh_ref,
                            lambda hh: st_ref[hh], proj_ref[...])
        parts = [r["bg"] * r["conv"]]
        for hh, hd in enumerate(r["heads"]):
            rg = cw_ref[3:4, hh * HEAD:(hh + 1) * HEAD]
            parts.append((hd["gate"] * hd["sig"]) * (hd["on"] * rg))
        m_ref[...] = _bf(jnp.concatenate(parts, axis=1))
        h_ref[...] = _bf(r["h"])

        dx1 = dx1_ref[...]
        dy = mp_ref[2:3, :] * dx1
        dyb = _bf(dy)
        dy_ref[...] = dyb
        sg_ref[2:3, :] += _sum0(dx1 * y_ref[...])
        dmix = _mm_nt(dyb, wout_ref[...].reshape(d, d))

        da_out = dmix[:, 0:cd]
        dbg = da_out * r["conv"]
        dconv = da_out * r["bg"]
        dc1 = _shift_up(dconv, 1, halo_d)
        dc2 = _shift_up(dconv, 2, halo_d)
        dz = cw_ref[2:3, :] * dconv + cw_ref[1:2, :] * dc1 + cw_ref[0:1, :] * dc2
        halo_d[...] = dconv[0:8, :]
        sg_ref[4:5, 0:cd] += _sum0(dconv * r["z2"])
        sg_ref[5:6, 0:cd] += _sum0(dconv * r["z1"])
        sg_ref[6:7, 0:cd] += _sum0(dconv * r["z"])
        dcg = dz * r["u"]
        du = dz * r["cg"]

        dqs, dks, dvs, dgs = [], [], [], []
        for hh, hd in enumerate(r["heads"]):
            lo = hh * HEAD
            rg = cw_ref[3:4, lo:lo + HEAD]
            dr = dmix[:, cd + lo:cd + lo + HEAD]
            sig, gate, on = hd["sig"], hd["gate"], hd["on"]
            rn = on * rg
            dgate = dr * rn * (sig * (1.0 + gate * (1.0 - sig)))
            drn = dr * (gate * sig)
            sg_ref[7:8, lo:lo + HEAD] += _sum0(drn * on)
            do = _rms_bwd(drn * rg, on, hd["orstd"])
            dob = _bf(do)
            gst = _bf(gstate[hh])
            scb = _bf(hd["sc"])
            vb = _bf(hd["v"])
            qrb, krb = _bf(hd["qr"]), _bf(hd["kr"])
            dv = _mm_tn(scb, dob) + _mm(_bf(hd["ks"]), gst)
            dsc = _bf(_mm_nt(dob, vb) * dm_ref[hh])
            dqr = _mm(dsc, krb) + _mm_nt(dob, _bf(st_ref[hh])) * qd_ref[:, lo:lo + HEAD]
            dkr = _mm_tn(dsc, qrb) + _mm_nt(vb, gst) * kd_ref[:, lo:lo + HEAD]
            gstate[hh] = gstate[hh] * tdec[hh] + _mm_tn(_bf(hd["qs"]), dob)
            dqs.append(_rope_t(dqr, cos, sin))
            dks.append(_rope_t(dkr * scale, cos, sin))
            dvs.append(dv)
            dgs.append(dgate)

        dproj = _bf(jnp.concatenate([dbg, dcg, du] + dqs + dks + dvs + dgs, axis=1))
        dproj_ref[...] = dproj
        dh = _mm(dproj, win)
        dxn, dsh, dsc1, dg = _normmod_bwd(dh, r["n"], r["rstd"], mp_ref[6:7, :], mp_ref[1:2, :])
        dx_ref[...] = dx1 + dxn
        sg_ref[0:1, :] += dsh
        sg_ref[1:2, :] += dsc1
        sg_ref[3:4, :] += dg

    rev = lambda i: (nt - 1 - i, 0)
    tile = pl.BlockSpec((ts, d), rev)
    rt = pl.BlockSpec((ts, HEAD), rev)
    bft = jax.ShapeDtypeStruct((s, d), BF16)
    return _pcall(
        body, name="even_bwd", grid=(nt,),
        out_shape=(jax.ShapeDtypeStruct((s, d), F32), jax.ShapeDtypeStruct((s, e_in), BF16), bft, bft, bft,
                   jax.ShapeDtypeStruct((8, d), F32)),
        in_specs=[tile, tile, tile,
                  pl.BlockSpec((None, RET_HEADS, HEAD, HEAD), lambda i: (nt - 1 - i, 0, 0, 0)),
                  pl.BlockSpec((None, 8, cd), lambda i: (nt - 1 - i, 0, 0)), pl.BlockSpec((ts, e_in), rev),
                  _full(modp.shape), _wspec4(w448, l), _full(cw.shape), rt, rt,
                  _full(dm.shape), _full(qd.shape), _full(kd.shape), _wspec4(w128, l)],
        out_specs=(tile, pl.BlockSpec((ts, e_in), rev), tile, tile, tile, _full((8, d))),
        scratch_shapes=[pltpu.VMEM((RET_HEADS, HEAD, HEAD), F32), pltpu.VMEM((8, cd), F32)],
        compiler_params=_params(1),
    )(x, dx1, y, states, zhalo, proj, modp, w448, cw, cos, sin, dm, qd, kd, w128)


def _odd_qkv_fwd(x, modp, w384, j, qkg, ts):
    s, d = x.shape
    n3 = N_DEV * w384.shape[2]

    def body(x_ref, mp_ref, w_ref, g_ref, o_ref, pre_ref):
        _, _, h = _normmod_fwd(x_ref[...], mp_ref[6:7, :], mp_ref[1:2, :], mp_ref[0:1, :])
        qkv = _mm_nt(_bf(h), w_ref[...].reshape(n3, d))
        pre_ref[...] = qkv[:, 0:2 * d]
        for hh in range(SB_HEADS):
            lo = hh * HEAD
            qn, _ = _rms_fwd(qkv[:, lo:lo + HEAD])
            kn, _ = _rms_fwd(qkv[:, d + lo:d + lo + HEAD])
            o_ref[:, lo:lo + HEAD] = _bf(qn * g_ref[0:1, :])
            o_ref[:, d + lo:d + lo + HEAD] = _bf(kn * g_ref[1:2, :])
        o_ref[:, 2 * d:3 * d] = _bf(qkv[:, 2 * d:3 * d])

    return _pcall(
        body, name="odd_qkv_fwd", grid=(s // ts,),
        out_shape=(jax.ShapeDtypeStruct((s, n3), BF16), jax.ShapeDtypeStruct((s, 2 * d), F32)),
        in_specs=[pl.BlockSpec((ts, d), lambda i: (i, 0)), _full(modp.shape), _wspec4(w384, j),
                  _full(qkg.shape)],
        out_specs=(pl.BlockSpec((ts, n3), lambda i: (i, 0)), pl.BlockSpec((ts, 2 * d), lambda i: (i, 0))),
        compiler_params=_params(1),
    )(x, modp, w384, qkg)


SB_QUERIES = 512
SB_WIDE = 256
SB_LOOP_BLOCKS = 2


def _sb_logits(q, kw, mask):
    z = _mm_nt(q, kw) * (HEAD ** -0.5)
    e = jnp.exp(-jnp.abs(z))
    lb = jnp.minimum(z, 0.0) - jnp.log(1.0 + e)
    lk = lb - z
    if mask is not None:
        lk = jnp.where(mask, lk, 0.0)
    return lb, lk


def _tri(n, above):
    ri = lax.broadcasted_iota(jnp.int32, (n, n), 0)
    ci = lax.broadcasted_iota(jnp.int32, (n, n), 1)
    return ((ri > ci) if above else (ri < ci)).astype(BF16)


def _split_dot(a, tri):
    hi = _bf(a)
    lo = _bf(a - hi.astype(F32))
    return _mm(hi, tri) + _mm(lo, tri)


def _sb_fwd(qkv, tq):
    s = qkv.shape[0]
    d = qkv.shape[1] // 3
    nq = s // tq
    assert tq % SB_WIDE == 0
    parts = tq // SB_WIDE

    def body(q_ref, k_ref, v_ref, o_ref, t_ref, o_acc, run):
        qi = pl.program_id(1)
        base = qi * tq
        upper = _tri(SB_WIDE, True)
        o_acc[...] = jnp.zeros_like(o_acc)
        run[...] = jnp.zeros_like(run)

        def wide_step(ks, row0, masked, nblk):
            rows = slice(row0, tq)
            width = nblk * SB_WIDE
            mask = None
            if masked:
                qpos = base + row0 + lax.broadcasted_iota(jnp.int32, (tq - row0, width), 0)
                mask = qpos > ks + lax.broadcasted_iota(jnp.int32, (tq - row0, width), 1)
            lb, lk = _sb_logits(q_ref[rows, :], k_ref[pl.ds(ks, width), :], mask)
            blocks = [lk[:, b * SB_WIDE:(b + 1) * SB_WIDE] for b in range(nblk)]
            right = run[rows, :]
            accs = [None] * nblk
            for b in reversed(range(nblk)):
                accs[b] = _split_dot(blocks[b], upper) + right
                right = right + jnp.sum(blocks[b], axis=1, keepdims=True)
            w = jnp.exp(lb + (accs[0] if nblk == 1 else jnp.concatenate(accs, axis=1)))
            if masked:
                w = jnp.where(mask, w, 0.0)
            o_acc[rows, :] += _mm(_bf(w), v_ref[pl.ds(ks, width), :])
            run[rows, :] = right

        for part in reversed(range(parts)):
            wide_step(pl.multiple_of(base + part * SB_WIDE, SB_WIDE), part * SB_WIDE, True, 1)
        loop_width = SB_LOOP_BLOCKS * SB_WIDE
        nsteps = qi * (tq // loop_width)

        def step(it, carry):
            wide_step(pl.multiple_of((nsteps - 1 - it) * loop_width, loop_width), 0, False, SB_LOOP_BLOCKS)
            return carry

        lax.fori_loop(0, nsteps, step, 0)
        o_ref[...] = _bf(o_acc[...])
        t_ref[...] = run[...]

    nh = d // HEAD
    return _pcall(
        body, name="sb_fwd", grid=(nh, nq),
        out_shape=(jax.ShapeDtypeStruct((s, d), BF16), jax.ShapeDtypeStruct((nh, s, 1), F32)),
        in_specs=[pl.BlockSpec((tq, HEAD), lambda h, i: (i, h)),
                  pl.BlockSpec((s, HEAD), lambda h, i: (0, nh + h)),
                  pl.BlockSpec((s, HEAD), lambda h, i: (0, 2 * nh + h))],
        out_specs=(pl.BlockSpec((tq, HEAD), lambda h, i: (i, h)),
                   pl.BlockSpec((None, tq, 1), lambda h, i: (h, i, 0))),
        scratch_shapes=[pltpu.VMEM((tq, HEAD), F32), pltpu.VMEM((tq, 1), F32)],
        compiler_params=_params(2),
    )(qkv, qkv, qkv)


def _sb_bwd(qkv, do, tot, tq):
    s = qkv.shape[0]
    d = qkv.shape[1] // 3
    nq = s // tq
    scale = HEAD ** -0.5
    assert tq % SB_WIDE == 0
    parts = tq // SB_WIDE

    def body(q_ref, k_ref, v_ref, do_ref, t_ref, dq_ref, dk_ref, dv_ref, pk, pd):
        qi = pl.program_id(1)

        @pl.when(qi == 0)
        def _():
            dk_ref[...] = jnp.zeros_like(dk_ref)
            dv_ref[...] = jnp.zeros_like(dv_ref)
        base = qi * tq
        upper = _tri(SB_WIDE, True)
        lower = _tri(SB_WIDE, False)
        dq_ref[...] = jnp.zeros_like(dq_ref)
        pk[...] = jnp.zeros_like(pk)
        pd[...] = jnp.zeros_like(pd)

        def wide_step(ks, row0, masked, nblk):
            rows = slice(row0, tq)
            width = nblk * SB_WIDE
            cut = lambda a: [a[:, b * SB_WIDE:(b + 1) * SB_WIDE] for b in range(nblk)]
            join = lambda parts_: parts_[0] if nblk == 1 else jnp.concatenate(parts_, axis=1)
            mask = None
            if masked:
                qpos = base + row0 + lax.broadcasted_iota(jnp.int32, (tq - row0, width), 0)
                mask = qpos > ks + lax.broadcasted_iota(jnp.int32, (tq - row0, width), 1)
            kw = k_ref[pl.ds(ks, width), :]
            lb, lk = _sb_logits(q_ref[rows, :], kw, mask)
            left = pk[rows, :]
            total = t_ref[rows, :]
            accs = []
            for blk in cut(lk):
                left = left + jnp.sum(blk, axis=1, keepdims=True)
                accs.append(_split_dot(blk, upper) + (total - left))
            pk[rows, :] = left
            w = jnp.exp(lb + join(accs))
            if masked:
                w = jnp.where(mask, w, 0.0)
            de = _mm_nt(do_ref[rows, :], v_ref[pl.ds(ks, width), :]) * w
            before = pd[rows, :]
            dlks = []
            for blk in cut(de):
                dlks.append(_split_dot(blk, lower) + before)
                before = before + jnp.sum(blk, axis=1, keepdims=True)
            pd[rows, :] = before
            dz = (de - jnp.exp(lb) * (de + join(dlks))) * scale
            if masked:
                dz = jnp.where(mask, dz, 0.0)
            dzb = _bf(dz)
            dq_ref[rows, :] += _mm(dzb, kw)
            dv_ref[pl.ds(ks, width), :] += _mm_tn(_bf(w), do_ref[rows, :])
            dk_ref[pl.ds(ks, width), :] += _mm_tn(dzb, q_ref[rows, :])

        loop_width = SB_LOOP_BLOCKS * SB_WIDE

        def step(jb, carry):
            wide_step(pl.multiple_of(jb * loop_width, loop_width), 0, False, SB_LOOP_BLOCKS)
            return carry

        lax.fori_loop(0, qi * (tq // loop_width), step, 0)
        for part in range(parts):
            wide_step(pl.multiple_of(base + part * SB_WIDE, SB_WIDE), part * SB_WIDE, True, 1)

    nh = d // HEAD
    shp = jax.ShapeDtypeStruct((s, d), F32)
    return _pcall(
        body, name="sb_bwd", grid=(nh, nq), out_shape=(shp, shp, shp),
        in_specs=[pl.BlockSpec((tq, HEAD), lambda h, i: (i, h)),
                  pl.BlockSpec((s, HEAD), lambda h, i: (0, nh + h)),
                  pl.BlockSpec((s, HEAD), lambda h, i: (0, 2 * nh + h)),
                  pl.BlockSpec((tq, HEAD), lambda h, i: (i, h)),
                  pl.BlockSpec((None, tq, 1), lambda h, i: (h, i, 0))],
        out_specs=(pl.BlockSpec((tq, HEAD), lambda h, i: (i, h)),
                   pl.BlockSpec((s, HEAD), lambda h, i: (0, h)),
                   pl.BlockSpec((s, HEAD), lambda h, i: (0, h))),
        scratch_shapes=[pltpu.VMEM((tq, 1), F32), pltpu.VMEM((tq, 1), F32)],
        compiler_params=_params(2),
    )(qkv, qkv, qkv, do, tot)


def _odd_out_fwd(o, x, modp, w128, slot, ts):
    s, d = x.shape

    def body(o_ref, x_ref, mp_ref, w_ref, x1_ref, y_ref):
        y = _mm(o_ref[...], w_ref[...].reshape(d, d))
        y_ref[...] = y
        x1_ref[...] = x_ref[...] + mp_ref[2:3, :] * y

    tile = pl.BlockSpec((ts, d), lambda i: (i, 0))
    shp = jax.ShapeDtypeStruct((s, d), F32)
    return _pcall(
        body, name="odd_out_fwd", grid=(s // ts,), out_shape=(shp, shp),
        in_specs=[tile, tile, _full(modp.shape), _wspec4(w128, slot)],
        out_specs=(tile, tile), compiler_params=_params(1),
    )(o, x, modp, w128)


def _odd_out_bwd(dx1, y, modp, w128, slot, ts):
    s, d = dx1.shape

    def body(dx1_ref, y_ref, mp_ref, w_ref, do_ref, dy_ref, sg_ref):
        @pl.when(pl.program_id(0) == 0)
        def _():
            sg_ref[...] = jnp.zeros_like(sg_ref)
        dx1v = dx1_ref[...]
        dyb = _bf(mp_ref[2:3, :] * dx1v)
        dy_ref[...] = dyb
        do_ref[...] = _bf(_mm_nt(dyb, w_ref[...].reshape(d, d)))
        sg_ref[2:3, :] += _sum0(dx1v * y_ref[...])

    tile = pl.BlockSpec((ts, d), lambda i: (i, 0))
    bft = jax.ShapeDtypeStruct((s, d), BF16)
    return _pcall(
        body, name="odd_out_bwd", grid=(s // ts,),
        out_shape=(bft, bft, jax.ShapeDtypeStruct((8, d), F32)),
        in_specs=[tile, tile, _full(modp.shape), _wspec4(w128, slot)],
        out_specs=(tile, tile, _full((8, d))), compiler_params=_params(1),
    )(dx1, y, modp, w128)


def _odd_qkv_bwd(x, dx1, dq, dk, dv, pre, sg_in, modp, w384, j, qkg, ts):
    s, d = x.shape
    n3 = N_DEV * w384.shape[2]

    def body(x_ref, dx1_ref, dq_ref, dk_ref, dv_ref, pre_ref, sgi_ref, mp_ref, w_ref, g_ref,
             dx_ref, dqkv_ref, h_ref, sg_ref):
        @pl.when(pl.program_id(0) == 0)
        def _():
            sg_ref[...] = sgi_ref[...]
        gmix, sc1 = mp_ref[6:7, :], mp_ref[1:2, :]
        n, rstd, h = _normmod_fwd(x_ref[...], gmix, sc1, mp_ref[0:1, :])
        hb = _bf(h)
        h_ref[...] = hb
        w = w_ref[...].reshape(n3, d)
        qkv = pre_ref[...]
        parts_q, parts_k = [], []
        gq, gk = g_ref[0:1, :], g_ref[1:2, :]
        dgq = jnp.zeros((1, HEAD), F32)
        dgk = jnp.zeros((1, HEAD), F32)
        for hh in range(SB_HEADS):
            lo = hh * HEAD
            qn, qr = _rms_fwd(qkv[:, lo:lo + HEAD])
            kn, kr = _rms_fwd(qkv[:, d + lo:d + lo + HEAD])
            dqn = dq_ref[:, lo:lo + HEAD]
            dkn = dk_ref[:, lo:lo + HEAD]
            dgq = dgq + _sum0(dqn * qn)
            dgk = dgk + _sum0(dkn * kn)
            parts_q.append(_rms_bwd(dqn * gq, qn, qr))
            parts_k.append(_rms_bwd(dkn * gk, kn, kr))
        dqkv = _bf(jnp.concatenate(parts_q + parts_k + [dv_ref[...]], axis=1))
        dqkv_ref[...] = dqkv
        dh = _mm(dqkv, w)
        dxn, dsh, dsc, dg = _normmod_bwd(dh, n, rstd, gmix, sc1)
        dx_ref[...] = dx1_ref[...] + dxn
        sg_ref[0:1, :] += dsh
        sg_ref[1:2, :] += dsc
        sg_ref[3:4, :] += dg
        sg_ref[4:5, 0:HEAD] += dgq
        sg_ref[5:6, 0:HEAD] += dgk

    tile = pl.BlockSpec((ts, d), lambda i: (i, 0))
    return _pcall(
        body, name="odd_qkv_bwd", grid=(s // ts,),
        out_shape=(jax.ShapeDtypeStruct((s, d), F32), jax.ShapeDtypeStruct((s, n3), BF16),
                   jax.ShapeDtypeStruct((s, d), BF16), jax.ShapeDtypeStruct((8, d), F32)),
        in_specs=[tile, tile, tile, tile, tile, pl.BlockSpec((ts, 2 * d), lambda i: (i, 0)), _full((8, d)),
                  _full(modp.shape), _wspec4(w384, j), _full(qkg.shape)],
        out_specs=(tile, pl.BlockSpec((ts, n3), lambda i: (i, 0)), tile, _full((8, d))),
        compiler_params=_params(1),
    )(x, dx1, dq, dk, dv, pre, sg_in, modp, w384, qkg)


def _pad_rows(a, rows):
    return jnp.concatenate([a, jnp.zeros((rows - a.shape[0],) + a.shape[1:], a.dtype)], axis=0)


def kernel(x, c, ada_w, ada_b, norm_mix_g, norm_ffn_g, ev_w_in, ev_conv_w, ev_ret_norm_g, ev_w_out, od_w_qkv, od_q_norm_g, od_k_norm_g, od_w_out, ffn_w_gate, ffn_w_up, ffn_w_down, loss_target, m_ada_w, m_ada_b, m_norm_mix_g, m_norm_ffn_g, m_ev_w_in, m_ev_conv_w, m_ev_ret_norm_g, m_ev_w_out, m_od_w_qkv, m_od_q_norm_g, m_od_k_norm_g, m_od_w_out, m_ffn_w_gate, m_ffn_w_up, m_ffn_w_down, v_ada_w, v_ada_b, v_norm_mix_g, v_norm_ffn_g, v_ev_w_in, v_ev_conv_w, v_ev_ret_norm_g, v_ev_w_out, v_od_w_qkv, v_od_q_norm_g, v_od_k_norm_g, v_od_w_out, v_ffn_w_gate, v_ffn_w_up, v_ffn_w_down):
    me = 4 * lax.axis_index("x") + 2 * lax.axis_index("y") + lax.axis_index("c")
    xs = x[0]
    tgt = loss_target[0]
    s, d = xs.shape
    depth = ada_w.shape[0]
    n_even, n_odd = ev_w_in.shape[0], od_w_qkv.shape[0]
    ts = 256
    tq = SB_QUERIES
    cd = 4 * HEAD
    cc = ev_conv_w.shape[2]

    pack0 = jnp.zeros((8, d), F32).at[0].set(c[0]).at[1, :n_even * 3 * cc].set(ev_conv_w.reshape(-1))
    got0, _ = _all_gather(pack0, "gather_cond")
    got0 = got0.reshape(N_DEV, 8, d)
    c_all = got0[:, 0, :]
    conv_all = got0[:, 1, :n_even * 3 * cc].reshape(N_DEV, n_even, 3, cc).transpose(1, 2, 0, 3)
    conv_all = conv_all.reshape(n_even, 3, N_DEV * cc)
    cols = ada_w.shape[2]
    ada_b_cols = lax.dynamic_slice(ada_b, (0, me * cols), (depth, cols))
    mod_cols = _ada_fwd(c_all, ada_w, ada_b_cols)
    got1, cond_done = _all_gather(mod_cols.reshape(depth * N_DEV, cols), "gather_mod")
    got1 = got1.reshape(N_DEV, depth, N_DEV, cols)
    mod = lax.dynamic_index_in_dim(got1, me, axis=2, keepdims=False)
    mod = mod.transpose(1, 0, 2).reshape(depth, 6, d)
    modps = [jnp.concatenate([mod[l], norm_mix_g[l][None], norm_ffn_g[l][None]], axis=0) for l in range(depth)]

    in_flight = []
    started = cond_done
    for l in range(depth):
        j = l // 2
        plain = lambda w: _bf(w + started)
        tr = lambda w: plain(w).T
        blocks = [tr(ev_w_in[j]), plain(ev_w_out[j])] if l % 2 == 0 else [tr(od_w_qkv[j]), plain(od_w_out[j])]
        mixer, started = _exchange_start(blocks, [_landing(b, me, False) for b in blocks], False,
                                         f"gather_start_mixer_{l}")
        blocks = [jnp.concatenate([tr(ffn_w_gate[l]), tr(ffn_w_up[l]), plain(ffn_w_down[l])], axis=0)]
        ffn, started = _exchange_start(blocks, [_landing(b, me, False) for b in blocks], False,
                                       f"gather_start_ffn_{l}")
        in_flight.append((mixer, ffn))
        modps[0] = modps[0] + started
    n_ffn = ffn_w_down.shape[1]

    def mixer_weights(l, after):
        got = _exchange_wait(*in_flight[l][0], False, [after], f"gather_wait_mixer_{l}")
        return got[0].reshape(N_DEV, 1, -1, d), got[1].reshape(N_DEV, 1, -1, d)

    def ffn_weights(l, after):
        got = _exchange_wait(*in_flight[l][1], False, [after], f"gather_wait_ffn_{l}")
        return got[0].reshape(N_DEV, 3, n_ffn, d)

    cos, sin = _rope_tables(s)
    consts = _retention_consts(ts)
    cws = [_pad_rows(jnp.concatenate([conv_all[j], ev_ret_norm_g[j][None]], axis=0), 8) for j in range(n_even)]
    qkgs = [_pad_rows(jnp.stack([od_q_norm_g[j], od_k_norm_g[j]]), 8) for j in range(n_odd)]

    saved = []
    weights = []
    cur = xs
    for l in range(depth):
        j = l // 2
        w_in, w_out = mixer_weights(l, cur)
        if l % 2 == 0:
            x1, y, states, zhalo, proj = _even_fwd(cur, modps[l], w_in, w_out, 0, cws[j], cos, sin, consts, ts)
            mix = (states, zhalo, proj)
        else:
            qkv, pre = _odd_qkv_fwd(cur, modps[l], w_in, 0, qkgs[j], ts)
            o, tot = _sb_fwd(qkv, tq)
            x1, y = _odd_out_fwd(o, cur, modps[l], w_out, 0, ts)
            mix = (qkv, o, tot, pre)
        w_ffn = ffn_weights(l, x1)
        weights.append((w_in, w_out, w_ffn))
        x2, f, ab = _ffn_fwd(x1, modps[l], w_ffn, 0, ts)
        saved.append((cur, x1, y, (f, ab), mix))
        cur = x2

    dx, loss_part = _loss_grad(cur, tgt, ts)
    loss = lax.psum(loss_part[0, 0], ("x", "y", "c"))

    dmod = [None] * depth
    d_gmix = [None] * depth
    d_gffn = [None] * depth
    d_conv = [None] * n_even
    d_retg = [None] * n_even
    d_qg = [None] * n_odd
    d_kg = [None] * n_odd
    grads_in_flight = [None] * depth
    for l in reversed(range(depth)):
        j = l // 2
        x0, x1, y, (f, ab), mix = saved[l]
        w_in, w_out, w_ffn = weights[l]
        g_ffn = lax.empty(w_ffn.shape, BF16)
        g_in = lax.empty(w_in.shape, BF16)
        g_out = lax.empty(w_out.shape, BF16)
        dx1, dab, h2, sv, df, sg2 = _ffn_bwd(x1, f, ab, dx, modps[l], w_ffn, 0, ts)
        g_ffn = _tn_matmul(dab, 0, h2, g_ffn, 0, "tn_gate")
        g_ffn = _tn_matmul(dab, 1, h2, g_ffn, 1, "tn_up")
        g_ffn = _tn_matmul(sv, 0, df, g_ffn, 2, "tn_down")
        if l == 0:
            pieces = [g_ffn.reshape(N_DEV, -1, d)]
            last_ffn_flight, started = _exchange_start(pieces, [_landing(p, me, True) for p in pieces], True,
                                                       "grads_start_ffn_0")
            modps[0] = modps[0] + started
        if l % 2 == 0:
            states, zhalo, proj = mix
            dx, dproj, hb, mb, dyb, sg1 = _even_bwd(x0, dx1, y, states, zhalo, proj, modps[l], w_in, w_out, 0,
                                                    cws[j], cos, sin, consts, ts)
            g_in = _tn_matmul(dproj, 0, hb, g_in, 0, "tn_ev_in")
            g_out = _tn_matmul(mb, 0, dyb, g_out, 0, "tn_ev_out")
            d_conv[j] = sg1[4:7, :cd]
            d_retg[j] = sg1[7, :cd]
        else:
            qkv, o, tot, pre = mix
            do, dyb, sg0 = _odd_out_bwd(dx1, y, modps[l], w_out, 0, ts)
            dq, dk, dv = _sb_bwd(qkv, do, tot, tq)
            dx, dqkv, hb, sg1 = _odd_qkv_bwd(x0, dx1, dq, dk, dv, pre, sg0, modps[l], w_in, 0, qkgs[j], ts)
            g_in = _tn_matmul(dqkv, 0, hb, g_in, 0, "tn_od_qkv")
            g_out = _tn_matmul(o, 0, dyb, g_out, 0, "tn_od_out")
            d_qg[j] = sg1[4, :HEAD]
            d_kg[j] = sg1[5, :HEAD]
        pieces = [g.reshape(N_DEV, -1, d) for g in ((g_ffn, g_in, g_out) if l > 0 else (g_in, g_out))]
        if l > 0:
            grads_in_flight[l], started = _exchange_start(pieces, [_landing(p, me, True) for p in pieces], True,
                                                          f"grads_start_{l}")
            modps[l - 1] = modps[l - 1] + started
        dmod[l] = jnp.concatenate([sg1[0:3], sg2[0:3]], axis=0).reshape(-1)
        d_gmix[l] = sg1[3]
        d_gffn[l] = sg2[3]

    small = jnp.concatenate(
        [jnp.stack(dmod).reshape(-1), jnp.stack(d_gmix).reshape(-1), jnp.stack(d_gffn).reshape(-1),
         jnp.stack(d_retg).reshape(-1), jnp.stack(d_qg).reshape(-1), jnp.stack(d_kg).reshape(-1),
         jnp.stack(d_conv).reshape(-1)])
    n_small = small.shape[0]
    rows_small = -(-n_small // (8 * 128)) * 8
    small = jnp.concatenate([small, jnp.zeros((rows_small * 128 - n_small,), F32)]).reshape(rows_small, 128)
    small_flight, started = _exchange_start([small], [_landing(small, me, False)], False, "small_start")
    grads_in_flight[0], started = _exchange_start(pieces, [_landing(p, me, True, started) for p in pieces],
                                                  True, "grads_start_0")

    res = {}
    big = {"ev_w_in": (ev_w_in, m_ev_w_in, v_ev_w_in), "ev_w_out": (ev_w_out, m_ev_w_out, v_ev_w_out),
           "od_w_qkv": (od_w_qkv, m_od_w_qkv, v_od_w_qkv), "od_w_out": (od_w_out, m_od_w_out, v_od_w_out),
           "ffn_w_gate": (ffn_w_gate, m_ffn_w_gate, v_ffn_w_gate), "ffn_w_up": (ffn_w_up, m_ffn_w_up, v_ffn_w_up),
           "ffn_w_down": (ffn_w_down, m_ffn_w_down, v_ffn_w_down)}
    flipped = ("ev_w_in", "ffn_w_gate", "ffn_w_up")
    flip = lambda a: a.transpose(0, 2, 1)
    for name in flipped:
        big[name] = tuple(flip(a) for a in big[name])
    for name, (w, _, _) in big.items():
        res[name] = tuple(lax.empty(w.shape, F32) for _ in range(4))

    def update(name, idx, g_layer, part=0):
        w, m, v = big[name]
        res[name] = _adamw_layer(w, g_layer, m, v, res[name], idx, "adamw_" + name, part)

    def ffn_updates(l, r_ffn):
        update("ffn_w_gate", l, r_ffn, 0)
        update("ffn_w_up", l, r_ffn, 1)
        update("ffn_w_down", l, r_ffn, 2)

    def mixer_updates(l, r_in, r_out):
        if l % 2 == 0:
            update("ev_w_in", l // 2, r_in)
        else:
            update("od_w_qkv", l // 2, _sum_slots(r_in, "sum_qkv").T)
        update("ev_w_out" if l % 2 == 0 else "od_w_out", l // 2, r_out)

    after = [dx + started]
    for l in reversed(range(1, depth)):
        recv = _exchange_wait(*grads_in_flight[l], True, after, f"grads_wait_{l}")
        ffn_updates(l, recv[0])
        mixer_updates(l, recv[1], recv[2])
        after = [res[name][1] for name in big]

    got2 = _exchange_wait(*small_flight, False, after, "small_wait")[0]
    tot_small = _sum_small(got2).reshape(-1)
    n_mod = depth * 6 * d
    dmod_all = got2.reshape(N_DEV, -1)[:, :n_mod].reshape(N_DEV, depth, 6 * d)
    dmod_cols = lax.dynamic_slice(dmod_all, (0, 0, me * cols), (N_DEV, depth, cols)).transpose(1, 0, 2)
    g_ada_w = _ada_bwd(c_all.T, dmod_cols)
    res["ada_w"] = (g_ada_w,) + _adamw_nd(ada_w, g_ada_w, m_ada_w, v_ada_w, "adamw_ada_w")

    off = [0]

    def take(shape):
        n = int(np.prod(shape))
        out = tot_small[off[0]:off[0] + n].reshape(shape)
        off[0] += n
        return out

    g_ada_b = take((depth, 6 * d))
    g_norm_mix = take((depth, d))
    g_norm_ffn = take((depth, d))
    g_ret_norm = take((n_even, cd))
    g_q_norm = take((n_odd, HEAD))
    g_k_norm = take((n_odd, HEAD))
    g_conv_full = take((n_even, 3, cd))
    g_conv = lax.dynamic_slice(g_conv_full, (0, 0, me * cc), (n_even, 3, cc))

    recv = _exchange_wait(*last_ffn_flight, True, [res["ada_w"][1]], "grads_wait_ffn_0")
    ffn_updates(0, recv[0])
    after = [res[name][1] for name in ("ffn_w_gate", "ffn_w_up", "ffn_w_down")]
    recv = _exchange_wait(*grads_in_flight[0], True, after, "grads_wait_0")
    mixer_updates(0, recv[0], recv[1])
    for name in flipped:
        res[name] = tuple(flip(a) for a in res[name])

    smalls = [("ada_b", ada_b, g_ada_b, m_ada_b, v_ada_b), ("norm_mix_g", norm_mix_g, g_norm_mix, m_norm_mix_g, v_norm_mix_g),
              ("norm_ffn_g", norm_ffn_g, g_norm_ffn, m_norm_ffn_g, v_norm_ffn_g),
              ("ev_conv_w", ev_conv_w, g_conv, m_ev_conv_w, v_ev_conv_w),
              ("ev_ret_norm_g", ev_ret_norm_g, g_ret_norm, m_ev_ret_norm_g, v_ev_ret_norm_g),
              ("od_q_norm_g", od_q_norm_g, g_q_norm, m_od_q_norm_g, v_od_q_norm_g),
              ("od_k_norm_g", od_k_norm_g, g_k_norm, m_od_k_norm_g, v_od_k_norm_g)]

    def pack(arrs):
        flat = jnp.concatenate([a.reshape(-1) for a in arrs])
        rows = -(-flat.shape[0] // (8 * 128)) * 8
        return jnp.concatenate([flat, jnp.zeros((rows * 128 - flat.shape[0],), F32)]).reshape(rows, 128)

    sd, sm, sv_ = _adamw(pack([t[1] for t in smalls]), pack([t[2] for t in smalls]),
                         pack([t[3] for t in smalls]), pack([t[4] for t in smalls]), "adamw_small")
    sd, sm, sv_ = sd.reshape(-1), sm.reshape(-1), sv_.reshape(-1)
    pos = 0
    for name, w, g, m, v in smalls:
        n = int(np.prod(w.shape))
        res[name] = (g, sd[pos:pos + n].reshape(w.shape), sm[pos:pos + n].reshape(w.shape),
                     sv_[pos:pos + n].reshape(w.shape))
        pos += n

    order = ["ada_w", "ada_b", "norm_mix_g", "norm_ffn_g", "ev_w_in", "ev_conv_w", "ev_ret_norm_g", "ev_w_out",
             "od_w_qkv", "od_q_norm_g", "od_k_norm_g", "od_w_out", "ffn_w_gate", "ffn_w_up", "ffn_w_down"]
    outs = [loss, dx[None]]
    for k in range(4):
        outs += [res[name][k] for name in order]
    return tuple(outs)
```

```python
import functools
import math

import numpy as np
import jax
import jax.numpy as jnp
from jax import lax
from jax.experimental import pallas as pl
from jax.experimental.pallas import tpu as pltpu

F32 = jnp.float32
BF16 = jnp.bfloat16
MESH = pl.DeviceIdType.MESH

N_DEV = 8
EPS = 1e-6
CHUNK = 64
HEAD = 128
RET_HEADS = 4
SB_HEADS = 8
ROPE_THETA = 10000.0
KEY_BLOCK = 128
ADAM_LR, ADAM_B1, ADAM_B2, ADAM_EPS, ADAM_WD, ADAM_STEP = 0.001, 0.9, 0.999, 1e-08, 0.01, 10
VMEM_LIMIT = 56 * 1024 * 1024


def _pcall(body, **kw):
    return pl.pallas_call(body, **kw)


def _params(n_grid=1, vmem=VMEM_LIMIT):
    return pltpu.CompilerParams(dimension_semantics=("arbitrary",) * n_grid, vmem_limit_bytes=vmem)


def _mm(a, b):
    return jnp.dot(a, b, preferred_element_type=F32)


def _mm_nt(a, b):
    return lax.dot_general(a, b, (((1,), (1,)), ((), ())), preferred_element_type=F32)


def _mm_tn(a, b):
    return lax.dot_general(a, b, (((0,), (0,)), ((), ())), preferred_element_type=F32)


def _bf(a):
    return a.astype(BF16)


def _sigmoid(a):
    return 1.0 / (1.0 + jnp.exp(-a))


def _sum0(a):
    return jnp.sum(a, axis=0, keepdims=True)


def _full(shape):
    nd = len(shape)
    return pl.BlockSpec(shape, lambda *_: (0,) * nd)


def _normmod_fwd(x, g, sc, sh):
    rstd = lax.rsqrt(jnp.mean(x * x, axis=-1, keepdims=True) + EPS)
    n = x * rstd
    return n, rstd, (n * g) * (1.0 + sc) + sh


def _normmod_bwd(dh, n, rstd, g, sc):
    dsh = _sum0(dh)
    dsc = _sum0(dh * (n * g))
    dg = _sum0(dh * n * (1.0 + sc))
    dn = dh * (g * (1.0 + sc))
    dx = rstd * (dn - n * jnp.mean(dn * n, axis=-1, keepdims=True))
    return dx, dsh, dsc, dg


def _rms_fwd(o):
    rstd = lax.rsqrt(jnp.mean(o * o, axis=-1, keepdims=True) + EPS)
    return o * rstd, rstd


def _rms_bwd(dn, n, rstd):
    return rstd * (dn - n * jnp.mean(dn * n, axis=-1, keepdims=True))


def _all_gather(x2d, name):
    m_per, n = x2d.shape
    space = pltpu.VMEM

    def body(x_ref, out_ref, done_ref, send_sems, recv_sems, local_sem):
        x, y, c = lax.axis_index("x"), lax.axis_index("y"), lax.axis_index("c")
        me, sibling = (x, y, c), (x, y, 1 - c)
        chips = [(1 - x, y), (x, 1 - y), (1 - x, 1 - y)]

        def rows(px, py, pc):
            return out_ref.at[pl.ds((4 * px + 2 * py + pc) * m_per, m_per), :]

        def copy(k, block, to, src=None):
            return pltpu.make_async_remote_copy(
                src_ref=rows(*block) if src is None else src, dst_ref=rows(*block),
                send_sem=send_sems.at[k], recv_sem=recv_sems.at[k],
                device_id=to, device_id_type=MESH)

        mine = pltpu.make_async_copy(x_ref, rows(*me), local_sem)
        mine.start()
        first = [copy(1 + j, me, (*chip, c), src=x_ref) for j, chip in enumerate(chips)]
        first += [copy(0, me, sibling, src=x_ref)]
        for cp in first:
            cp.start()
        passed = [copy(4 + j, (*chip, c), sibling) for j, chip in enumerate(chips)]
        for j, chip in enumerate(chips):
            copy(1 + j, (*chip, c), me).wait_recv()
            passed[j].start()
        copy(0, sibling, me).wait_recv()
        for j, chip in enumerate(chips):
            copy(4 + j, (*chip, 1 - c), me).wait_recv()
        for cp in first + passed:
            cp.wait_send()
        mine.wait()
        done_ref[...] = jnp.zeros_like(done_ref)

    out, done = _pcall(
        body, name=name,
        out_shape=(jax.ShapeDtypeStruct((N_DEV * m_per, n), x2d.dtype), jax.ShapeDtypeStruct((8, 128), F32)),
        in_specs=[pl.BlockSpec(memory_space=space)],
        out_specs=(pl.BlockSpec(memory_space=space), pl.BlockSpec(memory_space=pltpu.VMEM)),
        scratch_shapes=[pltpu.SemaphoreType.DMA((7,)), pltpu.SemaphoreType.DMA((7,)),
                        pltpu.SemaphoreType.DMA],
    )(x2d)
    return out, done[0, 0]


_HBM = pl.BlockSpec(memory_space=pltpu.HBM)
_SEM = pl.BlockSpec(memory_space=pltpu.SEMAPHORE)
_EFFECT = pltpu.SideEffectType.DATAFLOW_SIDE_EFFECTING


def _exchange_copies(src_refs, land_refs, send_sems, recv_sems, scatter):
    x, y, c = lax.axis_index("x"), lax.axis_index("y"), lax.axis_index("c")
    me = 4 * x + 2 * y + c
    out = []
    for i, (s_ref, l_ref) in enumerate(zip(src_refs, land_refs)):
        for k in (2, 4, 6, 3, 5, 7, 1):
            px = (1 - x) if (k >> 2) & 1 else x
            py = (1 - y) if (k >> 1) & 1 else y
            pc = (1 - c) if k & 1 else c
            out.append(pltpu.make_async_remote_copy(
                src_ref=s_ref.at[4 * px + 2 * py + pc] if scatter else s_ref, dst_ref=l_ref.at[me],
                send_sem=send_sems.at[7 * i + k - 1], recv_sem=recv_sems.at[7 * i + k - 1],
                device_id=(px, py, pc), device_id_type=MESH))
    return out


def _exchange_start(srcs, lands, scatter, name):
    n = len(srcs)

    def body(*refs):
        for cp in _exchange_copies(refs[:n], refs[n:2 * n], refs[2 * n], refs[2 * n + 1], scatter):
            cp.start()
        refs[-1][...] = jnp.zeros_like(refs[-1])

    arrays = list(srcs) + list(lands)
    outs = _pcall(
        body, name=name,
        out_shape=(pltpu.SemaphoreType.DMA((7 * n,)), pltpu.SemaphoreType.DMA((7 * n,)),
                   *[pltpu.HBM(a.shape, a.dtype) for a in arrays], jax.ShapeDtypeStruct((8, 128), F32)),
        in_specs=[_HBM] * (2 * n),
        out_specs=(_SEM, _SEM, *[_HBM] * (2 * n), pl.BlockSpec(memory_space=pltpu.VMEM)),
        input_output_aliases={i: 2 + i for i in range(2 * n)},
        compiler_params=pltpu.CompilerParams(has_side_effects=_EFFECT),
    )(*[pltpu.with_memory_space_constraint(a, pltpu.HBM) for a in arrays])
    return (outs[0], outs[1], list(outs[2:2 + n]), list(outs[2 + n:2 + 2 * n])), outs[-1][0, 0]


def _exchange_wait(send_sems, recv_sems, srcs, lands, scatter, after, name):
    n = len(srcs)
    after = list(after)

    def body(*refs):
        for cp in _exchange_copies(refs[:n], refs[n:2 * n], refs[2 * n], refs[2 * n + 1], scatter):
            cp.wait_send()
            cp.wait_recv()

    arrays = list(srcs) + list(lands)
    outs = _pcall(
        body, name=name,
        out_shape=tuple(pltpu.HBM(a.shape, a.dtype) for a in arrays),
        in_specs=[_HBM] * (2 * n) + [_SEM, _SEM] + [pl.BlockSpec(memory_space=pl.ANY)] * len(after),
        out_specs=tuple([_HBM] * (2 * n)),
        input_output_aliases={i: i for i in range(2 * n)},
        compiler_params=pltpu.CompilerParams(has_side_effects=_EFFECT),
    )(*arrays, send_sems, recv_sems, *after)
    return list(outs[n:])


def _landing(src, me, scatter, after=None):
    own = lax.dynamic_index_in_dim(src, me, 0, keepdims=True) if scatter else src[None]
    if after is not None:
        own = own + after.astype(own.dtype)
    shape = src.shape if scatter else (N_DEV,) + src.shape
    return lax.dynamic_update_slice(lax.empty(shape, src.dtype), own, (me, 0, 0))


def _sum_slots(recv, name):
    _, r, n = recv.shape
    tr = r
    for cand in (512, 448, 384, 352, 256, 128, 64, 32, 16, 8):
        if r % cand == 0:
            tr = cand
            break

    def body(r_ref, o_ref):
        acc = r_ref[0].astype(F32)
        for p in range(1, N_DEV):
            acc = acc + r_ref[p].astype(F32)
        o_ref[...] = acc

    return _pcall(
        body, name=name, grid=(r // tr,),
        out_shape=jax.ShapeDtypeStruct((r, n), F32),
        in_specs=[pl.BlockSpec((N_DEV, tr, n), lambda i: (0, i, 0))],
        out_specs=pl.BlockSpec((tr, n), lambda i: (i, 0)),
        compiler_params=_params(1),
    )(recv)


def _row_tile(rows, limit=512):
    for cand in range(min(limit, rows) // 8 * 8, 7, -8):
        if rows % cand == 0:
            return cand
    return rows


def _adamw(w, g, m, v, name):
    r, n = w.shape
    tr = _row_tile(r)

    def body(w_ref, g_ref, m_ref, v_ref, d_ref, nm_ref, nv_ref):
        d_ref[...], nm_ref[...], nv_ref[...] = _adam_update(w_ref[...], g_ref[...], m_ref[...], v_ref[...])

    spec = pl.BlockSpec((tr, n), lambda i: (i, 0))
    shp = jax.ShapeDtypeStruct((r, n), F32)
    return _pcall(
        body, name=name, grid=(r // tr,), out_shape=(shp, shp, shp),
        in_specs=[spec] * 4, out_specs=(spec, spec, spec), compiler_params=_params(1),
    )(w, g, m, v)


def _adam_update(wv, gv, mv, vv):
    bc1 = 1.0 / (1.0 - ADAM_B1 ** ADAM_STEP)
    bc2 = 1.0 / (1.0 - ADAM_B2 ** ADAM_STEP)
    nm = ADAM_B1 * mv + (1.0 - ADAM_B1) * gv
    nv = ADAM_B2 * vv + (1.0 - ADAM_B2) * (gv * gv)
    return -ADAM_LR * ((nm * bc1) / (jnp.sqrt(nv * bc2) + ADAM_EPS) + ADAM_WD * wv), nm, nv


def _adamw_layer(w, g_layer, m, v, outs, idx, name, part=0):
    _, a, b = w.shape
    tr = _row_tile(a)
    slots = g_layer.ndim == 3

    def body(w_ref, g_ref, m_ref, v_ref, o0, o1, o2, o3, go_ref, d_ref, nm_ref, nv_ref):
        if slots:
            gv = g_ref[0].astype(F32)
            for p in range(1, N_DEV):
                gv = gv + g_ref[p].astype(F32)
        else:
            gv = g_ref[...]
        go_ref[...] = gv
        d_ref[...], nm_ref[...], nv_ref[...] = _adam_update(w_ref[...], gv, m_ref[...], v_ref[...])

    layer = pl.BlockSpec((None, tr, b), lambda i: (idx, i, 0))
    steps = a // tr
    g_spec = (pl.BlockSpec((N_DEV, tr, b), lambda i: (0, part * steps + i, 0)) if slots
              else pl.BlockSpec((tr, b), lambda i: (i, 0)))
    anyw = pl.BlockSpec(memory_space=pl.ANY)
    shp = jax.ShapeDtypeStruct(w.shape, F32)
    return tuple(_pcall(
        body, name=name, grid=(steps,), out_shape=(shp,) * 4,
        in_specs=[layer, g_spec, layer, layer, anyw, anyw, anyw, anyw],
        out_specs=(layer,) * 4, input_output_aliases={4: 0, 5: 1, 6: 2, 7: 3},
        compiler_params=_params(1),
    )(w, g_layer, m, v, *outs))


def _adamw_nd(w, g, m, v, name):
    shp = w.shape
    f = lambda a: a.reshape(-1, shp[-1])
    d, nm, nv = _adamw(f(w), f(g), f(m), f(v), name)
    return d.reshape(shp), nm.reshape(shp), nv.reshape(shp)


def _ada_fwd(c_all, ada_w, ada_b_cols):
    n_l, d, cols = ada_w.shape

    def body(c_ref, w_ref, b_ref, o_ref):
        cv = c_ref[...]
        ca = cv * _sigmoid(cv)
        o_ref[...] = _mm(_bf(ca), _bf(w_ref[...])) + b_ref[...]

    return _pcall(
        body, name="ada_fwd", grid=(n_l,),
        out_shape=jax.ShapeDtypeStruct((n_l, N_DEV, cols), F32),
        in_specs=[_full((N_DEV, d)), pl.BlockSpec((None, d, cols), lambda l: (l, 0, 0)),
                  pl.BlockSpec((None, 1, cols), lambda l: (l, 0, 0))],
        out_specs=pl.BlockSpec((None, N_DEV, cols), lambda l: (l, 0, 0)),
        compiler_params=_params(1),
    )(c_all, ada_w, ada_b_cols.reshape(n_l, 1, cols))


def _ada_bwd(c_all_t, dmod_cols):
    d = c_all_t.shape[0]
    n_l, _, cols = dmod_cols.shape

    def body(ct_ref, dm_ref, o_ref):
        cv = ct_ref[...]
        ca = cv * _sigmoid(cv)
        dm = dm_ref[...]
        acc = ca[:, 0:1] * dm[0:1, :]
        for b in range(1, N_DEV):
            acc = acc + ca[:, b:b + 1] * dm[b:b + 1, :]
        o_ref[...] = acc

    return _pcall(
        body, name="ada_bwd", grid=(n_l,),
        out_shape=jax.ShapeDtypeStruct((n_l, d, cols), F32),
        in_specs=[_full((d, N_DEV)), pl.BlockSpec((None, N_DEV, cols), lambda l: (l, 0, 0))],
        out_specs=pl.BlockSpec((None, d, cols), lambda l: (l, 0, 0)),
        compiler_params=_params(1),
    )(c_all_t, dmod_cols)


def _sum_small(gathered):
    _, r, n = gathered.shape

    def body(g_ref, o_ref):
        acc = g_ref[0]
        for p in range(1, N_DEV):
            acc = acc + g_ref[p]
        o_ref[...] = acc

    return _pcall(
        body, name="sum_small", out_shape=jax.ShapeDtypeStruct((r, n), F32),
        in_specs=[_full((N_DEV, r, n))], out_specs=_full((r, n)),
    )(gathered)


def _loss_grad(xf, tgt, ts):
    s, d = xf.shape

    def body(x_ref, t_ref, dx_ref, l_ref):
        @pl.when(pl.program_id(0) == 0)
        def _():
            l_ref[...] = jnp.zeros_like(l_ref)
        e = x_ref[...] - t_ref[...]
        dx_ref[...] = e * (1.0 / d)
        l_ref[...] += (0.5 / d) * jnp.sum(jnp.sum(e * e, axis=1, keepdims=True), axis=0, keepdims=True)

    spec = pl.BlockSpec((ts, d), lambda i: (i, 0))
    return _pcall(
        body, name="loss_grad", grid=(s // ts,),
        out_shape=(jax.ShapeDtypeStruct((s, d), F32), jax.ShapeDtypeStruct((1, 1), F32)),
        in_specs=[spec, spec], out_specs=(spec, _full((1, 1))), compiler_params=_params(1),
    )(xf, tgt)


def _tn_matmul(a, col_block, b, buf, slot, name):
    s = a.shape[0]
    k = b.shape[1]
    n_p = buf.shape[2]
    mcols = N_DEV * n_p
    ts = _row_tile(s, 1024)
    nt = s // ts

    def body(a_ref, b_ref, buf_ref, o_ref, acc):
        i = pl.program_id(0)

        @pl.when(i == 0)
        def _():
            acc[...] = jnp.zeros_like(acc)
        acc[...] += _mm_tn(a_ref[...], b_ref[...])

        @pl.when(i == nt - 1)
        def _():
            o_ref[...] = acc[...].reshape(N_DEV, n_p, k).astype(BF16)

    return _pcall(
        body, name=name, grid=(nt,),
        out_shape=jax.ShapeDtypeStruct(buf.shape, BF16),
        in_specs=[pl.BlockSpec((ts, mcols), lambda i: (i, col_block)),
                  pl.BlockSpec((ts, k), lambda i: (i, 0)),
                  pl.BlockSpec(memory_space=pl.ANY)],
        out_specs=pl.BlockSpec((N_DEV, None, n_p, k), lambda i: (0, slot, 0, 0)),
        scratch_shapes=[pltpu.VMEM((mcols, k), F32)],
        input_output_aliases={2: 0},
        compiler_params=_params(1),
    )(a, b, buf)


def _wspec4(w, slot):
    _, _, n_p, k = w.shape
    return pl.BlockSpec((N_DEV, None, n_p, k), lambda i: (0, slot, 0, 0), pipeline_mode=pl.Buffered(1))


def _ffn_fwd(x1, modp, w352, l, ts):
    s, d = x1.shape
    n_l = w352.shape[1] // 3
    f_dim = N_DEV * w352.shape[2]

    def body(x_ref, mp_ref, wg_ref, wu_ref, wd_ref, x2_ref, f_ref, ab_ref):
        x = x_ref[...]
        _, _, h2 = _normmod_fwd(x, mp_ref[7:8, :], mp_ref[4:5, :], mp_ref[3:4, :])
        hb = _bf(h2)
        f = jnp.zeros((ts, d), F32)
        half_dev, fc = N_DEV // 2, f_dim // 2
        for part in range(2):
            dev0, c0 = part * half_dev, part * fc
            a = _mm_nt(hb, wg_ref[dev0:dev0 + half_dev].reshape(fc, d))
            b = _mm_nt(hb, wu_ref[dev0:dev0 + half_dev].reshape(fc, d))
            ab_ref[:, c0:c0 + fc] = a
            ab_ref[:, f_dim + c0:f_dim + c0 + fc] = b
            sv = (a * _sigmoid(a)) * b
            f = f + _mm(_bf(sv), wd_ref[dev0:dev0 + half_dev].reshape(fc, d))
        f_ref[...] = f
        x2_ref[...] = x + mp_ref[5:6, :] * f

    tile = pl.BlockSpec((ts, d), lambda i: (i, 0))
    shp = jax.ShapeDtypeStruct((s, d), F32)
    return _pcall(
        body, name="ffn_fwd", grid=(s // ts,),
        out_shape=(shp, shp, jax.ShapeDtypeStruct((s, 2 * f_dim), F32)),
        in_specs=[tile, _full(modp.shape), _wspec4(w352, l), _wspec4(w352, n_l + l),
                  _wspec4(w352, 2 * n_l + l)],
        out_specs=(tile, tile, pl.BlockSpec((ts, 2 * f_dim), lambda i: (i, 0))), compiler_params=_params(1),
    )(x1, modp, w352, w352, w352)


def _ffn_bwd(x1, f, ab, dx2, modp, w352, l, ts):
    s, d = x1.shape
    n_l = w352.shape[1] // 3
    f_dim = N_DEV * w352.shape[2]

    def body(x_ref, f_ref, ab_ref, dx2_ref, mp_ref, wg_ref, wu_ref, wd_ref,
             dx1_ref, dab_ref, h2_ref, s_ref, df_ref, sg_ref):
        @pl.when(pl.program_id(0) == 0)
        def _():
            sg_ref[...] = jnp.zeros_like(sg_ref)
        x = x_ref[...]
        gffn, sc2, g2 = mp_ref[7:8, :], mp_ref[4:5, :], mp_ref[5:6, :]
        n, rstd, h2 = _normmod_fwd(x, gffn, sc2, mp_ref[3:4, :])
        hb = _bf(h2)
        dx2 = dx2_ref[...]
        dfb = _bf(g2 * dx2)
        dh2 = jnp.zeros((ts, d), F32)
        half_dev, fc = N_DEV // 2, f_dim // 2
        for part in range(2):
            dev0, c0 = part * half_dev, part * fc
            wg = wg_ref[dev0:dev0 + half_dev].reshape(fc, d)
            wu = wu_ref[dev0:dev0 + half_dev].reshape(fc, d)
            a = ab_ref[:, c0:c0 + fc]
            b = ab_ref[:, f_dim + c0:f_dim + c0 + fc]
            sig = _sigmoid(a)
            sa = a * sig
            s_ref[:, c0:c0 + fc] = _bf(sa * b)
            ds = _mm_nt(dfb, wd_ref[dev0:dev0 + half_dev].reshape(fc, d))
            dab = _bf(ds * b * (sig * (1.0 + a * (1.0 - sig))))
            dbb = _bf(ds * sa)
            dab_ref[:, c0:c0 + fc] = dab
            dab_ref[:, f_dim + c0:f_dim + c0 + fc] = dbb
            dh2 = dh2 + _mm(dab, wg) + _mm(dbb, wu)
        dxn, dsh, dsc, dg = _normmod_bwd(dh2, n, rstd, gffn, sc2)
        dx1_ref[...] = dx2 + dxn
        h2_ref[...] = hb
        df_ref[...] = dfb
        sg_ref[0:1, :] += dsh
        sg_ref[1:2, :] += dsc
        sg_ref[2:3, :] += _sum0(dx2 * f_ref[...])
        sg_ref[3:4, :] += dg

    tile = pl.BlockSpec((ts, d), lambda i: (i, 0))
    f32t = jax.ShapeDtypeStruct((s, d), F32)
    bft = jax.ShapeDtypeStruct((s, d), BF16)
    return _pcall(
        body, name="ffn_bwd", grid=(s // ts,),
        out_shape=(f32t, jax.ShapeDtypeStruct((s, 2 * f_dim), BF16), bft,
                   jax.ShapeDtypeStruct((s, f_dim), BF16), bft, jax.ShapeDtypeStruct((8, d), F32)),
        in_specs=[tile, tile, pl.BlockSpec((ts, 2 * f_dim), lambda i: (i, 0)), tile, _full(modp.shape),
                  _wspec4(w352, l), _wspec4(w352, n_l + l), _wspec4(w352, 2 * n_l + l)],
        out_specs=(tile, pl.BlockSpec((ts, 2 * f_dim), lambda i: (i, 0)), tile,
                   pl.BlockSpec((ts, f_dim), lambda i: (i, 0)), tile, _full((8, d))),
        compiler_params=_params(1),
    )(x1, f, ab, dx2, modp, w352, w352, w352)


def _retention_consts(ts):
    h = np.arange(RET_HEADS, dtype=np.float64)
    log_g = np.log1p(-np.exp2(-5.0 - h))
    t = np.arange(ts)
    diff = t[:, None] - t[None, :]
    same = (t[:, None] // CHUNK) == (t[None, :] // CHUNK)
    later = (t[:, None] // CHUNK) > (t[None, :] // CHUNK)
    dm = np.where(same, np.abs(diff), np.where(later, diff, 0))[None] * log_g[:, None, None]
    dm = np.where((same | later)[None], np.exp(dm), 0.0)
    qd = np.exp((t[:, None] + 1.0) * log_g[None, :])
    kd = np.exp((ts - 1.0 - t[:, None]) * log_g[None, :])
    qd = np.repeat(qd, HEAD, axis=1)
    kd = np.repeat(kd, HEAD, axis=1)
    tdec = [float(np.exp(ts * lg)) for lg in log_g]
    return (jnp.asarray(dm, F32), jnp.asarray(qd, F32), jnp.asarray(kd, F32), tdec)


def _rope_tables(s):
    inv_freq = 1.0 / (ROPE_THETA ** (jnp.arange(0, HEAD, 2, dtype=F32) / HEAD))
    ang = jnp.arange(s, dtype=F32)[:, None] * inv_freq[None, :]
    cos, sin = jnp.cos(ang), jnp.sin(ang)
    return jnp.concatenate([cos, cos], axis=1), jnp.concatenate([-sin, sin], axis=1)


def _rope(v, cos, sin):
    return v * cos + pltpu.roll(v, HEAD // 2, 1) * sin


def _rope_t(dv, cos, sin):
    return dv * cos + pltpu.roll(dv * sin, HEAD // 2, 1)


def _shift_down(z, k, halo_ref):
    r = pltpu.roll(z, k, 0)
    rows = lax.broadcasted_iota(jnp.int32, z.shape, 0)
    for j in range(k):
        r = jnp.where(rows == j, halo_ref[8 - k + j:8 - k + j + 1, :], r)
    return r


def _shift_up(z, k, halo_ref):
    n = z.shape[0]
    r = pltpu.roll(z, n - k, 0)
    rows = lax.broadcasted_iota(jnp.int32, z.shape, 0)
    for j in range(k):
        r = jnp.where(rows == n - k + j, halo_ref[j:j + 1, :], r)
    return r


def _even_recompute(x, mp_ref, win, cw_ref, cos, sin, dm_ref, qd_ref, kd_ref, halo_ref, state_of, proj=None):
    cd = 4 * HEAD
    n, rstd, h = _normmod_fwd(x, mp_ref[6:7, :], mp_ref[1:2, :], mp_ref[0:1, :])
    if proj is None:
        proj = _mm_nt(_bf(h), win)
    bg, cg, u = proj[:, 0:cd], proj[:, cd:2 * cd], proj[:, 2 * cd:3 * cd]
    z = cg * u
    z1 = _shift_down(z, 1, halo_ref)
    z2 = _shift_down(z, 2, halo_ref)
    conv = cw_ref[0:1, :] * z2 + cw_ref[1:2, :] * z1 + cw_ref[2:3, :] * z
    heads = []
    scale = HEAD ** -0.5
    for hh in range(RET_HEADS):
        lo = hh * HEAD
        q = proj[:, 3 * cd + lo:3 * cd + lo + HEAD]
        k = proj[:, 4 * cd + lo:4 * cd + lo + HEAD]
        v = proj[:, 5 * cd + lo:5 * cd + lo + HEAD]
        gate = proj[:, 6 * cd + lo:6 * cd + lo + HEAD]
        qr = _rope(q, cos, sin)
        kr = _rope(k, cos, sin) * scale
        sc = _mm_nt(_bf(qr), _bf(kr)) * dm_ref[hh]
        qs = qr * qd_ref[:, lo:lo + HEAD]
        ks = kr * kd_ref[:, lo:lo + HEAD]
        o = _mm(_bf(sc), _bf(v)) + _mm(_bf(qs), _bf(state_of(hh)))
        on, orstd = _rms_fwd(o)
        sig = _sigmoid(gate)
        heads.append(dict(qr=qr, kr=kr, v=v, gate=gate, sc=sc, qs=qs, ks=ks, on=on, orstd=orstd, sig=sig))
    return dict(n=n, rstd=rstd, h=h, proj=proj, bg=bg, cg=cg, u=u, z=z, z1=z1, z2=z2, conv=conv, heads=heads)


def _even_fwd(x, modp, w448, w128, l, cw, cos, sin, consts, ts):
    s, d = x.shape
    nt = s // ts
    dm, qd, kd, tdec = consts
    cd = 4 * HEAD
    e_in = N_DEV * w448.shape[2]

    def body(x_ref, mp_ref, win_ref, cw_ref, cos_ref, sin_ref, dm_ref, qd_ref, kd_ref, wout_ref,
             x1_ref, y_ref, st_ref, zh_ref, proj_ref, state, halo):
        @pl.when(pl.program_id(0) == 0)
        def _():
            state[...] = jnp.zeros_like(state)
            halo[...] = jnp.zeros_like(halo)
        xv = x_ref[...]
        st_ref[...] = state[...]
        zh_ref[...] = halo[...]
        r = _even_recompute(xv, mp_ref, win_ref[...].reshape(e_in, d), cw_ref, cos_ref[...], sin_ref[...],
                            dm_ref, qd_ref, kd_ref, halo, lambda hh: state[hh])
        proj_ref[...] = r["proj"]
        halo[...] = r["z"][ts - 8:ts, :]
        parts = [r["bg"] * r["conv"]]
        for hh, hd in enumerate(r["heads"]):
            state[hh] = state[hh] * tdec[hh] + _mm_tn(_bf(hd["ks"]), _bf(hd["v"]))
            rg = cw_ref[3:4, hh * HEAD:(hh + 1) * HEAD]
            parts.append((hd["gate"] * hd["sig"]) * (hd["on"] * rg))
        mcat = jnp.concatenate(parts, axis=1)
        y = _mm(_bf(mcat), wout_ref[...].reshape(d, d))
        y_ref[...] = y
        x1_ref[...] = xv + mp_ref[2:3, :] * y

    tile = pl.BlockSpec((ts, d), lambda i: (i, 0))
    rt = pl.BlockSpec((ts, HEAD), lambda i: (i, 0))
    shp = jax.ShapeDtypeStruct((s, d), F32)
    return _pcall(
        body, name="even_fwd", grid=(nt,),
        out_shape=(shp, shp, jax.ShapeDtypeStruct((nt, RET_HEADS, HEAD, HEAD), F32),
                   jax.ShapeDtypeStruct((nt, 8, cd), F32), jax.ShapeDtypeStruct((s, e_in), F32)),
        in_specs=[tile, _full(modp.shape), _wspec4(w448, l), _full(cw.shape), rt, rt,
                  _full(dm.shape), _full(qd.shape), _full(kd.shape), _wspec4(w128, l)],
        out_specs=(tile, tile, pl.BlockSpec((None, RET_HEADS, HEAD, HEAD), lambda i: (i, 0, 0, 0)),
                   pl.BlockSpec((None, 8, cd), lambda i: (i, 0, 0)), pl.BlockSpec((ts, e_in), lambda i: (i, 0))),
        scratch_shapes=[pltpu.VMEM((RET_HEADS, HEAD, HEAD), F32), pltpu.VMEM((8, cd), F32)],
        compiler_params=_params(1),
    )(x, modp, w448, cw, cos, sin, dm, qd, kd, w128)


def _even_bwd(x, dx1, y, states, zhalo, proj, modp, w448, w128, l, cw, cos, sin, consts, ts):
    s, d = x.shape
    nt = s // ts
    dm, qd, kd, tdec = consts
    cd = 4 * HEAD
    e_in = N_DEV * w448.shape[2]
    scale = HEAD ** -0.5

    def body(x_ref, dx1_ref, y_ref, st_ref, zh_ref, proj_ref, mp_ref, win_ref, cw_ref, cos_ref, sin_ref,
             dm_ref, qd_ref, kd_ref, wout_ref,
             dx_ref, dproj_ref, h_ref, m_ref, dy_ref, sg_ref, gstate, halo_d):
        @pl.when(pl.program_id(0) == 0)
        def _():
            gstate[...] = jnp.zeros_like(gstate)
            halo_d[...] = jnp.zeros_like(halo_d)
            sg_ref[...] = jnp.zeros_like(sg_ref)
        xv = x_ref[...]
        cos, sin = cos_ref[...], sin_ref[...]
        win = win_ref[...].reshape(e_in, d)
        r = _even_recompute(xv, mp_ref, win, cw_ref, cos, sin, dm_ref, qd_ref, kd_ref, zh_ref,
                            lambda hh: st_ref[hh], proj_ref[...])
        parts = [r["bg"] * r["conv"]]
        for hh, hd in enumerate(r["heads"]):
            rg = cw_ref[3:4, hh * HEAD:(hh + 1) * HEAD]
            parts.append((hd["gate"] * hd["sig"]) * (hd["on"] * rg))
        m_ref[...] = _bf(jnp.concatenate(parts, axis=1))
        h_ref[...] = _bf(r["h"])

        dx1 = dx1_ref[...]
        dy = mp_ref[2:3, :] * dx1
        dyb = _bf(dy)
        dy_ref[...] = dyb
        sg_ref[2:3, :] += _sum0(dx1 * y_ref[...])
        dmix = _mm_nt(dyb, wout_ref[...].reshape(d, d))

        da_out = dmix[:, 0:cd]
        dbg = da_out * r["conv"]
        dconv = da_out * r["bg"]
        dc1 = _shift_up(dconv, 1, halo_d)
        dc2 = _shift_up(dconv, 2, halo_d)
        dz = cw_ref[2:3, :] * dconv + cw_ref[1:2, :] * dc1 + cw_ref[0:1, :] * dc2
        halo_d[...] = dconv[0:8, :]
        sg_ref[4:5, 0:cd] += _sum0(dconv * r["z2"])
        sg_ref[5:6, 0:cd] += _sum0(dconv * r["z1"])
        sg_ref[6:7, 0:cd] += _sum0(dconv * r["z"])
        dcg = dz * r["u"]
        du = dz * r["cg"]

        dqs, dks, dvs, dgs = [], [], [], []
        for hh, hd in enumerate(r["heads"]):
            lo = hh * HEAD
            rg = cw_ref[3:4, lo:lo + HEAD]
            dr = dmix[:, cd + lo:cd + lo + HEAD]
            sig, gate, on = hd["sig"], hd["gate"], hd["on"]
            rn = on * rg
            dgate = dr * rn * (sig * (1.0 + gate * (1.0 - sig)))
            drn = dr * (gate * sig)
            sg_ref[7:8, lo:lo + HEAD] += _sum0(drn * on)
            do = _rms_bwd(drn * rg, on, hd["orstd"])
            dob = _bf(do)
            gst = _bf(gstate[hh])
            scb = _bf(hd["sc"])
            vb = _bf(hd["v"])
            qrb, krb = _bf(hd["qr"]), _bf(hd["kr"])
            dv = _mm_tn(scb, dob) + _mm(_bf(hd["ks"]), gst)
            dsc = _bf(_mm_nt(dob, vb) * dm_ref[hh])
            dqr = _mm(dsc, krb) + _mm_nt(dob, _bf(st_ref[hh])) * qd_ref[:, lo:lo + HEAD]
            dkr = _mm_tn(dsc, qrb) + _mm_nt(vb, gst) * kd_ref[:, lo:lo + HEAD]
            gstate[hh] = gstate[hh] * tdec[hh] + _mm_tn(_bf(hd["qs"]), dob)
            dqs.append(_rope_t(dqr, cos, sin))
            dks.append(_rope_t(dkr * scale, cos, sin))
            dvs.append(dv)
            dgs.append(dgate)

        dproj = _bf(jnp.concatenate([dbg, dcg, du] + dqs + dks + dvs + dgs, axis=1))
        dproj_ref[...] = dproj
        dh = _mm(dproj, win)
        dxn, dsh, dsc1, dg = _normmod_bwd(dh, r["n"], r["rstd"], mp_ref[6:7, :], mp_ref[1:2, :])
        dx_ref[...] = dx1 + dxn
        sg_ref[0:1, :] += dsh
        sg_ref[1:2, :] += dsc1
        sg_ref[3:4, :] += dg

    rev = lambda i: (nt - 1 - i, 0)
    tile = pl.BlockSpec((ts, d), rev)
    rt = pl.BlockSpec((ts, HEAD), rev)
    bft = jax.ShapeDtypeStruct((s, d), BF16)
    return _pcall(
        body, name="even_bwd", grid=(nt,),
        out_shape=(jax.ShapeDtypeStruct((s, d), F32), jax.ShapeDtypeStruct((s, e_in), BF16), bft, bft, bft,
                   jax.ShapeDtypeStruct((8, d), F32)),
        in_specs=[tile, tile, tile,
                  pl.BlockSpec((None, RET_HEADS, HEAD, HEAD), lambda i: (nt - 1 - i, 0, 0, 0)),
                  pl.BlockSpec((None, 8, cd), lambda i: (nt - 1 - i, 0, 0)), pl.BlockSpec((ts, e_in), rev),
                  _full(modp.shape), _wspec4(w448, l), _full(cw.shape), rt, rt,
                  _full(dm.shape), _full(qd.shape), _full(kd.shape), _wspec4(w128, l)],
        out_specs=(tile, pl.BlockSpec((ts, e_in), rev), tile, tile, tile, _full((8, d))),
        scratch_shapes=[pltpu.VMEM((RET_HEADS, HEAD, HEAD), F32), pltpu.VMEM((8, cd), F32)],
        compiler_params=_params(1),
    )(x, dx1, y, states, zhalo, proj, modp, w448, cw, cos, sin, dm, qd, kd, w128)


def _odd_qkv_fwd(x, modp, w384, j, qkg, ts):
    s, d = x.shape
    n3 = N_DEV * w384.shape[2]

    def body(x_ref, mp_ref, w_ref, g_ref, o_ref, pre_ref):
        _, _, h = _normmod_fwd(x_ref[...], mp_ref[6:7, :], mp_ref[1:2, :], mp_ref[0:1, :])
        qkv = _mm_nt(_bf(h), w_ref[...].reshape(n3, d))
        pre_ref[...] = qkv[:, 0:2 * d]
        for hh in range(SB_HEADS):
            lo = hh * HEAD
            qn, _ = _rms_fwd(qkv[:, lo:lo + HEAD])
            kn, _ = _rms_fwd(qkv[:, d + lo:d + lo + HEAD])
            o_ref[:, lo:lo + HEAD] = _bf(qn * g_ref[0:1, :])
            o_ref[:, d + lo:d + lo + HEAD] = _bf(kn * g_ref[1:2, :])
        o_ref[:, 2 * d:3 * d] = _bf(qkv[:, 2 * d:3 * d])

    return _pcall(
        body, name="odd_qkv_fwd", grid=(s // ts,),
        out_shape=(jax.ShapeDtypeStruct((s, n3), BF16), jax.ShapeDtypeStruct((s, 2 * d), F32)),
        in_specs=[pl.BlockSpec((ts, d), lambda i: (i, 0)), _full(modp.shape), _wspec4(w384, j),
                  _full(qkg.shape)],
        out_specs=(pl.BlockSpec((ts, n3), lambda i: (i, 0)), pl.BlockSpec((ts, 2 * d), lambda i: (i, 0))),
        compiler_params=_params(1),
    )(x, modp, w384, qkg)


SB_QUERIES = 512
SB_WIDE = 256
SB_LOOP_BLOCKS = 2


def _sb_logits(q, kw, mask):
    z = _mm_nt(q, kw) * (HEAD ** -0.5)
    e = jnp.exp(-jnp.abs(z))
    lb = jnp.minimum(z, 0.0) - jnp.log(1.0 + e)
    lk = lb - z
    if mask is not None:
        lk = jnp.where(mask, lk, 0.0)
    return lb, lk


def _tri(n, above):
    ri = lax.broadcasted_iota(jnp.int32, (n, n), 0)
    ci = lax.broadcasted_iota(jnp.int32, (n, n), 1)
    return ((ri > ci) if above else (ri < ci)).astype(BF16)


def _split_dot(a, tri):
    hi = _bf(a)
    lo = _bf(a - hi.astype(F32))
    return _mm(hi, tri) + _mm(lo, tri)


def _sb_fwd(qkv, tq):
    s = qkv.shape[0]
    d = qkv.shape[1] // 3
    nq = s // tq
    assert tq % SB_WIDE == 0
    parts = tq // SB_WIDE

    def body(q_ref, k_ref, v_ref, o_ref, t_ref, o_acc, run):
        qi = pl.program_id(1)
        base = qi * tq
        upper = _tri(SB_WIDE, True)
        o_acc[...] = jnp.zeros_like(o_acc)
        run[...] = jnp.zeros_like(run)

        def wide_step(ks, row0, masked, nblk):
            rows = slice(row0, tq)
            width = nblk * SB_WIDE
            mask = None
            if masked:
                qpos = base + row0 + lax.broadcasted_iota(jnp.int32, (tq - row0, width), 0)
                mask = qpos > ks + lax.broadcasted_iota(jnp.int32, (tq - row0, width), 1)
            lb, lk = _sb_logits(q_ref[rows, :], k_ref[pl.ds(ks, width), :], mask)
            blocks = [lk[:, b * SB_WIDE:(b + 1) * SB_WIDE] for b in range(nblk)]
            right = run[rows, :]
            accs = [None] * nblk
            for b in reversed(range(nblk)):
                accs[b] = _split_dot(blocks[b], upper) + right
                right = right + jnp.sum(blocks[b], axis=1, keepdims=True)
            w = jnp.exp(lb + (accs[0] if nblk == 1 else jnp.concatenate(accs, axis=1)))
            if masked:
                w = jnp.where(mask, w, 0.0)
            o_acc[rows, :] += _mm(_bf(w), v_ref[pl.ds(ks, width), :])
            run[rows, :] = right

        for part in reversed(range(parts)):
            wide_step(pl.multiple_of(base + part * SB_WIDE, SB_WIDE), part * SB_WIDE, True, 1)
        loop_width = SB_LOOP_BLOCKS * SB_WIDE
        nsteps = qi * (tq // loop_width)

        def step(it, carry):
            wide_step(pl.multiple_of((nsteps - 1 - it) * loop_width, loop_width), 0, False, SB_LOOP_BLOCKS)
            return carry

        lax.fori_loop(0, nsteps, step, 0)
        o_ref[...] = _bf(o_acc[...])
        t_ref[...] = run[...]

    nh = d // HEAD
    return _pcall(
        body, name="sb_fwd", grid=(nh, nq),
        out_shape=(jax.ShapeDtypeStruct((s, d), BF16), jax.ShapeDtypeStruct((nh, s, 1), F32)),
        in_specs=[pl.BlockSpec((tq, HEAD), lambda h, i: (i, h)),
                  pl.BlockSpec((s, HEAD), lambda h, i: (0, nh + h)),
                  pl.BlockSpec((s, HEAD), lambda h, i: (0, 2 * nh + h))],
        out_specs=(pl.BlockSpec((tq, HEAD), lambda h, i: (i, h)),
                   pl.BlockSpec((None, tq, 1), lambda h, i: (h, i, 0))),
        scratch_shapes=[pltpu.VMEM((tq, HEAD), F32), pltpu.VMEM((tq, 1), F32)],
        compiler_params=_params(2),
    )(qkv, qkv, qkv)


def _sb_bwd(qkv, do, tot, tq):
    s = qkv.shape[0]
    d = qkv.shape[1] // 3
    nq = s // tq
    scale = HEAD ** -0.5
    assert tq % SB_WIDE == 0
    parts = tq // SB_WIDE

    def body(q_ref, k_ref, v_ref, do_ref, t_ref, dq_ref, dk_ref, dv_ref, pk, pd):
        qi = pl.program_id(1)

        @pl.when(qi == 0)
        def _():
            dk_ref[...] = jnp.zeros_like(dk_ref)
            dv_ref[...] = jnp.zeros_like(dv_ref)
        base = qi * tq
        upper = _tri(SB_WIDE, True)
        lower = _tri(SB_WIDE, False)
        dq_ref[...] = jnp.zeros_like(dq_ref)
        pk[...] = jnp.zeros_like(pk)
        pd[...] = jnp.zeros_like(pd)

        def wide_step(ks, row0, masked, nblk):
            rows = slice(row0, tq)
            width = nblk * SB_WIDE
            cut = lambda a: [a[:, b * SB_WIDE:(b + 1) * SB_WIDE] for b in range(nblk)]
            join = lambda parts_: parts_[0] if nblk == 1 else jnp.concatenate(parts_, axis=1)
            mask = None
            if masked:
                qpos = base + row0 + lax.broadcasted_iota(jnp.int32, (tq - row0, width), 0)
                mask = qpos > ks + lax.broadcasted_iota(jnp.int32, (tq - row0, width), 1)
            kw = k_ref[pl.ds(ks, width), :]
            lb, lk = _sb_logits(q_ref[rows, :], kw, mask)
            left = pk[rows, :]
            total = t_ref[rows, :]
            accs = []
            for blk in cut(lk):
                left = left + jnp.sum(blk, axis=1, keepdims=True)
                accs.append(_split_dot(blk, upper) + (total - left))
            pk[rows, :] = left
            w = jnp.exp(lb + join(accs))
            if masked:
                w = jnp.where(mask, w, 0.0)
            de = _mm_nt(do_ref[rows, :], v_ref[pl.ds(ks, width), :]) * w
            before = pd[rows, :]
            dlks = []
            for blk in cut(de):
                dlks.append(_split_dot(blk, lower) + before)
                before = before + jnp.sum(blk, axis=1, keepdims=True)
            pd[rows, :] = before
            dz = (de - jnp.exp(lb) * (de + join(dlks))) * scale
            if masked:
                dz = jnp.where(mask, dz, 0.0)
            dzb = _bf(dz)
            dq_ref[rows, :] += _mm(dzb, kw)
            dv_ref[pl.ds(ks, width), :] += _mm_tn(_bf(w), do_ref[rows, :])
            dk_ref[pl.ds(ks, width), :] += _mm_tn(dzb, q_ref[rows, :])

        loop_width = SB_LOOP_BLOCKS * SB_WIDE

        def step(jb, carry):
            wide_step(pl.multiple_of(jb * loop_width, loop_width), 0, False, SB_LOOP_BLOCKS)
            return carry

        lax.fori_loop(0, qi * (tq // loop_width), step, 0)
        for part in range(parts):
            wide_step(pl.multiple_of(base + part * SB_WIDE, SB_WIDE), part * SB_WIDE, True, 1)

    nh = d // HEAD
    shp = jax.ShapeDtypeStruct((s, d), F32)
    return _pcall(
        body, name="sb_bwd", grid=(nh, nq), out_shape=(shp, shp, shp),
        in_specs=[pl.BlockSpec((tq, HEAD), lambda h, i: (i, h)),
                  pl.BlockSpec((s, HEAD), lambda h, i: (0, nh + h)),
                  pl.BlockSpec((s, HEAD), lambda h, i: (0, 2 * nh + h)),
                  pl.BlockSpec((tq, HEAD), lambda h, i: (i, h)),
                  pl.BlockSpec((None, tq, 1), lambda h, i: (h, i, 0))],
        out_specs=(pl.BlockSpec((tq, HEAD), lambda h, i: (i, h)),
                   pl.BlockSpec((s, HEAD), lambda h, i: (0, h)),
                   pl.BlockSpec((s, HEAD), lambda h, i: (0, h))),
        scratch_shapes=[pltpu.VMEM((tq, 1), F32), pltpu.VMEM((tq, 1), F32)],
        compiler_params=_params(2),
    )(qkv, qkv, qkv, do, tot)


def _odd_out_fwd(o, x, modp, w128, slot, ts):
    s, d = x.shape

    def body(o_ref, x_ref, mp_ref, w_ref, x1_ref, y_ref):
        y = _mm(o_ref[...], w_ref[...].reshape(d, d))
        y_ref[...] = y
        x1_ref[...] = x_ref[...] + mp_ref[2:3, :] * y

    tile = pl.BlockSpec((ts, d), lambda i: (i, 0))
    shp = jax.ShapeDtypeStruct((s, d), F32)
    return _pcall(
        body, name="odd_out_fwd", grid=(s // ts,), out_shape=(shp, shp),
        in_specs=[tile, tile, _full(modp.shape), _wspec4(w128, slot)],
        out_specs=(tile, tile), compiler_params=_params(1),
    )(o, x, modp, w128)


def _odd_out_bwd(dx1, y, modp, w128, slot, ts):
    s, d = dx1.shape

    def body(dx1_ref, y_ref, mp_ref, w_ref, do_ref, dy_ref, sg_ref):
        @pl.when(pl.program_id(0) == 0)
        def _():
            sg_ref[...] = jnp.zeros_like(sg_ref)
        dx1v = dx1_ref[...]
        dyb = _bf(mp_ref[2:3, :] * dx1v)
        dy_ref[...] = dyb
        do_ref[...] = _bf(_mm_nt(dyb, w_ref[...].reshape(d, d)))
        sg_ref[2:3, :] += _sum0(dx1v * y_ref[...])

    tile = pl.BlockSpec((ts, d), lambda i: (i, 0))
    bft = jax.ShapeDtypeStruct((s, d), BF16)
    return _pcall(
        body, name="odd_out_bwd", grid=(s // ts,),
        out_shape=(bft, bft, jax.ShapeDtypeStruct((8, d), F32)),
        in_specs=[tile, tile, _full(modp.shape), _wspec4(w128, slot)],
        out_specs=(tile, tile, _full((8, d))), compiler_params=_params(1),
    )(dx1, y, modp, w128)


def _odd_qkv_bwd(x, dx1, dq, dk, dv, pre, sg_in, modp, w384, j, qkg, ts):
    s, d = x.shape
    n3 = N_DEV * w384.shape[2]

    def body(x_ref, dx1_ref, dq_ref, dk_ref, dv_ref, pre_ref, sgi_ref, mp_ref, w_ref, g_ref,
             dx_ref, dqkv_ref, h_ref, sg_ref):
        @pl.when(pl.program_id(0) == 0)
        def _():
            sg_ref[...] = sgi_ref[...]
        gmix, sc1 = mp_ref[6:7, :], mp_ref[1:2, :]
        n, rstd, h = _normmod_fwd(x_ref[...], gmix, sc1, mp_ref[0:1, :])
        hb = _bf(h)
        h_ref[...] = hb
        w = w_ref[...].reshape(n3, d)
        qkv = pre_ref[...]
        parts_q, parts_k = [], []
        gq, gk = g_ref[0:1, :], g_ref[1:2, :]
        dgq = jnp.zeros((1, HEAD), F32)
        dgk = jnp.zeros((1, HEAD), F32)
        for hh in range(SB_HEADS):
            lo = hh * HEAD
            qn, qr = _rms_fwd(qkv[:, lo:lo + HEAD])
            kn, kr = _rms_fwd(qkv[:, d + lo:d + lo + HEAD])
            dqn = dq_ref[:, lo:lo + HEAD]
            dkn = dk_ref[:, lo:lo + HEAD]
            dgq = dgq + _sum0(dqn * qn)
            dgk = dgk + _sum0(dkn * kn)
            parts_q.append(_rms_bwd(dqn * gq, qn, qr))
            parts_k.append(_rms_bwd(dkn * gk, kn, kr))
        dqkv = _bf(jnp.concatenate(parts_q + parts_k + [dv_ref[...]], axis=1))
        dqkv_ref[...] = dqkv
        dh = _mm(dqkv, w)
        dxn, dsh, dsc, dg = _normmod_bwd(dh, n, rstd, gmix, sc1)
        dx_ref[...] = dx1_ref[...] + dxn
        sg_ref[0:1, :] += dsh
        sg_ref[1:2, :] += dsc
        sg_ref[3:4, :] += dg
        sg_ref[4:5, 0:HEAD] += dgq
        sg_ref[5:6, 0:HEAD] += dgk

    tile = pl.BlockSpec((ts, d), lambda i: (i, 0))
    return _pcall(
        body, name="odd_qkv_bwd", grid=(s // ts,),
        out_shape=(jax.ShapeDtypeStruct((s, d), F32), jax.ShapeDtypeStruct((s, n3), BF16),
                   jax.ShapeDtypeStruct((s, d), BF16), jax.ShapeDtypeStruct((8, d), F32)),
        in_specs=[tile, tile, tile, tile, tile, pl.BlockSpec((ts, 2 * d), lambda i: (i, 0)), _full((8, d)),
                  _full(modp.shape), _wspec4(w384, j), _full(qkg.shape)],
        out_specs=(tile, pl.BlockSpec((ts, n3), lambda i: (i, 0)), tile, _full((8, d))),
        compiler_params=_params(1),
    )(x, dx1, dq, dk, dv, pre, sg_in, modp, w384, qkg)


def _pad_rows(a, rows):
    return jnp.concatenate([a, jnp.zeros((rows - a.shape[0],) + a.shape[1:], a.dtype)], axis=0)


def kernel(x, c, ada_w, ada_b, norm_mix_g, norm_ffn_g, ev_w_in, ev_conv_w, ev_ret_norm_g, ev_w_out, od_w_qkv, od_q_norm_g, od_k_norm_g, od_w_out, ffn_w_gate, ffn_w_up, ffn_w_down, loss_target, m_ada_w, m_ada_b, m_norm_mix_g, m_norm_ffn_g, m_ev_w_in, m_ev_conv_w, m_ev_ret_norm_g, m_ev_w_out, m_od_w_qkv, m_od_q_norm_g, m_od_k_norm_g, m_od_w_out, m_ffn_w_gate, m_ffn_w_up, m_ffn_w_down, v_ada_w, v_ada_b, v_norm_mix_g, v_norm_ffn_g, v_ev_w_in, v_ev_conv_w, v_ev_ret_norm_g, v_ev_w_out, v_od_w_qkv, v_od_q_norm_g, v_od_k_norm_g, v_od_w_out, v_ffn_w_gate, v_ffn_w_up, v_ffn_w_down):
    me = 4 * lax.axis_index("x") + 2 * lax.axis_index("y") + lax.axis_index("c")
    xs = x[0]
    tgt = loss_target[0]
    s, d = xs.shape
    depth = ada_w.shape[0]
    n_even, n_odd = ev_w_in.shape[0], od_w_qkv.shape[0]
    ts = 256
    tq = SB_QUERIES
    cd = 4 * HEAD
    cc = ev_conv_w.shape[2]

    blocks = [_bf(ev_w_in[0]).T, _bf(ev_w_out[0])]
    first_mixer, started = _exchange_start(blocks, [_landing(b, me, False) for b in blocks], False,
                                           "gather_start_mixer_0")
    pack0 = jnp.zeros((8, d), F32).at[0].set(c[0]).at[1, :n_even * 3 * cc].set(ev_conv_w.reshape(-1)) + started
    got0, _ = _all_gather(pack0, "gather_cond")
    got0 = got0.reshape(N_DEV, 8, d)
    c_all = got0[:, 0, :]
    conv_all = got0[:, 1, :n_even * 3 * cc].reshape(N_DEV, n_even, 3, cc).transpose(1, 2, 0, 3)
    conv_all = conv_all.reshape(n_even, 3, N_DEV * cc)
    cols = ada_w.shape[2]
    ada_b_cols = lax.dynamic_slice(ada_b, (0, me * cols), (depth, cols))
    mod_cols = _ada_fwd(c_all, ada_w, ada_b_cols)
    got1, cond_done = _all_gather(mod_cols.reshape(depth * N_DEV, cols), "gather_mod")
    got1 = got1.reshape(N_DEV, depth, N_DEV, cols)
    mod = lax.dynamic_index_in_dim(got1, me, axis=2, keepdims=False)
    mod = mod.transpose(1, 0, 2).reshape(depth, 6, d)
    modps = [jnp.concatenate([mod[l], norm_mix_g[l][None], norm_ffn_g[l][None]], axis=0) for l in range(depth)]

    in_flight = []
    started = cond_done
    for l in range(depth):
        j = l // 2
        plain = lambda w: _bf(w + started)
        tr = lambda w: plain(w).T
        blocks = [tr(ev_w_in[j]), plain(ev_w_out[j])] if l % 2 == 0 else [tr(od_w_qkv[j]), plain(od_w_out[j])]
        if l == 0:
            mixer = first_mixer
        else:
            mixer, started = _exchange_start(blocks, [_landing(b, me, False) for b in blocks], False,
                                             f"gather_start_mixer_{l}")
        blocks = [jnp.concatenate([tr(ffn_w_gate[l]), tr(ffn_w_up[l]), plain(ffn_w_down[l])], axis=0)]
        ffn, started = _exchange_start(blocks, [_landing(b, me, False) for b in blocks], False,
                                       f"gather_start_ffn_{l}")
        in_flight.append((mixer, ffn))
        modps[0] = modps[0] + started
    n_ffn = ffn_w_down.shape[1]

    def mixer_weights(l, after):
        got = _exchange_wait(*in_flight[l][0], False, [after], f"gather_wait_mixer_{l}")
        return got[0].reshape(N_DEV, 1, -1, d), got[1].reshape(N_DEV, 1, -1, d)

    def ffn_weights(l, after):
        got = _exchange_wait(*in_flight[l][1], False, [after], f"gather_wait_ffn_{l}")
        return got[0].reshape(N_DEV, 3, n_ffn, d)

    cos, sin = _rope_tables(s)
    consts = _retention_consts(ts)
    cws = [_pad_rows(jnp.concatenate([conv_all[j], ev_ret_norm_g[j][None]], axis=0), 8) for j in range(n_even)]
    qkgs = [_pad_rows(jnp.stack([od_q_norm_g[j], od_k_norm_g[j]]), 8) for j in range(n_odd)]

    saved = []
    weights = []
    cur = xs
    for l in range(depth):
        j = l // 2
        w_in, w_out = mixer_weights(l, cur)
        if l % 2 == 0:
            x1, y, states, zhalo, proj = _even_fwd(cur, modps[l], w_in, w_out, 0, cws[j], cos, sin, consts, ts)
            mix = (states, zhalo, proj)
        else:
            qkv, pre = _odd_qkv_fwd(cur, modps[l], w_in, 0, qkgs[j], ts)
            o, tot = _sb_fwd(qkv, tq)
            x1, y = _odd_out_fwd(o, cur, modps[l], w_out, 0, ts)
            mix = (qkv, o, tot, pre)
        w_ffn = ffn_weights(l, x1)
        weights.append((w_in, w_out, w_ffn))
        x2, f, ab = _ffn_fwd(x1, modps[l], w_ffn, 0, ts)
        saved.append((cur, x1, y, (f, ab), mix))
        cur = x2

    dx, loss_part = _loss_grad(cur, tgt, ts)
    loss = lax.psum(loss_part[0, 0], ("x", "y", "c"))

    dmod = [None] * depth
    d_gmix = [None] * depth
    d_gffn = [None] * depth
    d_conv = [None] * n_even
    d_retg = [None] * n_even
    d_qg = [None] * n_odd
    d_kg = [None] * n_odd
    grads_in_flight = [None] * depth
    for l in reversed(range(depth)):
        j = l // 2
        x0, x1, y, (f, ab), mix = saved[l]
        w_in, w_out, w_ffn = weights[l]
        g_ffn = lax.empty(w_ffn.shape, BF16)
        g_in = lax.empty(w_in.shape, BF16)
        g_out = lax.empty(w_out.shape, BF16)
        dx1, dab, h2, sv, df, sg2 = _ffn_bwd(x1, f, ab, dx, modps[l], w_ffn, 0, ts)
        g_ffn = _tn_matmul(dab, 0, h2, g_ffn, 0, "tn_gate")
        g_ffn = _tn_matmul(dab, 1, h2, g_ffn, 1, "tn_up")
        g_ffn = _tn_matmul(sv, 0, df, g_ffn, 2, "tn_down")
        if l == 0:
            pieces = [g_ffn.reshape(N_DEV, -1, d)]
            last_ffn_flight, started = _exchange_start(pieces, [_landing(p, me, True) for p in pieces], True,
                                                       "grads_start_ffn_0")
            modps[0] = modps[0] + started
        if l % 2 == 0:
            states, zhalo, proj = mix
            dx, dproj, hb, mb, dyb, sg1 = _even_bwd(x0, dx1, y, states, zhalo, proj, modps[l], w_in, w_out, 0,
                                                    cws[j], cos, sin, consts, ts)
            g_in = _tn_matmul(dproj, 0, hb, g_in, 0, "tn_ev_in")
            g_out = _tn_matmul(mb, 0, dyb, g_out, 0, "tn_ev_out")
            d_conv[j] = sg1[4:7, :cd]
            d_retg[j] = sg1[7, :cd]
        else:
            qkv, o, tot, pre = mix
            do, dyb, sg0 = _odd_out_bwd(dx1, y, modps[l], w_out, 0, ts)
            dq, dk, dv = _sb_bwd(qkv, do, tot, tq)
            dx, dqkv, hb, sg1 = _odd_qkv_bwd(x0, dx1, dq, dk, dv, pre, sg0, modps[l], w_in, 0, qkgs[j], ts)
            g_in = _tn_matmul(dqkv, 0, hb, g_in, 0, "tn_od_qkv")
            g_out = _tn_matmul(o, 0, dyb, g_out, 0, "tn_od_out")
            d_qg[j] = sg1[4, :HEAD]
            d_kg[j] = sg1[5, :HEAD]
        pieces = [g.reshape(N_DEV, -1, d) for g in ((g_ffn, g_in, g_out) if l > 0 else (g_in, g_out))]
        if l > 0:
            grads_in_flight[l], started = _exchange_start(pieces, [_landing(p, me, True) for p in pieces], True,
                                                          f"grads_start_{l}")
            modps[l - 1] = modps[l - 1] + started
        dmod[l] = jnp.concatenate([sg1[0:3], sg2[0:3]], axis=0).reshape(-1)
        d_gmix[l] = sg1[3]
        d_gffn[l] = sg2[3]

    small = jnp.concatenate(
        [jnp.stack(dmod).reshape(-1), jnp.stack(d_gmix).reshape(-1), jnp.stack(d_gffn).reshape(-1),
         jnp.stack(d_retg).reshape(-1), jnp.stack(d_qg).reshape(-1), jnp.stack(d_kg).reshape(-1),
         jnp.stack(d_conv).reshape(-1)])
    n_small = small.shape[0]
    rows_small = -(-n_small // (8 * 128)) * 8
    small = jnp.concatenate([small, jnp.zeros((rows_small * 128 - n_small,), F32)]).reshape(rows_small, 128)
    small_flight, started = _exchange_start([small], [_landing(small, me, False)], False, "small_start")
    grads_in_flight[0], started = _exchange_start(pieces, [_landing(p, me, True, started) for p in pieces],
                                                  True, "grads_start_0")

    res = {}
    big = {"ev_w_in": (ev_w_in, m_ev_w_in, v_ev_w_in), "ev_w_out": (ev_w_out, m_ev_w_out, v_ev_w_out),
           "od_w_qkv": (od_w_qkv, m_od_w_qkv, v_od_w_qkv), "od_w_out": (od_w_out, m_od_w_out, v_od_w_out),
           "ffn_w_gate": (ffn_w_gate, m_ffn_w_gate, v_ffn_w_gate), "ffn_w_up": (ffn_w_up, m_ffn_w_up, v_ffn_w_up),
           "ffn_w_down": (ffn_w_down, m_ffn_w_down, v_ffn_w_down)}
    flipped = ("ev_w_in", "ffn_w_gate", "ffn_w_up")
    flip = lambda a: a.transpose(0, 2, 1)
    for name in flipped:
        big[name] = tuple(flip(a) for a in big[name])
    for name, (w, _, _) in big.items():
        res[name] = tuple(lax.empty(w.shape, F32) for _ in range(4))

    def update(name, idx, g_layer, part=0):
        w, m, v = big[name]
        res[name] = _adamw_layer(w, g_layer, m, v, res[name], idx, "adamw_" + name, part)

    def ffn_updates(l, r_ffn):
        update("ffn_w_gate", l, r_ffn, 0)
        update("ffn_w_up", l, r_ffn, 1)
        update("ffn_w_down", l, r_ffn, 2)

    def mixer_updates(l, r_in, r_out):
        if l % 2 == 0:
            update("ev_w_in", l // 2, r_in)
        else:
            update("od_w_qkv", l // 2, _sum_slots(r_in, "sum_qkv").T)
        update("ev_w_out" if l % 2 == 0 else "od_w_out", l // 2, r_out)

    after = [dx + started]
    for l in reversed(range(1, depth)):
        recv = _exchange_wait(*grads_in_flight[l], True, after, f"grads_wait_{l}")
        ffn_updates(l, recv[0])
        mixer_updates(l, recv[1], recv[2])
        after = [res[name][1] for name in big]

    got2 = _exchange_wait(*small_flight, False, after, "small_wait")[0]
    tot_small = _sum_small(got2).reshape(-1)
    n_mod = depth * 6 * d
    dmod_all = got2.reshape(N_DEV, -1)[:, :n_mod].reshape(N_DEV, depth, 6 * d)
    dmod_cols = lax.dynamic_slice(dmod_all, (0, 0, me * cols), (N_DEV, depth, cols)).transpose(1, 0, 2)
    g_ada_w = _ada_bwd(c_all.T, dmod_cols)
    res["ada_w"] = (g_ada_w,) + _adamw_nd(ada_w, g_ada_w, m_ada_w, v_ada_w, "adamw_ada_w")

    off = [0]

    def take(shape):
        n = int(np.prod(shape))
        out = tot_small[off[0]:off[0] + n].reshape(shape)
        off[0] += n
        return out

    g_ada_b = take((depth, 6 * d))
    g_norm_mix = take((depth, d))
    g_norm_ffn = take((depth, d))
    g_ret_norm = take((n_even, cd))
    g_q_norm = take((n_odd, HEAD))
    g_k_norm = take((n_odd, HEAD))
    g_conv_full = take((n_even, 3, cd))
    g_conv = lax.dynamic_slice(g_conv_full, (0, 0, me * cc), (n_even, 3, cc))

    recv = _exchange_wait(*last_ffn_flight, True, [res["ada_w"][1]], "grads_wait_ffn_0")
    ffn_updates(0, recv[0])
    after = [res[name][1] for name in ("ffn_w_gate", "ffn_w_up", "ffn_w_down")]
    recv = _exchange_wait(*grads_in_flight[0], True, after, "grads_wait_0")
    mixer_updates(0, recv[0], recv[1])
    for name in flipped:
        res[name] = tuple(flip(a) for a in res[name])

    smalls = [("ada_b", ada_b, g_ada_b, m_ada_b, v_ada_b), ("norm_mix_g", norm_mix_g, g_norm_mix, m_norm_mix_g, v_norm_mix_g),
              ("norm_ffn_g", norm_ffn_g, g_norm_ffn, m_norm_ffn_g, v_norm_ffn_g),
              ("ev_conv_w", ev_conv_w, g_conv, m_ev_conv_w, v_ev_conv_w),
              ("ev_ret_norm_g", ev_ret_norm_g, g_ret_norm, m_ev_ret_norm_g, v_ev_ret_norm_g),
              ("od_q_norm_g", od_q_norm_g, g_q_norm, m_od_q_norm_g, v_od_q_norm_g),
              ("od_k_norm_g", od_k_norm_g, g_k_norm, m_od_k_norm_g, v_od_k_norm_g)]

    def pack(arrs):
        flat = jnp.concatenate([a.reshape(-1) for a in arrs])
        rows = -(-flat.shape[0] // (8 * 128)) * 8
        return jnp.concatenate([flat, jnp.zeros((rows * 128 - flat.shape[0],), F32)]).reshape(rows, 128)

    sd, sm, sv_ = _adamw(pack([t[1] for t in smalls]), pack([t[2] for t in smalls]),
                         pack([t[3] for t in smalls]), pack([t[4] for t in smalls]), "adamw_small")
    sd, sm, sv_ = sd.reshape(-1), sm.reshape(-1), sv_.reshape(-1)
    pos = 0
    for name, w, g, m, v in smalls:
        n = int(np.prod(w.shape))
        res[name] = (g, sd[pos:pos + n].reshape(w.shape), sm[pos:pos + n].reshape(w.shape),
                     sv_[pos:pos + n].reshape(w.shape))
        pos += n

    order = ["ada_w", "ada_b", "norm_mix_g", "norm_ffn_g", "ev_w_in", "ev_conv_w", "ev_ret_norm_g", "ev_w_out",
             "od_w_qkv", "od_q_norm_g", "od_k_norm_g", "od_w_out", "ffn_w_gate", "ffn_w_up", "ffn_w_down"]
    outs = [loss, dx[None]]
    for k in range(4):
        outs += [res[name][k] for name in order]
    return tuple(outs)
```

```python
import functools
import math

import numpy as np
import jax
import jax.numpy as jnp
from jax import lax
from jax.experimental import pallas as pl
from jax.experimental.pallas import tpu as pltpu

F32 = jnp.float32
BF16 = jnp.bfloat16
MESH = pl.DeviceIdType.MESH

N_DEV = 8
EPS = 1e-6
CHUNK = 64
HEAD = 128
RET_HEADS = 4
SB_HEADS = 8
ROPE_THETA = 10000.0
KEY_BLOCK = 128
ADAM_LR, ADAM_B1, ADAM_B2, ADAM_EPS, ADAM_WD, ADAM_STEP = 0.001, 0.9, 0.999, 1e-08, 0.01, 10
VMEM_LIMIT = 56 * 1024 * 1024


def _pcall(body, **kw):
    return pl.pallas_call(body, **kw)


def _params(n_grid=1, vmem=VMEM_LIMIT):
    return pltpu.CompilerParams(dimension_semantics=("arbitrary",) * n_grid, vmem_limit_bytes=vmem)


def _mm(a, b):
    return jnp.dot(a, b, preferred_element_type=F32)


def _mm_nt(a, b):
    return lax.dot_general(a, b, (((1,), (1,)), ((), ())), preferred_element_type=F32)


def _mm_tn(a, b):
    return lax.dot_general(a, b, (((0,), (0,)), ((), ())), preferred_element_type=F32)


def _bf(a):
    return a.astype(BF16)


def _sigmoid(a):
    return 1.0 / (1.0 + jnp.exp(-a))


def _sum0(a):
    return jnp.sum(a, axis=0, keepdims=True)


def _full(shape):
    nd = len(shape)
    return pl.BlockSpec(shape, lambda *_: (0,) * nd)


def _normmod_fwd(x, g, sc, sh):
    rstd = lax.rsqrt(jnp.mean(x * x, axis=-1, keepdims=True) + EPS)
    n = x * rstd
    return n, rstd, (n * g) * (1.0 + sc) + sh


def _normmod_bwd(dh, n, rstd, g, sc):
    dsh = _sum0(dh)
    dsc = _sum0(dh * (n * g))
    dg = _sum0(dh * n * (1.0 + sc))
    dn = dh * (g * (1.0 + sc))
    dx = rstd * (dn - n * jnp.mean(dn * n, axis=-1, keepdims=True))
    return dx, dsh, dsc, dg


def _rms_fwd(o):
    rstd = lax.rsqrt(jnp.mean(o * o, axis=-1, keepdims=True) + EPS)
    return o * rstd, rstd


def _rms_bwd(dn, n, rstd):
    return rstd * (dn - n * jnp.mean(dn * n, axis=-1, keepdims=True))


def _all_gather(x2d, name):
    m_per, n = x2d.shape
    space = pltpu.VMEM

    def body(x_ref, out_ref, done_ref, send_sems, recv_sems, local_sem):
        x, y, c = lax.axis_index("x"), lax.axis_index("y"), lax.axis_index("c")
        me, sibling = (x, y, c), (x, y, 1 - c)
        chips = [(1 - x, y), (x, 1 - y), (1 - x, 1 - y)]

        def rows(px, py, pc):
            return out_ref.at[pl.ds((4 * px + 2 * py + pc) * m_per, m_per), :]

        def copy(k, block, to, src=None):
            return pltpu.make_async_remote_copy(
                src_ref=rows(*block) if src is None else src, dst_ref=rows(*block),
                send_sem=send_sems.at[k], recv_sem=recv_sems.at[k],
                device_id=to, device_id_type=MESH)

        mine = pltpu.make_async_copy(x_ref, rows(*me), local_sem)
        mine.start()
        first = [copy(1 + j, me, (*chip, c), src=x_ref) for j, chip in enumerate(chips)]
        first += [copy(0, me, sibling, src=x_ref)]
        for cp in first:
            cp.start()
        passed = [copy(4 + j, (*chip, c), sibling) for j, chip in enumerate(chips)]
        for j, chip in enumerate(chips):
            copy(1 + j, (*chip, c), me).wait_recv()
            passed[j].start()
        copy(0, sibling, me).wait_recv()
        for j, chip in enumerate(chips):
            copy(4 + j, (*chip, 1 - c), me).wait_recv()
        for cp in first + passed:
            cp.wait_send()
        mine.wait()
        done_ref[...] = jnp.zeros_like(done_ref)

    out, done = _pcall(
        body, name=name,
        out_shape=(jax.ShapeDtypeStruct((N_DEV * m_per, n), x2d.dtype), jax.ShapeDtypeStruct((8, 128), F32)),
        in_specs=[pl.BlockSpec(memory_space=space)],
        out_specs=(pl.BlockSpec(memory_space=space), pl.BlockSpec(memory_space=pltpu.VMEM)),
        scratch_shapes=[pltpu.SemaphoreType.DMA((7,)), pltpu.SemaphoreType.DMA((7,)),
                        pltpu.SemaphoreType.DMA],
    )(x2d)
    return out, done[0, 0]


_HBM = pl.BlockSpec(memory_space=pltpu.HBM)
_SEM = pl.BlockSpec(memory_space=pltpu.SEMAPHORE)
_EFFECT = pltpu.SideEffectType.DATAFLOW_SIDE_EFFECTING


def _exchange_copies(src_refs, land_refs, send_sems, recv_sems, scatter):
    x, y, c = lax.axis_index("x"), lax.axis_index("y"), lax.axis_index("c")
    me = 4 * x + 2 * y + c
    out = []
    for i, (s_ref, l_ref) in enumerate(zip(src_refs, land_refs)):
        for k in (2, 4, 6, 3, 5, 7, 1):
            px = (1 - x) if (k >> 2) & 1 else x
            py = (1 - y) if (k >> 1) & 1 else y
            pc = (1 - c) if k & 1 else c
            out.append(pltpu.make_async_remote_copy(
                src_ref=s_ref.at[4 * px + 2 * py + pc] if scatter else s_ref, dst_ref=l_ref.at[me],
                send_sem=send_sems.at[7 * i + k - 1], recv_sem=recv_sems.at[7 * i + k - 1],
                device_id=(px, py, pc), device_id_type=MESH))
    return out


def _exchange_start(srcs, lands, scatter, name):
    n = len(srcs)

    def body(*refs):
        for cp in _exchange_copies(refs[:n], refs[n:2 * n], refs[2 * n], refs[2 * n + 1], scatter):
            cp.start()
        refs[-1][...] = jnp.zeros_like(refs[-1])

    arrays = list(srcs) + list(lands)
    outs = _pcall(
        body, name=name,
        out_shape=(pltpu.SemaphoreType.DMA((7 * n,)), pltpu.SemaphoreType.DMA((7 * n,)),
                   *[pltpu.HBM(a.shape, a.dtype) for a in arrays], jax.ShapeDtypeStruct((8, 128), F32)),
        in_specs=[_HBM] * (2 * n),
        out_specs=(_SEM, _SEM, *[_HBM] * (2 * n), pl.BlockSpec(memory_space=pltpu.VMEM)),
        input_output_aliases={i: 2 + i for i in range(2 * n)},
        compiler_params=pltpu.CompilerParams(has_side_effects=_EFFECT),
    )(*[pltpu.with_memory_space_constraint(a, pltpu.HBM) for a in arrays])
    return (outs[0], outs[1], list(outs[2:2 + n]), list(outs[2 + n:2 + 2 * n])), outs[-1][0, 0]


def _exchange_wait(send_sems, recv_sems, srcs, lands, scatter, after, name):
    n = len(srcs)
    after = list(after)

    def body(*refs):
        for cp in _exchange_copies(refs[:n], refs[n:2 * n], refs[2 * n], refs[2 * n + 1], scatter):
            cp.wait_send()
            cp.wait_recv()

    arrays = list(srcs) + list(lands)
    outs = _pcall(
        body, name=name,
        out_shape=tuple(pltpu.HBM(a.shape, a.dtype) for a in arrays),
        in_specs=[_HBM] * (2 * n) + [_SEM, _SEM] + [pl.BlockSpec(memory_space=pl.ANY)] * len(after),
        out_specs=tuple([_HBM] * (2 * n)),
        input_output_aliases={i: i for i in range(2 * n)},
        compiler_params=pltpu.CompilerParams(has_side_effects=_EFFECT),
    )(*arrays, send_sems, recv_sems, *after)
    return list(outs[n:])


def _landing(src, me, scatter, after=None):
    own = lax.dynamic_index_in_dim(src, me, 0, keepdims=True) if scatter else src[None]
    if after is not None:
        own = own + after.astype(own.dtype)
    shape = src.shape if scatter else (N_DEV,) + src.shape
    return lax.dynamic_update_slice(lax.empty(shape, src.dtype), own, (me, 0, 0))


def _sum_slots(recv, name):
    _, r, n = recv.shape
    tr = r
    for cand in (512, 448, 384, 352, 256, 128, 64, 32, 16, 8):
        if r % cand == 0:
            tr = cand
            break

    def body(r_ref, o_ref):
        acc = r_ref[0].astype(F32)
        for p in range(1, N_DEV):
            acc = acc + r_ref[p].astype(F32)
        o_ref[...] = acc

    return _pcall(
        body, name=name, grid=(r // tr,),
        out_shape=jax.ShapeDtypeStruct((r, n), F32),
        in_specs=[pl.BlockSpec((N_DEV, tr, n), lambda i: (0, i, 0))],
        out_specs=pl.BlockSpec((tr, n), lambda i: (i, 0)),
        compiler_params=_params(1),
    )(recv)


def _row_tile(rows, limit=512):
    for cand in range(min(limit, rows) // 8 * 8, 7, -8):
        if rows % cand == 0:
            return cand
    return rows


def _adamw(w, g, m, v, name):
    r, n = w.shape
    tr = _row_tile(r)

    def body(w_ref, g_ref, m_ref, v_ref, d_ref, nm_ref, nv_ref):
        d_ref[...], nm_ref[...], nv_ref[...] = _adam_update(w_ref[...], g_ref[...], m_ref[...], v_ref[...])

    spec = pl.BlockSpec((tr, n), lambda i: (i, 0))
    shp = jax.ShapeDtypeStruct((r, n), F32)
    return _pcall(
        body, name=name, grid=(r // tr,), out_shape=(shp, shp, shp),
        in_specs=[spec] * 4, out_specs=(spec, spec, spec), compiler_params=_params(1),
    )(w, g, m, v)


def _adam_update(wv, gv, mv, vv):
    bc1 = 1.0 / (1.0 - ADAM_B1 ** ADAM_STEP)
    bc2 = 1.0 / (1.0 - ADAM_B2 ** ADAM_STEP)
    nm = ADAM_B1 * mv + (1.0 - ADAM_B1) * gv
    nv = ADAM_B2 * vv + (1.0 - ADAM_B2) * (gv * gv)
    return -ADAM_LR * ((nm * bc1) / (jnp.sqrt(nv * bc2) + ADAM_EPS) + ADAM_WD * wv), nm, nv


def _adamw_layer(w, g_layer, m, v, outs, idx, name, part=0):
    _, a, b = w.shape
    tr = _row_tile(a)
    slots = g_layer.ndim == 3

    def body(w_ref, g_ref, m_ref, v_ref, o0, o1, o2, o3, go_ref, d_ref, nm_ref, nv_ref):
        if slots:
            gv = g_ref[0].astype(F32)
            for p in range(1, N_DEV):
                gv = gv + g_ref[p].astype(F32)
        else:
            gv = g_ref[...]
        go_ref[...] = gv
        d_ref[...], nm_ref[...], nv_ref[...] = _adam_update(w_ref[...], gv, m_ref[...], v_ref[...])

    layer = pl.BlockSpec((None, tr, b), lambda i: (idx, i, 0))
    steps = a // tr
    g_spec = (pl.BlockSpec((N_DEV, tr, b), lambda i: (0, part * steps + i, 0)) if slots
              else pl.BlockSpec((tr, b), lambda i: (i, 0)))
    anyw = pl.BlockSpec(memory_space=pl.ANY)
    shp = jax.ShapeDtypeStruct(w.shape, F32)
    return tuple(_pcall(
        body, name=name, grid=(steps,), out_shape=(shp,) * 4,
        in_specs=[layer, g_spec, layer, layer, anyw, anyw, anyw, anyw],
        out_specs=(layer,) * 4, input_output_aliases={4: 0, 5: 1, 6: 2, 7: 3},
        compiler_params=_params(1),
    )(w, g_layer, m, v, *outs))


def _adamw_nd(w, g, m, v, name):
    shp = w.shape
    f = lambda a: a.reshape(-1, shp[-1])
    d, nm, nv = _adamw(f(w), f(g), f(m), f(v), name)
    return d.reshape(shp), nm.reshape(shp), nv.reshape(shp)


def _ada_fwd(c_all, ada_w, ada_b_cols):
    n_l, d, cols = ada_w.shape

    def body(c_ref, w_ref, b_ref, o_ref):
        cv = c_ref[...]
        ca = cv * _sigmoid(cv)
        o_ref[...] = _mm(_bf(ca), _bf(w_ref[...])) + b_ref[...]

    return _pcall(
        body, name="ada_fwd", grid=(n_l,),
        out_shape=jax.ShapeDtypeStruct((n_l, N_DEV, cols), F32),
        in_specs=[_full((N_DEV, d)), pl.BlockSpec((None, d, cols), lambda l: (l, 0, 0)),
                  pl.BlockSpec((None, 1, cols), lambda l: (l, 0, 0))],
        out_specs=pl.BlockSpec((None, N_DEV, cols), lambda l: (l, 0, 0)),
        compiler_params=_params(1),
    )(c_all, ada_w, ada_b_cols.reshape(n_l, 1, cols))


def _ada_bwd(c_all_t, dmod_cols):
    d = c_all_t.shape[0]
    n_l, _, cols = dmod_cols.shape

    def body(ct_ref, dm_ref, o_ref):
        cv = ct_ref[...]
        ca = cv * _sigmoid(cv)
        dm = dm_ref[...]
        acc = ca[:, 0:1] * dm[0:1, :]
        for b in range(1, N_DEV):
            acc = acc + ca[:, b:b + 1] * dm[b:b + 1, :]
        o_ref[...] = acc

    return _pcall(
        body, name="ada_bwd", grid=(n_l,),
        out_shape=jax.ShapeDtypeStruct((n_l, d, cols), F32),
        in_specs=[_full((d, N_DEV)), pl.BlockSpec((None, N_DEV, cols), lambda l: (l, 0, 0))],
        out_specs=pl.BlockSpec((None, d, cols), lambda l: (l, 0, 0)),
        compiler_params=_params(1),
    )(c_all_t, dmod_cols)


def _sum_small(gathered):
    _, r, n = gathered.shape

    def body(g_ref, o_ref):
        acc = g_ref[0]
        for p in range(1, N_DEV):
            acc = acc + g_ref[p]
        o_ref[...] = acc

    return _pcall(
        body, name="sum_small", out_shape=jax.ShapeDtypeStruct((r, n), F32),
        in_specs=[_full((N_DEV, r, n))], out_specs=_full((r, n)),
    )(gathered)


def _loss_grad(xf, tgt, ts):
    s, d = xf.shape

    def body(x_ref, t_ref, dx_ref, l_ref):
        @pl.when(pl.program_id(0) == 0)
        def _():
            l_ref[...] = jnp.zeros_like(l_ref)
        e = x_ref[...] - t_ref[...]
        dx_ref[...] = e * (1.0 / d)
        l_ref[...] += (0.5 / d) * jnp.sum(jnp.sum(e * e, axis=1, keepdims=True), axis=0, keepdims=True)

    spec = pl.BlockSpec((ts, d), lambda i: (i, 0))
    return _pcall(
        body, name="loss_grad", grid=(s // ts,),
        out_shape=(jax.ShapeDtypeStruct((s, d), F32), jax.ShapeDtypeStruct((1, 1), F32)),
        in_specs=[spec, spec], out_specs=(spec, _full((1, 1))), compiler_params=_params(1),
    )(xf, tgt)


def _tn_matmul(a, col_block, b, buf, slot, name):
    s = a.shape[0]
    k = b.shape[1]
    n_p = buf.shape[2]
    mcols = N_DEV * n_p
    ts = _row_tile(s, 1024)
    nt = s // ts

    def body(a_ref, b_ref, buf_ref, o_ref, acc):
        i = pl.program_id(0)

        @pl.when(i == 0)
        def _():
            acc[...] = jnp.zeros_like(acc)
        acc[...] += _mm_tn(a_ref[...], b_ref[...])

        @pl.when(i == nt - 1)
        def _():
            o_ref[...] = acc[...].reshape(N_DEV, n_p, k).astype(BF16)

    return _pcall(
        body, name=name, grid=(nt,),
        out_shape=jax.ShapeDtypeStruct(buf.shape, BF16),
        in_specs=[pl.BlockSpec((ts, mcols), lambda i: (i, col_block)),
                  pl.BlockSpec((ts, k), lambda i: (i, 0)),
                  pl.BlockSpec(memory_space=pl.ANY)],
        out_specs=pl.BlockSpec((N_DEV, None, n_p, k), lambda i: (0, slot, 0, 0)),
        scratch_shapes=[pltpu.VMEM((mcols, k), F32)],
        input_output_aliases={2: 0},
        compiler_params=_params(1),
    )(a, b, buf)


def _wspec4(w, slot):
    _, _, n_p, k = w.shape
    return pl.BlockSpec((N_DEV, None, n_p, k), lambda i: (0, slot, 0, 0), pipeline_mode=pl.Buffered(1))


def _ffn_fwd(x1, modp, w352, l, ts):
    s, d = x1.shape
    n_l = w352.shape[1] // 3
    f_dim = N_DEV * w352.shape[2]

    def body(x_ref, mp_ref, wg_ref, wu_ref, wd_ref, x2_ref, f_ref, ab_ref):
        x = x_ref[...]
        _, _, h2 = _normmod_fwd(x, mp_ref[7:8, :], mp_ref[4:5, :], mp_ref[3:4, :])
        hb = _bf(h2)
        f = jnp.zeros((ts, d), F32)
        half_dev, fc = N_DEV // 2, f_dim // 2
        for part in range(2):
            dev0, c0 = part * half_dev, part * fc
            a = _mm_nt(hb, wg_ref[dev0:dev0 + half_dev].reshape(fc, d))
            b = _mm_nt(hb, wu_ref[dev0:dev0 + half_dev].reshape(fc, d))
            ab_ref[:, c0:c0 + fc] = a
            ab_ref[:, f_dim + c0:f_dim + c0 + fc] = b
            sv = (a * _sigmoid(a)) * b
            f = f + _mm(_bf(sv), wd_ref[dev0:dev0 + half_dev].reshape(fc, d))
        f_ref[...] = f
        x2_ref[...] = x + mp_ref[5:6, :] * f

    tile = pl.BlockSpec((ts, d), lambda i: (i, 0))
    shp = jax.ShapeDtypeStruct((s, d), F32)
    return _pcall(
        body, name="ffn_fwd", grid=(s // ts,),
        out_shape=(shp, shp, jax.ShapeDtypeStruct((s, 2 * f_dim), F32)),
        in_specs=[tile, _full(modp.shape), _wspec4(w352, l), _wspec4(w352, n_l + l),
                  _wspec4(w352, 2 * n_l + l)],
        out_specs=(tile, tile, pl.BlockSpec((ts, 2 * f_dim), lambda i: (i, 0))), compiler_params=_params(1),
    )(x1, modp, w352, w352, w352)


def _ffn_bwd(x1, f, ab, dx2, modp, w352, l, ts):
    s, d = x1.shape
    n_l = w352.shape[1] // 3
    f_dim = N_DEV * w352.shape[2]

    def body(x_ref, f_ref, ab_ref, dx2_ref, mp_ref, wg_ref, wu_ref, wd_ref,
             dx1_ref, dab_ref, h2_ref, s_ref, df_ref, sg_ref):
        @pl.when(pl.program_id(0) == 0)
        def _():
            sg_ref[...] = jnp.zeros_like(sg_ref)
        x = x_ref[...]
        gffn, sc2, g2 = mp_ref[7:8, :], mp_ref[4:5, :], mp_ref[5:6, :]
        n, rstd, h2 = _normmod_fwd(x, gffn, sc2, mp_ref[3:4, :])
        hb = _bf(h2)
        dx2 = dx2_ref[...]
        dfb = _bf(g2 * dx2)
        dh2 = jnp.zeros((ts, d), F32)
        half_dev, fc = N_DEV // 2, f_dim // 2
        for part in range(2):
            dev0, c0 = part * half_dev, part * fc
            wg = wg_ref[dev0:dev0 + half_dev].reshape(fc, d)
            wu = wu_ref[dev0:dev0 + half_dev].reshape(fc, d)
            a = ab_ref[:, c0:c0 + fc]
            b = ab_ref[:, f_dim + c0:f_dim + c0 + fc]
            sig = _sigmoid(a)
            sa = a * sig
            s_ref[:, c0:c0 + fc] = _bf(sa * b)
            ds = _mm_nt(dfb, wd_ref[dev0:dev0 + half_dev].reshape(fc, d))
            dab = _bf(ds * b * (sig * (1.0 + a * (1.0 - sig))))
            dbb = _bf(ds * sa)
            dab_ref[:, c0:c0 + fc] = dab
            dab_ref[:, f_dim + c0:f_dim + c0 + fc] = dbb
            dh2 = dh2 + _mm(dab, wg) + _mm(dbb, wu)
        dxn, dsh, dsc, dg = _normmod_bwd(dh2, n, rstd, gffn, sc2)
        dx1_ref[...] = dx2 + dxn
        h2_ref[...] = hb
        df_ref[...] = dfb
        sg_ref[0:1, :] += dsh
        sg_ref[1:2, :] += dsc
        sg_ref[2:3, :] += _sum0(dx2 * f_ref[...])
        sg_ref[3:4, :] += dg

    tile = pl.BlockSpec((ts, d), lambda i: (i, 0))
    f32t = jax.ShapeDtypeStruct((s, d), F32)
    bft = jax.ShapeDtypeStruct((s, d), BF16)
    return _pcall(
        body, name="ffn_bwd", grid=(s // ts,),
        out_shape=(f32t, jax.ShapeDtypeStruct((s, 2 * f_dim), BF16), bft,
                   jax.ShapeDtypeStruct((s, f_dim), BF16), bft, jax.ShapeDtypeStruct((8, d), F32)),
        in_specs=[tile, tile, pl.BlockSpec((ts, 2 * f_dim), lambda i: (i, 0)), tile, _full(modp.shape),
                  _wspec4(w352, l), _wspec4(w352, n_l + l), _wspec4(w352, 2 * n_l + l)],
        out_specs=(tile, pl.BlockSpec((ts, 2 * f_dim), lambda i: (i, 0)), tile,
                   pl.BlockSpec((ts, f_dim), lambda i: (i, 0)), tile, _full((8, d))),
        compiler_params=_params(1),
    )(x1, f, ab, dx2, modp, w352, w352, w352)


def _retention_consts(ts):
    h = np.arange(RET_HEADS, dtype=np.float64)
    log_g = np.log1p(-np.exp2(-5.0 - h))
    t = np.arange(ts)
    diff = t[:, None] - t[None, :]
    same = (t[:, None] // CHUNK) == (t[None, :] // CHUNK)
    later = (t[:, None] // CHUNK) > (t[None, :] // CHUNK)
    dm = np.where(same, np.abs(diff), np.where(later, diff, 0))[None] * log_g[:, None, None]
    dm = np.where((same | later)[None], np.exp(dm), 0.0)
    qd = np.exp((t[:, None] + 1.0) * log_g[None, :])
    kd = np.exp((ts - 1.0 - t[:, None]) * log_g[None, :])
    qd = np.repeat(qd, HEAD, axis=1)
    kd = np.repeat(kd, HEAD, axis=1)
    tdec = [float(np.exp(ts * lg)) for lg in log_g]
    return (jnp.asarray(dm, F32), jnp.asarray(qd, F32), jnp.asarray(kd, F32), tdec)


def _rope_tables(s):
    inv_freq = 1.0 / (ROPE_THETA ** (jnp.arange(0, HEAD, 2, dtype=F32) / HEAD))
    ang = jnp.arange(s, dtype=F32)[:, None] * inv_freq[None, :]
    cos, sin = jnp.cos(ang), jnp.sin(ang)
    return jnp.concatenate([cos, cos], axis=1), jnp.concatenate([-sin, sin], axis=1)


def _rope(v, cos, sin):
    return v * cos + pltpu.roll(v, HEAD // 2, 1) * sin


def _rope_t(dv, cos, sin):
    return dv * cos + pltpu.roll(dv * sin, HEAD // 2, 1)


def _shift_down(z, k, halo_ref):
    r = pltpu.roll(z, k, 0)
    rows = lax.broadcasted_iota(jnp.int32, z.shape, 0)
    for j in range(k):
        r = jnp.where(rows == j, halo_ref[8 - k + j:8 - k + j + 1, :], r)
    return r


def _shift_up(z, k, halo_ref):
    n = z.shape[0]
    r = pltpu.roll(z, n - k, 0)
    rows = lax.broadcasted_iota(jnp.int32, z.shape, 0)
    for j in range(k):
        r = jnp.where(rows == n - k + j, halo_ref[j:j + 1, :], r)
    return r


def _even_recompute(x, mp_ref, win, cw_ref, cos, sin, dm_ref, qd_ref, kd_ref, halo_ref, state_of, proj=None):
    cd = 4 * HEAD
    n, rstd, h = _normmod_fwd(x, mp_ref[6:7, :], mp_ref[1:2, :], mp_ref[0:1, :])
    if proj is None:
        proj = _mm_nt(_bf(h), win)
    bg, cg, u = proj[:, 0:cd], proj[:, cd:2 * cd], proj[:, 2 * cd:3 * cd]
    z = cg * u
    z1 = _shift_down(z, 1, halo_ref)
    z2 = _shift_down(z, 2, halo_ref)
    conv = cw_ref[0:1, :] * z2 + cw_ref[1:2, :] * z1 + cw_ref[2:3, :] * z
    heads = []
    scale = HEAD ** -0.5
    for hh in range(RET_HEADS):
        lo = hh * HEAD
        q = proj[:, 3 * cd + lo:3 * cd + lo + HEAD]
        k = proj[:, 4 * cd + lo:4 * cd + lo + HEAD]
        v = proj[:, 5 * cd + lo:5 * cd + lo + HEAD]
        gate = proj[:, 6 * cd + lo:6 * cd + lo + HEAD]
        qr = _rope(q, cos, sin)
        kr = _rope(k, cos, sin) * scale
        sc = _mm_nt(_bf(qr), _bf(kr)) * dm_ref[hh]
        qs = qr * qd_ref[:, lo:lo + HEAD]
        ks = kr * kd_ref[:, lo:lo + HEAD]
        o = _mm(_bf(sc), _bf(v)) + _mm(_bf(qs), _bf(state_of(hh)))
        on, orstd = _rms_fwd(o)
        sig = _sigmoid(gate)
        heads.append(dict(qr=qr, kr=kr, v=v, gate=gate, sc=sc, qs=qs, ks=ks, on=on, orstd=orstd, sig=sig))
    return dict(n=n, rstd=rstd, h=h, proj=proj, bg=bg, cg=cg, u=u, z=z, z1=z1, z2=z2, conv=conv, heads=heads)


def _even_fwd(x, modp, w448, w128, l, cw, cos, sin, consts, ts):
    s, d = x.shape
    nt = s // ts
    dm, qd, kd, tdec = consts
    cd = 4 * HEAD
    e_in = N_DEV * w448.shape[2]

    def body(x_ref, mp_ref, win_ref, cw_ref, cos_ref, sin_ref, dm_ref, qd_ref, kd_ref, wout_ref,
             x1_ref, y_ref, st_ref, zh_ref, proj_ref, state, halo):
        @pl.when(pl.program_id(0) == 0)
        def _():
            state[...] = jnp.zeros_like(state)
            halo[...] = jnp.zeros_like(halo)
        xv = x_ref[...]
        st_ref[...] = state[...]
        zh_ref[...] = halo[...]
        r = _even_recompute(xv, mp_ref, win_ref[...].reshape(e_in, d), cw_ref, cos_ref[...], sin_ref[...],
                            dm_ref, qd_ref, kd_ref, halo, lambda hh: state[hh])
        proj_ref[...] = r["proj"]
        halo[...] = r["z"][ts - 8:ts, :]
        parts = [r["bg"] * r["conv"]]
        for hh, hd in enumerate(r["heads"]):
            state[hh] = state[hh] * tdec[hh] + _mm_tn(_bf(hd["ks"]), _bf(hd["v"]))
            rg = cw_ref[3:4, hh * HEAD:(hh + 1) * HEAD]
            parts.append((hd["gate"] * hd["sig"]) * (hd["on"] * rg))
        mcat = jnp.concatenate(parts, axis=1)
        y = _mm(_bf(mcat), wout_ref[...].reshape(d, d))
        y_ref[...] = y
        x1_ref[...] = xv + mp_ref[2:3, :] * y

    tile = pl.BlockSpec((ts, d), lambda i: (i, 0))
    rt = pl.BlockSpec((ts, HEAD), lambda i: (i, 0))
    shp = jax.ShapeDtypeStruct((s, d), F32)
    return _pcall(
        body, name="even_fwd", grid=(nt,),
        out_shape=(shp, shp, jax.ShapeDtypeStruct((nt, RET_HEADS, HEAD, HEAD), F32),
                   jax.ShapeDtypeStruct((nt, 8, cd), F32), jax.ShapeDtypeStruct((s, e_in), F32)),
        in_specs=[tile, _full(modp.shape), _wspec4(w448, l), _full(cw.shape), rt, rt,
                  _full(dm.shape), _full(qd.shape), _full(kd.shape), _wspec4(w128, l)],
        out_specs=(tile, tile, pl.BlockSpec((None, RET_HEADS, HEAD, HEAD), lambda i: (i, 0, 0, 0)),
                   pl.BlockSpec((None, 8, cd), lambda i: (i, 0, 0)), pl.BlockSpec((ts, e_in), lambda i: (i, 0))),
        scratch_shapes=[pltpu.VMEM((RET_HEADS, HEAD, HEAD), F32), pltpu.VMEM((8, cd), F32)],
        compiler_params=_params(1),
    )(x, modp, w448, cw, cos, sin, dm, qd, kd, w128)


def _even_bwd(x, dx1, y, states, zhalo, proj, modp, w448, w128, l, cw, cos, sin, consts, ts):
    s, d = x.shape
    nt = s // ts
    dm, qd, kd, tdec = consts
    cd = 4 * HEAD
    e_in = N_DEV * w448.shape[2]
    scale = HEAD ** -0.5

    def body(x_ref, dx1_ref, y_ref, st_ref, zh_ref, proj_ref, mp_ref, win_ref, cw_ref, cos_ref, sin_ref,
             dm_ref, qd_ref, kd_ref, wout_ref,
             dx_ref, dproj_ref, h_ref, m_ref, dy_ref, sg_ref, gstate, halo_d):
        @pl.when(pl.program_id(0) == 0)
        def _():
            gstate[...] = jnp.zeros_like(gstate)
            halo_d[...] = jnp.zeros_like(halo_d)
            sg_ref[...] = jnp.zeros_like(sg_ref)
        xv = x_ref[...]
        cos, sin = cos_ref[...], sin_ref[...]
        win = win_ref[...].reshape(e_in, d)
        r = _even_recompute(xv, mp_ref, win, cw_ref, cos, sin, dm_ref, qd_ref, kd_ref, zh_ref,
                            lambda hh: st_ref[hh], proj_ref[...])
        parts = [r["bg"] * r["conv"]]
        for hh, hd in enumerate(r["heads"]):
            rg = cw_ref[3:4, hh * HEAD:(hh + 1) * HEAD]
            parts.append((hd["gate"] * hd["sig"]) * (hd["on"] * rg))
        m_ref[...] = _bf(jnp.concatenate(parts, axis=1))
        h_ref[...] = _bf(r["h"])

        dx1 = dx1_ref[...]
        dy = mp_ref[2:3, :] * dx1
        dyb = _bf(dy)
        dy_ref[...] = dyb
        sg_ref[2:3, :] += _sum0(dx1 * y_ref[...])
        dmix = _mm_nt(dyb, wout_ref[...].reshape(d, d))

        da_out = dmix[:, 0:cd]
        dbg = da_out * r["conv"]
        dconv = da_out * r["bg"]
        dc1 = _shift_up(dconv, 1, halo_d)
        dc2 = _shift_up(dconv, 2, halo_d)
        dz = cw_ref[2:3, :] * dconv + cw_ref[1:2, :] * dc1 + cw_ref[0:1, :] * dc2
        halo_d[...] = dconv[0:8, :]
        sg_ref[4:5, 0:cd] += _sum0(dconv * r["z2"])
        sg_ref[5:6, 0:cd] += _sum0(dconv * r["z1"])
        sg_ref[6:7, 0:cd] += _sum0(dconv * r["z"])
        dcg = dz * r["u"]
        du = dz * r["cg"]

        dqs, dks, dvs, dgs = [], [], [], []
        for hh, hd in enumerate(r["heads"]):
            lo = hh * HEAD
            rg = cw_ref[3:4, lo:lo + HEAD]
            dr = dmix[:, cd + lo:cd + lo + HEAD]
            sig, gate, on = hd["sig"], hd["gate"], hd["on"]
            rn = on * rg
            dgate = dr * rn * (sig * (1.0 + gate * (1.0 - sig)))
            drn = dr * (gate * sig)
            sg_ref[7:8, lo:lo + HEAD] += _sum0(drn * on)
            do = _rms_bwd(drn * rg, on, hd["orstd"])
            dob = _bf(do)
            gst = _bf(gstate[hh])
            scb = _bf(hd["sc"])
            vb = _bf(hd["v"])
            qrb, krb = _bf(hd["qr"]), _bf(hd["kr"])
            dv = _mm_tn(scb, dob) + _mm(_bf(hd["ks"]), gst)
            dsc = _bf(_mm_nt(dob, vb) * dm_ref[hh])
            dqr = _mm(dsc, krb) + _mm_nt(dob, _bf(st_ref[hh])) * qd_ref[:, lo:lo + HEAD]
            dkr = _mm_tn(dsc, qrb) + _mm_nt(vb, gst) * kd_ref[:, lo:lo + HEAD]
            gstate[hh] = gstate[hh] * tdec[hh] + _mm_tn(_bf(hd["qs"]), dob)
            dqs.append(_rope_t(dqr, cos, sin))
            dks.append(_rope_t(dkr * scale, cos, sin))
            dvs.append(dv)
            dgs.append(dgate)

        dproj = _bf(jnp.concatenate([dbg, dcg, du] + dqs + dks + dvs + dgs, axis=1))
        dproj_ref[...] = dproj
        dh = _mm(dproj, win)
        dxn, dsh, dsc1, dg = _normmod_bwd(dh, r["n"], r["rstd"], mp_ref[6:7, :], mp_ref[1:2, :])
        dx_ref[...] = dx1 + dxn
        sg_ref[0:1, :] += dsh
        sg_ref[1:2, :] += dsc1
        sg_ref[3:4, :] += dg

    rev = lambda i: (nt - 1 - i, 0)
    tile = pl.BlockSpec((ts, d), rev)
    rt = pl.BlockSpec((ts, HEAD), rev)
    bft = jax.ShapeDtypeStruct((s, d), BF16)
    return _pcall(
        body, name="even_bwd", grid=(nt,),
        out_shape=(jax.ShapeDtypeStruct((s, d), F32), jax.ShapeDtypeStruct((s, e_in), BF16), bft, bft, bft,
                   jax.ShapeDtypeStruct((8, d), F32)),
        in_specs=[tile, tile, tile,
                  pl.BlockSpec((None, RET_HEADS, HEAD, HEAD), lambda i: (nt - 1 - i, 0, 0, 0)),
                  pl.BlockSpec((None, 8, cd), lambda i: (nt - 1 - i, 0, 0)), pl.BlockSpec((ts, e_in), rev),
                  _full(modp.shape), _wspec4(w448, l), _full(cw.shape), rt, rt,
                  _full(dm.shape), _full(qd.shape), _full(kd.shape), _wspec4(w128, l)],
        out_specs=(tile, pl.BlockSpec((ts, e_in), rev), tile, tile, tile, _full((8, d))),
        scratch_shapes=[pltpu.VMEM((RET_HEADS, HEAD, HEAD), F32), pltpu.VMEM((8, cd), F32)],
        compiler_params=_params(1),
    )(x, dx1, y, states, zhalo, proj, modp, w448, cw, cos, sin, dm, qd, kd, w128)


def _odd_qkv_fwd(x, modp, w384, j, qkg, ts):
    s, d = x.shape
    n3 = N_DEV * w384.shape[2]

    def body(x_ref, mp_ref, w_ref, g_ref, o_ref, pre_ref):
        _, _, h = _normmod_fwd(x_ref[...], mp_ref[6:7, :], mp_ref[1:2, :], mp_ref[0:1, :])
        qkv = _mm_nt(_bf(h), w_ref[...].reshape(n3, d))
        pre_ref[...] = qkv[:, 0:2 * d]
        for hh in range(SB_HEADS):
            lo = hh * HEAD
            qn, _ = _rms_fwd(qkv[:, lo:lo + HEAD])
            kn, _ = _rms_fwd(qkv[:, d + lo:d + lo + HEAD])
            o_ref[:, lo:lo + HEAD] = _bf(qn * g_ref[0:1, :])
            o_ref[:, d + lo:d + lo + HEAD] = _bf(kn * g_ref[1:2, :])
        o_ref[:, 2 * d:3 * d] = _bf(qkv[:, 2 * d:3 * d])

    return _pcall(
        body, name="odd_qkv_fwd", grid=(s // ts,),
        out_shape=(jax.ShapeDtypeStruct((s, n3), BF16), jax.ShapeDtypeStruct((s, 2 * d), F32)),
        in_specs=[pl.BlockSpec((ts, d), lambda i: (i, 0)), _full(modp.shape), _wspec4(w384, j),
                  _full(qkg.shape)],
        out_specs=(pl.BlockSpec((ts, n3), lambda i: (i, 0)), pl.BlockSpec((ts, 2 * d), lambda i: (i, 0))),
        compiler_params=_params(1),
    )(x, modp, w384, qkg)


SB_QUERIES = 512
SB_WIDE = 256
SB_LOOP_BLOCKS = 2


def _sb_logits(q, kw, mask):
    z = _mm_nt(q, kw) * (HEAD ** -0.5)
    e = jnp.exp(-jnp.abs(z))
    lb = jnp.minimum(z, 0.0) - jnp.log(1.0 + e)
    lk = lb - z
    if mask is not None:
        lk = jnp.where(mask, lk, 0.0)
    return lb, lk


def _tri(n, above):
    ri = lax.broadcasted_iota(jnp.int32, (n, n), 0)
    ci = lax.broadcasted_iota(jnp.int32, (n, n), 1)
    return ((ri > ci) if above else (ri < ci)).astype(BF16)


def _split_dot(a, tri):
    hi = _bf(a)
    lo = _bf(a - hi.astype(F32))
    return _mm(hi, tri) + _mm(lo, tri)


def _sb_fwd(qkv, tq):
    s = qkv.shape[0]
    d = qkv.shape[1] // 3
    nq = s // tq
    assert tq % SB_WIDE == 0
    parts = tq // SB_WIDE

    def body(q_ref, k_ref, v_ref, o_ref, t_ref, o_acc, run):
        qi = pl.program_id(1)
        base = qi * tq
        upper = _tri(SB_WIDE, True)
        o_acc[...] = jnp.zeros_like(o_acc)
        run[...] = jnp.zeros_like(run)

        def wide_step(ks, row0, masked, nblk):
            rows = slice(row0, tq)
            width = nblk * SB_WIDE
            mask = None
            if masked:
                qpos = base + row0 + lax.broadcasted_iota(jnp.int32, (tq - row0, width), 0)
                mask = qpos > ks + lax.broadcasted_iota(jnp.int32, (tq - row0, width), 1)
            lb, lk = _sb_logits(q_ref[rows, :], k_ref[pl.ds(ks, width), :], mask)
            blocks = [lk[:, b * SB_WIDE:(b + 1) * SB_WIDE] for b in range(nblk)]
            right = run[rows, :]
            accs = [None] * nblk
            for b in reversed(range(nblk)):
                accs[b] = _split_dot(blocks[b], upper) + right
                right = right + jnp.sum(blocks[b], axis=1, keepdims=True)
            w = jnp.exp(lb + (accs[0] if nblk == 1 else jnp.concatenate(accs, axis=1)))
            if masked:
                w = jnp.where(mask, w, 0.0)
            o_acc[rows, :] += _mm(_bf(w), v_ref[pl.ds(ks, width), :])
            run[rows, :] = right

        for part in reversed(range(parts)):
            wide_step(pl.multiple_of(base + part * SB_WIDE, SB_WIDE), part * SB_WIDE, True, 1)
        loop_width = SB_LOOP_BLOCKS * SB_WIDE
        nsteps = qi * (tq // loop_width)

        def step(it, carry):
            wide_step(pl.multiple_of((nsteps - 1 - it) * loop_width, loop_width), 0, False, SB_LOOP_BLOCKS)
            return carry

        lax.fori_loop(0, nsteps, step, 0)
        o_ref[...] = _bf(o_acc[...])
        t_ref[...] = run[...]

    nh = d // HEAD
    return _pcall(
        body, name="sb_fwd", grid=(nh, nq),
        out_shape=(jax.ShapeDtypeStruct((s, d), BF16), jax.ShapeDtypeStruct((nh, s, 1), F32)),
        in_specs=[pl.BlockSpec((tq, HEAD), lambda h, i: (i, h)),
                  pl.BlockSpec((s, HEAD), lambda h, i: (0, nh + h)),
                  pl.BlockSpec((s, HEAD), lambda h, i: (0, 2 * nh + h))],
        out_specs=(pl.BlockSpec((tq, HEAD), lambda h, i: (i, h)),
                   pl.BlockSpec((None, tq, 1), lambda h, i: (h, i, 0))),
        scratch_shapes=[pltpu.VMEM((tq, HEAD), F32), pltpu.VMEM((tq, 1), F32)],
        compiler_params=_params(2),
    )(qkv, qkv, qkv)


def _sb_bwd(qkv, do, tot, tq):
    s = qkv.shape[0]
    d = qkv.shape[1] // 3
    nq = s // tq
    scale = HEAD ** -0.5
    assert tq % SB_WIDE == 0
    parts = tq // SB_WIDE

    def body(q_ref, k_ref, v_ref, do_ref, t_ref, dq_ref, dk_ref, dv_ref, pk, pd):
        qi = pl.program_id(1)

        @pl.when(qi == 0)
        def _():
            dk_ref[...] = jnp.zeros_like(dk_ref)
            dv_ref[...] = jnp.zeros_like(dv_ref)
        base = qi * tq
        upper = _tri(SB_WIDE, True)
        lower = _tri(SB_WIDE, False)
        dq_ref[...] = jnp.zeros_like(dq_ref)
        pk[...] = jnp.zeros_like(pk)
        pd[...] = jnp.zeros_like(pd)

        def wide_step(ks, row0, masked, nblk):
            rows = slice(row0, tq)
            width = nblk * SB_WIDE
            cut = lambda a: [a[:, b * SB_WIDE:(b + 1) * SB_WIDE] for b in range(nblk)]
            join = lambda parts_: parts_[0] if nblk == 1 else jnp.concatenate(parts_, axis=1)
            mask = None
            if masked:
                qpos = base + row0 + lax.broadcasted_iota(jnp.int32, (tq - row0, width), 0)
                mask = qpos > ks + lax.broadcasted_iota(jnp.int32, (tq - row0, width), 1)
            kw = k_ref[pl.ds(ks, width), :]
            lb, lk = _sb_logits(q_ref[rows, :], kw, mask)
            left = pk[rows, :]
            total = t_ref[rows, :]
            accs = []
            for blk in cut(lk):
                left = left + jnp.sum(blk, axis=1, keepdims=True)
                accs.append(_split_dot(blk, upper) + (total - left))
            pk[rows, :] = left
            w = jnp.exp(lb + join(accs))
            if masked:
                w = jnp.where(mask, w, 0.0)
            de = _mm_nt(do_ref[rows, :], v_ref[pl.ds(ks, width), :]) * w
            before = pd[rows, :]
            dlks = []
            for blk in cut(de):
                dlks.append(_split_dot(blk, lower) + before)
                before = before + jnp.sum(blk, axis=1, keepdims=True)
            pd[rows, :] = before
            dz = (de - jnp.exp(lb) * (de + join(dlks))) * scale
            if masked:
                dz = jnp.where(mask, dz, 0.0)
            dzb = _bf(dz)
            dq_ref[rows, :] += _mm(dzb, kw)
            dv_ref[pl.ds(ks, width), :] += _mm_tn(_bf(w), do_ref[rows, :])
            dk_ref[pl.ds(ks, width), :] += _mm_tn(dzb, q_ref[rows, :])

        loop_width = SB_LOOP_BLOCKS * SB_WIDE

        def step(jb, carry):
            wide_step(pl.multiple_of(jb * loop_width, loop_width), 0, False, SB_LOOP_BLOCKS)
            return carry

        lax.fori_loop(0, qi * (tq // loop_width), step, 0)
        for part in range(parts):
            wide_step(pl.multiple_of(base + part * SB_WIDE, SB_WIDE), part * SB_WIDE, True, 1)

    nh = d // HEAD
    shp = jax.ShapeDtypeStruct((s, d), F32)
    return _pcall(
        body, name="sb_bwd", grid=(nh, nq), out_shape=(shp, shp, shp),
        in_specs=[pl.BlockSpec((tq, HEAD), lambda h, i: (i, h)),
                  pl.BlockSpec((s, HEAD), lambda h, i: (0, nh + h)),
                  pl.BlockSpec((s, HEAD), lambda h, i: (0, 2 * nh + h)),
                  pl.BlockSpec((tq, HEAD), lambda h, i: (i, h)),
                  pl.BlockSpec((None, tq, 1), lambda h, i: (h, i, 0))],
        out_specs=(pl.BlockSpec((tq, HEAD), lambda h, i: (i, h)),
                   pl.BlockSpec((s, HEAD), lambda h, i: (0, h)),
                   pl.BlockSpec((s, HEAD), lambda h, i: (0, h))),
        scratch_shapes=[pltpu.VMEM((tq, 1), F32), pltpu.VMEM((tq, 1), F32)],
        compiler_params=_params(2),
    )(qkv, qkv, qkv, do, tot)


def _odd_out_fwd(o, x, modp, w128, slot, ts):
    s, d = x.shape

    def body(o_ref, x_ref, mp_ref, w_ref, x1_ref, y_ref):
        y = _mm(o_ref[...], w_ref[...].reshape(d, d))
        y_ref[...] = y
        x1_ref[...] = x_ref[...] + mp_ref[2:3, :] * y

    tile = pl.BlockSpec((ts, d), lambda i: (i, 0))
    shp = jax.ShapeDtypeStruct((s, d), F32)
    return _pcall(
        body, name="odd_out_fwd", grid=(s // ts,), out_shape=(shp, shp),
        in_specs=[tile, tile, _full(modp.shape), _wspec4(w128, slot)],
        out_specs=(tile, tile), compiler_params=_params(1),
    )(o, x, modp, w128)


def _odd_out_bwd(dx1, y, modp, w128, slot, ts):
    s, d = dx1.shape

    def body(dx1_ref, y_ref, mp_ref, w_ref, do_ref, dy_ref, sg_ref):
        @pl.when(pl.program_id(0) == 0)
        def _():
            sg_ref[...] = jnp.zeros_like(sg_ref)
        dx1v = dx1_ref[...]
        dyb = _bf(mp_ref[2:3, :] * dx1v)
        dy_ref[...] = dyb
        do_ref[...] = _bf(_mm_nt(dyb, w_ref[...].reshape(d, d)))
        sg_ref[2:3, :] += _sum0(dx1v * y_ref[...])

    tile = pl.BlockSpec((ts, d), lambda i: (i, 0))
    bft = jax.ShapeDtypeStruct((s, d), BF16)
    return _pcall(
        body, name="odd_out_bwd", grid=(s // ts,),
        out_shape=(bft, bft, jax.ShapeDtypeStruct((8, d), F32)),
        in_specs=[tile, tile, _full(modp.shape), _wspec4(w128, slot)],
        out_specs=(tile, tile, _full((8, d))), compiler_params=_params(1),
    )(dx1, y, modp, w128)


def _odd_qkv_bwd(x, dx1, dq, dk, dv, pre, sg_in, modp, w384, j, qkg, ts):
    s, d = x.shape
    n3 = N_DEV * w384.shape[2]

    def body(x_ref, dx1_ref, dq_ref, dk_ref, dv_ref, pre_ref, sgi_ref, mp_ref, w_ref, g_ref,
             dx_ref, dqkv_ref, h_ref, sg_ref):
        @pl.when(pl.program_id(0) == 0)
        def _():
            sg_ref[...] = sgi_ref[...]
        gmix, sc1 = mp_ref[6:7, :], mp_ref[1:2, :]
        n, rstd, h = _normmod_fwd(x_ref[...], gmix, sc1, mp_ref[0:1, :])
        hb = _bf(h)
        h_ref[...] = hb
        w = w_ref[...].reshape(n3, d)
        qkv = pre_ref[...]
        parts_q, parts_k = [], []
        gq, gk = g_ref[0:1, :], g_ref[1:2, :]
        dgq = jnp.zeros((1, HEAD), F32)
        dgk = jnp.zeros((1, HEAD), F32)
        for hh in range(SB_HEADS):
            lo = hh * HEAD
            qn, qr = _rms_fwd(qkv[:, lo:lo + HEAD])
            kn, kr = _rms_fwd(qkv[:, d + lo:d + lo + HEAD])
            dqn = dq_ref[:, lo:lo + HEAD]
            dkn = dk_ref[:, lo:lo + HEAD]
            dgq = dgq + _sum0(dqn * qn)
            dgk = dgk + _sum0(dkn * kn)
            parts_q.append(_rms_bwd(dqn * gq, qn, qr))
            parts_k.append(_rms_bwd(dkn * gk, kn, kr))
        dqkv = _bf(jnp.concatenate(parts_q + parts_k + [dv_ref[...]], axis=1))
        dqkv_ref[...] = dqkv
        dh = _mm(dqkv, w)
        dxn, dsh, dsc, dg = _normmod_bwd(dh, n, rstd, gmix, sc1)
        dx_ref[...] = dx1_ref[...] + dxn
        sg_ref[0:1, :] += dsh
        sg_ref[1:2, :] += dsc
        sg_ref[3:4, :] += dg
        sg_ref[4:5, 0:HEAD] += dgq
        sg_ref[5:6, 0:HEAD] += dgk

    tile = pl.BlockSpec((ts, d), lambda i: (i, 0))
    return _pcall(
        body, name="odd_qkv_bwd", grid=(s // ts,),
        out_shape=(jax.ShapeDtypeStruct((s, d), F32), jax.ShapeDtypeStruct((s, n3), BF16),
                   jax.ShapeDtypeStruct((s, d), BF16), jax.ShapeDtypeStruct((8, d), F32)),
        in_specs=[tile, tile, tile, tile, tile, pl.BlockSpec((ts, 2 * d), lambda i: (i, 0)), _full((8, d)),
                  _full(modp.shape), _wspec4(w384, j), _full(qkg.shape)],
        out_specs=(tile, pl.BlockSpec((ts, n3), lambda i: (i, 0)), tile, _full((8, d))),
        compiler_params=_params(1),
    )(x, dx1, dq, dk, dv, pre, sg_in, modp, w384, qkg)


def _pad_rows(a, rows):
    return jnp.concatenate([a, jnp.zeros((rows - a.shape[0],) + a.shape[1:], a.dtype)], axis=0)


def kernel(x, c, ada_w, ada_b, norm_mix_g, norm_ffn_g, ev_w_in, ev_conv_w, ev_ret_norm_g, ev_w_out, od_w_qkv, od_q_norm_g, od_k_norm_g, od_w_out, ffn_w_gate, ffn_w_up, ffn_w_down, loss_target, m_ada_w, m_ada_b, m_norm_mix_g, m_norm_ffn_g, m_ev_w_in, m_ev_conv_w, m_ev_ret_norm_g, m_ev_w_out, m_od_w_qkv, m_od_q_norm_g, m_od_k_norm_g, m_od_w_out, m_ffn_w_gate, m_ffn_w_up, m_ffn_w_down, v_ada_w, v_ada_b, v_norm_mix_g, v_norm_ffn_g, v_ev_w_in, v_ev_conv_w, v_ev_ret_norm_g, v_ev_w_out, v_od_w_qkv, v_od_q_norm_g, v_od_k_norm_g, v_od_w_out, v_ffn_w_gate, v_ffn_w_up, v_ffn_w_down):
    me = 4 * lax.axis_index("x") + 2 * lax.axis_index("y") + lax.axis_index("c")
    xs = x[0]
    tgt = loss_target[0]
    s, d = xs.shape
    depth = ada_w.shape[0]
    n_even, n_odd = ev_w_in.shape[0], od_w_qkv.shape[0]
    ts = 256
    tq = SB_QUERIES
    cd = 4 * HEAD
    cc = ev_conv_w.shape[2]

    pack0 = jnp.zeros((8, d), F32).at[0].set(c[0]).at[1, :n_even * 3 * cc].set(ev_conv_w.reshape(-1))
    got0, _ = _all_gather(pack0, "gather_cond")
    got0 = got0.reshape(N_DEV, 8, d)
    c_all = got0[:, 0, :]
    conv_all = got0[:, 1, :n_even * 3 * cc].reshape(N_DEV, n_even, 3, cc).transpose(1, 2, 0, 3)
    conv_all = conv_all.reshape(n_even, 3, N_DEV * cc)
    cols = ada_w.shape[2]
    ada_b_cols = lax.dynamic_slice(ada_b, (0, me * cols), (depth, cols))
    mod_cols = _ada_fwd(c_all, ada_w, ada_b_cols)
    got1, cond_done = _all_gather(mod_cols.reshape(depth * N_DEV, cols), "gather_mod")
    got1 = got1.reshape(N_DEV, depth, N_DEV, cols)
    mod = lax.dynamic_index_in_dim(got1, me, axis=2, keepdims=False)
    mod = mod.transpose(1, 0, 2).reshape(depth, 6, d)
    modps = [jnp.concatenate([mod[l], norm_mix_g[l][None], norm_ffn_g[l][None]], axis=0) for l in range(depth)]

    in_flight = []
    started = cond_done
    for l in range(depth):
        j = l // 2
        plain = lambda w: _bf(w + started)
        tr = lambda w: plain(w).T
        blocks = [tr(ev_w_in[j]), plain(ev_w_out[j])] if l % 2 == 0 else [tr(od_w_qkv[j]), plain(od_w_out[j])]
        mixer, started = _exchange_start(blocks, [_landing(b, me, False) for b in blocks], False,
                                         f"gather_start_mixer_{l}")
        blocks = [jnp.concatenate([tr(ffn_w_gate[l]), tr(ffn_w_up[l]), plain(ffn_w_down[l])], axis=0)]
        ffn, started = _exchange_start(blocks, [_landing(b, me, False) for b in blocks], False,
                                       f"gather_start_ffn_{l}")
        in_flight.append((mixer, ffn))
        modps[0] = modps[0] + started
    n_ffn = ffn_w_down.shape[1]

    def mixer_weights(l, after):
        got = _exchange_wait(*in_flight[l][0], False, [after], f"gather_wait_mixer_{l}")
        return got[0].reshape(N_DEV, 1, -1, d), got[1].reshape(N_DEV, 1, -1, d)

    def ffn_weights(l, after):
        got = _exchange_wait(*in_flight[l][1], False, [after], f"gather_wait_ffn_{l}")
        return got[0].reshape(N_DEV, 3, n_ffn, d)

    cos, sin = _rope_tables(s)
    consts = _retention_consts(ts)
    cws = [_pad_rows(jnp.concatenate([conv_all[j], ev_ret_norm_g[j][None]], axis=0), 8) for j in range(n_even)]
    qkgs = [_pad_rows(jnp.stack([od_q_norm_g[j], od_k_norm_g[j]]), 8) for j in range(n_odd)]

    saved = []
    weights = []
    cur = xs
    for l in range(depth):
        j = l // 2
        w_in, w_out = mixer_weights(l, cur)
        if l % 2 == 0:
            x1, y, states, zhalo, proj = _even_fwd(cur, modps[l], w_in, w_out, 0, cws[j], cos, sin, consts, ts)
            mix = (states, zhalo, proj)
        else:
            qkv, pre = _odd_qkv_fwd(cur, modps[l], w_in, 0, qkgs[j], ts)
            o, tot = _sb_fwd(qkv, tq)
            x1, y = _odd_out_fwd(o, cur, modps[l], w_out, 0, ts)
            mix = (qkv, o, tot, pre)
        w_ffn = ffn_weights(l, x1)
        weights.append((w_in, w_out, w_ffn))
        x2, f, ab = _ffn_fwd(x1, modps[l], w_ffn, 0, ts)
        saved.append((cur, x1, y, (f, ab), mix))
        cur = x2

    dx, loss_part = _loss_grad(cur, tgt, ts)
    loss = lax.psum(loss_part[0, 0], ("x", "y", "c"))

    dmod = [None] * depth
    d_gmix = [None] * depth
    d_gffn = [None] * depth
    d_conv = [None] * n_even
    d_retg = [None] * n_even
    d_qg = [None] * n_odd
    d_kg = [None] * n_odd
    grads_in_flight = [None] * depth
    for l in reversed(range(depth)):
        j = l // 2
        x0, x1, y, (f, ab), mix = saved[l]
        w_in, w_out, w_ffn = weights[l]
        g_ffn = lax.empty(w_ffn.shape, BF16)
        g_in = lax.empty(w_in.shape, BF16)
        g_out = lax.empty(w_out.shape, BF16)
        dx1, dab, h2, sv, df, sg2 = _ffn_bwd(x1, f, ab, dx, modps[l], w_ffn, 0, ts)
        g_ffn = _tn_matmul(dab, 0, h2, g_ffn, 0, "tn_gate")
        g_ffn = _tn_matmul(dab, 1, h2, g_ffn, 1, "tn_up")
        g_ffn = _tn_matmul(sv, 0, df, g_ffn, 2, "tn_down")
        if l == 0:
            pieces = [g_ffn.reshape(N_DEV, -1, d)]
            last_ffn_flight, started = _exchange_start(pieces, [_landing(p, me, True) for p in pieces], True,
                                                       "grads_start_ffn_0")
            modps[0] = modps[0] + started
        if l % 2 == 0:
            states, zhalo, proj = mix
            dx, dproj, hb, mb, dyb, sg1 = _even_bwd(x0, dx1, y, states, zhalo, proj, modps[l], w_in, w_out, 0,
                                                    cws[j], cos, sin, consts, ts)
            g_in = _tn_matmul(dproj, 0, hb, g_in, 0, "tn_ev_in")
            g_out = _tn_matmul(mb, 0, dyb, g_out, 0, "tn_ev_out")
            d_conv[j] = sg1[4:7, :cd]
            d_retg[j] = sg1[7, :cd]
        else:
            qkv, o, tot, pre = mix
            do, dyb, sg0 = _odd_out_bwd(dx1, y, modps[l], w_out, 0, ts)
            dq, dk, dv = _sb_bwd(qkv, do, tot, tq)
            dx, dqkv, hb, sg1 = _odd_qkv_bwd(x0, dx1, dq, dk, dv, pre, sg0, modps[l], w_in, 0, qkgs[j], 2 * ts)
            g_in = _tn_matmul(dqkv, 0, hb, g_in, 0, "tn_od_qkv")
            g_out = _tn_matmul(o, 0, dyb, g_out, 0, "tn_od_out")
            d_qg[j] = sg1[4, :HEAD]
            d_kg[j] = sg1[5, :HEAD]
        pieces = [g.reshape(N_DEV, -1, d) for g in ((g_ffn, g_in, g_out) if l > 0 else (g_in, g_out))]
        if l > 0:
            grads_in_flight[l], started = _exchange_start(pieces, [_landing(p, me, True) for p in pieces], True,
                                                          f"grads_start_{l}")
            modps[l - 1] = modps[l - 1] + started
        dmod[l] = jnp.concatenate([sg1[0:3], sg2[0:3]], axis=0).reshape(-1)
        d_gmix[l] = sg1[3]
        d_gffn[l] = sg2[3]

    small = jnp.concatenate(
        [jnp.stack(dmod).reshape(-1), jnp.stack(d_gmix).reshape(-1), jnp.stack(d_gffn).reshape(-1),
         jnp.stack(d_retg).reshape(-1), jnp.stack(d_qg).reshape(-1), jnp.stack(d_kg).reshape(-1),
         jnp.stack(d_conv).reshape(-1)])
    n_small = small.shape[0]
    rows_small = -(-n_small // (8 * 128)) * 8
    small = jnp.concatenate([small, jnp.zeros((rows_small * 128 - n_small,), F32)]).reshape(rows_small, 128)
    small_flight, started = _exchange_start([small], [_landing(small, me, False)], False, "small_start")
    grads_in_flight[0], started = _exchange_start(pieces, [_landing(p, me, True, started) for p in pieces],
                                                  True, "grads_start_0")

    res = {}
    big = {"ev_w_in": (ev_w_in, m_ev_w_in, v_ev_w_in), "ev_w_out": (ev_w_out, m_ev_w_out, v_ev_w_out),
           "od_w_qkv": (od_w_qkv, m_od_w_qkv, v_od_w_qkv), "od_w_out": (od_w_out, m_od_w_out, v_od_w_out),
           "ffn_w_gate": (ffn_w_gate, m_ffn_w_gate, v_ffn_w_gate), "ffn_w_up": (ffn_w_up, m_ffn_w_up, v_ffn_w_up),
           "ffn_w_down": (ffn_w_down, m_ffn_w_down, v_ffn_w_down)}
    flipped = ("ev_w_in", "ffn_w_gate", "ffn_w_up")
    flip = lambda a: a.transpose(0, 2, 1)
    for name in flipped:
        big[name] = tuple(flip(a) for a in big[name])
    for name, (w, _, _) in big.items():
        res[name] = tuple(lax.empty(w.shape, F32) for _ in range(4))

    def update(name, idx, g_layer, part=0):
        w, m, v = big[name]
        res[name] = _adamw_layer(w, g_layer, m, v, res[name], idx, "adamw_" + name, part)

    def ffn_updates(l, r_ffn):
        update("ffn_w_gate", l, r_ffn, 0)
        update("ffn_w_up", l, r_ffn, 1)
        update("ffn_w_down", l, r_ffn, 2)

    def mixer_updates(l, r_in, r_out):
        if l % 2 == 0:
            update("ev_w_in", l // 2, r_in)
        else:
            update("od_w_qkv", l // 2, _sum_slots(r_in, "sum_qkv").T)
        update("ev_w_out" if l % 2 == 0 else "od_w_out", l // 2, r_out)

    after = [dx + started]
    for l in reversed(range(1, depth)):
        recv = _exchange_wait(*grads_in_flight[l], True, after, f"grads_wait_{l}")
        ffn_updates(l, recv[0])
        mixer_updates(l, recv[1], recv[2])
        after = [res[name][1] for name in big]

    got2 = _exchange_wait(*small_flight, False, after, "small_wait")[0]
    tot_small = _sum_small(got2).reshape(-1)
    n_mod = depth * 6 * d
    dmod_all = got2.reshape(N_DEV, -1)[:, :n_mod].reshape(N_DEV, depth, 6 * d)
    dmod_cols = lax.dynamic_slice(dmod_all, (0, 0, me * cols), (N_DEV, depth, cols)).transpose(1, 0, 2)
    g_ada_w = _ada_bwd(c_all.T, dmod_cols)
    res["ada_w"] = (g_ada_w,) + _adamw_nd(ada_w, g_ada_w, m_ada_w, v_ada_w, "adamw_ada_w")

    off = [0]

    def take(shape):
        n = int(np.prod(shape))
        out = tot_small[off[0]:off[0] + n].reshape(shape)
        off[0] += n
        return out

    g_ada_b = take((depth, 6 * d))
    g_norm_mix = take((depth, d))
    g_norm_ffn = take((depth, d))
    g_ret_norm = take((n_even, cd))
    g_q_norm = take((n_odd, HEAD))
    g_k_norm = take((n_odd, HEAD))
    g_conv_full = take((n_even, 3, cd))
    g_conv = lax.dynamic_slice(g_conv_full, (0, 0, me * cc), (n_even, 3, cc))

    recv = _exchange_wait(*last_ffn_flight, True, [res["ada_w"][1]], "grads_wait_ffn_0")
    ffn_updates(0, recv[0])
    after = [res[name][1] for name in ("ffn_w_gate", "ffn_w_up", "ffn_w_down")]
    recv = _exchange_wait(*grads_in_flight[0], True, after, "grads_wait_0")
    mixer_updates(0, recv[0], recv[1])
    for name in flipped:
        res[name] = tuple(flip(a) for a in res[name])

    smalls = [("ada_b", ada_b, g_ada_b, m_ada_b, v_ada_b), ("norm_mix_g", norm_mix_g, g_norm_mix, m_norm_mix_g, v_norm_mix_g),
              ("norm_ffn_g", norm_ffn_g, g_norm_ffn, m_norm_ffn_g, v_norm_ffn_g),
              ("ev_conv_w", ev_conv_w, g_conv, m_ev_conv_w, v_ev_conv_w),
              ("ev_ret_norm_g", ev_ret_norm_g, g_ret_norm, m_ev_ret_norm_g, v_ev_ret_norm_g),
              ("od_q_norm_g", od_q_norm_g, g_q_norm, m_od_q_norm_g, v_od_q_norm_g),
              ("od_k_norm_g", od_k_norm_g, g_k_norm, m_od_k_norm_g, v_od_k_norm_g)]

    def pack(arrs):
        flat = jnp.concatenate([a.reshape(-1) for a in arrs])
        rows = -(-flat.shape[0] // (8 * 128)) * 8
        return jnp.concatenate([flat, jnp.zeros((rows * 128 - flat.shape[0],), F32)]).reshape(rows, 128)

    sd, sm, sv_ = _adamw(pack([t[1] for t in smalls]), pack([t[2] for t in smalls]),
                         pack([t[3] for t in smalls]), pack([t[4] for t in smalls]), "adamw_small")
    sd, sm, sv_ = sd.reshape(-1), sm.reshape(-1), sv_.reshape(-1)
    pos = 0
    for name, w, g, m, v in smalls:
        n = int(np.prod(w.shape))
        res[name] = (g, sd[pos:pos + n].reshape(w.shape), sm[pos:pos + n].reshape(w.shape),
                     sv_[pos:pos + n].reshape(w.shape))
        pos += n

    order = ["ada_w", "ada_b", "norm_mix_g", "norm_ffn_g", "ev_w_in", "ev_conv_w", "ev_ret_norm_g", "ev_w_out",
             "od_w_qkv", "od_q_norm_g", "od_k_norm_g", "od_w_out", "ffn_w_gate", "ffn_w_up", "ffn_w_down"]
    outs = [loss, dx[None]]
    for k in range(4):
        outs += [res[name][k] for name in order]
    return tuple(outs)
```
